```python
import math
import jax, jax.numpy as jnp
from jax import lax
import numpy as np

D_MODEL = 2048
BATCH = 1
SEQ = 8192
DEPTH = 1
DEC_BATCH = 8
DEC_SEQ = 2048
PAST_LEN = 128

D_MIX = D_MODEL
ATT_WIDTH = D_MIX // 2
HYENA_WIDTH = D_MIX - ATT_WIDTH
ATT_HEADS = 8
ATT_VDIM = ATT_WIDTH // ATT_HEADS
ATT_QKDIM = ATT_VDIM // 2
Q_BLOCK = 128
HYENA_ORDER = 2
FILTER_EMB = 33
FILTER_BANDS = (FILTER_EMB - 1) // 2
FILTER_HIDDEN = 64
FILTER_OUT = HYENA_ORDER * 2 * HYENA_WIDTH
DECAY_FAST = 0.3
DECAY_SLOW = 1.5
DECAY_TARGET = 1e-2
PEER_HEADS = 8
PEER_NKEYS = 128
PEER_EXPERTS = PEER_NKEYS * PEER_NKEYS
PEER_DQ = 256
PEER_TOPK = 16
TOKEN_BLOCK = 128
IN_COLS = 3 * ATT_WIDTH + (HYENA_ORDER + 1) * HYENA_WIDTH
RMS_EPS = 1e-6

kernel_name = 'hybrid_diffattn_hyena_peer_encoder'


def rms_norm(x, g):
    xf = x.astype(jnp.float32)
    y = xf * lax.rsqrt(jnp.mean(xf * xf, axis=-1, keepdims=True) + RMS_EPS)
    return (y * g.astype(jnp.float32)).astype(x.dtype)


def alibi_slopes(n_heads):
    return 2.0 ** (-8.0 * jnp.arange(1, n_heads + 1, dtype=jnp.float32) / n_heads)


def diff_attention(q, k, v, lam, lambda_init, sub_g):
    B, L = q.shape[0], q.shape[1]
    nb = L // Q_BLOCK
    slopes = alibi_slopes(ATT_HEADS)
    kpos = jnp.arange(L, dtype=jnp.float32)
    scale = ATT_QKDIM ** -0.5
    qb = q.reshape(B, nb, Q_BLOCK, ATT_HEADS, 2, ATT_QKDIM).transpose(1, 0, 2, 3, 4, 5)

    def block(args):
        qi, bi = args
        s = jnp.einsum('bqhcd,bkhcd->bhcqk', qi, k).astype(jnp.float32) * scale
        qpos = (bi * Q_BLOCK + jnp.arange(Q_BLOCK)).astype(jnp.float32)
        dist = jnp.abs(qpos[:, None] - kpos[None, :])
        s = s - slopes[None, :, None, None, None] * dist[None, None, None]
        p = jax.nn.softmax(s, axis=-1)
        a = p[:, :, 0] - lam * p[:, :, 1]
        return jnp.einsum('bhqk,bkhe->bqhe', a.astype(v.dtype), v)

    o = lax.map(block, (qb, jnp.arange(nb)))
    o = o.transpose(1, 0, 2, 3, 4).reshape(B, L, ATT_HEADS, ATT_VDIM)
    o = rms_norm(o, sub_g) * (1.0 - lambda_init)
    return o.reshape(B, L, ATT_WIDTH)


def hyena_filters(L, w1, b1, w2, b2, w3, b3, w4, freq):
    t = jnp.linspace(0.0, 1.0, L, dtype=jnp.float32)[:, None]
    w = 2.0 * math.pi * jnp.arange(L, dtype=jnp.float32)[:, None] / L
    f = jnp.linspace(1e-4, FILTER_BANDS - 1, FILTER_BANDS, dtype=jnp.float32)[None, :]
    z = jnp.concatenate([t, jnp.cos(f * w), -jnp.sin(f * w)], axis=-1)
    fr = freq.astype(jnp.float32)
    h = jnp.sin(fr * (z @ w1 + b1))
    h = jnp.sin(fr * (h @ w2 + b2))
    h = jnp.sin(fr * (h @ w3 + b3))
    h = (h @ w4).astype(jnp.float32)
    deltas = jnp.abs(jnp.linspace(math.log(DECAY_FAST) / DECAY_TARGET,
                                  math.log(DECAY_SLOW) / DECAY_TARGET,
                                  HYENA_WIDTH, dtype=jnp.float32))
    decay = jnp.exp(-t * deltas[None, :])
    h = h.reshape(L, HYENA_ORDER, 2, HYENA_WIDTH) * decay[:, None, None, :]
    fwd, bwd = h[:, :, 0], h[:, :, 1]
    k = jnp.concatenate([fwd, jnp.zeros_like(fwd[:1]), bwd[1:][::-1]], axis=0)
    return jnp.fft.rfft(k, axis=0)


def hyena_mixer(z, conv_w, conv_b, filt_f, fft_bias, out_g):
    L = z.shape[1]
    zp = jnp.pad(z, ((0, 0), (1, 1), (0, 0)))
    z = zp[:, :-2] * conv_w[0] + zp[:, 1:-1] * conv_w[1] + zp[:, 2:] * conv_w[2] + conv_b
    x1, x2, s = jnp.split(z, HYENA_ORDER + 1, axis=-1)
    dtype = z.dtype
    s = s.astype(jnp.float32)
    for o, gate in enumerate((x1, x2)):
        S = jnp.fft.rfft(s, n=2 * L, axis=1)
        y = jnp.fft.irfft(S * filt_f[None, :, o], n=2 * L, axis=1)[:, :L]
        s = gate.astype(jnp.float32) * (y + s * fft_bias[o].astype(jnp.float32))
    return rms_norm(s.astype(dtype), out_g)


def peer(x, w_q, sub_keys, u_tab, v_tab):
    B, L, D = x.shape
    xt = x.reshape((B * L) // TOKEN_BLOCK, TOKEN_BLOCK, D)

    def block(xb):
        q = (xb @ w_q).reshape(TOKEN_BLOCK, PEER_HEADS, 2, PEER_DQ // 2)
        s = jnp.einsum('thcd,hcnd->thcn', q, sub_keys).astype(jnp.float32)
        s1, i1 = lax.top_k(s[:, :, 0], PEER_TOPK)
        s2, i2 = lax.top_k(s[:, :, 1], PEER_TOPK)
        cand = (s1[..., :, None] + s2[..., None, :]).reshape(TOKEN_BLOCK, PEER_HEADS, PEER_TOPK * PEER_TOPK)
        cid = (i1[..., :, None] * PEER_NKEYS + i2[..., None, :]).reshape(TOKEN_BLOCK, PEER_HEADS, PEER_TOPK * PEER_TOPK)
        top_s, top_c = lax.top_k(cand, PEER_TOPK)
        eid = jnp.take_along_axis(cid, top_c, axis=-1)
        g = jax.nn.softmax(top_s, axis=-1)
        a = jnp.einsum('td,thkd->thk', xb, u_tab[eid]).astype(jnp.float32)
        coef = (g * jax.nn.gelu(a, approximate=False)).astype(xb.dtype)
        return jnp.einsum('thk,thkd->td', coef, v_tab[eid])

    return lax.map(block, xt).reshape(B, L, D)


def trunk(x, p):
    B, L, _ = x.shape
    for l in range(DEPTH):
        lambda_init = 0.8 - 0.6 * math.exp(-0.3 * l)
        h = rms_norm(x, p['norm1_g'][l])
        proj = h @ p['w_in'][l]
        q, k, v, zh = jnp.split(proj, [ATT_WIDTH, 2 * ATT_WIDTH, 3 * ATT_WIDTH], axis=-1)
        q = rms_norm(q.reshape(B, L, ATT_HEADS, 2, ATT_QKDIM), p['q_norm_g'][l])
        k = rms_norm(k.reshape(B, L, ATT_HEADS, 2, ATT_QKDIM), p['k_norm_g'][l])
        v = v.reshape(B, L, ATT_HEADS, ATT_VDIM)
        lam = (jnp.exp(jnp.sum(p['lambda_q1'][l].astype(jnp.float32) * p['lambda_k1'][l].astype(jnp.float32)))
               - jnp.exp(jnp.sum(p['lambda_q2'][l].astype(jnp.float32) * p['lambda_k2'][l].astype(jnp.float32)))
               + lambda_init)
        att = diff_attention(q, k, v, lam, lambda_init, p['attn_sub_g'][l])
        filt_f = hyena_filters(L, p['filt_w1'][l], p['filt_b1'][l], p['filt_w2'][l], p['filt_b2'][l],
                               p['filt_w3'][l], p['filt_b3'][l], p['filt_w4'][l], p['filt_freq'][l])
        hy = hyena_mixer(zh, p['conv_w'][l], p['conv_b'][l], filt_f, p['fft_bias'][l], p['hyena_out_g'][l])
        x = x + jnp.concatenate([att, hy], axis=-1) @ p['w_out'][l]
        x = x + peer(rms_norm(x, p['norm2_g'][l]), p['peer_wq'][l], p['peer_keys'][l],
                     p['peer_u'][l], p['peer_v'][l])
    return x


def setup_inputs(seed: int = 0) -> dict:
    key = jax.random.key(seed)
    ks = jax.random.split(key, 32)
    n = lambda i, shape, s: jax.random.normal(ks[i], shape, jnp.float32) * s
    gain = lambda i, shape: 1.0 + 0.02 * jax.random.normal(ks[i], shape, jnp.float32)
    return {
        'x_prompt': n(0, (BATCH, SEQ, D_MODEL), 1.0),
        'x_sample': n(1, (DEC_BATCH, DEC_SEQ, D_MODEL), 1.0),
        'norm1_g': gain(2, (DEPTH, D_MODEL)),
        'w_in': n(3, (DEPTH, D_MODEL, IN_COLS), D_MODEL ** -0.5),
        'q_norm_g': gain(4, (DEPTH, ATT_QKDIM)),
        'k_norm_g': gain(5, (DEPTH, ATT_QKDIM)),
        'lambda_q1': n(6, (DEPTH, ATT_QKDIM), 0.1),
        'lambda_k1': n(7, (DEPTH, ATT_QKDIM), 0.1),
        'lambda_q2': n(8, (DEPTH, ATT_QKDIM), 0.1),
        'lambda_k2': n(9, (DEPTH, ATT_QKDIM), 0.1),
        'attn_sub_g': gain(10, (DEPTH, ATT_VDIM)),
        'conv_w': n(11, (DEPTH, 3, (HYENA_ORDER + 1) * HYENA_WIDTH), 0.5),
        'conv_b': n(12, (DEPTH, (HYENA_ORDER + 1) * HYENA_WIDTH), 0.02),
        'filt_w1': n(13, (DEPTH, FILTER_EMB, FILTER_HIDDEN), FILTER_EMB ** -0.5),
        'filt_b1': n(14, (DEPTH, FILTER_HIDDEN), 0.1),
        'filt_w2': n(15, (DEPTH, FILTER_HIDDEN, FILTER_HIDDEN), FILTER_HIDDEN ** -0.5),
        'filt_b2': n(16, (DEPTH, FILTER_HIDDEN), 0.1),
        'filt_w3': n(17, (DEPTH, FILTER_HIDDEN, FILTER_HIDDEN), FILTER_HIDDEN ** -0.5),
        'filt_b3': n(18, (DEPTH, FILTER_HIDDEN), 0.1),
        'filt_w4': n(19, (DEPTH, FILTER_HIDDEN, FILTER_OUT), FILTER_HIDDEN ** -0.5),
        'filt_freq': gain(20, (DEPTH, FILTER_HIDDEN)),
        'fft_bias': n(21, (DEPTH, HYENA_ORDER, HYENA_WIDTH), 1.0),
        'hyena_out_g': gain(22, (DEPTH, HYENA_WIDTH)),
        'w_out': n(23, (DEPTH, D_MIX, D_MODEL), D_MIX ** -0.5),
        'norm2_g': gain(24, (DEPTH, D_MODEL)),
        'peer_wq': n(25, (DEPTH, D_MODEL, PEER_HEADS * PEER_DQ), D_MODEL ** -0.5),
        'peer_keys': n(26, (DEPTH, PEER_HEADS, 2, PEER_NKEYS, PEER_DQ // 2), (PEER_DQ // 2) ** -0.5),
        'peer_u': n(27, (DEPTH, PEER_EXPERTS, D_MODEL), D_MODEL ** -0.5),
        'peer_v': n(28, (DEPTH, PEER_EXPERTS, D_MODEL), PEER_HEADS ** -0.5),
    }


def reference(x_prompt, x_sample, norm1_g, w_in, q_norm_g, k_norm_g, lambda_q1, lambda_k1,
              lambda_q2, lambda_k2, attn_sub_g, conv_w, conv_b, filt_w1, filt_b1, filt_w2, filt_b2,
              filt_w3, filt_b3, filt_w4, filt_freq, fft_bias, hyena_out_g, w_out, norm2_g,
              peer_wq, peer_keys, peer_u, peer_v):
    p = {
        'norm1_g': norm1_g, 'w_in': w_in, 'q_norm_g': q_norm_g, 'k_norm_g': k_norm_g,
        'lambda_q1': lambda_q1, 'lambda_k1': lambda_k1, 'lambda_q2': lambda_q2, 'lambda_k2': lambda_k2,
        'attn_sub_g': attn_sub_g, 'conv_w': conv_w, 'conv_b': conv_b,
        'filt_w1': filt_w1, 'filt_b1': filt_b1, 'filt_w2': filt_w2, 'filt_b2': filt_b2,
        'filt_w3': filt_w3, 'filt_b3': filt_b3, 'filt_w4': filt_w4, 'filt_freq': filt_freq,
        'fft_bias': fft_bias, 'hyena_out_g': hyena_out_g, 'w_out': w_out, 'norm2_g': norm2_g,
        'peer_wq': peer_wq, 'peer_keys': peer_keys, 'peer_u': peer_u, 'peer_v': peer_v,
    }
    y_prompt = trunk(x_prompt, p)
    y_sample = trunk(x_sample, p)
    return (y_prompt, y_sample)
```

```python
import functools
import math

import jax
import jax.numpy as jnp
from jax import lax
from jax.experimental import pallas as pl
from jax.experimental.pallas import tpu as pltpu

F32 = jnp.float32
BF16 = jnp.bfloat16
I32 = jnp.int32
U32 = jnp.uint32

RMS_EPS = 1e-6
LOG2E = 1.4426950408889634
LANES = 128
SUBLANES = 8
VMEM_LIMIT_BYTES = 56 * 1024 * 1024

ATT_HEADS = 8
ATT_VDIM = 128
ATT_QKDIM = 64
HYENA_ORDER = 2
FILTER_BANDS = 16
DECAY_FAST = 0.3
DECAY_SLOW = 1.5
DECAY_TARGET = 1e-2
PEER_HEADS = 8
PEER_NKEYS = 128
PEER_TOPK = 16
PEER_HALF = PEER_NKEYS * PEER_NKEYS // 2


def _cparams(sem, vmem=VMEM_LIMIT_BYTES):
    return pltpu.CompilerParams(dimension_semantics=sem, vmem_limit_bytes=vmem)


def _rmsnorm_kernel(x_ref, g_ref, *o_refs):
    x = x_ref[...]
    ms = jnp.mean(x * x, axis=-1, keepdims=True)
    y = x * lax.rsqrt(ms + RMS_EPS) * g_ref[...]
    for o_ref in o_refs:
        o_ref[...] = y.astype(o_ref.dtype)


def _rmsnorm(x, g, out_dtypes, tm=512):
    t, d = x.shape
    tm = min(tm, t)
    spec = pl.BlockSpec((tm, d), lambda i: (i, 0))
    return pl.pallas_call(
        _rmsnorm_kernel,
        grid=(t // tm,),
        in_specs=[spec, pl.BlockSpec((1, d), lambda i: (0, 0))],
        out_specs=[spec for _ in out_dtypes],
        out_shape=[jax.ShapeDtypeStruct((t, d), dt) for dt in out_dtypes],
        compiler_params=_cparams(("parallel",)),
        name="rmsnorm",
    )(x, g.reshape(1, d).astype(F32))


def _group_rms_scale(x, gain):
    lane = lax.broadcasted_iota(I32, x.shape, 1)
    lo = lane < ATT_QKDIM
    x2 = x * x
    s_lo = jnp.sum(jnp.where(lo, x2, 0.0), axis=-1, keepdims=True)
    s_hi = jnp.sum(jnp.where(lo, 0.0, x2), axis=-1, keepdims=True)
    ms = jnp.where(lo, s_lo, s_hi) * (1.0 / ATT_QKDIM)
    return x * lax.rsqrt(ms + RMS_EPS) * gain


def _mm_kernel(a_ref, b_ref, *rest, mode):
    acc = jnp.dot(a_ref[...], b_ref[...], preferred_element_type=F32)
    if mode == "plain":
        (o_ref,) = rest
        o_ref[...] = acc.astype(o_ref.dtype)
    elif mode == "residual":
        r_ref, o_ref = rest
        o_ref[...] = (acc + r_ref[...]).astype(o_ref.dtype)
    elif mode == "qknorm":
        g_ref, o_ref = rest
        for c in range(acc.shape[1] // LANES):
            sl = slice(c * LANES, (c + 1) * LANES)
            o_ref[:, sl] = _group_rms_scale(acc[:, sl], g_ref[:, sl]).astype(o_ref.dtype)
    else:
        raise ValueError(mode)


def _matmul(a, b, n_cols, col_off, out_dtype, mode="plain", extra=None, tm=1024, tn=1024):
    m, k = a.shape
    tm = min(tm, m)
    tn = min(tn, n_cols)
    assert col_off % tn == 0 and n_cols % tn == 0 and m % tm == 0
    off = col_off // tn
    in_specs = [pl.BlockSpec((tm, k), lambda i, j: (i, 0)),
                pl.BlockSpec((k, tn), lambda i, j: (0, j + off))]
    args = [a, b]
    if mode == "residual":
        in_specs.append(pl.BlockSpec((tm, tn), lambda i, j: (i, j)))
        args.append(extra)
    elif mode == "qknorm":
        in_specs.append(pl.BlockSpec((1, tn), lambda i, j: (0, j)))
        args.append(extra)
    return pl.pallas_call(
        functools.partial(_mm_kernel, mode=mode),
        grid=(m // tm, n_cols // tn),
        in_specs=in_specs,
        out_specs=pl.BlockSpec((tm, tn), lambda i, j: (i, j)),
        out_shape=jax.ShapeDtypeStruct((m, n_cols), out_dtype),
        compiler_params=_cparams(("parallel", "arbitrary")),
        name="matmul_" + mode,
    )(*args)


def _attn_kernel(slope_ref, lam_ref, q_ref, k_ref, v_ref, g_ref, o_ref,
                 m0_ref, l0_ref, a0_ref, m1_ref, l1_ref, a1_ref, *, tq, tk, nk, out_scale):
    h = pl.program_id(1)
    i = pl.program_id(2)
    j = pl.program_id(3)

    @pl.when(j == 0)
    def _():
        for m_ref, l_ref, a_ref in ((m0_ref, l0_ref, a0_ref), (m1_ref, l1_ref, a1_ref)):
            m_ref[...] = jnp.full(m_ref.shape, -jnp.inf, F32)
            l_ref[...] = jnp.zeros(l_ref.shape, F32)
            a_ref[...] = jnp.zeros(a_ref.shape, F32)

    q = q_ref[...]
    k = k_ref[...]
    v = v_ref[...]
    lane = lax.broadcasted_iota(I32, q.shape, 1)
    zero = jnp.zeros_like(q)
    q_maps = (jnp.where(lane < ATT_QKDIM, q, zero), jnp.where(lane < ATT_QKDIM, zero, q))

    rows = lax.broadcasted_iota(I32, (tq, tk), 0) + i * tq
    cols = lax.broadcasted_iota(I32, (tq, tk), 1) + j * tk
    dist = jnp.abs(rows - cols).astype(F32)
    bias = dist * (-(slope_ref[h] * LOG2E))

    for qm, m_ref, l_ref, a_ref in ((q_maps[0], m0_ref, l0_ref, a0_ref),
                                    (q_maps[1], m1_ref, l1_ref, a1_ref)):
        s = lax.dot_general(qm, k, (((1,), (1,)), ((), ())), preferred_element_type=F32) + bias
        m_old = m_ref[...]
        m_new = jnp.maximum(m_old, jnp.max(s, axis=-1, keepdims=True))
        alpha = jnp.exp2(m_old - m_new)
        p = jnp.exp2(s - m_new)
        l_ref[...] = alpha * l_ref[...] + jnp.sum(p, axis=-1, keepdims=True)
        a_ref[...] = alpha * a_ref[...] + jnp.dot(p.astype(BF16), v, preferred_element_type=F32)
        m_ref[...] = m_new

    @pl.when(j == nk - 1)
    def _():
        o = a0_ref[...] / l0_ref[...] - lam_ref[0] * (a1_ref[...] / l1_ref[...])
        ms = jnp.mean(o * o, axis=-1, keepdims=True)
        o_ref[...] = (o * lax.rsqrt(ms + RMS_EPS) * g_ref[...] * out_scale).astype(o_ref.dtype)


def _attention(qk, v, row_off, slopes, lam, sub_g, batch, seq, out_scale, tq=512, tk=1024):
    tq = min(tq, seq)
    tk = min(tk, seq)
    nq, nk = seq // tq, seq // tk
    assert row_off % tq == 0 and row_off % tk == 0
    oq, ok = row_off // tq, row_off // tk
    kern = functools.partial(_attn_kernel, tq=tq, tk=tk, nk=nk, out_scale=out_scale)
    smem = pl.BlockSpec(memory_space=pltpu.SMEM)
    return pl.pallas_call(
        kern,
        grid=(batch, ATT_HEADS, nq, nk),
        in_specs=[smem, smem,
                  pl.BlockSpec((tq, LANES), lambda b, h, i, j: (oq + b * nq + i, h)),
                  pl.BlockSpec((tk, LANES), lambda b, h, i, j: (ok + b * nk + j, ATT_HEADS + h)),
                  pl.BlockSpec((tk, LANES), lambda b, h, i, j: (ok + b * nk + j, h)),
                  pl.BlockSpec((1, LANES), lambda b, h, i, j: (0, 0))],
        out_specs=pl.BlockSpec((tq, LANES), lambda b, h, i, j: (b * nq + i, h)),
        out_shape=jax.ShapeDtypeStruct((batch * seq, ATT_HEADS * ATT_VDIM), BF16),
        scratch_shapes=[pltpu.VMEM((tq, 1), F32), pltpu.VMEM((tq, 1), F32), pltpu.VMEM((tq, LANES), F32),
                        pltpu.VMEM((tq, 1), F32), pltpu.VMEM((tq, 1), F32), pltpu.VMEM((tq, LANES), F32)],
        compiler_params=_cparams(("parallel", "parallel", "parallel", "arbitrary")),
        name="diff_attention",
    )(slopes, lam, qk, qk, v, sub_g.reshape(1, LANES).astype(F32))


def _shortconv_kernel(z_ref, w_ref, b_ref, o_ref):
    z = z_ref[...]
    n = z.shape[0]
    row = lax.broadcasted_iota(I32, z.shape, 0)
    prev = jnp.where(row == 0, 0.0, pltpu.roll(z, 1, axis=0))
    nxt = jnp.where(row == n - 1, 0.0, pltpu.roll(z, n - 1, axis=0))
    w = w_ref[...]
    o_ref[...] = prev * w[0:1] + z * w[1:2] + nxt * w[2:3] + b_ref[...]


def _shortconv(zh, row_off, conv_w, conv_b, batch, seq, cb=256):
    c3 = zh.shape[1]
    c = c3 // 3
    ncb = c // cb
    assert row_off % seq == 0
    ob = row_off // seq
    out = pl.pallas_call(
        _shortconv_kernel,
        grid=(batch, 3, ncb),
        in_specs=[pl.BlockSpec((seq, cb), lambda b, p, j: (ob + b, p * ncb + j)),
                  pl.BlockSpec((3, cb), lambda b, p, j: (0, p * ncb + j)),
                  pl.BlockSpec((1, cb), lambda b, p, j: (0, p * ncb + j))],
        out_specs=pl.BlockSpec((None, None, seq, cb), lambda b, p, j: (p, b, 0, j)),
        out_shape=jax.ShapeDtypeStruct((3, batch, seq, c), F32),
        compiler_params=_cparams(("parallel", "parallel", "parallel")),
        name="hyena_shortconv",
    )(zh, conv_w.astype(F32), conv_b.reshape(1, c3).astype(F32))
    return out


def _filter_kernel(z_ref, w1_ref, b1_ref, w2_ref, b2_ref, w3_ref, b3_ref, fr_ref, w4_ref,
                   t_ref, d_ref, o_ref, *, tl):
    g = pl.program_id(0)
    i = pl.program_id(1)
    hi = lax.Precision.HIGHEST
    fr = fr_ref[...]
    h = jnp.sin(fr * (jnp.dot(z_ref[...], w1_ref[...], precision=hi, preferred_element_type=F32) + b1_ref[...]))
    h = jnp.sin(fr * (jnp.dot(h, w2_ref[...], precision=hi, preferred_element_type=F32) + b2_ref[...]))
    h = jnp.sin(fr * (jnp.dot(h, w3_ref[...], precision=hi, preferred_element_type=F32) + b3_ref[...]))
    f = jnp.dot(h, w4_ref[...], precision=hi, preferred_element_type=F32)
    f = f * jnp.exp(-t_ref[...] * d_ref[...])
    row = lax.broadcasted_iota(I32, f.shape, 0) + i * tl
    drop = jnp.logical_and(row == 0, g % 2 == 1)
    o_ref[...] = jnp.where(drop, 0.0, f)


def _hyena_filter_signals(seq, w1, b1, w2, b2, w3, b3, w4, freq, n_ch, tl=512):
    t = jnp.linspace(0.0, 1.0, seq, dtype=F32)[:, None]
    w = 2.0 * math.pi * jnp.arange(seq, dtype=F32)[:, None] / seq
    f = jnp.linspace(1e-4, FILTER_BANDS - 1, FILTER_BANDS, dtype=F32)[None, :]
    z = jnp.concatenate([t, jnp.cos(f * w), -jnp.sin(f * w)], axis=-1)
    deltas = jnp.abs(jnp.linspace(math.log(DECAY_FAST) / DECAY_TARGET,
                                  math.log(DECAY_SLOW) / DECAY_TARGET, n_ch, dtype=F32))[None, :]
    hid = w1.shape[1]
    emb = LANES
    z = jnp.pad(z, ((0, 0), (0, emb - z.shape[1])))
    w1 = jnp.pad(w1.astype(F32), ((0, emb - w1.shape[0]), (0, 0)))
    tl = min(tl, seq)
    full = lambda shape: pl.BlockSpec(shape, lambda g, i: tuple(0 for _ in shape))
    return pl.pallas_call(
        functools.partial(_filter_kernel, tl=tl),
        grid=(2 * HYENA_ORDER, seq // tl),
        in_specs=[pl.BlockSpec((tl, emb), lambda g, i: (i, 0)),
                  full((emb, hid)), full((1, hid)), full((hid, hid)), full((1, hid)),
                  full((hid, hid)), full((1, hid)), full((1, hid)),
                  pl.BlockSpec((hid, n_ch), lambda g, i: (0, g)),
                  pl.BlockSpec((tl, 1), lambda g, i: (i, 0)),
                  full((1, n_ch))],
        out_specs=pl.BlockSpec((None, tl, n_ch), lambda g, i: (g, i, 0)),
        out_shape=jax.ShapeDtypeStruct((2 * HYENA_ORDER, seq, n_ch), F32),
        compiler_params=_cparams(("parallel", "parallel")),
        name="hyena_filter_mlp",
    )(z, w1.astype(F32), b1.reshape(1, hid).astype(F32), w2.astype(F32), b2.reshape(1, hid).astype(F32),
      w3.astype(F32), b3.reshape(1, hid).astype(F32), freq.reshape(1, hid).astype(F32), w4.astype(F32),
      t, deltas)


def _dft_tables(r):
    n = r * r
    k2 = jnp.arange(r, dtype=I32)
    n2 = jnp.arange(r // 2, dtype=I32)
    ang1 = (2.0 * math.pi / r) * ((k2[:, None] * n2[None, :]) % r).astype(F32)
    f1 = jnp.concatenate([jnp.cos(ang1), -jnp.sin(ang1)], axis=0)
    k1 = jnp.arange(r, dtype=I32)
    n1 = jnp.arange(r, dtype=I32)
    kk = r * k1[None, :, None] + k2[:, None, None]
    ang2 = (2.0 * math.pi / n) * ((kk * n1[None, None, :]) % n).astype(F32)
    mr, mi_ = jnp.cos(ang2), -jnp.sin(ang2)
    mf = jnp.concatenate([jnp.concatenate([mr, -mi_], axis=2),
                          jnp.concatenate([mi_, mr], axis=2)], axis=1)
    minv = jnp.swapaxes(mf, 1, 2)
    g3 = jnp.concatenate([jnp.cos(ang1.T), -jnp.sin(ang1.T)], axis=1) * (1.0 / n)
    return f1.astype(BF16), mf.astype(BF16), minv.astype(BF16), g3.astype(BF16)


def _fft1_kernel(f_ref, x_ref, o_ref):
    o_ref[...] = jnp.dot(f_ref[...], x_ref[...].astype(BF16), preferred_element_type=F32).astype(o_ref.dtype)


def _fft_stage1(x4, part, f1, r, n_ch, w_lanes=4096):
    nb = x4.shape[1]
    xv = x4.reshape(x4.shape[0], nb, r // 2, r * n_ch)
    nw = (r * n_ch) // w_lanes
    return pl.pallas_call(
        _fft1_kernel,
        grid=(nb, nw),
        in_specs=[pl.BlockSpec((2 * r, r // 2), lambda b, j: (0, 0)),
                  pl.BlockSpec((None, None, r // 2, w_lanes), lambda b, j: (part, b, 0, j))],
        out_specs=pl.BlockSpec((None, 2 * r, w_lanes), lambda b, j: (b, 0, j)),
        out_shape=jax.ShapeDtypeStruct((nb, 2 * r, r * n_ch), BF16),
        compiler_params=_cparams(("parallel", "parallel")),
        name="hyena_dft_stage1",
    )(f1, xv)


def _filter_spec_kernel(mf_ref, bf_ref, bb_ref, o_ref, *, r):
    m = mf_ref[...]
    c = bf_ref.shape[-1]
    xf = jnp.dot(m, bf_ref[...].reshape(2 * r, c), preferred_element_type=F32)
    xb = jnp.dot(m, bb_ref[...].reshape(2 * r, c), preferred_element_type=F32)
    o_ref[0] = (xf[:r] + xb[:r]).astype(o_ref.dtype)
    o_ref[1] = (xf[r:] - xb[r:]).astype(o_ref.dtype)


def _filter_spectrum(b1, mf, r, n_ch):
    bv = b1.reshape(2 * HYENA_ORDER, 2, r, r, n_ch)
    blk = lambda sel: pl.BlockSpec((None, 2, None, r, n_ch), lambda o, k: (2 * o + sel, 0, k, 0, 0))
    return pl.pallas_call(
        functools.partial(_filter_spec_kernel, r=r),
        grid=(HYENA_ORDER, r),
        in_specs=[pl.BlockSpec((None, 2 * r, 2 * r), lambda o, k: (k, 0, 0)), blk(0), blk(1)],
        out_specs=pl.BlockSpec((None, 2, None, r, n_ch), lambda o, k: (o, 0, k, 0, 0)),
        out_shape=jax.ShapeDtypeStruct((HYENA_ORDER, 2, r, r, n_ch), BF16),
        compiler_params=_cparams(("parallel", "parallel")),
        name="hyena_filter_spectrum",
    )(mf, bv, bv)


def _fft2_kernel(mf_ref, mi_ref, b_ref, h_ref, o_ref, *, r):
    c = b_ref.shape[-1]
    x = jnp.dot(mf_ref[...], b_ref[...].reshape(2 * r, c), preferred_element_type=F32)
    xr, xi = x[:r], x[r:]
    hr = h_ref[0].astype(F32)
    hi = h_ref[1].astype(F32)
    y = jnp.concatenate([xr * hr - xi * hi, xr * hi + xi * hr], axis=0).astype(BF16)
    o_ref[...] = jnp.dot(mi_ref[...], y, preferred_element_type=F32).reshape(2, r, c).astype(o_ref.dtype)


def _fft_stage2(b1, h, order, mf, minv, r, n_ch):
    nb = b1.shape[0]
    bv = b1.reshape(nb, 2, r, r, n_ch)
    out = pl.pallas_call(
        functools.partial(_fft2_kernel, r=r),
        grid=(nb, r),
        in_specs=[pl.BlockSpec((None, 2 * r, 2 * r), lambda b, k: (k, 0, 0)),
                  pl.BlockSpec((None, 2 * r, 2 * r), lambda b, k: (k, 0, 0)),
                  pl.BlockSpec((None, 2, None, r, n_ch), lambda b, k: (b, 0, k, 0, 0)),
                  pl.BlockSpec((None, 2, None, r, n_ch), lambda b, k: (order, 0, k, 0, 0))],
        out_specs=pl.BlockSpec((None, 2, None, r, n_ch), lambda b, k: (b, 0, k, 0, 0)),
        out_shape=jax.ShapeDtypeStruct((nb, 2, r, r, n_ch), BF16),
        compiler_params=_cparams(("parallel", "parallel")),
        name="hyena_dft_stage2",
    )(mf, minv, bv, h)
    return out.reshape(nb, 2 * r, r * n_ch)


def _fft3_kernel(g_ref, c_ref, gate_ref, s_ref, bias_ref, ng_ref, o_ref, *, final, n_ch):
    y = jnp.dot(g_ref[...], c_ref[...], preferred_element_type=F32)
    s_new = gate_ref[...] * (y + s_ref[...] * bias_ref[...])
    if not final:
        o_ref[...] = s_new
        return
    for c in range(s_new.shape[1] // n_ch):
        sl = slice(c * n_ch, (c + 1) * n_ch)
        x = s_new[:, sl]
        ms = jnp.mean(x * x, axis=-1, keepdims=True)
        o_ref[:, sl] = (x * lax.rsqrt(ms + RMS_EPS) * ng_ref[...]).astype(o_ref.dtype)


def _fft_stage3(c2, g3, z4, gate_part, s4, s_part, bias, norm_g, final, r, n_ch, w_lanes=4096):
    nb = c2.shape[0]
    zv = z4.reshape(z4.shape[0], nb, r // 2, r * n_ch)
    sv = s4.reshape(s4.shape[0], nb, r // 2, r * n_ch)
    nw = (r * n_ch) // w_lanes
    bias_row = jnp.tile(bias.reshape(1, n_ch).astype(F32), (1, w_lanes // n_ch))
    out_dtype = BF16 if final else F32
    dspec = lambda part: pl.BlockSpec((None, None, r // 2, w_lanes), lambda b, j: (part, b, 0, j))
    out = pl.pallas_call(
        functools.partial(_fft3_kernel, final=final, n_ch=n_ch),
        grid=(nb, nw),
        in_specs=[pl.BlockSpec((r // 2, 2 * r), lambda b, j: (0, 0)),
                  pl.BlockSpec((None, 2 * r, w_lanes), lambda b, j: (b, 0, j)),
                  dspec(gate_part), dspec(s_part),
                  pl.BlockSpec((1, w_lanes), lambda b, j: (0, 0)),
                  pl.BlockSpec((1, n_ch), lambda b, j: (0, 0))],
        out_specs=pl.BlockSpec((None, r // 2, w_lanes), lambda b, j: (b, 0, j)),
        out_shape=jax.ShapeDtypeStruct((nb, r // 2, r * n_ch), out_dtype),
        compiler_params=_cparams(("parallel", "parallel")),
        name="hyena_dft_stage3",
    )(g3, c2, zv, sv, bias_row, norm_g.reshape(1, n_ch).astype(F32))
    return out


def _hyena(zh, row_off, batch, seq, conv_w, conv_b, filt, fft_bias, out_g):
    n_ch = zh.shape[1] // 3
    r = int(round(math.sqrt(2 * seq)))
    assert r * r == 2 * seq and r % 16 == 0
    f1, mf, minv, g3 = _dft_tables(r)
    sig = _hyena_filter_signals(seq, *filt, n_ch=n_ch)
    hb1 = _fft_stage1(sig[None], 0, f1, r, n_ch)
    h = _filter_spectrum(hb1, mf, r, n_ch)
    z4 = _shortconv(zh, row_off, conv_w, conv_b, batch, seq)
    s4, s_part = z4, 2
    for o in range(HYENA_ORDER):
        b1 = _fft_stage1(s4, s_part, f1, r, n_ch)
        c2 = _fft_stage2(b1, h, o, mf, minv, r, n_ch)
        final = o == HYENA_ORDER - 1
        s = _fft_stage3(c2, g3, z4, o, s4, s_part, fft_bias[o], out_g, final, r, n_ch)
        s4, s_part = s.reshape(1, batch, seq, n_ch), 0
    return s4.reshape(batch * seq, n_ch)


def _extract_top(s, key, count):
    vals, keys = [], []
    for _ in range(count):
        m = jnp.max(s, axis=0, keepdims=True)
        kmin = jnp.min(jnp.where(s == m, key, jnp.inf), axis=0, keepdims=True)
        s = jnp.where(key == kmin, -jnp.inf, s)
        vals.append(m)
        keys.append(kmin)
    return vals, keys


def _peer_topk_kernel(q_ref, keys_ref, eid_ref, gate_ref):
    t = q_ref.shape[0]
    nk = PEER_NKEYS
    q = q_ref[...]
    row_key = lax.broadcasted_iota(I32, (nk, t), 0).astype(F32)
    tops = []
    for c in range(2):
        s = lax.dot_general(keys_ref[c], q[:, c * nk:(c + 1) * nk], (((1,), (1,)), ((), ())),
                            preferred_element_type=F32)
        tops.append(_extract_top(s, row_key, PEER_TOPK))
    (v1, i1), (v2, i2) = tops
    rows16 = lax.broadcasted_iota(I32, (PEER_TOPK, t), 0)
    v2a = jnp.zeros((PEER_TOPK, t), F32)
    i2a = jnp.zeros((PEER_TOPK, t), F32)
    for j in range(PEER_TOPK):
        v2a = jnp.where(rows16 == j, v2[j], v2a)
        i2a = jnp.where(rows16 == j, i2[j], i2a)
    pos16 = rows16.astype(F32)
    n_exp = float(nk * nk)
    cand = jnp.concatenate([v1[i] + v2a for i in range(PEER_TOPK)], axis=0)
    ckey = jnp.concatenate([(pos16 + float(i * PEER_TOPK)) * n_exp + (i1[i] * float(nk) + i2a)
                            for i in range(PEER_TOPK)], axis=0)
    tv, tk_ = _extract_top(cand, ckey, PEER_TOPK)
    denom = jnp.zeros((1, t), F32)
    es = []
    for k in range(PEER_TOPK):
        e = jnp.exp(tv[k] - tv[0])
        es.append(e)
        denom = denom + e
    eid = jnp.zeros((PEER_TOPK, t), F32)
    gate = jnp.zeros((PEER_TOPK, t), F32)
    for k in range(PEER_TOPK):
        pos = jnp.floor(tk_[k] * (1.0 / n_exp))
        eid = jnp.where(rows16 == k, tk_[k] - pos * n_exp, eid)
        gate = jnp.where(rows16 == k, es[k] / denom, gate)
    eid_ref[...] = eid.astype(I32)
    gate_ref[...] = gate


def _peer_topk(q, keys, tm=256):
    t = q.shape[0]
    tm = min(tm, t)
    out_spec = pl.BlockSpec((PEER_TOPK, tm), lambda i, h: (h, i))
    return pl.pallas_call(
        _peer_topk_kernel,
        grid=(t // tm, PEER_HEADS),
        in_specs=[pl.BlockSpec((tm, 2 * PEER_NKEYS), lambda i, h: (i, h)),
                  pl.BlockSpec((None, 2, PEER_NKEYS, PEER_NKEYS), lambda i, h: (h, 0, 0, 0))],
        out_specs=[out_spec, out_spec],
        out_shape=[jax.ShapeDtypeStruct((PEER_HEADS * PEER_TOPK, t), I32),
                   jax.ShapeDtypeStruct((PEER_HEADS * PEER_TOPK, t), F32)],
        compiler_params=_cparams(("parallel", "parallel")),
        name="peer_topk",
    )(q, keys)


def _pack_table(tab):
    e, d = tab.shape
    bits = lax.bitcast_convert_type(tab.astype(BF16), jnp.uint16).astype(U32)
    packed = bits[:, :d // 2] | (bits[:, d // 2:] << 16)
    return packed.reshape(e, d // (2 * LANES), LANES)


def _unpack(w):
    lo = lax.bitcast_convert_type(w << 16, F32)
    hi = lax.bitcast_convert_type(w & jnp.uint32(0xFFFF0000), F32)
    return lo, hi


_BITREV8 = (0, 4, 2, 6, 1, 5, 3, 7)


def _sublane_fold8(parts):
    sub = lax.broadcasted_iota(I32, (SUBLANES, LANES), 0)
    lvl = [parts[_BITREV8[r]] for r in range(8)]
    for shift, mask in ((4, sub < 4), (2, (sub % 4) < 2), (1, (sub % 2) < 1)):
        nxt = []
        for a, b in zip(lvl[0::2], lvl[1::2]):
            nxt.append(jnp.where(mask, a + pltpu.roll(a, SUBLANES - shift, axis=0),
                                 b + pltpu.roll(b, shift, axis=0)))
        lvl = nxt
    return lvl[0]


def _peer_u_kernel(eid_ref, tab_ref, x_ref, o_ref, *, tm):
    lane = lax.broadcasted_iota(I32, (SUBLANES, LANES), 1)
    sub = lax.broadcasted_iota(I32, (SUBLANES, LANES), 0)
    lane_grp = lax.shift_right_logical(lane, 3)
    diag = sub == (lane & (SUBLANES - 1))
    n_pairs = PEER_HEADS * PEER_TOPK

    def token(t, carry):
        xlo = x_ref[t, 0]
        xhi = x_ref[t, 1]
        mat = jnp.zeros((SUBLANES, LANES), F32)
        for g in range(n_pairs // SUBLANES):
            parts = []
            for r in range(SUBLANES):
                e = eid_ref[t, g * SUBLANES + r]
                lo, hi = _unpack(tab_ref[e & (PEER_HALF - 1)])
                parts.append(lo * xlo + hi * xhi)
            col = jnp.sum(_sublane_fold8(parts), axis=-1, keepdims=True)
            mat = jnp.where(lane_grp == g, col, mat)
        row = jnp.sum(jnp.where(diag, mat, 0.0), axis=0, keepdims=True)
        o_ref[pl.ds(t, 1), :] = row
        return carry

    lax.fori_loop(0, tm, token, 0)


def _peer_u(eid, tab, x4, tm=128):
    t = eid.shape[0]
    tm = min(tm, t)
    return pl.pallas_call(
        functools.partial(_peer_u_kernel, tm=tm),
        grid=(2, t // tm),
        in_specs=[pl.BlockSpec((tm, LANES), lambda h, i: (i, 0), memory_space=pltpu.SMEM),
                  pl.BlockSpec((PEER_HALF, SUBLANES, LANES), lambda h, i: (h, 0, 0),
                               pipeline_mode=pl.Buffered(1)),
                  pl.BlockSpec((tm, 2, SUBLANES, LANES), lambda h, i: (i, 0, 0, 0))],
        out_specs=pl.BlockSpec((None, tm, LANES), lambda h, i: (h, i, 0)),
        out_shape=jax.ShapeDtypeStruct((2, t, LANES), F32),
        compiler_params=_cparams(("arbitrary", "arbitrary")),
        name="peer_expert_scores",
    )(eid, tab, x4)


def _peer_coef_kernel(a_ref, eid_ref, gate_ref, o_ref):
    low = eid_ref[...] < PEER_HALF
    a = jnp.where(low, a_ref[0], a_ref[1])
    coef = gate_ref[...] * (0.5 * a * (1.0 + lax.erf(a * (1.0 / math.sqrt(2.0)))))
    o_ref[0] = jnp.where(low, coef, 0.0)
    o_ref[1] = jnp.where(low, 0.0, coef)


def _peer_coef(a2, eid, gate, tm=1024):
    t = eid.shape[0]
    tm = min(tm, t)
    s2 = pl.BlockSpec((2, tm, LANES), lambda i: (0, i, 0))
    s1 = pl.BlockSpec((tm, LANES), lambda i: (i, 0))
    return pl.pallas_call(
        _peer_coef_kernel,
        grid=(t // tm,),
        in_specs=[s2, s1, s1],
        out_specs=s2,
        out_shape=jax.ShapeDtypeStruct((2, t, LANES), F32),
        compiler_params=_cparams(("parallel",)),
        name="peer_coef",
    )(a2, eid, gate)


def _peer_v_kernel(eid_ref, coef_ref, tab_ref, o_ref, *, tm):
    n_pairs = PEER_HEADS * PEER_TOPK
    n_acc = 4

    def token(t, carry):
        acc_lo = [jnp.zeros((SUBLANES, LANES), F32) for _ in range(n_acc)]
        acc_hi = [jnp.zeros((SUBLANES, LANES), F32) for _ in range(n_acc)]
        for p in range(n_pairs):
            e = eid_ref[t, p]
            c = coef_ref[t, p]
            lo, hi = _unpack(tab_ref[e & (PEER_HALF - 1)])
            acc_lo[p % n_acc] = acc_lo[p % n_acc] + c * lo
            acc_hi[p % n_acc] = acc_hi[p % n_acc] + c * hi
        o_ref[t, 0] = (acc_lo[0] + acc_lo[1]) + (acc_lo[2] + acc_lo[3])
        o_ref[t, 1] = (acc_hi[0] + acc_hi[1]) + (acc_hi[2] + acc_hi[3])
        return carry

    lax.fori_loop(0, tm, token, 0)


def _peer_v(eid, coef2, tab, tm=128):
    t = eid.shape[0]
    tm = min(tm, t)
    return pl.pallas_call(
        functools.partial(_peer_v_kernel, tm=tm),
        grid=(2, t // tm),
        in_specs=[pl.BlockSpec((tm, LANES), lambda h, i: (i, 0), memory_space=pltpu.SMEM),
                  pl.BlockSpec((None, tm, LANES), lambda h, i: (h, i, 0), memory_space=pltpu.SMEM),
                  pl.BlockSpec((PEER_HALF, SUBLANES, LANES), lambda h, i: (h, 0, 0),
                               pipeline_mode=pl.Buffered(1))],
        out_specs=pl.BlockSpec((None, tm, 2, SUBLANES, LANES), lambda h, i: (h, i, 0, 0, 0)),
        out_shape=jax.ShapeDtypeStruct((2, t, 2, SUBLANES, LANES), F32),
        compiler_params=_cparams(("arbitrary", "arbitrary")),
        name="peer_expert_sum",
    )(eid, coef2, tab)


def _add3_kernel(x_ref, p_ref, o_ref):
    o_ref[...] = x_ref[...] + (p_ref[0] + p_ref[1])


def _add3(x, p2, tm=512):
    t, d = x.shape
    tm = min(tm, t)
    return pl.pallas_call(
        _add3_kernel,
        grid=(t // tm,),
        in_specs=[pl.BlockSpec((tm, d), lambda i: (i, 0)), pl.BlockSpec((2, tm, d), lambda i: (0, i, 0))],
        out_specs=pl.BlockSpec((tm, d), lambda i: (i, 0)),
        out_shape=jax.ShapeDtypeStruct((t, d), F32),
        compiler_params=_cparams(("parallel",)),
        name="peer_residual",
    )(x, p2)


def _peer(x1, norm2_g, wq_bf, keys_bf, u_packed, v_packed):
    t, d = x1.shape
    xn_bf, xn_f32 = _rmsnorm(x1, norm2_g, (BF16, F32))
    q = _matmul(xn_bf, wq_bf, wq_bf.shape[1], 0, BF16)
    eid_t, gate_t = _peer_topk(q, keys_bf)
    eid, gate = eid_t.T, gate_t.T
    x4 = xn_f32.reshape(t, 2, SUBLANES, LANES)
    a2 = _peer_u(eid, u_packed, x4)
    coef2 = _peer_coef(a2, eid, gate)
    out2 = _peer_v(eid, coef2, v_packed)
    return _add3(x1, out2.reshape(2, t, d))


def kernel(x_prompt, x_sample, norm1_g, w_in, q_norm_g, k_norm_g, lambda_q1, lambda_k1, lambda_q2, lambda_k2, attn_sub_g, conv_w, conv_b, filt_w1, filt_b1, filt_w2, filt_b2, filt_w3, filt_b3, filt_w4, filt_freq, fft_bias, hyena_out_g, w_out, norm2_g, peer_wq, peer_keys, peer_u, peer_v):
    depth = w_in.shape[0]
    d_model = x_prompt.shape[-1]
    att_w = ATT_HEADS * ATT_VDIM
    shapes = [x_prompt.shape[:2], x_sample.shape[:2]]
    x = jnp.concatenate([x_prompt.reshape(-1, d_model), x_sample.reshape(-1, d_model)], axis=0)
    slopes = 2.0 ** (-8.0 * jnp.arange(1, ATT_HEADS + 1, dtype=F32) / ATT_HEADS)

    for l in range(depth):
        lambda_init = 0.8 - 0.6 * math.exp(-0.3 * l)
        lam = (jnp.exp(jnp.sum(lambda_q1[l].astype(F32) * lambda_k1[l].astype(F32)))
               - jnp.exp(jnp.sum(lambda_q2[l].astype(F32) * lambda_k2[l].astype(F32)))
               + lambda_init).reshape(1)
        w_in_bf = w_in[l].astype(BF16)
        q_gain = jnp.tile(q_norm_g[l].astype(F32), 2 * ATT_HEADS) * (ATT_QKDIM ** -0.5 * LOG2E)
        k_gain = jnp.tile(k_norm_g[l].astype(F32), 2 * ATT_HEADS)
        qk_gain = jnp.concatenate([q_gain, k_gain]).reshape(1, 2 * att_w)

        (h_bf,) = _rmsnorm(x, norm1_g[l], (BF16,))
        qk = _matmul(h_bf, w_in_bf, 2 * att_w, 0, BF16, mode="qknorm", extra=qk_gain)
        v = _matmul(h_bf, w_in_bf, att_w, 2 * att_w, BF16)
        zh = _matmul(h_bf, w_in_bf, w_in.shape[2] - 3 * att_w, 3 * att_w, F32)

        filt = (filt_w1[l], filt_b1[l], filt_w2[l], filt_b2[l], filt_w3[l], filt_b3[l], filt_w4[l], filt_freq[l])
        mixes, row = [], 0
        for (b, s) in shapes:
            n = b * s
            att = _attention(qk, v, row, slopes, lam, attn_sub_g[l], b, s, 1.0 - lambda_init)
            hy = _hyena(zh, row, b, s, conv_w[l], conv_b[l], filt, fft_bias[l], hyena_out_g[l])
            mixes.append(jnp.concatenate([att, hy], axis=-1))
            row += n
        mix = jnp.concatenate(mixes, axis=0)
        x = _matmul(mix, w_out[l].astype(BF16), d_model, 0, F32, mode="residual", extra=x)

        x = _peer(x, norm2_g[l], peer_wq[l].astype(BF16), peer_keys[l].astype(BF16),
                  _pack_table(peer_u[l]), _pack_table(peer_v[l]))

    n0 = shapes[0][0] * shapes[0][1]
    return (x[:n0].reshape(x_prompt.shape), x[n0:].reshape(x_sample.shape))
```

```python
import functools
import math

import jax
import jax.numpy as jnp
from jax import lax
from jax.experimental import pallas as pl
from jax.experimental.pallas import tpu as pltpu

F32 = jnp.float32
BF16 = jnp.bfloat16
I32 = jnp.int32
U32 = jnp.uint32

RMS_EPS = 1e-6
LOG2E = 1.4426950408889634
LANES = 128
SUBLANES = 8
VMEM_LIMIT_BYTES = 56 * 1024 * 1024

ATT_HEADS = 8
ATT_VDIM = 128
ATT_QKDIM = 64
HYENA_ORDER = 2
FILTER_BANDS = 16
DECAY_FAST = 0.3
DECAY_SLOW = 1.5
DECAY_TARGET = 1e-2
PEER_HEADS = 8
PEER_NKEYS = 128
PEER_TOPK = 16
PEER_HALF = PEER_NKEYS * PEER_NKEYS // 2
PEER_PAIRS = PEER_HEADS * PEER_TOPK
PEER_WINDOW = 80


def _cparams(sem, vmem=VMEM_LIMIT_BYTES):
    return pltpu.CompilerParams(dimension_semantics=sem, vmem_limit_bytes=vmem)


def _rmsnorm_kernel(x_ref, g_ref, *o_refs):
    x = x_ref[...]
    ms = jnp.mean(x * x, axis=-1, keepdims=True)
    y = x * lax.rsqrt(ms + RMS_EPS) * g_ref[...]
    for o_ref in o_refs:
        o_ref[...] = y.astype(o_ref.dtype)


def _rmsnorm(x, g, out_dtypes, tm=512):
    t, d = x.shape
    tm = min(tm, t)
    spec = pl.BlockSpec((tm, d), lambda i: (i, 0))
    return pl.pallas_call(
        _rmsnorm_kernel,
        grid=(t // tm,),
        in_specs=[spec, pl.BlockSpec((1, d), lambda i: (0, 0))],
        out_specs=[spec for _ in out_dtypes],
        out_shape=[jax.ShapeDtypeStruct((t, d), dt) for dt in out_dtypes],
        compiler_params=_cparams(("parallel",)),
        name="rmsnorm",
    )(x, g.reshape(1, d).astype(F32))


def _group_rms_scale(x, gain):
    lane = lax.broadcasted_iota(I32, x.shape, 1)
    lo = lane < ATT_QKDIM
    x2 = x * x
    s_lo = jnp.sum(jnp.where(lo, x2, 0.0), axis=-1, keepdims=True)
    s_hi = jnp.sum(jnp.where(lo, 0.0, x2), axis=-1, keepdims=True)
    ms = jnp.where(lo, s_lo, s_hi) * (1.0 / ATT_QKDIM)
    return x * lax.rsqrt(ms + RMS_EPS) * gain


def _mm_kernel(a_ref, b_ref, *rest, mode):
    acc = jnp.dot(a_ref[...], b_ref[...], preferred_element_type=F32)
    if mode == "plain":
        (o_ref,) = rest
        o_ref[...] = acc.astype(o_ref.dtype)
    elif mode == "residual":
        r_ref, o_ref = rest
        o_ref[...] = (acc + r_ref[...]).astype(o_ref.dtype)
    elif mode == "qknorm":
        g_ref, o_ref = rest
        for c in range(acc.shape[1] // LANES):
            sl = slice(c * LANES, (c + 1) * LANES)
            o_ref[:, sl] = _group_rms_scale(acc[:, sl], g_ref[:, sl]).astype(o_ref.dtype)
    else:
        raise ValueError(mode)


def _matmul(a, b, n_cols, col_off, out_dtype, mode="plain", extra=None, tm=1024, tn=1024):
    m, k = a.shape
    tm = min(tm, m)
    tn = min(tn, n_cols)
    assert col_off % tn == 0 and n_cols % tn == 0 and m % tm == 0
    off = col_off // tn
    in_specs = [pl.BlockSpec((tm, k), lambda i, j: (i, 0)),
                pl.BlockSpec((k, tn), lambda i, j: (0, j + off))]
    args = [a, b]
    if mode == "residual":
        in_specs.append(pl.BlockSpec((tm, tn), lambda i, j: (i, j)))
        args.append(extra)
    elif mode == "qknorm":
        in_specs.append(pl.BlockSpec((1, tn), lambda i, j: (0, j)))
        args.append(extra)
    return pl.pallas_call(
        functools.partial(_mm_kernel, mode=mode),
        grid=(m // tm, n_cols // tn),
        in_specs=in_specs,
        out_specs=pl.BlockSpec((tm, tn), lambda i, j: (i, j)),
        out_shape=jax.ShapeDtypeStruct((m, n_cols), out_dtype),
        compiler_params=_cparams(("parallel", "arbitrary")),
        name="matmul_" + mode,
    )(*args)


def _attn_kernel(slope_ref, lam_ref, q_ref, k_ref, v_ref, g_ref, o_ref,
                 m0_ref, l0_ref, a0_ref, m1_ref, l1_ref, a1_ref, *, tq, tk, nk, out_scale):
    h = pl.program_id(1)
    i = pl.program_id(2)
    j = pl.program_id(3)

    @pl.when(j == 0)
    def _():
        for m_ref, l_ref, a_ref in ((m0_ref, l0_ref, a0_ref), (m1_ref, l1_ref, a1_ref)):
            m_ref[...] = jnp.full(m_ref.shape, -jnp.inf, F32)
            l_ref[...] = jnp.zeros(l_ref.shape, F32)
            a_ref[...] = jnp.zeros(a_ref.shape, F32)

    q = q_ref[...]
    k = k_ref[...]
    v = v_ref[...]
    lane = lax.broadcasted_iota(I32, q.shape, 1)
    zero = jnp.zeros_like(q)
    q_maps = (jnp.where(lane < ATT_QKDIM, q, zero), jnp.where(lane < ATT_QKDIM, zero, q))

    rows = lax.broadcasted_iota(I32, (tq, tk), 0) + i * tq
    cols = lax.broadcasted_iota(I32, (tq, tk), 1) + j * tk
    dist = jnp.abs(rows - cols).astype(F32)
    bias = dist * (-(slope_ref[h] * LOG2E))

    for qm, m_ref, l_ref, a_ref in ((q_maps[0], m0_ref, l0_ref, a0_ref),
                                    (q_maps[1], m1_ref, l1_ref, a1_ref)):
        s = lax.dot_general(qm, k, (((1,), (1,)), ((), ())), preferred_element_type=F32) + bias
        m_old = m_ref[...]
        m_new = jnp.maximum(m_old, jnp.max(s, axis=-1, keepdims=True))
        alpha = jnp.exp2(m_old - m_new)
        p = jnp.exp2(s - m_new)
        l_ref[...] = alpha * l_ref[...] + jnp.sum(p, axis=-1, keepdims=True)
        a_ref[...] = alpha * a_ref[...] + jnp.dot(p.astype(BF16), v, preferred_element_type=F32)
        m_ref[...] = m_new

    @pl.when(j == nk - 1)
    def _():
        o = a0_ref[...] / l0_ref[...] - lam_ref[0] * (a1_ref[...] / l1_ref[...])
        ms = jnp.mean(o * o, axis=-1, keepdims=True)
        o_ref[...] = (o * lax.rsqrt(ms + RMS_EPS) * g_ref[...] * out_scale).astype(o_ref.dtype)


def _attention(qk, v, row_off, slopes, lam, sub_g, batch, seq, out_scale, tq=512, tk=1024):
    tq = min(tq, seq)
    tk = min(tk, seq)
    nq, nk = seq // tq, seq // tk
    assert row_off % tq == 0 and row_off % tk == 0
    oq, ok = row_off // tq, row_off // tk
    kern = functools.partial(_attn_kernel, tq=tq, tk=tk, nk=nk, out_scale=out_scale)
    smem = pl.BlockSpec(memory_space=pltpu.SMEM)
    return pl.pallas_call(
        kern,
        grid=(batch, ATT_HEADS, nq, nk),
        in_specs=[smem, smem,
                  pl.BlockSpec((tq, LANES), lambda b, h, i, j: (oq + b * nq + i, h)),
                  pl.BlockSpec((tk, LANES), lambda b, h, i, j: (ok + b * nk + j, ATT_HEADS + h)),
                  pl.BlockSpec((tk, LANES), lambda b, h, i, j: (ok + b * nk + j, h)),
                  pl.BlockSpec((1, LANES), lambda b, h, i, j: (0, 0))],
        out_specs=pl.BlockSpec((tq, LANES), lambda b, h, i, j: (b * nq + i, h)),
        out_shape=jax.ShapeDtypeStruct((batch * seq, ATT_HEADS * ATT_VDIM), BF16),
        scratch_shapes=[pltpu.VMEM((tq, 1), F32), pltpu.VMEM((tq, 1), F32), pltpu.VMEM((tq, LANES), F32),
                        pltpu.VMEM((tq, 1), F32), pltpu.VMEM((tq, 1), F32), pltpu.VMEM((tq, LANES), F32)],
        compiler_params=_cparams(("parallel", "parallel", "parallel", "arbitrary")),
        name="diff_attention",
    )(slopes, lam, qk, qk, v, sub_g.reshape(1, LANES).astype(F32))


def _shortconv_kernel(z_ref, w_ref, b_ref, o_ref):
    z = z_ref[...]
    n = z.shape[0]
    row = lax.broadcasted_iota(I32, z.shape, 0)
    prev = jnp.where(row == 0, 0.0, pltpu.roll(z, 1, axis=0))
    nxt = jnp.where(row == n - 1, 0.0, pltpu.roll(z, n - 1, axis=0))
    w = w_ref[...]
    o_ref[...] = prev * w[0:1] + z * w[1:2] + nxt * w[2:3] + b_ref[...]


def _shortconv(zh, row_off, conv_w, conv_b, batch, seq, cb=256):
    c3 = zh.shape[1]
    c = c3 // 3
    ncb = c // cb
    assert row_off % seq == 0
    ob = row_off // seq
    out = pl.pallas_call(
        _shortconv_kernel,
        grid=(batch, 3, ncb),
        in_specs=[pl.BlockSpec((seq, cb), lambda b, p, j: (ob + b, p * ncb + j)),
                  pl.BlockSpec((3, cb), lambda b, p, j: (0, p * ncb + j)),
                  pl.BlockSpec((1, cb), lambda b, p, j: (0, p * ncb + j))],
        out_specs=pl.BlockSpec((None, None, seq, cb), lambda b, p, j: (p, b, 0, j)),
        out_shape=jax.ShapeDtypeStruct((3, batch, seq, c), F32),
        compiler_params=_cparams(("parallel", "parallel", "parallel")),
        name="hyena_shortconv",
    )(zh, conv_w.astype(F32), conv_b.reshape(1, c3).astype(F32))
    return out


def _filter_kernel(z_ref, w1_ref, b1_ref, w2_ref, b2_ref, w3_ref, b3_ref, fr_ref, w4_ref,
                   t_ref, d_ref, o_ref, *, tl):
    g = pl.program_id(0)
    i = pl.program_id(1)
    hi = lax.Precision.HIGHEST
    fr = fr_ref[...]
    h = jnp.sin(fr * (jnp.dot(z_ref[...], w1_ref[...], precision=hi, preferred_element_type=F32) + b1_ref[...]))
    h = jnp.sin(fr * (jnp.dot(h, w2_ref[...], precision=hi, preferred_element_type=F32) + b2_ref[...]))
    h = jnp.sin(fr * (jnp.dot(h, w3_ref[...], precision=hi, preferred_element_type=F32) + b3_ref[...]))
    f = jnp.dot(h, w4_ref[...], precision=hi, preferred_element_type=F32)
    f = f * jnp.exp(-t_ref[...] * d_ref[...])
    row = lax.broadcasted_iota(I32, f.shape, 0) + i * tl
    drop = jnp.logical_and(row == 0, g % 2 == 1)
    o_ref[...] = jnp.where(drop, 0.0, f)


def _hyena_filter_signals(seq, w1, b1, w2, b2, w3, b3, w4, freq, n_ch, tl=512):
    t = jnp.linspace(0.0, 1.0, seq, dtype=F32)[:, None]
    w = 2.0 * math.pi * jnp.arange(seq, dtype=F32)[:, None] / seq
    f = jnp.linspace(1e-4, FILTER_BANDS - 1, FILTER_BANDS, dtype=F32)[None, :]
    z = jnp.concatenate([t, jnp.cos(f * w), -jnp.sin(f * w)], axis=-1)
    deltas = jnp.abs(jnp.linspace(math.log(DECAY_FAST) / DECAY_TARGET,
                                  math.log(DECAY_SLOW) / DECAY_TARGET, n_ch, dtype=F32))[None, :]
    hid = w1.shape[1]
    emb = LANES
    z = jnp.pad(z, ((0, 0), (0, emb - z.shape[1])))
    w1 = jnp.pad(w1.astype(F32), ((0, emb - w1.shape[0]), (0, 0)))
    tl = min(tl, seq)
    full = lambda shape: pl.BlockSpec(shape, lambda g, i: tuple(0 for _ in shape))
    return pl.pallas_call(
        functools.partial(_filter_kernel, tl=tl),
        grid=(2 * HYENA_ORDER, seq // tl),
        in_specs=[pl.BlockSpec((tl, emb), lambda g, i: (i, 0)),
                  full((emb, hid)), full((1, hid)), full((hid, hid)), full((1, hid)),
                  full((hid, hid)), full((1, hid)), full((1, hid)),
                  pl.BlockSpec((hid, n_ch), lambda g, i: (0, g)),
                  pl.BlockSpec((tl, 1), lambda g, i: (i, 0)),
                  full((1, n_ch))],
        out_specs=pl.BlockSpec((None, tl, n_ch), lambda g, i: (g, i, 0)),
        out_shape=jax.ShapeDtypeStruct((2 * HYENA_ORDER, seq, n_ch), F32),
        compiler_params=_cparams(("parallel", "parallel")),
        name="hyena_filter_mlp",
    )(z, w1.astype(F32), b1.reshape(1, hid).astype(F32), w2.astype(F32), b2.reshape(1, hid).astype(F32),
      w3.astype(F32), b3.reshape(1, hid).astype(F32), freq.reshape(1, hid).astype(F32), w4.astype(F32),
      t, deltas)


def _dft_tables(r):
    n = r * r
    k2 = jnp.arange(r, dtype=I32)
    n2 = jnp.arange(r // 2, dtype=I32)
    ang1 = (2.0 * math.pi / r) * ((k2[:, None] * n2[None, :]) % r).astype(F32)
    f1 = jnp.concatenate([jnp.cos(ang1), -jnp.sin(ang1)], axis=0)
    k1 = jnp.arange(r, dtype=I32)
    n1 = jnp.arange(r, dtype=I32)
    kk = r * k1[None, :, None] + k2[:, None, None]
    ang2 = (2.0 * math.pi / n) * ((kk * n1[None, None, :]) % n).astype(F32)
    mr, mi_ = jnp.cos(ang2), -jnp.sin(ang2)
    mf = jnp.concatenate([jnp.concatenate([mr, -mi_], axis=2),
                          jnp.concatenate([mi_, mr], axis=2)], axis=1)
    minv = jnp.swapaxes(mf, 1, 2)
    g3 = jnp.concatenate([jnp.cos(ang1.T), -jnp.sin(ang1.T)], axis=1) * (1.0 / n)
    return f1.astype(BF16), mf.astype(BF16), minv.astype(BF16), g3.astype(BF16)


def _fft1_kernel(f_ref, x_ref, o_ref):
    o_ref[...] = jnp.dot(f_ref[...], x_ref[...].astype(BF16), preferred_element_type=F32).astype(o_ref.dtype)


def _fft_stage1(x4, part, f1, r, n_ch, w_lanes=4096):
    nb = x4.shape[1]
    xv = x4.reshape(x4.shape[0], nb, r // 2, r * n_ch)
    nw = (r * n_ch) // w_lanes
    return pl.pallas_call(
        _fft1_kernel,
        grid=(nb, nw),
        in_specs=[pl.BlockSpec((2 * r, r // 2), lambda b, j: (0, 0)),
                  pl.BlockSpec((None, None, r // 2, w_lanes), lambda b, j: (part, b, 0, j))],
        out_specs=pl.BlockSpec((None, 2 * r, w_lanes), lambda b, j: (b, 0, j)),
        out_shape=jax.ShapeDtypeStruct((nb, 2 * r, r * n_ch), BF16),
        compiler_params=_cparams(("parallel", "parallel")),
        name="hyena_dft_stage1",
    )(f1, xv)


def _filter_spec_kernel(mf_ref, bf_ref, bb_ref, o_ref, *, r):
    m = mf_ref[...]
    c = bf_ref.shape[-1]
    xf = jnp.dot(m, bf_ref[...].reshape(2 * r, c), preferred_element_type=F32)
    xb = jnp.dot(m, bb_ref[...].reshape(2 * r, c), preferred_element_type=F32)
    o_ref[0] = (xf[:r] + xb[:r]).astype(o_ref.dtype)
    o_ref[1] = (xf[r:] - xb[r:]).astype(o_ref.dtype)


def _filter_spectrum(b1, mf, r, n_ch):
    bv = b1.reshape(2 * HYENA_ORDER, 2, r, r, n_ch)
    blk = lambda sel: pl.BlockSpec((None, 2, None, r, n_ch), lambda o, k: (2 * o + sel, 0, k, 0, 0))
    return pl.pallas_call(
        functools.partial(_filter_spec_kernel, r=r),
        grid=(HYENA_ORDER, r),
        in_specs=[pl.BlockSpec((None, 2 * r, 2 * r), lambda o, k: (k, 0, 0)), blk(0), blk(1)],
        out_specs=pl.BlockSpec((None, 2, None, r, n_ch), lambda o, k: (o, 0, k, 0, 0)),
        out_shape=jax.ShapeDtypeStruct((HYENA_ORDER, 2, r, r, n_ch), BF16),
        compiler_params=_cparams(("parallel", "parallel")),
        name="hyena_filter_spectrum",
    )(mf, bv, bv)


def _fft2_kernel(mf_ref, mi_ref, b_ref, h_ref, o_ref, *, r):
    c = b_ref.shape[-1]
    x = jnp.dot(mf_ref[...], b_ref[...].reshape(2 * r, c), preferred_element_type=F32)
    xr, xi = x[:r], x[r:]
    hr = h_ref[0].astype(F32)
    hi = h_ref[1].astype(F32)
    y = jnp.concatenate([xr * hr - xi * hi, xr * hi + xi * hr], axis=0).astype(BF16)
    o_ref[...] = jnp.dot(mi_ref[...], y, preferred_element_type=F32).reshape(2, r, c).astype(o_ref.dtype)


def _fft_stage2(b1, h, order, mf, minv, r, n_ch):
    nb = b1.shape[0]
    bv = b1.reshape(nb, 2, r, r, n_ch)
    out = pl.pallas_call(
        functools.partial(_fft2_kernel, r=r),
        grid=(nb, r),
        in_specs=[pl.BlockSpec((None, 2 * r, 2 * r), lambda b, k: (k, 0, 0)),
                  pl.BlockSpec((None, 2 * r, 2 * r), lambda b, k: (k, 0, 0)),
                  pl.BlockSpec((None, 2, None, r, n_ch), lambda b, k: (b, 0, k, 0, 0)),
                  pl.BlockSpec((None, 2, None, r, n_ch), lambda b, k: (order, 0, k, 0, 0))],
        out_specs=pl.BlockSpec((None, 2, None, r, n_ch), lambda b, k: (b, 0, k, 0, 0)),
        out_shape=jax.ShapeDtypeStruct((nb, 2, r, r, n_ch), BF16),
        compiler_params=_cparams(("parallel", "parallel")),
        name="hyena_dft_stage2",
    )(mf, minv, bv, h)
    return out.reshape(nb, 2 * r, r * n_ch)


def _fft3_kernel(g_ref, c_ref, gate_ref, s_ref, bias_ref, ng_ref, o_ref, *, final, n_ch):
    y = jnp.dot(g_ref[...], c_ref[...], preferred_element_type=F32)
    s_new = gate_ref[...] * (y + s_ref[...] * bias_ref[...])
    if not final:
        o_ref[...] = s_new
        return
    for c in range(s_new.shape[1] // n_ch):
        sl = slice(c * n_ch, (c + 1) * n_ch)
        x = s_new[:, sl]
        ms = jnp.mean(x * x, axis=-1, keepdims=True)
        o_ref[:, sl] = (x * lax.rsqrt(ms + RMS_EPS) * ng_ref[...]).astype(o_ref.dtype)


def _fft_stage3(c2, g3, z4, gate_part, s4, s_part, bias, norm_g, final, r, n_ch, w_lanes=4096):
    nb = c2.shape[0]
    zv = z4.reshape(z4.shape[0], nb, r // 2, r * n_ch)
    sv = s4.reshape(s4.shape[0], nb, r // 2, r * n_ch)
    nw = (r * n_ch) // w_lanes
    bias_row = jnp.tile(bias.reshape(1, n_ch).astype(F32), (1, w_lanes // n_ch))
    out_dtype = BF16 if final else F32
    dspec = lambda part: pl.BlockSpec((None, None, r // 2, w_lanes), lambda b, j: (part, b, 0, j))
    out = pl.pallas_call(
        functools.partial(_fft3_kernel, final=final, n_ch=n_ch),
        grid=(nb, nw),
        in_specs=[pl.BlockSpec((r // 2, 2 * r), lambda b, j: (0, 0)),
                  pl.BlockSpec((None, 2 * r, w_lanes), lambda b, j: (b, 0, j)),
                  dspec(gate_part), dspec(s_part),
                  pl.BlockSpec((1, w_lanes), lambda b, j: (0, 0)),
                  pl.BlockSpec((1, n_ch), lambda b, j: (0, 0))],
        out_specs=pl.BlockSpec((None, r // 2, w_lanes), lambda b, j: (b, 0, j)),
        out_shape=jax.ShapeDtypeStruct((nb, r // 2, r * n_ch), out_dtype),
        compiler_params=_cparams(("parallel", "parallel")),
        name="hyena_dft_stage3",
    )(g3, c2, zv, sv, bias_row, norm_g.reshape(1, n_ch).astype(F32))
    return out


def _hyena(zh, row_off, batch, seq, conv_w, conv_b, filt, fft_bias, out_g):
    n_ch = zh.shape[1] // 3
    r = int(round(math.sqrt(2 * seq)))
    assert r * r == 2 * seq and r % 16 == 0
    f1, mf, minv, g3 = _dft_tables(r)
    sig = _hyena_filter_signals(seq, *filt, n_ch=n_ch)
    hb1 = _fft_stage1(sig[None], 0, f1, r, n_ch)
    h = _filter_spectrum(hb1, mf, r, n_ch)
    z4 = _shortconv(zh, row_off, conv_w, conv_b, batch, seq)
    s4, s_part = z4, 2
    for o in range(HYENA_ORDER):
        b1 = _fft_stage1(s4, s_part, f1, r, n_ch)
        c2 = _fft_stage2(b1, h, o, mf, minv, r, n_ch)
        final = o == HYENA_ORDER - 1
        s = _fft_stage3(c2, g3, z4, o, s4, s_part, fft_bias[o], out_g, final, r, n_ch)
        s4, s_part = s.reshape(1, batch, seq, n_ch), 0
    return s4.reshape(batch * seq, n_ch)


def _extract_top(s, key, count):
    vals, keys = [], []
    for _ in range(count):
        m = jnp.max(s, axis=0, keepdims=True)
        kmin = jnp.min(jnp.where(s == m, key, jnp.inf), axis=0, keepdims=True)
        s = jnp.where(key == kmin, -jnp.inf, s)
        vals.append(m)
        keys.append(kmin)
    return vals, keys


def _peer_topk_kernel(q_ref, keys_ref, eid_ref, gate_ref):
    t = q_ref.shape[0]
    nk = PEER_NKEYS
    q = q_ref[...]
    row_key = lax.broadcasted_iota(I32, (nk, t), 0).astype(F32)
    tops = []
    for c in range(2):
        s = lax.dot_general(keys_ref[c], q[:, c * nk:(c + 1) * nk], (((1,), (1,)), ((), ())),
                            preferred_element_type=F32)
        tops.append(_extract_top(s, row_key, PEER_TOPK))
    (v1, i1), (v2, i2) = tops
    rows16 = lax.broadcasted_iota(I32, (PEER_TOPK, t), 0)
    v2a = jnp.zeros((PEER_TOPK, t), F32)
    i2a = jnp.zeros((PEER_TOPK, t), F32)
    for j in range(PEER_TOPK):
        v2a = jnp.where(rows16 == j, v2[j], v2a)
        i2a = jnp.where(rows16 == j, i2[j], i2a)
    pos16 = rows16.astype(F32)
    n_exp = float(nk * nk)
    cand = jnp.concatenate([v1[i] + v2a for i in range(PEER_TOPK)], axis=0)
    ckey = jnp.concatenate([(pos16 + float(i * PEER_TOPK)) * n_exp + (i1[i] * float(nk) + i2a)
                            for i in range(PEER_TOPK)], axis=0)
    tv, tk_ = _extract_top(cand, ckey, PEER_TOPK)
    denom = jnp.zeros((1, t), F32)
    es = []
    for k in range(PEER_TOPK):
        e = jnp.exp(tv[k] - tv[0])
        es.append(e)
        denom = denom + e
    eid = jnp.zeros((PEER_TOPK, t), F32)
    gate = jnp.zeros((PEER_TOPK, t), F32)
    for k in range(PEER_TOPK):
        pos = jnp.floor(tk_[k] * (1.0 / n_exp))
        eid = jnp.where(rows16 == k, tk_[k] - pos * n_exp, eid)
        gate = jnp.where(rows16 == k, es[k] / denom, gate)
    eid_ref[...] = eid.astype(I32)
    gate_ref[...] = gate


def _peer_topk(q, keys, tm=256):
    t = q.shape[0]
    tm = min(tm, t)
    out_spec = pl.BlockSpec((PEER_TOPK, tm), lambda i, h: (h, i))
    return pl.pallas_call(
        _peer_topk_kernel,
        grid=(t // tm, PEER_HEADS),
        in_specs=[pl.BlockSpec((tm, 2 * PEER_NKEYS), lambda i, h: (i, h)),
                  pl.BlockSpec((None, 2, PEER_NKEYS, PEER_NKEYS), lambda i, h: (h, 0, 0, 0))],
        out_specs=[out_spec, out_spec],
        out_shape=[jax.ShapeDtypeStruct((PEER_HEADS * PEER_TOPK, t), I32),
                   jax.ShapeDtypeStruct((PEER_HEADS * PEER_TOPK, t), F32)],
        compiler_params=_cparams(("parallel", "parallel")),
        name="peer_topk",
    )(q, keys)


def _pack_table(tab):
    e, d = tab.shape
    assert d == 2 * SUBLANES * LANES
    bits = lax.bitcast_convert_type(tab.astype(BF16), jnp.uint16).astype(U32)
    packed = bits[:, :d // 2] | (bits[:, d // 2:] << 16)
    return packed.reshape(e * SUBLANES, LANES)


def _unpack(w):
    lo = lax.bitcast_convert_type(w << 16, F32)
    hi = lax.bitcast_convert_type(w & jnp.uint32(0xFFFF0000), F32)
    return lo, hi


_BITREV8 = (0, 4, 2, 6, 1, 5, 3, 7)


def _sublane_fold8(parts):
    sub = lax.broadcasted_iota(I32, (SUBLANES, LANES), 0)
    lvl = [parts[_BITREV8[r]] for r in range(8)]
    for shift, mask in ((4, sub < 4), (2, (sub % 4) < 2), (1, (sub % 2) < 1)):
        nxt = []
        for a, b in zip(lvl[0::2], lvl[1::2]):
            nxt.append(jnp.where(mask, a + pltpu.roll(a, SUBLANES - shift, axis=0),
                                 b + pltpu.roll(b, shift, axis=0)))
        lvl = nxt
    return lvl[0]


def _pair_ranges(half):
    if half == 0:
        return (0, PEER_WINDOW), (PEER_WINDOW, PEER_PAIRS)
    return (PEER_PAIRS - PEER_WINDOW, PEER_PAIRS), (0, PEER_PAIRS - PEER_WINDOW)


def _overflow(n_low, half):
    return n_low > PEER_WINDOW if half == 0 else n_low < PEER_PAIRS - PEER_WINDOW


def _table_rows(tab_ref, off):
    return _unpack(tab_ref[pl.ds(pl.multiple_of(off, SUBLANES), SUBLANES), :])


def _peer_u_kernel(off_ref, nlow_ref, tab_ref, x_ref, o_ref, extra_ref, *, tm, half):
    lane = lax.broadcasted_iota(I32, (SUBLANES, LANES), 1)
    sub = lax.broadcasted_iota(I32, (SUBLANES, LANES), 0)
    lane_grp = lax.shift_right_logical(lane, 3)
    diag = sub == (lane & (SUBLANES - 1))
    main, rest = _pair_ranges(half)

    def folded(t, p0, p1):
        xlo = x_ref[t, 0]
        xhi = x_ref[t, 1]
        out = []
        for g in range(p0 // SUBLANES, p1 // SUBLANES):
            parts = []
            for r in range(SUBLANES):
                lo, hi = _table_rows(tab_ref, off_ref[t, g * SUBLANES + r])
                parts.append(lo * xlo + hi * xhi)
            out.append(_sublane_fold8(parts))
        return out

    def lane_sums(folds, p0):
        mat = jnp.zeros((SUBLANES, LANES), F32)
        for i, f in enumerate(folds):
            mat = jnp.where(lane_grp == p0 // SUBLANES + i, jnp.sum(f, axis=-1, keepdims=True), mat)
        return jnp.sum(jnp.where(diag, mat, 0.0), axis=0, keepdims=True)

    def finish(t, folds):
        o_ref[pl.ds(t, 1), :] = lane_sums(folds, main[0]) + extra_ref[pl.ds(t, 1), :]

    def token(t, prev):
        cur = folded(t, *main)
        finish(jnp.maximum(t - 1, 0), prev)
        extra_ref[pl.ds(t, 1), :] = jnp.zeros((1, LANES), F32)

        @pl.when(_overflow(nlow_ref[0, t], half))
        def _():
            extra_ref[pl.ds(t, 1), :] = lane_sums(folded(t, *rest), rest[0])

        return tuple(cur)

    extra_ref[pl.ds(0, 1), :] = jnp.zeros((1, LANES), F32)
    zeros = tuple(jnp.zeros((SUBLANES, LANES), F32) for _ in range((main[1] - main[0]) // SUBLANES))
    last = lax.fori_loop(0, tm, token, zeros)
    finish(tm - 1, last)


def _peer_u(off, nlow3, tab, x4, half, tm=128):
    t = off.shape[0]
    rows = PEER_HALF * SUBLANES
    return pl.pallas_call(
        functools.partial(_peer_u_kernel, tm=tm, half=half),
        grid=(t // tm,),
        in_specs=[pl.BlockSpec((tm, LANES), lambda i: (i, 0), memory_space=pltpu.SMEM),
                  pl.BlockSpec((None, 1, tm), lambda i: (i, 0, 0), memory_space=pltpu.SMEM),
                  pl.BlockSpec((rows, LANES), lambda i: (half, 0), pipeline_mode=pl.Buffered(1)),
                  pl.BlockSpec((tm, 2, SUBLANES, LANES), lambda i: (i, 0, 0, 0))],
        out_specs=pl.BlockSpec((tm, LANES), lambda i: (i, 0)),
        out_shape=jax.ShapeDtypeStruct((t, LANES), F32),
        scratch_shapes=[pltpu.VMEM((tm, LANES), F32)],
        compiler_params=_cparams(("arbitrary",)),
        name="peer_expert_scores",
    )(off, nlow3, tab, x4)


def _peer_coef_kernel(a0_ref, a1_ref, eid_ref, gate_ref, c0_ref, c1_ref):
    low = eid_ref[...] < PEER_HALF
    a = jnp.where(low, a0_ref[...], a1_ref[...])
    coef = gate_ref[...] * (0.5 * a * (1.0 + lax.erf(a * (1.0 / math.sqrt(2.0)))))
    c0_ref[...] = jnp.where(low, coef, 0.0)
    c1_ref[...] = jnp.where(low, 0.0, coef)


def _peer_coef(a0, a1, eid, gate, tm=1024):
    t = eid.shape[0]
    tm = min(tm, t)
    spec = pl.BlockSpec((tm, LANES), lambda i: (i, 0))
    return pl.pallas_call(
        _peer_coef_kernel,
        grid=(t // tm,),
        in_specs=[spec, spec, spec, spec],
        out_specs=[spec, spec],
        out_shape=[jax.ShapeDtypeStruct((t, LANES), F32)] * 2,
        compiler_params=_cparams(("parallel",)),
        name="peer_coef",
    )(a0, a1, eid, gate)


def _peer_v_kernel(off_ref, nlow_ref, coef_ref, tab_ref, o_ref, *, tm, half):
    n_acc = 4
    main, rest = _pair_ranges(half)

    def weighted(t, p0, p1):
        acc_lo = [jnp.zeros((SUBLANES, LANES), F32) for _ in range(n_acc)]
        acc_hi = [jnp.zeros((SUBLANES, LANES), F32) for _ in range(n_acc)]
        for p in range(p0, p1):
            c = coef_ref[t, p]
            lo, hi = _table_rows(tab_ref, off_ref[t, p])
            acc_lo[p % n_acc] = acc_lo[p % n_acc] + c * lo
            acc_hi[p % n_acc] = acc_hi[p % n_acc] + c * hi
        return ((acc_lo[0] + acc_lo[1]) + (acc_lo[2] + acc_lo[3]),
                (acc_hi[0] + acc_hi[1]) + (acc_hi[2] + acc_hi[3]))

    def token(t, carry):
        lo, hi = weighted(t, *main)
        o_ref[t, 0] = lo
        o_ref[t, 1] = hi

        @pl.when(_overflow(nlow_ref[0, t], half))
        def _():
            lo2, hi2 = weighted(t, *rest)
            o_ref[t, 0] = lo + lo2
            o_ref[t, 1] = hi + hi2

        return carry

    lax.fori_loop(0, tm, token, 0)


def _peer_v(off, nlow3, coef, tab, half, tm=128):
    t = off.shape[0]
    rows = PEER_HALF * SUBLANES
    smem = pl.BlockSpec((tm, LANES), lambda i: (i, 0), memory_space=pltpu.SMEM)
    return pl.pallas_call(
        functools.partial(_peer_v_kernel, tm=tm, half=half),
        grid=(t // tm,),
        in_specs=[smem,
                  pl.BlockSpec((None, 1, tm), lambda i: (i, 0, 0), memory_space=pltpu.SMEM),
                  smem,
                  pl.BlockSpec((rows, LANES), lambda i: (half, 0), pipeline_mode=pl.Buffered(1))],
        out_specs=pl.BlockSpec((tm, 2, SUBLANES, LANES), lambda i: (i, 0, 0, 0)),
        out_shape=jax.ShapeDtypeStruct((t, 2, SUBLANES, LANES), F32),
        compiler_params=_cparams(("arbitrary",)),
        name="peer_expert_sum",
    )(off, nlow3, coef, tab)


def _add3_kernel(x_ref, p_ref, q_ref, o_ref):
    o_ref[...] = x_ref[...] + (p_ref[...] + q_ref[...])


def _add3(x, p, q, tm=512):
    t, d = x.shape
    tm = min(tm, t)
    spec = pl.BlockSpec((tm, d), lambda i: (i, 0))
    return pl.pallas_call(
        _add3_kernel,
        grid=(t // tm,),
        in_specs=[spec, spec, spec],
        out_specs=spec,
        out_shape=jax.ShapeDtypeStruct((t, d), F32),
        compiler_params=_cparams(("parallel",)),
        name="peer_residual",
    )(x, p, q)


def _peer(x1, norm2_g, wq_bf, keys_bf, u_packed, v_packed, tm=128):
    t, d = x1.shape
    tm = min(tm, t)
    xn_bf, xn_f32 = _rmsnorm(x1, norm2_g, (BF16, F32))
    q = _matmul(xn_bf, wq_bf, wq_bf.shape[1], 0, BF16)
    eid_t, gate_t = _peer_topk(q, keys_bf)
    eid, gate = eid_t.T, gate_t.T
    high = eid >= PEER_HALF
    order = jnp.argsort(high, axis=-1, stable=True)
    eid = jnp.take_along_axis(eid, order, axis=-1)
    gate = jnp.take_along_axis(gate, order, axis=-1)
    nlow3 = (PEER_PAIRS - jnp.sum(high, axis=-1, dtype=I32)).reshape(t // tm, 1, tm)
    off = (eid & (PEER_HALF - 1)) * SUBLANES
    x4 = xn_f32.reshape(t, 2, SUBLANES, LANES)
    a0 = _peer_u(off, nlow3, u_packed, x4, 0, tm)
    a1 = _peer_u(off, nlow3, u_packed, x4, 1, tm)
    c0, c1 = _peer_coef(a0, a1, eid, gate)
    o0 = _peer_v(off, nlow3, c0, v_packed, 0, tm)
    o1 = _peer_v(off, nlow3, c1, v_packed, 1, tm)
    return _add3(x1, o0.reshape(t, d), o1.reshape(t, d))


def kernel(x_prompt, x_sample, norm1_g, w_in, q_norm_g, k_norm_g, lambda_q1, lambda_k1, lambda_q2, lambda_k2, attn_sub_g, conv_w, conv_b, filt_w1, filt_b1, filt_w2, filt_b2, filt_w3, filt_b3, filt_w4, filt_freq, fft_bias, hyena_out_g, w_out, norm2_g, peer_wq, peer_keys, peer_u, peer_v):
    depth = w_in.shape[0]
    d_model = x_prompt.shape[-1]
    att_w = ATT_HEADS * ATT_VDIM
    shapes = [x_prompt.shape[:2], x_sample.shape[:2]]
    x = jnp.concatenate([x_prompt.reshape(-1, d_model), x_sample.reshape(-1, d_model)], axis=0)
    slopes = 2.0 ** (-8.0 * jnp.arange(1, ATT_HEADS + 1, dtype=F32) / ATT_HEADS)

    for l in range(depth):
        lambda_init = 0.8 - 0.6 * math.exp(-0.3 * l)
        lam = (jnp.exp(jnp.sum(lambda_q1[l].astype(F32) * lambda_k1[l].astype(F32)))
               - jnp.exp(jnp.sum(lambda_q2[l].astype(F32) * lambda_k2[l].astype(F32)))
               + lambda_init).reshape(1)
        w_in_bf = w_in[l].astype(BF16)
        q_gain = jnp.tile(q_norm_g[l].astype(F32), 2 * ATT_HEADS) * (ATT_QKDIM ** -0.5 * LOG2E)
        k_gain = jnp.tile(k_norm_g[l].astype(F32), 2 * ATT_HEADS)
        qk_gain = jnp.concatenate([q_gain, k_gain]).reshape(1, 2 * att_w)

        (h_bf,) = _rmsnorm(x, norm1_g[l], (BF16,))
        qk = _matmul(h_bf, w_in_bf, 2 * att_w, 0, BF16, mode="qknorm", extra=qk_gain)
        v = _matmul(h_bf, w_in_bf, att_w, 2 * att_w, BF16)
        zh = _matmul(h_bf, w_in_bf, w_in.shape[2] - 3 * att_w, 3 * att_w, F32)

        filt = (filt_w1[l], filt_b1[l], filt_w2[l], filt_b2[l], filt_w3[l], filt_b3[l], filt_w4[l], filt_freq[l])
        mixes, row = [], 0
        for (b, s) in shapes:
            n = b * s
            att = _attention(qk, v, row, slopes, lam, attn_sub_g[l], b, s, 1.0 - lambda_init)
            hy = _hyena(zh, row, b, s, conv_w[l], conv_b[l], filt, fft_bias[l], hyena_out_g[l])
            mixes.append(jnp.concatenate([att, hy], axis=-1))
            row += n
        mix = jnp.concatenate(mixes, axis=0)
        x = _matmul(mix, w_out[l].astype(BF16), d_model, 0, F32, mode="residual", extra=x)

        x = _peer(x, norm2_g[l], peer_wq[l].astype(BF16), peer_keys[l].astype(BF16),
                  _pack_table(peer_u[l]), _pack_table(peer_v[l]))

    n0 = shapes[0][0] * shapes[0][1]
    return (x[:n0].reshape(x_prompt.shape), x[n0:].reshape(x_sample.shape))
```

```python
import functools
import math

import jax
import jax.numpy as jnp
from jax import lax
from jax.experimental import pallas as pl
from jax.experimental.pallas import tpu as pltpu

F32 = jnp.float32
BF16 = jnp.bfloat16
I32 = jnp.int32
U32 = jnp.uint32

RMS_EPS = 1e-6
LOG2E = 1.4426950408889634
LANES = 128
SUBLANES = 8
VMEM_LIMIT_BYTES = 56 * 1024 * 1024

ATT_HEADS = 8
ATT_VDIM = 128
ATT_QKDIM = 64
HYENA_ORDER = 2
FILTER_BANDS = 16
DECAY_FAST = 0.3
DECAY_SLOW = 1.5
DECAY_TARGET = 1e-2
PEER_HEADS = 8
PEER_NKEYS = 128
PEER_TOPK = 16
PEER_HALF = PEER_NKEYS * PEER_NKEYS // 2
PEER_PAIRS = PEER_HEADS * PEER_TOPK
PEER_WINDOW = 80


def _cparams(sem, vmem=VMEM_LIMIT_BYTES):
    return pltpu.CompilerParams(dimension_semantics=sem, vmem_limit_bytes=vmem)


def _rmsnorm_kernel(x_ref, g_ref, *o_refs):
    x = x_ref[...]
    ms = jnp.mean(x * x, axis=-1, keepdims=True)
    y = x * lax.rsqrt(ms + RMS_EPS) * g_ref[...]
    for o_ref in o_refs:
        o_ref[...] = y.astype(o_ref.dtype)


def _rmsnorm(x, g, out_dtypes, tm=512):
    t, d = x.shape
    tm = min(tm, t)
    spec = pl.BlockSpec((tm, d), lambda i: (i, 0))
    return pl.pallas_call(
        _rmsnorm_kernel,
        grid=(t // tm,),
        in_specs=[spec, pl.BlockSpec((1, d), lambda i: (0, 0))],
        out_specs=[spec for _ in out_dtypes],
        out_shape=[jax.ShapeDtypeStruct((t, d), dt) for dt in out_dtypes],
        compiler_params=_cparams(("parallel",)),
        name="rmsnorm",
    )(x, g.reshape(1, d).astype(F32))


def _group_rms_scale(x, gain):
    lane = lax.broadcasted_iota(I32, x.shape, 1)
    lo = lane < ATT_QKDIM
    x2 = x * x
    s_lo = jnp.sum(jnp.where(lo, x2, 0.0), axis=-1, keepdims=True)
    s_hi = jnp.sum(jnp.where(lo, 0.0, x2), axis=-1, keepdims=True)
    ms = jnp.where(lo, s_lo, s_hi) * (1.0 / ATT_QKDIM)
    return x * lax.rsqrt(ms + RMS_EPS) * gain


def _mm_kernel(a_ref, b_ref, *rest, mode):
    acc = jnp.dot(a_ref[...], b_ref[...], preferred_element_type=F32)
    if mode == "plain":
        (o_ref,) = rest
        o_ref[...] = acc.astype(o_ref.dtype)
    elif mode == "residual":
        r_ref, o_ref = rest
        o_ref[...] = (acc + r_ref[...]).astype(o_ref.dtype)
    elif mode == "qknorm":
        g_ref, o_ref = rest
        for c in range(acc.shape[1] // LANES):
            sl = slice(c * LANES, (c + 1) * LANES)
            o_ref[:, sl] = _group_rms_scale(acc[:, sl], g_ref[:, sl]).astype(o_ref.dtype)
    else:
        raise ValueError(mode)


def _matmul(a, b, n_cols, col_off, out_dtype, mode="plain", extra=None, tm=1024, tn=1024):
    m, k = a.shape
    tm = min(tm, m)
    tn = min(tn, n_cols)
    assert col_off % tn == 0 and n_cols % tn == 0 and m % tm == 0
    off = col_off // tn
    in_specs = [pl.BlockSpec((tm, k), lambda i, j: (i, 0)),
                pl.BlockSpec((k, tn), lambda i, j: (0, j + off))]
    args = [a, b]
    if mode == "residual":
        in_specs.append(pl.BlockSpec((tm, tn), lambda i, j: (i, j)))
        args.append(extra)
    elif mode == "qknorm":
        in_specs.append(pl.BlockSpec((1, tn), lambda i, j: (0, j)))
        args.append(extra)
    return pl.pallas_call(
        functools.partial(_mm_kernel, mode=mode),
        grid=(m // tm, n_cols // tn),
        in_specs=in_specs,
        out_specs=pl.BlockSpec((tm, tn), lambda i, j: (i, j)),
        out_shape=jax.ShapeDtypeStruct((m, n_cols), out_dtype),
        compiler_params=_cparams(("parallel", "arbitrary")),
        name="matmul_" + mode,
    )(*args)


def _alibi_columns(slopes, tq, tk):
    return [_alibi_side(slopes, tq, True), _alibi_side(slopes, tk, False)]


def _alibi_side(slopes, n, query_side):
    pos = jnp.arange(n, dtype=F32)
    val = (slopes.astype(F32) * LOG2E)[:, None] * pos[None, :]

    def pieces(x):
        p1 = x.astype(BF16)
        r1 = x - p1.astype(F32)
        p2 = r1.astype(BF16)
        p3 = (r1 - p2.astype(F32)).astype(BF16)
        return [p1, p2, p3]

    ones = [jnp.ones_like(val, BF16)] * 3
    six = jnp.stack(pieces(-val) + ones if query_side else ones + pieces(val), axis=-1)
    pad = jnp.zeros(val.shape + (ATT_QKDIM - 6,), BF16)
    return jnp.concatenate([six, pad, six, pad], axis=-1)


def _attn_kernel(slope_ref, lam_ref, q_ref, k_ref, v_ref, aq_ref, ak_ref, g_ref, o_ref,
                 m0_ref, l0_ref, a0_ref, m1_ref, l1_ref, a1_ref, *, tq, tk, nk, out_scale):
    h = pl.program_id(1)
    i = pl.program_id(2)
    j = pl.program_id(3)

    @pl.when(j == 0)
    def _():
        for m_ref, l_ref, a_ref in ((m0_ref, l0_ref, a0_ref), (m1_ref, l1_ref, a1_ref)):
            m_ref[...] = jnp.full(m_ref.shape, -jnp.inf, F32)
            l_ref[...] = jnp.zeros(l_ref.shape, F32)
            a_ref[...] = jnp.zeros(a_ref.shape, F32)

    q_first = lax.broadcasted_iota(I32, (tq, LANES), 1) < ATT_QKDIM
    k_first = lax.broadcasted_iota(I32, (tk, LANES), 1) < ATT_QKDIM
    nt = (((1,), (1,)), ((), ()))

    def update(scores, shift):
        v = v_ref[...]
        for s, m_ref, l_ref, a_ref in ((scores[0], m0_ref, l0_ref, a0_ref),
                                       (scores[1], m1_ref, l1_ref, a1_ref)):
            m_old = m_ref[...]
            m_new = jnp.maximum(m_old, jnp.max(s, axis=-1, keepdims=True) + shift)
            alpha = jnp.exp2(m_old - m_new)
            p = jnp.exp2(s - (m_new - shift))
            l_ref[...] = alpha * l_ref[...] + jnp.sum(p, axis=-1, keepdims=True)
            a_ref[...] = alpha * a_ref[...] + jnp.dot(p.astype(BF16), v, preferred_element_type=F32)
            m_ref[...] = m_new

    keys_before = i * tq >= (j + 1) * tk
    keys_after = (i + 1) * tq <= j * tk
    off_diagonal = jnp.logical_or(keys_before, keys_after)

    @pl.when(off_diagonal)
    def _():
        q = q_ref[...]
        k = k_ref[...]
        aq = aq_ref[...]
        ak = ak_ref[...] * jnp.where(keys_after, -1.0, 1.0).astype(BF16)
        s0 = lax.dot_general(jnp.where(q_first, q, aq), jnp.where(k_first, k, ak), nt,
                             preferred_element_type=F32)
        s1 = lax.dot_general(jnp.where(q_first, aq, q), jnp.where(k_first, ak, k), nt,
                             preferred_element_type=F32)
        gap = jnp.abs(i * tq - j * tk).astype(F32)
        update((s0, s1), -(slope_ref[h] * LOG2E) * gap)

    @pl.when(jnp.logical_not(off_diagonal))
    def _():
        q = q_ref[...]
        k = k_ref[...]
        zero = jnp.zeros_like(q)
        rows = lax.broadcasted_iota(I32, (tq, tk), 0) + i * tq
        cols = lax.broadcasted_iota(I32, (tq, tk), 1) + j * tk
        bias = jnp.abs(rows - cols).astype(F32) * (-(slope_ref[h] * LOG2E))
        s0 = lax.dot_general(jnp.where(q_first, q, zero), k, nt, preferred_element_type=F32) + bias
        s1 = lax.dot_general(jnp.where(q_first, zero, q), k, nt, preferred_element_type=F32) + bias
        update((s0, s1), 0.0)

    @pl.when(j == nk - 1)
    def _():
        o = a0_ref[...] / l0_ref[...] - lam_ref[0] * (a1_ref[...] / l1_ref[...])
        ms = jnp.mean(o * o, axis=-1, keepdims=True)
        o_ref[...] = (o * lax.rsqrt(ms + RMS_EPS) * g_ref[...] * out_scale).astype(o_ref.dtype)


def _attention(qk, v, row_off, slopes, lam, sub_g, batch, seq, out_scale, tq=512, tk=1024):
    tq = min(tq, seq)
    tk = min(tk, seq // 4)
    nq, nk = seq // tq, seq // tk
    assert row_off % tq == 0 and row_off % tk == 0
    oq, ok = row_off // tq, row_off // tk
    aq, ak = _alibi_columns(slopes, tq, tk)
    kern = functools.partial(_attn_kernel, tq=tq, tk=tk, nk=nk, out_scale=out_scale)
    smem = pl.BlockSpec(memory_space=pltpu.SMEM)
    return pl.pallas_call(
        kern,
        grid=(batch, ATT_HEADS, nq, nk),
        in_specs=[smem, smem,
                  pl.BlockSpec((tq, LANES), lambda b, h, i, j: (oq + b * nq + i, h)),
                  pl.BlockSpec((tk, LANES), lambda b, h, i, j: (ok + b * nk + j, ATT_HEADS + h)),
                  pl.BlockSpec((tk, LANES), lambda b, h, i, j: (ok + b * nk + j, h)),
                  pl.BlockSpec((None, tq, LANES), lambda b, h, i, j: (h, 0, 0)),
                  pl.BlockSpec((None, tk, LANES), lambda b, h, i, j: (h, 0, 0)),
                  pl.BlockSpec((1, LANES), lambda b, h, i, j: (0, 0))],
        out_specs=pl.BlockSpec((tq, LANES), lambda b, h, i, j: (b * nq + i, h)),
        out_shape=jax.ShapeDtypeStruct((batch * seq, ATT_HEADS * ATT_VDIM), BF16),
        scratch_shapes=[pltpu.VMEM((tq, 1), F32), pltpu.VMEM((tq, 1), F32), pltpu.VMEM((tq, LANES), F32),
                        pltpu.VMEM((tq, 1), F32), pltpu.VMEM((tq, 1), F32), pltpu.VMEM((tq, LANES), F32)],
        compiler_params=_cparams(("parallel", "parallel", "parallel", "arbitrary")),
        name="diff_attention",
    )(slopes, lam, qk, qk, v, aq, ak, sub_g.reshape(1, LANES).astype(F32))


def _shortconv_kernel(z_ref, w_ref, b_ref, o_ref):
    z = z_ref[...]
    n = z.shape[0]
    row = lax.broadcasted_iota(I32, z.shape, 0)
    prev = jnp.where(row == 0, 0.0, pltpu.roll(z, 1, axis=0))
    nxt = jnp.where(row == n - 1, 0.0, pltpu.roll(z, n - 1, axis=0))
    w = w_ref[...]
    o_ref[...] = prev * w[0:1] + z * w[1:2] + nxt * w[2:3] + b_ref[...]


def _shortconv(zh, row_off, conv_w, conv_b, batch, seq, cb=256):
    c3 = zh.shape[1]
    c = c3 // 3
    ncb = c // cb
    assert row_off % seq == 0
    ob = row_off // seq
    out = pl.pallas_call(
        _shortconv_kernel,
        grid=(batch, 3, ncb),
        in_specs=[pl.BlockSpec((seq, cb), lambda b, p, j: (ob + b, p * ncb + j)),
                  pl.BlockSpec((3, cb), lambda b, p, j: (0, p * ncb + j)),
                  pl.BlockSpec((1, cb), lambda b, p, j: (0, p * ncb + j))],
        out_specs=pl.BlockSpec((None, None, seq, cb), lambda b, p, j: (p, b, 0, j)),
        out_shape=jax.ShapeDtypeStruct((3, batch, seq, c), F32),
        compiler_params=_cparams(("parallel", "parallel", "parallel")),
        name="hyena_shortconv",
    )(zh, conv_w.astype(F32), conv_b.reshape(1, c3).astype(F32))
    return out


def _filter_kernel(z_ref, w1_ref, b1_ref, w2_ref, b2_ref, w3_ref, b3_ref, fr_ref, w4_ref,
                   t_ref, d_ref, o_ref, *, tl):
    g = pl.program_id(0)
    i = pl.program_id(1)
    hi = lax.Precision.HIGHEST
    fr = fr_ref[...]
    h = jnp.sin(fr * (jnp.dot(z_ref[...], w1_ref[...], precision=hi, preferred_element_type=F32) + b1_ref[...]))
    h = jnp.sin(fr * (jnp.dot(h, w2_ref[...], precision=hi, preferred_element_type=F32) + b2_ref[...]))
    h = jnp.sin(fr * (jnp.dot(h, w3_ref[...], precision=hi, preferred_element_type=F32) + b3_ref[...]))
    f = jnp.dot(h, w4_ref[...], precision=hi, preferred_element_type=F32)
    f = f * jnp.exp(-t_ref[...] * d_ref[...])
    row = lax.broadcasted_iota(I32, f.shape, 0) + i * tl
    drop = jnp.logical_and(row == 0, g % 2 == 1)
    o_ref[...] = jnp.where(drop, 0.0, f)


def _hyena_filter_signals(seq, w1, b1, w2, b2, w3, b3, w4, freq, n_ch, tl=512):
    t = jnp.linspace(0.0, 1.0, seq, dtype=F32)[:, None]
    w = 2.0 * math.pi * jnp.arange(seq, dtype=F32)[:, None] / seq
    f = jnp.linspace(1e-4, FILTER_BANDS - 1, FILTER_BANDS, dtype=F32)[None, :]
    z = jnp.concatenate([t, jnp.cos(f * w), -jnp.sin(f * w)], axis=-1)
    deltas = jnp.abs(jnp.linspace(math.log(DECAY_FAST) / DECAY_TARGET,
                                  math.log(DECAY_SLOW) / DECAY_TARGET, n_ch, dtype=F32))[None, :]
    hid = w1.shape[1]
    emb = LANES
    z = jnp.pad(z, ((0, 0), (0, emb - z.shape[1])))
    w1 = jnp.pad(w1.astype(F32), ((0, emb - w1.shape[0]), (0, 0)))
    tl = min(tl, seq)
    full = lambda shape: pl.BlockSpec(shape, lambda g, i: tuple(0 for _ in shape))
    return pl.pallas_call(
        functools.partial(_filter_kernel, tl=tl),
        grid=(2 * HYENA_ORDER, seq // tl),
        in_specs=[pl.BlockSpec((tl, emb), lambda g, i: (i, 0)),
                  full((emb, hid)), full((1, hid)), full((hid, hid)), full((1, hid)),
                  full((hid, hid)), full((1, hid)), full((1, hid)),
                  pl.BlockSpec((hid, n_ch), lambda g, i: (0, g)),
                  pl.BlockSpec((tl, 1), lambda g, i: (i, 0)),
                  full((1, n_ch))],
        out_specs=pl.BlockSpec((None, tl, n_ch), lambda g, i: (g, i, 0)),
        out_shape=jax.ShapeDtypeStruct((2 * HYENA_ORDER, seq, n_ch), F32),
        compiler_params=_cparams(("parallel", "parallel")),
        name="hyena_filter_mlp",
    )(z, w1.astype(F32), b1.reshape(1, hid).astype(F32), w2.astype(F32), b2.reshape(1, hid).astype(F32),
      w3.astype(F32), b3.reshape(1, hid).astype(F32), freq.reshape(1, hid).astype(F32), w4.astype(F32),
      t, deltas)


def _dft_tables(r):
    n = r * r
    k2 = jnp.arange(r, dtype=I32)
    n2 = jnp.arange(r // 2, dtype=I32)
    ang1 = (2.0 * math.pi / r) * ((k2[:, None] * n2[None, :]) % r).astype(F32)
    f1 = jnp.concatenate([jnp.cos(ang1), -jnp.sin(ang1)], axis=0)
    k1 = jnp.arange(r, dtype=I32)
    n1 = jnp.arange(r, dtype=I32)
    kk = r * k1[None, :, None] + k2[:, None, None]
    ang2 = (2.0 * math.pi / n) * ((kk * n1[None, None, :]) % n).astype(F32)
    mr, mi_ = jnp.cos(ang2), -jnp.sin(ang2)
    mf = jnp.concatenate([jnp.concatenate([mr, -mi_], axis=2),
                          jnp.concatenate([mi_, mr], axis=2)], axis=1)
    minv = jnp.swapaxes(mf, 1, 2)
    g3 = jnp.concatenate([jnp.cos(ang1.T), -jnp.sin(ang1.T)], axis=1) * (1.0 / n)
    return f1.astype(BF16), mf.astype(BF16), minv.astype(BF16), g3.astype(BF16)


def _fft1_kernel(f_ref, x_ref, o_ref):
    o_ref[...] = jnp.dot(f_ref[...], x_ref[...].astype(BF16), preferred_element_type=F32).astype(o_ref.dtype)


def _fft_stage1(x4, part, f1, r, n_ch, w_lanes=4096):
    nb = x4.shape[1]
    xv = x4.reshape(x4.shape[0], nb, r // 2, r * n_ch)
    nw = (r * n_ch) // w_lanes
    return pl.pallas_call(
        _fft1_kernel,
        grid=(nb, nw),
        in_specs=[pl.BlockSpec((2 * r, r // 2), lambda b, j: (0, 0)),
                  pl.BlockSpec((None, None, r // 2, w_lanes), lambda b, j: (part, b, 0, j))],
        out_specs=pl.BlockSpec((None, 2 * r, w_lanes), lambda b, j: (b, 0, j)),
        out_shape=jax.ShapeDtypeStruct((nb, 2 * r, r * n_ch), BF16),
        compiler_params=_cparams(("parallel", "parallel")),
        name="hyena_dft_stage1",
    )(f1, xv)


def _filter_spec_kernel(mf_ref, bf_ref, bb_ref, o_ref, *, r):
    m = mf_ref[...]
    c = bf_ref.shape[-1]
    xf = jnp.dot(m, bf_ref[...].reshape(2 * r, c), preferred_element_type=F32)
    xb = jnp.dot(m, bb_ref[...].reshape(2 * r, c), preferred_element_type=F32)
    o_ref[0] = (xf[:r] + xb[:r]).astype(o_ref.dtype)
    o_ref[1] = (xf[r:] - xb[r:]).astype(o_ref.dtype)


def _filter_spectrum(b1, mf, r, n_ch):
    bv = b1.reshape(2 * HYENA_ORDER, 2, r, r, n_ch)
    blk = lambda sel: pl.BlockSpec((None, 2, None, r, n_ch), lambda o, k: (2 * o + sel, 0, k, 0, 0))
    return pl.pallas_call(
        functools.partial(_filter_spec_kernel, r=r),
        grid=(HYENA_ORDER, r),
        in_specs=[pl.BlockSpec((None, 2 * r, 2 * r), lambda o, k: (k, 0, 0)), blk(0), blk(1)],
        out_specs=pl.BlockSpec((None, 2, None, r, n_ch), lambda o, k: (o, 0, k, 0, 0)),
        out_shape=jax.ShapeDtypeStruct((HYENA_ORDER, 2, r, r, n_ch), BF16),
        compiler_params=_cparams(("parallel", "parallel")),
        name="hyena_filter_spectrum",
    )(mf, bv, bv)


def _fft2_kernel(mf_ref, mi_ref, b_ref, h_ref, o_ref, *, r):
    c = b_ref.shape[-1]
    x = jnp.dot(mf_ref[...], b_ref[...].reshape(2 * r, c), preferred_element_type=F32)
    xr, xi = x[:r], x[r:]
    hr = h_ref[0].astype(F32)
    hi = h_ref[1].astype(F32)
    y = jnp.concatenate([xr * hr - xi * hi, xr * hi + xi * hr], axis=0).astype(BF16)
    o_ref[...] = jnp.dot(mi_ref[...], y, preferred_element_type=F32).reshape(2, r, c).astype(o_ref.dtype)


def _fft_stage2(b1, h, order, mf, minv, r, n_ch):
    nb = b1.shape[0]
    bv = b1.reshape(nb, 2, r, r, n_ch)
    out = pl.pallas_call(
        functools.partial(_fft2_kernel, r=r),
        grid=(nb, r),
        in_specs=[pl.BlockSpec((None, 2 * r, 2 * r), lambda b, k: (k, 0, 0)),
                  pl.BlockSpec((None, 2 * r, 2 * r), lambda b, k: (k, 0, 0)),
                  pl.BlockSpec((None, 2, None, r, n_ch), lambda b, k: (b, 0, k, 0, 0)),
                  pl.BlockSpec((None, 2, None, r, n_ch), lambda b, k: (order, 0, k, 0, 0))],
        out_specs=pl.BlockSpec((None, 2, None, r, n_ch), lambda b, k: (b, 0, k, 0, 0)),
        out_shape=jax.ShapeDtypeStruct((nb, 2, r, r, n_ch), BF16),
        compiler_params=_cparams(("parallel", "parallel")),
        name="hyena_dft_stage2",
    )(mf, minv, bv, h)
    return out.reshape(nb, 2 * r, r * n_ch)


def _fft3_kernel(g_ref, c_ref, gate_ref, s_ref, bias_ref, ng_ref, o_ref, *, final, n_ch):
    y = jnp.dot(g_ref[...], c_ref[...], preferred_element_type=F32)
    s_new = gate_ref[...] * (y + s_ref[...] * bias_ref[...])
    if not final:
        o_ref[...] = s_new
        return
    for c in range(s_new.shape[1] // n_ch):
        sl = slice(c * n_ch, (c + 1) * n_ch)
        x = s_new[:, sl]
        ms = jnp.mean(x * x, axis=-1, keepdims=True)
        o_ref[:, sl] = (x * lax.rsqrt(ms + RMS_EPS) * ng_ref[...]).astype(o_ref.dtype)


def _fft_stage3(c2, g3, z4, gate_part, s4, s_part, bias, norm_g, final, r, n_ch, w_lanes=4096):
    nb = c2.shape[0]
    zv = z4.reshape(z4.shape[0], nb, r // 2, r * n_ch)
    sv = s4.reshape(s4.shape[0], nb, r // 2, r * n_ch)
    nw = (r * n_ch) // w_lanes
    bias_row = jnp.tile(bias.reshape(1, n_ch).astype(F32), (1, w_lanes // n_ch))
    out_dtype = BF16 if final else F32
    dspec = lambda part: pl.BlockSpec((None, None, r // 2, w_lanes), lambda b, j: (part, b, 0, j))
    out = pl.pallas_call(
        functools.partial(_fft3_kernel, final=final, n_ch=n_ch),
        grid=(nb, nw),
        in_specs=[pl.BlockSpec((r // 2, 2 * r), lambda b, j: (0, 0)),
                  pl.BlockSpec((None, 2 * r, w_lanes), lambda b, j: (b, 0, j)),
                  dspec(gate_part), dspec(s_part),
                  pl.BlockSpec((1, w_lanes), lambda b, j: (0, 0)),
                  pl.BlockSpec((1, n_ch), lambda b, j: (0, 0))],
        out_specs=pl.BlockSpec((None, r // 2, w_lanes), lambda b, j: (b, 0, j)),
        out_shape=jax.ShapeDtypeStruct((nb, r // 2, r * n_ch), out_dtype),
        compiler_params=_cparams(("parallel", "parallel")),
        name="hyena_dft_stage3",
    )(g3, c2, zv, sv, bias_row, norm_g.reshape(1, n_ch).astype(F32))
    return out


def _hyena(zh, row_off, batch, seq, conv_w, conv_b, filt, fft_bias, out_g):
    n_ch = zh.shape[1] // 3
    r = int(round(math.sqrt(2 * seq)))
    assert r * r == 2 * seq and r % 16 == 0
    f1, mf, minv, g3 = _dft_tables(r)
    sig = _hyena_filter_signals(seq, *filt, n_ch=n_ch)
    hb1 = _fft_stage1(sig[None], 0, f1, r, n_ch)
    h = _filter_spectrum(hb1, mf, r, n_ch)
    z4 = _shortconv(zh, row_off, conv_w, conv_b, batch, seq)
    s4, s_part = z4, 2
    for o in range(HYENA_ORDER):
        b1 = _fft_stage1(s4, s_part, f1, r, n_ch)
        c2 = _fft_stage2(b1, h, o, mf, minv, r, n_ch)
        final = o == HYENA_ORDER - 1
        s = _fft_stage3(c2, g3, z4, o, s4, s_part, fft_bias[o], out_g, final, r, n_ch)
        s4, s_part = s.reshape(1, batch, seq, n_ch), 0
    return s4.reshape(batch * seq, n_ch)


def _extract_top(s, key, count):
    vals, keys = [], []
    for _ in range(count):
        m = jnp.max(s, axis=0, keepdims=True)
        kmin = jnp.min(jnp.where(s == m, key, jnp.inf), axis=0, keepdims=True)
        s = jnp.where(key == kmin, -jnp.inf, s)
        vals.append(m)
        keys.append(kmin)
    return vals, keys


def _peer_topk_kernel(q_ref, keys_ref, eid_ref, gate_ref):
    t = q_ref.shape[0]
    nk = PEER_NKEYS
    q = q_ref[...]
    row_key = lax.broadcasted_iota(I32, (nk, t), 0).astype(F32)
    tops = []
    for c in range(2):
        s = lax.dot_general(keys_ref[c], q[:, c * nk:(c + 1) * nk], (((1,), (1,)), ((), ())),
                            preferred_element_type=F32)
        tops.append(_extract_top(s, row_key, PEER_TOPK))
    (v1, i1), (v2, i2) = tops
    rows16 = lax.broadcasted_iota(I32, (PEER_TOPK, t), 0)
    v2a = jnp.zeros((PEER_TOPK, t), F32)
    i2a = jnp.zeros((PEER_TOPK, t), F32)
    for j in range(PEER_TOPK):
        v2a = jnp.where(rows16 == j, v2[j], v2a)
        i2a = jnp.where(rows16 == j, i2[j], i2a)
    n_exp = float(nk * nk)
    half = PEER_TOPK // 2
    rows8 = lax.broadcasted_iota(I32, (half, t), 0)
    pos8 = rows8.astype(F32)
    v2h, i2h = v2a[:half], i2a[:half]
    cand = [v1[0] + v2a]
    ckey = [rows16.astype(F32) * n_exp + (i1[0] * float(nk) + i2a)]
    for i in range(1, half):
        cand.append(jnp.where(rows8 < PEER_TOPK // (i + 1), v1[i] + v2h, -jnp.inf))
        ckey.append((pos8 + float(i * PEER_TOPK)) * n_exp + (i1[i] * float(nk) + i2h))
    v1t = jnp.zeros((half, t), F32)
    i1t = jnp.zeros((half, t), F32)
    for r in range(half):
        v1t = jnp.where(rows8 == r, v1[half + r], v1t)
        i1t = jnp.where(rows8 == r, i1[half + r], i1t)
    cand.append(v1t + v2[0])
    ckey.append((pos8 + float(half)) * (PEER_TOPK * n_exp) + (i1t * float(nk) + i2[0]))
    tv, tk_ = _extract_top(jnp.concatenate(cand, axis=0), jnp.concatenate(ckey, axis=0), PEER_TOPK)
    denom = jnp.zeros((1, t), F32)
    es = []
    for k in range(PEER_TOPK):
        e = jnp.exp(tv[k] - tv[0])
        es.append(e)
        denom = denom + e
    eid = jnp.zeros((PEER_TOPK, t), F32)
    gate = jnp.zeros((PEER_TOPK, t), F32)
    for k in range(PEER_TOPK):
        pos = jnp.floor(tk_[k] * (1.0 / n_exp))
        eid = jnp.where(rows16 == k, tk_[k] - pos * n_exp, eid)
        gate = jnp.where(rows16 == k, es[k] / denom, gate)
    eid_ref[...] = eid.astype(I32)
    gate_ref[...] = gate


def _peer_topk(q, keys, tm=256):
    t = q.shape[0]
    tm = min(tm, t)
    out_spec = pl.BlockSpec((PEER_TOPK, tm), lambda i, h: (h, i))
    return pl.pallas_call(
        _peer_topk_kernel,
        grid=(t // tm, PEER_HEADS),
        in_specs=[pl.BlockSpec((tm, 2 * PEER_NKEYS), lambda i, h: (i, h)),
                  pl.BlockSpec((None, 2, PEER_NKEYS, PEER_NKEYS), lambda i, h: (h, 0, 0, 0))],
        out_specs=[out_spec, out_spec],
        out_shape=[jax.ShapeDtypeStruct((PEER_HEADS * PEER_TOPK, t), I32),
                   jax.ShapeDtypeStruct((PEER_HEADS * PEER_TOPK, t), F32)],
        compiler_params=_cparams(("parallel", "parallel")),
        name="peer_topk",
    )(q, keys)


def _pack_table(tab):
    e, d = tab.shape
    assert d == 2 * SUBLANES * LANES
    bits = lax.bitcast_convert_type(tab.astype(BF16), jnp.uint16).astype(U32)
    packed = bits[:, :d // 2] | (bits[:, d // 2:] << 16)
    return packed.reshape(e * SUBLANES, LANES)


def _unpack(w):
    lo = lax.bitcast_convert_type(w << 16, F32)
    hi = lax.bitcast_convert_type(w & jnp.uint32(0xFFFF0000), F32)
    return lo, hi


_BITREV8 = (0, 4, 2, 6, 1, 5, 3, 7)


def _sublane_fold8(parts):
    sub = lax.broadcasted_iota(I32, (2 * SUBLANES, LANES), 0) // 2

    def rolled(a, shift):
        return pltpu.bitcast(pltpu.roll(pltpu.bitcast(a, U32), shift, axis=0), BF16)

    lvl = [parts[_BITREV8[r]] for r in range(8)]
    for shift, mask in ((4, sub < 4), (2, (sub % 4) < 2), (1, (sub % 2) < 1)):
        nxt = []
        for a, b in zip(lvl[0::2], lvl[1::2]):
            nxt.append(jnp.where(mask, a + rolled(a, SUBLANES - shift), b + rolled(b, shift)))
        lvl = nxt
    return lvl[0]


def _pair_ranges(half):
    if half == 0:
        return (0, PEER_WINDOW), (PEER_WINDOW, PEER_PAIRS)
    return (PEER_PAIRS - PEER_WINDOW, PEER_PAIRS), (0, PEER_PAIRS - PEER_WINDOW)


def _overflow(n_low, half):
    return n_low > PEER_WINDOW if half == 0 else n_low < PEER_PAIRS - PEER_WINDOW


def _table_rows(tab_ref, off):
    return _unpack(tab_ref[pl.ds(pl.multiple_of(off, SUBLANES), SUBLANES), :])


def _peer_u_kernel(off_ref, nlow_ref, tab_ref, x_ref, o_ref, extra_ref, *, tm, half):
    lane = lax.broadcasted_iota(I32, (SUBLANES, LANES), 1)
    sub = lax.broadcasted_iota(I32, (SUBLANES, LANES), 0)
    lane_grp = lax.shift_right_logical(lane, 3)
    diag = sub == (lane & (SUBLANES - 1))
    main, rest = _pair_ranges(half)

    def folded(t, p0, p1):
        xb = pltpu.bitcast(x_ref[t], BF16)
        out = []
        for g in range(p0 // SUBLANES, p1 // SUBLANES):
            parts = []
            for r in range(SUBLANES):
                off = pl.multiple_of(off_ref[t, g * SUBLANES + r], SUBLANES)
                parts.append(pltpu.bitcast(tab_ref[pl.ds(off, SUBLANES), :], BF16) * xb)
            out.append(pltpu.bitcast(_sublane_fold8(parts), U32))
        return out

    def lane_sums(folds, p0):
        mat = jnp.zeros((SUBLANES, LANES), F32)
        for i, f in enumerate(folds):
            lo, hi = _unpack(f)
            mat = jnp.where(lane_grp == p0 // SUBLANES + i, jnp.sum(lo + hi, axis=-1, keepdims=True), mat)
        return jnp.sum(jnp.where(diag, mat, 0.0), axis=0, keepdims=True)

    def finish(t, folds):
        o_ref[pl.ds(t, 1), :] = lane_sums(folds, main[0]) + extra_ref[pl.ds(t, 1), :]

    def token(t, prev):
        cur = folded(t, *main)
        finish(jnp.maximum(t - 1, 0), prev)
        extra_ref[pl.ds(t, 1), :] = jnp.zeros((1, LANES), F32)

        @pl.when(_overflow(nlow_ref[0, t], half))
        def _():
            extra_ref[pl.ds(t, 1), :] = lane_sums(folded(t, *rest), rest[0])

        return tuple(cur)

    extra_ref[pl.ds(0, 1), :] = jnp.zeros((1, LANES), F32)
    zeros = tuple(jnp.zeros((SUBLANES, LANES), U32) for _ in range((main[1] - main[0]) // SUBLANES))
    last = lax.fori_loop(0, tm, token, zeros)
    finish(tm - 1, last)


def _peer_u(off, nlow3, tab, x4, half, tm=128):
    t = off.shape[0]
    rows = PEER_HALF * SUBLANES
    return pl.pallas_call(
        functools.partial(_peer_u_kernel, tm=tm, half=half),
        grid=(t // tm,),
        in_specs=[pl.BlockSpec((tm, LANES), lambda i: (i, 0), memory_space=pltpu.SMEM),
                  pl.BlockSpec((None, 1, tm), lambda i: (i, 0, 0), memory_space=pltpu.SMEM),
                  pl.BlockSpec((rows, LANES), lambda i: (half, 0), pipeline_mode=pl.Buffered(1)),
                  pl.BlockSpec((tm, SUBLANES, LANES), lambda i: (i, 0, 0))],
        out_specs=pl.BlockSpec((tm, LANES), lambda i: (i, 0)),
        out_shape=jax.ShapeDtypeStruct((t, LANES), F32),
        scratch_shapes=[pltpu.VMEM((tm, LANES), F32)],
        compiler_params=_cparams(("arbitrary",)),
        name="peer_expert_scores",
    )(off, nlow3, tab, x4)


def _peer_coef_kernel(a0_ref, a1_ref, eid_ref, gate_ref, c0_ref, c1_ref):
    low = eid_ref[...] < PEER_HALF
    a = jnp.where(low, a0_ref[...], a1_ref[...])
    coef = gate_ref[...] * (0.5 * a * (1.0 + lax.erf(a * (1.0 / math.sqrt(2.0)))))
    c0_ref[...] = jnp.where(low, coef, 0.0)
    c1_ref[...] = jnp.where(low, 0.0, coef)


def _peer_coef(a0, a1, eid, gate, tm=1024):
    t = eid.shape[0]
    tm = min(tm, t)
    spec = pl.BlockSpec((tm, LANES), lambda i: (i, 0))
    return pl.pallas_call(
        _peer_coef_kernel,
        grid=(t // tm,),
        in_specs=[spec, spec, spec, spec],
        out_specs=[spec, spec],
        out_shape=[jax.ShapeDtypeStruct((t, LANES), F32)] * 2,
        compiler_params=_cparams(("parallel",)),
        name="peer_coef",
    )(a0, a1, eid, gate)


def _peer_v_kernel(off_ref, nlow_ref, coef_ref, tab_ref, o_ref, *, tm, half):
    n_acc = 4
    main, rest = _pair_ranges(half)

    def weighted(t, p0, p1):
        acc_lo = [jnp.zeros((SUBLANES, LANES), F32) for _ in range(n_acc)]
        acc_hi = [jnp.zeros((SUBLANES, LANES), F32) for _ in range(n_acc)]
        for p in range(p0, p1):
            c = coef_ref[t, p]
            lo, hi = _table_rows(tab_ref, off_ref[t, p])
            acc_lo[p % n_acc] = acc_lo[p % n_acc] + c * lo
            acc_hi[p % n_acc] = acc_hi[p % n_acc] + c * hi
        return ((acc_lo[0] + acc_lo[1]) + (acc_lo[2] + acc_lo[3]),
                (acc_hi[0] + acc_hi[1]) + (acc_hi[2] + acc_hi[3]))

    def token(t, carry):
        lo, hi = weighted(t, *main)
        o_ref[t, 0] = lo
        o_ref[t, 1] = hi

        @pl.when(_overflow(nlow_ref[0, t], half))
        def _():
            lo2, hi2 = weighted(t, *rest)
            o_ref[t, 0] = lo + lo2
            o_ref[t, 1] = hi + hi2

        return carry

    lax.fori_loop(0, tm, token, 0)


def _peer_v(off, nlow3, coef, tab, half, tm=128):
    t = off.shape[0]
    rows = PEER_HALF * SUBLANES
    smem = pl.BlockSpec((tm, LANES), lambda i: (i, 0), memory_space=pltpu.SMEM)
    return pl.pallas_call(
        functools.partial(_peer_v_kernel, tm=tm, half=half),
        grid=(t // tm,),
        in_specs=[smem,
                  pl.BlockSpec((None, 1, tm), lambda i: (i, 0, 0), memory_space=pltpu.SMEM),
                  smem,
                  pl.BlockSpec((rows, LANES), lambda i: (half, 0), pipeline_mode=pl.Buffered(1))],
        out_specs=pl.BlockSpec((tm, 2, SUBLANES, LANES), lambda i: (i, 0, 0, 0)),
        out_shape=jax.ShapeDtypeStruct((t, 2, SUBLANES, LANES), F32),
        compiler_params=_cparams(("arbitrary",)),
        name="peer_expert_sum",
    )(off, nlow3, coef, tab)


def _add3_kernel(x_ref, p_ref, q_ref, o_ref):
    o_ref[...] = x_ref[...] + (p_ref[...] + q_ref[...])


def _add3(x, p, q, tm=512):
    t, d = x.shape
    tm = min(tm, t)
    spec = pl.BlockSpec((tm, d), lambda i: (i, 0))
    return pl.pallas_call(
        _add3_kernel,
        grid=(t // tm,),
        in_specs=[spec, spec, spec],
        out_specs=spec,
        out_shape=jax.ShapeDtypeStruct((t, d), F32),
        compiler_params=_cparams(("parallel",)),
        name="peer_residual",
    )(x, p, q)


def _peer(x1, norm2_g, wq_bf, keys_bf, u_packed, v_packed, tm=128):
    t, d = x1.shape
    tm = min(tm, t)
    (xn_bf,) = _rmsnorm(x1, norm2_g, (BF16,))
    q = _matmul(xn_bf, wq_bf, wq_bf.shape[1], 0, BF16)
    eid_t, gate_t = _peer_topk(q, keys_bf)
    eid, gate = eid_t.T, gate_t.T
    high = eid >= PEER_HALF
    order = jnp.argsort(high, axis=-1, stable=True)
    eid = jnp.take_along_axis(eid, order, axis=-1)
    gate = jnp.take_along_axis(gate, order, axis=-1)
    nlow3 = (PEER_PAIRS - jnp.sum(high, axis=-1, dtype=I32)).reshape(t // tm, 1, tm)
    off = (eid & (PEER_HALF - 1)) * SUBLANES
    xw = _pack_table(xn_bf).reshape(t, SUBLANES, LANES)
    a0 = _peer_u(off, nlow3, u_packed, xw, 0, tm)
    a1 = _peer_u(off, nlow3, u_packed, xw, 1, tm)
    c0, c1 = _peer_coef(a0, a1, eid, gate)
    o0 = _peer_v(off, nlow3, c0, v_packed, 0, tm)
    o1 = _peer_v(off, nlow3, c1, v_packed, 1, tm)
    return _add3(x1, o0.reshape(t, d), o1.reshape(t, d))


def kernel(x_prompt, x_sample, norm1_g, w_in, q_norm_g, k_norm_g, lambda_q1, lambda_k1, lambda_q2, lambda_k2, attn_sub_g, conv_w, conv_b, filt_w1, filt_b1, filt_w2, filt_b2, filt_w3, filt_b3, filt_w4, filt_freq, fft_bias, hyena_out_g, w_out, norm2_g, peer_wq, peer_keys, peer_u, peer_v):
    depth = w_in.shape[0]
    d_model = x_prompt.shape[-1]
    att_w = ATT_HEADS * ATT_VDIM
    shapes = [x_prompt.shape[:2], x_sample.shape[:2]]
    x = jnp.concatenate([x_prompt.reshape(-1, d_model), x_sample.reshape(-1, d_model)], axis=0)
    slopes = 2.0 ** (-8.0 * jnp.arange(1, ATT_HEADS + 1, dtype=F32) / ATT_HEADS)

    for l in range(depth):
        lambda_init = 0.8 - 0.6 * math.exp(-0.3 * l)
        lam = (jnp.exp(jnp.sum(lambda_q1[l].astype(F32) * lambda_k1[l].astype(F32)))
               - jnp.exp(jnp.sum(lambda_q2[l].astype(F32) * lambda_k2[l].astype(F32)))
               + lambda_init).reshape(1)
        w_in_bf = w_in[l].astype(BF16)
        q_gain = jnp.tile(q_norm_g[l].astype(F32), 2 * ATT_HEADS) * (ATT_QKDIM ** -0.5 * LOG2E)
        k_gain = jnp.tile(k_norm_g[l].astype(F32), 2 * ATT_HEADS)
        qk_gain = jnp.concatenate([q_gain, k_gain]).reshape(1, 2 * att_w)

        (h_bf,) = _rmsnorm(x, norm1_g[l], (BF16,))
        qk = _matmul(h_bf, w_in_bf, 2 * att_w, 0, BF16, mode="qknorm", extra=qk_gain)
        v = _matmul(h_bf, w_in_bf, att_w, 2 * att_w, BF16)
        zh = _matmul(h_bf, w_in_bf, w_in.shape[2] - 3 * att_w, 3 * att_w, F32)

        filt = (filt_w1[l], filt_b1[l], filt_w2[l], filt_b2[l], filt_w3[l], filt_b3[l], filt_w4[l], filt_freq[l])
        mixes, row = [], 0
        for (b, s) in shapes:
            n = b * s
            att = _attention(qk, v, row, slopes, lam, attn_sub_g[l], b, s, 1.0 - lambda_init)
            hy = _hyena(zh, row, b, s, conv_w[l], conv_b[l], filt, fft_bias[l], hyena_out_g[l])
            mixes.append(jnp.concatenate([att, hy], axis=-1))
            row += n
        mix = jnp.concatenate(mixes, axis=0)
        x = _matmul(mix, w_out[l].astype(BF16), d_model, 0, F32, mode="residual", extra=x)

        x = _peer(x, norm2_g[l], peer_wq[l].astype(BF16), peer_keys[l].astype(BF16),
                  _pack_table(peer_u[l]), _pack_table(peer_v[l]))

    n0 = shapes[0][0] * shapes[0][1]
    return (x[:n0].reshape(x_prompt.shape), x[n0:].reshape(x_sample.shape))
```

```python
import functools
import math

import jax
import jax.numpy as jnp
from jax import lax
from jax.experimental import pallas as pl
from jax.experimental.pallas import tpu as pltpu

F32 = jnp.float32
BF16 = jnp.bfloat16
I32 = jnp.int32
U32 = jnp.uint32

RMS_EPS = 1e-6
LOG2E = 1.4426950408889634
LANES = 128
SUBLANES = 8
VMEM_LIMIT_BYTES = 56 * 1024 * 1024

ATT_HEADS = 8
ATT_VDIM = 128
ATT_QKDIM = 64
HYENA_ORDER = 2
FILTER_BANDS = 16
DECAY_FAST = 0.3
DECAY_SLOW = 1.5
DECAY_TARGET = 1e-2
PEER_HEADS = 8
PEER_NKEYS = 128
PEER_TOPK = 16
PEER_HALF = PEER_NKEYS * PEER_NKEYS // 2
PEER_PAIRS = PEER_HEADS * PEER_TOPK
PEER_WINDOW = 80


def _cparams(sem, vmem=VMEM_LIMIT_BYTES):
    return pltpu.CompilerParams(dimension_semantics=sem, vmem_limit_bytes=vmem)


def _rmsnorm_kernel(x_ref, g_ref, *o_refs):
    x = x_ref[...]
    ms = jnp.mean(x * x, axis=-1, keepdims=True)
    y = x * lax.rsqrt(ms + RMS_EPS) * g_ref[...]
    for o_ref in o_refs:
        o_ref[...] = y.astype(o_ref.dtype)


def _rmsnorm(x, g, out_dtypes, tm=512):
    t, d = x.shape
    tm = min(tm, t)
    spec = pl.BlockSpec((tm, d), lambda i: (i, 0))
    return pl.pallas_call(
        _rmsnorm_kernel,
        grid=(t // tm,),
        in_specs=[spec, pl.BlockSpec((1, d), lambda i: (0, 0))],
        out_specs=[spec for _ in out_dtypes],
        out_shape=[jax.ShapeDtypeStruct((t, d), dt) for dt in out_dtypes],
        compiler_params=_cparams(("parallel",)),
        name="rmsnorm",
    )(x, g.reshape(1, d).astype(F32))


def _group_rms_scale(x, gain):
    lane = lax.broadcasted_iota(I32, x.shape, 1)
    lo = lane < ATT_QKDIM
    x2 = x * x
    s_lo = jnp.sum(jnp.where(lo, x2, 0.0), axis=-1, keepdims=True)
    s_hi = jnp.sum(jnp.where(lo, 0.0, x2), axis=-1, keepdims=True)
    ms = jnp.where(lo, s_lo, s_hi) * (1.0 / ATT_QKDIM)
    return x * lax.rsqrt(ms + RMS_EPS) * gain


def _mm_kernel(a_ref, b_ref, *rest, mode):
    acc = jnp.dot(a_ref[...], b_ref[...], preferred_element_type=F32)
    if mode == "plain":
        (o_ref,) = rest
        o_ref[...] = acc.astype(o_ref.dtype)
    elif mode == "residual":
        r_ref, o_ref = rest
        o_ref[...] = (acc + r_ref[...]).astype(o_ref.dtype)
    elif mode == "qknorm":
        g_ref, o_ref = rest
        for c in range(acc.shape[1] // LANES):
            sl = slice(c * LANES, (c + 1) * LANES)
            o_ref[:, sl] = _group_rms_scale(acc[:, sl], g_ref[:, sl]).astype(o_ref.dtype)
    else:
        raise ValueError(mode)


def _matmul(a, b, n_cols, col_off, out_dtype, mode="plain", extra=None, tm=1024, tn=1024):
    m, k = a.shape
    tm = min(tm, m)
    tn = min(tn, n_cols)
    assert col_off % tn == 0 and n_cols % tn == 0 and m % tm == 0
    off = col_off // tn
    in_specs = [pl.BlockSpec((tm, k), lambda i, j: (i, 0)),
                pl.BlockSpec((k, tn), lambda i, j: (0, j + off))]
    args = [a, b]
    if mode == "residual":
        in_specs.append(pl.BlockSpec((tm, tn), lambda i, j: (i, j)))
        args.append(extra)
    elif mode == "qknorm":
        in_specs.append(pl.BlockSpec((1, tn), lambda i, j: (0, j)))
        args.append(extra)
    return pl.pallas_call(
        functools.partial(_mm_kernel, mode=mode),
        grid=(m // tm, n_cols // tn),
        in_specs=in_specs,
        out_specs=pl.BlockSpec((tm, tn), lambda i, j: (i, j)),
        out_shape=jax.ShapeDtypeStruct((m, n_cols), out_dtype),
        compiler_params=_cparams(("parallel", "arbitrary")),
        name="matmul_" + mode,
    )(*args)


def _mm_nt_kernel(w_ref, a_ref, o_ref):
    o_ref[...] = lax.dot_general(w_ref[...], a_ref[...], (((1,), (1,)), ((), ())),
                                 preferred_element_type=F32).astype(o_ref.dtype)


def _matmul_nt(wt, a, out_dtype, tm=1024):
    n, k = wt.shape
    m = a.shape[0]
    tm = min(tm, m)
    return pl.pallas_call(
        _mm_nt_kernel,
        grid=(m // tm,),
        in_specs=[pl.BlockSpec((n, k), lambda i: (0, 0)), pl.BlockSpec((tm, k), lambda i: (i, 0))],
        out_specs=pl.BlockSpec((n, tm), lambda i: (0, i)),
        out_shape=jax.ShapeDtypeStruct((n, m), out_dtype),
        compiler_params=_cparams(("parallel",)),
        name="matmul_nt",
    )(wt, a)


def _alibi_columns(slopes, tq, tk):
    return [_alibi_side(slopes, tq, True), _alibi_side(slopes, tk, False)]


def _alibi_side(slopes, n, query_side):
    pos = jnp.arange(n, dtype=F32)
    val = (slopes.astype(F32) * LOG2E)[:, None] * pos[None, :]

    def pieces(x):
        p1 = x.astype(BF16)
        r1 = x - p1.astype(F32)
        p2 = r1.astype(BF16)
        p3 = (r1 - p2.astype(F32)).astype(BF16)
        return [p1, p2, p3]

    ones = [jnp.ones_like(val, BF16)] * 3
    six = jnp.stack(pieces(-val) + ones if query_side else ones + pieces(val), axis=-1)
    pad = jnp.zeros(val.shape + (ATT_QKDIM - 6,), BF16)
    return jnp.concatenate([six, pad, six, pad], axis=-1)


def _attn_kernel(slope_ref, lam_ref, q_ref, k_ref, vt_ref, aq_ref, ak_ref, g_ref, o_ref,
                 m_ref, l_ref, acc_ref, *, tq, tk, nk, hb, out_scale):
    hg = pl.program_id(1)
    i = pl.program_id(2)
    j = pl.program_id(3)

    @pl.when(j == 0)
    def _():
        m_ref[...] = jnp.full(m_ref.shape, -jnp.inf, F32)
        l_ref[...] = jnp.zeros(l_ref.shape, F32)
        acc_ref[...] = jnp.zeros(acc_ref.shape, F32)

    q_first = lax.broadcasted_iota(I32, (tq, LANES), 1) < ATT_QKDIM
    k_first = lax.broadcasted_iota(I32, (tk, LANES), 1) < ATT_QKDIM
    nt = (((1,), (1,)), ((), ()))

    def update(hh, scores, shift):
        vt = vt_ref[hh * LANES:(hh + 1) * LANES, :]
        for c, s in enumerate(scores):
            m_old = m_ref[hh, c]
            m_new = jnp.maximum(m_old, jnp.max(s, axis=0, keepdims=True) + shift)
            alpha = jnp.exp2(m_old - m_new)
            p = jnp.exp2(s - (m_new - shift))
            l_ref[hh, c] = alpha * l_ref[hh, c] + jnp.sum(p, axis=0, keepdims=True)
            acc_ref[hh, c] = alpha * acc_ref[hh, c] + jnp.dot(vt, p.astype(BF16),
                                                              preferred_element_type=F32)
            m_ref[hh, c] = m_new

    keys_before = i * tq >= (j + 1) * tk
    keys_after = (i + 1) * tq <= j * tk
    off_diagonal = jnp.logical_or(keys_before, keys_after)

    @pl.when(off_diagonal)
    def _():
        sign = jnp.where(keys_after, -1.0, 1.0).astype(BF16)
        gap = jnp.abs(i * tq - j * tk).astype(F32)
        for hh in range(hb):
            sl = slice(hh * LANES, (hh + 1) * LANES)
            q = q_ref[:, sl]
            k = k_ref[:, sl]
            aq = aq_ref[hh]
            ak = ak_ref[hh] * sign
            s0 = lax.dot_general(jnp.where(k_first, k, ak), jnp.where(q_first, q, aq), nt,
                                 preferred_element_type=F32)
            s1 = lax.dot_general(jnp.where(k_first, ak, k), jnp.where(q_first, aq, q), nt,
                                 preferred_element_type=F32)
            update(hh, (s0, s1), -(slope_ref[hg * hb + hh] * LOG2E) * gap)

    @pl.when(jnp.logical_not(off_diagonal))
    def _():
        kpos = lax.broadcasted_iota(I32, (tk, tq), 0) + j * tk
        qpos = lax.broadcasted_iota(I32, (tk, tq), 1) + i * tq
        dist = jnp.abs(kpos - qpos).astype(F32)
        for hh in range(hb):
            sl = slice(hh * LANES, (hh + 1) * LANES)
            q = q_ref[:, sl]
            k = k_ref[:, sl]
            zero = jnp.zeros_like(q)
            bias = dist * (-(slope_ref[hg * hb + hh] * LOG2E))
            s0 = lax.dot_general(k, jnp.where(q_first, q, zero), nt, preferred_element_type=F32) + bias
            s1 = lax.dot_general(k, jnp.where(q_first, zero, q), nt, preferred_element_type=F32) + bias
            update(hh, (s0, s1), 0.0)

    @pl.when(j == nk - 1)
    def _():
        for hh in range(hb):
            o = acc_ref[hh, 0] / l_ref[hh, 0] - lam_ref[0] * (acc_ref[hh, 1] / l_ref[hh, 1])
            ms = jnp.mean(o * o, axis=0, keepdims=True)
            y = o * lax.rsqrt(ms + RMS_EPS) * (g_ref[...] * out_scale)
            o_ref[:, hh * LANES:(hh + 1) * LANES] = y.T.astype(o_ref.dtype)


def _attention(qk, vt, row_off, slopes, lam, sub_g, batch, seq, out_scale, tq=512, tk=1024, hb=2):
    tq = min(tq, seq)
    tk = min(tk, seq // 4)
    nq, nk = seq // tq, seq // tk
    assert row_off % tq == 0 and row_off % tk == 0 and ATT_HEADS % hb == 0
    oq, ok = row_off // tq, row_off // tk
    ng = ATT_HEADS // hb
    aq, ak = _alibi_columns(slopes, tq, tk)
    kern = functools.partial(_attn_kernel, tq=tq, tk=tk, nk=nk, hb=hb, out_scale=out_scale)
    smem = pl.BlockSpec(memory_space=pltpu.SMEM)
    return pl.pallas_call(
        kern,
        grid=(batch, ng, nq, nk),
        in_specs=[smem, smem,
                  pl.BlockSpec((tq, hb * LANES), lambda b, h, i, j: (oq + b * nq + i, h)),
                  pl.BlockSpec((tk, hb * LANES), lambda b, h, i, j: (ok + b * nk + j, ng + h)),
                  pl.BlockSpec((hb * LANES, tk), lambda b, h, i, j: (h, ok + b * nk + j)),
                  pl.BlockSpec((hb, tq, LANES), lambda b, h, i, j: (h, 0, 0)),
                  pl.BlockSpec((hb, tk, LANES), lambda b, h, i, j: (h, 0, 0)),
                  pl.BlockSpec((LANES, 1), lambda b, h, i, j: (0, 0))],
        out_specs=pl.BlockSpec((tq, hb * LANES), lambda b, h, i, j: (b * nq + i, h)),
        out_shape=jax.ShapeDtypeStruct((batch * seq, ATT_HEADS * ATT_VDIM), BF16),
        scratch_shapes=[pltpu.VMEM((hb, 2, 1, tq), F32), pltpu.VMEM((hb, 2, 1, tq), F32),
                        pltpu.VMEM((hb, 2, LANES, tq), F32)],
        compiler_params=_cparams(("parallel", "parallel", "parallel", "arbitrary")),
        name="diff_attention",
    )(slopes, lam, qk, qk, vt, aq, ak, sub_g.reshape(LANES, 1).astype(F32))


def _shortconv_kernel(z_ref, w_ref, b_ref, o_ref):
    z = z_ref[...]
    n = z.shape[0]
    row = lax.broadcasted_iota(I32, z.shape, 0)
    prev = jnp.where(row == 0, 0.0, pltpu.roll(z, 1, axis=0))
    nxt = jnp.where(row == n - 1, 0.0, pltpu.roll(z, n - 1, axis=0))
    w = w_ref[...]
    o_ref[...] = prev * w[0:1] + z * w[1:2] + nxt * w[2:3] + b_ref[...]


def _shortconv(zh, row_off, conv_w, conv_b, batch, seq, cb=256):
    c3 = zh.shape[1]
    c = c3 // 3
    ncb = c // cb
    assert row_off % seq == 0
    ob = row_off // seq
    out = pl.pallas_call(
        _shortconv_kernel,
        grid=(batch, 3, ncb),
        in_specs=[pl.BlockSpec((seq, cb), lambda b, p, j: (ob + b, p * ncb + j)),
                  pl.BlockSpec((3, cb), lambda b, p, j: (0, p * ncb + j)),
                  pl.BlockSpec((1, cb), lambda b, p, j: (0, p * ncb + j))],
        out_specs=pl.BlockSpec((None, None, seq, cb), lambda b, p, j: (p, b, 0, j)),
        out_shape=jax.ShapeDtypeStruct((3, batch, seq, c), F32),
        compiler_params=_cparams(("parallel", "parallel", "parallel")),
        name="hyena_shortconv",
    )(zh, conv_w.astype(F32), conv_b.reshape(1, c3).astype(F32))
    return out


def _filter_kernel(z_ref, w1_ref, b1_ref, w2_ref, b2_ref, w3_ref, b3_ref, fr_ref, w4_ref,
                   t_ref, d_ref, o_ref, *, tl):
    g = pl.program_id(0)
    i = pl.program_id(1)
    hi = lax.Precision.HIGHEST
    fr = fr_ref[...]
    h = jnp.sin(fr * (jnp.dot(z_ref[...], w1_ref[...], precision=hi, preferred_element_type=F32) + b1_ref[...]))
    h = jnp.sin(fr * (jnp.dot(h, w2_ref[...], precision=hi, preferred_element_type=F32) + b2_ref[...]))
    h = jnp.sin(fr * (jnp.dot(h, w3_ref[...], precision=hi, preferred_element_type=F32) + b3_ref[...]))
    f = jnp.dot(h, w4_ref[...], precision=hi, preferred_element_type=F32)
    f = f * jnp.exp(-t_ref[...] * d_ref[...])
    row = lax.broadcasted_iota(I32, f.shape, 0) + i * tl
    drop = jnp.logical_and(row == 0, g % 2 == 1)
    o_ref[...] = jnp.where(drop, 0.0, f)


def _hyena_filter_signals(seq, w1, b1, w2, b2, w3, b3, w4, freq, n_ch, tl=512):
    t = jnp.linspace(0.0, 1.0, seq, dtype=F32)[:, None]
    w = 2.0 * math.pi * jnp.arange(seq, dtype=F32)[:, None] / seq
    f = jnp.linspace(1e-4, FILTER_BANDS - 1, FILTER_BANDS, dtype=F32)[None, :]
    z = jnp.concatenate([t, jnp.cos(f * w), -jnp.sin(f * w)], axis=-1)
    deltas = jnp.abs(jnp.linspace(math.log(DECAY_FAST) / DECAY_TARGET,
                                  math.log(DECAY_SLOW) / DECAY_TARGET, n_ch, dtype=F32))[None, :]
    hid = w1.shape[1]
    emb = LANES
    z = jnp.pad(z, ((0, 0), (0, emb - z.shape[1])))
    w1 = jnp.pad(w1.astype(F32), ((0, emb - w1.shape[0]), (0, 0)))
    tl = min(tl, seq)
    full = lambda shape: pl.BlockSpec(shape, lambda g, i: tuple(0 for _ in shape))
    return pl.pallas_call(
        functools.partial(_filter_kernel, tl=tl),
        grid=(2 * HYENA_ORDER, seq // tl),
        in_specs=[pl.BlockSpec((tl, emb), lambda g, i: (i, 0)),
                  full((emb, hid)), full((1, hid)), full((hid, hid)), full((1, hid)),
                  full((hid, hid)), full((1, hid)), full((1, hid)),
                  pl.BlockSpec((hid, n_ch), lambda g, i: (0, g)),
                  pl.BlockSpec((tl, 1), lambda g, i: (i, 0)),
                  full((1, n_ch))],
        out_specs=pl.BlockSpec((None, tl, n_ch), lambda g, i: (g, i, 0)),
        out_shape=jax.ShapeDtypeStruct((2 * HYENA_ORDER, seq, n_ch), F32),
        compiler_params=_cparams(("parallel", "parallel")),
        name="hyena_filter_mlp",
    )(z, w1.astype(F32), b1.reshape(1, hid).astype(F32), w2.astype(F32), b2.reshape(1, hid).astype(F32),
      w3.astype(F32), b3.reshape(1, hid).astype(F32), freq.reshape(1, hid).astype(F32), w4.astype(F32),
      t, deltas)


def _dft_tables(r):
    n = r * r
    k2 = jnp.arange(r, dtype=I32)
    n2 = jnp.arange(r // 2, dtype=I32)
    ang1 = (2.0 * math.pi / r) * ((k2[:, None] * n2[None, :]) % r).astype(F32)
    f1 = jnp.concatenate([jnp.cos(ang1), -jnp.sin(ang1)], axis=0)
    k1 = jnp.arange(r, dtype=I32)
    n1 = jnp.arange(r, dtype=I32)
    kk = r * k1[None, :, None] + k2[:, None, None]
    ang2 = (2.0 * math.pi / n) * ((kk * n1[None, None, :]) % n).astype(F32)
    mr, mi_ = jnp.cos(ang2), -jnp.sin(ang2)
    mf = jnp.concatenate([jnp.concatenate([mr, -mi_], axis=2),
                          jnp.concatenate([mi_, mr], axis=2)], axis=1)
    minv = jnp.swapaxes(mf, 1, 2)
    g3 = jnp.concatenate([jnp.cos(ang1.T), -jnp.sin(ang1.T)], axis=1) * (1.0 / n)
    return f1.astype(BF16), mf.astype(BF16), minv.astype(BF16), g3.astype(BF16)


def _fft1_kernel(f_ref, x_ref, o_ref):
    o_ref[...] = jnp.dot(f_ref[...], x_ref[...].astype(BF16), preferred_element_type=F32).astype(o_ref.dtype)


def _fft_stage1(x4, part, f1, r, n_ch, w_lanes=4096):
    nb = x4.shape[1]
    xv = x4.reshape(x4.shape[0], nb, r // 2, r * n_ch)
    nw = (r * n_ch) // w_lanes
    return pl.pallas_call(
        _fft1_kernel,
        grid=(nb, nw),
        in_specs=[pl.BlockSpec((2 * r, r // 2), lambda b, j: (0, 0)),
                  pl.BlockSpec((None, None, r // 2, w_lanes), lambda b, j: (part, b, 0, j))],
        out_specs=pl.BlockSpec((None, 2 * r, w_lanes), lambda b, j: (b, 0, j)),
        out_shape=jax.ShapeDtypeStruct((nb, 2 * r, r * n_ch), BF16),
        compiler_params=_cparams(("parallel", "parallel")),
        name="hyena_dft_stage1",
    )(f1, xv)


def _filter_spec_kernel(mf_ref, bf_ref, bb_ref, o_ref, *, r):
    m = mf_ref[...]
    c = bf_ref.shape[-1]
    xf = jnp.dot(m, bf_ref[...].reshape(2 * r, c), preferred_element_type=F32)
    xb = jnp.dot(m, bb_ref[...].reshape(2 * r, c), preferred_element_type=F32)
    o_ref[0] = (xf[:r] + xb[:r]).astype(o_ref.dtype)
    o_ref[1] = (xf[r:] - xb[r:]).astype(o_ref.dtype)


def _filter_spectrum(b1, mf, r, n_ch):
    bv = b1.reshape(2 * HYENA_ORDER, 2, r, r, n_ch)
    blk = lambda sel: pl.BlockSpec((None, 2, None, r, n_ch), lambda o, k: (2 * o + sel, 0, k, 0, 0))
    return pl.pallas_call(
        functools.partial(_filter_spec_kernel, r=r),
        grid=(HYENA_ORDER, r),
        in_specs=[pl.BlockSpec((None, 2 * r, 2 * r), lambda o, k: (k, 0, 0)), blk(0), blk(1)],
        out_specs=pl.BlockSpec((None, 2, None, r, n_ch), lambda o, k: (o, 0, k, 0, 0)),
        out_shape=jax.ShapeDtypeStruct((HYENA_ORDER, 2, r, r, n_ch), BF16),
        compiler_params=_cparams(("parallel", "parallel")),
        name="hyena_filter_spectrum",
    )(mf, bv, bv)


def _fft2_kernel(mf_ref, mi_ref, b_ref, h_ref, o_ref, *, r):
    c = b_ref.shape[-1]
    x = jnp.dot(mf_ref[...], b_ref[...].reshape(2 * r, c), preferred_element_type=F32)
    xr, xi = x[:r], x[r:]
    hr = h_ref[0].astype(F32)
    hi = h_ref[1].astype(F32)
    y = jnp.concatenate([xr * hr - xi * hi, xr * hi + xi * hr], axis=0).astype(BF16)
    o_ref[...] = jnp.dot(mi_ref[...], y, preferred_element_type=F32).reshape(2, r, c).astype(o_ref.dtype)


def _fft_stage2(b1, h, order, mf, minv, r, n_ch):
    nb = b1.shape[0]
    bv = b1.reshape(nb, 2, r, r, n_ch)
    out = pl.pallas_call(
        functools.partial(_fft2_kernel, r=r),
        grid=(nb, r),
        in_specs=[pl.BlockSpec((None, 2 * r, 2 * r), lambda b, k: (k, 0, 0)),
                  pl.BlockSpec((None, 2 * r, 2 * r), lambda b, k: (k, 0, 0)),
                  pl.BlockSpec((None, 2, None, r, n_ch), lambda b, k: (b, 0, k, 0, 0)),
                  pl.BlockSpec((None, 2, None, r, n_ch), lambda b, k: (order, 0, k, 0, 0))],
        out_specs=pl.BlockSpec((None, 2, None, r, n_ch), lambda b, k: (b, 0, k, 0, 0)),
        out_shape=jax.ShapeDtypeStruct((nb, 2, r, r, n_ch), BF16),
        compiler_params=_cparams(("parallel", "parallel")),
        name="hyena_dft_stage2",
    )(mf, minv, bv, h)
    return out.reshape(nb, 2 * r, r * n_ch)


def _fft3_kernel(g_ref, c_ref, gate_ref, s_ref, bias_ref, ng_ref, o_ref, *, final, n_ch):
    y = jnp.dot(g_ref[...], c_ref[...], preferred_element_type=F32)
    s_new = gate_ref[...] * (y + s_ref[...] * bias_ref[...])
    if not final:
        o_ref[...] = s_new
        return
    for c in range(s_new.shape[1] // n_ch):
        sl = slice(c * n_ch, (c + 1) * n_ch)
        x = s_new[:, sl]
        ms = jnp.mean(x * x, axis=-1, keepdims=True)
        o_ref[:, sl] = (x * lax.rsqrt(ms + RMS_EPS) * ng_ref[...]).astype(o_ref.dtype)


def _fft_stage3(c2, g3, z4, gate_part, s4, s_part, bias, norm_g, final, r, n_ch, w_lanes=4096):
    nb = c2.shape[0]
    zv = z4.reshape(z4.shape[0], nb, r // 2, r * n_ch)
    sv = s4.reshape(s4.shape[0], nb, r // 2, r * n_ch)
    nw = (r * n_ch) // w_lanes
    bias_row = jnp.tile(bias.reshape(1, n_ch).astype(F32), (1, w_lanes // n_ch))
    out_dtype = BF16 if final else F32
    dspec = lambda part: pl.BlockSpec((None, None, r // 2, w_lanes), lambda b, j: (part, b, 0, j))
    out = pl.pallas_call(
        functools.partial(_fft3_kernel, final=final, n_ch=n_ch),
        grid=(nb, nw),
        in_specs=[pl.BlockSpec((r // 2, 2 * r), lambda b, j: (0, 0)),
                  pl.BlockSpec((None, 2 * r, w_lanes), lambda b, j: (b, 0, j)),
                  dspec(gate_part), dspec(s_part),
                  pl.BlockSpec((1, w_lanes), lambda b, j: (0, 0)),
                  pl.BlockSpec((1, n_ch), lambda b, j: (0, 0))],
        out_specs=pl.BlockSpec((None, r // 2, w_lanes), lambda b, j: (b, 0, j)),
        out_shape=jax.ShapeDtypeStruct((nb, r // 2, r * n_ch), out_dtype),
        compiler_params=_cparams(("parallel", "parallel")),
        name="hyena_dft_stage3",
    )(g3, c2, zv, sv, bias_row, norm_g.reshape(1, n_ch).astype(F32))
    return out


def _hyena(zh, row_off, batch, seq, conv_w, conv_b, filt, fft_bias, out_g):
    n_ch = zh.shape[1] // 3
    r = int(round(math.sqrt(2 * seq)))
    assert r * r == 2 * seq and r % 16 == 0
    f1, mf, minv, g3 = _dft_tables(r)
    sig = _hyena_filter_signals(seq, *filt, n_ch=n_ch)
    hb1 = _fft_stage1(sig[None], 0, f1, r, n_ch)
    h = _filter_spectrum(hb1, mf, r, n_ch)
    z4 = _shortconv(zh, row_off, conv_w, conv_b, batch, seq)
    s4, s_part = z4, 2
    for o in range(HYENA_ORDER):
        b1 = _fft_stage1(s4, s_part, f1, r, n_ch)
        c2 = _fft_stage2(b1, h, o, mf, minv, r, n_ch)
        final = o == HYENA_ORDER - 1
        s = _fft_stage3(c2, g3, z4, o, s4, s_part, fft_bias[o], out_g, final, r, n_ch)
        s4, s_part = s.reshape(1, batch, seq, n_ch), 0
    return s4.reshape(batch * seq, n_ch)


def _extract_top(s, key, count):
    vals, keys = [], []
    for _ in range(count):
        m = jnp.max(s, axis=0, keepdims=True)
        kmin = jnp.min(jnp.where(s == m, key, jnp.inf), axis=0, keepdims=True)
        s = jnp.where(key == kmin, -jnp.inf, s)
        vals.append(m)
        keys.append(kmin)
    return vals, keys


def _peer_topk_kernel(q_ref, keys_ref, eid_ref, gate_ref):
    t = q_ref.shape[0]
    nk = PEER_NKEYS
    q = q_ref[...]
    row_key = lax.broadcasted_iota(I32, (nk, t), 0).astype(F32)
    tops = []
    for c in range(2):
        s = lax.dot_general(keys_ref[c], q[:, c * nk:(c + 1) * nk], (((1,), (1,)), ((), ())),
                            preferred_element_type=F32)
        tops.append(_extract_top(s, row_key, PEER_TOPK))
    (v1, i1), (v2, i2) = tops
    rows16 = lax.broadcasted_iota(I32, (PEER_TOPK, t), 0)
    v2a = jnp.zeros((PEER_TOPK, t), F32)
    i2a = jnp.zeros((PEER_TOPK, t), F32)
    for j in range(PEER_TOPK):
        v2a = jnp.where(rows16 == j, v2[j], v2a)
        i2a = jnp.where(rows16 == j, i2[j], i2a)
    n_exp = float(nk * nk)
    half = PEER_TOPK // 2
    rows8 = lax.broadcasted_iota(I32, (half, t), 0)
    pos8 = rows8.astype(F32)
    v2h, i2h = v2a[:half], i2a[:half]
    cand = [v1[0] + v2a]
    ckey = [rows16.astype(F32) * n_exp + (i1[0] * float(nk) + i2a)]
    for i in range(1, half):
        cand.append(jnp.where(rows8 < PEER_TOPK // (i + 1), v1[i] + v2h, -jnp.inf))
        ckey.append((pos8 + float(i * PEER_TOPK)) * n_exp + (i1[i] * float(nk) + i2h))
    v1t = jnp.zeros((half, t), F32)
    i1t = jnp.zeros((half, t), F32)
    for r in range(half):
        v1t = jnp.where(rows8 == r, v1[half + r], v1t)
        i1t = jnp.where(rows8 == r, i1[half + r], i1t)
    cand.append(v1t + v2[0])
    ckey.append((pos8 + float(half)) * (PEER_TOPK * n_exp) + (i1t * float(nk) + i2[0]))
    tv, tk_ = _extract_top(jnp.concatenate(cand, axis=0), jnp.concatenate(ckey, axis=0), PEER_TOPK)
    denom = jnp.zeros((1, t), F32)
    es = []
    for k in range(PEER_TOPK):
        e = jnp.exp(tv[k] - tv[0])
        es.append(e)
        denom = denom + e
    eid = jnp.zeros((PEER_TOPK, t), F32)
    gate = jnp.zeros((PEER_TOPK, t), F32)
    for k in range(PEER_TOPK):
        pos = jnp.floor(tk_[k] * (1.0 / n_exp))
        eid = jnp.where(rows16 == k, tk_[k] - pos * n_exp, eid)
        gate = jnp.where(rows16 == k, es[k] / denom, gate)
    eid_ref[...] = eid.astype(I32)
    gate_ref[...] = gate


def _peer_topk(q, keys, tm=256):
    t = q.shape[0]
    tm = min(tm, t)
    out_spec = pl.BlockSpec((PEER_TOPK, tm), lambda i, h: (h, i))
    return pl.pallas_call(
        _peer_topk_kernel,
        grid=(t // tm, PEER_HEADS),
        in_specs=[pl.BlockSpec((tm, 2 * PEER_NKEYS), lambda i, h: (i, h)),
                  pl.BlockSpec((None, 2, PEER_NKEYS, PEER_NKEYS), lambda i, h: (h, 0, 0, 0))],
        out_specs=[out_spec, out_spec],
        out_shape=[jax.ShapeDtypeStruct((PEER_HEADS * PEER_TOPK, t), I32),
                   jax.ShapeDtypeStruct((PEER_HEADS * PEER_TOPK, t), F32)],
        compiler_params=_cparams(("parallel", "parallel")),
        name="peer_topk",
    )(q, keys)


def _pack_table(tab):
    e, d = tab.shape
    assert d == 2 * SUBLANES * LANES
    bits = lax.bitcast_convert_type(tab.astype(BF16), jnp.uint16).astype(U32)
    packed = bits[:, :d // 2] | (bits[:, d // 2:] << 16)
    return packed.reshape(e * SUBLANES, LANES)


def _unpack(w):
    lo = lax.bitcast_convert_type(w << 16, F32)
    hi = lax.bitcast_convert_type(w & jnp.uint32(0xFFFF0000), F32)
    return lo, hi


_BITREV8 = (0, 4, 2, 6, 1, 5, 3, 7)


def _sublane_fold8(parts):
    sub = lax.broadcasted_iota(I32, (2 * SUBLANES, LANES), 0) // 2

    def rolled(a, shift):
        return pltpu.bitcast(pltpu.roll(pltpu.bitcast(a, U32), shift, axis=0), BF16)

    lvl = [parts[_BITREV8[r]] for r in range(8)]
    for shift, mask in ((4, sub < 4), (2, (sub % 4) < 2), (1, (sub % 2) < 1)):
        nxt = []
        for a, b in zip(lvl[0::2], lvl[1::2]):
            nxt.append(jnp.where(mask, a + rolled(a, SUBLANES - shift), b + rolled(b, shift)))
        lvl = nxt
    return lvl[0]


def _pair_ranges(half):
    if half == 0:
        return (0, PEER_WINDOW), (PEER_WINDOW, PEER_PAIRS)
    return (PEER_PAIRS - PEER_WINDOW, PEER_PAIRS), (0, PEER_PAIRS - PEER_WINDOW)


def _overflow(n_low, half):
    return n_low > PEER_WINDOW if half == 0 else n_low < PEER_PAIRS - PEER_WINDOW


def _table_rows(tab_ref, off):
    return _unpack(tab_ref[pl.ds(pl.multiple_of(off, SUBLANES), SUBLANES), :])


def _peer_u_kernel(off_ref, nlow_ref, tab_ref, x_ref, o_ref, extra_ref, *, tm, half):
    lane = lax.broadcasted_iota(I32, (SUBLANES, LANES), 1)
    sub = lax.broadcasted_iota(I32, (SUBLANES, LANES), 0)
    lane_grp = lax.shift_right_logical(lane, 3)
    diag = sub == (lane & (SUBLANES - 1))
    main, rest = _pair_ranges(half)

    def folded(t, p0, p1):
        xb = pltpu.bitcast(x_ref[t], BF16)
        out = []
        for g in range(p0 // SUBLANES, p1 // SUBLANES):
            parts = []
            for r in range(SUBLANES):
                off = pl.multiple_of(off_ref[t, g * SUBLANES + r], SUBLANES)
                parts.append(pltpu.bitcast(tab_ref[pl.ds(off, SUBLANES), :], BF16) * xb)
            out.append(pltpu.bitcast(_sublane_fold8(parts), U32))
        return out

    def lane_sums(folds, p0):
        mat = jnp.zeros((SUBLANES, LANES), F32)
        for i, f in enumerate(folds):
            lo, hi = _unpack(f)
            mat = jnp.where(lane_grp == p0 // SUBLANES + i, jnp.sum(lo + hi, axis=-1, keepdims=True), mat)
        return jnp.sum(jnp.where(diag, mat, 0.0), axis=0, keepdims=True)

    def finish(t, folds):
        o_ref[pl.ds(t, 1), :] = lane_sums(folds, main[0]) + extra_ref[pl.ds(t, 1), :]

    def token(t, prev):
        cur = folded(t, *main)
        finish(jnp.maximum(t - 1, 0), prev)
        extra_ref[pl.ds(t, 1), :] = jnp.zeros((1, LANES), F32)

        @pl.when(_overflow(nlow_ref[0, t], half))
        def _():
            extra_ref[pl.ds(t, 1), :] = lane_sums(folded(t, *rest), rest[0])

        return tuple(cur)

    extra_ref[pl.ds(0, 1), :] = jnp.zeros((1, LANES), F32)
    zeros = tuple(jnp.zeros((SUBLANES, LANES), U32) for _ in range((main[1] - main[0]) // SUBLANES))
    last = lax.fori_loop(0, tm, token, zeros)
    finish(tm - 1, last)


def _peer_u(off, nlow3, tab, x4, half, tm=128):
    t = off.shape[0]
    rows = PEER_HALF * SUBLANES
    return pl.pallas_call(
        functools.partial(_peer_u_kernel, tm=tm, half=half),
        grid=(t // tm,),
        in_specs=[pl.BlockSpec((tm, LANES), lambda i: (i, 0), memory_space=pltpu.SMEM),
                  pl.BlockSpec((None, 1, tm), lambda i: (i, 0, 0), memory_space=pltpu.SMEM),
                  pl.BlockSpec((rows, LANES), lambda i: (half, 0), pipeline_mode=pl.Buffered(1)),
                  pl.BlockSpec((tm, SUBLANES, LANES), lambda i: (i, 0, 0))],
        out_specs=pl.BlockSpec((tm, LANES), lambda i: (i, 0)),
        out_shape=jax.ShapeDtypeStruct((t, LANES), F32),
        scratch_shapes=[pltpu.VMEM((tm, LANES), F32)],
        compiler_params=_cparams(("arbitrary",)),
        name="peer_expert_scores",
    )(off, nlow3, tab, x4)


def _peer_coef_kernel(a0_ref, a1_ref, eid_ref, gate_ref, c0_ref, c1_ref):
    low = eid_ref[...] < PEER_HALF
    a = jnp.where(low, a0_ref[...], a1_ref[...])
    coef = gate_ref[...] * (0.5 * a * (1.0 + lax.erf(a * (1.0 / math.sqrt(2.0)))))
    c0_ref[...] = jnp.where(low, coef, 0.0)
    c1_ref[...] = jnp.where(low, 0.0, coef)


def _peer_coef(a0, a1, eid, gate, tm=1024):
    t = eid.shape[0]
    tm = min(tm, t)
    spec = pl.BlockSpec((tm, LANES), lambda i: (i, 0))
    return pl.pallas_call(
        _peer_coef_kernel,
        grid=(t // tm,),
        in_specs=[spec, spec, spec, spec],
        out_specs=[spec, spec],
        out_shape=[jax.ShapeDtypeStruct((t, LANES), F32)] * 2,
        compiler_params=_cparams(("parallel",)),
        name="peer_coef",
    )(a0, a1, eid, gate)


def _peer_v_kernel(off_ref, nlow_ref, coef_ref, tab_ref, o_ref, *, tm, half):
    n_acc = 4
    main, rest = _pair_ranges(half)

    def weighted(t, p0, p1):
        acc_lo = [jnp.zeros((SUBLANES, LANES), F32) for _ in range(n_acc)]
        acc_hi = [jnp.zeros((SUBLANES, LANES), F32) for _ in range(n_acc)]
        for p in range(p0, p1):
            c = coef_ref[t, p]
            lo, hi = _table_rows(tab_ref, off_ref[t, p])
            acc_lo[p % n_acc] = acc_lo[p % n_acc] + c * lo
            acc_hi[p % n_acc] = acc_hi[p % n_acc] + c * hi
        return ((acc_lo[0] + acc_lo[1]) + (acc_lo[2] + acc_lo[3]),
                (acc_hi[0] + acc_hi[1]) + (acc_hi[2] + acc_hi[3]))

    def token(t, carry):
        lo, hi = weighted(t, *main)
        o_ref[t, 0] = lo
        o_ref[t, 1] = hi

        @pl.when(_overflow(nlow_ref[0, t], half))
        def _():
            lo2, hi2 = weighted(t, *rest)
            o_ref[t, 0] = lo + lo2
            o_ref[t, 1] = hi + hi2

        return carry

    lax.fori_loop(0, tm, token, 0)


def _peer_v(off, nlow3, coef, tab, half, tm=128):
    t = off.shape[0]
    rows = PEER_HALF * SUBLANES
    smem = pl.BlockSpec((tm, LANES), lambda i: (i, 0), memory_space=pltpu.SMEM)
    return pl.pallas_call(
        functools.partial(_peer_v_kernel, tm=tm, half=half),
        grid=(t // tm,),
        in_specs=[smem,
                  pl.BlockSpec((None, 1, tm), lambda i: (i, 0, 0), memory_space=pltpu.SMEM),
                  smem,
                  pl.BlockSpec((rows, LANES), lambda i: (half, 0), pipeline_mode=pl.Buffered(1))],
        out_specs=pl.BlockSpec((tm, 2, SUBLANES, LANES), lambda i: (i, 0, 0, 0)),
        out_shape=jax.ShapeDtypeStruct((t, 2, SUBLANES, LANES), F32),
        compiler_params=_cparams(("arbitrary",)),
        name="peer_expert_sum",
    )(off, nlow3, coef, tab)


def _add3_kernel(x_ref, p_ref, q_ref, o_ref):
    o_ref[...] = x_ref[...] + (p_ref[...] + q_ref[...])


def _add3(x, p, q, tm=512):
    t, d = x.shape
    tm = min(tm, t)
    spec = pl.BlockSpec((tm, d), lambda i: (i, 0))
    return pl.pallas_call(
        _add3_kernel,
        grid=(t // tm,),
        in_specs=[spec, spec, spec],
        out_specs=spec,
        out_shape=jax.ShapeDtypeStruct((t, d), F32),
        compiler_params=_cparams(("parallel",)),
        name="peer_residual",
    )(x, p, q)


def _peer(x1, norm2_g, wq_bf, keys_bf, u_packed, v_packed, tm=128):
    t, d = x1.shape
    tm = min(tm, t)
    (xn_bf,) = _rmsnorm(x1, norm2_g, (BF16,))
    q = _matmul(xn_bf, wq_bf, wq_bf.shape[1], 0, BF16)
    eid_t, gate_t = _peer_topk(q, keys_bf)
    eid, gate = eid_t.T, gate_t.T
    high = eid >= PEER_HALF
    order = jnp.argsort(high, axis=-1, stable=True)
    eid = jnp.take_along_axis(eid, order, axis=-1)
    gate = jnp.take_along_axis(gate, order, axis=-1)
    nlow3 = (PEER_PAIRS - jnp.sum(high, axis=-1, dtype=I32)).reshape(t // tm, 1, tm)
    off = (eid & (PEER_HALF - 1)) * SUBLANES
    xw = _pack_table(xn_bf).reshape(t, SUBLANES, LANES)
    a0 = _peer_u(off, nlow3, u_packed, xw, 0, tm)
    a1 = _peer_u(off, nlow3, u_packed, xw, 1, tm)
    c0, c1 = _peer_coef(a0, a1, eid, gate)
    o0 = _peer_v(off, nlow3, c0, v_packed, 0, tm)
    o1 = _peer_v(off, nlow3, c1, v_packed, 1, tm)
    return _add3(x1, o0.reshape(t, d), o1.reshape(t, d))


def kernel(x_prompt, x_sample, norm1_g, w_in, q_norm_g, k_norm_g, lambda_q1, lambda_k1, lambda_q2, lambda_k2, attn_sub_g, conv_w, conv_b, filt_w1, filt_b1, filt_w2, filt_b2, filt_w3, filt_b3, filt_w4, filt_freq, fft_bias, hyena_out_g, w_out, norm2_g, peer_wq, peer_keys, peer_u, peer_v):
    depth = w_in.shape[0]
    d_model = x_prompt.shape[-1]
    att_w = ATT_HEADS * ATT_VDIM
    shapes = [x_prompt.shape[:2], x_sample.shape[:2]]
    x = jnp.concatenate([x_prompt.reshape(-1, d_model), x_sample.reshape(-1, d_model)], axis=0)
    slopes = 2.0 ** (-8.0 * jnp.arange(1, ATT_HEADS + 1, dtype=F32) / ATT_HEADS)

    for l in range(depth):
        lambda_init = 0.8 - 0.6 * math.exp(-0.3 * l)
        lam = (jnp.exp(jnp.sum(lambda_q1[l].astype(F32) * lambda_k1[l].astype(F32)))
               - jnp.exp(jnp.sum(lambda_q2[l].astype(F32) * lambda_k2[l].astype(F32)))
               + lambda_init).reshape(1)
        w_in_bf = w_in[l].astype(BF16)
        q_gain = jnp.tile(q_norm_g[l].astype(F32), 2 * ATT_HEADS) * (ATT_QKDIM ** -0.5 * LOG2E)
        k_gain = jnp.tile(k_norm_g[l].astype(F32), 2 * ATT_HEADS)
        qk_gain = jnp.concatenate([q_gain, k_gain]).reshape(1, 2 * att_w)

        (h_bf,) = _rmsnorm(x, norm1_g[l], (BF16,))
        qk = _matmul(h_bf, w_in_bf, 2 * att_w, 0, BF16, mode="qknorm", extra=qk_gain)
        vt = _matmul_nt(w_in_bf[:, 2 * att_w:3 * att_w].T, h_bf, BF16)
        zh = _matmul(h_bf, w_in_bf, w_in.shape[2] - 3 * att_w, 3 * att_w, F32)

        filt = (filt_w1[l], filt_b1[l], filt_w2[l], filt_b2[l], filt_w3[l], filt_b3[l], filt_w4[l], filt_freq[l])
        mixes, row = [], 0
        for (b, s) in shapes:
            n = b * s
            att = _attention(qk, vt, row, slopes, lam, attn_sub_g[l], b, s, 1.0 - lambda_init)
            hy = _hyena(zh, row, b, s, conv_w[l], conv_b[l], filt, fft_bias[l], hyena_out_g[l])
            mixes.append(jnp.concatenate([att, hy], axis=-1))
            row += n
        mix = jnp.concatenate(mixes, axis=0)
        x = _matmul(mix, w_out[l].astype(BF16), d_model, 0, F32, mode="residual", extra=x)

        x = _peer(x, norm2_g[l], peer_wq[l].astype(BF16), peer_keys[l].astype(BF16),
                  _pack_table(peer_u[l]), _pack_table(peer_v[l]))

    n0 = shapes[0][0] * shapes[0][1]
    return (x[:n0].reshape(x_prompt.shape), x[n0:].reshape(x_sample.shape))
```

```python
import functools
import math

import jax
import jax.numpy as jnp
from jax import lax
from jax.experimental import pallas as pl
from jax.experimental.pallas import tpu as pltpu

F32 = jnp.float32
BF16 = jnp.bfloat16
I32 = jnp.int32
U32 = jnp.uint32

RMS_EPS = 1e-6
LOG2E = 1.4426950408889634
LANES = 128
SUBLANES = 8
VMEM_LIMIT_BYTES = 56 * 1024 * 1024

ATT_HEADS = 8
ATT_VDIM = 128
ATT_QKDIM = 64
HYENA_ORDER = 2
FILTER_BANDS = 16
DECAY_FAST = 0.3
DECAY_SLOW = 1.5
DECAY_TARGET = 1e-2
PEER_HEADS = 8
PEER_NKEYS = 128
PEER_TOPK = 16
PEER_HALF = PEER_NKEYS * PEER_NKEYS // 2
PEER_PAIRS = PEER_HEADS * PEER_TOPK
PEER_WINDOW = 80


def _cparams(sem, vmem=VMEM_LIMIT_BYTES):
    return pltpu.CompilerParams(dimension_semantics=sem, vmem_limit_bytes=vmem)


def _rmsnorm_kernel(x_ref, g_ref, *o_refs):
    x = x_ref[...]
    ms = jnp.mean(x * x, axis=-1, keepdims=True)
    y = x * lax.rsqrt(ms + RMS_EPS) * g_ref[...]
    for o_ref in o_refs:
        o_ref[...] = y.astype(o_ref.dtype)


def _rmsnorm(x, g, out_dtypes, tm=512):
    t, d = x.shape
    tm = min(tm, t)
    spec = pl.BlockSpec((tm, d), lambda i: (i, 0))
    return pl.pallas_call(
        _rmsnorm_kernel,
        grid=(t // tm,),
        in_specs=[spec, pl.BlockSpec((1, d), lambda i: (0, 0))],
        out_specs=[spec for _ in out_dtypes],
        out_shape=[jax.ShapeDtypeStruct((t, d), dt) for dt in out_dtypes],
        compiler_params=_cparams(("parallel",)),
        name="rmsnorm",
    )(x, g.reshape(1, d).astype(F32))


def _group_rms_scale(x, gain):
    lane = lax.broadcasted_iota(I32, x.shape, 1)
    lo = lane < ATT_QKDIM
    x2 = x * x
    s_lo = jnp.sum(jnp.where(lo, x2, 0.0), axis=-1, keepdims=True)
    s_hi = jnp.sum(jnp.where(lo, 0.0, x2), axis=-1, keepdims=True)
    ms = jnp.where(lo, s_lo, s_hi) * (1.0 / ATT_QKDIM)
    return x * lax.rsqrt(ms + RMS_EPS) * gain


def _mm_kernel(a_ref, b_ref, *rest, mode):
    acc = jnp.dot(a_ref[...], b_ref[...], preferred_element_type=F32)
    if mode == "plain":
        (o_ref,) = rest
        o_ref[...] = acc.astype(o_ref.dtype)
    elif mode == "residual":
        r_ref, o_ref = rest
        o_ref[...] = (acc + r_ref[...]).astype(o_ref.dtype)
    elif mode == "qknorm":
        g_ref, o_ref = rest
        for c in range(acc.shape[1] // LANES):
            sl = slice(c * LANES, (c + 1) * LANES)
            o_ref[:, sl] = _group_rms_scale(acc[:, sl], g_ref[:, sl]).astype(o_ref.dtype)
    else:
        raise ValueError(mode)


def _matmul(a, b, n_cols, col_off, out_dtype, mode="plain", extra=None, tm=1024, tn=1024):
    m, k = a.shape
    tm = min(tm, m)
    tn = min(tn, n_cols)
    assert col_off % tn == 0 and n_cols % tn == 0 and m % tm == 0
    off = col_off // tn
    in_specs = [pl.BlockSpec((tm, k), lambda i, j: (i, 0)),
                pl.BlockSpec((k, tn), lambda i, j: (0, j + off))]
    args = [a, b]
    if mode == "residual":
        in_specs.append(pl.BlockSpec((tm, tn), lambda i, j: (i, j)))
        args.append(extra)
    elif mode == "qknorm":
        in_specs.append(pl.BlockSpec((1, tn), lambda i, j: (0, j)))
        args.append(extra)
    return pl.pallas_call(
        functools.partial(_mm_kernel, mode=mode),
        grid=(m // tm, n_cols // tn),
        in_specs=in_specs,
        out_specs=pl.BlockSpec((tm, tn), lambda i, j: (i, j)),
        out_shape=jax.ShapeDtypeStruct((m, n_cols), out_dtype),
        compiler_params=_cparams(("parallel", "arbitrary")),
        name="matmul_" + mode,
    )(*args)


def _mm_nt_kernel(w_ref, a_ref, o_ref):
    o_ref[...] = lax.dot_general(w_ref[...], a_ref[...], (((1,), (1,)), ((), ())),
                                 preferred_element_type=F32).astype(o_ref.dtype)


def _matmul_nt(wt, a, out_dtype, tm=1024):
    n, k = wt.shape
    m = a.shape[0]
    tm = min(tm, m)
    return pl.pallas_call(
        _mm_nt_kernel,
        grid=(m // tm,),
        in_specs=[pl.BlockSpec((n, k), lambda i: (0, 0)), pl.BlockSpec((tm, k), lambda i: (i, 0))],
        out_specs=pl.BlockSpec((n, tm), lambda i: (0, i)),
        out_shape=jax.ShapeDtypeStruct((n, m), out_dtype),
        compiler_params=_cparams(("parallel",)),
        name="matmul_nt",
    )(wt, a)


def _alibi_columns(slopes, tq, tk):
    return [_alibi_side(slopes, tq, True), _alibi_side(slopes, tk, False)]


def _alibi_side(slopes, n, query_side):
    pos = jnp.arange(n, dtype=F32)
    val = (slopes.astype(F32) * LOG2E)[:, None] * pos[None, :]

    def pieces(x):
        p1 = x.astype(BF16)
        r1 = x - p1.astype(F32)
        p2 = r1.astype(BF16)
        p3 = (r1 - p2.astype(F32)).astype(BF16)
        return [p1, p2, p3]

    ones = [jnp.ones_like(val, BF16)] * 3
    six = jnp.stack(pieces(-val) + ones if query_side else ones + pieces(val), axis=-1)
    pad = jnp.zeros(val.shape + (ATT_QKDIM - 6,), BF16)
    return jnp.concatenate([six, pad, six, pad], axis=-1)


def _attn_kernel(slope_ref, lam_ref, q_ref, k_ref, vt_ref, aq_ref, ak_ref, g_ref, o_ref,
                 m_ref, l_ref, acc_ref, *, tq, tk, nk, hb, out_scale):
    hg = pl.program_id(1)
    i = pl.program_id(2)
    j = pl.program_id(3)

    @pl.when(j == 0)
    def _():
        m_ref[...] = jnp.full(m_ref.shape, -jnp.inf, F32)
        l_ref[...] = jnp.zeros(l_ref.shape, F32)
        acc_ref[...] = jnp.zeros(acc_ref.shape, F32)

    q_first = lax.broadcasted_iota(I32, (tq, LANES), 1) < ATT_QKDIM
    k_first = lax.broadcasted_iota(I32, (tk, LANES), 1) < ATT_QKDIM
    nt = (((1,), (1,)), ((), ()))

    def update(hh, scores, shift):
        vt = vt_ref[hh * LANES:(hh + 1) * LANES, :]
        for c, s in enumerate(scores):
            m_old = m_ref[hh, c]
            m_new = jnp.maximum(m_old, jnp.max(s, axis=0, keepdims=True) + shift)
            alpha = jnp.exp2(m_old - m_new)
            p = jnp.exp2(s - (m_new - shift))
            l_ref[hh, c] = alpha * l_ref[hh, c] + jnp.sum(p, axis=0, keepdims=True)
            acc_ref[hh, c] = alpha * acc_ref[hh, c] + jnp.dot(vt, p.astype(BF16),
                                                              preferred_element_type=F32)
            m_ref[hh, c] = m_new

    keys_before = i * tq >= (j + 1) * tk
    keys_after = (i + 1) * tq <= j * tk
    off_diagonal = jnp.logical_or(keys_before, keys_after)

    @pl.when(off_diagonal)
    def _():
        sign = jnp.where(keys_after, -1.0, 1.0).astype(BF16)
        gap = jnp.abs(i * tq - j * tk).astype(F32)
        for hh in range(hb):
            sl = slice(hh * LANES, (hh + 1) * LANES)
            q = q_ref[:, sl]
            k = k_ref[:, sl]
            aq = aq_ref[hh]
            ak = ak_ref[hh] * sign
            s0 = lax.dot_general(jnp.where(k_first, k, ak), jnp.where(q_first, q, aq), nt,
                                 preferred_element_type=F32)
            s1 = lax.dot_general(jnp.where(k_first, ak, k), jnp.where(q_first, aq, q), nt,
                                 preferred_element_type=F32)
            update(hh, (s0, s1), -(slope_ref[hg * hb + hh] * LOG2E) * gap)

    @pl.when(jnp.logical_not(off_diagonal))
    def _():
        kpos = lax.broadcasted_iota(I32, (tk, tq), 0) + j * tk
        qpos = lax.broadcasted_iota(I32, (tk, tq), 1) + i * tq
        dist = jnp.abs(kpos - qpos).astype(F32)
        for hh in range(hb):
            sl = slice(hh * LANES, (hh + 1) * LANES)
            q = q_ref[:, sl]
            k = k_ref[:, sl]
            zero = jnp.zeros_like(q)
            bias = dist * (-(slope_ref[hg * hb + hh] * LOG2E))
            s0 = lax.dot_general(k, jnp.where(q_first, q, zero), nt, preferred_element_type=F32) + bias
            s1 = lax.dot_general(k, jnp.where(q_first, zero, q), nt, preferred_element_type=F32) + bias
            update(hh, (s0, s1), 0.0)

    @pl.when(j == nk - 1)
    def _():
        for hh in range(hb):
            o = acc_ref[hh, 0] / l_ref[hh, 0] - lam_ref[0] * (acc_ref[hh, 1] / l_ref[hh, 1])
            ms = jnp.mean(o * o, axis=0, keepdims=True)
            y = o * lax.rsqrt(ms + RMS_EPS) * (g_ref[...] * out_scale)
            o_ref[:, hh * LANES:(hh + 1) * LANES] = y.T.astype(o_ref.dtype)


def _attention(qk, vt, row_off, slopes, lam, sub_g, batch, seq, out_scale, tq=512, tk=1024, hb=2):
    tq = min(tq, seq)
    tk = min(tk, seq // 4)
    nq, nk = seq // tq, seq // tk
    assert row_off % tq == 0 and row_off % tk == 0 and ATT_HEADS % hb == 0
    oq, ok = row_off // tq, row_off // tk
    ng = ATT_HEADS // hb
    aq, ak = _alibi_columns(slopes, tq, tk)
    kern = functools.partial(_attn_kernel, tq=tq, tk=tk, nk=nk, hb=hb, out_scale=out_scale)
    smem = pl.BlockSpec(memory_space=pltpu.SMEM)
    return pl.pallas_call(
        kern,
        grid=(batch, ng, nq, nk),
        in_specs=[smem, smem,
                  pl.BlockSpec((tq, hb * LANES), lambda b, h, i, j: (oq + b * nq + i, h)),
                  pl.BlockSpec((tk, hb * LANES), lambda b, h, i, j: (ok + b * nk + j, ng + h)),
                  pl.BlockSpec((hb * LANES, tk), lambda b, h, i, j: (h, ok + b * nk + j)),
                  pl.BlockSpec((hb, tq, LANES), lambda b, h, i, j: (h, 0, 0)),
                  pl.BlockSpec((hb, tk, LANES), lambda b, h, i, j: (h, 0, 0)),
                  pl.BlockSpec((LANES, 1), lambda b, h, i, j: (0, 0))],
        out_specs=pl.BlockSpec((tq, hb * LANES), lambda b, h, i, j: (b * nq + i, h)),
        out_shape=jax.ShapeDtypeStruct((batch * seq, ATT_HEADS * ATT_VDIM), BF16),
        scratch_shapes=[pltpu.VMEM((hb, 2, 1, tq), F32), pltpu.VMEM((hb, 2, 1, tq), F32),
                        pltpu.VMEM((hb, 2, LANES, tq), F32)],
        compiler_params=_cparams(("parallel", "parallel", "parallel", "arbitrary")),
        name="diff_attention",
    )(slopes, lam, qk, qk, vt, aq, ak, sub_g.reshape(LANES, 1).astype(F32))


def _shortconv_kernel(z_ref, w_ref, b_ref, o_ref):
    z = z_ref[...]
    n = z.shape[0]
    row = lax.broadcasted_iota(I32, z.shape, 0)
    prev = jnp.where(row == 0, 0.0, pltpu.roll(z, 1, axis=0))
    nxt = jnp.where(row == n - 1, 0.0, pltpu.roll(z, n - 1, axis=0))
    w = w_ref[...]
    o_ref[...] = prev * w[0:1] + z * w[1:2] + nxt * w[2:3] + b_ref[...]


def _shortconv(zh, row_off, conv_w, conv_b, batch, seq, cb=256):
    c3 = zh.shape[1]
    c = c3 // 3
    ncb = c // cb
    assert row_off % seq == 0
    ob = row_off // seq
    out = pl.pallas_call(
        _shortconv_kernel,
        grid=(batch, 3, ncb),
        in_specs=[pl.BlockSpec((seq, cb), lambda b, p, j: (ob + b, p * ncb + j)),
                  pl.BlockSpec((3, cb), lambda b, p, j: (0, p * ncb + j)),
                  pl.BlockSpec((1, cb), lambda b, p, j: (0, p * ncb + j))],
        out_specs=pl.BlockSpec((None, None, seq, cb), lambda b, p, j: (p, b, 0, j)),
        out_shape=jax.ShapeDtypeStruct((3, batch, seq, c), F32),
        compiler_params=_cparams(("parallel", "parallel", "parallel")),
        name="hyena_shortconv",
    )(zh, conv_w.astype(F32), conv_b.reshape(1, c3).astype(F32))
    return out


def _filter_kernel(z_ref, w1_ref, b1_ref, w2_ref, b2_ref, w3_ref, b3_ref, fr_ref, w4_ref,
                   t_ref, d_ref, o_ref, *, tl):
    g = pl.program_id(0)
    i = pl.program_id(1)
    hi = lax.Precision.HIGHEST
    fr = fr_ref[...]
    h = jnp.sin(fr * (jnp.dot(z_ref[...], w1_ref[...], precision=hi, preferred_element_type=F32) + b1_ref[...]))
    h = jnp.sin(fr * (jnp.dot(h, w2_ref[...], precision=hi, preferred_element_type=F32) + b2_ref[...]))
    h = jnp.sin(fr * (jnp.dot(h, w3_ref[...], precision=hi, preferred_element_type=F32) + b3_ref[...]))
    f = jnp.dot(h, w4_ref[...], precision=hi, preferred_element_type=F32)
    f = f * jnp.exp(-t_ref[...] * d_ref[...])
    row = lax.broadcasted_iota(I32, f.shape, 0) + i * tl
    drop = jnp.logical_and(row == 0, g % 2 == 1)
    o_ref[...] = jnp.where(drop, 0.0, f)


def _hyena_filter_signals(seq, w1, b1, w2, b2, w3, b3, w4, freq, n_ch, tl=512):
    t = jnp.linspace(0.0, 1.0, seq, dtype=F32)[:, None]
    w = 2.0 * math.pi * jnp.arange(seq, dtype=F32)[:, None] / seq
    f = jnp.linspace(1e-4, FILTER_BANDS - 1, FILTER_BANDS, dtype=F32)[None, :]
    z = jnp.concatenate([t, jnp.cos(f * w), -jnp.sin(f * w)], axis=-1)
    deltas = jnp.abs(jnp.linspace(math.log(DECAY_FAST) / DECAY_TARGET,
                                  math.log(DECAY_SLOW) / DECAY_TARGET, n_ch, dtype=F32))[None, :]
    hid = w1.shape[1]
    emb = LANES
    z = jnp.pad(z, ((0, 0), (0, emb - z.shape[1])))
    w1 = jnp.pad(w1.astype(F32), ((0, emb - w1.shape[0]), (0, 0)))
    tl = min(tl, seq)
    full = lambda shape: pl.BlockSpec(shape, lambda g, i: tuple(0 for _ in shape))
    return pl.pallas_call(
        functools.partial(_filter_kernel, tl=tl),
        grid=(2 * HYENA_ORDER, seq // tl),
        in_specs=[pl.BlockSpec((tl, emb), lambda g, i: (i, 0)),
                  full((emb, hid)), full((1, hid)), full((hid, hid)), full((1, hid)),
                  full((hid, hid)), full((1, hid)), full((1, hid)),
                  pl.BlockSpec((hid, n_ch), lambda g, i: (0, g)),
                  pl.BlockSpec((tl, 1), lambda g, i: (i, 0)),
                  full((1, n_ch))],
        out_specs=pl.BlockSpec((None, tl, n_ch), lambda g, i: (g, i, 0)),
        out_shape=jax.ShapeDtypeStruct((2 * HYENA_ORDER, seq, n_ch), F32),
        compiler_params=_cparams(("parallel", "parallel")),
        name="hyena_filter_mlp",
    )(z, w1.astype(F32), b1.reshape(1, hid).astype(F32), w2.astype(F32), b2.reshape(1, hid).astype(F32),
      w3.astype(F32), b3.reshape(1, hid).astype(F32), freq.reshape(1, hid).astype(F32), w4.astype(F32),
      t, deltas)


def _dft_tables(r):
    n = r * r
    k2 = jnp.arange(r, dtype=I32)
    n2 = jnp.arange(r // 2, dtype=I32)
    ang1 = (2.0 * math.pi / r) * ((k2[:, None] * n2[None, :]) % r).astype(F32)
    f1 = jnp.concatenate([jnp.cos(ang1), -jnp.sin(ang1)], axis=0)
    k1 = jnp.arange(r, dtype=I32)
    n1 = jnp.arange(r, dtype=I32)
    kk = r * k1[None, :, None] + k2[:, None, None]
    ang2 = (2.0 * math.pi / n) * ((kk * n1[None, None, :]) % n).astype(F32)
    mr, mi_ = jnp.cos(ang2), -jnp.sin(ang2)
    mf = jnp.concatenate([jnp.concatenate([mr, -mi_], axis=2),
                          jnp.concatenate([mi_, mr], axis=2)], axis=1)
    minv = jnp.swapaxes(mf, 1, 2)
    g3 = jnp.concatenate([jnp.cos(ang1.T), -jnp.sin(ang1.T)], axis=1) * (1.0 / n)
    return f1.astype(BF16), mf.astype(BF16), minv.astype(BF16), g3.astype(BF16)


N1_BLOCK = SUBLANES


def _pack_complex(re, im):
    rb = lax.bitcast_convert_type(re.astype(BF16).astype(F32), U32)
    ib = lax.bitcast_convert_type(im.astype(BF16).astype(F32), U32)
    return (rb >> 16) | ib


def _unpack_complex_rows(w):
    re, im = _unpack(w)
    return jnp.concatenate([re, im], axis=0).astype(BF16)


def _fft1_kernel(f_ref, x_ref, o_ref, *, r):
    f = f_ref[...]
    for m in range(N1_BLOCK):
        res = jnp.dot(f, x_ref[:, m, :].astype(BF16), preferred_element_type=F32)
        o_ref[:, m, :] = _pack_complex(res[:r], res[r:])


def _fft_stage1(x4, part, f1, r, n_ch):
    nb = x4.shape[1]
    xv = x4.reshape(x4.shape[0], nb, r // 2, r, n_ch)
    return pl.pallas_call(
        functools.partial(_fft1_kernel, r=r),
        grid=(nb, r // N1_BLOCK),
        in_specs=[pl.BlockSpec((2 * r, r // 2), lambda b, j: (0, 0)),
                  pl.BlockSpec((None, None, r // 2, N1_BLOCK, n_ch), lambda b, j: (part, b, 0, j, 0))],
        out_specs=pl.BlockSpec((None, r, N1_BLOCK, n_ch), lambda b, j: (b, 0, j, 0)),
        out_shape=jax.ShapeDtypeStruct((nb, r, r, n_ch), U32),
        compiler_params=_cparams(("parallel", "parallel")),
        name="hyena_dft_stage1",
    )(f1, xv)


def _filter_spec_kernel(mf_ref, bf_ref, bb_ref, o_ref, *, r):
    m = mf_ref[...]
    xf = jnp.dot(m, _unpack_complex_rows(bf_ref[...]), preferred_element_type=F32)
    xb = jnp.dot(m, _unpack_complex_rows(bb_ref[...]), preferred_element_type=F32)
    o_ref[...] = _pack_complex(xf[:r] + xb[:r], xf[r:] - xb[r:])


def _filter_spectrum(b1, mf, r, n_ch):
    blk = lambda sel: pl.BlockSpec((None, None, r, n_ch), lambda o, k: (2 * o + sel, k, 0, 0))
    return pl.pallas_call(
        functools.partial(_filter_spec_kernel, r=r),
        grid=(HYENA_ORDER, r),
        in_specs=[pl.BlockSpec((None, 2 * r, 2 * r), lambda o, k: (k, 0, 0)), blk(0), blk(1)],
        out_specs=pl.BlockSpec((None, None, r, n_ch), lambda o, k: (o, k, 0, 0)),
        out_shape=jax.ShapeDtypeStruct((HYENA_ORDER, r, r, n_ch), U32),
        compiler_params=_cparams(("parallel", "parallel")),
        name="hyena_filter_spectrum",
    )(mf, b1, b1)


def _fft2_kernel(mf_ref, mi_ref, b_ref, h_ref, o_ref, *, r):
    x = jnp.dot(mf_ref[...], _unpack_complex_rows(b_ref[...]), preferred_element_type=F32)
    xr, xi = x[:r], x[r:]
    hr, hi = _unpack(h_ref[...])
    y = jnp.concatenate([xr * hr - xi * hi, xr * hi + xi * hr], axis=0).astype(BF16)
    c = jnp.dot(mi_ref[...], y, preferred_element_type=F32)
    o_ref[...] = _pack_complex(c[:r], c[r:])


def _fft_stage2(b1, h, order, mf, minv, r, n_ch):
    nb = b1.shape[0]
    blk = pl.BlockSpec((None, None, r, n_ch), lambda b, k: (b, k, 0, 0))
    return pl.pallas_call(
        functools.partial(_fft2_kernel, r=r),
        grid=(nb, r),
        in_specs=[pl.BlockSpec((None, 2 * r, 2 * r), lambda b, k: (k, 0, 0)),
                  pl.BlockSpec((None, 2 * r, 2 * r), lambda b, k: (k, 0, 0)),
                  blk,
                  pl.BlockSpec((None, None, r, n_ch), lambda b, k: (order, k, 0, 0))],
        out_specs=blk,
        out_shape=jax.ShapeDtypeStruct((nb, r, r, n_ch), U32),
        compiler_params=_cparams(("parallel", "parallel")),
        name="hyena_dft_stage2",
    )(mf, minv, b1, h)


def _fft3_kernel(g_ref, c_ref, gate_ref, s_ref, bias_ref, ng_ref, o_ref, *, final):
    g = g_ref[...]
    for m in range(N1_BLOCK):
        y = jnp.dot(g, _unpack_complex_rows(c_ref[:, m, :]), preferred_element_type=F32)
        s_new = gate_ref[:, m, :] * (y + s_ref[:, m, :] * bias_ref[...])
        if final:
            ms = jnp.mean(s_new * s_new, axis=-1, keepdims=True)
            s_new = s_new * lax.rsqrt(ms + RMS_EPS) * ng_ref[...]
        o_ref[:, m, :] = s_new


def _fft_stage3(c2, g3, z4, gate_part, s4, s_part, bias, norm_g, final, r, n_ch):
    nb = c2.shape[0]
    zv = z4.reshape(z4.shape[0], nb, r // 2, r, n_ch)
    sv = s4.reshape(s4.shape[0], nb, r // 2, r, n_ch)
    dspec = lambda part: pl.BlockSpec((None, None, r // 2, N1_BLOCK, n_ch), lambda b, j: (part, b, 0, j, 0))
    return pl.pallas_call(
        functools.partial(_fft3_kernel, final=final),
        grid=(nb, r // N1_BLOCK),
        in_specs=[pl.BlockSpec((r // 2, 2 * r), lambda b, j: (0, 0)),
                  pl.BlockSpec((None, r, N1_BLOCK, n_ch), lambda b, j: (b, 0, j, 0)),
                  dspec(gate_part), dspec(s_part),
                  pl.BlockSpec((1, n_ch), lambda b, j: (0, 0)),
                  pl.BlockSpec((1, n_ch), lambda b, j: (0, 0))],
        out_specs=pl.BlockSpec((None, r // 2, N1_BLOCK, n_ch), lambda b, j: (b, 0, j, 0)),
        out_shape=jax.ShapeDtypeStruct((nb, r // 2, r, n_ch), F32),
        compiler_params=_cparams(("parallel", "parallel")),
        name="hyena_dft_stage3",
    )(g3, c2, zv, sv, bias.reshape(1, n_ch).astype(F32), norm_g.reshape(1, n_ch).astype(F32))


def _hyena(zh, row_off, batch, seq, conv_w, conv_b, filt, fft_bias, out_g):
    n_ch = zh.shape[1] // 3
    r = int(round(math.sqrt(2 * seq)))
    assert r * r == 2 * seq and r % 16 == 0
    f1, mf, minv, g3 = _dft_tables(r)
    sig = _hyena_filter_signals(seq, *filt, n_ch=n_ch)
    hb1 = _fft_stage1(sig[None], 0, f1, r, n_ch)
    h = _filter_spectrum(hb1, mf, r, n_ch)
    z4 = _shortconv(zh, row_off, conv_w, conv_b, batch, seq)
    s4, s_part = z4, 2
    for o in range(HYENA_ORDER):
        b1 = _fft_stage1(s4, s_part, f1, r, n_ch)
        c2 = _fft_stage2(b1, h, o, mf, minv, r, n_ch)
        final = o == HYENA_ORDER - 1
        s = _fft_stage3(c2, g3, z4, o, s4, s_part, fft_bias[o], out_g, final, r, n_ch)
        s4, s_part = s.reshape(1, batch, seq, n_ch), 0
    return s4.reshape(batch * seq, n_ch)


def _extract_top(s, key, count):
    vals, keys = [], []
    for _ in range(count):
        m = jnp.max(s, axis=0, keepdims=True)
        kmin = jnp.min(jnp.where(s == m, key, jnp.inf), axis=0, keepdims=True)
        s = jnp.where(key == kmin, -jnp.inf, s)
        vals.append(m)
        keys.append(kmin)
    return vals, keys


def _peer_topk_kernel(q_ref, keys_ref, eid_ref, gate_ref):
    t = q_ref.shape[0]
    nk = PEER_NKEYS
    q = q_ref[...]
    row_key = lax.broadcasted_iota(I32, (nk, t), 0).astype(F32)
    tops = []
    for c in range(2):
        s = lax.dot_general(keys_ref[c], q[:, c * nk:(c + 1) * nk], (((1,), (1,)), ((), ())),
                            preferred_element_type=F32)
        tops.append(_extract_top(s, row_key, PEER_TOPK))
    (v1, i1), (v2, i2) = tops
    rows16 = lax.broadcasted_iota(I32, (PEER_TOPK, t), 0)
    v2a = jnp.zeros((PEER_TOPK, t), F32)
    i2a = jnp.zeros((PEER_TOPK, t), F32)
    for j in range(PEER_TOPK):
        v2a = jnp.where(rows16 == j, v2[j], v2a)
        i2a = jnp.where(rows16 == j, i2[j], i2a)
    n_exp = float(nk * nk)
    half = PEER_TOPK // 2
    rows8 = lax.broadcasted_iota(I32, (half, t), 0)
    pos8 = rows8.astype(F32)
    v2h, i2h = v2a[:half], i2a[:half]
    cand = [v1[0] + v2a]
    ckey = [rows16.astype(F32) * n_exp + (i1[0] * float(nk) + i2a)]
    for i in range(1, half):
        cand.append(jnp.where(rows8 < PEER_TOPK // (i + 1), v1[i] + v2h, -jnp.inf))
        ckey.append((pos8 + float(i * PEER_TOPK)) * n_exp + (i1[i] * float(nk) + i2h))
    v1t = jnp.zeros((half, t), F32)
    i1t = jnp.zeros((half, t), F32)
    for r in range(half):
        v1t = jnp.where(rows8 == r, v1[half + r], v1t)
        i1t = jnp.where(rows8 == r, i1[half + r], i1t)
    cand.append(v1t + v2[0])
    ckey.append((pos8 + float(half)) * (PEER_TOPK * n_exp) + (i1t * float(nk) + i2[0]))
    tv, tk_ = _extract_top(jnp.concatenate(cand, axis=0), jnp.concatenate(ckey, axis=0), PEER_TOPK)
    denom = jnp.zeros((1, t), F32)
    es = []
    for k in range(PEER_TOPK):
        e = jnp.exp(tv[k] - tv[0])
        es.append(e)
        denom = denom + e
    eid = jnp.zeros((PEER_TOPK, t), F32)
    gate = jnp.zeros((PEER_TOPK, t), F32)
    for k in range(PEER_TOPK):
        pos = jnp.floor(tk_[k] * (1.0 / n_exp))
        eid = jnp.where(rows16 == k, tk_[k] - pos * n_exp, eid)
        gate = jnp.where(rows16 == k, es[k] / denom, gate)
    eid_ref[...] = eid.astype(I32)
    gate_ref[...] = gate


def _peer_topk(q, keys, tm=256):
    t = q.shape[0]
    tm = min(tm, t)
    out_spec = pl.BlockSpec((PEER_TOPK, tm), lambda i, h: (h, i))
    return pl.pallas_call(
        _peer_topk_kernel,
        grid=(t // tm, PEER_HEADS),
        in_specs=[pl.BlockSpec((tm, 2 * PEER_NKEYS), lambda i, h: (i, h)),
                  pl.BlockSpec((None, 2, PEER_NKEYS, PEER_NKEYS), lambda i, h: (h, 0, 0, 0))],
        out_specs=[out_spec, out_spec],
        out_shape=[jax.ShapeDtypeStruct((PEER_HEADS * PEER_TOPK, t), I32),
                   jax.ShapeDtypeStruct((PEER_HEADS * PEER_TOPK, t), F32)],
        compiler_params=_cparams(("parallel", "parallel")),
        name="peer_topk",
    )(q, keys)


def _pack_table(tab):
    e, d = tab.shape
    assert d == 2 * SUBLANES * LANES
    bits = lax.bitcast_convert_type(tab.astype(BF16), jnp.uint16).astype(U32)
    packed = bits[:, :d // 2] | (bits[:, d // 2:] << 16)
    return packed.reshape(e * SUBLANES, LANES)


def _unpack(w):
    lo = lax.bitcast_convert_type(w << 16, F32)
    hi = lax.bitcast_convert_type(w & jnp.uint32(0xFFFF0000), F32)
    return lo, hi


_BITREV8 = (0, 4, 2, 6, 1, 5, 3, 7)


def _sublane_fold8(parts):
    sub = lax.broadcasted_iota(I32, (2 * SUBLANES, LANES), 0) // 2

    def rolled(a, shift):
        return pltpu.bitcast(pltpu.roll(pltpu.bitcast(a, U32), shift, axis=0), BF16)

    lvl = [parts[_BITREV8[r]] for r in range(8)]
    for shift, mask in ((4, sub < 4), (2, (sub % 4) < 2), (1, (sub % 2) < 1)):
        nxt = []
        for a, b in zip(lvl[0::2], lvl[1::2]):
            nxt.append(jnp.where(mask, a + rolled(a, SUBLANES - shift), b + rolled(b, shift)))
        lvl = nxt
    return lvl[0]


def _pair_ranges(half):
    if half == 0:
        return (0, PEER_WINDOW), (PEER_WINDOW, PEER_PAIRS)
    return (PEER_PAIRS - PEER_WINDOW, PEER_PAIRS), (0, PEER_PAIRS - PEER_WINDOW)


def _overflow(n_low, half):
    return n_low > PEER_WINDOW if half == 0 else n_low < PEER_PAIRS - PEER_WINDOW


def _table_rows(tab_ref, off):
    return _unpack(tab_ref[pl.ds(pl.multiple_of(off, SUBLANES), SUBLANES), :])


def _peer_u_kernel(off_ref, nlow_ref, tab_ref, x_ref, o_ref, extra_ref, *, tm, half):
    lane = lax.broadcasted_iota(I32, (SUBLANES, LANES), 1)
    sub = lax.broadcasted_iota(I32, (SUBLANES, LANES), 0)
    lane_grp = lax.shift_right_logical(lane, 3)
    diag = sub == (lane & (SUBLANES - 1))
    main, rest = _pair_ranges(half)

    def folded(t, p0, p1):
        xb = pltpu.bitcast(x_ref[t], BF16)
        out = []
        for g in range(p0 // SUBLANES, p1 // SUBLANES):
            parts = []
            for r in range(SUBLANES):
                off = pl.multiple_of(off_ref[t, g * SUBLANES + r], SUBLANES)
                parts.append(pltpu.bitcast(tab_ref[pl.ds(off, SUBLANES), :], BF16) * xb)
            out.append(pltpu.bitcast(_sublane_fold8(parts), U32))
        return out

    def lane_sums(folds, p0):
        mat = jnp.zeros((SUBLANES, LANES), F32)
        for i, f in enumerate(folds):
            lo, hi = _unpack(f)
            mat = jnp.where(lane_grp == p0 // SUBLANES + i, jnp.sum(lo + hi, axis=-1, keepdims=True), mat)
        return jnp.sum(jnp.where(diag, mat, 0.0), axis=0, keepdims=True)

    def finish(t, folds):
        o_ref[pl.ds(t, 1), :] = lane_sums(folds, main[0]) + extra_ref[pl.ds(t, 1), :]

    def token(t, prev):
        cur = folded(t, *main)
        finish(jnp.maximum(t - 1, 0), prev)
        extra_ref[pl.ds(t, 1), :] = jnp.zeros((1, LANES), F32)

        @pl.when(_overflow(nlow_ref[0, t], half))
        def _():
            extra_ref[pl.ds(t, 1), :] = lane_sums(folded(t, *rest), rest[0])

        return tuple(cur)

    extra_ref[pl.ds(0, 1), :] = jnp.zeros((1, LANES), F32)
    zeros = tuple(jnp.zeros((SUBLANES, LANES), U32) for _ in range((main[1] - main[0]) // SUBLANES))
    last = lax.fori_loop(0, tm, token, zeros)
    finish(tm - 1, last)


def _peer_u(off, nlow3, tab, x4, half, tm=128):
    t = off.shape[0]
    rows = PEER_HALF * SUBLANES
    return pl.pallas_call(
        functools.partial(_peer_u_kernel, tm=tm, half=half),
        grid=(t // tm,),
        in_specs=[pl.BlockSpec((tm, LANES), lambda i: (i, 0), memory_space=pltpu.SMEM),
                  pl.BlockSpec((None, 1, tm), lambda i: (i, 0, 0), memory_space=pltpu.SMEM),
                  pl.BlockSpec((rows, LANES), lambda i: (half, 0), pipeline_mode=pl.Buffered(1)),
                  pl.BlockSpec((tm, SUBLANES, LANES), lambda i: (i, 0, 0))],
        out_specs=pl.BlockSpec((tm, LANES), lambda i: (i, 0)),
        out_shape=jax.ShapeDtypeStruct((t, LANES), F32),
        scratch_shapes=[pltpu.VMEM((tm, LANES), F32)],
        compiler_params=_cparams(("arbitrary",)),
        name="peer_expert_scores",
    )(off, nlow3, tab, x4)


def _peer_coef_kernel(a0_ref, a1_ref, eid_ref, gate_ref, c0_ref, c1_ref):
    low = eid_ref[...] < PEER_HALF
    a = jnp.where(low, a0_ref[...], a1_ref[...])
    coef = gate_ref[...] * (0.5 * a * (1.0 + lax.erf(a * (1.0 / math.sqrt(2.0)))))
    c0_ref[...] = jnp.where(low, coef, 0.0)
    c1_ref[...] = jnp.where(low, 0.0, coef)


def _peer_coef(a0, a1, eid, gate, tm=1024):
    t = eid.shape[0]
    tm = min(tm, t)
    spec = pl.BlockSpec((tm, LANES), lambda i: (i, 0))
    return pl.pallas_call(
        _peer_coef_kernel,
        grid=(t // tm,),
        in_specs=[spec, spec, spec, spec],
        out_specs=[spec, spec],
        out_shape=[jax.ShapeDtypeStruct((t, LANES), F32)] * 2,
        compiler_params=_cparams(("parallel",)),
        name="peer_coef",
    )(a0, a1, eid, gate)


def _peer_v_kernel(off_ref, nlow_ref, coef_ref, tab_ref, o_ref, *, tm, half):
    n_acc = 4
    main, rest = _pair_ranges(half)

    def weighted(t, p0, p1):
        acc_lo = [jnp.zeros((SUBLANES, LANES), F32) for _ in range(n_acc)]
        acc_hi = [jnp.zeros((SUBLANES, LANES), F32) for _ in range(n_acc)]
        for p in range(p0, p1):
            c = coef_ref[t, p]
            lo, hi = _table_rows(tab_ref, off_ref[t, p])
            acc_lo[p % n_acc] = acc_lo[p % n_acc] + c * lo
            acc_hi[p % n_acc] = acc_hi[p % n_acc] + c * hi
        return ((acc_lo[0] + acc_lo[1]) + (acc_lo[2] + acc_lo[3]),
                (acc_hi[0] + acc_hi[1]) + (acc_hi[2] + acc_hi[3]))

    def token(t, carry):
        lo, hi = weighted(t, *main)
        o_ref[t, 0] = lo
        o_ref[t, 1] = hi

        @pl.when(_overflow(nlow_ref[0, t], half))
        def _():
            lo2, hi2 = weighted(t, *rest)
            o_ref[t, 0] = lo + lo2
            o_ref[t, 1] = hi + hi2

        return carry

    lax.fori_loop(0, tm, token, 0)


def _peer_v(off, nlow3, coef, tab, half, tm=128):
    t = off.shape[0]
    rows = PEER_HALF * SUBLANES
    smem = pl.BlockSpec((tm, LANES), lambda i: (i, 0), memory_space=pltpu.SMEM)
    return pl.pallas_call(
        functools.partial(_peer_v_kernel, tm=tm, half=half),
        grid=(t // tm,),
        in_specs=[smem,
                  pl.BlockSpec((None, 1, tm), lambda i: (i, 0, 0), memory_space=pltpu.SMEM),
                  smem,
                  pl.BlockSpec((rows, LANES), lambda i: (half, 0), pipeline_mode=pl.Buffered(1))],
        out_specs=pl.BlockSpec((tm, 2, SUBLANES, LANES), lambda i: (i, 0, 0, 0)),
        out_shape=jax.ShapeDtypeStruct((t, 2, SUBLANES, LANES), F32),
        compiler_params=_cparams(("arbitrary",)),
        name="peer_expert_sum",
    )(off, nlow3, coef, tab)


def _add3_kernel(x_ref, p_ref, q_ref, o_ref):
    o_ref[...] = x_ref[...] + (p_ref[...] + q_ref[...])


def _add3(x, p, q, tm=512):
    t, d = x.shape
    tm = min(tm, t)
    spec = pl.BlockSpec((tm, d), lambda i: (i, 0))
    return pl.pallas_call(
        _add3_kernel,
        grid=(t // tm,),
        in_specs=[spec, spec, spec],
        out_specs=spec,
        out_shape=jax.ShapeDtypeStruct((t, d), F32),
        compiler_params=_cparams(("parallel",)),
        name="peer_residual",
    )(x, p, q)


def _peer(x1, norm2_g, wq_bf, keys_bf, u_packed, v_packed, tm=128):
    t, d = x1.shape
    tm = min(tm, t)
    (xn_bf,) = _rmsnorm(x1, norm2_g, (BF16,))
    q = _matmul(xn_bf, wq_bf, wq_bf.shape[1], 0, BF16)
    eid_t, gate_t = _peer_topk(q, keys_bf)
    eid, gate = eid_t.T, gate_t.T
    high = eid >= PEER_HALF
    order = jnp.argsort(high, axis=-1, stable=True)
    eid = jnp.take_along_axis(eid, order, axis=-1)
    gate = jnp.take_along_axis(gate, order, axis=-1)
    nlow3 = (PEER_PAIRS - jnp.sum(high, axis=-1, dtype=I32)).reshape(t // tm, 1, tm)
    off = (eid & (PEER_HALF - 1)) * SUBLANES
    xw = _pack_table(xn_bf).reshape(t, SUBLANES, LANES)
    a0 = _peer_u(off, nlow3, u_packed, xw, 0, tm)
    a1 = _peer_u(off, nlow3, u_packed, xw, 1, tm)
    c0, c1 = _peer_coef(a0, a1, eid, gate)
    o0 = _peer_v(off, nlow3, c0, v_packed, 0, tm)
    o1 = _peer_v(off, nlow3, c1, v_packed, 1, tm)
    return _add3(x1, o0.reshape(t, d), o1.reshape(t, d))


def kernel(x_prompt, x_sample, norm1_g, w_in, q_norm_g, k_norm_g, lambda_q1, lambda_k1, lambda_q2, lambda_k2, attn_sub_g, conv_w, conv_b, filt_w1, filt_b1, filt_w2, filt_b2, filt_w3, filt_b3, filt_w4, filt_freq, fft_bias, hyena_out_g, w_out, norm2_g, peer_wq, peer_keys, peer_u, peer_v):
    depth = w_in.shape[0]
    d_model = x_prompt.shape[-1]
    att_w = ATT_HEADS * ATT_VDIM
    shapes = [x_prompt.shape[:2], x_sample.shape[:2]]
    x = jnp.concatenate([x_prompt.reshape(-1, d_model), x_sample.reshape(-1, d_model)], axis=0)
    slopes = 2.0 ** (-8.0 * jnp.arange(1, ATT_HEADS + 1, dtype=F32) / ATT_HEADS)

    for l in range(depth):
        lambda_init = 0.8 - 0.6 * math.exp(-0.3 * l)
        lam = (jnp.exp(jnp.sum(lambda_q1[l].astype(F32) * lambda_k1[l].astype(F32)))
               - jnp.exp(jnp.sum(lambda_q2[l].astype(F32) * lambda_k2[l].astype(F32)))
               + lambda_init).reshape(1)
        w_in_bf = w_in[l].astype(BF16)
        q_gain = jnp.tile(q_norm_g[l].astype(F32), 2 * ATT_HEADS) * (ATT_QKDIM ** -0.5 * LOG2E)
        k_gain = jnp.tile(k_norm_g[l].astype(F32), 2 * ATT_HEADS)
        qk_gain = jnp.concatenate([q_gain, k_gain]).reshape(1, 2 * att_w)

        (h_bf,) = _rmsnorm(x, norm1_g[l], (BF16,))
        qk = _matmul(h_bf, w_in_bf, 2 * att_w, 0, BF16, mode="qknorm", extra=qk_gain)
        vt = _matmul_nt(w_in_bf[:, 2 * att_w:3 * att_w].T, h_bf, BF16)
        zh = _matmul(h_bf, w_in_bf, w_in.shape[2] - 3 * att_w, 3 * att_w, F32)

        filt = (filt_w1[l], filt_b1[l], filt_w2[l], filt_b2[l], filt_w3[l], filt_b3[l], filt_w4[l], filt_freq[l])
        mixes, row = [], 0
        for (b, s) in shapes:
            n = b * s
            att = _attention(qk, vt, row, slopes, lam, attn_sub_g[l], b, s, 1.0 - lambda_init)
            hy = _hyena(zh, row, b, s, conv_w[l], conv_b[l], filt, fft_bias[l], hyena_out_g[l])
            mixes.append(jnp.concatenate([att, hy.astype(BF16)], axis=-1))
            row += n
        mix = jnp.concatenate(mixes, axis=0)
        x = _matmul(mix, w_out[l].astype(BF16), d_model, 0, F32, mode="residual", extra=x)

        x = _peer(x, norm2_g[l], peer_wq[l].astype(BF16), peer_keys[l].astype(BF16),
                  _pack_table(peer_u[l]), _pack_table(peer_v[l]))

    n0 = shapes[0][0] * shapes[0][1]
    return (x[:n0].reshape(x_prompt.shape), x[n0:].reshape(x_sample.shape))
```

```python
import functools
import math

import jax
import jax.numpy as jnp
from jax import lax
from jax.experimental import pallas as pl
from jax.experimental.pallas import tpu as pltpu

F32 = jnp.float32
BF16 = jnp.bfloat16
I32 = jnp.int32
U32 = jnp.uint32

RMS_EPS = 1e-6
LOG2E = 1.4426950408889634
LANES = 128
SUBLANES = 8
VMEM_LIMIT_BYTES = 56 * 1024 * 1024

ATT_HEADS = 8
ATT_VDIM = 128
ATT_QKDIM = 64
HYENA_ORDER = 2
FILTER_BANDS = 16
DECAY_FAST = 0.3
DECAY_SLOW = 1.5
DECAY_TARGET = 1e-2
PEER_HEADS = 8
PEER_NKEYS = 128
PEER_TOPK = 16
PEER_HALF = PEER_NKEYS * PEER_NKEYS // 2
PEER_PAIRS = PEER_HEADS * PEER_TOPK
PEER_WINDOW = 80


def _cparams(sem, vmem=VMEM_LIMIT_BYTES):
    return pltpu.CompilerParams(dimension_semantics=sem, vmem_limit_bytes=vmem)


def _rmsnorm_kernel(x_ref, g_ref, *o_refs):
    x = x_ref[...]
    ms = jnp.mean(x * x, axis=-1, keepdims=True)
    y = x * lax.rsqrt(ms + RMS_EPS) * g_ref[...]
    for o_ref in o_refs:
        o_ref[...] = y.astype(o_ref.dtype)


def _rmsnorm(x, g, out_dtypes, tm=512):
    t, d = x.shape
    tm = min(tm, t)
    spec = pl.BlockSpec((tm, d), lambda i: (i, 0))
    return pl.pallas_call(
        _rmsnorm_kernel,
        grid=(t // tm,),
        in_specs=[spec, pl.BlockSpec((1, d), lambda i: (0, 0))],
        out_specs=[spec for _ in out_dtypes],
        out_shape=[jax.ShapeDtypeStruct((t, d), dt) for dt in out_dtypes],
        compiler_params=_cparams(("parallel",)),
        name="rmsnorm",
    )(x, g.reshape(1, d).astype(F32))


def _group_rms_scale(x, gain):
    lane = lax.broadcasted_iota(I32, x.shape, 1)
    lo = lane < ATT_QKDIM
    x2 = x * x
    s_lo = jnp.sum(jnp.where(lo, x2, 0.0), axis=-1, keepdims=True)
    s_hi = jnp.sum(jnp.where(lo, 0.0, x2), axis=-1, keepdims=True)
    ms = jnp.where(lo, s_lo, s_hi) * (1.0 / ATT_QKDIM)
    return x * lax.rsqrt(ms + RMS_EPS) * gain


def _mm_kernel(a_ref, b_ref, *rest, mode):
    acc = jnp.dot(a_ref[...], b_ref[...], preferred_element_type=F32)
    if mode == "plain":
        (o_ref,) = rest
        o_ref[...] = acc.astype(o_ref.dtype)
    elif mode == "residual":
        r_ref, o_ref = rest
        o_ref[...] = (acc + r_ref[...]).astype(o_ref.dtype)
    elif mode == "qknorm":
        g_ref, o_ref = rest
        for c in range(acc.shape[1] // LANES):
            sl = slice(c * LANES, (c + 1) * LANES)
            o_ref[:, sl] = _group_rms_scale(acc[:, sl], g_ref[:, sl]).astype(o_ref.dtype)
    else:
        raise ValueError(mode)


def _matmul(a, b, n_cols, col_off, out_dtype, mode="plain", extra=None, tm=1024, tn=1024):
    m, k = a.shape
    tm = min(tm, m)
    tn = min(tn, n_cols)
    assert col_off % tn == 0 and n_cols % tn == 0 and m % tm == 0
    off = col_off // tn
    in_specs = [pl.BlockSpec((tm, k), lambda i, j: (i, 0)),
                pl.BlockSpec((k, tn), lambda i, j: (0, j + off))]
    args = [a, b]
    if mode == "residual":
        in_specs.append(pl.BlockSpec((tm, tn), lambda i, j: (i, j)))
        args.append(extra)
    elif mode == "qknorm":
        in_specs.append(pl.BlockSpec((1, tn), lambda i, j: (0, j)))
        args.append(extra)
    return pl.pallas_call(
        functools.partial(_mm_kernel, mode=mode),
        grid=(m // tm, n_cols // tn),
        in_specs=in_specs,
        out_specs=pl.BlockSpec((tm, tn), lambda i, j: (i, j)),
        out_shape=jax.ShapeDtypeStruct((m, n_cols), out_dtype),
        compiler_params=_cparams(("parallel", "arbitrary")),
        name="matmul_" + mode,
    )(*args)


def _mm_nt_kernel(w_ref, a_ref, o_ref):
    o_ref[...] = lax.dot_general(w_ref[...], a_ref[...], (((1,), (1,)), ((), ())),
                                 preferred_element_type=F32).astype(o_ref.dtype)


def _matmul_nt(wt, a, out_dtype, tm=1024):
    n, k = wt.shape
    m = a.shape[0]
    tm = min(tm, m)
    return pl.pallas_call(
        _mm_nt_kernel,
        grid=(m // tm,),
        in_specs=[pl.BlockSpec((n, k), lambda i: (0, 0)), pl.BlockSpec((tm, k), lambda i: (i, 0))],
        out_specs=pl.BlockSpec((n, tm), lambda i: (0, i)),
        out_shape=jax.ShapeDtypeStruct((n, m), out_dtype),
        compiler_params=_cparams(("parallel",)),
        name="matmul_nt",
    )(wt, a)


def _alibi_columns(slopes, tq, tk):
    return [_alibi_side(slopes, tq, True), _alibi_side(slopes, tk, False)]


def _alibi_side(slopes, n, query_side):
    pos = jnp.arange(n, dtype=F32)
    val = (slopes.astype(F32) * LOG2E)[:, None] * pos[None, :]

    def pieces(x):
        p1 = x.astype(BF16)
        r1 = x - p1.astype(F32)
        p2 = r1.astype(BF16)
        p3 = (r1 - p2.astype(F32)).astype(BF16)
        return [p1, p2, p3]

    ones = [jnp.ones_like(val, BF16)] * 3
    six = jnp.stack(pieces(-val) + ones if query_side else ones + pieces(val), axis=-1)
    pad = jnp.zeros(val.shape + (ATT_QKDIM - 6,), BF16)
    return jnp.concatenate([six, pad, six, pad], axis=-1)


def _attn_kernel(slope_ref, lam_ref, q_ref, k_ref, vt_ref, aq_ref, ak_ref, g_ref, o_ref,
                 m_ref, l_ref, acc_ref, *, tq, tk, nk, hb, out_scale):
    hg = pl.program_id(1)
    i = pl.program_id(2)
    j = pl.program_id(3)

    @pl.when(j == 0)
    def _():
        m_ref[...] = jnp.full(m_ref.shape, -jnp.inf, F32)
        l_ref[...] = jnp.zeros(l_ref.shape, F32)
        acc_ref[...] = jnp.zeros(acc_ref.shape, F32)

    q_first = lax.broadcasted_iota(I32, (tq, LANES), 1) < ATT_QKDIM
    k_first = lax.broadcasted_iota(I32, (tk, LANES), 1) < ATT_QKDIM
    nt = (((1,), (1,)), ((), ()))

    def update(hh, scores, shift):
        vt = vt_ref[hh * LANES:(hh + 1) * LANES, :]
        for c, s in enumerate(scores):
            m_old = m_ref[hh, c]
            m_new = jnp.maximum(m_old, jnp.max(s, axis=0, keepdims=True) + shift)
            alpha = jnp.exp2(m_old - m_new)
            p = jnp.exp2(s - (m_new - shift))
            l_ref[hh, c] = alpha * l_ref[hh, c] + jnp.sum(p, axis=0, keepdims=True)
            acc_ref[hh, c] = alpha * acc_ref[hh, c] + jnp.dot(vt, p.astype(BF16),
                                                              preferred_element_type=F32)
            m_ref[hh, c] = m_new

    keys_before = i * tq >= (j + 1) * tk
    keys_after = (i + 1) * tq <= j * tk
    off_diagonal = jnp.logical_or(keys_before, keys_after)

    @pl.when(off_diagonal)
    def _():
        sign = jnp.where(keys_after, -1.0, 1.0).astype(BF16)
        gap = jnp.abs(i * tq - j * tk).astype(F32)
        for hh in range(hb):
            sl = slice(hh * LANES, (hh + 1) * LANES)
            q = q_ref[:, sl]
            k = k_ref[:, sl]
            aq = aq_ref[hh]
            ak = ak_ref[hh] * sign
            s0 = lax.dot_general(jnp.where(k_first, k, ak), jnp.where(q_first, q, aq), nt,
                                 preferred_element_type=F32)
            s1 = lax.dot_general(jnp.where(k_first, ak, k), jnp.where(q_first, aq, q), nt,
                                 preferred_element_type=F32)
            update(hh, (s0, s1), -(slope_ref[hg * hb + hh] * LOG2E) * gap)

    @pl.when(jnp.logical_not(off_diagonal))
    def _():
        kpos = lax.broadcasted_iota(I32, (tk, tq), 0) + j * tk
        qpos = lax.broadcasted_iota(I32, (tk, tq), 1) + i * tq
        dist = jnp.abs(kpos - qpos).astype(F32)
        for hh in range(hb):
            sl = slice(hh * LANES, (hh + 1) * LANES)
            q = q_ref[:, sl]
            k = k_ref[:, sl]
            zero = jnp.zeros_like(q)
            bias = dist * (-(slope_ref[hg * hb + hh] * LOG2E))
            s0 = lax.dot_general(k, jnp.where(q_first, q, zero), nt, preferred_element_type=F32) + bias
            s1 = lax.dot_general(k, jnp.where(q_first, zero, q), nt, preferred_element_type=F32) + bias
            update(hh, (s0, s1), 0.0)

    @pl.when(j == nk - 1)
    def _():
        for hh in range(hb):
            o = acc_ref[hh, 0] / l_ref[hh, 0] - lam_ref[0] * (acc_ref[hh, 1] / l_ref[hh, 1])
            ms = jnp.mean(o * o, axis=0, keepdims=True)
            y = o * lax.rsqrt(ms + RMS_EPS) * (g_ref[...] * out_scale)
            o_ref[:, hh * LANES:(hh + 1) * LANES] = y.T.astype(o_ref.dtype)


def _attention(qk, vt, row_off, slopes, lam, sub_g, batch, seq, out_scale, tq=512, tk=1024, hb=2):
    tq = min(tq, seq)
    tk = min(tk, seq // 4)
    nq, nk = seq // tq, seq // tk
    assert row_off % tq == 0 and row_off % tk == 0 and ATT_HEADS % hb == 0
    oq, ok = row_off // tq, row_off // tk
    ng = ATT_HEADS // hb
    aq, ak = _alibi_columns(slopes, tq, tk)
    kern = functools.partial(_attn_kernel, tq=tq, tk=tk, nk=nk, hb=hb, out_scale=out_scale)
    smem = pl.BlockSpec(memory_space=pltpu.SMEM)
    return pl.pallas_call(
        kern,
        grid=(batch, ng, nq, nk),
        in_specs=[smem, smem,
                  pl.BlockSpec((tq, hb * LANES), lambda b, h, i, j: (oq + b * nq + i, h)),
                  pl.BlockSpec((tk, hb * LANES), lambda b, h, i, j: (ok + b * nk + j, ng + h)),
                  pl.BlockSpec((hb * LANES, tk), lambda b, h, i, j: (h, ok + b * nk + j)),
                  pl.BlockSpec((hb, tq, LANES), lambda b, h, i, j: (h, 0, 0)),
                  pl.BlockSpec((hb, tk, LANES), lambda b, h, i, j: (h, 0, 0)),
                  pl.BlockSpec((LANES, 1), lambda b, h, i, j: (0, 0))],
        out_specs=pl.BlockSpec((tq, hb * LANES), lambda b, h, i, j: (b * nq + i, h)),
        out_shape=jax.ShapeDtypeStruct((batch * seq, ATT_HEADS * ATT_VDIM), BF16),
        scratch_shapes=[pltpu.VMEM((hb, 2, 1, tq), F32), pltpu.VMEM((hb, 2, 1, tq), F32),
                        pltpu.VMEM((hb, 2, LANES, tq), F32)],
        compiler_params=_cparams(("parallel", "parallel", "parallel", "arbitrary")),
        name="diff_attention",
    )(slopes, lam, qk, qk, vt, aq, ak, sub_g.reshape(LANES, 1).astype(F32))


def _shortconv_kernel(z_ref, w_ref, b_ref, o_ref):
    z = z_ref[...]
    n = z.shape[0]
    row = lax.broadcasted_iota(I32, z.shape, 0)
    prev = jnp.where(row == 0, 0.0, pltpu.roll(z, 1, axis=0))
    nxt = jnp.where(row == n - 1, 0.0, pltpu.roll(z, n - 1, axis=0))
    w = w_ref[...]
    o_ref[...] = prev * w[0:1] + z * w[1:2] + nxt * w[2:3] + b_ref[...]


def _shortconv(zh, row_off, conv_w, conv_b, batch, seq, cb=256):
    c3 = zh.shape[1]
    c = c3 // 3
    ncb = c // cb
    assert row_off % seq == 0
    ob = row_off // seq
    out = pl.pallas_call(
        _shortconv_kernel,
        grid=(batch, 3, ncb),
        in_specs=[pl.BlockSpec((seq, cb), lambda b, p, j: (ob + b, p * ncb + j)),
                  pl.BlockSpec((3, cb), lambda b, p, j: (0, p * ncb + j)),
                  pl.BlockSpec((1, cb), lambda b, p, j: (0, p * ncb + j))],
        out_specs=pl.BlockSpec((None, None, seq, cb), lambda b, p, j: (p, b, 0, j)),
        out_shape=jax.ShapeDtypeStruct((3, batch, seq, c), F32),
        compiler_params=_cparams(("parallel", "parallel", "parallel")),
        name="hyena_shortconv",
    )(zh, conv_w.astype(F32), conv_b.reshape(1, c3).astype(F32))
    return out


def _filter_kernel(z_ref, w1_ref, b1_ref, w2_ref, b2_ref, w3_ref, b3_ref, fr_ref, w4_ref,
                   t_ref, d_ref, o_ref, h_ref, *, tl):
    i = pl.program_id(0)
    g = pl.program_id(1)
    hi = lax.Precision.HIGHEST

    @pl.when(g == 0)
    def _():
        fr = fr_ref[...]
        h = jnp.sin(fr * (jnp.dot(z_ref[...], w1_ref[...], precision=hi, preferred_element_type=F32) + b1_ref[...]))
        h = jnp.sin(fr * (jnp.dot(h, w2_ref[...], precision=hi, preferred_element_type=F32) + b2_ref[...]))
        h_ref[...] = jnp.sin(fr * (jnp.dot(h, w3_ref[...], precision=hi, preferred_element_type=F32) + b3_ref[...]))

    f = jnp.dot(h_ref[...].astype(BF16), w4_ref[...], preferred_element_type=F32)
    f = f * jnp.exp(-t_ref[...] * d_ref[...])
    row = lax.broadcasted_iota(I32, f.shape, 0) + i * tl
    drop = jnp.logical_and(row == 0, g % 2 == 1)
    o_ref[...] = jnp.where(drop, 0.0, f)


def _hyena_filter_signals(seq, w1, b1, w2, b2, w3, b3, w4, freq, n_ch, tl=512):
    t = jnp.linspace(0.0, 1.0, seq, dtype=F32)[:, None]
    w = 2.0 * math.pi * jnp.arange(seq, dtype=F32)[:, None] / seq
    f = jnp.linspace(1e-4, FILTER_BANDS - 1, FILTER_BANDS, dtype=F32)[None, :]
    z = jnp.concatenate([t, jnp.cos(f * w), -jnp.sin(f * w)], axis=-1)
    deltas = jnp.abs(jnp.linspace(math.log(DECAY_FAST) / DECAY_TARGET,
                                  math.log(DECAY_SLOW) / DECAY_TARGET, n_ch, dtype=F32))[None, :]
    hid = w1.shape[1]
    emb = LANES
    z = jnp.pad(z, ((0, 0), (0, emb - z.shape[1])))
    w1 = jnp.pad(w1.astype(F32), ((0, emb - w1.shape[0]), (0, 0)))
    tl = min(tl, seq)
    full = lambda shape: pl.BlockSpec(shape, lambda i, g: tuple(0 for _ in shape))
    return pl.pallas_call(
        functools.partial(_filter_kernel, tl=tl),
        grid=(seq // tl, 2 * HYENA_ORDER),
        in_specs=[pl.BlockSpec((tl, emb), lambda i, g: (i, 0)),
                  full((emb, hid)), full((1, hid)), full((hid, hid)), full((1, hid)),
                  full((hid, hid)), full((1, hid)), full((1, hid)),
                  pl.BlockSpec((hid, n_ch), lambda i, g: (0, g)),
                  pl.BlockSpec((tl, 1), lambda i, g: (i, 0)),
                  full((1, n_ch))],
        out_specs=pl.BlockSpec((None, tl, n_ch), lambda i, g: (g, i, 0)),
        out_shape=jax.ShapeDtypeStruct((2 * HYENA_ORDER, seq, n_ch), F32),
        scratch_shapes=[pltpu.VMEM((tl, hid), F32)],
        compiler_params=_cparams(("parallel", "arbitrary")),
        name="hyena_filter_mlp",
    )(z, w1.astype(F32), b1.reshape(1, hid).astype(F32), w2.astype(F32), b2.reshape(1, hid).astype(F32),
      w3.astype(F32), b3.reshape(1, hid).astype(F32), freq.reshape(1, hid).astype(F32), w4.astype(BF16),
      t, deltas)


def _dft_tables(r):
    n = r * r
    k2 = jnp.arange(r, dtype=I32)
    n2 = jnp.arange(r // 2, dtype=I32)
    ang1 = (2.0 * math.pi / r) * ((k2[:, None] * n2[None, :]) % r).astype(F32)
    f1 = jnp.concatenate([jnp.cos(ang1), -jnp.sin(ang1)], axis=0)
    k1 = jnp.arange(r, dtype=I32)
    n1 = jnp.arange(r, dtype=I32)
    kk = r * k1[None, :, None] + k2[:, None, None]
    ang2 = (2.0 * math.pi / n) * ((kk * n1[None, None, :]) % n).astype(F32)
    mr, mi_ = jnp.cos(ang2), -jnp.sin(ang2)
    mf = jnp.concatenate([jnp.concatenate([mr, -mi_], axis=2),
                          jnp.concatenate([mi_, mr], axis=2)], axis=1)
    minv = jnp.swapaxes(mf, 1, 2)
    g3 = jnp.concatenate([jnp.cos(ang1.T), -jnp.sin(ang1.T)], axis=1) * (1.0 / n)
    return f1.astype(BF16), mf.astype(BF16), minv.astype(BF16), g3.astype(BF16)


N1_BLOCK = SUBLANES


def _pack_complex(re, im):
    rb = lax.bitcast_convert_type(re.astype(BF16).astype(F32), U32)
    ib = lax.bitcast_convert_type(im.astype(BF16).astype(F32), U32)
    return (rb >> 16) | ib


def _unpack_complex_rows(w):
    re, im = _unpack(w)
    return jnp.concatenate([re, im], axis=0).astype(BF16)


def _fft1_kernel(f_ref, x_ref, o_ref, *, r):
    f = f_ref[...]
    for m in range(N1_BLOCK):
        res = jnp.dot(f, x_ref[:, m, :].astype(BF16), preferred_element_type=F32)
        o_ref[:, m, :] = _pack_complex(res[:r], res[r:])


def _fft_stage1(x4, part, f1, r, n_ch):
    nb = x4.shape[1]
    xv = x4.reshape(x4.shape[0], nb, r // 2, r, n_ch)
    return pl.pallas_call(
        functools.partial(_fft1_kernel, r=r),
        grid=(nb, r // N1_BLOCK),
        in_specs=[pl.BlockSpec((2 * r, r // 2), lambda b, j: (0, 0)),
                  pl.BlockSpec((None, None, r // 2, N1_BLOCK, n_ch), lambda b, j: (part, b, 0, j, 0))],
        out_specs=pl.BlockSpec((None, r, N1_BLOCK, n_ch), lambda b, j: (b, 0, j, 0)),
        out_shape=jax.ShapeDtypeStruct((nb, r, r, n_ch), U32),
        compiler_params=_cparams(("parallel", "parallel")),
        name="hyena_dft_stage1",
    )(f1, xv)


def _k2_block(r):
    return max(1, (2 * LANES) // r)


def _filter_spec_kernel(mf_ref, bf_ref, bb_ref, o_ref, *, r):
    for kk in range(mf_ref.shape[0]):
        m = mf_ref[kk]
        xf = jnp.dot(m, _unpack_complex_rows(bf_ref[kk]), preferred_element_type=F32)
        xb = jnp.dot(m, _unpack_complex_rows(bb_ref[kk]), preferred_element_type=F32)
        o_ref[kk] = _pack_complex(xf[:r] + xb[:r], xf[r:] - xb[r:])


def _filter_spectrum(b1, mf, r, n_ch):
    kb = _k2_block(r)
    blk = lambda sel: pl.BlockSpec((None, kb, r, n_ch), lambda o, k: (2 * o + sel, k, 0, 0))
    return pl.pallas_call(
        functools.partial(_filter_spec_kernel, r=r),
        grid=(HYENA_ORDER, r // kb),
        in_specs=[pl.BlockSpec((kb, 2 * r, 2 * r), lambda o, k: (k, 0, 0)), blk(0), blk(1)],
        out_specs=pl.BlockSpec((None, kb, r, n_ch), lambda o, k: (o, k, 0, 0)),
        out_shape=jax.ShapeDtypeStruct((HYENA_ORDER, r, r, n_ch), U32),
        compiler_params=_cparams(("parallel", "parallel")),
        name="hyena_filter_spectrum",
    )(mf, b1, b1)


def _fft2_kernel(mf_ref, mi_ref, b_ref, h_ref, o_ref, *, r):
    for kk in range(mf_ref.shape[0]):
        x = jnp.dot(mf_ref[kk], _unpack_complex_rows(b_ref[kk]), preferred_element_type=F32)
        xr, xi = x[:r], x[r:]
        hr, hi = _unpack(h_ref[kk])
        y = jnp.concatenate([xr * hr - xi * hi, xr * hi + xi * hr], axis=0).astype(BF16)
        c = jnp.dot(mi_ref[kk], y, preferred_element_type=F32)
        o_ref[kk] = _pack_complex(c[:r], c[r:])


def _fft_stage2(b1, h, order, mf, minv, r, n_ch):
    nb = b1.shape[0]
    kb = _k2_block(r)
    blk = pl.BlockSpec((None, kb, r, n_ch), lambda b, k: (b, k, 0, 0))
    return pl.pallas_call(
        functools.partial(_fft2_kernel, r=r),
        grid=(nb, r // kb),
        in_specs=[pl.BlockSpec((kb, 2 * r, 2 * r), lambda b, k: (k, 0, 0)),
                  pl.BlockSpec((kb, 2 * r, 2 * r), lambda b, k: (k, 0, 0)),
                  blk,
                  pl.BlockSpec((None, kb, r, n_ch), lambda b, k: (order, k, 0, 0))],
        out_specs=blk,
        out_shape=jax.ShapeDtypeStruct((nb, r, r, n_ch), U32),
        compiler_params=_cparams(("parallel", "parallel")),
        name="hyena_dft_stage2",
    )(mf, minv, b1, h)


def _fft3_kernel(g_ref, c_ref, gate_ref, s_ref, bias_ref, ng_ref, o_ref, *, final):
    g = g_ref[...]
    for m in range(N1_BLOCK):
        y = jnp.dot(g, _unpack_complex_rows(c_ref[:, m, :]), preferred_element_type=F32)
        s_new = gate_ref[:, m, :] * (y + s_ref[:, m, :] * bias_ref[...])
        if final:
            ms = jnp.mean(s_new * s_new, axis=-1, keepdims=True)
            s_new = s_new * lax.rsqrt(ms + RMS_EPS) * ng_ref[...]
        o_ref[:, m, :] = s_new


def _fft_stage3(c2, g3, z4, gate_part, s4, s_part, bias, norm_g, final, r, n_ch):
    nb = c2.shape[0]
    zv = z4.reshape(z4.shape[0], nb, r // 2, r, n_ch)
    sv = s4.reshape(s4.shape[0], nb, r // 2, r, n_ch)
    dspec = lambda part: pl.BlockSpec((None, None, r // 2, N1_BLOCK, n_ch), lambda b, j: (part, b, 0, j, 0))
    return pl.pallas_call(
        functools.partial(_fft3_kernel, final=final),
        grid=(nb, r // N1_BLOCK),
        in_specs=[pl.BlockSpec((r // 2, 2 * r), lambda b, j: (0, 0)),
                  pl.BlockSpec((None, r, N1_BLOCK, n_ch), lambda b, j: (b, 0, j, 0)),
                  dspec(gate_part), dspec(s_part),
                  pl.BlockSpec((1, n_ch), lambda b, j: (0, 0)),
                  pl.BlockSpec((1, n_ch), lambda b, j: (0, 0))],
        out_specs=pl.BlockSpec((None, r // 2, N1_BLOCK, n_ch), lambda b, j: (b, 0, j, 0)),
        out_shape=jax.ShapeDtypeStruct((nb, r // 2, r, n_ch), F32),
        compiler_params=_cparams(("parallel", "parallel")),
        name="hyena_dft_stage3",
    )(g3, c2, zv, sv, bias.reshape(1, n_ch).astype(F32), norm_g.reshape(1, n_ch).astype(F32))


def _hyena(zh, row_off, batch, seq, conv_w, conv_b, filt, fft_bias, out_g):
    n_ch = zh.shape[1] // 3
    r = int(round(math.sqrt(2 * seq)))
    assert r * r == 2 * seq and r % 16 == 0
    f1, mf, minv, g3 = _dft_tables(r)
    sig = _hyena_filter_signals(seq, *filt, n_ch=n_ch)
    hb1 = _fft_stage1(sig[None], 0, f1, r, n_ch)
    h = _filter_spectrum(hb1, mf, r, n_ch)
    z4 = _shortconv(zh, row_off, conv_w, conv_b, batch, seq)
    s4, s_part = z4, 2
    for o in range(HYENA_ORDER):
        b1 = _fft_stage1(s4, s_part, f1, r, n_ch)
        c2 = _fft_stage2(b1, h, o, mf, minv, r, n_ch)
        final = o == HYENA_ORDER - 1
        s = _fft_stage3(c2, g3, z4, o, s4, s_part, fft_bias[o], out_g, final, r, n_ch)
        s4, s_part = s.reshape(1, batch, seq, n_ch), 0
    return s4.reshape(batch * seq, n_ch)


def _extract_top(s, key, count):
    vals, keys = [], []
    for _ in range(count):
        m = jnp.max(s, axis=0, keepdims=True)
        kmin = jnp.min(jnp.where(s == m, key, jnp.inf), axis=0, keepdims=True)
        s = jnp.where(key == kmin, -jnp.inf, s)
        vals.append(m)
        keys.append(kmin)
    return vals, keys


def _peer_topk_kernel(q_ref, keys_ref, eid_ref, gate_ref):
    t = q_ref.shape[0]
    nk = PEER_NKEYS
    q = q_ref[...]
    row_key = lax.broadcasted_iota(I32, (nk, t), 0).astype(F32)
    tops = []
    for c in range(2):
        s = lax.dot_general(keys_ref[c], q[:, c * nk:(c + 1) * nk], (((1,), (1,)), ((), ())),
                            preferred_element_type=F32)
        tops.append(_extract_top(s, row_key, PEER_TOPK))
    (v1, i1), (v2, i2) = tops
    rows16 = lax.broadcasted_iota(I32, (PEER_TOPK, t), 0)
    v2a = jnp.zeros((PEER_TOPK, t), F32)
    i2a = jnp.zeros((PEER_TOPK, t), F32)
    for j in range(PEER_TOPK):
        v2a = jnp.where(rows16 == j, v2[j], v2a)
        i2a = jnp.where(rows16 == j, i2[j], i2a)
    n_exp = float(nk * nk)
    half = PEER_TOPK // 2
    rows8 = lax.broadcasted_iota(I32, (half, t), 0)
    pos8 = rows8.astype(F32)
    v2h, i2h = v2a[:half], i2a[:half]
    cand = [v1[0] + v2a]
    ckey = [rows16.astype(F32) * n_exp + (i1[0] * float(nk) + i2a)]
    for i in range(1, half):
        cand.append(jnp.where(rows8 < PEER_TOPK // (i + 1), v1[i] + v2h, -jnp.inf))
        ckey.append((pos8 + float(i * PEER_TOPK)) * n_exp + (i1[i] * float(nk) + i2h))
    v1t = jnp.zeros((half, t), F32)
    i1t = jnp.zeros((half, t), F32)
    for r in range(half):
        v1t = jnp.where(rows8 == r, v1[half + r], v1t)
        i1t = jnp.where(rows8 == r, i1[half + r], i1t)
    cand.append(v1t + v2[0])
    ckey.append((pos8 + float(half)) * (PEER_TOPK * n_exp) + (i1t * float(nk) + i2[0]))
    tv, tk_ = _extract_top(jnp.concatenate(cand, axis=0), jnp.concatenate(ckey, axis=0), PEER_TOPK)
    denom = jnp.zeros((1, t), F32)
    es = []
    for k in range(PEER_TOPK):
        e = jnp.exp(tv[k] - tv[0])
        es.append(e)
        denom = denom + e
    eid = jnp.zeros((PEER_TOPK, t), F32)
    gate = jnp.zeros((PEER_TOPK, t), F32)
    for k in range(PEER_TOPK):
        pos = jnp.floor(tk_[k] * (1.0 / n_exp))
        eid = jnp.where(rows16 == k, tk_[k] - pos * n_exp, eid)
        gate = jnp.where(rows16 == k, es[k] / denom, gate)
    eid_ref[...] = eid.astype(I32)
    gate_ref[...] = gate


def _peer_topk(q, keys, tm=256):
    t = q.shape[0]
    tm = min(tm, t)
    out_spec = pl.BlockSpec((PEER_TOPK, tm), lambda i, h: (h, i))
    return pl.pallas_call(
        _peer_topk_kernel,
        grid=(t // tm, PEER_HEADS),
        in_specs=[pl.BlockSpec((tm, 2 * PEER_NKEYS), lambda i, h: (i, h)),
                  pl.BlockSpec((None, 2, PEER_NKEYS, PEER_NKEYS), lambda i, h: (h, 0, 0, 0))],
        out_specs=[out_spec, out_spec],
        out_shape=[jax.ShapeDtypeStruct((PEER_HEADS * PEER_TOPK, t), I32),
                   jax.ShapeDtypeStruct((PEER_HEADS * PEER_TOPK, t), F32)],
        compiler_params=_cparams(("parallel", "parallel")),
        name="peer_topk",
    )(q, keys)


def _pack_table(tab):
    e, d = tab.shape
    assert d == 2 * SUBLANES * LANES
    bits = lax.bitcast_convert_type(tab.astype(BF16), jnp.uint16).astype(U32)
    packed = bits[:, :d // 2] | (bits[:, d // 2:] << 16)
    return packed.reshape(e * SUBLANES, LANES)


def _unpack(w):
    lo = lax.bitcast_convert_type(w << 16, F32)
    hi = lax.bitcast_convert_type(w & jnp.uint32(0xFFFF0000), F32)
    return lo, hi


_BITREV8 = (0, 4, 2, 6, 1, 5, 3, 7)


def _sublane_fold8(parts):
    sub = lax.broadcasted_iota(I32, (2 * SUBLANES, LANES), 0) // 2

    def rolled(a, shift):
        return pltpu.bitcast(pltpu.roll(pltpu.bitcast(a, U32), shift, axis=0), BF16)

    lvl = [parts[_BITREV8[r]] for r in range(8)]
    for shift, mask in ((4, sub < 4), (2, (sub % 4) < 2), (1, (sub % 2) < 1)):
        nxt = []
        for a, b in zip(lvl[0::2], lvl[1::2]):
            nxt.append(jnp.where(mask, a + rolled(a, SUBLANES - shift), b + rolled(b, shift)))
        lvl = nxt
    return lvl[0]


def _pair_ranges(half):
    if half == 0:
        return (0, PEER_WINDOW), (PEER_WINDOW, PEER_PAIRS)
    return (PEER_PAIRS - PEER_WINDOW, PEER_PAIRS), (0, PEER_PAIRS - PEER_WINDOW)


def _overflow(n_low, half):
    return n_low > PEER_WINDOW if half == 0 else n_low < PEER_PAIRS - PEER_WINDOW


def _table_rows(tab_ref, off):
    return _unpack(tab_ref[pl.ds(pl.multiple_of(off, SUBLANES), SUBLANES), :])


def _peer_u_kernel(off_ref, nlow_ref, tab_ref, x_ref, o_ref, extra_ref, *, tm, half):
    lane = lax.broadcasted_iota(I32, (SUBLANES, LANES), 1)
    sub = lax.broadcasted_iota(I32, (SUBLANES, LANES), 0)
    lane_grp = lax.shift_right_logical(lane, 3)
    diag = sub == (lane & (SUBLANES - 1))
    main, rest = _pair_ranges(half)

    def folded(t, p0, p1):
        xb = pltpu.bitcast(x_ref[t], BF16)
        out = []
        for g in range(p0 // SUBLANES, p1 // SUBLANES):
            parts = []
            for r in range(SUBLANES):
                off = pl.multiple_of(off_ref[t, g * SUBLANES + r], SUBLANES)
                parts.append(pltpu.bitcast(tab_ref[pl.ds(off, SUBLANES), :], BF16) * xb)
            out.append(pltpu.bitcast(_sublane_fold8(parts), U32))
        return out

    def lane_sums(folds, p0):
        mat = jnp.zeros((SUBLANES, LANES), F32)
        for i, f in enumerate(folds):
            lo, hi = _unpack(f)
            mat = jnp.where(lane_grp == p0 // SUBLANES + i, jnp.sum(lo + hi, axis=-1, keepdims=True), mat)
        return jnp.sum(jnp.where(diag, mat, 0.0), axis=0, keepdims=True)

    def finish(t, folds):
        o_ref[pl.ds(t, 1), :] = lane_sums(folds, main[0]) + extra_ref[pl.ds(t, 1), :]

    def token(t, prev):
        cur = folded(t, *main)
        finish(jnp.maximum(t - 1, 0), prev)
        extra_ref[pl.ds(t, 1), :] = jnp.zeros((1, LANES), F32)

        @pl.when(_overflow(nlow_ref[0, t], half))
        def _():
            extra_ref[pl.ds(t, 1), :] = lane_sums(folded(t, *rest), rest[0])

        return tuple(cur)

    extra_ref[pl.ds(0, 1), :] = jnp.zeros((1, LANES), F32)
    zeros = tuple(jnp.zeros((SUBLANES, LANES), U32) for _ in range((main[1] - main[0]) // SUBLANES))
    last = lax.fori_loop(0, tm, token, zeros)
    finish(tm - 1, last)


def _peer_u(off, nlow3, tab, x4, half, tm=128):
    t = off.shape[0]
    rows = PEER_HALF * SUBLANES
    return pl.pallas_call(
        functools.partial(_peer_u_kernel, tm=tm, half=half),
        grid=(t // tm,),
        in_specs=[pl.BlockSpec((tm, LANES), lambda i: (i, 0), memory_space=pltpu.SMEM),
                  pl.BlockSpec((None, 1, tm), lambda i: (i, 0, 0), memory_space=pltpu.SMEM),
                  pl.BlockSpec((rows, LANES), lambda i: (half, 0), pipeline_mode=pl.Buffered(1)),
                  pl.BlockSpec((tm, SUBLANES, LANES), lambda i: (i, 0, 0))],
        out_specs=pl.BlockSpec((tm, LANES), lambda i: (i, 0)),
        out_shape=jax.ShapeDtypeStruct((t, LANES), F32),
        scratch_shapes=[pltpu.VMEM((tm, LANES), F32)],
        compiler_params=_cparams(("arbitrary",)),
        name="peer_expert_scores",
    )(off, nlow3, tab, x4)


def _peer_coef_kernel(a0_ref, a1_ref, eid_ref, gate_ref, c0_ref, c1_ref):
    low = eid_ref[...] < PEER_HALF
    a = jnp.where(low, a0_ref[...], a1_ref[...])
    coef = gate_ref[...] * (0.5 * a * (1.0 + lax.erf(a * (1.0 / math.sqrt(2.0)))))
    c0_ref[...] = jnp.where(low, coef, 0.0)
    c1_ref[...] = jnp.where(low, 0.0, coef)


def _peer_coef(a0, a1, eid, gate, tm=1024):
    t = eid.shape[0]
    tm = min(tm, t)
    spec = pl.BlockSpec((tm, LANES), lambda i: (i, 0))
    return pl.pallas_call(
        _peer_coef_kernel,
        grid=(t // tm,),
        in_specs=[spec, spec, spec, spec],
        out_specs=[spec, spec],
        out_shape=[jax.ShapeDtypeStruct((t, LANES), F32)] * 2,
        compiler_params=_cparams(("parallel",)),
        name="peer_coef",
    )(a0, a1, eid, gate)


def _peer_v_kernel(off_ref, nlow_ref, coef_ref, tab_ref, base_ref, o_ref, *, tm, half):
    n_acc = 4
    main, rest = _pair_ranges(half)

    def store_row(t, lo, hi):
        pieces = [x[j:j + 1, :] for x in (lo, hi) for j in range(SUBLANES)]
        o_ref[pl.ds(t, 1), :] = jnp.concatenate(pieces, axis=1)

    def weighted(t, p0, p1):
        acc_lo = [jnp.zeros((SUBLANES, LANES), F32) for _ in range(n_acc)]
        acc_hi = [jnp.zeros((SUBLANES, LANES), F32) for _ in range(n_acc)]
        for p in range(p0, p1):
            c = coef_ref[t, p]
            lo, hi = _table_rows(tab_ref, off_ref[t, p])
            acc_lo[p % n_acc] = acc_lo[p % n_acc] + c * lo
            acc_hi[p % n_acc] = acc_hi[p % n_acc] + c * hi
        return ((acc_lo[0] + acc_lo[1]) + (acc_lo[2] + acc_lo[3]),
                (acc_hi[0] + acc_hi[1]) + (acc_hi[2] + acc_hi[3]))

    def token(t, carry):
        lo, hi = weighted(t, *main)
        store_row(t, lo, hi)

        @pl.when(_overflow(nlow_ref[0, t], half))
        def _():
            lo2, hi2 = weighted(t, *rest)
            store_row(t, lo + lo2, hi + hi2)

        return carry

    lax.fori_loop(0, tm, token, 0)
    o_ref[...] = o_ref[...] + base_ref[...]


def _peer_v(off, nlow3, coef, tab, base, half, tm=128):
    t, d = base.shape
    rows = PEER_HALF * SUBLANES
    smem = pl.BlockSpec((tm, LANES), lambda i: (i, 0), memory_space=pltpu.SMEM)
    return pl.pallas_call(
        functools.partial(_peer_v_kernel, tm=tm, half=half),
        grid=(t // tm,),
        in_specs=[smem,
                  pl.BlockSpec((None, 1, tm), lambda i: (i, 0, 0), memory_space=pltpu.SMEM),
                  smem,
                  pl.BlockSpec((rows, LANES), lambda i: (half, 0), pipeline_mode=pl.Buffered(1)),
                  pl.BlockSpec((tm, d), lambda i: (i, 0))],
        out_specs=pl.BlockSpec((tm, d), lambda i: (i, 0)),
        out_shape=jax.ShapeDtypeStruct((t, d), F32),
        compiler_params=_cparams(("arbitrary",)),
        name="peer_expert_sum",
    )(off, nlow3, coef, tab, base)


def _peer(x1, norm2_g, wq_bf, keys_bf, u_packed, v_packed, tm=128):
    t, d = x1.shape
    tm = min(tm, t)
    (xn_bf,) = _rmsnorm(x1, norm2_g, (BF16,))
    q = _matmul(xn_bf, wq_bf, wq_bf.shape[1], 0, BF16)
    eid_t, gate_t = _peer_topk(q, keys_bf)
    eid, gate = eid_t.T, gate_t.T
    high = eid >= PEER_HALF
    order = jnp.argsort(high, axis=-1, stable=True)
    eid = jnp.take_along_axis(eid, order, axis=-1)
    gate = jnp.take_along_axis(gate, order, axis=-1)
    nlow3 = (PEER_PAIRS - jnp.sum(high, axis=-1, dtype=I32)).reshape(t // tm, 1, tm)
    off = (eid & (PEER_HALF - 1)) * SUBLANES
    xw = _pack_table(xn_bf).reshape(t, SUBLANES, LANES)
    a0 = _peer_u(off, nlow3, u_packed, xw, 0, tm)
    a1 = _peer_u(off, nlow3, u_packed, xw, 1, tm)
    c0, c1 = _peer_coef(a0, a1, eid, gate)
    y = _peer_v(off, nlow3, c0, v_packed, x1, 0, tm)
    return _peer_v(off, nlow3, c1, v_packed, y, 1, tm)


def kernel(x_prompt, x_sample, norm1_g, w_in, q_norm_g, k_norm_g, lambda_q1, lambda_k1, lambda_q2, lambda_k2, attn_sub_g, conv_w, conv_b, filt_w1, filt_b1, filt_w2, filt_b2, filt_w3, filt_b3, filt_w4, filt_freq, fft_bias, hyena_out_g, w_out, norm2_g, peer_wq, peer_keys, peer_u, peer_v):
    depth = w_in.shape[0]
    d_model = x_prompt.shape[-1]
    att_w = ATT_HEADS * ATT_VDIM
    shapes = [x_prompt.shape[:2], x_sample.shape[:2]]
    x = jnp.concatenate([x_prompt.reshape(-1, d_model), x_sample.reshape(-1, d_model)], axis=0)
    slopes = 2.0 ** (-8.0 * jnp.arange(1, ATT_HEADS + 1, dtype=F32) / ATT_HEADS)

    for l in range(depth):
        lambda_init = 0.8 - 0.6 * math.exp(-0.3 * l)
        lam = (jnp.exp(jnp.sum(lambda_q1[l].astype(F32) * lambda_k1[l].astype(F32)))
               - jnp.exp(jnp.sum(lambda_q2[l].astype(F32) * lambda_k2[l].astype(F32)))
               + lambda_init).reshape(1)
        w_in_bf = w_in[l].astype(BF16)
        q_gain = jnp.tile(q_norm_g[l].astype(F32), 2 * ATT_HEADS) * (ATT_QKDIM ** -0.5 * LOG2E)
        k_gain = jnp.tile(k_norm_g[l].astype(F32), 2 * ATT_HEADS)
        qk_gain = jnp.concatenate([q_gain, k_gain]).reshape(1, 2 * att_w)

        (h_bf,) = _rmsnorm(x, norm1_g[l], (BF16,))
        qk = _matmul(h_bf, w_in_bf, 2 * att_w, 0, BF16, mode="qknorm", extra=qk_gain)
        vt = _matmul_nt(w_in_bf[:, 2 * att_w:3 * att_w].T, h_bf, BF16)
        zh = _matmul(h_bf, w_in_bf, w_in.shape[2] - 3 * att_w, 3 * att_w, F32)

        filt = (filt_w1[l], filt_b1[l], filt_w2[l], filt_b2[l], filt_w3[l], filt_b3[l], filt_w4[l], filt_freq[l])
        mixes, row = [], 0
        for (b, s) in shapes:
            n = b * s
            att = _attention(qk, vt, row, slopes, lam, attn_sub_g[l], b, s, 1.0 - lambda_init)
            hy = _hyena(zh, row, b, s, conv_w[l], conv_b[l], filt, fft_bias[l], hyena_out_g[l])
            mixes.append(jnp.concatenate([att, hy.astype(BF16)], axis=-1))
            row += n
        mix = jnp.concatenate(mixes, axis=0)
        x = _matmul(mix, w_out[l].astype(BF16), d_model, 0, F32, mode="residual", extra=x)

        x = _peer(x, norm2_g[l], peer_wq[l].astype(BF16), peer_keys[l].astype(BF16),
                  _pack_table(peer_u[l]), _pack_table(peer_v[l]))

    n0 = shapes[0][0] * shapes[0][1]
    return (x[:n0].reshape(x_prompt.shape), x[n0:].reshape(x_sample.shape))
```

```python
import functools
import math

import jax
import jax.numpy as jnp
from jax import lax
from jax.experimental import pallas as pl
from jax.experimental.pallas import tpu as pltpu

F32 = jnp.float32
BF16 = jnp.bfloat16
I32 = jnp.int32
U32 = jnp.uint32

RMS_EPS = 1e-6
LOG2E = 1.4426950408889634
LANES = 128
SUBLANES = 8
VMEM_LIMIT_BYTES = 56 * 1024 * 1024

ATT_HEADS = 8
ATT_VDIM = 128
ATT_QKDIM = 64
HYENA_ORDER = 2
FILTER_BANDS = 16
DECAY_FAST = 0.3
DECAY_SLOW = 1.5
DECAY_TARGET = 1e-2
PEER_HEADS = 8
PEER_NKEYS = 128
PEER_TOPK = 16
PEER_HALF = PEER_NKEYS * PEER_NKEYS // 2
PEER_PAIRS = PEER_HEADS * PEER_TOPK
PEER_WINDOW = 72


def _cparams(sem, vmem=VMEM_LIMIT_BYTES):
    return pltpu.CompilerParams(dimension_semantics=sem, vmem_limit_bytes=vmem)


def _rmsnorm_kernel(x_ref, g_ref, *o_refs):
    x = x_ref[...]
    ms = jnp.mean(x * x, axis=-1, keepdims=True)
    y = x * lax.rsqrt(ms + RMS_EPS) * g_ref[...]
    for o_ref in o_refs:
        o_ref[...] = y.astype(o_ref.dtype)


def _rmsnorm(x, g, out_dtypes, tm=512):
    t, d = x.shape
    tm = min(tm, t)
    spec = pl.BlockSpec((tm, d), lambda i: (i, 0))
    return pl.pallas_call(
        _rmsnorm_kernel,
        grid=(t // tm,),
        in_specs=[spec, pl.BlockSpec((1, d), lambda i: (0, 0))],
        out_specs=[spec for _ in out_dtypes],
        out_shape=[jax.ShapeDtypeStruct((t, d), dt) for dt in out_dtypes],
        compiler_params=_cparams(("parallel",)),
        name="rmsnorm",
    )(x, g.reshape(1, d).astype(F32))


def _group_rms_scale(x, gain):
    lane = lax.broadcasted_iota(I32, x.shape, 1)
    lo = lane < ATT_QKDIM
    x2 = x * x
    s_lo = jnp.sum(jnp.where(lo, x2, 0.0), axis=-1, keepdims=True)
    s_hi = jnp.sum(jnp.where(lo, 0.0, x2), axis=-1, keepdims=True)
    ms = jnp.where(lo, s_lo, s_hi) * (1.0 / ATT_QKDIM)
    return x * lax.rsqrt(ms + RMS_EPS) * gain


def _mm_kernel(a_ref, b_ref, *rest, mode):
    acc = jnp.dot(a_ref[...], b_ref[...], preferred_element_type=F32)
    if mode == "plain":
        (o_ref,) = rest
        o_ref[...] = acc.astype(o_ref.dtype)
    elif mode == "residual":
        r_ref, o_ref = rest
        o_ref[...] = (acc + r_ref[...]).astype(o_ref.dtype)
    elif mode == "qknorm":
        g_ref, o_ref = rest
        for c in range(acc.shape[1] // LANES):
            sl = slice(c * LANES, (c + 1) * LANES)
            o_ref[:, sl] = _group_rms_scale(acc[:, sl], g_ref[:, sl]).astype(o_ref.dtype)
    else:
        raise ValueError(mode)


def _matmul(a, b, n_cols, col_off, out_dtype, mode="plain", extra=None, tm=1024, tn=1024):
    m, k = a.shape
    tm = min(tm, m)
    tn = min(tn, n_cols)
    assert col_off % tn == 0 and n_cols % tn == 0 and m % tm == 0
    off = col_off // tn
    in_specs = [pl.BlockSpec((tm, k), lambda i, j: (i, 0)),
                pl.BlockSpec((k, tn), lambda i, j: (0, j + off))]
    args = [a, b]
    if mode == "residual":
        in_specs.append(pl.BlockSpec((tm, tn), lambda i, j: (i, j)))
        args.append(extra)
    elif mode == "qknorm":
        in_specs.append(pl.BlockSpec((1, tn), lambda i, j: (0, j)))
        args.append(extra)
    return pl.pallas_call(
        functools.partial(_mm_kernel, mode=mode),
        grid=(m // tm, n_cols // tn),
        in_specs=in_specs,
        out_specs=pl.BlockSpec((tm, tn), lambda i, j: (i, j)),
        out_shape=jax.ShapeDtypeStruct((m, n_cols), out_dtype),
        compiler_params=_cparams(("parallel", "arbitrary")),
        name="matmul_" + mode,
    )(*args)


def _mm_nt_kernel(w_ref, a_ref, o_ref):
    o_ref[...] = lax.dot_general(w_ref[...], a_ref[...], (((1,), (1,)), ((), ())),
                                 preferred_element_type=F32).astype(o_ref.dtype)


def _matmul_nt(wt, a, out_dtype, tm=1024):
    n, k = wt.shape
    m = a.shape[0]
    tm = min(tm, m)
    return pl.pallas_call(
        _mm_nt_kernel,
        grid=(m // tm,),
        in_specs=[pl.BlockSpec((n, k), lambda i: (0, 0)), pl.BlockSpec((tm, k), lambda i: (i, 0))],
        out_specs=pl.BlockSpec((n, tm), lambda i: (0, i)),
        out_shape=jax.ShapeDtypeStruct((n, m), out_dtype),
        compiler_params=_cparams(("parallel",)),
        name="matmul_nt",
    )(wt, a)


def _alibi_columns(slopes, tq, tk):
    return [_alibi_side(slopes, tq, True), _alibi_side(slopes, tk, False)]


def _alibi_side(slopes, n, query_side):
    pos = jnp.arange(n, dtype=F32)
    val = (slopes.astype(F32) * LOG2E)[:, None] * pos[None, :]

    def pieces(x):
        p1 = x.astype(BF16)
        r1 = x - p1.astype(F32)
        p2 = r1.astype(BF16)
        p3 = (r1 - p2.astype(F32)).astype(BF16)
        return [p1, p2, p3]

    ones = [jnp.ones_like(val, BF16)] * 3
    six = jnp.stack(pieces(-val) + ones if query_side else ones + pieces(val), axis=-1)
    pad = jnp.zeros(val.shape + (ATT_QKDIM - 6,), BF16)
    return jnp.concatenate([six, pad, six, pad], axis=-1)


def _attn_kernel(slope_ref, lam_ref, q_ref, k_ref, vt_ref, aq_ref, ak_ref, g_ref, o_ref,
                 m_ref, l_ref, acc_ref, *, tq, tk, nk, hb, out_scale):
    hg = pl.program_id(1)
    i = pl.program_id(2)
    j = pl.program_id(3)

    @pl.when(j == 0)
    def _():
        m_ref[...] = jnp.full(m_ref.shape, -jnp.inf, F32)
        l_ref[...] = jnp.zeros(l_ref.shape, F32)
        acc_ref[...] = jnp.zeros(acc_ref.shape, F32)

    q_first = lax.broadcasted_iota(I32, (tq, LANES), 1) < ATT_QKDIM
    k_first = lax.broadcasted_iota(I32, (tk, LANES), 1) < ATT_QKDIM
    nt = (((1,), (1,)), ((), ()))

    def update(hh, scores, shift):
        vt = vt_ref[hh * LANES:(hh + 1) * LANES, :]
        for c, s in enumerate(scores):
            m_old = m_ref[hh, c]
            m_new = jnp.maximum(m_old, jnp.max(s, axis=0, keepdims=True) + shift)
            alpha = jnp.exp2(m_old - m_new)
            p = jnp.exp2(s - (m_new - shift))
            l_ref[hh, c] = alpha * l_ref[hh, c] + jnp.sum(p, axis=0, keepdims=True)
            acc_ref[hh, c] = alpha * acc_ref[hh, c] + jnp.dot(vt, p.astype(BF16),
                                                              preferred_element_type=F32)
            m_ref[hh, c] = m_new

    keys_before = i * tq >= (j + 1) * tk
    keys_after = (i + 1) * tq <= j * tk
    off_diagonal = jnp.logical_or(keys_before, keys_after)

    @pl.when(off_diagonal)
    def _():
        sign = jnp.where(keys_after, -1.0, 1.0).astype(BF16)
        gap = jnp.abs(i * tq - j * tk).astype(F32)
        for hh in range(hb):
            sl = slice(hh * LANES, (hh + 1) * LANES)
            q = q_ref[:, sl]
            k = k_ref[:, sl]
            aq = aq_ref[hh]
            ak = ak_ref[hh] * sign
            s0 = lax.dot_general(jnp.where(k_first, k, ak), jnp.where(q_first, q, aq), nt,
                                 preferred_element_type=F32)
            s1 = lax.dot_general(jnp.where(k_first, ak, k), jnp.where(q_first, aq, q), nt,
                                 preferred_element_type=F32)
            update(hh, (s0, s1), -(slope_ref[hg * hb + hh] * LOG2E) * gap)

    @pl.when(jnp.logical_not(off_diagonal))
    def _():
        kpos = lax.broadcasted_iota(I32, (tk, tq), 0) + j * tk
        qpos = lax.broadcasted_iota(I32, (tk, tq), 1) + i * tq
        dist = jnp.abs(kpos - qpos).astype(F32)
        for hh in range(hb):
            sl = slice(hh * LANES, (hh + 1) * LANES)
            q = q_ref[:, sl]
            k = k_ref[:, sl]
            zero = jnp.zeros_like(q)
            bias = dist * (-(slope_ref[hg * hb + hh] * LOG2E))
            s0 = lax.dot_general(k, jnp.where(q_first, q, zero), nt, preferred_element_type=F32) + bias
            s1 = lax.dot_general(k, jnp.where(q_first, zero, q), nt, preferred_element_type=F32) + bias
            update(hh, (s0, s1), 0.0)

    @pl.when(j == nk - 1)
    def _():
        for hh in range(hb):
            o = acc_ref[hh, 0] / l_ref[hh, 0] - lam_ref[0] * (acc_ref[hh, 1] / l_ref[hh, 1])
            ms = jnp.mean(o * o, axis=0, keepdims=True)
            y = o * lax.rsqrt(ms + RMS_EPS) * (g_ref[...] * out_scale)
            o_ref[:, hh * LANES:(hh + 1) * LANES] = y.T.astype(o_ref.dtype)


def _attention(qk, vt, row_off, slopes, lam, sub_g, batch, seq, out_scale, tq=512, tk=1024, hb=2):
    tq = min(tq, seq)
    tk = min(tk, seq // 4)
    nq, nk = seq // tq, seq // tk
    assert row_off % tq == 0 and row_off % tk == 0 and ATT_HEADS % hb == 0
    oq, ok = row_off // tq, row_off // tk
    ng = ATT_HEADS // hb
    aq, ak = _alibi_columns(slopes, tq, tk)
    kern = functools.partial(_attn_kernel, tq=tq, tk=tk, nk=nk, hb=hb, out_scale=out_scale)
    smem = pl.BlockSpec(memory_space=pltpu.SMEM)
    return pl.pallas_call(
        kern,
        grid=(batch, ng, nq, nk),
        in_specs=[smem, smem,
                  pl.BlockSpec((tq, hb * LANES), lambda b, h, i, j: (oq + b * nq + i, h)),
                  pl.BlockSpec((tk, hb * LANES), lambda b, h, i, j: (ok + b * nk + j, ng + h)),
                  pl.BlockSpec((hb * LANES, tk), lambda b, h, i, j: (h, ok + b * nk + j)),
                  pl.BlockSpec((hb, tq, LANES), lambda b, h, i, j: (h, 0, 0)),
                  pl.BlockSpec((hb, tk, LANES), lambda b, h, i, j: (h, 0, 0)),
                  pl.BlockSpec((LANES, 1), lambda b, h, i, j: (0, 0))],
        out_specs=pl.BlockSpec((tq, hb * LANES), lambda b, h, i, j: (b * nq + i, h)),
        out_shape=jax.ShapeDtypeStruct((batch * seq, ATT_HEADS * ATT_VDIM), BF16),
        scratch_shapes=[pltpu.VMEM((hb, 2, 1, tq), F32), pltpu.VMEM((hb, 2, 1, tq), F32),
                        pltpu.VMEM((hb, 2, LANES, tq), F32)],
        compiler_params=_cparams(("parallel", "parallel", "parallel", "arbitrary")),
        name="diff_attention",
    )(slopes, lam, qk, qk, vt, aq, ak, sub_g.reshape(LANES, 1).astype(F32))


def _shortconv_kernel(z_ref, w_ref, b_ref, o_ref):
    z = z_ref[...]
    n = z.shape[0]
    row = lax.broadcasted_iota(I32, z.shape, 0)
    prev = jnp.where(row == 0, 0.0, pltpu.roll(z, 1, axis=0))
    nxt = jnp.where(row == n - 1, 0.0, pltpu.roll(z, n - 1, axis=0))
    w = w_ref[...]
    o_ref[...] = prev * w[0:1] + z * w[1:2] + nxt * w[2:3] + b_ref[...]


def _shortconv(zh, row_off, conv_w, conv_b, batch, seq, cb=256):
    c3 = zh.shape[1]
    c = c3 // 3
    ncb = c // cb
    assert row_off % seq == 0
    ob = row_off // seq
    out = pl.pallas_call(
        _shortconv_kernel,
        grid=(batch, 3, ncb),
        in_specs=[pl.BlockSpec((seq, cb), lambda b, p, j: (ob + b, p * ncb + j)),
                  pl.BlockSpec((3, cb), lambda b, p, j: (0, p * ncb + j)),
                  pl.BlockSpec((1, cb), lambda b, p, j: (0, p * ncb + j))],
        out_specs=pl.BlockSpec((None, None, seq, cb), lambda b, p, j: (p, b, 0, j)),
        out_shape=jax.ShapeDtypeStruct((3, batch, seq, c), F32),
        compiler_params=_cparams(("parallel", "parallel", "parallel")),
        name="hyena_shortconv",
    )(zh, conv_w.astype(F32), conv_b.reshape(1, c3).astype(F32))
    return out


def _filter_kernel(z_ref, w1_ref, b1_ref, w2_ref, b2_ref, w3_ref, b3_ref, fr_ref, w4_ref,
                   t_ref, d_ref, o_ref, h_ref, *, tl):
    i = pl.program_id(0)
    g = pl.program_id(1)
    hi = lax.Precision.HIGHEST

    @pl.when(g == 0)
    def _():
        fr = fr_ref[...]
        h = jnp.sin(fr * (jnp.dot(z_ref[...], w1_ref[...], precision=hi, preferred_element_type=F32) + b1_ref[...]))
        h = jnp.sin(fr * (jnp.dot(h, w2_ref[...], precision=hi, preferred_element_type=F32) + b2_ref[...]))
        h_ref[...] = jnp.sin(fr * (jnp.dot(h, w3_ref[...], precision=hi, preferred_element_type=F32) + b3_ref[...]))

    f = jnp.dot(h_ref[...].astype(BF16), w4_ref[...], preferred_element_type=F32)
    f = f * jnp.exp(-t_ref[...] * d_ref[...])
    row = lax.broadcasted_iota(I32, f.shape, 0) + i * tl
    drop = jnp.logical_and(row == 0, g % 2 == 1)
    o_ref[...] = jnp.where(drop, 0.0, f)


def _hyena_filter_signals(seq, w1, b1, w2, b2, w3, b3, w4, freq, n_ch, tl=512):
    t = jnp.linspace(0.0, 1.0, seq, dtype=F32)[:, None]
    w = 2.0 * math.pi * jnp.arange(seq, dtype=F32)[:, None] / seq
    f = jnp.linspace(1e-4, FILTER_BANDS - 1, FILTER_BANDS, dtype=F32)[None, :]
    z = jnp.concatenate([t, jnp.cos(f * w), -jnp.sin(f * w)], axis=-1)
    deltas = jnp.abs(jnp.linspace(math.log(DECAY_FAST) / DECAY_TARGET,
                                  math.log(DECAY_SLOW) / DECAY_TARGET, n_ch, dtype=F32))[None, :]
    hid = w1.shape[1]
    emb = LANES
    z = jnp.pad(z, ((0, 0), (0, emb - z.shape[1])))
    w1 = jnp.pad(w1.astype(F32), ((0, emb - w1.shape[0]), (0, 0)))
    tl = min(tl, seq)
    full = lambda shape: pl.BlockSpec(shape, lambda i, g: tuple(0 for _ in shape))
    return pl.pallas_call(
        functools.partial(_filter_kernel, tl=tl),
        grid=(seq // tl, 2 * HYENA_ORDER),
        in_specs=[pl.BlockSpec((tl, emb), lambda i, g: (i, 0)),
                  full((emb, hid)), full((1, hid)), full((hid, hid)), full((1, hid)),
                  full((hid, hid)), full((1, hid)), full((1, hid)),
                  pl.BlockSpec((hid, n_ch), lambda i, g: (0, g)),
                  pl.BlockSpec((tl, 1), lambda i, g: (i, 0)),
                  full((1, n_ch))],
        out_specs=pl.BlockSpec((None, tl, n_ch), lambda i, g: (g, i, 0)),
        out_shape=jax.ShapeDtypeStruct((2 * HYENA_ORDER, seq, n_ch), F32),
        scratch_shapes=[pltpu.VMEM((tl, hid), F32)],
        compiler_params=_cparams(("parallel", "arbitrary")),
        name="hyena_filter_mlp",
    )(z, w1.astype(F32), b1.reshape(1, hid).astype(F32), w2.astype(F32), b2.reshape(1, hid).astype(F32),
      w3.astype(F32), b3.reshape(1, hid).astype(F32), freq.reshape(1, hid).astype(F32), w4.astype(BF16),
      t, deltas)


def _dft_tables(r):
    n = r * r
    k2 = jnp.arange(r, dtype=I32)
    n2 = jnp.arange(r // 2, dtype=I32)
    ang1 = (2.0 * math.pi / r) * ((k2[:, None] * n2[None, :]) % r).astype(F32)
    f1 = jnp.concatenate([jnp.cos(ang1), -jnp.sin(ang1)], axis=0)
    k1 = jnp.arange(r, dtype=I32)
    n1 = jnp.arange(r, dtype=I32)
    kk = r * k1[None, :, None] + k2[:, None, None]
    ang2 = (2.0 * math.pi / n) * ((kk * n1[None, None, :]) % n).astype(F32)
    mr, mi_ = jnp.cos(ang2), -jnp.sin(ang2)
    mf = jnp.concatenate([jnp.concatenate([mr, -mi_], axis=2),
                          jnp.concatenate([mi_, mr], axis=2)], axis=1)
    minv = jnp.swapaxes(mf, 1, 2)
    g3 = jnp.concatenate([jnp.cos(ang1.T), -jnp.sin(ang1.T)], axis=1) * (1.0 / n)
    eye = jnp.eye(N1_BLOCK, dtype=F32)
    f1, g3 = jnp.kron(f1, eye), jnp.kron(g3, eye)
    return f1.astype(BF16), mf.astype(BF16), minv.astype(BF16), g3.astype(BF16)


N1_BLOCK = SUBLANES


def _pack_complex(re, im):
    rb = lax.bitcast_convert_type(re.astype(BF16).astype(F32), U32)
    ib = lax.bitcast_convert_type(im.astype(BF16).astype(F32), U32)
    return (rb >> 16) | ib


def _unpack_complex_rows(w):
    re, im = _unpack(w)
    return jnp.concatenate([re, im], axis=0).astype(BF16)


def _fft1_kernel(f_ref, x_ref, o_ref, *, r):
    n_ch = x_ref.shape[-1]
    x = x_ref[...].reshape((r // 2) * N1_BLOCK, n_ch).astype(BF16)
    res = jnp.dot(f_ref[...], x, preferred_element_type=F32)
    half = r * N1_BLOCK
    o_ref[...] = _pack_complex(res[:half], res[half:]).reshape(r, N1_BLOCK, n_ch)


def _fft_stage1(x4, part, f1, r, n_ch):
    nb = x4.shape[1]
    xv = x4.reshape(x4.shape[0], nb, r // 2, r, n_ch)
    return pl.pallas_call(
        functools.partial(_fft1_kernel, r=r),
        grid=(nb, r // N1_BLOCK),
        in_specs=[pl.BlockSpec(f1.shape, lambda b, j: (0, 0)),
                  pl.BlockSpec((None, None, r // 2, N1_BLOCK, n_ch), lambda b, j: (part, b, 0, j, 0))],
        out_specs=pl.BlockSpec((None, r, N1_BLOCK, n_ch), lambda b, j: (b, 0, j, 0)),
        out_shape=jax.ShapeDtypeStruct((nb, r, r, n_ch), U32),
        compiler_params=_cparams(("parallel", "parallel")),
        name="hyena_dft_stage1",
    )(f1, xv)


def _k2_block(r):
    return max(1, (2 * LANES) // r)


def _filter_spec_kernel(mf_ref, bf_ref, bb_ref, o_ref, *, r):
    for kk in range(mf_ref.shape[0]):
        m = mf_ref[kk]
        xf = jnp.dot(m, _unpack_complex_rows(bf_ref[kk]), preferred_element_type=F32)
        xb = jnp.dot(m, _unpack_complex_rows(bb_ref[kk]), preferred_element_type=F32)
        o_ref[kk] = _pack_complex(xf[:r] + xb[:r], xf[r:] - xb[r:])


def _filter_spectrum(b1, mf, r, n_ch):
    kb = _k2_block(r)
    blk = lambda sel: pl.BlockSpec((None, kb, r, n_ch), lambda o, k: (2 * o + sel, k, 0, 0))
    return pl.pallas_call(
        functools.partial(_filter_spec_kernel, r=r),
        grid=(HYENA_ORDER, r // kb),
        in_specs=[pl.BlockSpec((kb, 2 * r, 2 * r), lambda o, k: (k, 0, 0)), blk(0), blk(1)],
        out_specs=pl.BlockSpec((None, kb, r, n_ch), lambda o, k: (o, k, 0, 0)),
        out_shape=jax.ShapeDtypeStruct((HYENA_ORDER, r, r, n_ch), U32),
        compiler_params=_cparams(("parallel", "parallel")),
        name="hyena_filter_spectrum",
    )(mf, b1, b1)


def _fft2_kernel(mf_ref, mi_ref, b_ref, h_ref, o_ref, *, r):
    for kk in range(mf_ref.shape[0]):
        x = jnp.dot(mf_ref[kk], _unpack_complex_rows(b_ref[kk]), preferred_element_type=F32)
        xr, xi = x[:r], x[r:]
        hr, hi = _unpack(h_ref[kk])
        y = jnp.concatenate([xr * hr - xi * hi, xr * hi + xi * hr], axis=0).astype(BF16)
        c = jnp.dot(mi_ref[kk], y, preferred_element_type=F32)
        o_ref[kk] = _pack_complex(c[:r], c[r:])


def _fft_stage2(b1, h, order, mf, minv, r, n_ch):
    nb = b1.shape[0]
    kb = _k2_block(r)
    blk = pl.BlockSpec((None, kb, r, n_ch), lambda b, k: (b, k, 0, 0))
    return pl.pallas_call(
        functools.partial(_fft2_kernel, r=r),
        grid=(nb, r // kb),
        in_specs=[pl.BlockSpec((kb, 2 * r, 2 * r), lambda b, k: (k, 0, 0)),
                  pl.BlockSpec((kb, 2 * r, 2 * r), lambda b, k: (k, 0, 0)),
                  blk,
                  pl.BlockSpec((None, kb, r, n_ch), lambda b, k: (order, k, 0, 0))],
        out_specs=blk,
        out_shape=jax.ShapeDtypeStruct((nb, r, r, n_ch), U32),
        compiler_params=_cparams(("parallel", "parallel")),
        name="hyena_dft_stage2",
    )(mf, minv, b1, h)


def _fft3_kernel(g_ref, c_ref, gate_ref, s_ref, bias_ref, ng_ref, o_ref, *, final):
    r, nb, n_ch = c_ref.shape
    cc = _unpack_complex_rows(c_ref[...].reshape(r * nb, n_ch))
    y = jnp.dot(g_ref[...], cc, preferred_element_type=F32)
    rows = (r // 2) * nb
    s_new = gate_ref[...].reshape(rows, n_ch) * (y + s_ref[...].reshape(rows, n_ch) * bias_ref[...])
    if final:
        ms = jnp.mean(s_new * s_new, axis=-1, keepdims=True)
        s_new = s_new * lax.rsqrt(ms + RMS_EPS) * ng_ref[...]
    o_ref[...] = s_new.reshape(r // 2, nb, n_ch)


def _fft_stage3(c2, g3, z4, gate_part, s4, s_part, bias, norm_g, final, r, n_ch):
    nb = c2.shape[0]
    zv = z4.reshape(z4.shape[0], nb, r // 2, r, n_ch)
    sv = s4.reshape(s4.shape[0], nb, r // 2, r, n_ch)
    dspec = lambda part: pl.BlockSpec((None, None, r // 2, N1_BLOCK, n_ch), lambda b, j: (part, b, 0, j, 0))
    return pl.pallas_call(
        functools.partial(_fft3_kernel, final=final),
        grid=(nb, r // N1_BLOCK),
        in_specs=[pl.BlockSpec(g3.shape, lambda b, j: (0, 0)),
                  pl.BlockSpec((None, r, N1_BLOCK, n_ch), lambda b, j: (b, 0, j, 0)),
                  dspec(gate_part), dspec(s_part),
                  pl.BlockSpec((1, n_ch), lambda b, j: (0, 0)),
                  pl.BlockSpec((1, n_ch), lambda b, j: (0, 0))],
        out_specs=pl.BlockSpec((None, r // 2, N1_BLOCK, n_ch), lambda b, j: (b, 0, j, 0)),
        out_shape=jax.ShapeDtypeStruct((nb, r // 2, r, n_ch), F32),
        compiler_params=_cparams(("parallel", "parallel")),
        name="hyena_dft_stage3",
    )(g3, c2, zv, sv, bias.reshape(1, n_ch).astype(F32), norm_g.reshape(1, n_ch).astype(F32))


def _hyena(zh, row_off, batch, seq, conv_w, conv_b, filt, fft_bias, out_g):
    n_ch = zh.shape[1] // 3
    r = int(round(math.sqrt(2 * seq)))
    assert r * r == 2 * seq and r % 16 == 0
    f1, mf, minv, g3 = _dft_tables(r)
    sig = _hyena_filter_signals(seq, *filt, n_ch=n_ch)
    hb1 = _fft_stage1(sig[None], 0, f1, r, n_ch)
    h = _filter_spectrum(hb1, mf, r, n_ch)
    z4 = _shortconv(zh, row_off, conv_w, conv_b, batch, seq)
    s4, s_part = z4, 2
    for o in range(HYENA_ORDER):
        b1 = _fft_stage1(s4, s_part, f1, r, n_ch)
        c2 = _fft_stage2(b1, h, o, mf, minv, r, n_ch)
        final = o == HYENA_ORDER - 1
        s = _fft_stage3(c2, g3, z4, o, s4, s_part, fft_bias[o], out_g, final, r, n_ch)
        s4, s_part = s.reshape(1, batch, seq, n_ch), 0
    return s4.reshape(batch * seq, n_ch)


def _extract_top(s, key, count):
    vals, keys = [], []
    for _ in range(count):
        m = jnp.max(s, axis=0, keepdims=True)
        kmin = jnp.min(jnp.where(s == m, key, jnp.inf), axis=0, keepdims=True)
        s = jnp.where(key == kmin, -jnp.inf, s)
        vals.append(m)
        keys.append(kmin)
    return vals, keys


def _peer_topk_kernel(q_ref, keys_ref, eid_ref, gate_ref):
    t = q_ref.shape[0]
    nk = PEER_NKEYS
    q = q_ref[...]
    row_key = lax.broadcasted_iota(I32, (nk, t), 0).astype(F32)
    tops = []
    for c in range(2):
        s = lax.dot_general(keys_ref[c], q[:, c * nk:(c + 1) * nk], (((1,), (1,)), ((), ())),
                            preferred_element_type=F32)
        tops.append(_extract_top(s, row_key, PEER_TOPK))
    (v1, i1), (v2, i2) = tops
    rows16 = lax.broadcasted_iota(I32, (PEER_TOPK, t), 0)
    v2a = jnp.zeros((PEER_TOPK, t), F32)
    i2a = jnp.zeros((PEER_TOPK, t), F32)
    for j in range(PEER_TOPK):
        v2a = jnp.where(rows16 == j, v2[j], v2a)
        i2a = jnp.where(rows16 == j, i2[j], i2a)
    n_exp = float(nk * nk)
    half = PEER_TOPK // 2
    rows8 = lax.broadcasted_iota(I32, (half, t), 0)
    pos8 = rows8.astype(F32)
    v2h, i2h = v2a[:half], i2a[:half]
    cand = [v1[0] + v2a]
    ckey = [rows16.astype(F32) * n_exp + (i1[0] * float(nk) + i2a)]
    for i in range(1, half):
        cand.append(jnp.where(rows8 < PEER_TOPK // (i + 1), v1[i] + v2h, -jnp.inf))
        ckey.append((pos8 + float(i * PEER_TOPK)) * n_exp + (i1[i] * float(nk) + i2h))
    v1t = jnp.zeros((half, t), F32)
    i1t = jnp.zeros((half, t), F32)
    for r in range(half):
        v1t = jnp.where(rows8 == r, v1[half + r], v1t)
        i1t = jnp.where(rows8 == r, i1[half + r], i1t)
    cand.append(v1t + v2[0])
    ckey.append((pos8 + float(half)) * (PEER_TOPK * n_exp) + (i1t * float(nk) + i2[0]))
    tv, tk_ = _extract_top(jnp.concatenate(cand, axis=0), jnp.concatenate(ckey, axis=0), PEER_TOPK)
    denom = jnp.zeros((1, t), F32)
    es = []
    for k in range(PEER_TOPK):
        e = jnp.exp(tv[k] - tv[0])
        es.append(e)
        denom = denom + e
    eid = jnp.zeros((PEER_TOPK, t), F32)
    gate = jnp.zeros((PEER_TOPK, t), F32)
    for k in range(PEER_TOPK):
        pos = jnp.floor(tk_[k] * (1.0 / n_exp))
        eid = jnp.where(rows16 == k, tk_[k] - pos * n_exp, eid)
        gate = jnp.where(rows16 == k, es[k] / denom, gate)
    eid_ref[...] = eid.astype(I32)
    gate_ref[...] = gate


def _peer_topk(q, keys, tm=256):
    t = q.shape[0]
    tm = min(tm, t)
    out_spec = pl.BlockSpec((PEER_TOPK, tm), lambda i, h: (h, i))
    return pl.pallas_call(
        _peer_topk_kernel,
        grid=(t // tm, PEER_HEADS),
        in_specs=[pl.BlockSpec((tm, 2 * PEER_NKEYS), lambda i, h: (i, h)),
                  pl.BlockSpec((None, 2, PEER_NKEYS, PEER_NKEYS), lambda i, h: (h, 0, 0, 0))],
        out_specs=[out_spec, out_spec],
        out_shape=[jax.ShapeDtypeStruct((PEER_HEADS * PEER_TOPK, t), I32),
                   jax.ShapeDtypeStruct((PEER_HEADS * PEER_TOPK, t), F32)],
        compiler_params=_cparams(("parallel", "parallel")),
        name="peer_topk",
    )(q, keys)


def _pack_table(tab):
    e, d = tab.shape
    assert d == 2 * SUBLANES * LANES
    bits = lax.bitcast_convert_type(tab.astype(BF16), jnp.uint16).astype(U32)
    packed = bits[:, :d // 2] | (bits[:, d // 2:] << 16)
    return packed.reshape(e * SUBLANES, LANES)


def _unpack(w):
    lo = lax.bitcast_convert_type(w << 16, F32)
    hi = lax.bitcast_convert_type(w & jnp.uint32(0xFFFF0000), F32)
    return lo, hi


_BITREV8 = (0, 4, 2, 6, 1, 5, 3, 7)


def _sublane_fold8(parts):
    sub = lax.broadcasted_iota(I32, (2 * SUBLANES, LANES), 0) // 2

    def rolled(a, shift):
        return pltpu.bitcast(pltpu.roll(pltpu.bitcast(a, U32), shift, axis=0), BF16)

    lvl = [parts[_BITREV8[r]] for r in range(8)]
    for shift, mask in ((4, sub < 4), (2, (sub % 4) < 2), (1, (sub % 2) < 1)):
        nxt = []
        for a, b in zip(lvl[0::2], lvl[1::2]):
            nxt.append(jnp.where(mask, a + rolled(a, SUBLANES - shift), b + rolled(b, shift)))
        lvl = nxt
    return lvl[0]


def _pair_ranges(half):
    if half == 0:
        return (0, PEER_WINDOW), (PEER_WINDOW, PEER_PAIRS)
    return (PEER_PAIRS - PEER_WINDOW, PEER_PAIRS), (0, PEER_PAIRS - PEER_WINDOW)


def _overflow(n_low, half):
    return n_low > PEER_WINDOW if half == 0 else n_low < PEER_PAIRS - PEER_WINDOW


def _table_rows(tab_ref, off):
    return _unpack(tab_ref[pl.ds(pl.multiple_of(off, SUBLANES), SUBLANES), :])


def _peer_u_kernel(off_ref, nlow_ref, tab_ref, x_ref, o_ref, extra_ref, *, tm, half):
    lane = lax.broadcasted_iota(I32, (SUBLANES, LANES), 1)
    sub = lax.broadcasted_iota(I32, (SUBLANES, LANES), 0)
    lane_grp = lax.shift_right_logical(lane, 3)
    diag = sub == (lane & (SUBLANES - 1))
    main, rest = _pair_ranges(half)

    def folded(t, p0, p1):
        xb = pltpu.bitcast(x_ref[t], BF16)
        out = []
        for g in range(p0 // SUBLANES, p1 // SUBLANES):
            parts = []
            for r in range(SUBLANES):
                off = pl.multiple_of(off_ref[t, g * SUBLANES + r], SUBLANES)
                parts.append(pltpu.bitcast(tab_ref[pl.ds(off, SUBLANES), :], BF16) * xb)
            out.append(pltpu.bitcast(_sublane_fold8(parts), U32))
        return out

    def lane_sums(folds, p0):
        mat = jnp.zeros((SUBLANES, LANES), F32)
        for i, f in enumerate(folds):
            lo, hi = _unpack(f)
            mat = jnp.where(lane_grp == p0 // SUBLANES + i, jnp.sum(lo + hi, axis=-1, keepdims=True), mat)
        return jnp.sum(jnp.where(diag, mat, 0.0), axis=0, keepdims=True)

    def finish(t, folds):
        o_ref[pl.ds(t, 1), :] = lane_sums(folds, main[0]) + extra_ref[pl.ds(t, 1), :]

    def token(t, prev):
        cur = folded(t, *main)
        finish(jnp.maximum(t - 1, 0), prev)
        extra_ref[pl.ds(t, 1), :] = jnp.zeros((1, LANES), F32)

        @pl.when(_overflow(nlow_ref[0, t], half))
        def _():
            extra_ref[pl.ds(t, 1), :] = lane_sums(folded(t, *rest), rest[0])

        return tuple(cur)

    extra_ref[pl.ds(0, 1), :] = jnp.zeros((1, LANES), F32)
    zeros = tuple(jnp.zeros((SUBLANES, LANES), U32) for _ in range((main[1] - main[0]) // SUBLANES))
    last = lax.fori_loop(0, tm, token, zeros)
    finish(tm - 1, last)


def _peer_u(off, nlow3, tab, x4, half, tm=128):
    t = off.shape[0]
    rows = PEER_HALF * SUBLANES
    return pl.pallas_call(
        functools.partial(_peer_u_kernel, tm=tm, half=half),
        grid=(t // tm,),
        in_specs=[pl.BlockSpec((tm, LANES), lambda i: (i, 0), memory_space=pltpu.SMEM),
                  pl.BlockSpec((None, 1, tm), lambda i: (i, 0, 0), memory_space=pltpu.SMEM),
                  pl.BlockSpec((rows, LANES), lambda i: (half, 0), pipeline_mode=pl.Buffered(1)),
                  pl.BlockSpec((tm, SUBLANES, LANES), lambda i: (i, 0, 0))],
        out_specs=pl.BlockSpec((tm, LANES), lambda i: (i, 0)),
        out_shape=jax.ShapeDtypeStruct((t, LANES), F32),
        scratch_shapes=[pltpu.VMEM((tm, LANES), F32)],
        compiler_params=_cparams(("arbitrary",)),
        name="peer_expert_scores",
    )(off, nlow3, tab, x4)


def _peer_coef_kernel(a0_ref, a1_ref, eid_ref, gate_ref, c0_ref, c1_ref):
    low = eid_ref[...] < PEER_HALF
    a = jnp.where(low, a0_ref[...], a1_ref[...])
    coef = gate_ref[...] * (0.5 * a * (1.0 + lax.erf(a * (1.0 / math.sqrt(2.0)))))
    c0_ref[...] = jnp.where(low, coef, 0.0)
    c1_ref[...] = jnp.where(low, 0.0, coef)


def _peer_coef(a0, a1, eid, gate, tm=1024):
    t = eid.shape[0]
    tm = min(tm, t)
    spec = pl.BlockSpec((tm, LANES), lambda i: (i, 0))
    return pl.pallas_call(
        _peer_coef_kernel,
        grid=(t // tm,),
        in_specs=[spec, spec, spec, spec],
        out_specs=[spec, spec],
        out_shape=[jax.ShapeDtypeStruct((t, LANES), F32)] * 2,
        compiler_params=_cparams(("parallel",)),
        name="peer_coef",
    )(a0, a1, eid, gate)


def _peer_v_kernel(off_ref, nlow_ref, coef_ref, tab_ref, base_ref, o_ref, *, tm, half):
    n_acc = 4
    main, rest = _pair_ranges(half)

    def store_row(t, lo, hi):
        o_ref[t, 0] = base_ref[t, 0] + lo
        o_ref[t, 1] = base_ref[t, 1] + hi

    def weighted(t, p0, p1):
        acc_lo = [jnp.zeros((SUBLANES, LANES), F32) for _ in range(n_acc)]
        acc_hi = [jnp.zeros((SUBLANES, LANES), F32) for _ in range(n_acc)]
        for p in range(p0, p1):
            c = coef_ref[t, p]
            lo, hi = _table_rows(tab_ref, off_ref[t, p])
            acc_lo[p % n_acc] = acc_lo[p % n_acc] + c * lo
            acc_hi[p % n_acc] = acc_hi[p % n_acc] + c * hi
        return ((acc_lo[0] + acc_lo[1]) + (acc_lo[2] + acc_lo[3]),
                (acc_hi[0] + acc_hi[1]) + (acc_hi[2] + acc_hi[3]))

    def token(t, carry):
        lo, hi = weighted(t, *main)
        store_row(t, lo, hi)

        @pl.when(_overflow(nlow_ref[0, t], half))
        def _():
            lo2, hi2 = weighted(t, *rest)
            store_row(t, lo + lo2, hi + hi2)

        return carry

    lax.fori_loop(0, tm, token, 0)


def _peer_v(off, nlow3, coef, tab, base, half, tm=128):
    t = base.shape[0]
    rows = PEER_HALF * SUBLANES
    smem = pl.BlockSpec((tm, LANES), lambda i: (i, 0), memory_space=pltpu.SMEM)
    tile = pl.BlockSpec((tm, 2, SUBLANES, LANES), lambda i: (i, 0, 0, 0))
    return pl.pallas_call(
        functools.partial(_peer_v_kernel, tm=tm, half=half),
        grid=(t // tm,),
        in_specs=[smem,
                  pl.BlockSpec((None, 1, tm), lambda i: (i, 0, 0), memory_space=pltpu.SMEM),
                  smem,
                  pl.BlockSpec((rows, LANES), lambda i: (half, 0), pipeline_mode=pl.Buffered(1)),
                  tile],
        out_specs=tile,
        out_shape=jax.ShapeDtypeStruct((t, 2, SUBLANES, LANES), F32),
        compiler_params=_cparams(("arbitrary",)),
        name="peer_expert_sum",
    )(off, nlow3, coef, tab, base)


def _peer(x1, norm2_g, wq_bf, keys_bf, u_packed, v_packed, tm=128):
    t, d = x1.shape
    tm = min(tm, t)
    (xn_bf,) = _rmsnorm(x1, norm2_g, (BF16,))
    q = _matmul(xn_bf, wq_bf, wq_bf.shape[1], 0, BF16)
    eid_t, gate_t = _peer_topk(q, keys_bf)
    eid, gate = eid_t.T, gate_t.T
    high = eid >= PEER_HALF
    order = jnp.argsort(high, axis=-1, stable=True)
    eid = jnp.take_along_axis(eid, order, axis=-1)
    gate = jnp.take_along_axis(gate, order, axis=-1)
    nlow3 = (PEER_PAIRS - jnp.sum(high, axis=-1, dtype=I32)).reshape(t // tm, 1, tm)
    off = (eid & (PEER_HALF - 1)) * SUBLANES
    xw = _pack_table(xn_bf).reshape(t, SUBLANES, LANES)
    a0 = _peer_u(off, nlow3, u_packed, xw, 0, tm)
    a1 = _peer_u(off, nlow3, u_packed, xw, 1, tm)
    c0, c1 = _peer_coef(a0, a1, eid, gate)
    y = _peer_v(off, nlow3, c0, v_packed, x1.reshape(t, 2, SUBLANES, LANES), 0, tm)
    return _peer_v(off, nlow3, c1, v_packed, y, 1, tm).reshape(t, d)


def kernel(x_prompt, x_sample, norm1_g, w_in, q_norm_g, k_norm_g, lambda_q1, lambda_k1, lambda_q2, lambda_k2, attn_sub_g, conv_w, conv_b, filt_w1, filt_b1, filt_w2, filt_b2, filt_w3, filt_b3, filt_w4, filt_freq, fft_bias, hyena_out_g, w_out, norm2_g, peer_wq, peer_keys, peer_u, peer_v):
    depth = w_in.shape[0]
    d_model = x_prompt.shape[-1]
    att_w = ATT_HEADS * ATT_VDIM
    shapes = [x_prompt.shape[:2], x_sample.shape[:2]]
    x = jnp.concatenate([x_prompt.reshape(-1, d_model), x_sample.reshape(-1, d_model)], axis=0)
    slopes = 2.0 ** (-8.0 * jnp.arange(1, ATT_HEADS + 1, dtype=F32) / ATT_HEADS)

    for l in range(depth):
        lambda_init = 0.8 - 0.6 * math.exp(-0.3 * l)
        lam = (jnp.exp(jnp.sum(lambda_q1[l].astype(F32) * lambda_k1[l].astype(F32)))
               - jnp.exp(jnp.sum(lambda_q2[l].astype(F32) * lambda_k2[l].astype(F32)))
               + lambda_init).reshape(1)
        w_in_bf = w_in[l].astype(BF16)
        q_gain = jnp.tile(q_norm_g[l].astype(F32), 2 * ATT_HEADS) * (ATT_QKDIM ** -0.5 * LOG2E)
        k_gain = jnp.tile(k_norm_g[l].astype(F32), 2 * ATT_HEADS)
        qk_gain = jnp.concatenate([q_gain, k_gain]).reshape(1, 2 * att_w)

        (h_bf,) = _rmsnorm(x, norm1_g[l], (BF16,))
        qk = _matmul(h_bf, w_in_bf, 2 * att_w, 0, BF16, mode="qknorm", extra=qk_gain)
        vt = _matmul_nt(w_in_bf[:, 2 * att_w:3 * att_w].T, h_bf, BF16)
        zh = _matmul(h_bf, w_in_bf, w_in.shape[2] - 3 * att_w, 3 * att_w, F32)

        filt = (filt_w1[l], filt_b1[l], filt_w2[l], filt_b2[l], filt_w3[l], filt_b3[l], filt_w4[l], filt_freq[l])
        mixes, row = [], 0
        for (b, s) in shapes:
            n = b * s
            att = _attention(qk, vt, row, slopes, lam, attn_sub_g[l], b, s, 1.0 - lambda_init)
            hy = _hyena(zh, row, b, s, conv_w[l], conv_b[l], filt, fft_bias[l], hyena_out_g[l])
            mixes.append(jnp.concatenate([att, hy.astype(BF16)], axis=-1))
            row += n
        mix = jnp.concatenate(mixes, axis=0)
        x = _matmul(mix, w_out[l].astype(BF16), d_model, 0, F32, mode="residual", extra=x)

        x = _peer(x, norm2_g[l], peer_wq[l].astype(BF16), peer_keys[l].astype(BF16),
                  _pack_table(peer_u[l]), _pack_table(peer_v[l]))

    n0 = shapes[0][0] * shapes[0][1]
    return (x[:n0].reshape(x_prompt.shape), x[n0:].reshape(x_sample.shape))
```

```python
import functools
import math

import jax
import jax.numpy as jnp
from jax import lax
from jax.experimental import pallas as pl
from jax.experimental.pallas import tpu as pltpu

F32 = jnp.float32
BF16 = jnp.bfloat16
I32 = jnp.int32
U32 = jnp.uint32

RMS_EPS = 1e-6
LOG2E = 1.4426950408889634
LANES = 128
SUBLANES = 8
VMEM_LIMIT_BYTES = 56 * 1024 * 1024

ATT_HEADS = 8
ATT_VDIM = 128
ATT_QKDIM = 64
HYENA_ORDER = 2
FILTER_BANDS = 16
DECAY_FAST = 0.3
DECAY_SLOW = 1.5
DECAY_TARGET = 1e-2
PEER_HEADS = 8
PEER_NKEYS = 128
PEER_TOPK = 16
PEER_HALF = PEER_NKEYS * PEER_NKEYS // 2
PEER_PAIRS = PEER_HEADS * PEER_TOPK
PEER_WINDOW = 80


def _cparams(sem, vmem=VMEM_LIMIT_BYTES):
    return pltpu.CompilerParams(dimension_semantics=sem, vmem_limit_bytes=vmem)


def _rmsnorm_kernel(x_ref, g_ref, *o_refs):
    x = x_ref[...]
    ms = jnp.mean(x * x, axis=-1, keepdims=True)
    y = x * lax.rsqrt(ms + RMS_EPS) * g_ref[...]
    for o_ref in o_refs:
        o_ref[...] = y.astype(o_ref.dtype)


def _rmsnorm(x, g, out_dtypes, tm=512):
    t, d = x.shape
    tm = min(tm, t)
    spec = pl.BlockSpec((tm, d), lambda i: (i, 0))
    return pl.pallas_call(
        _rmsnorm_kernel,
        grid=(t // tm,),
        in_specs=[spec, pl.BlockSpec((1, d), lambda i: (0, 0))],
        out_specs=[spec for _ in out_dtypes],
        out_shape=[jax.ShapeDtypeStruct((t, d), dt) for dt in out_dtypes],
        compiler_params=_cparams(("parallel",)),
        name="rmsnorm",
    )(x, g.reshape(1, d).astype(F32))


def _group_rms_scale(x, gain):
    lane = lax.broadcasted_iota(I32, x.shape, 1)
    lo = lane < ATT_QKDIM
    x2 = x * x
    s_lo = jnp.sum(jnp.where(lo, x2, 0.0), axis=-1, keepdims=True)
    s_hi = jnp.sum(jnp.where(lo, 0.0, x2), axis=-1, keepdims=True)
    ms = jnp.where(lo, s_lo, s_hi) * (1.0 / ATT_QKDIM)
    return x * lax.rsqrt(ms + RMS_EPS) * gain


def _mm_kernel(a_ref, b_ref, *rest, mode):
    acc = jnp.dot(a_ref[...], b_ref[...], preferred_element_type=F32)
    if mode == "plain":
        (o_ref,) = rest
        o_ref[...] = acc.astype(o_ref.dtype)
    elif mode == "residual":
        r_ref, o_ref = rest
        o_ref[...] = (acc + r_ref[...]).astype(o_ref.dtype)
    elif mode == "qknorm":
        g_ref, o_ref = rest
        for c in range(acc.shape[1] // LANES):
            sl = slice(c * LANES, (c + 1) * LANES)
            o_ref[:, sl] = _group_rms_scale(acc[:, sl], g_ref[:, sl]).astype(o_ref.dtype)
    else:
        raise ValueError(mode)


def _matmul(a, b, n_cols, col_off, out_dtype, mode="plain", extra=None, tm=1024, tn=1024):
    m, k = a.shape
    tm = min(tm, m)
    tn = min(tn, n_cols)
    assert col_off % tn == 0 and n_cols % tn == 0 and m % tm == 0
    off = col_off // tn
    in_specs = [pl.BlockSpec((tm, k), lambda i, j: (i, 0)),
                pl.BlockSpec((k, tn), lambda i, j: (0, j + off))]
    args = [a, b]
    if mode == "residual":
        in_specs.append(pl.BlockSpec((tm, tn), lambda i, j: (i, j)))
        args.append(extra)
    elif mode == "qknorm":
        in_specs.append(pl.BlockSpec((1, tn), lambda i, j: (0, j)))
        args.append(extra)
    return pl.pallas_call(
        functools.partial(_mm_kernel, mode=mode),
        grid=(m // tm, n_cols // tn),
        in_specs=in_specs,
        out_specs=pl.BlockSpec((tm, tn), lambda i, j: (i, j)),
        out_shape=jax.ShapeDtypeStruct((m, n_cols), out_dtype),
        compiler_params=_cparams(("parallel", "arbitrary")),
        name="matmul_" + mode,
    )(*args)


def _mm_nt_kernel(w_ref, a_ref, o_ref):
    o_ref[...] = lax.dot_general(w_ref[...], a_ref[...], (((1,), (1,)), ((), ())),
                                 preferred_element_type=F32).astype(o_ref.dtype)


def _matmul_nt(wt, a, out_dtype, tm=1024):
    n, k = wt.shape
    m = a.shape[0]
    tm = min(tm, m)
    return pl.pallas_call(
        _mm_nt_kernel,
        grid=(m // tm,),
        in_specs=[pl.BlockSpec((n, k), lambda i: (0, 0)), pl.BlockSpec((tm, k), lambda i: (i, 0))],
        out_specs=pl.BlockSpec((n, tm), lambda i: (0, i)),
        out_shape=jax.ShapeDtypeStruct((n, m), out_dtype),
        compiler_params=_cparams(("parallel",)),
        name="matmul_nt",
    )(wt, a)


def _alibi_columns(slopes, tq, tk):
    return [_alibi_side(slopes, tq, True), _alibi_side(slopes, tk, False)]


def _alibi_side(slopes, n, query_side):
    pos = jnp.arange(n, dtype=F32)
    val = (slopes.astype(F32) * LOG2E)[:, None] * pos[None, :]

    def pieces(x):
        p1 = x.astype(BF16)
        r1 = x - p1.astype(F32)
        p2 = r1.astype(BF16)
        p3 = (r1 - p2.astype(F32)).astype(BF16)
        return [p1, p2, p3]

    ones = [jnp.ones_like(val, BF16)] * 3
    six = jnp.stack(pieces(-val) + ones if query_side else ones + pieces(val), axis=-1)
    pad = jnp.zeros(val.shape + (ATT_QKDIM - 6,), BF16)
    return jnp.concatenate([six, pad, six, pad], axis=-1)


def _attn_kernel(slope_ref, lam_ref, q_ref, k_ref, vt_ref, aq_ref, ak_ref, g_ref, o_ref,
                 m_ref, l_ref, acc_ref, *, tq, tk, nk, hb, out_scale):
    hg = pl.program_id(1)
    i = pl.program_id(2)
    j = pl.program_id(3)

    @pl.when(j == 0)
    def _():
        m_ref[...] = jnp.full(m_ref.shape, -jnp.inf, F32)
        l_ref[...] = jnp.zeros(l_ref.shape, F32)
        acc_ref[...] = jnp.zeros(acc_ref.shape, F32)

    q_first = lax.broadcasted_iota(I32, (tq, LANES), 1) < ATT_QKDIM
    k_first = lax.broadcasted_iota(I32, (tk, LANES), 1) < ATT_QKDIM
    nt = (((1,), (1,)), ((), ()))

    def update(hh, scores, shift):
        vt = vt_ref[hh * LANES:(hh + 1) * LANES, :]
        for c, s in enumerate(scores):
            m_old = m_ref[hh, c]
            m_new = jnp.maximum(m_old, jnp.max(s, axis=0, keepdims=True) + shift)
            alpha = jnp.exp2(m_old - m_new)
            p = jnp.exp2(s - (m_new - shift))
            l_ref[hh, c] = alpha * l_ref[hh, c] + jnp.sum(p, axis=0, keepdims=True)
            acc_ref[hh, c] = alpha * acc_ref[hh, c] + jnp.dot(vt, p.astype(BF16),
                                                              preferred_element_type=F32)
            m_ref[hh, c] = m_new

    keys_before = i * tq >= (j + 1) * tk
    keys_after = (i + 1) * tq <= j * tk
    off_diagonal = jnp.logical_or(keys_before, keys_after)

    @pl.when(off_diagonal)
    def _():
        sign = jnp.where(keys_after, -1.0, 1.0).astype(BF16)
        gap = jnp.abs(i * tq - j * tk).astype(F32)
        for hh in range(hb):
            sl = slice(hh * LANES, (hh + 1) * LANES)
            q = q_ref[:, sl]
            k = k_ref[:, sl]
            aq = aq_ref[hh]
            ak = ak_ref[hh] * sign
            s0 = lax.dot_general(jnp.where(k_first, k, ak), jnp.where(q_first, q, aq), nt,
                                 preferred_element_type=F32)
            s1 = lax.dot_general(jnp.where(k_first, ak, k), jnp.where(q_first, aq, q), nt,
                                 preferred_element_type=F32)
            update(hh, (s0, s1), -(slope_ref[hg * hb + hh] * LOG2E) * gap)

    @pl.when(jnp.logical_not(off_diagonal))
    def _():
        kpos = lax.broadcasted_iota(I32, (tk, tq), 0) + j * tk
        qpos = lax.broadcasted_iota(I32, (tk, tq), 1) + i * tq
        dist = jnp.abs(kpos - qpos).astype(F32)
        for hh in range(hb):
            sl = slice(hh * LANES, (hh + 1) * LANES)
            q = q_ref[:, sl]
            k = k_ref[:, sl]
            zero = jnp.zeros_like(q)
            bias = dist * (-(slope_ref[hg * hb + hh] * LOG2E))
            s0 = lax.dot_general(k, jnp.where(q_first, q, zero), nt, preferred_element_type=F32) + bias
            s1 = lax.dot_general(k, jnp.where(q_first, zero, q), nt, preferred_element_type=F32) + bias
            update(hh, (s0, s1), 0.0)

    @pl.when(j == nk - 1)
    def _():
        for hh in range(hb):
            o = acc_ref[hh, 0] / l_ref[hh, 0] - lam_ref[0] * (acc_ref[hh, 1] / l_ref[hh, 1])
            ms = jnp.mean(o * o, axis=0, keepdims=True)
            y = o * lax.rsqrt(ms + RMS_EPS) * (g_ref[...] * out_scale)
            o_ref[:, hh * LANES:(hh + 1) * LANES] = y.T.astype(o_ref.dtype)


def _attention(qk, vt, row_off, slopes, lam, sub_g, batch, seq, out_scale, tq=512, tk=1024, hb=2):
    tq = min(tq, seq)
    tk = min(tk, seq // 4)
    nq, nk = seq // tq, seq // tk
    assert row_off % tq == 0 and row_off % tk == 0 and ATT_HEADS % hb == 0
    oq, ok = row_off // tq, row_off // tk
    ng = ATT_HEADS // hb
    aq, ak = _alibi_columns(slopes, tq, tk)
    kern = functools.partial(_attn_kernel, tq=tq, tk=tk, nk=nk, hb=hb, out_scale=out_scale)
    smem = pl.BlockSpec(memory_space=pltpu.SMEM)
    return pl.pallas_call(
        kern,
        grid=(batch, ng, nq, nk),
        in_specs=[smem, smem,
                  pl.BlockSpec((tq, hb * LANES), lambda b, h, i, j: (oq + b * nq + i, h)),
                  pl.BlockSpec((tk, hb * LANES), lambda b, h, i, j: (ok + b * nk + j, ng + h)),
                  pl.BlockSpec((hb * LANES, tk), lambda b, h, i, j: (h, ok + b * nk + j)),
                  pl.BlockSpec((hb, tq, LANES), lambda b, h, i, j: (h, 0, 0)),
                  pl.BlockSpec((hb, tk, LANES), lambda b, h, i, j: (h, 0, 0)),
                  pl.BlockSpec((LANES, 1), lambda b, h, i, j: (0, 0))],
        out_specs=pl.BlockSpec((tq, hb * LANES), lambda b, h, i, j: (b * nq + i, h)),
        out_shape=jax.ShapeDtypeStruct((batch * seq, ATT_HEADS * ATT_VDIM), BF16),
        scratch_shapes=[pltpu.VMEM((hb, 2, 1, tq), F32), pltpu.VMEM((hb, 2, 1, tq), F32),
                        pltpu.VMEM((hb, 2, LANES, tq), F32)],
        compiler_params=_cparams(("parallel", "parallel", "parallel", "arbitrary")),
        name="diff_attention",
    )(slopes, lam, qk, qk, vt, aq, ak, sub_g.reshape(LANES, 1).astype(F32))


def _shortconv_kernel(z_ref, w_ref, b_ref, o_ref):
    z = z_ref[...]
    n = z.shape[0]
    row = lax.broadcasted_iota(I32, z.shape, 0)
    prev = jnp.where(row == 0, 0.0, pltpu.roll(z, 1, axis=0))
    nxt = jnp.where(row == n - 1, 0.0, pltpu.roll(z, n - 1, axis=0))
    w = w_ref[...]
    o_ref[...] = prev * w[0:1] + z * w[1:2] + nxt * w[2:3] + b_ref[...]


def _shortconv(zh, row_off, conv_w, conv_b, batch, seq, cb=256):
    c3 = zh.shape[1]
    c = c3 // 3
    ncb = c // cb
    assert row_off % seq == 0
    ob = row_off // seq
    out = pl.pallas_call(
        _shortconv_kernel,
        grid=(batch, 3, ncb),
        in_specs=[pl.BlockSpec((seq, cb), lambda b, p, j: (ob + b, p * ncb + j)),
                  pl.BlockSpec((3, cb), lambda b, p, j: (0, p * ncb + j)),
                  pl.BlockSpec((1, cb), lambda b, p, j: (0, p * ncb + j))],
        out_specs=pl.BlockSpec((None, None, seq, cb), lambda b, p, j: (p, b, 0, j)),
        out_shape=jax.ShapeDtypeStruct((3, batch, seq, c), F32),
        compiler_params=_cparams(("parallel", "parallel", "parallel")),
        name="hyena_shortconv",
    )(zh, conv_w.astype(F32), conv_b.reshape(1, c3).astype(F32))
    return out


def _filter_kernel(z_ref, w1_ref, b1_ref, w2_ref, b2_ref, w3_ref, b3_ref, fr_ref, w4_ref,
                   t_ref, d_ref, o_ref, h_ref, *, tl):
    i = pl.program_id(0)
    g = pl.program_id(1)
    hi = lax.Precision.HIGHEST

    @pl.when(g == 0)
    def _():
        fr = fr_ref[...]
        h = jnp.sin(fr * (jnp.dot(z_ref[...], w1_ref[...], precision=hi, preferred_element_type=F32) + b1_ref[...]))
        h = jnp.sin(fr * (jnp.dot(h, w2_ref[...], precision=hi, preferred_element_type=F32) + b2_ref[...]))
        h_ref[...] = jnp.sin(fr * (jnp.dot(h, w3_ref[...], precision=hi, preferred_element_type=F32) + b3_ref[...]))

    f = jnp.dot(h_ref[...].astype(BF16), w4_ref[...], preferred_element_type=F32)
    f = f * jnp.exp(-t_ref[...] * d_ref[...])
    row = lax.broadcasted_iota(I32, f.shape, 0) + i * tl
    drop = jnp.logical_and(row == 0, g % 2 == 1)
    o_ref[...] = jnp.where(drop, 0.0, f)


def _hyena_filter_signals(seq, w1, b1, w2, b2, w3, b3, w4, freq, n_ch, tl=512):
    t = jnp.linspace(0.0, 1.0, seq, dtype=F32)[:, None]
    w = 2.0 * math.pi * jnp.arange(seq, dtype=F32)[:, None] / seq
    f = jnp.linspace(1e-4, FILTER_BANDS - 1, FILTER_BANDS, dtype=F32)[None, :]
    z = jnp.concatenate([t, jnp.cos(f * w), -jnp.sin(f * w)], axis=-1)
    deltas = jnp.abs(jnp.linspace(math.log(DECAY_FAST) / DECAY_TARGET,
                                  math.log(DECAY_SLOW) / DECAY_TARGET, n_ch, dtype=F32))[None, :]
    hid = w1.shape[1]
    emb = LANES
    z = jnp.pad(z, ((0, 0), (0, emb - z.shape[1])))
    w1 = jnp.pad(w1.astype(F32), ((0, emb - w1.shape[0]), (0, 0)))
    tl = min(tl, seq)
    full = lambda shape: pl.BlockSpec(shape, lambda i, g: tuple(0 for _ in shape))
    return pl.pallas_call(
        functools.partial(_filter_kernel, tl=tl),
        grid=(seq // tl, 2 * HYENA_ORDER),
        in_specs=[pl.BlockSpec((tl, emb), lambda i, g: (i, 0)),
                  full((emb, hid)), full((1, hid)), full((hid, hid)), full((1, hid)),
                  full((hid, hid)), full((1, hid)), full((1, hid)),
                  pl.BlockSpec((hid, n_ch), lambda i, g: (0, g)),
                  pl.BlockSpec((tl, 1), lambda i, g: (i, 0)),
                  full((1, n_ch))],
        out_specs=pl.BlockSpec((None, tl, n_ch), lambda i, g: (g, i, 0)),
        out_shape=jax.ShapeDtypeStruct((2 * HYENA_ORDER, seq, n_ch), F32),
        scratch_shapes=[pltpu.VMEM((tl, hid), F32)],
        compiler_params=_cparams(("parallel", "arbitrary")),
        name="hyena_filter_mlp",
    )(z, w1.astype(F32), b1.reshape(1, hid).astype(F32), w2.astype(F32), b2.reshape(1, hid).astype(F32),
      w3.astype(F32), b3.reshape(1, hid).astype(F32), freq.reshape(1, hid).astype(F32), w4.astype(BF16),
      t, deltas)


def _dft_tables(r):
    n = r * r
    k2 = jnp.arange(r, dtype=I32)
    n2 = jnp.arange(r // 2, dtype=I32)
    ang1 = (2.0 * math.pi / r) * ((k2[:, None] * n2[None, :]) % r).astype(F32)
    f1 = jnp.concatenate([jnp.cos(ang1), -jnp.sin(ang1)], axis=0)
    k1 = jnp.arange(r, dtype=I32)
    n1 = jnp.arange(r, dtype=I32)
    kk = r * k1[None, :, None] + k2[:, None, None]
    ang2 = (2.0 * math.pi / n) * ((kk * n1[None, None, :]) % n).astype(F32)
    mr, mi_ = jnp.cos(ang2), -jnp.sin(ang2)
    mf = jnp.concatenate([jnp.concatenate([mr, -mi_], axis=2),
                          jnp.concatenate([mi_, mr], axis=2)], axis=1)
    minv = jnp.swapaxes(mf, 1, 2)
    g3 = jnp.concatenate([jnp.cos(ang1.T), -jnp.sin(ang1.T)], axis=1) * (1.0 / n)
    eye = jnp.eye(N1_BLOCK, dtype=F32)
    f1, g3 = jnp.kron(f1, eye), jnp.kron(g3, eye)
    return f1.astype(BF16), mf.astype(BF16), minv.astype(BF16), g3.astype(BF16)


N1_BLOCK = SUBLANES


def _pack_complex(re, im):
    rb = lax.bitcast_convert_type(re.astype(BF16).astype(F32), U32)
    ib = lax.bitcast_convert_type(im.astype(BF16).astype(F32), U32)
    return (rb >> 16) | ib


def _unpack_complex_rows(w):
    re, im = _unpack(w)
    return jnp.concatenate([re, im], axis=0).astype(BF16)


def _fft1_kernel(f_ref, x_ref, o_ref, *, r):
    n_ch = x_ref.shape[-1]
    x = x_ref[...].reshape((r // 2) * N1_BLOCK, n_ch).astype(BF16)
    res = jnp.dot(f_ref[...], x, preferred_element_type=F32)
    half = r * N1_BLOCK
    o_ref[...] = _pack_complex(res[:half], res[half:]).reshape(r, N1_BLOCK, n_ch)


def _fft_stage1(x4, part, f1, r, n_ch):
    nb = x4.shape[1]
    xv = x4.reshape(x4.shape[0], nb, r // 2, r, n_ch)
    return pl.pallas_call(
        functools.partial(_fft1_kernel, r=r),
        grid=(nb, r // N1_BLOCK),
        in_specs=[pl.BlockSpec(f1.shape, lambda b, j: (0, 0)),
                  pl.BlockSpec((None, None, r // 2, N1_BLOCK, n_ch), lambda b, j: (part, b, 0, j, 0))],
        out_specs=pl.BlockSpec((None, r, N1_BLOCK, n_ch), lambda b, j: (b, 0, j, 0)),
        out_shape=jax.ShapeDtypeStruct((nb, r, r, n_ch), U32),
        compiler_params=_cparams(("parallel", "parallel")),
        name="hyena_dft_stage1",
    )(f1, xv)


def _k2_block(r):
    return max(1, (2 * LANES) // r)


def _filter_spec_kernel(mf_ref, bf_ref, bb_ref, o_ref, *, r):
    for kk in range(mf_ref.shape[0]):
        m = mf_ref[kk]
        xf = jnp.dot(m, _unpack_complex_rows(bf_ref[kk]), preferred_element_type=F32)
        xb = jnp.dot(m, _unpack_complex_rows(bb_ref[kk]), preferred_element_type=F32)
        o_ref[kk] = _pack_complex(xf[:r] + xb[:r], xf[r:] - xb[r:])


def _filter_spectrum(b1, mf, r, n_ch):
    kb = _k2_block(r)
    blk = lambda sel: pl.BlockSpec((None, kb, r, n_ch), lambda o, k: (2 * o + sel, k, 0, 0))
    return pl.pallas_call(
        functools.partial(_filter_spec_kernel, r=r),
        grid=(HYENA_ORDER, r // kb),
        in_specs=[pl.BlockSpec((kb, 2 * r, 2 * r), lambda o, k: (k, 0, 0)), blk(0), blk(1)],
        out_specs=pl.BlockSpec((None, kb, r, n_ch), lambda o, k: (o, k, 0, 0)),
        out_shape=jax.ShapeDtypeStruct((HYENA_ORDER, r, r, n_ch), U32),
        compiler_params=_cparams(("parallel", "parallel")),
        name="hyena_filter_spectrum",
    )(mf, b1, b1)


def _fft2_kernel(mf_ref, mi_ref, b_ref, h_ref, o_ref, *, r):
    for kk in range(mf_ref.shape[0]):
        x = jnp.dot(mf_ref[kk], _unpack_complex_rows(b_ref[kk]), preferred_element_type=F32)
        xr, xi = x[:r], x[r:]
        hr, hi = _unpack(h_ref[kk])
        y = jnp.concatenate([xr * hr - xi * hi, xr * hi + xi * hr], axis=0).astype(BF16)
        c = jnp.dot(mi_ref[kk], y, preferred_element_type=F32)
        o_ref[kk] = _pack_complex(c[:r], c[r:])


def _fft_stage2(b1, h, order, mf, minv, r, n_ch):
    nb = b1.shape[0]
    kb = _k2_block(r)
    blk = pl.BlockSpec((None, kb, r, n_ch), lambda b, k: (b, k, 0, 0))
    return pl.pallas_call(
        functools.partial(_fft2_kernel, r=r),
        grid=(nb, r // kb),
        in_specs=[pl.BlockSpec((kb, 2 * r, 2 * r), lambda b, k: (k, 0, 0)),
                  pl.BlockSpec((kb, 2 * r, 2 * r), lambda b, k: (k, 0, 0)),
                  blk,
                  pl.BlockSpec((None, kb, r, n_ch), lambda b, k: (order, k, 0, 0))],
        out_specs=blk,
        out_shape=jax.ShapeDtypeStruct((nb, r, r, n_ch), U32),
        compiler_params=_cparams(("parallel", "parallel")),
        name="hyena_dft_stage2",
    )(mf, minv, b1, h)


def _fft3_kernel(g_ref, c_ref, gate_ref, s_ref, bias_ref, ng_ref, o_ref, *, final):
    r, nb, n_ch = c_ref.shape
    cc = _unpack_complex_rows(c_ref[...].reshape(r * nb, n_ch))
    y = jnp.dot(g_ref[...], cc, preferred_element_type=F32)
    rows = (r // 2) * nb
    s_new = gate_ref[...].reshape(rows, n_ch) * (y + s_ref[...].reshape(rows, n_ch) * bias_ref[...])
    if final:
        ms = jnp.mean(s_new * s_new, axis=-1, keepdims=True)
        s_new = s_new * lax.rsqrt(ms + RMS_EPS) * ng_ref[...]
    o_ref[...] = s_new.reshape(r // 2, nb, n_ch)


def _fft_stage3(c2, g3, z4, gate_part, s4, s_part, bias, norm_g, final, r, n_ch):
    nb = c2.shape[0]
    zv = z4.reshape(z4.shape[0], nb, r // 2, r, n_ch)
    sv = s4.reshape(s4.shape[0], nb, r // 2, r, n_ch)
    dspec = lambda part: pl.BlockSpec((None, None, r // 2, N1_BLOCK, n_ch), lambda b, j: (part, b, 0, j, 0))
    return pl.pallas_call(
        functools.partial(_fft3_kernel, final=final),
        grid=(nb, r // N1_BLOCK),
        in_specs=[pl.BlockSpec(g3.shape, lambda b, j: (0, 0)),
                  pl.BlockSpec((None, r, N1_BLOCK, n_ch), lambda b, j: (b, 0, j, 0)),
                  dspec(gate_part), dspec(s_part),
                  pl.BlockSpec((1, n_ch), lambda b, j: (0, 0)),
                  pl.BlockSpec((1, n_ch), lambda b, j: (0, 0))],
        out_specs=pl.BlockSpec((None, r // 2, N1_BLOCK, n_ch), lambda b, j: (b, 0, j, 0)),
        out_shape=jax.ShapeDtypeStruct((nb, r // 2, r, n_ch), F32),
        compiler_params=_cparams(("parallel", "parallel")),
        name="hyena_dft_stage3",
    )(g3, c2, zv, sv, bias.reshape(1, n_ch).astype(F32), norm_g.reshape(1, n_ch).astype(F32))


def _hyena(zh, row_off, batch, seq, conv_w, conv_b, filt, fft_bias, out_g):
    n_ch = zh.shape[1] // 3
    r = int(round(math.sqrt(2 * seq)))
    assert r * r == 2 * seq and r % 16 == 0
    f1, mf, minv, g3 = _dft_tables(r)
    sig = _hyena_filter_signals(seq, *filt, n_ch=n_ch)
    hb1 = _fft_stage1(sig[None], 0, f1, r, n_ch)
    h = _filter_spectrum(hb1, mf, r, n_ch)
    z4 = _shortconv(zh, row_off, conv_w, conv_b, batch, seq)
    s4, s_part = z4, 2
    for o in range(HYENA_ORDER):
        b1 = _fft_stage1(s4, s_part, f1, r, n_ch)
        c2 = _fft_stage2(b1, h, o, mf, minv, r, n_ch)
        final = o == HYENA_ORDER - 1
        s = _fft_stage3(c2, g3, z4, o, s4, s_part, fft_bias[o], out_g, final, r, n_ch)
        s4, s_part = s.reshape(1, batch, seq, n_ch), 0
    return s4.reshape(batch * seq, n_ch)


def _extract_top(s, key, count):
    vals, keys = [], []
    for _ in range(count):
        m = jnp.max(s, axis=0, keepdims=True)
        kmin = jnp.min(jnp.where(s == m, key, jnp.inf), axis=0, keepdims=True)
        s = jnp.where(key == kmin, -jnp.inf, s)
        vals.append(m)
        keys.append(kmin)
    return vals, keys


def _peer_topk_kernel(q_ref, keys_ref, eid_ref, gate_ref):
    t = q_ref.shape[0]
    nk = PEER_NKEYS
    q = q_ref[...]
    row_key = lax.broadcasted_iota(I32, (nk, t), 0).astype(F32)
    tops = []
    for c in range(2):
        s = lax.dot_general(keys_ref[c], q[:, c * nk:(c + 1) * nk], (((1,), (1,)), ((), ())),
                            preferred_element_type=F32)
        tops.append(_extract_top(s, row_key, PEER_TOPK))
    (v1, i1), (v2, i2) = tops
    rows16 = lax.broadcasted_iota(I32, (PEER_TOPK, t), 0)
    v2a = jnp.zeros((PEER_TOPK, t), F32)
    i2a = jnp.zeros((PEER_TOPK, t), F32)
    for j in range(PEER_TOPK):
        v2a = jnp.where(rows16 == j, v2[j], v2a)
        i2a = jnp.where(rows16 == j, i2[j], i2a)
    n_exp = float(nk * nk)
    half = PEER_TOPK // 2
    rows8 = lax.broadcasted_iota(I32, (half, t), 0)
    pos8 = rows8.astype(F32)
    v2h, i2h = v2a[:half], i2a[:half]
    cand = [v1[0] + v2a]
    ckey = [rows16.astype(F32) * n_exp + (i1[0] * float(nk) + i2a)]
    for i in range(1, half):
        cand.append(jnp.where(rows8 < PEER_TOPK // (i + 1), v1[i] + v2h, -jnp.inf))
        ckey.append((pos8 + float(i * PEER_TOPK)) * n_exp + (i1[i] * float(nk) + i2h))
    v1t = jnp.zeros((half, t), F32)
    i1t = jnp.zeros((half, t), F32)
    for r in range(half):
        v1t = jnp.where(rows8 == r, v1[half + r], v1t)
        i1t = jnp.where(rows8 == r, i1[half + r], i1t)
    cand.append(v1t + v2[0])
    ckey.append((pos8 + float(half)) * (PEER_TOPK * n_exp) + (i1t * float(nk) + i2[0]))
    tv, tk_ = _extract_top(jnp.concatenate(cand, axis=0), jnp.concatenate(ckey, axis=0), PEER_TOPK)
    denom = jnp.zeros((1, t), F32)
    es = []
    for k in range(PEER_TOPK):
        e = jnp.exp(tv[k] - tv[0])
        es.append(e)
        denom = denom + e
    eid = jnp.zeros((PEER_TOPK, t), F32)
    gate = jnp.zeros((PEER_TOPK, t), F32)
    for k in range(PEER_TOPK):
        pos = jnp.floor(tk_[k] * (1.0 / n_exp))
        eid = jnp.where(rows16 == k, tk_[k] - pos * n_exp, eid)
        gate = jnp.where(rows16 == k, es[k] / denom, gate)
    eid_ref[...] = eid.astype(I32)
    gate_ref[...] = gate


def _peer_topk(q, keys, tm=256):
    t = q.shape[0]
    tm = min(tm, t)
    out_spec = pl.BlockSpec((PEER_TOPK, tm), lambda i, h: (h, i))
    return pl.pallas_call(
        _peer_topk_kernel,
        grid=(t // tm, PEER_HEADS),
        in_specs=[pl.BlockSpec((tm, 2 * PEER_NKEYS), lambda i, h: (i, h)),
                  pl.BlockSpec((None, 2, PEER_NKEYS, PEER_NKEYS), lambda i, h: (h, 0, 0, 0))],
        out_specs=[out_spec, out_spec],
        out_shape=[jax.ShapeDtypeStruct((PEER_HEADS * PEER_TOPK, t), I32),
                   jax.ShapeDtypeStruct((PEER_HEADS * PEER_TOPK, t), F32)],
        compiler_params=_cparams(("parallel", "parallel")),
        name="peer_topk",
    )(q, keys)


def _pack_table(tab):
    e, d = tab.shape
    assert d == 2 * SUBLANES * LANES
    bits = lax.bitcast_convert_type(tab.astype(BF16), jnp.uint16).astype(U32)
    packed = bits[:, :d // 2] | (bits[:, d // 2:] << 16)
    return packed.reshape(e * SUBLANES, LANES)


def _unpack(w):
    lo = lax.bitcast_convert_type(w << 16, F32)
    hi = lax.bitcast_convert_type(w & jnp.uint32(0xFFFF0000), F32)
    return lo, hi


_BITREV8 = (0, 4, 2, 6, 1, 5, 3, 7)


def _sublane_fold8(parts):
    sub = lax.broadcasted_iota(I32, (2 * SUBLANES, LANES), 0) // 2

    def rolled(a, shift):
        return pltpu.bitcast(pltpu.roll(pltpu.bitcast(a, U32), shift, axis=0), BF16)

    lvl = [parts[_BITREV8[r]] for r in range(8)]
    for shift, mask in ((4, sub < 4), (2, (sub % 4) < 2), (1, (sub % 2) < 1)):
        nxt = []
        for a, b in zip(lvl[0::2], lvl[1::2]):
            nxt.append(jnp.where(mask, a + rolled(a, SUBLANES - shift), b + rolled(b, shift)))
        lvl = nxt
    return lvl[0]


def _pair_ranges(half):
    if half == 0:
        return (0, PEER_WINDOW), (PEER_WINDOW, PEER_PAIRS)
    return (PEER_PAIRS - PEER_WINDOW, PEER_PAIRS), (0, PEER_PAIRS - PEER_WINDOW)


def _overflow(n_low, half):
    return n_low > PEER_WINDOW if half == 0 else n_low < PEER_PAIRS - PEER_WINDOW


def _table_rows(tab_ref, off):
    return _unpack(tab_ref[pl.ds(pl.multiple_of(off, SUBLANES), SUBLANES), :])


def _peer_u_kernel(off_ref, nlow_ref, tab_ref, x_ref, o_ref, extra_ref, *, tm, half):
    lane = lax.broadcasted_iota(I32, (SUBLANES, LANES), 1)
    sub = lax.broadcasted_iota(I32, (SUBLANES, LANES), 0)
    lane_grp = lax.shift_right_logical(lane, 3)
    diag = sub == (lane & (SUBLANES - 1))
    main, rest = _pair_ranges(half)

    def folded(t, p0, p1):
        xb = pltpu.bitcast(x_ref[t], BF16)
        out = []
        for g in range(p0 // SUBLANES, p1 // SUBLANES):
            parts = []
            for r in range(SUBLANES):
                off = pl.multiple_of(off_ref[t, g * SUBLANES + r], SUBLANES)
                parts.append(pltpu.bitcast(tab_ref[pl.ds(off, SUBLANES), :], BF16) * xb)
            out.append(pltpu.bitcast(_sublane_fold8(parts), U32))
        return out

    def lane_sums(folds, p0):
        mat = jnp.zeros((SUBLANES, LANES), F32)
        for i, f in enumerate(folds):
            lo, hi = _unpack(f)
            mat = jnp.where(lane_grp == p0 // SUBLANES + i, jnp.sum(lo + hi, axis=-1, keepdims=True), mat)
        return jnp.sum(jnp.where(diag, mat, 0.0), axis=0, keepdims=True)

    def finish(t, folds):
        o_ref[pl.ds(t, 1), :] = lane_sums(folds, main[0]) + extra_ref[pl.ds(t, 1), :]

    def token(t, prev):
        cur = folded(t, *main)
        finish(jnp.maximum(t - 1, 0), prev)
        extra_ref[pl.ds(t, 1), :] = jnp.zeros((1, LANES), F32)

        @pl.when(_overflow(nlow_ref[0, t], half))
        def _():
            extra_ref[pl.ds(t, 1), :] = lane_sums(folded(t, *rest), rest[0])

        return tuple(cur)

    extra_ref[pl.ds(0, 1), :] = jnp.zeros((1, LANES), F32)
    zeros = tuple(jnp.zeros((SUBLANES, LANES), U32) for _ in range((main[1] - main[0]) // SUBLANES))
    last = lax.fori_loop(0, tm, token, zeros)
    finish(tm - 1, last)


def _peer_u(off, nlow3, tab, x4, half, tm=128):
    t = off.shape[0]
    rows = PEER_HALF * SUBLANES
    return pl.pallas_call(
        functools.partial(_peer_u_kernel, tm=tm, half=half),
        grid=(t // tm,),
        in_specs=[pl.BlockSpec((tm, LANES), lambda i: (i, 0), memory_space=pltpu.SMEM),
                  pl.BlockSpec((None, 1, tm), lambda i: (i, 0, 0), memory_space=pltpu.SMEM),
                  pl.BlockSpec((rows, LANES), lambda i: (half, 0), pipeline_mode=pl.Buffered(1)),
                  pl.BlockSpec((tm, SUBLANES, LANES), lambda i: (i, 0, 0))],
        out_specs=pl.BlockSpec((tm, LANES), lambda i: (i, 0)),
        out_shape=jax.ShapeDtypeStruct((t, LANES), F32),
        scratch_shapes=[pltpu.VMEM((tm, LANES), F32)],
        compiler_params=_cparams(("arbitrary",)),
        name="peer_expert_scores",
    )(off, nlow3, tab, x4)


def _peer_coef_kernel(a0_ref, a1_ref, eid_ref, gate_ref, c0_ref, c1_ref):
    low = eid_ref[...] < PEER_HALF
    a = jnp.where(low, a0_ref[...], a1_ref[...])
    coef = gate_ref[...] * (0.5 * a * (1.0 + lax.erf(a * (1.0 / math.sqrt(2.0)))))
    c0_ref[...] = jnp.where(low, coef, 0.0)
    c1_ref[...] = jnp.where(low, 0.0, coef)


def _peer_coef(a0, a1, eid, gate, tm=1024):
    t = eid.shape[0]
    tm = min(tm, t)
    spec = pl.BlockSpec((tm, LANES), lambda i: (i, 0))
    return pl.pallas_call(
        _peer_coef_kernel,
        grid=(t // tm,),
        in_specs=[spec, spec, spec, spec],
        out_specs=[spec, spec],
        out_shape=[jax.ShapeDtypeStruct((t, LANES), F32)] * 2,
        compiler_params=_cparams(("parallel",)),
        name="peer_coef",
    )(a0, a1, eid, gate)


def _peer_v_kernel(off_ref, nlow_ref, coef_ref, tab_ref, base_ref, o_ref, *, tm, half):
    n_acc = 4
    main, rest = _pair_ranges(half)

    def store_row(t, lo, hi):
        o_ref[t, 0] = base_ref[t, 0] + lo
        o_ref[t, 1] = base_ref[t, 1] + hi

    def weighted(t, p0, p1):
        acc_lo = [jnp.zeros((SUBLANES, LANES), F32) for _ in range(n_acc)]
        acc_hi = [jnp.zeros((SUBLANES, LANES), F32) for _ in range(n_acc)]
        for p in range(p0, p1):
            c = coef_ref[t, p]
            lo, hi = _table_rows(tab_ref, off_ref[t, p])
            acc_lo[p % n_acc] = acc_lo[p % n_acc] + c * lo
            acc_hi[p % n_acc] = acc_hi[p % n_acc] + c * hi
        return ((acc_lo[0] + acc_lo[1]) + (acc_lo[2] + acc_lo[3]),
                (acc_hi[0] + acc_hi[1]) + (acc_hi[2] + acc_hi[3]))

    def token(t, carry):
        lo, hi = weighted(t, *main)
        store_row(t, lo, hi)

        @pl.when(_overflow(nlow_ref[0, t], half))
        def _():
            lo2, hi2 = weighted(t, *rest)
            store_row(t, lo + lo2, hi + hi2)

        return carry

    lax.fori_loop(0, tm, token, 0)


def _peer_v(off, nlow3, coef, tab, base, half, tm=128):
    t = base.shape[0]
    rows = PEER_HALF * SUBLANES
    smem = pl.BlockSpec((tm, LANES), lambda i: (i, 0), memory_space=pltpu.SMEM)
    tile = pl.BlockSpec((tm, 2, SUBLANES, LANES), lambda i: (i, 0, 0, 0))
    return pl.pallas_call(
        functools.partial(_peer_v_kernel, tm=tm, half=half),
        grid=(t // tm,),
        in_specs=[smem,
                  pl.BlockSpec((None, 1, tm), lambda i: (i, 0, 0), memory_space=pltpu.SMEM),
                  smem,
                  pl.BlockSpec((rows, LANES), lambda i: (half, 0), pipeline_mode=pl.Buffered(1)),
                  tile],
        out_specs=tile,
        out_shape=jax.ShapeDtypeStruct((t, 2, SUBLANES, LANES), F32),
        compiler_params=_cparams(("arbitrary",)),
        name="peer_expert_sum",
    )(off, nlow3, coef, tab, base)


def _peer(x1, norm2_g, wq_bf, keys_bf, u_packed, v_packed, tm=128):
    t, d = x1.shape
    tm = min(tm, t)
    (xn_bf,) = _rmsnorm(x1, norm2_g, (BF16,))
    q = _matmul(xn_bf, wq_bf, wq_bf.shape[1], 0, BF16)
    eid_t, gate_t = _peer_topk(q, keys_bf)
    eid, gate = eid_t.T, gate_t.T
    high = eid >= PEER_HALF
    order = jnp.argsort(high, axis=-1, stable=True)
    eid = jnp.take_along_axis(eid, order, axis=-1)
    gate = jnp.take_along_axis(gate, order, axis=-1)
    nlow3 = (PEER_PAIRS - jnp.sum(high, axis=-1, dtype=I32)).reshape(t // tm, 1, tm)
    off = (eid & (PEER_HALF - 1)) * SUBLANES
    xw = _pack_table(xn_bf).reshape(t, SUBLANES, LANES)
    a0 = _peer_u(off, nlow3, u_packed, xw, 0, tm)
    a1 = _peer_u(off, nlow3, u_packed, xw, 1, tm)
    c0, c1 = _peer_coef(a0, a1, eid, gate)
    y = _peer_v(off, nlow3, c0, v_packed, x1.reshape(t, 2, SUBLANES, LANES), 0, tm)
    return _peer_v(off, nlow3, c1, v_packed, y, 1, tm).reshape(t, d)


def kernel(x_prompt, x_sample, norm1_g, w_in, q_norm_g, k_norm_g, lambda_q1, lambda_k1, lambda_q2, lambda_k2, attn_sub_g, conv_w, conv_b, filt_w1, filt_b1, filt_w2, filt_b2, filt_w3, filt_b3, filt_w4, filt_freq, fft_bias, hyena_out_g, w_out, norm2_g, peer_wq, peer_keys, peer_u, peer_v):
    depth = w_in.shape[0]
    d_model = x_prompt.shape[-1]
    att_w = ATT_HEADS * ATT_VDIM
    shapes = [x_prompt.shape[:2], x_sample.shape[:2]]
    x = jnp.concatenate([x_prompt.reshape(-1, d_model), x_sample.reshape(-1, d_model)], axis=0)
    slopes = 2.0 ** (-8.0 * jnp.arange(1, ATT_HEADS + 1, dtype=F32) / ATT_HEADS)

    for l in range(depth):
        lambda_init = 0.8 - 0.6 * math.exp(-0.3 * l)
        lam = (jnp.exp(jnp.sum(lambda_q1[l].astype(F32) * lambda_k1[l].astype(F32)))
               - jnp.exp(jnp.sum(lambda_q2[l].astype(F32) * lambda_k2[l].astype(F32)))
               + lambda_init).reshape(1)
        w_in_bf = w_in[l].astype(BF16)
        q_gain = jnp.tile(q_norm_g[l].astype(F32), 2 * ATT_HEADS) * (ATT_QKDIM ** -0.5 * LOG2E)
        k_gain = jnp.tile(k_norm_g[l].astype(F32), 2 * ATT_HEADS)
        qk_gain = jnp.concatenate([q_gain, k_gain]).reshape(1, 2 * att_w)

        (h_bf,) = _rmsnorm(x, norm1_g[l], (BF16,))
        qk = _matmul(h_bf, w_in_bf, 2 * att_w, 0, BF16, mode="qknorm", extra=qk_gain)
        vt = _matmul_nt(w_in_bf[:, 2 * att_w:3 * att_w].T, h_bf, BF16)
        zh = _matmul(h_bf, w_in_bf, w_in.shape[2] - 3 * att_w, 3 * att_w, F32)

        filt = (filt_w1[l], filt_b1[l], filt_w2[l], filt_b2[l], filt_w3[l], filt_b3[l], filt_w4[l], filt_freq[l])
        mixes, row = [], 0
        for (b, s) in shapes:
            n = b * s
            att = _attention(qk, vt, row, slopes, lam, attn_sub_g[l], b, s, 1.0 - lambda_init)
            hy = _hyena(zh, row, b, s, conv_w[l], conv_b[l], filt, fft_bias[l], hyena_out_g[l])
            mixes.append(jnp.concatenate([att, hy.astype(BF16)], axis=-1))
            row += n
        mix = jnp.concatenate(mixes, axis=0)
        x = _matmul(mix, w_out[l].astype(BF16), d_model, 0, F32, mode="residual", extra=x)

        x = _peer(x, norm2_g[l], peer_wq[l].astype(BF16), peer_keys[l].astype(BF16),
                  _pack_table(peer_u[l]), _pack_table(peer_v[l]))

    n0 = shapes[0][0] * shapes[0][1]
    return (x[:n0].reshape(x_prompt.shape), x[n0:].reshape(x_sample.shape))
```

```python
import functools
import math

import jax
import jax.numpy as jnp
from jax import lax
from jax.experimental import pallas as pl
from jax.experimental.pallas import tpu as pltpu

F32 = jnp.float32
BF16 = jnp.bfloat16
I32 = jnp.int32
U32 = jnp.uint32

RMS_EPS = 1e-6
LOG2E = 1.4426950408889634
LANES = 128
SUBLANES = 8
VMEM_LIMIT_BYTES = 56 * 1024 * 1024

ATT_HEADS = 8
ATT_VDIM = 128
ATT_QKDIM = 64
HYENA_ORDER = 2
FILTER_BANDS = 16
DECAY_FAST = 0.3
DECAY_SLOW = 1.5
DECAY_TARGET = 1e-2
PEER_HEADS = 8
PEER_NKEYS = 128
PEER_TOPK = 16
PEER_HALF = PEER_NKEYS * PEER_NKEYS // 2
PEER_PAIRS = PEER_HEADS * PEER_TOPK
PEER_WINDOW = 80


def _cparams(sem, vmem=VMEM_LIMIT_BYTES):
    return pltpu.CompilerParams(dimension_semantics=sem, vmem_limit_bytes=vmem)


def _segment_spec(block, start, count, minor=0):
    return pl.BlockSpec(block, lambda i: (jnp.clip(i - start, 0, count - 1), minor))


def _rmsnorm_kernel(x_ref, g_ref, *o_refs):
    x = x_ref[...]
    ms = jnp.mean(x * x, axis=-1, keepdims=True)
    y = x * lax.rsqrt(ms + RMS_EPS) * g_ref[...]
    for o_ref in o_refs:
        o_ref[...] = y.astype(o_ref.dtype)


def _rmsnorm2_kernel(xa_ref, xb_ref, g_ref, o_ref, *, na):
    first = pl.program_id(0) < na
    x = jnp.where(first, xa_ref[...], xb_ref[...])
    ms = jnp.mean(x * x, axis=-1, keepdims=True)
    o_ref[...] = (x * lax.rsqrt(ms + RMS_EPS) * g_ref[...]).astype(o_ref.dtype)


def _rmsnorm_stacked(xa, xb, g, out_dtype, tm=512):
    d = xa.shape[1]
    na, nb = xa.shape[0] // tm, xb.shape[0] // tm
    assert xa.shape[0] % tm == 0 and xb.shape[0] % tm == 0
    return pl.pallas_call(
        functools.partial(_rmsnorm2_kernel, na=na),
        grid=(na + nb,),
        in_specs=[_segment_spec((tm, d), 0, na), _segment_spec((tm, d), na, nb),
                  pl.BlockSpec((1, d), lambda i: (0, 0))],
        out_specs=pl.BlockSpec((tm, d), lambda i: (i, 0)),
        out_shape=jax.ShapeDtypeStruct((xa.shape[0] + xb.shape[0], d), out_dtype),
        compiler_params=_cparams(("parallel",)),
        name="rmsnorm_stacked",
    )(xa, xb, g.reshape(1, d).astype(F32))


def _rmsnorm(x, g, out_dtypes, tm=512):
    t, d = x.shape
    tm = min(tm, t)
    spec = pl.BlockSpec((tm, d), lambda i: (i, 0))
    return pl.pallas_call(
        _rmsnorm_kernel,
        grid=(t // tm,),
        in_specs=[spec, pl.BlockSpec((1, d), lambda i: (0, 0))],
        out_specs=[spec for _ in out_dtypes],
        out_shape=[jax.ShapeDtypeStruct((t, d), dt) for dt in out_dtypes],
        compiler_params=_cparams(("parallel",)),
        name="rmsnorm",
    )(x, g.reshape(1, d).astype(F32))


def _group_rms_scale(x, gain):
    lane = lax.broadcasted_iota(I32, x.shape, 1)
    lo = lane < ATT_QKDIM
    x2 = x * x
    s_lo = jnp.sum(jnp.where(lo, x2, 0.0), axis=-1, keepdims=True)
    s_hi = jnp.sum(jnp.where(lo, 0.0, x2), axis=-1, keepdims=True)
    ms = jnp.where(lo, s_lo, s_hi) * (1.0 / ATT_QKDIM)
    return x * lax.rsqrt(ms + RMS_EPS) * gain


def _mm_kernel(a_ref, b_ref, *rest, mode):
    acc = jnp.dot(a_ref[...], b_ref[...], preferred_element_type=F32)
    if mode == "plain":
        (o_ref,) = rest
        o_ref[...] = acc.astype(o_ref.dtype)
    elif mode == "residual":
        r_ref, o_ref = rest
        o_ref[...] = (acc + r_ref[...]).astype(o_ref.dtype)
    elif mode == "qknorm":
        g_ref, o_ref = rest
        for c in range(acc.shape[1] // LANES):
            sl = slice(c * LANES, (c + 1) * LANES)
            o_ref[:, sl] = _group_rms_scale(acc[:, sl], g_ref[:, sl]).astype(o_ref.dtype)
    else:
        raise ValueError(mode)


def _matmul(a, b, n_cols, col_off, out_dtype, mode="plain", extra=None, tm=1024, tn=1024):
    m, k = a.shape
    tm = min(tm, m)
    tn = min(tn, n_cols)
    assert col_off % tn == 0 and n_cols % tn == 0 and m % tm == 0
    off = col_off // tn
    in_specs = [pl.BlockSpec((tm, k), lambda i, j: (i, 0)),
                pl.BlockSpec((k, tn), lambda i, j: (0, j + off))]
    args = [a, b]
    if mode == "residual":
        in_specs.append(pl.BlockSpec((tm, tn), lambda i, j: (i, j)))
        args.append(extra)
    elif mode == "qknorm":
        in_specs.append(pl.BlockSpec((1, tn), lambda i, j: (0, j)))
        args.append(extra)
    return pl.pallas_call(
        functools.partial(_mm_kernel, mode=mode),
        grid=(m // tm, n_cols // tn),
        in_specs=in_specs,
        out_specs=pl.BlockSpec((tm, tn), lambda i, j: (i, j)),
        out_shape=jax.ShapeDtypeStruct((m, n_cols), out_dtype),
        compiler_params=_cparams(("parallel", "arbitrary")),
        name="matmul_" + mode,
    )(*args)


def _outproj_kernel(att_a, hy_a, x_a, att_b, hy_b, x_b, w_ref, o_ref, *, na):
    kw = att_a.shape[1]

    def run(att_ref, hy_ref, x_ref):
        acc = jnp.dot(att_ref[...], w_ref[:kw, :], preferred_element_type=F32)
        acc = acc + jnp.dot(hy_ref[...].astype(BF16), w_ref[kw:, :], preferred_element_type=F32)
        o_ref[...] = acc + x_ref[...]

    first = pl.program_id(0) < na
    pl.when(first)(lambda: run(att_a, hy_a, x_a))
    pl.when(jnp.logical_not(first))(lambda: run(att_b, hy_b, x_b))


def _outproj_stacked(seg_a, seg_b, w, tm=512, tn=1024):
    n_out = w.shape[1]
    na, nb = seg_a[0].shape[0] // tm, seg_b[0].shape[0] // tm
    assert seg_a[0].shape[0] % tm == 0 and seg_b[0].shape[0] % tm == 0 and n_out % tn == 0

    def specs(seg, start, count):
        att, hy, x = seg
        row = lambda i, j: (jnp.clip(i - start, 0, count - 1), 0)
        return [pl.BlockSpec((tm, att.shape[1]), row), pl.BlockSpec((tm, hy.shape[1]), row),
                pl.BlockSpec((tm, tn), lambda i, j: (jnp.clip(i - start, 0, count - 1), j))]

    return pl.pallas_call(
        functools.partial(_outproj_kernel, na=na),
        grid=(na + nb, n_out // tn),
        in_specs=specs(seg_a, 0, na) + specs(seg_b, na, nb) + [pl.BlockSpec((w.shape[0], tn), lambda i, j: (0, j))],
        out_specs=pl.BlockSpec((tm, tn), lambda i, j: (i, j)),
        out_shape=jax.ShapeDtypeStruct(((na + nb) * tm, n_out), F32),
        compiler_params=_cparams(("parallel", "arbitrary")),
        name="out_projection",
    )(*seg_a, *seg_b, w)


def _mm_nt_kernel(w_ref, a_ref, o_ref):
    o_ref[...] = lax.dot_general(w_ref[...], a_ref[...], (((1,), (1,)), ((), ())),
                                 preferred_element_type=F32).astype(o_ref.dtype)


def _matmul_nt(wt, a, out_dtype, tm=1024):
    n, k = wt.shape
    m = a.shape[0]
    tm = min(tm, m)
    return pl.pallas_call(
        _mm_nt_kernel,
        grid=(m // tm,),
        in_specs=[pl.BlockSpec((n, k), lambda i: (0, 0)), pl.BlockSpec((tm, k), lambda i: (i, 0))],
        out_specs=pl.BlockSpec((n, tm), lambda i: (0, i)),
        out_shape=jax.ShapeDtypeStruct((n, m), out_dtype),
        compiler_params=_cparams(("parallel",)),
        name="matmul_nt",
    )(wt, a)


def _alibi_columns(slopes, tq, tk):
    return [_alibi_side(slopes, tq, True), _alibi_side(slopes, tk, False)]


def _alibi_side(slopes, n, query_side):
    pos = jnp.arange(n, dtype=F32)
    val = (slopes.astype(F32) * LOG2E)[:, None] * pos[None, :]

    def pieces(x):
        p1 = x.astype(BF16)
        r1 = x - p1.astype(F32)
        p2 = r1.astype(BF16)
        p3 = (r1 - p2.astype(F32)).astype(BF16)
        return [p1, p2, p3]

    ones = [jnp.ones_like(val, BF16)] * 3
    six = jnp.stack(pieces(-val) + ones if query_side else ones + pieces(val), axis=-1)
    pad = jnp.zeros(val.shape + (ATT_QKDIM - 6,), BF16)
    return jnp.concatenate([six, pad, six, pad], axis=-1)


def _attn_kernel(slope_ref, lam_ref, q_ref, k_ref, vt_ref, aq_ref, ak_ref, g_ref, o_ref,
                 m_ref, l_ref, acc_ref, *, tq, tk, nk, hb, out_scale):
    hg = pl.program_id(1)
    i = pl.program_id(2)
    j = pl.program_id(3)

    @pl.when(j == 0)
    def _():
        m_ref[...] = jnp.full(m_ref.shape, -jnp.inf, F32)
        l_ref[...] = jnp.zeros(l_ref.shape, F32)
        acc_ref[...] = jnp.zeros(acc_ref.shape, F32)

    q_first = lax.broadcasted_iota(I32, (tq, LANES), 1) < ATT_QKDIM
    k_first = lax.broadcasted_iota(I32, (tk, LANES), 1) < ATT_QKDIM
    nt = (((1,), (1,)), ((), ()))

    def update(hh, scores, shift):
        vt = vt_ref[hh * LANES:(hh + 1) * LANES, :]
        for c, s in enumerate(scores):
            m_old = m_ref[hh, c]
            m_new = jnp.maximum(m_old, jnp.max(s, axis=0, keepdims=True) + shift)
            alpha = jnp.exp2(m_old - m_new)
            p = jnp.exp2(s - (m_new - shift))
            l_ref[hh, c] = alpha * l_ref[hh, c] + jnp.sum(p, axis=0, keepdims=True)
            acc_ref[hh, c] = alpha * acc_ref[hh, c] + jnp.dot(vt, p.astype(BF16),
                                                              preferred_element_type=F32)
            m_ref[hh, c] = m_new

    keys_before = i * tq >= (j + 1) * tk
    keys_after = (i + 1) * tq <= j * tk
    off_diagonal = jnp.logical_or(keys_before, keys_after)

    @pl.when(off_diagonal)
    def _():
        sign = jnp.where(keys_after, -1.0, 1.0).astype(BF16)
        gap = jnp.abs(i * tq - j * tk).astype(F32)
        for hh in range(hb):
            sl = slice(hh * LANES, (hh + 1) * LANES)
            q = q_ref[:, sl]
            k = k_ref[:, sl]
            aq = aq_ref[hh]
            ak = ak_ref[hh] * sign
            s0 = lax.dot_general(jnp.where(k_first, k, ak), jnp.where(q_first, q, aq), nt,
                                 preferred_element_type=F32)
            s1 = lax.dot_general(jnp.where(k_first, ak, k), jnp.where(q_first, aq, q), nt,
                                 preferred_element_type=F32)
            update(hh, (s0, s1), -(slope_ref[hg * hb + hh] * LOG2E) * gap)

    @pl.when(jnp.logical_not(off_diagonal))
    def _():
        kpos = lax.broadcasted_iota(I32, (tk, tq), 0) + j * tk
        qpos = lax.broadcasted_iota(I32, (tk, tq), 1) + i * tq
        dist = jnp.abs(kpos - qpos).astype(F32)
        for hh in range(hb):
            sl = slice(hh * LANES, (hh + 1) * LANES)
            q = q_ref[:, sl]
            k = k_ref[:, sl]
            zero = jnp.zeros_like(q)
            bias = dist * (-(slope_ref[hg * hb + hh] * LOG2E))
            s0 = lax.dot_general(k, jnp.where(q_first, q, zero), nt, preferred_element_type=F32) + bias
            s1 = lax.dot_general(k, jnp.where(q_first, zero, q), nt, preferred_element_type=F32) + bias
            update(hh, (s0, s1), 0.0)

    @pl.when(j == nk - 1)
    def _():
        for hh in range(hb):
            o = acc_ref[hh, 0] / l_ref[hh, 0] - lam_ref[0] * (acc_ref[hh, 1] / l_ref[hh, 1])
            ms = jnp.mean(o * o, axis=0, keepdims=True)
            y = o * lax.rsqrt(ms + RMS_EPS) * (g_ref[...] * out_scale)
            o_ref[:, hh * LANES:(hh + 1) * LANES] = y.T.astype(o_ref.dtype)


def _attention(qk, vt, row_off, slopes, lam, sub_g, batch, seq, out_scale, tq=512, tk=1024, hb=2):
    tq = min(tq, seq)
    tk = min(tk, seq // 4)
    nq, nk = seq // tq, seq // tk
    assert row_off % tq == 0 and row_off % tk == 0 and ATT_HEADS % hb == 0
    oq, ok = row_off // tq, row_off // tk
    ng = ATT_HEADS // hb
    aq, ak = _alibi_columns(slopes, tq, tk)
    kern = functools.partial(_attn_kernel, tq=tq, tk=tk, nk=nk, hb=hb, out_scale=out_scale)
    smem = pl.BlockSpec(memory_space=pltpu.SMEM)
    return pl.pallas_call(
        kern,
        grid=(batch, ng, nq, nk),
        in_specs=[smem, smem,
                  pl.BlockSpec((tq, hb * LANES), lambda b, h, i, j: (oq + b * nq + i, h)),
                  pl.BlockSpec((tk, hb * LANES), lambda b, h, i, j: (ok + b * nk + j, ng + h)),
                  pl.BlockSpec((hb * LANES, tk), lambda b, h, i, j: (h, ok + b * nk + j)),
                  pl.BlockSpec((hb, tq, LANES), lambda b, h, i, j: (h, 0, 0)),
                  pl.BlockSpec((hb, tk, LANES), lambda b, h, i, j: (h, 0, 0)),
                  pl.BlockSpec((LANES, 1), lambda b, h, i, j: (0, 0))],
        out_specs=pl.BlockSpec((tq, hb * LANES), lambda b, h, i, j: (b * nq + i, h)),
        out_shape=jax.ShapeDtypeStruct((batch * seq, ATT_HEADS * ATT_VDIM), BF16),
        scratch_shapes=[pltpu.VMEM((hb, 2, 1, tq), F32), pltpu.VMEM((hb, 2, 1, tq), F32),
                        pltpu.VMEM((hb, 2, LANES, tq), F32)],
        compiler_params=_cparams(("parallel", "parallel", "parallel", "arbitrary")),
        name="diff_attention",
    )(slopes, lam, qk, qk, vt, aq, ak, sub_g.reshape(LANES, 1).astype(F32))


def _shortconv_kernel(z_ref, w_ref, b_ref, o_ref):
    z = z_ref[...]
    n = z.shape[0]
    row = lax.broadcasted_iota(I32, z.shape, 0)
    prev = jnp.where(row == 0, 0.0, pltpu.roll(z, 1, axis=0))
    nxt = jnp.where(row == n - 1, 0.0, pltpu.roll(z, n - 1, axis=0))
    w = w_ref[...]
    o_ref[...] = prev * w[0:1] + z * w[1:2] + nxt * w[2:3] + b_ref[...]


def _shortconv(zh, row_off, conv_w, conv_b, batch, seq, cb=256):
    c3 = zh.shape[1]
    c = c3 // 3
    ncb = c // cb
    assert row_off % seq == 0
    ob = row_off // seq
    out = pl.pallas_call(
        _shortconv_kernel,
        grid=(batch, 3, ncb),
        in_specs=[pl.BlockSpec((seq, cb), lambda b, p, j: (ob + b, p * ncb + j)),
                  pl.BlockSpec((3, cb), lambda b, p, j: (0, p * ncb + j)),
                  pl.BlockSpec((1, cb), lambda b, p, j: (0, p * ncb + j))],
        out_specs=pl.BlockSpec((None, None, seq, cb), lambda b, p, j: (p, b, 0, j)),
        out_shape=jax.ShapeDtypeStruct((3, batch, seq, c), F32),
        compiler_params=_cparams(("parallel", "parallel", "parallel")),
        name="hyena_shortconv",
    )(zh, conv_w.astype(F32), conv_b.reshape(1, c3).astype(F32))
    return out


def _filter_kernel(z_ref, w1_ref, b1_ref, w2_ref, b2_ref, w3_ref, b3_ref, fr_ref, w4_ref,
                   t_ref, d_ref, o_ref, h_ref, *, tl):
    i = pl.program_id(0)
    g = pl.program_id(1)
    hi = lax.Precision.HIGHEST

    @pl.when(g == 0)
    def _():
        fr = fr_ref[...]
        h = jnp.sin(fr * (jnp.dot(z_ref[...], w1_ref[...], precision=hi, preferred_element_type=F32) + b1_ref[...]))
        h = jnp.sin(fr * (jnp.dot(h, w2_ref[...], precision=hi, preferred_element_type=F32) + b2_ref[...]))
        h_ref[...] = jnp.sin(fr * (jnp.dot(h, w3_ref[...], precision=hi, preferred_element_type=F32) + b3_ref[...]))

    f = jnp.dot(h_ref[...].astype(BF16), w4_ref[...], preferred_element_type=F32)
    f = f * jnp.exp(-t_ref[...] * d_ref[...])
    row = lax.broadcasted_iota(I32, f.shape, 0) + i * tl
    drop = jnp.logical_and(row == 0, g % 2 == 1)
    o_ref[...] = jnp.where(drop, 0.0, f)


def _hyena_filter_signals(seq, w1, b1, w2, b2, w3, b3, w4, freq, n_ch, tl=512):
    t = jnp.linspace(0.0, 1.0, seq, dtype=F32)[:, None]
    w = 2.0 * math.pi * jnp.arange(seq, dtype=F32)[:, None] / seq
    f = jnp.linspace(1e-4, FILTER_BANDS - 1, FILTER_BANDS, dtype=F32)[None, :]
    z = jnp.concatenate([t, jnp.cos(f * w), -jnp.sin(f * w)], axis=-1)
    deltas = jnp.abs(jnp.linspace(math.log(DECAY_FAST) / DECAY_TARGET,
                                  math.log(DECAY_SLOW) / DECAY_TARGET, n_ch, dtype=F32))[None, :]
    hid = w1.shape[1]
    emb = LANES
    z = jnp.pad(z, ((0, 0), (0, emb - z.shape[1])))
    w1 = jnp.pad(w1.astype(F32), ((0, emb - w1.shape[0]), (0, 0)))
    tl = min(tl, seq)
    full = lambda shape: pl.BlockSpec(shape, lambda i, g: tuple(0 for _ in shape))
    return pl.pallas_call(
        functools.partial(_filter_kernel, tl=tl),
        grid=(seq // tl, 2 * HYENA_ORDER),
        in_specs=[pl.BlockSpec((tl, emb), lambda i, g: (i, 0)),
                  full((emb, hid)), full((1, hid)), full((hid, hid)), full((1, hid)),
                  full((hid, hid)), full((1, hid)), full((1, hid)),
                  pl.BlockSpec((hid, n_ch), lambda i, g: (0, g)),
                  pl.BlockSpec((tl, 1), lambda i, g: (i, 0)),
                  full((1, n_ch))],
        out_specs=pl.BlockSpec((None, tl, n_ch), lambda i, g: (g, i, 0)),
        out_shape=jax.ShapeDtypeStruct((2 * HYENA_ORDER, seq, n_ch), F32),
        scratch_shapes=[pltpu.VMEM((tl, hid), F32)],
        compiler_params=_cparams(("parallel", "arbitrary")),
        name="hyena_filter_mlp",
    )(z, w1.astype(F32), b1.reshape(1, hid).astype(F32), w2.astype(F32), b2.reshape(1, hid).astype(F32),
      w3.astype(F32), b3.reshape(1, hid).astype(F32), freq.reshape(1, hid).astype(F32), w4.astype(BF16),
      t, deltas)


def _dft_tables(r):
    n = r * r
    k2 = jnp.arange(r, dtype=I32)
    n2 = jnp.arange(r // 2, dtype=I32)
    ang1 = (2.0 * math.pi / r) * ((k2[:, None] * n2[None, :]) % r).astype(F32)
    f1 = jnp.concatenate([jnp.cos(ang1), -jnp.sin(ang1)], axis=0)
    k1 = jnp.arange(r, dtype=I32)
    n1 = jnp.arange(r, dtype=I32)
    kk = r * k1[None, :, None] + k2[:, None, None]
    ang2 = (2.0 * math.pi / n) * ((kk * n1[None, None, :]) % n).astype(F32)
    mr, mi_ = jnp.cos(ang2), -jnp.sin(ang2)
    mf = jnp.concatenate([jnp.concatenate([mr, -mi_], axis=2),
                          jnp.concatenate([mi_, mr], axis=2)], axis=1)
    minv = jnp.swapaxes(mf, 1, 2)
    g3 = jnp.concatenate([jnp.cos(ang1.T), -jnp.sin(ang1.T)], axis=1) * (1.0 / n)
    eye = jnp.eye(N1_BLOCK, dtype=F32)
    f1, g3 = jnp.kron(f1, eye), jnp.kron(g3, eye)
    return f1.astype(BF16), mf.astype(BF16), minv.astype(BF16), g3.astype(BF16)


N1_BLOCK = SUBLANES


def _pack_complex(re, im):
    rb = lax.bitcast_convert_type(re.astype(BF16).astype(F32), U32)
    ib = lax.bitcast_convert_type(im.astype(BF16).astype(F32), U32)
    return (rb >> 16) | ib


def _unpack_complex_rows(w):
    re, im = _unpack(w)
    return jnp.concatenate([re, im], axis=0).astype(BF16)


def _fft1_kernel(f_ref, x_ref, o_ref, *, r):
    n_ch = x_ref.shape[-1]
    x = x_ref[...].reshape((r // 2) * N1_BLOCK, n_ch).astype(BF16)
    res = jnp.dot(f_ref[...], x, preferred_element_type=F32)
    half = r * N1_BLOCK
    o_ref[...] = _pack_complex(res[:half], res[half:]).reshape(r, N1_BLOCK, n_ch)


def _fft_stage1(x4, part, f1, r, n_ch):
    nb = x4.shape[1]
    xv = x4.reshape(x4.shape[0], nb, r // 2, r, n_ch)
    return pl.pallas_call(
        functools.partial(_fft1_kernel, r=r),
        grid=(nb, r // N1_BLOCK),
        in_specs=[pl.BlockSpec(f1.shape, lambda b, j: (0, 0)),
                  pl.BlockSpec((None, None, r // 2, N1_BLOCK, n_ch), lambda b, j: (part, b, 0, j, 0))],
        out_specs=pl.BlockSpec((None, r, N1_BLOCK, n_ch), lambda b, j: (b, 0, j, 0)),
        out_shape=jax.ShapeDtypeStruct((nb, r, r, n_ch), U32),
        compiler_params=_cparams(("parallel", "parallel")),
        name="hyena_dft_stage1",
    )(f1, xv)


def _k2_block(r):
    return max(1, (2 * LANES) // r)


def _filter_spec_kernel(mf_ref, bf_ref, bb_ref, o_ref, *, r):
    for kk in range(mf_ref.shape[0]):
        m = mf_ref[kk]
        xf = jnp.dot(m, _unpack_complex_rows(bf_ref[kk]), preferred_element_type=F32)
        xb = jnp.dot(m, _unpack_complex_rows(bb_ref[kk]), preferred_element_type=F32)
        o_ref[kk] = _pack_complex(xf[:r] + xb[:r], xf[r:] - xb[r:])


def _filter_spectrum(b1, mf, r, n_ch):
    kb = _k2_block(r)
    blk = lambda sel: pl.BlockSpec((None, kb, r, n_ch), lambda o, k: (2 * o + sel, k, 0, 0))
    return pl.pallas_call(
        functools.partial(_filter_spec_kernel, r=r),
        grid=(HYENA_ORDER, r // kb),
        in_specs=[pl.BlockSpec((kb, 2 * r, 2 * r), lambda o, k: (k, 0, 0)), blk(0), blk(1)],
        out_specs=pl.BlockSpec((None, kb, r, n_ch), lambda o, k: (o, k, 0, 0)),
        out_shape=jax.ShapeDtypeStruct((HYENA_ORDER, r, r, n_ch), U32),
        compiler_params=_cparams(("parallel", "parallel")),
        name="hyena_filter_spectrum",
    )(mf, b1, b1)


def _fft2_kernel(mf_ref, mi_ref, b_ref, h_ref, o_ref, *, r):
    for kk in range(mf_ref.shape[0]):
        x = jnp.dot(mf_ref[kk], _unpack_complex_rows(b_ref[kk]), preferred_element_type=F32)
        xr, xi = x[:r], x[r:]
        hr, hi = _unpack(h_ref[kk])
        y = jnp.concatenate([xr * hr - xi * hi, xr * hi + xi * hr], axis=0).astype(BF16)
        c = jnp.dot(mi_ref[kk], y, preferred_element_type=F32)
        o_ref[kk] = _pack_complex(c[:r], c[r:])


def _fft_stage2(b1, h, order, mf, minv, r, n_ch):
    nb = b1.shape[0]
    kb = _k2_block(r)
    blk = pl.BlockSpec((None, kb, r, n_ch), lambda b, k: (b, k, 0, 0))
    return pl.pallas_call(
        functools.partial(_fft2_kernel, r=r),
        grid=(nb, r // kb),
        in_specs=[pl.BlockSpec((kb, 2 * r, 2 * r), lambda b, k: (k, 0, 0)),
                  pl.BlockSpec((kb, 2 * r, 2 * r), lambda b, k: (k, 0, 0)),
                  blk,
                  pl.BlockSpec((None, kb, r, n_ch), lambda b, k: (order, k, 0, 0))],
        out_specs=blk,
        out_shape=jax.ShapeDtypeStruct((nb, r, r, n_ch), U32),
        compiler_params=_cparams(("parallel", "parallel")),
        name="hyena_dft_stage2",
    )(mf, minv, b1, h)


def _fft3_kernel(g_ref, c_ref, gate_ref, s_ref, bias_ref, ng_ref, o_ref, *, final):
    r, nb, n_ch = c_ref.shape
    cc = _unpack_complex_rows(c_ref[...].reshape(r * nb, n_ch))
    y = jnp.dot(g_ref[...], cc, preferred_element_type=F32)
    rows = (r // 2) * nb
    s_new = gate_ref[...].reshape(rows, n_ch) * (y + s_ref[...].reshape(rows, n_ch) * bias_ref[...])
    if final:
        ms = jnp.mean(s_new * s_new, axis=-1, keepdims=True)
        s_new = s_new * lax.rsqrt(ms + RMS_EPS) * ng_ref[...]
    o_ref[...] = s_new.reshape(r // 2, nb, n_ch)


def _fft_stage3(c2, g3, z4, gate_part, s4, s_part, bias, norm_g, final, r, n_ch):
    nb = c2.shape[0]
    zv = z4.reshape(z4.shape[0], nb, r // 2, r, n_ch)
    sv = s4.reshape(s4.shape[0], nb, r // 2, r, n_ch)
    dspec = lambda part: pl.BlockSpec((None, None, r // 2, N1_BLOCK, n_ch), lambda b, j: (part, b, 0, j, 0))
    return pl.pallas_call(
        functools.partial(_fft3_kernel, final=final),
        grid=(nb, r // N1_BLOCK),
        in_specs=[pl.BlockSpec(g3.shape, lambda b, j: (0, 0)),
                  pl.BlockSpec((None, r, N1_BLOCK, n_ch), lambda b, j: (b, 0, j, 0)),
                  dspec(gate_part), dspec(s_part),
                  pl.BlockSpec((1, n_ch), lambda b, j: (0, 0)),
                  pl.BlockSpec((1, n_ch), lambda b, j: (0, 0))],
        out_specs=pl.BlockSpec((None, r // 2, N1_BLOCK, n_ch), lambda b, j: (b, 0, j, 0)),
        out_shape=jax.ShapeDtypeStruct((nb, r // 2, r, n_ch), F32),
        compiler_params=_cparams(("parallel", "parallel")),
        name="hyena_dft_stage3",
    )(g3, c2, zv, sv, bias.reshape(1, n_ch).astype(F32), norm_g.reshape(1, n_ch).astype(F32))


def _hyena(zh, row_off, batch, seq, conv_w, conv_b, filt, fft_bias, out_g):
    n_ch = zh.shape[1] // 3
    r = int(round(math.sqrt(2 * seq)))
    assert r * r == 2 * seq and r % 16 == 0
    f1, mf, minv, g3 = _dft_tables(r)
    sig = _hyena_filter_signals(seq, *filt, n_ch=n_ch)
    hb1 = _fft_stage1(sig[None], 0, f1, r, n_ch)
    h = _filter_spectrum(hb1, mf, r, n_ch)
    z4 = _shortconv(zh, row_off, conv_w, conv_b, batch, seq)
    s4, s_part = z4, 2
    for o in range(HYENA_ORDER):
        b1 = _fft_stage1(s4, s_part, f1, r, n_ch)
        c2 = _fft_stage2(b1, h, o, mf, minv, r, n_ch)
        final = o == HYENA_ORDER - 1
        s = _fft_stage3(c2, g3, z4, o, s4, s_part, fft_bias[o], out_g, final, r, n_ch)
        s4, s_part = s.reshape(1, batch, seq, n_ch), 0
    return s4.reshape(batch * seq, n_ch)


def _extract_top(s, key, count):
    vals, keys = [], []
    for _ in range(count):
        m = jnp.max(s, axis=0, keepdims=True)
        kmin = jnp.min(jnp.where(s == m, key, jnp.inf), axis=0, keepdims=True)
        s = jnp.where(key == kmin, -jnp.inf, s)
        vals.append(m)
        keys.append(kmin)
    return vals, keys


def _peer_topk_kernel(q_ref, keys_ref, eid_ref, gate_ref):
    t = q_ref.shape[0]
    nk = PEER_NKEYS
    q = q_ref[...]
    row_key = lax.broadcasted_iota(I32, (nk, t), 0).astype(F32)
    tops = []
    for c in range(2):
        s = lax.dot_general(keys_ref[c], q[:, c * nk:(c + 1) * nk], (((1,), (1,)), ((), ())),
                            preferred_element_type=F32)
        tops.append(_extract_top(s, row_key, PEER_TOPK))
    (v1, i1), (v2, i2) = tops
    rows16 = lax.broadcasted_iota(I32, (PEER_TOPK, t), 0)
    v2a = jnp.zeros((PEER_TOPK, t), F32)
    i2a = jnp.zeros((PEER_TOPK, t), F32)
    for j in range(PEER_TOPK):
        v2a = jnp.where(rows16 == j, v2[j], v2a)
        i2a = jnp.where(rows16 == j, i2[j], i2a)
    n_exp = float(nk * nk)
    half = PEER_TOPK // 2
    rows8 = lax.broadcasted_iota(I32, (half, t), 0)
    pos8 = rows8.astype(F32)
    v2h, i2h = v2a[:half], i2a[:half]
    cand = [v1[0] + v2a]
    ckey = [rows16.astype(F32) * n_exp + (i1[0] * float(nk) + i2a)]
    for i in range(1, half):
        cand.append(jnp.where(rows8 < PEER_TOPK // (i + 1), v1[i] + v2h, -jnp.inf))
        ckey.append((pos8 + float(i * PEER_TOPK)) * n_exp + (i1[i] * float(nk) + i2h))
    v1t = jnp.zeros((half, t), F32)
    i1t = jnp.zeros((half, t), F32)
    for r in range(half):
        v1t = jnp.where(rows8 == r, v1[half + r], v1t)
        i1t = jnp.where(rows8 == r, i1[half + r], i1t)
    cand.append(v1t + v2[0])
    ckey.append((pos8 + float(half)) * (PEER_TOPK * n_exp) + (i1t * float(nk) + i2[0]))
    tv, tk_ = _extract_top(jnp.concatenate(cand, axis=0), jnp.concatenate(ckey, axis=0), PEER_TOPK)
    denom = jnp.zeros((1, t), F32)
    es = []
    for k in range(PEER_TOPK):
        e = jnp.exp(tv[k] - tv[0])
        es.append(e)
        denom = denom + e
    eid = jnp.zeros((PEER_TOPK, t), F32)
    gate = jnp.zeros((PEER_TOPK, t), F32)
    for k in range(PEER_TOPK):
        pos = jnp.floor(tk_[k] * (1.0 / n_exp))
        eid = jnp.where(rows16 == k, tk_[k] - pos * n_exp, eid)
        gate = jnp.where(rows16 == k, es[k] / denom, gate)
    eid_ref[...] = eid.astype(I32)
    gate_ref[...] = gate


def _peer_topk(q, keys, tm=256):
    t = q.shape[0]
    tm = min(tm, t)
    out_spec = pl.BlockSpec((PEER_TOPK, tm), lambda i, h: (h, i))
    return pl.pallas_call(
        _peer_topk_kernel,
        grid=(t // tm, PEER_HEADS),
        in_specs=[pl.BlockSpec((tm, 2 * PEER_NKEYS), lambda i, h: (i, h)),
                  pl.BlockSpec((None, 2, PEER_NKEYS, PEER_NKEYS), lambda i, h: (h, 0, 0, 0))],
        out_specs=[out_spec, out_spec],
        out_shape=[jax.ShapeDtypeStruct((PEER_HEADS * PEER_TOPK, t), I32),
                   jax.ShapeDtypeStruct((PEER_HEADS * PEER_TOPK, t), F32)],
        compiler_params=_cparams(("parallel", "parallel")),
        name="peer_topk",
    )(q, keys)


def _pack_table(tab):
    e, d = tab.shape
    assert d == 2 * SUBLANES * LANES
    bits = lax.bitcast_convert_type(tab.astype(BF16), jnp.uint16).astype(U32)
    packed = bits[:, :d // 2] | (bits[:, d // 2:] << 16)
    return packed.reshape(e * SUBLANES, LANES)


def _unpack(w):
    lo = lax.bitcast_convert_type(w << 16, F32)
    hi = lax.bitcast_convert_type(w & jnp.uint32(0xFFFF0000), F32)
    return lo, hi


_BITREV8 = (0, 4, 2, 6, 1, 5, 3, 7)


def _sublane_fold8(parts):
    sub = lax.broadcasted_iota(I32, (2 * SUBLANES, LANES), 0) // 2

    def rolled(a, shift):
        return pltpu.bitcast(pltpu.roll(pltpu.bitcast(a, U32), shift, axis=0), BF16)

    lvl = [parts[_BITREV8[r]] for r in range(8)]
    for shift, mask in ((4, sub < 4), (2, (sub % 4) < 2), (1, (sub % 2) < 1)):
        nxt = []
        for a, b in zip(lvl[0::2], lvl[1::2]):
            nxt.append(jnp.where(mask, a + rolled(a, SUBLANES - shift), b + rolled(b, shift)))
        lvl = nxt
    return lvl[0]


def _pair_ranges(half):
    if half == 0:
        return (0, PEER_WINDOW), (PEER_WINDOW, PEER_PAIRS)
    return (PEER_PAIRS - PEER_WINDOW, PEER_PAIRS), (0, PEER_PAIRS - PEER_WINDOW)


def _overflow(n_low, half):
    return n_low > PEER_WINDOW if half == 0 else n_low < PEER_PAIRS - PEER_WINDOW


def _table_rows(tab_ref, off):
    return _unpack(tab_ref[pl.ds(pl.multiple_of(off, SUBLANES), SUBLANES), :])


def _peer_u_kernel(off_ref, nlow_ref, tab_ref, x_ref, o_ref, extra_ref, *, tm, half):
    lane = lax.broadcasted_iota(I32, (SUBLANES, LANES), 1)
    sub = lax.broadcasted_iota(I32, (SUBLANES, LANES), 0)
    lane_grp = lax.shift_right_logical(lane, 3)
    diag = sub == (lane & (SUBLANES - 1))
    main, rest = _pair_ranges(half)

    def folded(t, p0, p1):
        xb = pltpu.bitcast(x_ref[t], BF16)
        out = []
        for g in range(p0 // SUBLANES, p1 // SUBLANES):
            parts = []
            for r in range(SUBLANES):
                off = pl.multiple_of(off_ref[t, g * SUBLANES + r], SUBLANES)
                parts.append(pltpu.bitcast(tab_ref[pl.ds(off, SUBLANES), :], BF16) * xb)
            out.append(pltpu.bitcast(_sublane_fold8(parts), U32))
        return out

    def lane_sums(folds, p0):
        mat = jnp.zeros((SUBLANES, LANES), F32)
        for i, f in enumerate(folds):
            lo, hi = _unpack(f)
            mat = jnp.where(lane_grp == p0 // SUBLANES + i, jnp.sum(lo + hi, axis=-1, keepdims=True), mat)
        return jnp.sum(jnp.where(diag, mat, 0.0), axis=0, keepdims=True)

    def finish(t, folds):
        o_ref[pl.ds(t, 1), :] = lane_sums(folds, main[0]) + extra_ref[pl.ds(t, 1), :]

    def token(t, prev):
        cur = folded(t, *main)
        finish(jnp.maximum(t - 1, 0), prev)
        extra_ref[pl.ds(t, 1), :] = jnp.zeros((1, LANES), F32)

        @pl.when(_overflow(nlow_ref[0, t], half))
        def _():
            extra_ref[pl.ds(t, 1), :] = lane_sums(folded(t, *rest), rest[0])

        return tuple(cur)

    extra_ref[pl.ds(0, 1), :] = jnp.zeros((1, LANES), F32)
    zeros = tuple(jnp.zeros((SUBLANES, LANES), U32) for _ in range((main[1] - main[0]) // SUBLANES))
    last = lax.fori_loop(0, tm, token, zeros)
    finish(tm - 1, last)


def _peer_u(off, nlow3, tab, x4, half, tm=128):
    t = off.shape[0]
    rows = PEER_HALF * SUBLANES
    return pl.pallas_call(
        functools.partial(_peer_u_kernel, tm=tm, half=half),
        grid=(t // tm,),
        in_specs=[pl.BlockSpec((tm, LANES), lambda i: (i, 0), memory_space=pltpu.SMEM),
                  pl.BlockSpec((None, 1, tm), lambda i: (i, 0, 0), memory_space=pltpu.SMEM),
                  pl.BlockSpec((rows, LANES), lambda i: (half, 0), pipeline_mode=pl.Buffered(1)),
                  pl.BlockSpec((tm, SUBLANES, LANES), lambda i: (i, 0, 0))],
        out_specs=pl.BlockSpec((tm, LANES), lambda i: (i, 0)),
        out_shape=jax.ShapeDtypeStruct((t, LANES), F32),
        scratch_shapes=[pltpu.VMEM((tm, LANES), F32)],
        compiler_params=_cparams(("arbitrary",)),
        name="peer_expert_scores",
    )(off, nlow3, tab, x4)


def _peer_coef_kernel(a0_ref, a1_ref, eid_ref, gate_ref, c0_ref, c1_ref):
    low = eid_ref[...] < PEER_HALF
    a = jnp.where(low, a0_ref[...], a1_ref[...])
    coef = gate_ref[...] * (0.5 * a * (1.0 + lax.erf(a * (1.0 / math.sqrt(2.0)))))
    c0_ref[...] = jnp.where(low, coef, 0.0)
    c1_ref[...] = jnp.where(low, 0.0, coef)


def _peer_coef(a0, a1, eid, gate, tm=1024):
    t = eid.shape[0]
    tm = min(tm, t)
    spec = pl.BlockSpec((tm, LANES), lambda i: (i, 0))
    return pl.pallas_call(
        _peer_coef_kernel,
        grid=(t // tm,),
        in_specs=[spec, spec, spec, spec],
        out_specs=[spec, spec],
        out_shape=[jax.ShapeDtypeStruct((t, LANES), F32)] * 2,
        compiler_params=_cparams(("parallel",)),
        name="peer_coef",
    )(a0, a1, eid, gate)


def _peer_v_kernel(off_ref, nlow_ref, coef_ref, tab_ref, base_ref, *o_refs, tm, half, n_first):
    n_acc = 4
    main, rest = _pair_ranges(half)

    def weighted(t, p0, p1):
        acc_lo = [jnp.zeros((SUBLANES, LANES), F32) for _ in range(n_acc)]
        acc_hi = [jnp.zeros((SUBLANES, LANES), F32) for _ in range(n_acc)]
        for p in range(p0, p1):
            c = coef_ref[t, p]
            lo, hi = _table_rows(tab_ref, off_ref[t, p])
            acc_lo[p % n_acc] = acc_lo[p % n_acc] + c * lo
            acc_hi[p % n_acc] = acc_hi[p % n_acc] + c * hi
        return ((acc_lo[0] + acc_lo[1]) + (acc_lo[2] + acc_lo[3]),
                (acc_hi[0] + acc_hi[1]) + (acc_hi[2] + acc_hi[3]))

    def run(o_ref):
        def store_row(t, lo, hi):
            o_ref[t, 0] = base_ref[t, 0] + lo
            o_ref[t, 1] = base_ref[t, 1] + hi

        def token(t, carry):
            lo, hi = weighted(t, *main)
            store_row(t, lo, hi)

            @pl.when(_overflow(nlow_ref[0, t], half))
            def _():
                lo2, hi2 = weighted(t, *rest)
                store_row(t, lo + lo2, hi + hi2)

            return carry

        lax.fori_loop(0, tm, token, 0)

    if len(o_refs) == 1:
        run(o_refs[0])
    else:
        first = pl.program_id(0) < n_first
        pl.when(first)(lambda: run(o_refs[0]))
        pl.when(jnp.logical_not(first))(lambda: run(o_refs[1]))


def _peer_v(off, nlow3, coef, tab, base, half, tm=128, split=None):
    t = base.shape[0]
    rows = PEER_HALF * SUBLANES
    smem = pl.BlockSpec((tm, LANES), lambda i: (i, 0), memory_space=pltpu.SMEM)
    tile = pl.BlockSpec((tm, 2, SUBLANES, LANES), lambda i: (i, 0, 0, 0))
    if split is None:
        n_first, out_specs = 0, tile
        out_shape = jax.ShapeDtypeStruct((t, 2, SUBLANES, LANES), F32)
    else:
        assert split % tm == 0
        n_first, n_rest = split // tm, (t - split) // tm
        blk = (tm, 2, SUBLANES, LANES)
        out_specs = [pl.BlockSpec(blk, lambda i: (jnp.clip(i, 0, n_first - 1), 0, 0, 0)),
                     pl.BlockSpec(blk, lambda i: (jnp.clip(i - n_first, 0, n_rest - 1), 0, 0, 0))]
        out_shape = [jax.ShapeDtypeStruct((split, 2, SUBLANES, LANES), F32),
                     jax.ShapeDtypeStruct((t - split, 2, SUBLANES, LANES), F32)]
    return pl.pallas_call(
        functools.partial(_peer_v_kernel, tm=tm, half=half, n_first=n_first),
        grid=(t // tm,),
        in_specs=[smem,
                  pl.BlockSpec((None, 1, tm), lambda i: (i, 0, 0), memory_space=pltpu.SMEM),
                  smem,
                  pl.BlockSpec((rows, LANES), lambda i: (half, 0), pipeline_mode=pl.Buffered(1)),
                  tile],
        out_specs=out_specs,
        out_shape=out_shape,
        compiler_params=_cparams(("arbitrary",)),
        name="peer_expert_sum",
    )(off, nlow3, coef, tab, base)


def _peer(x1, split, norm2_g, wq_bf, keys_bf, u_packed, v_packed, tm=128):
    t, d = x1.shape
    tm = min(tm, t)
    (xn_bf,) = _rmsnorm(x1, norm2_g, (BF16,))
    q = _matmul(xn_bf, wq_bf, wq_bf.shape[1], 0, BF16)
    eid_t, gate_t = _peer_topk(q, keys_bf)
    eid, gate = eid_t.T, gate_t.T
    high = (eid >= PEER_HALF).astype(I32)
    nlow3 = (PEER_PAIRS - jnp.sum(high, axis=-1, dtype=I32)).reshape(t // tm, 1, tm)
    _, eid, gate = lax.sort((high, eid, gate), dimension=1, is_stable=True, num_keys=1)
    off = (eid & (PEER_HALF - 1)) * SUBLANES
    xw = _pack_table(xn_bf).reshape(t, SUBLANES, LANES)
    a0 = _peer_u(off, nlow3, u_packed, xw, 0, tm)
    a1 = _peer_u(off, nlow3, u_packed, xw, 1, tm)
    c0, c1 = _peer_coef(a0, a1, eid, gate)
    y = _peer_v(off, nlow3, c0, v_packed, x1.reshape(t, 2, SUBLANES, LANES), 0, tm)
    ya, yb = _peer_v(off, nlow3, c1, v_packed, y, 1, tm, split=split)
    return ya.reshape(split, d), yb.reshape(t - split, d)


def kernel(x_prompt, x_sample, norm1_g, w_in, q_norm_g, k_norm_g, lambda_q1, lambda_k1, lambda_q2, lambda_k2, attn_sub_g, conv_w, conv_b, filt_w1, filt_b1, filt_w2, filt_b2, filt_w3, filt_b3, filt_w4, filt_freq, fft_bias, hyena_out_g, w_out, norm2_g, peer_wq, peer_keys, peer_u, peer_v):
    depth = w_in.shape[0]
    d_model = x_prompt.shape[-1]
    att_w = ATT_HEADS * ATT_VDIM
    shapes = [x_prompt.shape[:2], x_sample.shape[:2]]
    xs = [x_prompt.reshape(-1, d_model), x_sample.reshape(-1, d_model)]
    n0 = xs[0].shape[0]
    slopes = 2.0 ** (-8.0 * jnp.arange(1, ATT_HEADS + 1, dtype=F32) / ATT_HEADS)

    for l in range(depth):
        lambda_init = 0.8 - 0.6 * math.exp(-0.3 * l)
        lam = (jnp.exp(jnp.sum(lambda_q1[l].astype(F32) * lambda_k1[l].astype(F32)))
               - jnp.exp(jnp.sum(lambda_q2[l].astype(F32) * lambda_k2[l].astype(F32)))
               + lambda_init).reshape(1)
        w_in_bf = w_in[l].astype(BF16)
        q_gain = jnp.tile(q_norm_g[l].astype(F32), 2 * ATT_HEADS) * (ATT_QKDIM ** -0.5 * LOG2E)
        k_gain = jnp.tile(k_norm_g[l].astype(F32), 2 * ATT_HEADS)
        qk_gain = jnp.concatenate([q_gain, k_gain]).reshape(1, 2 * att_w)

        h_bf = _rmsnorm_stacked(xs[0], xs[1], norm1_g[l], BF16)
        qk = _matmul(h_bf, w_in_bf, 2 * att_w, 0, BF16, mode="qknorm", extra=qk_gain)
        vt = _matmul_nt(w_in_bf[:, 2 * att_w:3 * att_w].T, h_bf, BF16)
        zh = _matmul(h_bf, w_in_bf, w_in.shape[2] - 3 * att_w, 3 * att_w, F32)

        filt = (filt_w1[l], filt_b1[l], filt_w2[l], filt_b2[l], filt_w3[l], filt_b3[l], filt_w4[l], filt_freq[l])
        segs, row = [], 0
        for (b, s), x_seg in zip(shapes, xs):
            att = _attention(qk, vt, row, slopes, lam, attn_sub_g[l], b, s, 1.0 - lambda_init)
            hy = _hyena(zh, row, b, s, conv_w[l], conv_b[l], filt, fft_bias[l], hyena_out_g[l])
            segs.append((att, hy, x_seg))
            row += b * s
        x1 = _outproj_stacked(segs[0], segs[1], w_out[l].astype(BF16))

        xs = _peer(x1, n0, norm2_g[l], peer_wq[l].astype(BF16), peer_keys[l].astype(BF16),
                   _pack_table(peer_u[l]), _pack_table(peer_v[l]))

    return (xs[0].reshape(x_prompt.shape), xs[1].reshape(x_sample.shape))
```

```python
import functools
import math

import jax
import jax.numpy as jnp
from jax import lax
from jax.experimental import pallas as pl
from jax.experimental.pallas import tpu as pltpu

F32 = jnp.float32
BF16 = jnp.bfloat16
I32 = jnp.int32
U32 = jnp.uint32

RMS_EPS = 1e-6
LOG2E = 1.4426950408889634
LANES = 128
SUBLANES = 8
VMEM_LIMIT_BYTES = 56 * 1024 * 1024

ATT_HEADS = 8
ATT_VDIM = 128
ATT_QKDIM = 64
HYENA_ORDER = 2
FILTER_BANDS = 16
DECAY_FAST = 0.3
DECAY_SLOW = 1.5
DECAY_TARGET = 1e-2
PEER_HEADS = 8
PEER_NKEYS = 128
PEER_TOPK = 16
PEER_HALF = PEER_NKEYS * PEER_NKEYS // 2
PEER_PAIRS = PEER_HEADS * PEER_TOPK
PEER_WINDOW = 80


def _cparams(sem, vmem=VMEM_LIMIT_BYTES):
    return pltpu.CompilerParams(dimension_semantics=sem, vmem_limit_bytes=vmem)


def _segment_spec(block, start, count, minor=0):
    return pl.BlockSpec(block, lambda i: (jnp.clip(i - start, 0, count - 1), minor))


def _rmsnorm_kernel(x_ref, g_ref, *o_refs):
    x = x_ref[...]
    ms = jnp.mean(x * x, axis=-1, keepdims=True)
    y = x * lax.rsqrt(ms + RMS_EPS) * g_ref[...]
    for o_ref in o_refs:
        o_ref[...] = y.astype(o_ref.dtype)


def _rmsnorm2_kernel(xa_ref, xb_ref, g_ref, o_ref, *, na):
    first = pl.program_id(0) < na
    x = jnp.where(first, xa_ref[...], xb_ref[...])
    ms = jnp.mean(x * x, axis=-1, keepdims=True)
    o_ref[...] = (x * lax.rsqrt(ms + RMS_EPS) * g_ref[...]).astype(o_ref.dtype)


def _rmsnorm_stacked(xa, xb, g, out_dtype, tm=512):
    d = xa.shape[1]
    na, nb = xa.shape[0] // tm, xb.shape[0] // tm
    assert xa.shape[0] % tm == 0 and xb.shape[0] % tm == 0
    return pl.pallas_call(
        functools.partial(_rmsnorm2_kernel, na=na),
        grid=(na + nb,),
        in_specs=[_segment_spec((tm, d), 0, na), _segment_spec((tm, d), na, nb),
                  pl.BlockSpec((1, d), lambda i: (0, 0))],
        out_specs=pl.BlockSpec((tm, d), lambda i: (i, 0)),
        out_shape=jax.ShapeDtypeStruct((xa.shape[0] + xb.shape[0], d), out_dtype),
        compiler_params=_cparams(("parallel",)),
        name="rmsnorm_stacked",
    )(xa, xb, g.reshape(1, d).astype(F32))


def _rmsnorm(x, g, out_dtypes, tm=512):
    t, d = x.shape
    tm = min(tm, t)
    spec = pl.BlockSpec((tm, d), lambda i: (i, 0))
    return pl.pallas_call(
        _rmsnorm_kernel,
        grid=(t // tm,),
        in_specs=[spec, pl.BlockSpec((1, d), lambda i: (0, 0))],
        out_specs=[spec for _ in out_dtypes],
        out_shape=[jax.ShapeDtypeStruct((t, d), dt) for dt in out_dtypes],
        compiler_params=_cparams(("parallel",)),
        name="rmsnorm",
    )(x, g.reshape(1, d).astype(F32))


def _group_rms_scale(x, gain):
    lane = lax.broadcasted_iota(I32, x.shape, 1)
    lo = lane < ATT_QKDIM
    x2 = x * x
    s_lo = jnp.sum(jnp.where(lo, x2, 0.0), axis=-1, keepdims=True)
    s_hi = jnp.sum(jnp.where(lo, 0.0, x2), axis=-1, keepdims=True)
    ms = jnp.where(lo, s_lo, s_hi) * (1.0 / ATT_QKDIM)
    return x * lax.rsqrt(ms + RMS_EPS) * gain


def _mm_kernel(a_ref, b_ref, *rest, mode):
    acc = jnp.dot(a_ref[...], b_ref[...], preferred_element_type=F32)
    if mode == "plain":
        (o_ref,) = rest
        o_ref[...] = acc.astype(o_ref.dtype)
    elif mode == "residual":
        r_ref, o_ref = rest
        o_ref[...] = (acc + r_ref[...]).astype(o_ref.dtype)
    elif mode == "qknorm":
        g_ref, o_ref = rest
        for c in range(acc.shape[1] // LANES):
            sl = slice(c * LANES, (c + 1) * LANES)
            o_ref[:, sl] = _group_rms_scale(acc[:, sl], g_ref[:, sl]).astype(o_ref.dtype)
    else:
        raise ValueError(mode)


def _matmul(a, b, n_cols, col_off, out_dtype, mode="plain", extra=None, tm=1024, tn=1024):
    m, k = a.shape
    tm = min(tm, m)
    tn = min(tn, n_cols)
    assert col_off % tn == 0 and n_cols % tn == 0 and m % tm == 0
    off = col_off // tn
    in_specs = [pl.BlockSpec((tm, k), lambda i, j: (i, 0)),
                pl.BlockSpec((k, tn), lambda i, j: (0, j + off))]
    args = [a, b]
    if mode == "residual":
        in_specs.append(pl.BlockSpec((tm, tn), lambda i, j: (i, j)))
        args.append(extra)
    elif mode == "qknorm":
        in_specs.append(pl.BlockSpec((1, tn), lambda i, j: (0, j)))
        args.append(extra)
    return pl.pallas_call(
        functools.partial(_mm_kernel, mode=mode),
        grid=(m // tm, n_cols // tn),
        in_specs=in_specs,
        out_specs=pl.BlockSpec((tm, tn), lambda i, j: (i, j)),
        out_shape=jax.ShapeDtypeStruct((m, n_cols), out_dtype),
        compiler_params=_cparams(("parallel", "arbitrary")),
        name="matmul_" + mode,
    )(*args)


def _outproj_kernel(att_a, hy_a, x_a, att_b, hy_b, x_b, w_ref, o_ref, *, na):
    kw = att_a.shape[1]

    def run(att_ref, hy_ref, x_ref):
        acc = jnp.dot(att_ref[...], w_ref[:kw, :], preferred_element_type=F32)
        acc = acc + jnp.dot(hy_ref[...].astype(BF16), w_ref[kw:, :], preferred_element_type=F32)
        o_ref[...] = acc + x_ref[...]

    first = pl.program_id(0) < na
    pl.when(first)(lambda: run(att_a, hy_a, x_a))
    pl.when(jnp.logical_not(first))(lambda: run(att_b, hy_b, x_b))


def _outproj_stacked(seg_a, seg_b, w, tm=512, tn=1024):
    n_out = w.shape[1]
    na, nb = seg_a[0].shape[0] // tm, seg_b[0].shape[0] // tm
    assert seg_a[0].shape[0] % tm == 0 and seg_b[0].shape[0] % tm == 0 and n_out % tn == 0

    def specs(seg, start, count):
        att, hy, x = seg
        row = lambda i, j: (jnp.clip(i - start, 0, count - 1), 0)
        return [pl.BlockSpec((tm, att.shape[1]), row), pl.BlockSpec((tm, hy.shape[1]), row),
                pl.BlockSpec((tm, tn), lambda i, j: (jnp.clip(i - start, 0, count - 1), j))]

    return pl.pallas_call(
        functools.partial(_outproj_kernel, na=na),
        grid=(na + nb, n_out // tn),
        in_specs=specs(seg_a, 0, na) + specs(seg_b, na, nb) + [pl.BlockSpec((w.shape[0], tn), lambda i, j: (0, j))],
        out_specs=pl.BlockSpec((tm, tn), lambda i, j: (i, j)),
        out_shape=jax.ShapeDtypeStruct(((na + nb) * tm, n_out), F32),
        compiler_params=_cparams(("parallel", "arbitrary")),
        name="out_projection",
    )(*seg_a, *seg_b, w)


def _mm_nt_kernel(w_ref, a_ref, o_ref):
    o_ref[...] = lax.dot_general(w_ref[...], a_ref[...], (((1,), (1,)), ((), ())),
                                 preferred_element_type=F32).astype(o_ref.dtype)


def _matmul_nt(wt, a, out_dtype, tm=1024):
    n, k = wt.shape
    m = a.shape[0]
    tm = min(tm, m)
    return pl.pallas_call(
        _mm_nt_kernel,
        grid=(m // tm,),
        in_specs=[pl.BlockSpec((n, k), lambda i: (0, 0)), pl.BlockSpec((tm, k), lambda i: (i, 0))],
        out_specs=pl.BlockSpec((n, tm), lambda i: (0, i)),
        out_shape=jax.ShapeDtypeStruct((n, m), out_dtype),
        compiler_params=_cparams(("parallel",)),
        name="matmul_nt",
    )(wt, a)


def _alibi_columns(slopes, tq, tk):
    return [_alibi_side(slopes, tq, True), _alibi_side(slopes, tk, False)]


def _alibi_side(slopes, n, query_side):
    pos = jnp.arange(n, dtype=F32)
    val = (slopes.astype(F32) * LOG2E)[:, None] * pos[None, :]

    def pieces(x):
        p1 = x.astype(BF16)
        r1 = x - p1.astype(F32)
        p2 = r1.astype(BF16)
        p3 = (r1 - p2.astype(F32)).astype(BF16)
        return [p1, p2, p3]

    ones = [jnp.ones_like(val, BF16)] * 3
    six = jnp.stack(pieces(-val) + ones if query_side else ones + pieces(val), axis=-1)
    pad = jnp.zeros(val.shape + (ATT_QKDIM - 6,), BF16)
    return jnp.concatenate([six, pad, six, pad], axis=-1)


def _attn_kernel(slope_ref, lam_ref, q_ref, k_ref, vt_ref, aq_ref, ak_ref, g_ref, o_ref,
                 m_ref, l_ref, acc_ref, *, tq, tk, nk, hb, out_scale):
    hg = pl.program_id(1)
    i = pl.program_id(2)
    j = pl.program_id(3)

    @pl.when(j == 0)
    def _():
        m_ref[...] = jnp.full(m_ref.shape, -jnp.inf, F32)
        l_ref[...] = jnp.zeros(l_ref.shape, F32)
        acc_ref[...] = jnp.zeros(acc_ref.shape, F32)

    q_first = lax.broadcasted_iota(I32, (tq, LANES), 1) < ATT_QKDIM
    k_first = lax.broadcasted_iota(I32, (tk, LANES), 1) < ATT_QKDIM
    nt = (((1,), (1,)), ((), ()))

    def update(hh, scores, shift):
        vt = vt_ref[hh * LANES:(hh + 1) * LANES, :]
        for c, s in enumerate(scores):
            m_old = m_ref[hh, c]
            m_new = jnp.maximum(m_old, jnp.max(s, axis=0, keepdims=True) + shift)
            alpha = jnp.exp2(m_old - m_new)
            p = jnp.exp2(s - (m_new - shift))
            l_ref[hh, c] = alpha * l_ref[hh, c] + jnp.sum(p, axis=0, keepdims=True)
            acc_ref[hh, c] = alpha * acc_ref[hh, c] + jnp.dot(vt, p.astype(BF16),
                                                              preferred_element_type=F32)
            m_ref[hh, c] = m_new

    keys_before = i * tq >= (j + 1) * tk
    keys_after = (i + 1) * tq <= j * tk
    off_diagonal = jnp.logical_or(keys_before, keys_after)

    @pl.when(off_diagonal)
    def _():
        sign = jnp.where(keys_after, -1.0, 1.0).astype(BF16)
        gap = jnp.abs(i * tq - j * tk).astype(F32)
        for hh in range(hb):
            sl = slice(hh * LANES, (hh + 1) * LANES)
            q = q_ref[:, sl]
            k = k_ref[:, sl]
            aq = aq_ref[hh]
            ak = ak_ref[hh] * sign
            s0 = lax.dot_general(jnp.where(k_first, k, ak), jnp.where(q_first, q, aq), nt,
                                 preferred_element_type=F32)
            s1 = lax.dot_general(jnp.where(k_first, ak, k), jnp.where(q_first, aq, q), nt,
                                 preferred_element_type=F32)
            update(hh, (s0, s1), -(slope_ref[hg * hb + hh] * LOG2E) * gap)

    @pl.when(jnp.logical_not(off_diagonal))
    def _():
        kpos = lax.broadcasted_iota(I32, (tk, tq), 0) + j * tk
        qpos = lax.broadcasted_iota(I32, (tk, tq), 1) + i * tq
        dist = jnp.abs(kpos - qpos).astype(F32)
        for hh in range(hb):
            sl = slice(hh * LANES, (hh + 1) * LANES)
            q = q_ref[:, sl]
            k = k_ref[:, sl]
            zero = jnp.zeros_like(q)
            bias = dist * (-(slope_ref[hg * hb + hh] * LOG2E))
            s0 = lax.dot_general(k, jnp.where(q_first, q, zero), nt, preferred_element_type=F32) + bias
            s1 = lax.dot_general(k, jnp.where(q_first, zero, q), nt, preferred_element_type=F32) + bias
            update(hh, (s0, s1), 0.0)

    @pl.when(j == nk - 1)
    def _():
        for hh in range(hb):
            o = acc_ref[hh, 0] / l_ref[hh, 0] - lam_ref[0] * (acc_ref[hh, 1] / l_ref[hh, 1])
            ms = jnp.mean(o * o, axis=0, keepdims=True)
            y = o * lax.rsqrt(ms + RMS_EPS) * (g_ref[...] * out_scale)
            o_ref[:, hh * LANES:(hh + 1) * LANES] = y.T.astype(o_ref.dtype)


def _attention(qk, vt, row_off, slopes, lam, sub_g, batch, seq, out_scale, tq=512, tk=1024, hb=4):
    tq = min(tq, seq)
    tk = min(tk, seq // 4)
    nq, nk = seq // tq, seq // tk
    assert row_off % tq == 0 and row_off % tk == 0 and ATT_HEADS % hb == 0
    oq, ok = row_off // tq, row_off // tk
    ng = ATT_HEADS // hb
    aq, ak = _alibi_columns(slopes, tq, tk)
    kern = functools.partial(_attn_kernel, tq=tq, tk=tk, nk=nk, hb=hb, out_scale=out_scale)
    smem = pl.BlockSpec(memory_space=pltpu.SMEM)
    return pl.pallas_call(
        kern,
        grid=(batch, ng, nq, nk),
        in_specs=[smem, smem,
                  pl.BlockSpec((tq, hb * LANES), lambda b, h, i, j: (oq + b * nq + i, h)),
                  pl.BlockSpec((tk, hb * LANES), lambda b, h, i, j: (ok + b * nk + j, ng + h)),
                  pl.BlockSpec((hb * LANES, tk), lambda b, h, i, j: (h, ok + b * nk + j)),
                  pl.BlockSpec((hb, tq, LANES), lambda b, h, i, j: (h, 0, 0)),
                  pl.BlockSpec((hb, tk, LANES), lambda b, h, i, j: (h, 0, 0)),
                  pl.BlockSpec((LANES, 1), lambda b, h, i, j: (0, 0))],
        out_specs=pl.BlockSpec((tq, hb * LANES), lambda b, h, i, j: (b * nq + i, h)),
        out_shape=jax.ShapeDtypeStruct((batch * seq, ATT_HEADS * ATT_VDIM), BF16),
        scratch_shapes=[pltpu.VMEM((hb, 2, 1, tq), F32), pltpu.VMEM((hb, 2, 1, tq), F32),
                        pltpu.VMEM((hb, 2, LANES, tq), F32)],
        compiler_params=_cparams(("parallel", "parallel", "parallel", "arbitrary")),
        name="diff_attention",
    )(slopes, lam, qk, qk, vt, aq, ak, sub_g.reshape(LANES, 1).astype(F32))


def _shortconv_kernel(z_ref, w_ref, b_ref, o_ref):
    z = z_ref[...]
    n = z.shape[0]
    row = lax.broadcasted_iota(I32, z.shape, 0)
    prev = jnp.where(row == 0, 0.0, pltpu.roll(z, 1, axis=0))
    nxt = jnp.where(row == n - 1, 0.0, pltpu.roll(z, n - 1, axis=0))
    w = w_ref[...]
    o_ref[...] = prev * w[0:1] + z * w[1:2] + nxt * w[2:3] + b_ref[...]


def _shortconv(zh, row_off, conv_w, conv_b, batch, seq, cb=256):
    c3 = zh.shape[1]
    c = c3 // 3
    ncb = c // cb
    assert row_off % seq == 0
    ob = row_off // seq
    out = pl.pallas_call(
        _shortconv_kernel,
        grid=(batch, 3, ncb),
        in_specs=[pl.BlockSpec((seq, cb), lambda b, p, j: (ob + b, p * ncb + j)),
                  pl.BlockSpec((3, cb), lambda b, p, j: (0, p * ncb + j)),
                  pl.BlockSpec((1, cb), lambda b, p, j: (0, p * ncb + j))],
        out_specs=pl.BlockSpec((None, None, seq, cb), lambda b, p, j: (p, b, 0, j)),
        out_shape=jax.ShapeDtypeStruct((3, batch, seq, c), F32),
        compiler_params=_cparams(("parallel", "parallel", "parallel")),
        name="hyena_shortconv",
    )(zh, conv_w.astype(F32), conv_b.reshape(1, c3).astype(F32))
    return out


def _filter_kernel(z_ref, w1_ref, b1_ref, w2_ref, b2_ref, w3_ref, b3_ref, fr_ref, w4_ref,
                   t_ref, d_ref, o_ref, h_ref, *, tl):
    i = pl.program_id(0)
    g = pl.program_id(1)
    hi = lax.Precision.HIGHEST

    @pl.when(g == 0)
    def _():
        fr = fr_ref[...]
        h = jnp.sin(fr * (jnp.dot(z_ref[...], w1_ref[...], precision=hi, preferred_element_type=F32) + b1_ref[...]))
        h = jnp.sin(fr * (jnp.dot(h, w2_ref[...], precision=hi, preferred_element_type=F32) + b2_ref[...]))
        h_ref[...] = jnp.sin(fr * (jnp.dot(h, w3_ref[...], precision=hi, preferred_element_type=F32) + b3_ref[...]))

    f = jnp.dot(h_ref[...].astype(BF16), w4_ref[...], preferred_element_type=F32)
    f = f * jnp.exp(-t_ref[...] * d_ref[...])
    row = lax.broadcasted_iota(I32, f.shape, 0) + i * tl
    drop = jnp.logical_and(row == 0, g % 2 == 1)
    o_ref[...] = jnp.where(drop, 0.0, f)


def _hyena_filter_signals(seq, w1, b1, w2, b2, w3, b3, w4, freq, n_ch, tl=512):
    t = jnp.linspace(0.0, 1.0, seq, dtype=F32)[:, None]
    w = 2.0 * math.pi * jnp.arange(seq, dtype=F32)[:, None] / seq
    f = jnp.linspace(1e-4, FILTER_BANDS - 1, FILTER_BANDS, dtype=F32)[None, :]
    z = jnp.concatenate([t, jnp.cos(f * w), -jnp.sin(f * w)], axis=-1)
    deltas = jnp.abs(jnp.linspace(math.log(DECAY_FAST) / DECAY_TARGET,
                                  math.log(DECAY_SLOW) / DECAY_TARGET, n_ch, dtype=F32))[None, :]
    hid = w1.shape[1]
    emb = LANES
    z = jnp.pad(z, ((0, 0), (0, emb - z.shape[1])))
    w1 = jnp.pad(w1.astype(F32), ((0, emb - w1.shape[0]), (0, 0)))
    tl = min(tl, seq)
    full = lambda shape: pl.BlockSpec(shape, lambda i, g: tuple(0 for _ in shape))
    return pl.pallas_call(
        functools.partial(_filter_kernel, tl=tl),
        grid=(seq // tl, 2 * HYENA_ORDER),
        in_specs=[pl.BlockSpec((tl, emb), lambda i, g: (i, 0)),
                  full((emb, hid)), full((1, hid)), full((hid, hid)), full((1, hid)),
                  full((hid, hid)), full((1, hid)), full((1, hid)),
                  pl.BlockSpec((hid, n_ch), lambda i, g: (0, g)),
                  pl.BlockSpec((tl, 1), lambda i, g: (i, 0)),
                  full((1, n_ch))],
        out_specs=pl.BlockSpec((None, tl, n_ch), lambda i, g: (g, i, 0)),
        out_shape=jax.ShapeDtypeStruct((2 * HYENA_ORDER, seq, n_ch), F32),
        scratch_shapes=[pltpu.VMEM((tl, hid), F32)],
        compiler_params=_cparams(("parallel", "arbitrary")),
        name="hyena_filter_mlp",
    )(z, w1.astype(F32), b1.reshape(1, hid).astype(F32), w2.astype(F32), b2.reshape(1, hid).astype(F32),
      w3.astype(F32), b3.reshape(1, hid).astype(F32), freq.reshape(1, hid).astype(F32), w4.astype(BF16),
      t, deltas)


def _dft_tables(r):
    n = r * r
    k2 = jnp.arange(r, dtype=I32)
    n2 = jnp.arange(r // 2, dtype=I32)
    ang1 = (2.0 * math.pi / r) * ((k2[:, None] * n2[None, :]) % r).astype(F32)
    f1 = jnp.concatenate([jnp.cos(ang1), -jnp.sin(ang1)], axis=0)
    k1 = jnp.arange(r, dtype=I32)
    n1 = jnp.arange(r, dtype=I32)
    kk = r * k1[None, :, None] + k2[:, None, None]
    ang2 = (2.0 * math.pi / n) * ((kk * n1[None, None, :]) % n).astype(F32)
    mr, mi_ = jnp.cos(ang2), -jnp.sin(ang2)
    mf = jnp.concatenate([jnp.concatenate([mr, -mi_], axis=2),
                          jnp.concatenate([mi_, mr], axis=2)], axis=1)
    minv = jnp.swapaxes(mf, 1, 2)
    g3 = jnp.concatenate([jnp.cos(ang1.T), -jnp.sin(ang1.T)], axis=1) * (1.0 / n)
    eye = jnp.eye(N1_BLOCK, dtype=F32)
    f1, g3 = jnp.kron(f1, eye), jnp.kron(g3, eye)
    return f1.astype(BF16), mf.astype(BF16), minv.astype(BF16), g3.astype(BF16)


N1_BLOCK = SUBLANES


def _pack_complex(re, im):
    rb = lax.bitcast_convert_type(re.astype(BF16).astype(F32), U32)
    ib = lax.bitcast_convert_type(im.astype(BF16).astype(F32), U32)
    return (rb >> 16) | ib


def _unpack_complex_rows(w):
    re, im = _unpack(w)
    return jnp.concatenate([re, im], axis=0).astype(BF16)


def _fft1_kernel(f_ref, x_ref, o_ref, *, r):
    n_ch = x_ref.shape[-1]
    x = x_ref[...].reshape((r // 2) * N1_BLOCK, n_ch).astype(BF16)
    res = jnp.dot(f_ref[...], x, preferred_element_type=F32)
    half = r * N1_BLOCK
    o_ref[...] = _pack_complex(res[:half], res[half:]).reshape(r, N1_BLOCK, n_ch)


def _fft_stage1(x4, part, f1, r, n_ch):
    nb = x4.shape[1]
    xv = x4.reshape(x4.shape[0], nb, r // 2, r, n_ch)
    return pl.pallas_call(
        functools.partial(_fft1_kernel, r=r),
        grid=(nb, r // N1_BLOCK),
        in_specs=[pl.BlockSpec(f1.shape, lambda b, j: (0, 0)),
                  pl.BlockSpec((None, None, r // 2, N1_BLOCK, n_ch), lambda b, j: (part, b, 0, j, 0))],
        out_specs=pl.BlockSpec((None, r, N1_BLOCK, n_ch), lambda b, j: (b, 0, j, 0)),
        out_shape=jax.ShapeDtypeStruct((nb, r, r, n_ch), U32),
        compiler_params=_cparams(("parallel", "parallel")),
        name="hyena_dft_stage1",
    )(f1, xv)


def _k2_block(r):
    return max(1, (2 * LANES) // r)


def _filter_spec_kernel(mf_ref, bf_ref, bb_ref, o_ref, *, r):
    for kk in range(mf_ref.shape[0]):
        m = mf_ref[kk]
        xf = jnp.dot(m, _unpack_complex_rows(bf_ref[kk]), preferred_element_type=F32)
        xb = jnp.dot(m, _unpack_complex_rows(bb_ref[kk]), preferred_element_type=F32)
        o_ref[kk] = _pack_complex(xf[:r] + xb[:r], xf[r:] - xb[r:])


def _filter_spectrum(b1, mf, r, n_ch):
    kb = _k2_block(r)
    blk = lambda sel: pl.BlockSpec((None, kb, r, n_ch), lambda o, k: (2 * o + sel, k, 0, 0))
    return pl.pallas_call(
        functools.partial(_filter_spec_kernel, r=r),
        grid=(HYENA_ORDER, r // kb),
        in_specs=[pl.BlockSpec((kb, 2 * r, 2 * r), lambda o, k: (k, 0, 0)), blk(0), blk(1)],
        out_specs=pl.BlockSpec((None, kb, r, n_ch), lambda o, k: (o, k, 0, 0)),
        out_shape=jax.ShapeDtypeStruct((HYENA_ORDER, r, r, n_ch), U32),
        compiler_params=_cparams(("parallel", "parallel")),
        name="hyena_filter_spectrum",
    )(mf, b1, b1)


def _fft2_kernel(mf_ref, mi_ref, b_ref, h_ref, o_ref, *, r):
    for kk in range(mf_ref.shape[0]):
        x = jnp.dot(mf_ref[kk], _unpack_complex_rows(b_ref[kk]), preferred_element_type=F32)
        xr, xi = x[:r], x[r:]
        hr, hi = _unpack(h_ref[kk])
        y = jnp.concatenate([xr * hr - xi * hi, xr * hi + xi * hr], axis=0).astype(BF16)
        c = jnp.dot(mi_ref[kk], y, preferred_element_type=F32)
        o_ref[kk] = _pack_complex(c[:r], c[r:])


def _fft_stage2(b1, h, order, mf, minv, r, n_ch):
    nb = b1.shape[0]
    kb = _k2_block(r)
    blk = pl.BlockSpec((None, kb, r, n_ch), lambda k, b: (b, k, 0, 0))
    return pl.pallas_call(
        functools.partial(_fft2_kernel, r=r),
        grid=(r // kb, nb),
        in_specs=[pl.BlockSpec((kb, 2 * r, 2 * r), lambda k, b: (k, 0, 0)),
                  pl.BlockSpec((kb, 2 * r, 2 * r), lambda k, b: (k, 0, 0)),
                  blk,
                  pl.BlockSpec((None, kb, r, n_ch), lambda k, b: (order, k, 0, 0))],
        out_specs=blk,
        out_shape=jax.ShapeDtypeStruct((nb, r, r, n_ch), U32),
        compiler_params=_cparams(("parallel", "parallel")),
        name="hyena_dft_stage2",
    )(mf, minv, b1, h)


def _fft3_kernel(g_ref, c_ref, gate_ref, s_ref, bias_ref, ng_ref, *rest, final):
    if final:
        (o_ref,) = rest
    else:
        f_ref, o_ref, b_ref = rest
    r, nb, n_ch = c_ref.shape
    cc = _unpack_complex_rows(c_ref[...].reshape(r * nb, n_ch))
    y = jnp.dot(g_ref[...], cc, preferred_element_type=F32)
    rows = (r // 2) * nb
    s_new = gate_ref[...].reshape(rows, n_ch) * (y + s_ref[...].reshape(rows, n_ch) * bias_ref[...])
    if final:
        ms = jnp.mean(s_new * s_new, axis=-1, keepdims=True)
        s_new = s_new * lax.rsqrt(ms + RMS_EPS) * ng_ref[...]
    o_ref[...] = s_new.reshape(r // 2, nb, n_ch)
    if not final:
        res = jnp.dot(f_ref[...], s_new.astype(BF16), preferred_element_type=F32)
        half = r * nb
        b_ref[...] = _pack_complex(res[:half], res[half:]).reshape(r, nb, n_ch)


def _fft_stage3(c2, g3, z4, gate_part, s4, s_part, bias, norm_g, f1, r, n_ch):
    final = f1 is None
    nb = c2.shape[0]
    zv = z4.reshape(z4.shape[0], nb, r // 2, r, n_ch)
    sv = s4.reshape(s4.shape[0], nb, r // 2, r, n_ch)
    dspec = lambda part: pl.BlockSpec((None, None, r // 2, N1_BLOCK, n_ch), lambda b, j: (part, b, 0, j, 0))
    packed = pl.BlockSpec((None, r, N1_BLOCK, n_ch), lambda b, j: (b, 0, j, 0))
    s_spec = pl.BlockSpec((None, r // 2, N1_BLOCK, n_ch), lambda b, j: (b, 0, j, 0))
    s_shape = jax.ShapeDtypeStruct((nb, r // 2, r, n_ch), F32)
    in_specs = [pl.BlockSpec(g3.shape, lambda b, j: (0, 0)), packed, dspec(gate_part), dspec(s_part),
                pl.BlockSpec((1, n_ch), lambda b, j: (0, 0)), pl.BlockSpec((1, n_ch), lambda b, j: (0, 0))]
    args = [g3, c2, zv, sv, bias.reshape(1, n_ch).astype(F32), norm_g.reshape(1, n_ch).astype(F32)]
    if final:
        out_specs, out_shape = s_spec, s_shape
    else:
        in_specs.append(pl.BlockSpec(f1.shape, lambda b, j: (0, 0)))
        args.append(f1)
        out_specs = [s_spec, packed]
        out_shape = [s_shape, jax.ShapeDtypeStruct((nb, r, r, n_ch), U32)]
    return pl.pallas_call(
        functools.partial(_fft3_kernel, final=final),
        grid=(nb, r // N1_BLOCK),
        in_specs=in_specs,
        out_specs=out_specs,
        out_shape=out_shape,
        compiler_params=_cparams(("parallel", "parallel")),
        name="hyena_dft_stage3",
    )(*args)


def _hyena(zh, row_off, batch, seq, conv_w, conv_b, filt, fft_bias, out_g):
    n_ch = zh.shape[1] // 3
    r = int(round(math.sqrt(2 * seq)))
    assert r * r == 2 * seq and r % 16 == 0
    f1, mf, minv, g3 = _dft_tables(r)
    sig = _hyena_filter_signals(seq, *filt, n_ch=n_ch)
    hb1 = _fft_stage1(sig[None], 0, f1, r, n_ch)
    h = _filter_spectrum(hb1, mf, r, n_ch)
    z4 = _shortconv(zh, row_off, conv_w, conv_b, batch, seq)
    s4, s_part = z4, 2
    b1 = _fft_stage1(s4, s_part, f1, r, n_ch)
    for o in range(HYENA_ORDER):
        c2 = _fft_stage2(b1, h, o, mf, minv, r, n_ch)
        if o < HYENA_ORDER - 1:
            s, b1 = _fft_stage3(c2, g3, z4, o, s4, s_part, fft_bias[o], out_g, f1, r, n_ch)
        else:
            s = _fft_stage3(c2, g3, z4, o, s4, s_part, fft_bias[o], out_g, None, r, n_ch)
        s4, s_part = s.reshape(1, batch, seq, n_ch), 0
    return s4.reshape(batch * seq, n_ch)


def _extract_top(s, key, count):
    vals, keys = [], []
    for _ in range(count):
        m = jnp.max(s, axis=0, keepdims=True)
        kmin = jnp.min(jnp.where(s == m, key, jnp.inf), axis=0, keepdims=True)
        s = jnp.where(key == kmin, -jnp.inf, s)
        vals.append(m)
        keys.append(kmin)
    return vals, keys


def _peer_topk_kernel(q_ref, keys_ref, eid_ref, gate_ref):
    t = q_ref.shape[0]
    nk = PEER_NKEYS
    q = q_ref[...]
    row_key = lax.broadcasted_iota(I32, (nk, t), 0).astype(F32)
    tops = []
    for c in range(2):
        s = lax.dot_general(keys_ref[c], q[:, c * nk:(c + 1) * nk], (((1,), (1,)), ((), ())),
                            preferred_element_type=F32)
        tops.append(_extract_top(s, row_key, PEER_TOPK))
    (v1, i1), (v2, i2) = tops
    rows16 = lax.broadcasted_iota(I32, (PEER_TOPK, t), 0)
    v2a = jnp.zeros((PEER_TOPK, t), F32)
    i2a = jnp.zeros((PEER_TOPK, t), F32)
    for j in range(PEER_TOPK):
        v2a = jnp.where(rows16 == j, v2[j], v2a)
        i2a = jnp.where(rows16 == j, i2[j], i2a)
    n_exp = float(nk * nk)
    half = PEER_TOPK // 2
    rows8 = lax.broadcasted_iota(I32, (half, t), 0)
    pos8 = rows8.astype(F32)
    v2h, i2h = v2a[:half], i2a[:half]
    cand = [v1[0] + v2a]
    ckey = [rows16.astype(F32) * n_exp + (i1[0] * float(nk) + i2a)]
    for i in range(1, half):
        cand.append(jnp.where(rows8 < PEER_TOPK // (i + 1), v1[i] + v2h, -jnp.inf))
        ckey.append((pos8 + float(i * PEER_TOPK)) * n_exp + (i1[i] * float(nk) + i2h))
    v1t = jnp.zeros((half, t), F32)
    i1t = jnp.zeros((half, t), F32)
    for r in range(half):
        v1t = jnp.where(rows8 == r, v1[half + r], v1t)
        i1t = jnp.where(rows8 == r, i1[half + r], i1t)
    cand.append(v1t + v2[0])
    ckey.append((pos8 + float(half)) * (PEER_TOPK * n_exp) + (i1t * float(nk) + i2[0]))
    tv, tk_ = _extract_top(jnp.concatenate(cand, axis=0), jnp.concatenate(ckey, axis=0), PEER_TOPK)
    denom = jnp.zeros((1, t), F32)
    es = []
    for k in range(PEER_TOPK):
        e = jnp.exp(tv[k] - tv[0])
        es.append(e)
        denom = denom + e
    eid = jnp.zeros((PEER_TOPK, t), F32)
    gate = jnp.zeros((PEER_TOPK, t), F32)
    for k in range(PEER_TOPK):
        pos = jnp.floor(tk_[k] * (1.0 / n_exp))
        eid = jnp.where(rows16 == k, tk_[k] - pos * n_exp, eid)
        gate = jnp.where(rows16 == k, es[k] / denom, gate)
    eid_ref[...] = eid.astype(I32)
    gate_ref[...] = gate


def _peer_topk(q, keys, tm=256):
    t = q.shape[0]
    tm = min(tm, t)
    out_spec = pl.BlockSpec((PEER_TOPK, tm), lambda i, h: (h, i))
    return pl.pallas_call(
        _peer_topk_kernel,
        grid=(t // tm, PEER_HEADS),
        in_specs=[pl.BlockSpec((tm, 2 * PEER_NKEYS), lambda i, h: (i, h)),
                  pl.BlockSpec((None, 2, PEER_NKEYS, PEER_NKEYS), lambda i, h: (h, 0, 0, 0))],
        out_specs=[out_spec, out_spec],
        out_shape=[jax.ShapeDtypeStruct((PEER_HEADS * PEER_TOPK, t), I32),
                   jax.ShapeDtypeStruct((PEER_HEADS * PEER_TOPK, t), F32)],
        compiler_params=_cparams(("parallel", "parallel")),
        name="peer_topk",
    )(q, keys)


def _pack_table(tab):
    e, d = tab.shape
    assert d == 2 * SUBLANES * LANES
    bits = lax.bitcast_convert_type(tab.astype(BF16), jnp.uint16).astype(U32)
    packed = bits[:, :d // 2] | (bits[:, d // 2:] << 16)
    return packed.reshape(e * SUBLANES, LANES)


def _unpack(w):
    lo = lax.bitcast_convert_type(w << 16, F32)
    hi = lax.bitcast_convert_type(w & jnp.uint32(0xFFFF0000), F32)
    return lo, hi


_BITREV8 = (0, 4, 2, 6, 1, 5, 3, 7)


def _sublane_fold8(parts):
    sub = lax.broadcasted_iota(I32, (2 * SUBLANES, LANES), 0) // 2

    def rolled(a, shift):
        return pltpu.bitcast(pltpu.roll(pltpu.bitcast(a, U32), shift, axis=0), BF16)

    lvl = [parts[_BITREV8[r]] for r in range(8)]
    for shift, mask in ((4, sub < 4), (2, (sub % 4) < 2), (1, (sub % 2) < 1)):
        nxt = []
        for a, b in zip(lvl[0::2], lvl[1::2]):
            nxt.append(jnp.where(mask, a + rolled(a, SUBLANES - shift), b + rolled(b, shift)))
        lvl = nxt
    return lvl[0]


def _pair_ranges(half):
    if half == 0:
        return (0, PEER_WINDOW), (PEER_WINDOW, PEER_PAIRS)
    return (PEER_PAIRS - PEER_WINDOW, PEER_PAIRS), (0, PEER_PAIRS - PEER_WINDOW)


def _overflow(n_low, half):
    return n_low > PEER_WINDOW if half == 0 else n_low < PEER_PAIRS - PEER_WINDOW


def _table_rows(tab_ref, off):
    return _unpack(tab_ref[pl.ds(pl.multiple_of(off, SUBLANES), SUBLANES), :])


def _peer_u_kernel(off_ref, nlow_ref, tab_ref, x_ref, o_ref, extra_ref, *, tm, half):
    lane = lax.broadcasted_iota(I32, (SUBLANES, LANES), 1)
    sub = lax.broadcasted_iota(I32, (SUBLANES, LANES), 0)
    lane_grp = lax.shift_right_logical(lane, 3)
    diag = sub == (lane & (SUBLANES - 1))
    main, rest = _pair_ranges(half)

    def folded(t, p0, p1):
        xb = pltpu.bitcast(x_ref[t], BF16)
        out = []
        for g in range(p0 // SUBLANES, p1 // SUBLANES):
            parts = []
            for r in range(SUBLANES):
                off = pl.multiple_of(off_ref[t, g * SUBLANES + r], SUBLANES)
                parts.append(pltpu.bitcast(tab_ref[pl.ds(off, SUBLANES), :], BF16) * xb)
            out.append(pltpu.bitcast(_sublane_fold8(parts), U32))
        return out

    def lane_sums(folds, p0):
        mat = jnp.zeros((SUBLANES, LANES), F32)
        for i, f in enumerate(folds):
            lo, hi = _unpack(f)
            mat = jnp.where(lane_grp == p0 // SUBLANES + i, jnp.sum(lo + hi, axis=-1, keepdims=True), mat)
        return jnp.sum(jnp.where(diag, mat, 0.0), axis=0, keepdims=True)

    def finish(t, folds):
        o_ref[pl.ds(t, 1), :] = lane_sums(folds, main[0]) + extra_ref[pl.ds(t, 1), :]

    def token(t, prev):
        cur = folded(t, *main)
        finish(jnp.maximum(t - 1, 0), prev)
        extra_ref[pl.ds(t, 1), :] = jnp.zeros((1, LANES), F32)

        @pl.when(_overflow(nlow_ref[0, t], half))
        def _():
            extra_ref[pl.ds(t, 1), :] = lane_sums(folded(t, *rest), rest[0])

        return tuple(cur)

    extra_ref[pl.ds(0, 1), :] = jnp.zeros((1, LANES), F32)
    zeros = tuple(jnp.zeros((SUBLANES, LANES), U32) for _ in range((main[1] - main[0]) // SUBLANES))
    last = lax.fori_loop(0, tm, token, zeros)
    finish(tm - 1, last)


def _peer_u(off, nlow3, tab, x4, half, tm=128):
    t = off.shape[0]
    rows = PEER_HALF * SUBLANES
    return pl.pallas_call(
        functools.partial(_peer_u_kernel, tm=tm, half=half),
        grid=(t // tm,),
        in_specs=[pl.BlockSpec((tm, LANES), lambda i: (i, 0), memory_space=pltpu.SMEM),
                  pl.BlockSpec((None, 1, tm), lambda i: (i, 0, 0), memory_space=pltpu.SMEM),
                  pl.BlockSpec((rows, LANES), lambda i: (half, 0), pipeline_mode=pl.Buffered(1)),
                  pl.BlockSpec((tm, SUBLANES, LANES), lambda i: (i, 0, 0))],
        out_specs=pl.BlockSpec((tm, LANES), lambda i: (i, 0)),
        out_shape=jax.ShapeDtypeStruct((t, LANES), F32),
        scratch_shapes=[pltpu.VMEM((tm, LANES), F32)],
        compiler_params=_cparams(("arbitrary",)),
        name="peer_expert_scores",
    )(off, nlow3, tab, x4)


def _peer_coef_kernel(a0_ref, a1_ref, eid_ref, gate_ref, c0_ref, c1_ref):
    low = eid_ref[...] < PEER_HALF
    a = jnp.where(low, a0_ref[...], a1_ref[...])
    coef = gate_ref[...] * (0.5 * a * (1.0 + lax.erf(a * (1.0 / math.sqrt(2.0)))))
    c0_ref[...] = jnp.where(low, coef, 0.0)
    c1_ref[...] = jnp.where(low, 0.0, coef)


def _peer_coef(a0, a1, eid, gate, tm=1024):
    t = eid.shape[0]
    tm = min(tm, t)
    spec = pl.BlockSpec((tm, LANES), lambda i: (i, 0))
    return pl.pallas_call(
        _peer_coef_kernel,
        grid=(t // tm,),
        in_specs=[spec, spec, spec, spec],
        out_specs=[spec, spec],
        out_shape=[jax.ShapeDtypeStruct((t, LANES), F32)] * 2,
        compiler_params=_cparams(("parallel",)),
        name="peer_coef",
    )(a0, a1, eid, gate)


def _peer_v_kernel(off_ref, nlow_ref, coef_ref, tab_ref, base_ref, *o_refs, tm, half, n_first):
    n_acc = 4
    main, rest = _pair_ranges(half)

    def weighted(t, p0, p1):
        acc_lo = [jnp.zeros((SUBLANES, LANES), F32) for _ in range(n_acc)]
        acc_hi = [jnp.zeros((SUBLANES, LANES), F32) for _ in range(n_acc)]
        for p in range(p0, p1):
            c = coef_ref[t, p]
            lo, hi = _table_rows(tab_ref, off_ref[t, p])
            acc_lo[p % n_acc] = acc_lo[p % n_acc] + c * lo
            acc_hi[p % n_acc] = acc_hi[p % n_acc] + c * hi
        return ((acc_lo[0] + acc_lo[1]) + (acc_lo[2] + acc_lo[3]),
                (acc_hi[0] + acc_hi[1]) + (acc_hi[2] + acc_hi[3]))

    def run(o_ref):
        def store_row(t, lo, hi):
            o_ref[t, 0] = base_ref[t, 0] + lo
            o_ref[t, 1] = base_ref[t, 1] + hi

        def token(t, carry):
            lo, hi = weighted(t, *main)
            store_row(t, lo, hi)

            @pl.when(_overflow(nlow_ref[0, t], half))
            def _():
                lo2, hi2 = weighted(t, *rest)
                store_row(t, lo + lo2, hi + hi2)

            return carry

        lax.fori_loop(0, tm, token, 0)

    if len(o_refs) == 1:
        run(o_refs[0])
    else:
        first = pl.program_id(0) < n_first
        pl.when(first)(lambda: run(o_refs[0]))
        pl.when(jnp.logical_not(first))(lambda: run(o_refs[1]))


def _peer_v(off, nlow3, coef, tab, base, half, tm=128, split=None):
    t = base.shape[0]
    rows = PEER_HALF * SUBLANES
    smem = pl.BlockSpec((tm, LANES), lambda i: (i, 0), memory_space=pltpu.SMEM)
    tile = pl.BlockSpec((tm, 2, SUBLANES, LANES), lambda i: (i, 0, 0, 0))
    if split is None:
        n_first, out_specs = 0, tile
        out_shape = jax.ShapeDtypeStruct((t, 2, SUBLANES, LANES), F32)
    else:
        assert split % tm == 0
        n_first, n_rest = split // tm, (t - split) // tm
        blk = (tm, 2, SUBLANES, LANES)
        out_specs = [pl.BlockSpec(blk, lambda i: (jnp.clip(i, 0, n_first - 1), 0, 0, 0)),
                     pl.BlockSpec(blk, lambda i: (jnp.clip(i - n_first, 0, n_rest - 1), 0, 0, 0))]
        out_shape = [jax.ShapeDtypeStruct((split, 2, SUBLANES, LANES), F32),
                     jax.ShapeDtypeStruct((t - split, 2, SUBLANES, LANES), F32)]
    return pl.pallas_call(
        functools.partial(_peer_v_kernel, tm=tm, half=half, n_first=n_first),
        grid=(t // tm,),
        in_specs=[smem,
                  pl.BlockSpec((None, 1, tm), lambda i: (i, 0, 0), memory_space=pltpu.SMEM),
                  smem,
                  pl.BlockSpec((rows, LANES), lambda i: (half, 0), pipeline_mode=pl.Buffered(1)),
                  tile],
        out_specs=out_specs,
        out_shape=out_shape,
        compiler_params=_cparams(("arbitrary",)),
        name="peer_expert_sum",
    )(off, nlow3, coef, tab, base)


def _peer(x1, split, norm2_g, wq_bf, keys_bf, u_packed, v_packed, tm=128):
    t, d = x1.shape
    tm = min(tm, t)
    (xn_bf,) = _rmsnorm(x1, norm2_g, (BF16,))
    q = _matmul(xn_bf, wq_bf, wq_bf.shape[1], 0, BF16)
    eid_t, gate_t = _peer_topk(q, keys_bf)
    eid, gate = eid_t.T, gate_t.T
    high = (eid >= PEER_HALF).astype(I32)
    nlow3 = (PEER_PAIRS - jnp.sum(high, axis=-1, dtype=I32)).reshape(t // tm, 1, tm)
    _, eid, gate = lax.sort((high, eid, gate), dimension=1, is_stable=True, num_keys=1)
    off = (eid & (PEER_HALF - 1)) * SUBLANES
    xw = _pack_table(xn_bf).reshape(t, SUBLANES, LANES)
    a0 = _peer_u(off, nlow3, u_packed, xw, 0, tm)
    a1 = _peer_u(off, nlow3, u_packed, xw, 1, tm)
    c0, c1 = _peer_coef(a0, a1, eid, gate)
    y = _peer_v(off, nlow3, c0, v_packed, x1.reshape(t, 2, SUBLANES, LANES), 0, tm)
    ya, yb = _peer_v(off, nlow3, c1, v_packed, y, 1, tm, split=split)
    return ya.reshape(split, d), yb.reshape(t - split, d)


def kernel(x_prompt, x_sample, norm1_g, w_in, q_norm_g, k_norm_g, lambda_q1, lambda_k1, lambda_q2, lambda_k2, attn_sub_g, conv_w, conv_b, filt_w1, filt_b1, filt_w2, filt_b2, filt_w3, filt_b3, filt_w4, filt_freq, fft_bias, hyena_out_g, w_out, norm2_g, peer_wq, peer_keys, peer_u, peer_v):
    depth = w_in.shape[0]
    d_model = x_prompt.shape[-1]
    att_w = ATT_HEADS * ATT_VDIM
    shapes = [x_prompt.shape[:2], x_sample.shape[:2]]
    xs = [x_prompt.reshape(-1, d_model), x_sample.reshape(-1, d_model)]
    n0 = xs[0].shape[0]
    slopes = 2.0 ** (-8.0 * jnp.arange(1, ATT_HEADS + 1, dtype=F32) / ATT_HEADS)

    for l in range(depth):
        lambda_init = 0.8 - 0.6 * math.exp(-0.3 * l)
        lam = (jnp.exp(jnp.sum(lambda_q1[l].astype(F32) * lambda_k1[l].astype(F32)))
               - jnp.exp(jnp.sum(lambda_q2[l].astype(F32) * lambda_k2[l].astype(F32)))
               + lambda_init).reshape(1)
        w_in_bf = w_in[l].astype(BF16)
        q_gain = jnp.tile(q_norm_g[l].astype(F32), 2 * ATT_HEADS) * (ATT_QKDIM ** -0.5 * LOG2E)
        k_gain = jnp.tile(k_norm_g[l].astype(F32), 2 * ATT_HEADS)
        qk_gain = jnp.concatenate([q_gain, k_gain]).reshape(1, 2 * att_w)

        h_bf = _rmsnorm_stacked(xs[0], xs[1], norm1_g[l], BF16)
        qk = _matmul(h_bf, w_in_bf, 2 * att_w, 0, BF16, mode="qknorm", extra=qk_gain)
        vt = _matmul_nt(w_in_bf[:, 2 * att_w:3 * att_w].T, h_bf, BF16)
        zh = _matmul(h_bf, w_in_bf, w_in.shape[2] - 3 * att_w, 3 * att_w, F32)

        filt = (filt_w1[l], filt_b1[l], filt_w2[l], filt_b2[l], filt_w3[l], filt_b3[l], filt_w4[l], filt_freq[l])
        segs, row = [], 0
        for (b, s), x_seg in zip(shapes, xs):
            att = _attention(qk, vt, row, slopes, lam, attn_sub_g[l], b, s, 1.0 - lambda_init)
            hy = _hyena(zh, row, b, s, conv_w[l], conv_b[l], filt, fft_bias[l], hyena_out_g[l])
            segs.append((att, hy, x_seg))
            row += b * s
        x1 = _outproj_stacked(segs[0], segs[1], w_out[l].astype(BF16))

        xs = _peer(x1, n0, norm2_g[l], peer_wq[l].astype(BF16), peer_keys[l].astype(BF16),
                   _pack_table(peer_u[l]), _pack_table(peer_v[l]))

    return (xs[0].reshape(x_prompt.shape), xs[1].reshape(x_sample.shape))
```

```python
import functools
import math

import jax
import jax.numpy as jnp
from jax import lax
from jax.experimental import pallas as pl
from jax.experimental.pallas import tpu as pltpu

F32 = jnp.float32
BF16 = jnp.bfloat16
I32 = jnp.int32
U32 = jnp.uint32

RMS_EPS = 1e-6
LOG2E = 1.4426950408889634
LANES = 128
SUBLANES = 8
VMEM_LIMIT_BYTES = 56 * 1024 * 1024

ATT_HEADS = 8
ATT_VDIM = 128
ATT_QKDIM = 64
HYENA_ORDER = 2
FILTER_BANDS = 16
DECAY_FAST = 0.3
DECAY_SLOW = 1.5
DECAY_TARGET = 1e-2
PEER_HEADS = 8
PEER_NKEYS = 128
PEER_TOPK = 16
PEER_HALF = PEER_NKEYS * PEER_NKEYS // 2
PEER_PAIRS = PEER_HEADS * PEER_TOPK
PEER_WINDOW = 80


def _cparams(sem, vmem=VMEM_LIMIT_BYTES):
    return pltpu.CompilerParams(dimension_semantics=sem, vmem_limit_bytes=vmem)


def _segment_spec(block, start, count, minor=0):
    return pl.BlockSpec(block, lambda i: (jnp.clip(i - start, 0, count - 1), minor))


def _rmsnorm_kernel(x_ref, g_ref, *o_refs):
    x = x_ref[...]
    ms = jnp.mean(x * x, axis=-1, keepdims=True)
    y = x * lax.rsqrt(ms + RMS_EPS) * g_ref[...]
    for o_ref in o_refs:
        o_ref[...] = y.astype(o_ref.dtype)


def _rmsnorm2_kernel(xa_ref, xb_ref, g_ref, o_ref, *, na):
    first = pl.program_id(0) < na
    x = jnp.where(first, xa_ref[...], xb_ref[...])
    ms = jnp.mean(x * x, axis=-1, keepdims=True)
    o_ref[...] = (x * lax.rsqrt(ms + RMS_EPS) * g_ref[...]).astype(o_ref.dtype)


def _rmsnorm_stacked(xa, xb, g, out_dtype, tm=512):
    d = xa.shape[1]
    na, nb = xa.shape[0] // tm, xb.shape[0] // tm
    assert xa.shape[0] % tm == 0 and xb.shape[0] % tm == 0
    return pl.pallas_call(
        functools.partial(_rmsnorm2_kernel, na=na),
        grid=(na + nb,),
        in_specs=[_segment_spec((tm, d), 0, na), _segment_spec((tm, d), na, nb),
                  pl.BlockSpec((1, d), lambda i: (0, 0))],
        out_specs=pl.BlockSpec((tm, d), lambda i: (i, 0)),
        out_shape=jax.ShapeDtypeStruct((xa.shape[0] + xb.shape[0], d), out_dtype),
        compiler_params=_cparams(("parallel",)),
        name="rmsnorm_stacked",
    )(xa, xb, g.reshape(1, d).astype(F32))


def _rmsnorm(x, g, out_dtypes, tm=512):
    t, d = x.shape
    tm = min(tm, t)
    spec = pl.BlockSpec((tm, d), lambda i: (i, 0))
    return pl.pallas_call(
        _rmsnorm_kernel,
        grid=(t // tm,),
        in_specs=[spec, pl.BlockSpec((1, d), lambda i: (0, 0))],
        out_specs=[spec for _ in out_dtypes],
        out_shape=[jax.ShapeDtypeStruct((t, d), dt) for dt in out_dtypes],
        compiler_params=_cparams(("parallel",)),
        name="rmsnorm",
    )(x, g.reshape(1, d).astype(F32))


def _group_rms_scale(x, gain):
    lane = lax.broadcasted_iota(I32, x.shape, 1)
    lo = lane < ATT_QKDIM
    x2 = x * x
    s_lo = jnp.sum(jnp.where(lo, x2, 0.0), axis=-1, keepdims=True)
    s_hi = jnp.sum(jnp.where(lo, 0.0, x2), axis=-1, keepdims=True)
    ms = jnp.where(lo, s_lo, s_hi) * (1.0 / ATT_QKDIM)
    return x * lax.rsqrt(ms + RMS_EPS) * gain


def _mm_kernel(a_ref, b_ref, *rest, mode):
    acc = jnp.dot(a_ref[...], b_ref[...], preferred_element_type=F32)
    if mode == "plain":
        (o_ref,) = rest
        o_ref[...] = acc.astype(o_ref.dtype)
    elif mode == "residual":
        r_ref, o_ref = rest
        o_ref[...] = (acc + r_ref[...]).astype(o_ref.dtype)
    elif mode == "qknorm":
        g_ref, o_ref = rest
        for c in range(acc.shape[1] // LANES):
            sl = slice(c * LANES, (c + 1) * LANES)
            o_ref[:, sl] = _group_rms_scale(acc[:, sl], g_ref[:, sl]).astype(o_ref.dtype)
    else:
        raise ValueError(mode)


def _matmul(a, b, n_cols, col_off, out_dtype, mode="plain", extra=None, tm=1024, tn=1024):
    m, k = a.shape
    tm = min(tm, m)
    tn = min(tn, n_cols)
    assert col_off % tn == 0 and n_cols % tn == 0 and m % tm == 0
    off = col_off // tn
    in_specs = [pl.BlockSpec((tm, k), lambda i, j: (i, 0)),
                pl.BlockSpec((k, tn), lambda i, j: (0, j + off))]
    args = [a, b]
    if mode == "residual":
        in_specs.append(pl.BlockSpec((tm, tn), lambda i, j: (i, j)))
        args.append(extra)
    elif mode == "qknorm":
        in_specs.append(pl.BlockSpec((1, tn), lambda i, j: (0, j)))
        args.append(extra)
    return pl.pallas_call(
        functools.partial(_mm_kernel, mode=mode),
        grid=(m // tm, n_cols // tn),
        in_specs=in_specs,
        out_specs=pl.BlockSpec((tm, tn), lambda i, j: (i, j)),
        out_shape=jax.ShapeDtypeStruct((m, n_cols), out_dtype),
        compiler_params=_cparams(("parallel", "arbitrary")),
        name="matmul_" + mode,
    )(*args)


def _outproj_kernel(att_a, hy_a, x_a, att_b, hy_b, x_b, w_ref, o_ref, *, na):
    kw = att_a.shape[1]

    def run(att_ref, hy_ref, x_ref):
        acc = jnp.dot(att_ref[...], w_ref[:kw, :], preferred_element_type=F32)
        acc = acc + jnp.dot(hy_ref[...].astype(BF16), w_ref[kw:, :], preferred_element_type=F32)
        o_ref[...] = acc + x_ref[...]

    first = pl.program_id(0) < na
    pl.when(first)(lambda: run(att_a, hy_a, x_a))
    pl.when(jnp.logical_not(first))(lambda: run(att_b, hy_b, x_b))


def _outproj_stacked(seg_a, seg_b, w, tm=512, tn=1024):
    n_out = w.shape[1]
    na, nb = seg_a[0].shape[0] // tm, seg_b[0].shape[0] // tm
    assert seg_a[0].shape[0] % tm == 0 and seg_b[0].shape[0] % tm == 0 and n_out % tn == 0

    def specs(seg, start, count):
        att, hy, x = seg
        row = lambda i, j: (jnp.clip(i - start, 0, count - 1), 0)
        return [pl.BlockSpec((tm, att.shape[1]), row), pl.BlockSpec((tm, hy.shape[1]), row),
                pl.BlockSpec((tm, tn), lambda i, j: (jnp.clip(i - start, 0, count - 1), j))]

    return pl.pallas_call(
        functools.partial(_outproj_kernel, na=na),
        grid=(na + nb, n_out // tn),
        in_specs=specs(seg_a, 0, na) + specs(seg_b, na, nb) + [pl.BlockSpec((w.shape[0], tn), lambda i, j: (0, j))],
        out_specs=pl.BlockSpec((tm, tn), lambda i, j: (i, j)),
        out_shape=jax.ShapeDtypeStruct(((na + nb) * tm, n_out), F32),
        compiler_params=_cparams(("parallel", "arbitrary")),
        name="out_projection",
    )(*seg_a, *seg_b, w)


def _mm_nt_kernel(w_ref, a_ref, o_ref):
    o_ref[...] = lax.dot_general(w_ref[...], a_ref[...], (((1,), (1,)), ((), ())),
                                 preferred_element_type=F32).astype(o_ref.dtype)


def _matmul_nt(wt, a, out_dtype, tm=1024):
    n, k = wt.shape
    m = a.shape[0]
    tm = min(tm, m)
    return pl.pallas_call(
        _mm_nt_kernel,
        grid=(m // tm,),
        in_specs=[pl.BlockSpec((n, k), lambda i: (0, 0)), pl.BlockSpec((tm, k), lambda i: (i, 0))],
        out_specs=pl.BlockSpec((n, tm), lambda i: (0, i)),
        out_shape=jax.ShapeDtypeStruct((n, m), out_dtype),
        compiler_params=_cparams(("parallel",)),
        name="matmul_nt",
    )(wt, a)


def _alibi_columns(slopes, tq, tk):
    return [_alibi_side(slopes, tq, True), _alibi_side(slopes, tk, False)]


def _alibi_side(slopes, n, query_side):
    pos = jnp.arange(n, dtype=F32)
    val = (slopes.astype(F32) * LOG2E)[:, None] * pos[None, :]

    def pieces(x):
        p1 = x.astype(BF16)
        r1 = x - p1.astype(F32)
        p2 = r1.astype(BF16)
        p3 = (r1 - p2.astype(F32)).astype(BF16)
        return [p1, p2, p3]

    ones = [jnp.ones_like(val, BF16)] * 3
    six = jnp.stack(pieces(-val) + ones if query_side else ones + pieces(val), axis=-1)
    pad = jnp.zeros(val.shape + (ATT_QKDIM - 6,), BF16)
    return jnp.concatenate([six, pad, six, pad], axis=-1)


def _attn_kernel(slope_ref, lam_ref, q_ref, k_ref, vt_ref, aq_ref, ak_ref, g_ref, o_ref,
                 m_ref, l_ref, acc_ref, *, tq, tk, nk, hb, out_scale):
    hg = pl.program_id(1)
    i = pl.program_id(2)
    j = pl.program_id(3)

    @pl.when(j == 0)
    def _():
        m_ref[...] = jnp.full(m_ref.shape, -jnp.inf, F32)
        l_ref[...] = jnp.zeros(l_ref.shape, F32)
        acc_ref[...] = jnp.zeros(acc_ref.shape, F32)

    q_first = lax.broadcasted_iota(I32, (tq, LANES), 1) < ATT_QKDIM
    k_first = lax.broadcasted_iota(I32, (tk, LANES), 1) < ATT_QKDIM
    nt = (((1,), (1,)), ((), ()))

    def update(hh, scores, shift):
        vt = vt_ref[hh * LANES:(hh + 1) * LANES, :]
        for c, s in enumerate(scores):
            m_old = m_ref[hh, c]
            m_new = jnp.maximum(m_old, jnp.max(s, axis=0, keepdims=True) + shift)
            alpha = jnp.exp2(m_old - m_new)
            p = jnp.exp2(s - (m_new - shift))
            l_ref[hh, c] = alpha * l_ref[hh, c] + jnp.sum(p, axis=0, keepdims=True)
            acc_ref[hh, c] = alpha * acc_ref[hh, c] + jnp.dot(vt, p.astype(BF16),
                                                              preferred_element_type=F32)
            m_ref[hh, c] = m_new

    keys_before = i * tq >= (j + 1) * tk
    keys_after = (i + 1) * tq <= j * tk
    off_diagonal = jnp.logical_or(keys_before, keys_after)

    @pl.when(off_diagonal)
    def _():
        sign = jnp.where(keys_after, -1.0, 1.0).astype(BF16)
        gap = jnp.abs(i * tq - j * tk).astype(F32)
        for hh in range(hb):
            sl = slice(hh * LANES, (hh + 1) * LANES)
            q = q_ref[:, sl]
            k = k_ref[:, sl]
            aq = aq_ref[hh]
            ak = ak_ref[hh] * sign
            s0 = lax.dot_general(jnp.where(k_first, k, ak), jnp.where(q_first, q, aq), nt,
                                 preferred_element_type=F32)
            s1 = lax.dot_general(jnp.where(k_first, ak, k), jnp.where(q_first, aq, q), nt,
                                 preferred_element_type=F32)
            update(hh, (s0, s1), -(slope_ref[hg * hb + hh] * LOG2E) * gap)

    @pl.when(jnp.logical_not(off_diagonal))
    def _():
        kpos = lax.broadcasted_iota(I32, (tk, tq), 0) + j * tk
        qpos = lax.broadcasted_iota(I32, (tk, tq), 1) + i * tq
        dist = jnp.abs(kpos - qpos).astype(F32)
        for hh in range(hb):
            sl = slice(hh * LANES, (hh + 1) * LANES)
            q = q_ref[:, sl]
            k = k_ref[:, sl]
            zero = jnp.zeros_like(q)
            bias = dist * (-(slope_ref[hg * hb + hh] * LOG2E))
            s0 = lax.dot_general(k, jnp.where(q_first, q, zero), nt, preferred_element_type=F32) + bias
            s1 = lax.dot_general(k, jnp.where(q_first, zero, q), nt, preferred_element_type=F32) + bias
            update(hh, (s0, s1), 0.0)

    @pl.when(j == nk - 1)
    def _():
        for hh in range(hb):
            o = acc_ref[hh, 0] / l_ref[hh, 0] - lam_ref[0] * (acc_ref[hh, 1] / l_ref[hh, 1])
            ms = jnp.mean(o * o, axis=0, keepdims=True)
            y = o * lax.rsqrt(ms + RMS_EPS) * (g_ref[...] * out_scale)
            o_ref[:, hh * LANES:(hh + 1) * LANES] = y.T.astype(o_ref.dtype)


def _attention(qk, vt, row_off, slopes, lam, sub_g, batch, seq, out_scale, tq=512, tk=1024, hb=4):
    tq = min(tq, seq)
    tk = min(tk, seq // 4)
    nq, nk = seq // tq, seq // tk
    assert row_off % tq == 0 and row_off % tk == 0 and ATT_HEADS % hb == 0
    oq, ok = row_off // tq, row_off // tk
    ng = ATT_HEADS // hb
    aq, ak = _alibi_columns(slopes, tq, tk)
    kern = functools.partial(_attn_kernel, tq=tq, tk=tk, nk=nk, hb=hb, out_scale=out_scale)
    smem = pl.BlockSpec(memory_space=pltpu.SMEM)
    return pl.pallas_call(
        kern,
        grid=(batch, ng, nq, nk),
        in_specs=[smem, smem,
                  pl.BlockSpec((tq, hb * LANES), lambda b, h, i, j: (oq + b * nq + i, h)),
                  pl.BlockSpec((tk, hb * LANES), lambda b, h, i, j: (ok + b * nk + j, ng + h)),
                  pl.BlockSpec((hb * LANES, tk), lambda b, h, i, j: (h, ok + b * nk + j)),
                  pl.BlockSpec((hb, tq, LANES), lambda b, h, i, j: (h, 0, 0)),
                  pl.BlockSpec((hb, tk, LANES), lambda b, h, i, j: (h, 0, 0)),
                  pl.BlockSpec((LANES, 1), lambda b, h, i, j: (0, 0))],
        out_specs=pl.BlockSpec((tq, hb * LANES), lambda b, h, i, j: (b * nq + i, h)),
        out_shape=jax.ShapeDtypeStruct((batch * seq, ATT_HEADS * ATT_VDIM), BF16),
        scratch_shapes=[pltpu.VMEM((hb, 2, 1, tq), F32), pltpu.VMEM((hb, 2, 1, tq), F32),
                        pltpu.VMEM((hb, 2, LANES, tq), F32)],
        compiler_params=_cparams(("parallel", "parallel", "parallel", "arbitrary")),
        name="diff_attention",
    )(slopes, lam, qk, qk, vt, aq, ak, sub_g.reshape(LANES, 1).astype(F32))


def _shortconv_kernel(z_ref, w_ref, b_ref, o_ref):
    z = z_ref[...]
    n = z.shape[0]
    row = lax.broadcasted_iota(I32, z.shape, 0)
    prev = jnp.where(row == 0, 0.0, pltpu.roll(z, 1, axis=0))
    nxt = jnp.where(row == n - 1, 0.0, pltpu.roll(z, n - 1, axis=0))
    w = w_ref[...]
    o_ref[...] = prev * w[0:1] + z * w[1:2] + nxt * w[2:3] + b_ref[...]


def _shortconv(zh, row_off, conv_w, conv_b, batch, seq, cb=256):
    c3 = zh.shape[1]
    c = c3 // 3
    ncb = c // cb
    assert row_off % seq == 0
    ob = row_off // seq
    out = pl.pallas_call(
        _shortconv_kernel,
        grid=(batch, 3, ncb),
        in_specs=[pl.BlockSpec((seq, cb), lambda b, p, j: (ob + b, p * ncb + j)),
                  pl.BlockSpec((3, cb), lambda b, p, j: (0, p * ncb + j)),
                  pl.BlockSpec((1, cb), lambda b, p, j: (0, p * ncb + j))],
        out_specs=pl.BlockSpec((None, None, seq, cb), lambda b, p, j: (p, b, 0, j)),
        out_shape=jax.ShapeDtypeStruct((3, batch, seq, c), F32),
        compiler_params=_cparams(("parallel", "parallel", "parallel")),
        name="hyena_shortconv",
    )(zh, conv_w.astype(F32), conv_b.reshape(1, c3).astype(F32))
    return out


def _filter_kernel(z_ref, w1_ref, b1_ref, w2_ref, b2_ref, w3_ref, b3_ref, fr_ref, w4_ref,
                   t_ref, d_ref, o_ref, h_ref, *, tl):
    i = pl.program_id(0)
    g = pl.program_id(1)
    hi = lax.Precision.HIGHEST

    @pl.when(g == 0)
    def _():
        fr = fr_ref[...]
        h = jnp.sin(fr * (jnp.dot(z_ref[...], w1_ref[...], precision=hi, preferred_element_type=F32) + b1_ref[...]))
        h = jnp.sin(fr * (jnp.dot(h, w2_ref[...], precision=hi, preferred_element_type=F32) + b2_ref[...]))
        h_ref[...] = jnp.sin(fr * (jnp.dot(h, w3_ref[...], precision=hi, preferred_element_type=F32) + b3_ref[...]))

    f = jnp.dot(h_ref[...].astype(BF16), w4_ref[...], preferred_element_type=F32)
    f = f * jnp.exp(-t_ref[...] * d_ref[...])
    row = lax.broadcasted_iota(I32, f.shape, 0) + i * tl
    drop = jnp.logical_and(row == 0, g % 2 == 1)
    o_ref[...] = jnp.where(drop, 0.0, f)


def _hyena_filter_signals(seq, w1, b1, w2, b2, w3, b3, w4, freq, n_ch, tl=512):
    t = jnp.linspace(0.0, 1.0, seq, dtype=F32)[:, None]
    w = 2.0 * math.pi * jnp.arange(seq, dtype=F32)[:, None] / seq
    f = jnp.linspace(1e-4, FILTER_BANDS - 1, FILTER_BANDS, dtype=F32)[None, :]
    z = jnp.concatenate([t, jnp.cos(f * w), -jnp.sin(f * w)], axis=-1)
    deltas = jnp.abs(jnp.linspace(math.log(DECAY_FAST) / DECAY_TARGET,
                                  math.log(DECAY_SLOW) / DECAY_TARGET, n_ch, dtype=F32))[None, :]
    hid = w1.shape[1]
    emb = LANES
    z = jnp.pad(z, ((0, 0), (0, emb - z.shape[1])))
    w1 = jnp.pad(w1.astype(F32), ((0, emb - w1.shape[0]), (0, 0)))
    tl = min(tl, seq)
    full = lambda shape: pl.BlockSpec(shape, lambda i, g: tuple(0 for _ in shape))
    return pl.pallas_call(
        functools.partial(_filter_kernel, tl=tl),
        grid=(seq // tl, 2 * HYENA_ORDER),
        in_specs=[pl.BlockSpec((tl, emb), lambda i, g: (i, 0)),
                  full((emb, hid)), full((1, hid)), full((hid, hid)), full((1, hid)),
                  full((hid, hid)), full((1, hid)), full((1, hid)),
                  pl.BlockSpec((hid, n_ch), lambda i, g: (0, g)),
                  pl.BlockSpec((tl, 1), lambda i, g: (i, 0)),
                  full((1, n_ch))],
        out_specs=pl.BlockSpec((None, tl, n_ch), lambda i, g: (g, i, 0)),
        out_shape=jax.ShapeDtypeStruct((2 * HYENA_ORDER, seq, n_ch), F32),
        scratch_shapes=[pltpu.VMEM((tl, hid), F32)],
        compiler_params=_cparams(("parallel", "arbitrary")),
        name="hyena_filter_mlp",
    )(z, w1.astype(F32), b1.reshape(1, hid).astype(F32), w2.astype(F32), b2.reshape(1, hid).astype(F32),
      w3.astype(F32), b3.reshape(1, hid).astype(F32), freq.reshape(1, hid).astype(F32), w4.astype(BF16),
      t, deltas)


def _dft_tables(r):
    n = r * r
    k2 = jnp.arange(r, dtype=I32)
    n2 = jnp.arange(r // 2, dtype=I32)
    ang1 = (2.0 * math.pi / r) * ((k2[:, None] * n2[None, :]) % r).astype(F32)
    f1 = jnp.concatenate([jnp.cos(ang1), -jnp.sin(ang1)], axis=0)
    k1 = jnp.arange(r, dtype=I32)
    n1 = jnp.arange(r, dtype=I32)
    kk = r * k1[None, :, None] + k2[:, None, None]
    ang2 = (2.0 * math.pi / n) * ((kk * n1[None, None, :]) % n).astype(F32)
    mr, mi_ = jnp.cos(ang2), -jnp.sin(ang2)
    mf = jnp.concatenate([jnp.concatenate([mr, -mi_], axis=2),
                          jnp.concatenate([mi_, mr], axis=2)], axis=1)
    minv = jnp.swapaxes(mf, 1, 2)
    g3 = jnp.concatenate([jnp.cos(ang1.T), -jnp.sin(ang1.T)], axis=1) * (1.0 / n)
    eye = jnp.eye(N1_BLOCK, dtype=F32)
    f1, g3 = jnp.kron(f1, eye), jnp.kron(g3, eye)
    return f1.astype(BF16), mf.astype(BF16), minv.astype(BF16), g3.astype(BF16)


N1_BLOCK = SUBLANES


def _pack_complex(re, im):
    rb = lax.bitcast_convert_type(re.astype(BF16).astype(F32), U32)
    ib = lax.bitcast_convert_type(im.astype(BF16).astype(F32), U32)
    return (rb >> 16) | ib


def _unpack_complex_rows(w):
    re, im = _unpack(w)
    return jnp.concatenate([re, im], axis=0).astype(BF16)


def _fft1_kernel(f_ref, x_ref, o_ref, *, r):
    n_ch = x_ref.shape[-1]
    x = x_ref[...].reshape((r // 2) * N1_BLOCK, n_ch).astype(BF16)
    res = jnp.dot(f_ref[...], x, preferred_element_type=F32)
    half = r * N1_BLOCK
    o_ref[...] = _pack_complex(res[:half], res[half:]).reshape(r, N1_BLOCK, n_ch)


def _fft_stage1(x4, part, f1, r, n_ch):
    nb = x4.shape[1]
    xv = x4.reshape(x4.shape[0], nb, r // 2, r, n_ch)
    return pl.pallas_call(
        functools.partial(_fft1_kernel, r=r),
        grid=(nb, r // N1_BLOCK),
        in_specs=[pl.BlockSpec(f1.shape, lambda b, j: (0, 0)),
                  pl.BlockSpec((None, None, r // 2, N1_BLOCK, n_ch), lambda b, j: (part, b, 0, j, 0))],
        out_specs=pl.BlockSpec((None, r, N1_BLOCK, n_ch), lambda b, j: (b, 0, j, 0)),
        out_shape=jax.ShapeDtypeStruct((nb, r, r, n_ch), U32),
        compiler_params=_cparams(("parallel", "parallel")),
        name="hyena_dft_stage1",
    )(f1, xv)


def _k2_block(r):
    return max(1, (2 * LANES) // r)


def _filter_spec_kernel(mf_ref, bf_ref, bb_ref, o_ref, *, r):
    for kk in range(mf_ref.shape[0]):
        m = mf_ref[kk]
        xf = jnp.dot(m, _unpack_complex_rows(bf_ref[kk]), preferred_element_type=F32)
        xb = jnp.dot(m, _unpack_complex_rows(bb_ref[kk]), preferred_element_type=F32)
        o_ref[kk] = _pack_complex(xf[:r] + xb[:r], xf[r:] - xb[r:])


def _filter_spectrum(b1, mf, r, n_ch):
    kb = _k2_block(r)
    blk = lambda sel: pl.BlockSpec((None, kb, r, n_ch), lambda o, k: (2 * o + sel, k, 0, 0))
    return pl.pallas_call(
        functools.partial(_filter_spec_kernel, r=r),
        grid=(HYENA_ORDER, r // kb),
        in_specs=[pl.BlockSpec((kb, 2 * r, 2 * r), lambda o, k: (k, 0, 0)), blk(0), blk(1)],
        out_specs=pl.BlockSpec((None, kb, r, n_ch), lambda o, k: (o, k, 0, 0)),
        out_shape=jax.ShapeDtypeStruct((HYENA_ORDER, r, r, n_ch), U32),
        compiler_params=_cparams(("parallel", "parallel")),
        name="hyena_filter_spectrum",
    )(mf, b1, b1)


def _fft2_kernel(mf_ref, mi_ref, b_ref, h_ref, o_ref, *, r):
    for kk in range(mf_ref.shape[0]):
        x = jnp.dot(mf_ref[kk], _unpack_complex_rows(b_ref[kk]), preferred_element_type=F32)
        xr, xi = x[:r], x[r:]
        hr, hi = _unpack(h_ref[kk])
        y = jnp.concatenate([xr * hr - xi * hi, xr * hi + xi * hr], axis=0).astype(BF16)
        c = jnp.dot(mi_ref[kk], y, preferred_element_type=F32)
        o_ref[kk] = _pack_complex(c[:r], c[r:])


def _fft_stage2(b1, h, order, mf, minv, r, n_ch):
    nb = b1.shape[0]
    kb = _k2_block(r)
    blk = pl.BlockSpec((None, kb, r, n_ch), lambda k, b: (b, k, 0, 0))
    return pl.pallas_call(
        functools.partial(_fft2_kernel, r=r),
        grid=(r // kb, nb),
        in_specs=[pl.BlockSpec((kb, 2 * r, 2 * r), lambda k, b: (k, 0, 0)),
                  pl.BlockSpec((kb, 2 * r, 2 * r), lambda k, b: (k, 0, 0)),
                  blk,
                  pl.BlockSpec((None, kb, r, n_ch), lambda k, b: (order, k, 0, 0))],
        out_specs=blk,
        out_shape=jax.ShapeDtypeStruct((nb, r, r, n_ch), U32),
        compiler_params=_cparams(("parallel", "parallel")),
        name="hyena_dft_stage2",
    )(mf, minv, b1, h)


def _fft3_kernel(g_ref, c_ref, gate_ref, s_ref, bias_ref, ng_ref, *rest, final):
    if final:
        (o_ref,) = rest
    else:
        f_ref, o_ref, b_ref = rest
    r, nb, n_ch = c_ref.shape
    cc = _unpack_complex_rows(c_ref[...].reshape(r * nb, n_ch))
    y = jnp.dot(g_ref[...], cc, preferred_element_type=F32)
    rows = (r // 2) * nb
    s_new = gate_ref[...].reshape(rows, n_ch) * (y + s_ref[...].reshape(rows, n_ch) * bias_ref[...])
    if final:
        ms = jnp.mean(s_new * s_new, axis=-1, keepdims=True)
        s_new = s_new * lax.rsqrt(ms + RMS_EPS) * ng_ref[...]
    o_ref[...] = s_new.reshape(r // 2, nb, n_ch)
    if not final:
        res = jnp.dot(f_ref[...], s_new.astype(BF16), preferred_element_type=F32)
        half = r * nb
        b_ref[...] = _pack_complex(res[:half], res[half:]).reshape(r, nb, n_ch)


def _fft_stage3(c2, g3, z4, gate_part, s4, s_part, bias, norm_g, f1, r, n_ch):
    final = f1 is None
    nb = c2.shape[0]
    zv = z4.reshape(z4.shape[0], nb, r // 2, r, n_ch)
    sv = s4.reshape(s4.shape[0], nb, r // 2, r, n_ch)
    dspec = lambda part: pl.BlockSpec((None, None, r // 2, N1_BLOCK, n_ch), lambda b, j: (part, b, 0, j, 0))
    packed = pl.BlockSpec((None, r, N1_BLOCK, n_ch), lambda b, j: (b, 0, j, 0))
    s_spec = pl.BlockSpec((None, r // 2, N1_BLOCK, n_ch), lambda b, j: (b, 0, j, 0))
    s_shape = jax.ShapeDtypeStruct((nb, r // 2, r, n_ch), F32)
    in_specs = [pl.BlockSpec(g3.shape, lambda b, j: (0, 0)), packed, dspec(gate_part), dspec(s_part),
                pl.BlockSpec((1, n_ch), lambda b, j: (0, 0)), pl.BlockSpec((1, n_ch), lambda b, j: (0, 0))]
    args = [g3, c2, zv, sv, bias.reshape(1, n_ch).astype(F32), norm_g.reshape(1, n_ch).astype(F32)]
    if final:
        out_specs, out_shape = s_spec, s_shape
    else:
        in_specs.append(pl.BlockSpec(f1.shape, lambda b, j: (0, 0)))
        args.append(f1)
        out_specs = [s_spec, packed]
        out_shape = [s_shape, jax.ShapeDtypeStruct((nb, r, r, n_ch), U32)]
    return pl.pallas_call(
        functools.partial(_fft3_kernel, final=final),
        grid=(nb, r // N1_BLOCK),
        in_specs=in_specs,
        out_specs=out_specs,
        out_shape=out_shape,
        compiler_params=_cparams(("parallel", "parallel")),
        name="hyena_dft_stage3",
    )(*args)


def _hyena(zh, row_off, batch, seq, conv_w, conv_b, filt, fft_bias, out_g):
    n_ch = zh.shape[1] // 3
    r = int(round(math.sqrt(2 * seq)))
    assert r * r == 2 * seq and r % 16 == 0
    f1, mf, minv, g3 = _dft_tables(r)
    sig = _hyena_filter_signals(seq, *filt, n_ch=n_ch)
    hb1 = _fft_stage1(sig[None], 0, f1, r, n_ch)
    h = _filter_spectrum(hb1, mf, r, n_ch)
    z4 = _shortconv(zh, row_off, conv_w, conv_b, batch, seq)
    s4, s_part = z4, 2
    b1 = _fft_stage1(s4, s_part, f1, r, n_ch)
    for o in range(HYENA_ORDER):
        c2 = _fft_stage2(b1, h, o, mf, minv, r, n_ch)
        if o < HYENA_ORDER - 1:
            s, b1 = _fft_stage3(c2, g3, z4, o, s4, s_part, fft_bias[o], out_g, f1, r, n_ch)
        else:
            s = _fft_stage3(c2, g3, z4, o, s4, s_part, fft_bias[o], out_g, None, r, n_ch)
        s4, s_part = s.reshape(1, batch, seq, n_ch), 0
    return s4.reshape(batch * seq, n_ch)


def _extract_top(s, key, count):
    vals, keys = [], []
    for _ in range(count):
        m = jnp.max(s, axis=0, keepdims=True)
        kmin = jnp.min(jnp.where(s == m, key, jnp.inf), axis=0, keepdims=True)
        s = jnp.where(key == kmin, -jnp.inf, s)
        vals.append(m)
        keys.append(kmin)
    return vals, keys


def _peer_topk_kernel(q_ref, keys_ref, eid_ref, gate_ref):
    t = q_ref.shape[0]
    nk = PEER_NKEYS
    q = q_ref[...]
    row_key = lax.broadcasted_iota(I32, (nk, t), 0).astype(F32)
    tops = []
    for c in range(2):
        s = lax.dot_general(keys_ref[c], q[:, c * nk:(c + 1) * nk], (((1,), (1,)), ((), ())),
                            preferred_element_type=F32)
        tops.append(_extract_top(s, row_key, PEER_TOPK))
    (v1, i1), (v2, i2) = tops
    rows16 = lax.broadcasted_iota(I32, (PEER_TOPK, t), 0)
    v2a = jnp.zeros((PEER_TOPK, t), F32)
    i2a = jnp.zeros((PEER_TOPK, t), F32)
    for j in range(PEER_TOPK):
        v2a = jnp.where(rows16 == j, v2[j], v2a)
        i2a = jnp.where(rows16 == j, i2[j], i2a)
    n_exp = float(nk * nk)
    half = PEER_TOPK // 2
    rows8 = lax.broadcasted_iota(I32, (half, t), 0)
    pos8 = rows8.astype(F32)
    v2h, i2h = v2a[:half], i2a[:half]
    cand = [v1[0] + v2a]
    ckey = [rows16.astype(F32) * n_exp + (i1[0] * float(nk) + i2a)]
    for i in range(1, half):
        cand.append(jnp.where(rows8 < PEER_TOPK // (i + 1), v1[i] + v2h, -jnp.inf))
        ckey.append((pos8 + float(i * PEER_TOPK)) * n_exp + (i1[i] * float(nk) + i2h))
    v1t = jnp.zeros((half, t), F32)
    i1t = jnp.zeros((half, t), F32)
    for r in range(half):
        v1t = jnp.where(rows8 == r, v1[half + r], v1t)
        i1t = jnp.where(rows8 == r, i1[half + r], i1t)
    cand.append(v1t + v2[0])
    ckey.append((pos8 + float(half)) * (PEER_TOPK * n_exp) + (i1t * float(nk) + i2[0]))
    tv, tk_ = _extract_top(jnp.concatenate(cand, axis=0), jnp.concatenate(ckey, axis=0), PEER_TOPK)
    denom = jnp.zeros((1, t), F32)
    es = []
    for k in range(PEER_TOPK):
        e = jnp.exp(tv[k] - tv[0])
        es.append(e)
        denom = denom + e
    eid = jnp.zeros((PEER_TOPK, t), F32)
    gate = jnp.zeros((PEER_TOPK, t), F32)
    for k in range(PEER_TOPK):
        pos = jnp.floor(tk_[k] * (1.0 / n_exp))
        eid = jnp.where(rows16 == k, tk_[k] - pos * n_exp, eid)
        gate = jnp.where(rows16 == k, es[k] / denom, gate)
    eid_ref[...] = eid.astype(I32)
    gate_ref[...] = gate


def _peer_topk(q, keys, tm=512):
    t = q.shape[0]
    tm = min(tm, t)
    out_spec = pl.BlockSpec((PEER_TOPK, tm), lambda i, h: (h, i))
    return pl.pallas_call(
        _peer_topk_kernel,
        grid=(t // tm, PEER_HEADS),
        in_specs=[pl.BlockSpec((tm, 2 * PEER_NKEYS), lambda i, h: (i, h)),
                  pl.BlockSpec((None, 2, PEER_NKEYS, PEER_NKEYS), lambda i, h: (h, 0, 0, 0))],
        out_specs=[out_spec, out_spec],
        out_shape=[jax.ShapeDtypeStruct((PEER_HEADS * PEER_TOPK, t), I32),
                   jax.ShapeDtypeStruct((PEER_HEADS * PEER_TOPK, t), F32)],
        compiler_params=_cparams(("parallel", "parallel")),
        name="peer_topk",
    )(q, keys)


def _pack_table(tab):
    e, d = tab.shape
    assert d == 2 * SUBLANES * LANES
    bits = lax.bitcast_convert_type(tab.astype(BF16), jnp.uint16).astype(U32)
    packed = bits[:, :d // 2] | (bits[:, d // 2:] << 16)
    return packed.reshape(e * SUBLANES, LANES)


def _unpack(w):
    lo = lax.bitcast_convert_type(w << 16, F32)
    hi = lax.bitcast_convert_type(w & jnp.uint32(0xFFFF0000), F32)
    return lo, hi


_BITREV8 = (0, 4, 2, 6, 1, 5, 3, 7)


def _sublane_fold8(parts):
    sub = lax.broadcasted_iota(I32, (2 * SUBLANES, LANES), 0) // 2

    def rolled(a, shift):
        return pltpu.bitcast(pltpu.roll(pltpu.bitcast(a, U32), shift, axis=0), BF16)

    lvl = [parts[_BITREV8[r]] for r in range(8)]
    for shift, mask in ((4, sub < 4), (2, (sub % 4) < 2), (1, (sub % 2) < 1)):
        nxt = []
        for a, b in zip(lvl[0::2], lvl[1::2]):
            nxt.append(jnp.where(mask, a + rolled(a, SUBLANES - shift), b + rolled(b, shift)))
        lvl = nxt
    return lvl[0]


def _pair_ranges(half):
    if half == 0:
        return (0, PEER_WINDOW), (PEER_WINDOW, PEER_PAIRS)
    return (PEER_PAIRS - PEER_WINDOW, PEER_PAIRS), (0, PEER_PAIRS - PEER_WINDOW)


def _overflow(n_low, half):
    return n_low > PEER_WINDOW if half == 0 else n_low < PEER_PAIRS - PEER_WINDOW


def _table_rows(tab_ref, off):
    return _unpack(tab_ref[pl.ds(pl.multiple_of(off, SUBLANES), SUBLANES), :])


def _peer_u_kernel(off_ref, nlow_ref, tab_ref, x_ref, o_ref, extra_ref, *, tm, half):
    lane = lax.broadcasted_iota(I32, (SUBLANES, LANES), 1)
    sub = lax.broadcasted_iota(I32, (SUBLANES, LANES), 0)
    lane_grp = lax.shift_right_logical(lane, 3)
    diag = sub == (lane & (SUBLANES - 1))
    main, rest = _pair_ranges(half)

    def folded(t, p0, p1):
        xb = pltpu.bitcast(x_ref[t], BF16)
        out = []
        for g in range(p0 // SUBLANES, p1 // SUBLANES):
            parts = []
            for r in range(SUBLANES):
                off = pl.multiple_of(off_ref[t, g * SUBLANES + r], SUBLANES)
                parts.append(pltpu.bitcast(tab_ref[pl.ds(off, SUBLANES), :], BF16) * xb)
            out.append(pltpu.bitcast(_sublane_fold8(parts), U32))
        return out

    def lane_sums(folds, p0):
        mat = jnp.zeros((SUBLANES, LANES), F32)
        for i, f in enumerate(folds):
            lo, hi = _unpack(f)
            mat = jnp.where(lane_grp == p0 // SUBLANES + i, jnp.sum(lo + hi, axis=-1, keepdims=True), mat)
        return jnp.sum(jnp.where(diag, mat, 0.0), axis=0, keepdims=True)

    def finish(t, folds):
        o_ref[pl.ds(t, 1), :] = lane_sums(folds, main[0]) + extra_ref[pl.ds(t, 1), :]

    def token(t, prev):
        cur = folded(t, *main)
        finish(jnp.maximum(t - 1, 0), prev)
        extra_ref[pl.ds(t, 1), :] = jnp.zeros((1, LANES), F32)

        @pl.when(_overflow(nlow_ref[0, t], half))
        def _():
            extra_ref[pl.ds(t, 1), :] = lane_sums(folded(t, *rest), rest[0])

        return tuple(cur)

    extra_ref[pl.ds(0, 1), :] = jnp.zeros((1, LANES), F32)
    zeros = tuple(jnp.zeros((SUBLANES, LANES), U32) for _ in range((main[1] - main[0]) // SUBLANES))
    last = lax.fori_loop(0, tm, token, zeros)
    finish(tm - 1, last)


def _peer_u(off, nlow3, tab, x4, half, tm=128):
    t = off.shape[0]
    rows = PEER_HALF * SUBLANES
    return pl.pallas_call(
        functools.partial(_peer_u_kernel, tm=tm, half=half),
        grid=(t // tm,),
        in_specs=[pl.BlockSpec((tm, LANES), lambda i: (i, 0), memory_space=pltpu.SMEM),
                  pl.BlockSpec((None, 1, tm), lambda i: (i, 0, 0), memory_space=pltpu.SMEM),
                  pl.BlockSpec((rows, LANES), lambda i: (half, 0), pipeline_mode=pl.Buffered(1)),
                  pl.BlockSpec((tm, SUBLANES, LANES), lambda i: (i, 0, 0))],
        out_specs=pl.BlockSpec((tm, LANES), lambda i: (i, 0)),
        out_shape=jax.ShapeDtypeStruct((t, LANES), F32),
        scratch_shapes=[pltpu.VMEM((tm, LANES), F32)],
        compiler_params=_cparams(("arbitrary",)),
        name="peer_expert_scores",
    )(off, nlow3, tab, x4)


def _peer_coef_kernel(a0_ref, a1_ref, eid_ref, gate_ref, c0_ref, c1_ref):
    low = eid_ref[...] < PEER_HALF
    a = jnp.where(low, a0_ref[...], a1_ref[...])
    coef = gate_ref[...] * (0.5 * a * (1.0 + lax.erf(a * (1.0 / math.sqrt(2.0)))))
    c0_ref[...] = jnp.where(low, coef, 0.0)
    c1_ref[...] = jnp.where(low, 0.0, coef)


def _peer_coef(a0, a1, eid, gate, tm=1024):
    t = eid.shape[0]
    tm = min(tm, t)
    spec = pl.BlockSpec((tm, LANES), lambda i: (i, 0))
    return pl.pallas_call(
        _peer_coef_kernel,
        grid=(t // tm,),
        in_specs=[spec, spec, spec, spec],
        out_specs=[spec, spec],
        out_shape=[jax.ShapeDtypeStruct((t, LANES), F32)] * 2,
        compiler_params=_cparams(("parallel",)),
        name="peer_coef",
    )(a0, a1, eid, gate)


def _peer_v_kernel(off_ref, nlow_ref, coef_ref, tab_ref, base_ref, *o_refs, tm, half, n_first):
    n_acc = 4
    main, rest = _pair_ranges(half)

    def weighted(t, p0, p1):
        acc_lo = [jnp.zeros((SUBLANES, LANES), F32) for _ in range(n_acc)]
        acc_hi = [jnp.zeros((SUBLANES, LANES), F32) for _ in range(n_acc)]
        for p in range(p0, p1):
            c = coef_ref[t, p]
            lo, hi = _table_rows(tab_ref, off_ref[t, p])
            acc_lo[p % n_acc] = acc_lo[p % n_acc] + c * lo
            acc_hi[p % n_acc] = acc_hi[p % n_acc] + c * hi
        return ((acc_lo[0] + acc_lo[1]) + (acc_lo[2] + acc_lo[3]),
                (acc_hi[0] + acc_hi[1]) + (acc_hi[2] + acc_hi[3]))

    def run(o_ref):
        def store_row(t, lo, hi):
            o_ref[t, 0] = base_ref[t, 0] + lo
            o_ref[t, 1] = base_ref[t, 1] + hi

        def token(t, carry):
            lo, hi = weighted(t, *main)
            store_row(t, lo, hi)

            @pl.when(_overflow(nlow_ref[0, t], half))
            def _():
                lo2, hi2 = weighted(t, *rest)
                store_row(t, lo + lo2, hi + hi2)

            return carry

        lax.fori_loop(0, tm, token, 0)

    if len(o_refs) == 1:
        run(o_refs[0])
    else:
        first = pl.program_id(0) < n_first
        pl.when(first)(lambda: run(o_refs[0]))
        pl.when(jnp.logical_not(first))(lambda: run(o_refs[1]))


def _peer_v(off, nlow3, coef, tab, base, half, tm=128, split=None):
    t = base.shape[0]
    rows = PEER_HALF * SUBLANES
    smem = pl.BlockSpec((tm, LANES), lambda i: (i, 0), memory_space=pltpu.SMEM)
    tile = pl.BlockSpec((tm, 2, SUBLANES, LANES), lambda i: (i, 0, 0, 0))
    if split is None:
        n_first, out_specs = 0, tile
        out_shape = jax.ShapeDtypeStruct((t, 2, SUBLANES, LANES), F32)
    else:
        assert split % tm == 0
        n_first, n_rest = split // tm, (t - split) // tm
        blk = (tm, 2, SUBLANES, LANES)
        out_specs = [pl.BlockSpec(blk, lambda i: (jnp.clip(i, 0, n_first - 1), 0, 0, 0)),
                     pl.BlockSpec(blk, lambda i: (jnp.clip(i - n_first, 0, n_rest - 1), 0, 0, 0))]
        out_shape = [jax.ShapeDtypeStruct((split, 2, SUBLANES, LANES), F32),
                     jax.ShapeDtypeStruct((t - split, 2, SUBLANES, LANES), F32)]
    return pl.pallas_call(
        functools.partial(_peer_v_kernel, tm=tm, half=half, n_first=n_first),
        grid=(t // tm,),
        in_specs=[smem,
                  pl.BlockSpec((None, 1, tm), lambda i: (i, 0, 0), memory_space=pltpu.SMEM),
                  smem,
                  pl.BlockSpec((rows, LANES), lambda i: (half, 0), pipeline_mode=pl.Buffered(1)),
                  tile],
        out_specs=out_specs,
        out_shape=out_shape,
        compiler_params=_cparams(("arbitrary",)),
        name="peer_expert_sum",
    )(off, nlow3, coef, tab, base)


def _peer(x1, split, norm2_g, wq_bf, keys_bf, u_packed, v_packed, tm=128):
    t, d = x1.shape
    tm = min(tm, t)
    (xn_bf,) = _rmsnorm(x1, norm2_g, (BF16,))
    q = _matmul(xn_bf, wq_bf, wq_bf.shape[1], 0, BF16)
    eid_t, gate_t = _peer_topk(q, keys_bf)
    eid, gate = eid_t.T, gate_t.T
    n_exp = PEER_NKEYS * PEER_NKEYS
    high = (eid >= PEER_HALF).astype(I32)
    nlow3 = (PEER_PAIRS - jnp.sum(high, axis=-1, dtype=I32)).reshape(t // tm, 1, tm)
    slot = lax.broadcasted_iota(I32, eid.shape, 1)
    order = jnp.sort((high * PEER_PAIRS + slot) * n_exp + eid, axis=-1)
    eid = order % n_exp
    gate = jnp.take_along_axis(gate, (order // n_exp) % PEER_PAIRS, axis=-1)
    off = (eid & (PEER_HALF - 1)) * SUBLANES
    xw = _pack_table(xn_bf).reshape(t, SUBLANES, LANES)
    a0 = _peer_u(off, nlow3, u_packed, xw, 0, tm)
    a1 = _peer_u(off, nlow3, u_packed, xw, 1, tm)
    c0, c1 = _peer_coef(a0, a1, eid, gate)
    y = _peer_v(off, nlow3, c0, v_packed, x1.reshape(t, 2, SUBLANES, LANES), 0, tm)
    ya, yb = _peer_v(off, nlow3, c1, v_packed, y, 1, tm, split=split)
    return ya.reshape(split, d), yb.reshape(t - split, d)


def kernel(x_prompt, x_sample, norm1_g, w_in, q_norm_g, k_norm_g, lambda_q1, lambda_k1, lambda_q2, lambda_k2, attn_sub_g, conv_w, conv_b, filt_w1, filt_b1, filt_w2, filt_b2, filt_w3, filt_b3, filt_w4, filt_freq, fft_bias, hyena_out_g, w_out, norm2_g, peer_wq, peer_keys, peer_u, peer_v):
    depth = w_in.shape[0]
    d_model = x_prompt.shape[-1]
    att_w = ATT_HEADS * ATT_VDIM
    shapes = [x_prompt.shape[:2], x_sample.shape[:2]]
    xs = [x_prompt.reshape(-1, d_model), x_sample.reshape(-1, d_model)]
    n0 = xs[0].shape[0]
    slopes = 2.0 ** (-8.0 * jnp.arange(1, ATT_HEADS + 1, dtype=F32) / ATT_HEADS)

    for l in range(depth):
        lambda_init = 0.8 - 0.6 * math.exp(-0.3 * l)
        lam = (jnp.exp(jnp.sum(lambda_q1[l].astype(F32) * lambda_k1[l].astype(F32)))
               - jnp.exp(jnp.sum(lambda_q2[l].astype(F32) * lambda_k2[l].astype(F32)))
               + lambda_init).reshape(1)
        w_in_bf = w_in[l].astype(BF16)
        q_gain = jnp.tile(q_norm_g[l].astype(F32), 2 * ATT_HEADS) * (ATT_QKDIM ** -0.5 * LOG2E)
        k_gain = jnp.tile(k_norm_g[l].astype(F32), 2 * ATT_HEADS)
        qk_gain = jnp.concatenate([q_gain, k_gain]).reshape(1, 2 * att_w)

        h_bf = _rmsnorm_stacked(xs[0], xs[1], norm1_g[l], BF16)
        qk = _matmul(h_bf, w_in_bf, 2 * att_w, 0, BF16, mode="qknorm", extra=qk_gain)
        vt = _matmul_nt(w_in_bf[:, 2 * att_w:3 * att_w].T, h_bf, BF16)
        zh = _matmul(h_bf, w_in_bf, w_in.shape[2] - 3 * att_w, 3 * att_w, F32)

        filt = (filt_w1[l], filt_b1[l], filt_w2[l], filt_b2[l], filt_w3[l], filt_b3[l], filt_w4[l], filt_freq[l])
        segs, row = [], 0
        for (b, s), x_seg in zip(shapes, xs):
            att = _attention(qk, vt, row, slopes, lam, attn_sub_g[l], b, s, 1.0 - lambda_init)
            hy = _hyena(zh, row, b, s, conv_w[l], conv_b[l], filt, fft_bias[l], hyena_out_g[l])
            segs.append((att, hy, x_seg))
            row += b * s
        x1 = _outproj_stacked(segs[0], segs[1], w_out[l].astype(BF16))

        xs = _peer(x1, n0, norm2_g[l], peer_wq[l].astype(BF16), peer_keys[l].astype(BF16),
                   _pack_table(peer_u[l]), _pack_table(peer_v[l]))

    return (xs[0].reshape(x_prompt.shape), xs[1].reshape(x_sample.shape))
```

```python
import functools
import math

import jax
import jax.numpy as jnp
from jax import lax
from jax.experimental import pallas as pl
from jax.experimental.pallas import tpu as pltpu

F32 = jnp.float32
BF16 = jnp.bfloat16
I32 = jnp.int32
U32 = jnp.uint32

RMS_EPS = 1e-6
LOG2E = 1.4426950408889634
LANES = 128
SUBLANES = 8
VMEM_LIMIT_BYTES = 56 * 1024 * 1024

ATT_HEADS = 8
ATT_VDIM = 128
ATT_QKDIM = 64
HYENA_ORDER = 2
FILTER_BANDS = 16
DECAY_FAST = 0.3
DECAY_SLOW = 1.5
DECAY_TARGET = 1e-2
PEER_HEADS = 8
PEER_NKEYS = 128
PEER_TOPK = 16
PEER_HALF = PEER_NKEYS * PEER_NKEYS // 2
PEER_PAIRS = PEER_HEADS * PEER_TOPK
PEER_WINDOW = 80


def _cparams(sem, vmem=VMEM_LIMIT_BYTES):
    return pltpu.CompilerParams(dimension_semantics=sem, vmem_limit_bytes=vmem)


def _segment_spec(block, start, count, minor=0):
    return pl.BlockSpec(block, lambda i: (jnp.clip(i - start, 0, count - 1), minor))


def _rmsnorm_kernel(x_ref, g_ref, *o_refs):
    x = x_ref[...]
    ms = jnp.mean(x * x, axis=-1, keepdims=True)
    y = x * lax.rsqrt(ms + RMS_EPS) * g_ref[...]
    for o_ref in o_refs:
        o_ref[...] = y.astype(o_ref.dtype)


def _rmsnorm2_kernel(xa_ref, xb_ref, g_ref, o_ref, *, na):
    first = pl.program_id(0) < na
    x = jnp.where(first, xa_ref[...], xb_ref[...])
    ms = jnp.mean(x * x, axis=-1, keepdims=True)
    o_ref[...] = (x * lax.rsqrt(ms + RMS_EPS) * g_ref[...]).astype(o_ref.dtype)


def _rmsnorm_stacked(xa, xb, g, out_dtype, tm=512):
    d = xa.shape[1]
    na, nb = xa.shape[0] // tm, xb.shape[0] // tm
    assert xa.shape[0] % tm == 0 and xb.shape[0] % tm == 0
    return pl.pallas_call(
        functools.partial(_rmsnorm2_kernel, na=na),
        grid=(na + nb,),
        in_specs=[_segment_spec((tm, d), 0, na), _segment_spec((tm, d), na, nb),
                  pl.BlockSpec((1, d), lambda i: (0, 0))],
        out_specs=pl.BlockSpec((tm, d), lambda i: (i, 0)),
        out_shape=jax.ShapeDtypeStruct((xa.shape[0] + xb.shape[0], d), out_dtype),
        compiler_params=_cparams(("parallel",)),
        name="rmsnorm_stacked",
    )(xa, xb, g.reshape(1, d).astype(F32))


def _rmsnorm(x, g, out_dtypes, tm=512):
    t, d = x.shape
    tm = min(tm, t)
    spec = pl.BlockSpec((tm, d), lambda i: (i, 0))
    return pl.pallas_call(
        _rmsnorm_kernel,
        grid=(t // tm,),
        in_specs=[spec, pl.BlockSpec((1, d), lambda i: (0, 0))],
        out_specs=[spec for _ in out_dtypes],
        out_shape=[jax.ShapeDtypeStruct((t, d), dt) for dt in out_dtypes],
        compiler_params=_cparams(("parallel",)),
        name="rmsnorm",
    )(x, g.reshape(1, d).astype(F32))


def _group_rms_scale(x, gain):
    lane = lax.broadcasted_iota(I32, x.shape, 1)
    lo = lane < ATT_QKDIM
    x2 = x * x
    s_lo = jnp.sum(jnp.where(lo, x2, 0.0), axis=-1, keepdims=True)
    s_hi = jnp.sum(jnp.where(lo, 0.0, x2), axis=-1, keepdims=True)
    ms = jnp.where(lo, s_lo, s_hi) * (1.0 / ATT_QKDIM)
    return x * lax.rsqrt(ms + RMS_EPS) * gain


def _mm_kernel(a_ref, b_ref, *rest, mode):
    acc = jnp.dot(a_ref[...], b_ref[...], preferred_element_type=F32)
    if mode == "plain":
        (o_ref,) = rest
        o_ref[...] = acc.astype(o_ref.dtype)
    elif mode == "residual":
        r_ref, o_ref = rest
        o_ref[...] = (acc + r_ref[...]).astype(o_ref.dtype)
    elif mode == "qknorm":
        g_ref, o_ref = rest
        for c in range(acc.shape[1] // LANES):
            sl = slice(c * LANES, (c + 1) * LANES)
            o_ref[:, sl] = _group_rms_scale(acc[:, sl], g_ref[:, sl]).astype(o_ref.dtype)
    else:
        raise ValueError(mode)


def _matmul(a, b, n_cols, col_off, out_dtype, mode="plain", extra=None, tm=1024, tn=1024):
    m, k = a.shape
    tm = min(tm, m)
    tn = min(tn, n_cols)
    assert col_off % tn == 0 and n_cols % tn == 0 and m % tm == 0
    off = col_off // tn
    in_specs = [pl.BlockSpec((tm, k), lambda i, j: (i, 0)),
                pl.BlockSpec((k, tn), lambda i, j: (0, j + off))]
    args = [a, b]
    if mode == "residual":
        in_specs.append(pl.BlockSpec((tm, tn), lambda i, j: (i, j)))
        args.append(extra)
    elif mode == "qknorm":
        in_specs.append(pl.BlockSpec((1, tn), lambda i, j: (0, j)))
        args.append(extra)
    return pl.pallas_call(
        functools.partial(_mm_kernel, mode=mode),
        grid=(m // tm, n_cols // tn),
        in_specs=in_specs,
        out_specs=pl.BlockSpec((tm, tn), lambda i, j: (i, j)),
        out_shape=jax.ShapeDtypeStruct((m, n_cols), out_dtype),
        compiler_params=_cparams(("parallel", "arbitrary")),
        name="matmul_" + mode,
    )(*args)


def _outproj_kernel(att_a, hy_a, x_a, att_b, hy_b, x_b, w_ref, o_ref, *, na):
    kw = att_a.shape[1]

    def run(att_ref, hy_ref, x_ref):
        acc = jnp.dot(att_ref[...], w_ref[:kw, :], preferred_element_type=F32)
        acc = acc + jnp.dot(hy_ref[...].astype(BF16), w_ref[kw:, :], preferred_element_type=F32)
        o_ref[...] = acc + x_ref[...]

    first = pl.program_id(0) < na
    pl.when(first)(lambda: run(att_a, hy_a, x_a))
    pl.when(jnp.logical_not(first))(lambda: run(att_b, hy_b, x_b))


def _outproj_stacked(seg_a, seg_b, w, tm=512, tn=1024):
    n_out = w.shape[1]
    na, nb = seg_a[0].shape[0] // tm, seg_b[0].shape[0] // tm
    assert seg_a[0].shape[0] % tm == 0 and seg_b[0].shape[0] % tm == 0 and n_out % tn == 0

    def specs(seg, start, count):
        att, hy, x = seg
        row = lambda i, j: (jnp.clip(i - start, 0, count - 1), 0)
        return [pl.BlockSpec((tm, att.shape[1]), row), pl.BlockSpec((tm, hy.shape[1]), row),
                pl.BlockSpec((tm, tn), lambda i, j: (jnp.clip(i - start, 0, count - 1), j))]

    return pl.pallas_call(
        functools.partial(_outproj_kernel, na=na),
        grid=(na + nb, n_out // tn),
        in_specs=specs(seg_a, 0, na) + specs(seg_b, na, nb) + [pl.BlockSpec((w.shape[0], tn), lambda i, j: (0, j))],
        out_specs=pl.BlockSpec((tm, tn), lambda i, j: (i, j)),
        out_shape=jax.ShapeDtypeStruct(((na + nb) * tm, n_out), F32),
        compiler_params=_cparams(("parallel", "arbitrary")),
        name="out_projection",
    )(*seg_a, *seg_b, w)


def _mm_nt_kernel(w_ref, a_ref, o_ref):
    o_ref[...] = lax.dot_general(w_ref[...], a_ref[...], (((1,), (1,)), ((), ())),
                                 preferred_element_type=F32).astype(o_ref.dtype)


def _matmul_nt(wt, a, out_dtype, tm=1024):
    n, k = wt.shape
    m = a.shape[0]
    tm = min(tm, m)
    return pl.pallas_call(
        _mm_nt_kernel,
        grid=(m // tm,),
        in_specs=[pl.BlockSpec((n, k), lambda i: (0, 0)), pl.BlockSpec((tm, k), lambda i: (i, 0))],
        out_specs=pl.BlockSpec((n, tm), lambda i: (0, i)),
        out_shape=jax.ShapeDtypeStruct((n, m), out_dtype),
        compiler_params=_cparams(("parallel",)),
        name="matmul_nt",
    )(wt, a)


def _alibi_columns(slopes, tq, tk):
    return [_alibi_side(slopes, tq, True), _alibi_side(slopes, tk, False)]


def _alibi_side(slopes, n, query_side):
    pos = jnp.arange(n, dtype=F32)
    val = (slopes.astype(F32) * LOG2E)[:, None] * pos[None, :]

    def pieces(x):
        p1 = x.astype(BF16)
        r1 = x - p1.astype(F32)
        p2 = r1.astype(BF16)
        p3 = (r1 - p2.astype(F32)).astype(BF16)
        return [p1, p2, p3]

    ones = [jnp.ones_like(val, BF16)] * 3
    six = jnp.stack(pieces(-val) + ones if query_side else ones + pieces(val), axis=-1)
    pad = jnp.zeros(val.shape + (ATT_QKDIM - 6,), BF16)
    return jnp.concatenate([six, pad, six, pad], axis=-1)


def _attn_kernel(slope_ref, lam_ref, q_ref, k_ref, vt_ref, aq_ref, ak_ref, g_ref, o_ref,
                 m_ref, l_ref, acc_ref, *, tq, tk, nk, hb, out_scale):
    hg = pl.program_id(1)
    i = pl.program_id(2)
    j = pl.program_id(3)

    @pl.when(j == 0)
    def _():
        m_ref[...] = jnp.full(m_ref.shape, -jnp.inf, F32)
        l_ref[...] = jnp.zeros(l_ref.shape, F32)
        acc_ref[...] = jnp.zeros(acc_ref.shape, F32)

    q_first = lax.broadcasted_iota(I32, (tq, LANES), 1) < ATT_QKDIM
    k_first = lax.broadcasted_iota(I32, (tk, LANES), 1) < ATT_QKDIM
    nt = (((1,), (1,)), ((), ()))

    def update(hh, scores, shift):
        vt = vt_ref[hh * LANES:(hh + 1) * LANES, :]
        for c, s in enumerate(scores):
            m_old = m_ref[hh, c]
            m_new = jnp.maximum(m_old, jnp.max(s, axis=0, keepdims=True) + shift)
            alpha = jnp.exp2(m_old - m_new)
            p = jnp.exp2(s - (m_new - shift))
            l_ref[hh, c] = alpha * l_ref[hh, c] + jnp.sum(p, axis=0, keepdims=True)
            acc_ref[hh, c] = alpha * acc_ref[hh, c] + jnp.dot(vt, p.astype(BF16),
                                                              preferred_element_type=F32)
            m_ref[hh, c] = m_new

    keys_before = i * tq >= (j + 1) * tk
    keys_after = (i + 1) * tq <= j * tk
    off_diagonal = jnp.logical_or(keys_before, keys_after)

    @pl.when(off_diagonal)
    def _():
        sign = jnp.where(keys_after, -1.0, 1.0).astype(BF16)
        gap = jnp.abs(i * tq - j * tk).astype(F32)
        for hh in range(hb):
            sl = slice(hh * LANES, (hh + 1) * LANES)
            q = q_ref[:, sl]
            k = k_ref[:, sl]
            aq = aq_ref[hh]
            ak = ak_ref[hh] * sign
            s0 = lax.dot_general(jnp.where(k_first, k, ak), jnp.where(q_first, q, aq), nt,
                                 preferred_element_type=F32)
            s1 = lax.dot_general(jnp.where(k_first, ak, k), jnp.where(q_first, aq, q), nt,
                                 preferred_element_type=F32)
            update(hh, (s0, s1), -(slope_ref[hg * hb + hh] * LOG2E) * gap)

    @pl.when(jnp.logical_not(off_diagonal))
    def _():
        kpos = lax.broadcasted_iota(I32, (tk, tq), 0) + j * tk
        qpos = lax.broadcasted_iota(I32, (tk, tq), 1) + i * tq
        dist = jnp.abs(kpos - qpos).astype(F32)
        for hh in range(hb):
            sl = slice(hh * LANES, (hh + 1) * LANES)
            q = q_ref[:, sl]
            k = k_ref[:, sl]
            zero = jnp.zeros_like(q)
            bias = dist * (-(slope_ref[hg * hb + hh] * LOG2E))
            s0 = lax.dot_general(k, jnp.where(q_first, q, zero), nt, preferred_element_type=F32) + bias
            s1 = lax.dot_general(k, jnp.where(q_first, zero, q), nt, preferred_element_type=F32) + bias
            update(hh, (s0, s1), 0.0)

    @pl.when(j == nk - 1)
    def _():
        for hh in range(hb):
            o = acc_ref[hh, 0] / l_ref[hh, 0] - lam_ref[0] * (acc_ref[hh, 1] / l_ref[hh, 1])
            ms = jnp.mean(o * o, axis=0, keepdims=True)
            y = o * lax.rsqrt(ms + RMS_EPS) * (g_ref[...] * out_scale)
            o_ref[:, hh * LANES:(hh + 1) * LANES] = y.T.astype(o_ref.dtype)


def _attention(qk, vt, row_off, slopes, lam, sub_g, batch, seq, out_scale, tq=512, tk=1024, hb=4):
    tq = min(tq, seq)
    tk = min(tk, seq // 4)
    nq, nk = seq // tq, seq // tk
    assert row_off % tq == 0 and row_off % tk == 0 and ATT_HEADS % hb == 0
    oq, ok = row_off // tq, row_off // tk
    ng = ATT_HEADS // hb
    aq, ak = _alibi_columns(slopes, tq, tk)
    kern = functools.partial(_attn_kernel, tq=tq, tk=tk, nk=nk, hb=hb, out_scale=out_scale)
    smem = pl.BlockSpec(memory_space=pltpu.SMEM)
    return pl.pallas_call(
        kern,
        grid=(batch, ng, nq, nk),
        in_specs=[smem, smem,
                  pl.BlockSpec((tq, hb * LANES), lambda b, h, i, j: (oq + b * nq + i, h)),
                  pl.BlockSpec((tk, hb * LANES), lambda b, h, i, j: (ok + b * nk + j, ng + h)),
                  pl.BlockSpec((hb * LANES, tk), lambda b, h, i, j: (h, ok + b * nk + j)),
                  pl.BlockSpec((hb, tq, LANES), lambda b, h, i, j: (h, 0, 0)),
                  pl.BlockSpec((hb, tk, LANES), lambda b, h, i, j: (h, 0, 0)),
                  pl.BlockSpec((LANES, 1), lambda b, h, i, j: (0, 0))],
        out_specs=pl.BlockSpec((tq, hb * LANES), lambda b, h, i, j: (b * nq + i, h)),
        out_shape=jax.ShapeDtypeStruct((batch * seq, ATT_HEADS * ATT_VDIM), BF16),
        scratch_shapes=[pltpu.VMEM((hb, 2, 1, tq), F32), pltpu.VMEM((hb, 2, 1, tq), F32),
                        pltpu.VMEM((hb, 2, LANES, tq), F32)],
        compiler_params=_cparams(("parallel", "parallel", "parallel", "arbitrary")),
        name="diff_attention",
    )(slopes, lam, qk, qk, vt, aq, ak, sub_g.reshape(LANES, 1).astype(F32))


def _shortconv_kernel(z_ref, w_ref, b_ref, o_ref):
    z = z_ref[...]
    n = z.shape[0]
    row = lax.broadcasted_iota(I32, z.shape, 0)
    prev = jnp.where(row == 0, 0.0, pltpu.roll(z, 1, axis=0))
    nxt = jnp.where(row == n - 1, 0.0, pltpu.roll(z, n - 1, axis=0))
    w = w_ref[...]
    o_ref[...] = prev * w[0:1] + z * w[1:2] + nxt * w[2:3] + b_ref[...]


def _shortconv(zh, row_off, conv_w, conv_b, batch, seq, cb=256):
    c3 = zh.shape[1]
    c = c3 // 3
    ncb = c // cb
    assert row_off % seq == 0
    ob = row_off // seq
    out = pl.pallas_call(
        _shortconv_kernel,
        grid=(batch, 3, ncb),
        in_specs=[pl.BlockSpec((seq, cb), lambda b, p, j: (ob + b, p * ncb + j)),
                  pl.BlockSpec((3, cb), lambda b, p, j: (0, p * ncb + j)),
                  pl.BlockSpec((1, cb), lambda b, p, j: (0, p * ncb + j))],
        out_specs=pl.BlockSpec((None, None, seq, cb), lambda b, p, j: (p, b, 0, j)),
        out_shape=jax.ShapeDtypeStruct((3, batch, seq, c), F32),
        compiler_params=_cparams(("parallel", "parallel", "parallel")),
        name="hyena_shortconv",
    )(zh, conv_w.astype(F32), conv_b.reshape(1, c3).astype(F32))
    return out


def _filter_kernel(z_ref, w1_ref, b1_ref, w2_ref, b2_ref, w3_ref, b3_ref, fr_ref, w4_ref,
                   t_ref, d_ref, o_ref, h_ref, *, tl):
    i = pl.program_id(0)
    g = pl.program_id(1)
    hi = lax.Precision.HIGHEST

    @pl.when(g == 0)
    def _():
        fr = fr_ref[...]
        h = jnp.sin(fr * (jnp.dot(z_ref[...], w1_ref[...], precision=hi, preferred_element_type=F32) + b1_ref[...]))
        h = jnp.sin(fr * (jnp.dot(h, w2_ref[...], precision=hi, preferred_element_type=F32) + b2_ref[...]))
        h_ref[...] = jnp.sin(fr * (jnp.dot(h, w3_ref[...], precision=hi, preferred_element_type=F32) + b3_ref[...]))

    f = jnp.dot(h_ref[...].astype(BF16), w4_ref[...], preferred_element_type=F32)
    f = f * jnp.exp(-t_ref[...] * d_ref[...])
    row = lax.broadcasted_iota(I32, f.shape, 0) + i * tl
    drop = jnp.logical_and(row == 0, g % 2 == 1)
    o_ref[...] = jnp.where(drop, 0.0, f)


def _hyena_filter_signals(seq, w1, b1, w2, b2, w3, b3, w4, freq, n_ch, tl=512):
    t = jnp.linspace(0.0, 1.0, seq, dtype=F32)[:, None]
    w = 2.0 * math.pi * jnp.arange(seq, dtype=F32)[:, None] / seq
    f = jnp.linspace(1e-4, FILTER_BANDS - 1, FILTER_BANDS, dtype=F32)[None, :]
    z = jnp.concatenate([t, jnp.cos(f * w), -jnp.sin(f * w)], axis=-1)
    deltas = jnp.abs(jnp.linspace(math.log(DECAY_FAST) / DECAY_TARGET,
                                  math.log(DECAY_SLOW) / DECAY_TARGET, n_ch, dtype=F32))[None, :]
    hid = w1.shape[1]
    emb = LANES
    z = jnp.pad(z, ((0, 0), (0, emb - z.shape[1])))
    w1 = jnp.pad(w1.astype(F32), ((0, emb - w1.shape[0]), (0, 0)))
    tl = min(tl, seq)
    full = lambda shape: pl.BlockSpec(shape, lambda i, g: tuple(0 for _ in shape))
    return pl.pallas_call(
        functools.partial(_filter_kernel, tl=tl),
        grid=(seq // tl, 2 * HYENA_ORDER),
        in_specs=[pl.BlockSpec((tl, emb), lambda i, g: (i, 0)),
                  full((emb, hid)), full((1, hid)), full((hid, hid)), full((1, hid)),
                  full((hid, hid)), full((1, hid)), full((1, hid)),
                  pl.BlockSpec((hid, n_ch), lambda i, g: (0, g)),
                  pl.BlockSpec((tl, 1), lambda i, g: (i, 0)),
                  full((1, n_ch))],
        out_specs=pl.BlockSpec((None, tl, n_ch), lambda i, g: (g, i, 0)),
        out_shape=jax.ShapeDtypeStruct((2 * HYENA_ORDER, seq, n_ch), F32),
        scratch_shapes=[pltpu.VMEM((tl, hid), F32)],
        compiler_params=_cparams(("parallel", "arbitrary")),
        name="hyena_filter_mlp",
    )(z, w1.astype(F32), b1.reshape(1, hid).astype(F32), w2.astype(F32), b2.reshape(1, hid).astype(F32),
      w3.astype(F32), b3.reshape(1, hid).astype(F32), freq.reshape(1, hid).astype(F32), w4.astype(BF16),
      t, deltas)


def _dft_tables(r):
    n = r * r
    k2 = jnp.arange(r, dtype=I32)
    n2 = jnp.arange(r // 2, dtype=I32)
    ang1 = (2.0 * math.pi / r) * ((k2[:, None] * n2[None, :]) % r).astype(F32)
    f1 = jnp.concatenate([jnp.cos(ang1), -jnp.sin(ang1)], axis=0)
    k1 = jnp.arange(r, dtype=I32)
    n1 = jnp.arange(r, dtype=I32)
    kk = r * k1[None, :, None] + k2[:, None, None]
    ang2 = (2.0 * math.pi / n) * ((kk * n1[None, None, :]) % n).astype(F32)
    mr, mi_ = jnp.cos(ang2), -jnp.sin(ang2)
    mf = jnp.concatenate([jnp.concatenate([mr, -mi_], axis=2),
                          jnp.concatenate([mi_, mr], axis=2)], axis=1)
    minv = jnp.swapaxes(mf, 1, 2)
    g3 = jnp.concatenate([jnp.cos(ang1.T), -jnp.sin(ang1.T)], axis=1) * (1.0 / n)
    eye = jnp.eye(N1_BLOCK, dtype=F32)
    f1, g3 = jnp.kron(f1, eye), jnp.kron(g3, eye)
    return f1.astype(BF16), mf.astype(BF16), minv.astype(BF16), g3.astype(BF16)


N1_BLOCK = SUBLANES


def _pack_complex(re, im):
    rb = lax.bitcast_convert_type(re.astype(BF16).astype(F32), U32)
    ib = lax.bitcast_convert_type(im.astype(BF16).astype(F32), U32)
    return (rb >> 16) | ib


def _unpack_complex_rows(w):
    re, im = _unpack(w)
    return jnp.concatenate([re, im], axis=0).astype(BF16)


def _fft1_kernel(f_ref, x_ref, o_ref, *, r):
    n_ch = x_ref.shape[-1]
    x = x_ref[...].reshape((r // 2) * N1_BLOCK, n_ch).astype(BF16)
    res = jnp.dot(f_ref[...], x, preferred_element_type=F32)
    half = r * N1_BLOCK
    o_ref[...] = _pack_complex(res[:half], res[half:]).reshape(r, N1_BLOCK, n_ch)


def _fft_stage1(x4, part, f1, r, n_ch):
    nb = x4.shape[1]
    xv = x4.reshape(x4.shape[0], nb, r // 2, r, n_ch)
    return pl.pallas_call(
        functools.partial(_fft1_kernel, r=r),
        grid=(nb, r // N1_BLOCK),
        in_specs=[pl.BlockSpec(f1.shape, lambda b, j: (0, 0)),
                  pl.BlockSpec((None, None, r // 2, N1_BLOCK, n_ch), lambda b, j: (part, b, 0, j, 0))],
        out_specs=pl.BlockSpec((None, r, N1_BLOCK, n_ch), lambda b, j: (b, 0, j, 0)),
        out_shape=jax.ShapeDtypeStruct((nb, r, r, n_ch), U32),
        compiler_params=_cparams(("parallel", "parallel")),
        name="hyena_dft_stage1",
    )(f1, xv)


def _k2_block(r):
    return max(1, (2 * LANES) // r)


def _filter_spec_kernel(mf_ref, bf_ref, bb_ref, o_ref, *, r):
    for kk in range(mf_ref.shape[0]):
        m = mf_ref[kk]
        xf = jnp.dot(m, _unpack_complex_rows(bf_ref[kk]), preferred_element_type=F32)
        xb = jnp.dot(m, _unpack_complex_rows(bb_ref[kk]), preferred_element_type=F32)
        o_ref[kk] = _pack_complex(xf[:r] + xb[:r], xf[r:] - xb[r:])


def _filter_spectrum(b1, mf, r, n_ch):
    kb = _k2_block(r)
    blk = lambda sel: pl.BlockSpec((None, kb, r, n_ch), lambda o, k: (2 * o + sel, k, 0, 0))
    return pl.pallas_call(
        functools.partial(_filter_spec_kernel, r=r),
        grid=(HYENA_ORDER, r // kb),
        in_specs=[pl.BlockSpec((kb, 2 * r, 2 * r), lambda o, k: (k, 0, 0)), blk(0), blk(1)],
        out_specs=pl.BlockSpec((None, kb, r, n_ch), lambda o, k: (o, k, 0, 0)),
        out_shape=jax.ShapeDtypeStruct((HYENA_ORDER, r, r, n_ch), U32),
        compiler_params=_cparams(("parallel", "parallel")),
        name="hyena_filter_spectrum",
    )(mf, b1, b1)


def _fft2_kernel(mf_ref, mi_ref, b_ref, h_ref, o_ref, *, r):
    for kk in range(mf_ref.shape[0]):
        x = jnp.dot(mf_ref[kk], _unpack_complex_rows(b_ref[kk]), preferred_element_type=F32)
        xr, xi = x[:r], x[r:]
        hr, hi = _unpack(h_ref[kk])
        y = jnp.concatenate([xr * hr - xi * hi, xr * hi + xi * hr], axis=0).astype(BF16)
        c = jnp.dot(mi_ref[kk], y, preferred_element_type=F32)
        o_ref[kk] = _pack_complex(c[:r], c[r:])


def _fft_stage2(b1, h, order, mf, minv, r, n_ch):
    nb = b1.shape[0]
    kb = _k2_block(r)
    blk = pl.BlockSpec((None, kb, r, n_ch), lambda k, b: (b, k, 0, 0))
    return pl.pallas_call(
        functools.partial(_fft2_kernel, r=r),
        grid=(r // kb, nb),
        in_specs=[pl.BlockSpec((kb, 2 * r, 2 * r), lambda k, b: (k, 0, 0)),
                  pl.BlockSpec((kb, 2 * r, 2 * r), lambda k, b: (k, 0, 0)),
                  blk,
                  pl.BlockSpec((None, kb, r, n_ch), lambda k, b: (order, k, 0, 0))],
        out_specs=blk,
        out_shape=jax.ShapeDtypeStruct((nb, r, r, n_ch), U32),
        compiler_params=_cparams(("parallel", "parallel")),
        name="hyena_dft_stage2",
    )(mf, minv, b1, h)


def _fft3_kernel(g_ref, c_ref, gate_ref, s_ref, bias_ref, ng_ref, *rest, final):
    if final:
        (o_ref,) = rest
    else:
        f_ref, o_ref, b_ref = rest
    r, nb, n_ch = c_ref.shape
    cc = _unpack_complex_rows(c_ref[...].reshape(r * nb, n_ch))
    y = jnp.dot(g_ref[...], cc, preferred_element_type=F32)
    rows = (r // 2) * nb
    s_new = gate_ref[...].reshape(rows, n_ch) * (y + s_ref[...].reshape(rows, n_ch) * bias_ref[...])
    if final:
        ms = jnp.mean(s_new * s_new, axis=-1, keepdims=True)
        s_new = s_new * lax.rsqrt(ms + RMS_EPS) * ng_ref[...]
    o_ref[...] = s_new.reshape(r // 2, nb, n_ch)
    if not final:
        res = jnp.dot(f_ref[...], s_new.astype(BF16), preferred_element_type=F32)
        half = r * nb
        b_ref[...] = _pack_complex(res[:half], res[half:]).reshape(r, nb, n_ch)


def _fft_stage3(c2, g3, z4, gate_part, s4, s_part, bias, norm_g, f1, r, n_ch):
    final = f1 is None
    nb = c2.shape[0]
    zv = z4.reshape(z4.shape[0], nb, r // 2, r, n_ch)
    sv = s4.reshape(s4.shape[0], nb, r // 2, r, n_ch)
    dspec = lambda part: pl.BlockSpec((None, None, r // 2, N1_BLOCK, n_ch), lambda b, j: (part, b, 0, j, 0))
    packed = pl.BlockSpec((None, r, N1_BLOCK, n_ch), lambda b, j: (b, 0, j, 0))
    s_spec = pl.BlockSpec((None, r // 2, N1_BLOCK, n_ch), lambda b, j: (b, 0, j, 0))
    s_shape = jax.ShapeDtypeStruct((nb, r // 2, r, n_ch), F32)
    in_specs = [pl.BlockSpec(g3.shape, lambda b, j: (0, 0)), packed, dspec(gate_part), dspec(s_part),
                pl.BlockSpec((1, n_ch), lambda b, j: (0, 0)), pl.BlockSpec((1, n_ch), lambda b, j: (0, 0))]
    args = [g3, c2, zv, sv, bias.reshape(1, n_ch).astype(F32), norm_g.reshape(1, n_ch).astype(F32)]
    if final:
        out_specs, out_shape = s_spec, s_shape
    else:
        in_specs.append(pl.BlockSpec(f1.shape, lambda b, j: (0, 0)))
        args.append(f1)
        out_specs = [s_spec, packed]
        out_shape = [s_shape, jax.ShapeDtypeStruct((nb, r, r, n_ch), U32)]
    return pl.pallas_call(
        functools.partial(_fft3_kernel, final=final),
        grid=(nb, r // N1_BLOCK),
        in_specs=in_specs,
        out_specs=out_specs,
        out_shape=out_shape,
        compiler_params=_cparams(("parallel", "parallel")),
        name="hyena_dft_stage3",
    )(*args)


def _hyena(zh, row_off, batch, seq, conv_w, conv_b, filt, fft_bias, out_g):
    n_ch = zh.shape[1] // 3
    r = int(round(math.sqrt(2 * seq)))
    assert r * r == 2 * seq and r % 16 == 0
    f1, mf, minv, g3 = _dft_tables(r)
    sig = _hyena_filter_signals(seq, *filt, n_ch=n_ch)
    hb1 = _fft_stage1(sig[None], 0, f1, r, n_ch)
    h = _filter_spectrum(hb1, mf, r, n_ch)
    z4 = _shortconv(zh, row_off, conv_w, conv_b, batch, seq)
    s4, s_part = z4, 2
    b1 = _fft_stage1(s4, s_part, f1, r, n_ch)
    for o in range(HYENA_ORDER):
        c2 = _fft_stage2(b1, h, o, mf, minv, r, n_ch)
        if o < HYENA_ORDER - 1:
            s, b1 = _fft_stage3(c2, g3, z4, o, s4, s_part, fft_bias[o], out_g, f1, r, n_ch)
        else:
            s = _fft_stage3(c2, g3, z4, o, s4, s_part, fft_bias[o], out_g, None, r, n_ch)
        s4, s_part = s.reshape(1, batch, seq, n_ch), 0
    return s4.reshape(batch * seq, n_ch)


def _extract_top(s, key, count):
    vals, keys = [], []
    for _ in range(count):
        m = jnp.max(s, axis=0, keepdims=True)
        kmin = jnp.min(jnp.where(s == m, key, jnp.inf), axis=0, keepdims=True)
        s = jnp.where(key == kmin, -jnp.inf, s)
        vals.append(m)
        keys.append(kmin)
    return vals, keys


def _peer_topk_kernel(q_ref, keys_ref, eid_ref, gate_ref):
    t = q_ref.shape[0]
    nk = PEER_NKEYS
    q = q_ref[...]
    row_key = lax.broadcasted_iota(I32, (nk, t), 0).astype(F32)
    tops = []
    for c in range(2):
        s = lax.dot_general(keys_ref[c], q[:, c * nk:(c + 1) * nk], (((1,), (1,)), ((), ())),
                            preferred_element_type=F32)
        tops.append(_extract_top(s, row_key, PEER_TOPK))
    (v1, i1), (v2, i2) = tops
    rows16 = lax.broadcasted_iota(I32, (PEER_TOPK, t), 0)
    v2a = jnp.zeros((PEER_TOPK, t), F32)
    i2a = jnp.zeros((PEER_TOPK, t), F32)
    for j in range(PEER_TOPK):
        v2a = jnp.where(rows16 == j, v2[j], v2a)
        i2a = jnp.where(rows16 == j, i2[j], i2a)
    n_exp = float(nk * nk)
    half = PEER_TOPK // 2
    rows8 = lax.broadcasted_iota(I32, (half, t), 0)
    pos8 = rows8.astype(F32)
    v2h, i2h = v2a[:half], i2a[:half]
    cand = [v1[0] + v2a]
    ckey = [rows16.astype(F32) * n_exp + (i1[0] * float(nk) + i2a)]
    for i in range(1, half):
        cand.append(jnp.where(rows8 < PEER_TOPK // (i + 1), v1[i] + v2h, -jnp.inf))
        ckey.append((pos8 + float(i * PEER_TOPK)) * n_exp + (i1[i] * float(nk) + i2h))
    v1t = jnp.zeros((half, t), F32)
    i1t = jnp.zeros((half, t), F32)
    for r in range(half):
        v1t = jnp.where(rows8 == r, v1[half + r], v1t)
        i1t = jnp.where(rows8 == r, i1[half + r], i1t)
    cand.append(v1t + v2[0])
    ckey.append((pos8 + float(half)) * (PEER_TOPK * n_exp) + (i1t * float(nk) + i2[0]))
    tv, tk_ = _extract_top(jnp.concatenate(cand, axis=0), jnp.concatenate(ckey, axis=0), PEER_TOPK)
    denom = jnp.zeros((1, t), F32)
    es = []
    for k in range(PEER_TOPK):
        e = jnp.exp(tv[k] - tv[0])
        es.append(e)
        denom = denom + e
    eid = jnp.zeros((PEER_TOPK, t), F32)
    gate = jnp.zeros((PEER_TOPK, t), F32)
    for k in range(PEER_TOPK):
        pos = jnp.floor(tk_[k] * (1.0 / n_exp))
        eid = jnp.where(rows16 == k, tk_[k] - pos * n_exp, eid)
        gate = jnp.where(rows16 == k, es[k] / denom, gate)
    eid_ref[...] = eid.astype(I32)
    gate_ref[...] = gate


def _peer_topk(q, keys, tm=512):
    t = q.shape[0]
    tm = min(tm, t)
    out_spec = pl.BlockSpec((PEER_TOPK, tm), lambda i, h: (h, i))
    return pl.pallas_call(
        _peer_topk_kernel,
        grid=(t // tm, PEER_HEADS),
        in_specs=[pl.BlockSpec((tm, 2 * PEER_NKEYS), lambda i, h: (i, h)),
                  pl.BlockSpec((None, 2, PEER_NKEYS, PEER_NKEYS), lambda i, h: (h, 0, 0, 0))],
        out_specs=[out_spec, out_spec],
        out_shape=[jax.ShapeDtypeStruct((PEER_HEADS * PEER_TOPK, t), I32),
                   jax.ShapeDtypeStruct((PEER_HEADS * PEER_TOPK, t), F32)],
        compiler_params=_cparams(("parallel", "parallel")),
        name="peer_topk",
    )(q, keys)


def _pack_table(tab):
    e, d = tab.shape
    assert d == 2 * SUBLANES * LANES
    bits = lax.bitcast_convert_type(tab.astype(BF16), jnp.uint16).astype(U32)
    packed = bits[:, :d // 2] | (bits[:, d // 2:] << 16)
    return packed.reshape(e * SUBLANES, LANES)


def _unpack(w):
    lo = lax.bitcast_convert_type(w << 16, F32)
    hi = lax.bitcast_convert_type(w & jnp.uint32(0xFFFF0000), F32)
    return lo, hi


_BITREV8 = (0, 4, 2, 6, 1, 5, 3, 7)


def _sublane_fold8(parts):
    sub = lax.broadcasted_iota(I32, (2 * SUBLANES, LANES), 0) // 2

    def rolled(a, shift):
        return pltpu.bitcast(pltpu.roll(pltpu.bitcast(a, U32), shift, axis=0), BF16)

    lvl = [parts[_BITREV8[r]] for r in range(8)]
    for shift, mask in ((4, sub < 4), (2, (sub % 4) < 2), (1, (sub % 2) < 1)):
        nxt = []
        for a, b in zip(lvl[0::2], lvl[1::2]):
            nxt.append(jnp.where(mask, a + rolled(a, SUBLANES - shift), b + rolled(b, shift)))
        lvl = nxt
    return lvl[0]


def _pair_ranges(half):
    if half == 0:
        return (0, PEER_WINDOW), (PEER_WINDOW, PEER_PAIRS)
    return (PEER_PAIRS - PEER_WINDOW, PEER_PAIRS), (0, PEER_PAIRS - PEER_WINDOW)


def _overflow(n_low, half):
    return n_low > PEER_WINDOW if half == 0 else n_low < PEER_PAIRS - PEER_WINDOW


def _table_rows(tab_ref, off):
    return _unpack(tab_ref[pl.ds(pl.multiple_of(off, SUBLANES), SUBLANES), :])


def _peer_u_kernel(off_ref, nlow_ref, tab_ref, x_ref, o_ref, extra_ref, *, tm, half):
    lane = lax.broadcasted_iota(I32, (SUBLANES, LANES), 1)
    sub = lax.broadcasted_iota(I32, (SUBLANES, LANES), 0)
    lane_grp = lax.shift_right_logical(lane, 3)
    diag = sub == (lane & (SUBLANES - 1))
    main, rest = _pair_ranges(half)

    def folded(t, p0, p1):
        xb = pltpu.bitcast(x_ref[t], BF16)
        out = []
        for g in range(p0 // SUBLANES, p1 // SUBLANES):
            parts = []
            for r in range(SUBLANES):
                off = pl.multiple_of(off_ref[t, g * SUBLANES + r], SUBLANES)
                parts.append(pltpu.bitcast(tab_ref[pl.ds(off, SUBLANES), :], BF16) * xb)
            out.append(pltpu.bitcast(_sublane_fold8(parts), U32))
        return out

    def lane_sums(folds, p0):
        mat = jnp.zeros((SUBLANES, LANES), F32)
        for i, f in enumerate(folds):
            lo, hi = _unpack(f)
            mat = jnp.where(lane_grp == p0 // SUBLANES + i, jnp.sum(lo + hi, axis=-1, keepdims=True), mat)
        return jnp.sum(jnp.where(diag, mat, 0.0), axis=0, keepdims=True)

    def finish(t, folds):
        o_ref[pl.ds(t, 1), :] = lane_sums(folds, main[0]) + extra_ref[pl.ds(t, 1), :]

    def token(t, prev):
        cur = folded(t, *main)
        finish(jnp.maximum(t - 1, 0), prev)
        extra_ref[pl.ds(t, 1), :] = jnp.zeros((1, LANES), F32)

        @pl.when(_overflow(nlow_ref[0, t], half))
        def _():
            extra_ref[pl.ds(t, 1), :] = lane_sums(folded(t, *rest), rest[0])

        return tuple(cur)

    extra_ref[pl.ds(0, 1), :] = jnp.zeros((1, LANES), F32)
    zeros = tuple(jnp.zeros((SUBLANES, LANES), U32) for _ in range((main[1] - main[0]) // SUBLANES))
    last = lax.fori_loop(0, tm, token, zeros)
    finish(tm - 1, last)


def _peer_u(off, nlow3, tab, x4, half, tm=128):
    t = off.shape[0]
    rows = PEER_HALF * SUBLANES
    return pl.pallas_call(
        functools.partial(_peer_u_kernel, tm=tm, half=half),
        grid=(t // tm,),
        in_specs=[pl.BlockSpec((tm, LANES), lambda i: (i, 0), memory_space=pltpu.SMEM),
                  pl.BlockSpec((None, 1, tm), lambda i: (i, 0, 0), memory_space=pltpu.SMEM),
                  pl.BlockSpec((rows, LANES), lambda i: (half, 0), pipeline_mode=pl.Buffered(1)),
                  pl.BlockSpec((tm, SUBLANES, LANES), lambda i: (i, 0, 0))],
        out_specs=pl.BlockSpec((tm, LANES), lambda i: (i, 0)),
        out_shape=jax.ShapeDtypeStruct((t, LANES), F32),
        scratch_shapes=[pltpu.VMEM((tm, LANES), F32)],
        compiler_params=_cparams(("arbitrary",)),
        name="peer_expert_scores",
    )(off, nlow3, tab, x4)


def _peer_coef_kernel(a0_ref, a1_ref, eid_ref, gate_ref, c0_ref, c1_ref):
    low = eid_ref[...] < PEER_HALF
    a = jnp.where(low, a0_ref[...], a1_ref[...])
    coef = gate_ref[...] * (0.5 * a * (1.0 + lax.erf(a * (1.0 / math.sqrt(2.0)))))
    c0_ref[...] = jnp.where(low, coef, 0.0)
    c1_ref[...] = jnp.where(low, 0.0, coef)


def _peer_coef(a0, a1, eid, gate, tm=1024):
    t = eid.shape[0]
    tm = min(tm, t)
    spec = pl.BlockSpec((tm, LANES), lambda i: (i, 0))
    return pl.pallas_call(
        _peer_coef_kernel,
        grid=(t // tm,),
        in_specs=[spec, spec, spec, spec],
        out_specs=[spec, spec],
        out_shape=[jax.ShapeDtypeStruct((t, LANES), F32)] * 2,
        compiler_params=_cparams(("parallel",)),
        name="peer_coef",
    )(a0, a1, eid, gate)


PEER_SUM_UNROLL = 8


def _to_row_tiles(x):
    t, d = x.shape
    return x.reshape(t, 2, SUBLANES, LANES).transpose(0, 2, 1, 3).reshape(t, 2 * SUBLANES, LANES)


def _from_row_tiles(y):
    t = y.shape[0]
    return y.reshape(t, SUBLANES, 2, LANES).transpose(0, 2, 1, 3).reshape(t, 2 * SUBLANES * LANES)


def _peer_v_kernel(off_ref, nlow_ref, coef_ref, tab_ref, base_ref, *o_refs, tm, half, n_first):
    main, rest = _pair_ranges(half)
    rows16 = 2 * SUBLANES
    lane = lax.broadcasted_iota(I32, (rows16, LANES), 1)
    diag = (lane & (rows16 - 1)) == lax.broadcasted_iota(I32, (rows16, LANES), 0)
    pair_in_group = lax.shift_right_logical(lane, 4)

    def weighted(t, p0, p1):
        crow = jnp.broadcast_to(coef_ref[pl.ds(t, 1), :], (rows16, LANES))
        cmat = jnp.concatenate(
            [jnp.where(diag, jnp.take_along_axis(crow, pair_in_group + p, axis=1), 0.0)
             for p in range(p0, p1, SUBLANES)], axis=1).astype(BF16)
        tiles = [pltpu.bitcast(tab_ref[pl.ds(pl.multiple_of(off_ref[t, p], SUBLANES), SUBLANES), :], BF16)
                 for p in range(p0, p1)]
        return jnp.dot(cmat, jnp.concatenate(tiles, axis=0), preferred_element_type=F32)

    def run(o_ref):
        def group(g, carry):
            for u in range(PEER_SUM_UNROLL):
                t = g * PEER_SUM_UNROLL + u
                o_ref[t] = base_ref[t] + weighted(t, *main)
            return carry

        lax.fori_loop(0, tm // PEER_SUM_UNROLL, group, 0)

        def overflow(t, carry):
            @pl.when(_overflow(nlow_ref[0, t], half))
            def _():
                o_ref[t] = o_ref[t] + weighted(t, *rest)

            return carry

        lax.fori_loop(0, tm, overflow, 0)

    if len(o_refs) == 1:
        run(o_refs[0])
    else:
        first = pl.program_id(0) < n_first
        pl.when(first)(lambda: run(o_refs[0]))
        pl.when(jnp.logical_not(first))(lambda: run(o_refs[1]))


def _peer_v(off, nlow3, coef, tab, base, half, tm=128, split=None):
    t = base.shape[0]
    rows = PEER_HALF * SUBLANES
    assert tm % PEER_SUM_UNROLL == 0
    blk = (tm, 2 * SUBLANES, LANES)
    tile = pl.BlockSpec(blk, lambda i: (i, 0, 0))
    if split is None:
        n_first, out_specs = 0, tile
        out_shape = jax.ShapeDtypeStruct((t,) + blk[1:], F32)
    else:
        assert split % tm == 0
        n_first, n_rest = split // tm, (t - split) // tm
        out_specs = [pl.BlockSpec(blk, lambda i: (jnp.clip(i, 0, n_first - 1), 0, 0)),
                     pl.BlockSpec(blk, lambda i: (jnp.clip(i - n_first, 0, n_rest - 1), 0, 0))]
        out_shape = [jax.ShapeDtypeStruct((split,) + blk[1:], F32),
                     jax.ShapeDtypeStruct((t - split,) + blk[1:], F32)]
    return pl.pallas_call(
        functools.partial(_peer_v_kernel, tm=tm, half=half, n_first=n_first),
        grid=(t // tm,),
        in_specs=[pl.BlockSpec((tm, LANES), lambda i: (i, 0), memory_space=pltpu.SMEM),
                  pl.BlockSpec((None, 1, tm), lambda i: (i, 0, 0), memory_space=pltpu.SMEM),
                  pl.BlockSpec((tm, LANES), lambda i: (i, 0)),
                  pl.BlockSpec((rows, LANES), lambda i: (half, 0), pipeline_mode=pl.Buffered(1)),
                  tile],
        out_specs=out_specs,
        out_shape=out_shape,
        compiler_params=_cparams(("arbitrary",)),
        name="peer_expert_sum",
    )(off, nlow3, coef, tab, base)


def _peer(x1, split, norm2_g, wq_bf, keys_bf, u_packed, v_packed, tm=128):
    t, d = x1.shape
    tm = min(tm, t)
    (xn_bf,) = _rmsnorm(x1, norm2_g, (BF16,))
    q = _matmul(xn_bf, wq_bf, wq_bf.shape[1], 0, BF16)
    eid_t, gate_t = _peer_topk(q, keys_bf)
    eid, gate = eid_t.T, gate_t.T
    n_exp = PEER_NKEYS * PEER_NKEYS
    high = (eid >= PEER_HALF).astype(I32)
    nlow3 = (PEER_PAIRS - jnp.sum(high, axis=-1, dtype=I32)).reshape(t // tm, 1, tm)
    slot = lax.broadcasted_iota(I32, eid.shape, 1)
    order = jnp.sort((high * PEER_PAIRS + slot) * n_exp + eid, axis=-1)
    eid = order % n_exp
    gate = jnp.take_along_axis(gate, (order // n_exp) % PEER_PAIRS, axis=-1)
    off = (eid & (PEER_HALF - 1)) * SUBLANES
    xw = _pack_table(xn_bf).reshape(t, SUBLANES, LANES)
    a0 = _peer_u(off, nlow3, u_packed, xw, 0, tm)
    a1 = _peer_u(off, nlow3, u_packed, xw, 1, tm)
    c0, c1 = _peer_coef(a0, a1, eid, gate)
    y = _peer_v(off, nlow3, c0, v_packed, _to_row_tiles(x1), 0, tm)
    ya, yb = _peer_v(off, nlow3, c1, v_packed, y, 1, tm, split=split)
    return _from_row_tiles(ya), _from_row_tiles(yb)


def kernel(x_prompt, x_sample, norm1_g, w_in, q_norm_g, k_norm_g, lambda_q1, lambda_k1, lambda_q2, lambda_k2, attn_sub_g, conv_w, conv_b, filt_w1, filt_b1, filt_w2, filt_b2, filt_w3, filt_b3, filt_w4, filt_freq, fft_bias, hyena_out_g, w_out, norm2_g, peer_wq, peer_keys, peer_u, peer_v):
    depth = w_in.shape[0]
    d_model = x_prompt.shape[-1]
    att_w = ATT_HEADS * ATT_VDIM
    shapes = [x_prompt.shape[:2], x_sample.shape[:2]]
    xs = [x_prompt.reshape(-1, d_model), x_sample.reshape(-1, d_model)]
    n0 = xs[0].shape[0]
    slopes = 2.0 ** (-8.0 * jnp.arange(1, ATT_HEADS + 1, dtype=F32) / ATT_HEADS)

    for l in range(depth):
        lambda_init = 0.8 - 0.6 * math.exp(-0.3 * l)
        lam = (jnp.exp(jnp.sum(lambda_q1[l].astype(F32) * lambda_k1[l].astype(F32)))
               - jnp.exp(jnp.sum(lambda_q2[l].astype(F32) * lambda_k2[l].astype(F32)))
               + lambda_init).reshape(1)
        w_in_bf = w_in[l].astype(BF16)
        q_gain = jnp.tile(q_norm_g[l].astype(F32), 2 * ATT_HEADS) * (ATT_QKDIM ** -0.5 * LOG2E)
        k_gain = jnp.tile(k_norm_g[l].astype(F32), 2 * ATT_HEADS)
        qk_gain = jnp.concatenate([q_gain, k_gain]).reshape(1, 2 * att_w)

        h_bf = _rmsnorm_stacked(xs[0], xs[1], norm1_g[l], BF16)
        qk = _matmul(h_bf, w_in_bf, 2 * att_w, 0, BF16, mode="qknorm", extra=qk_gain)
        vt = _matmul_nt(w_in_bf[:, 2 * att_w:3 * att_w].T, h_bf, BF16)
        zh = _matmul(h_bf, w_in_bf, w_in.shape[2] - 3 * att_w, 3 * att_w, F32)

        filt = (filt_w1[l], filt_b1[l], filt_w2[l], filt_b2[l], filt_w3[l], filt_b3[l], filt_w4[l], filt_freq[l])
        segs, row = [], 0
        for (b, s), x_seg in zip(shapes, xs):
            att = _attention(qk, vt, row, slopes, lam, attn_sub_g[l], b, s, 1.0 - lambda_init)
            hy = _hyena(zh, row, b, s, conv_w[l], conv_b[l], filt, fft_bias[l], hyena_out_g[l])
            segs.append((att, hy, x_seg))
            row += b * s
        x1 = _outproj_stacked(segs[0], segs[1], w_out[l].astype(BF16))

        xs = _peer(x1, n0, norm2_g[l], peer_wq[l].astype(BF16), peer_keys[l].astype(BF16),
                   _pack_table(peer_u[l]), _pack_table(peer_v[l]))

    return (xs[0].reshape(x_prompt.shape), xs[1].reshape(x_sample.shape))
```

```python
import functools
import math

import jax
import jax.numpy as jnp
from jax import lax
from jax.experimental import pallas as pl
from jax.experimental.pallas import tpu as pltpu

F32 = jnp.float32
BF16 = jnp.bfloat16
I32 = jnp.int32
U32 = jnp.uint32

RMS_EPS = 1e-6
LOG2E = 1.4426950408889634
LANES = 128
SUBLANES = 8
VMEM_LIMIT_BYTES = 56 * 1024 * 1024

ATT_HEADS = 8
ATT_VDIM = 128
ATT_QKDIM = 64
HYENA_ORDER = 2
FILTER_BANDS = 16
DECAY_FAST = 0.3
DECAY_SLOW = 1.5
DECAY_TARGET = 1e-2
PEER_HEADS = 8
PEER_NKEYS = 128
PEER_TOPK = 16
PEER_HALF = PEER_NKEYS * PEER_NKEYS // 2
PEER_PAIRS = PEER_HEADS * PEER_TOPK
PEER_WINDOW = 80


def _cparams(sem, vmem=VMEM_LIMIT_BYTES):
    return pltpu.CompilerParams(dimension_semantics=sem, vmem_limit_bytes=vmem)


def _segment_spec(block, start, count, minor=0):
    return pl.BlockSpec(block, lambda i: (jnp.clip(i - start, 0, count - 1), minor))


def _rmsnorm_kernel(x_ref, g_ref, *o_refs):
    x = x_ref[...]
    ms = jnp.mean(x * x, axis=-1, keepdims=True)
    y = x * lax.rsqrt(ms + RMS_EPS) * g_ref[...]
    for o_ref in o_refs:
        o_ref[...] = y.astype(o_ref.dtype)


def _rmsnorm2_kernel(xa_ref, xb_ref, g_ref, o_ref, *, na):
    first = pl.program_id(0) < na
    x = jnp.where(first, xa_ref[...], xb_ref[...])
    ms = jnp.mean(x * x, axis=-1, keepdims=True)
    o_ref[...] = (x * lax.rsqrt(ms + RMS_EPS) * g_ref[...]).astype(o_ref.dtype)


def _rmsnorm_stacked(xa, xb, g, out_dtype, tm=512):
    d = xa.shape[1]
    na, nb = xa.shape[0] // tm, xb.shape[0] // tm
    assert xa.shape[0] % tm == 0 and xb.shape[0] % tm == 0
    return pl.pallas_call(
        functools.partial(_rmsnorm2_kernel, na=na),
        grid=(na + nb,),
        in_specs=[_segment_spec((tm, d), 0, na), _segment_spec((tm, d), na, nb),
                  pl.BlockSpec((1, d), lambda i: (0, 0))],
        out_specs=pl.BlockSpec((tm, d), lambda i: (i, 0)),
        out_shape=jax.ShapeDtypeStruct((xa.shape[0] + xb.shape[0], d), out_dtype),
        compiler_params=_cparams(("parallel",)),
        name="rmsnorm_stacked",
    )(xa, xb, g.reshape(1, d).astype(F32))


def _rmsnorm(x, g, out_dtypes, tm=512):
    t, d = x.shape
    tm = min(tm, t)
    spec = pl.BlockSpec((tm, d), lambda i: (i, 0))
    return pl.pallas_call(
        _rmsnorm_kernel,
        grid=(t // tm,),
        in_specs=[spec, pl.BlockSpec((1, d), lambda i: (0, 0))],
        out_specs=[spec for _ in out_dtypes],
        out_shape=[jax.ShapeDtypeStruct((t, d), dt) for dt in out_dtypes],
        compiler_params=_cparams(("parallel",)),
        name="rmsnorm",
    )(x, g.reshape(1, d).astype(F32))


def _group_rms_scale(x, gain):
    lane = lax.broadcasted_iota(I32, x.shape, 1)
    lo = lane < ATT_QKDIM
    x2 = x * x
    s_lo = jnp.sum(jnp.where(lo, x2, 0.0), axis=-1, keepdims=True)
    s_hi = jnp.sum(jnp.where(lo, 0.0, x2), axis=-1, keepdims=True)
    ms = jnp.where(lo, s_lo, s_hi) * (1.0 / ATT_QKDIM)
    return x * lax.rsqrt(ms + RMS_EPS) * gain


def _mm_kernel(a_ref, b_ref, *rest, mode):
    acc = jnp.dot(a_ref[...], b_ref[...], preferred_element_type=F32)
    if mode == "plain":
        (o_ref,) = rest
        o_ref[...] = acc.astype(o_ref.dtype)
    elif mode == "residual":
        r_ref, o_ref = rest
        o_ref[...] = (acc + r_ref[...]).astype(o_ref.dtype)
    elif mode == "qknorm":
        g_ref, o_ref = rest
        for c in range(acc.shape[1] // LANES):
            sl = slice(c * LANES, (c + 1) * LANES)
            o_ref[:, sl] = _group_rms_scale(acc[:, sl], g_ref[:, sl]).astype(o_ref.dtype)
    else:
        raise ValueError(mode)


def _matmul(a, b, n_cols, col_off, out_dtype, mode="plain", extra=None, tm=1024, tn=1024):
    m, k = a.shape
    tm = min(tm, m)
    tn = min(tn, n_cols)
    assert col_off % tn == 0 and n_cols % tn == 0 and m % tm == 0
    off = col_off // tn
    in_specs = [pl.BlockSpec((tm, k), lambda i, j: (i, 0)),
                pl.BlockSpec((k, tn), lambda i, j: (0, j + off))]
    args = [a, b]
    if mode == "residual":
        in_specs.append(pl.BlockSpec((tm, tn), lambda i, j: (i, j)))
        args.append(extra)
    elif mode == "qknorm":
        in_specs.append(pl.BlockSpec((1, tn), lambda i, j: (0, j)))
        args.append(extra)
    return pl.pallas_call(
        functools.partial(_mm_kernel, mode=mode),
        grid=(m // tm, n_cols // tn),
        in_specs=in_specs,
        out_specs=pl.BlockSpec((tm, tn), lambda i, j: (i, j)),
        out_shape=jax.ShapeDtypeStruct((m, n_cols), out_dtype),
        compiler_params=_cparams(("parallel", "arbitrary")),
        name="matmul_" + mode,
    )(*args)


def _outproj_kernel(att_a, hy_a, x_a, att_b, hy_b, x_b, w_ref, o_ref, *, na):
    kw = att_a.shape[1]

    def run(att_ref, hy_ref, x_ref):
        acc = jnp.dot(att_ref[...], w_ref[:kw, :], preferred_element_type=F32)
        acc = acc + jnp.dot(hy_ref[...].astype(BF16), w_ref[kw:, :], preferred_element_type=F32)
        o_ref[...] = acc + x_ref[...]

    first = pl.program_id(0) < na
    pl.when(first)(lambda: run(att_a, hy_a, x_a))
    pl.when(jnp.logical_not(first))(lambda: run(att_b, hy_b, x_b))


def _outproj_stacked(seg_a, seg_b, w, tm=512, tn=1024):
    n_out = w.shape[1]
    na, nb = seg_a[0].shape[0] // tm, seg_b[0].shape[0] // tm
    assert seg_a[0].shape[0] % tm == 0 and seg_b[0].shape[0] % tm == 0 and n_out % tn == 0

    def specs(seg, start, count):
        att, hy, x = seg
        row = lambda i, j: (jnp.clip(i - start, 0, count - 1), 0)
        return [pl.BlockSpec((tm, att.shape[1]), row), pl.BlockSpec((tm, hy.shape[1]), row),
                pl.BlockSpec((tm, tn), lambda i, j: (jnp.clip(i - start, 0, count - 1), j))]

    return pl.pallas_call(
        functools.partial(_outproj_kernel, na=na),
        grid=(na + nb, n_out // tn),
        in_specs=specs(seg_a, 0, na) + specs(seg_b, na, nb) + [pl.BlockSpec((w.shape[0], tn), lambda i, j: (0, j))],
        out_specs=pl.BlockSpec((tm, tn), lambda i, j: (i, j)),
        out_shape=jax.ShapeDtypeStruct(((na + nb) * tm, n_out), F32),
        compiler_params=_cparams(("parallel", "arbitrary")),
        name="out_projection",
    )(*seg_a, *seg_b, w)


def _mm_nt_kernel(w_ref, a_ref, o_ref):
    o_ref[...] = lax.dot_general(w_ref[...], a_ref[...], (((1,), (1,)), ((), ())),
                                 preferred_element_type=F32).astype(o_ref.dtype)


def _matmul_nt(wt, a, out_dtype, tm=1024):
    n, k = wt.shape
    m = a.shape[0]
    tm = min(tm, m)
    return pl.pallas_call(
        _mm_nt_kernel,
        grid=(m // tm,),
        in_specs=[pl.BlockSpec((n, k), lambda i: (0, 0)), pl.BlockSpec((tm, k), lambda i: (i, 0))],
        out_specs=pl.BlockSpec((n, tm), lambda i: (0, i)),
        out_shape=jax.ShapeDtypeStruct((n, m), out_dtype),
        compiler_params=_cparams(("parallel",)),
        name="matmul_nt",
    )(wt, a)


def _alibi_columns(slopes, tq, tk):
    return [_alibi_side(slopes, tq, True), _alibi_side(slopes, tk, False)]


def _alibi_side(slopes, n, query_side):
    pos = jnp.arange(n, dtype=F32)
    val = (slopes.astype(F32) * LOG2E)[:, None] * pos[None, :]

    def pieces(x):
        p1 = x.astype(BF16)
        r1 = x - p1.astype(F32)
        p2 = r1.astype(BF16)
        p3 = (r1 - p2.astype(F32)).astype(BF16)
        return [p1, p2, p3]

    ones = [jnp.ones_like(val, BF16)] * 3
    six = jnp.stack(pieces(-val) + ones if query_side else ones + pieces(val), axis=-1)
    pad = jnp.zeros(val.shape + (ATT_QKDIM - 6,), BF16)
    return jnp.concatenate([six, pad, six, pad], axis=-1)


def _attn_kernel(slope_ref, lam_ref, q_ref, k_ref, vt_ref, aq_ref, ak_ref, g_ref, o_ref,
                 m_ref, l_ref, acc_ref, *, tq, tk, nk, hb, out_scale):
    hg = pl.program_id(1)
    i = pl.program_id(2)
    j = pl.program_id(3)

    @pl.when(j == 0)
    def _():
        m_ref[...] = jnp.full(m_ref.shape, -jnp.inf, F32)
        l_ref[...] = jnp.zeros(l_ref.shape, F32)
        acc_ref[...] = jnp.zeros(acc_ref.shape, F32)

    q_first = lax.broadcasted_iota(I32, (tq, LANES), 1) < ATT_QKDIM
    k_first = lax.broadcasted_iota(I32, (tk, LANES), 1) < ATT_QKDIM
    nt = (((1,), (1,)), ((), ()))

    def update(hh, scores, shift):
        vt = vt_ref[hh * LANES:(hh + 1) * LANES, :]
        for c, s in enumerate(scores):
            m_old = m_ref[hh, c]
            m_new = jnp.maximum(m_old, jnp.max(s, axis=0, keepdims=True) + shift)
            alpha = jnp.exp2(m_old - m_new)
            p = jnp.exp2(s - (m_new - shift))
            l_ref[hh, c] = alpha * l_ref[hh, c] + jnp.sum(p, axis=0, keepdims=True)
            acc_ref[hh, c] = alpha * acc_ref[hh, c] + jnp.dot(vt, p.astype(BF16),
                                                              preferred_element_type=F32)
            m_ref[hh, c] = m_new

    keys_before = i * tq >= (j + 1) * tk
    keys_after = (i + 1) * tq <= j * tk
    off_diagonal = jnp.logical_or(keys_before, keys_after)

    @pl.when(off_diagonal)
    def _():
        sign = jnp.where(keys_after, -1.0, 1.0).astype(BF16)
        gap = jnp.abs(i * tq - j * tk).astype(F32)
        for hh in range(hb):
            sl = slice(hh * LANES, (hh + 1) * LANES)
            q = q_ref[:, sl]
            k = k_ref[:, sl]
            aq = aq_ref[hh]
            ak = ak_ref[hh] * sign
            s0 = lax.dot_general(jnp.where(k_first, k, ak), jnp.where(q_first, q, aq), nt,
                                 preferred_element_type=F32)
            s1 = lax.dot_general(jnp.where(k_first, ak, k), jnp.where(q_first, aq, q), nt,
                                 preferred_element_type=F32)
            update(hh, (s0, s1), -(slope_ref[hg * hb + hh] * LOG2E) * gap)

    @pl.when(jnp.logical_not(off_diagonal))
    def _():
        kpos = lax.broadcasted_iota(I32, (tk, tq), 0) + j * tk
        qpos = lax.broadcasted_iota(I32, (tk, tq), 1) + i * tq
        dist = jnp.abs(kpos - qpos).astype(F32)
        for hh in range(hb):
            sl = slice(hh * LANES, (hh + 1) * LANES)
            q = q_ref[:, sl]
            k = k_ref[:, sl]
            zero = jnp.zeros_like(q)
            bias = dist * (-(slope_ref[hg * hb + hh] * LOG2E))
            s0 = lax.dot_general(k, jnp.where(q_first, q, zero), nt, preferred_element_type=F32) + bias
            s1 = lax.dot_general(k, jnp.where(q_first, zero, q), nt, preferred_element_type=F32) + bias
            update(hh, (s0, s1), 0.0)

    @pl.when(j == nk - 1)
    def _():
        for hh in range(hb):
            o = acc_ref[hh, 0] / l_ref[hh, 0] - lam_ref[0] * (acc_ref[hh, 1] / l_ref[hh, 1])
            ms = jnp.mean(o * o, axis=0, keepdims=True)
            y = o * lax.rsqrt(ms + RMS_EPS) * (g_ref[...] * out_scale)
            o_ref[:, hh * LANES:(hh + 1) * LANES] = y.T.astype(o_ref.dtype)


def _attention(qk, vt, row_off, slopes, lam, sub_g, batch, seq, out_scale, tq=512, tk=1024, hb=4):
    tq = min(tq, seq)
    tk = min(tk, seq // 4)
    nq, nk = seq // tq, seq // tk
    assert row_off % tq == 0 and row_off % tk == 0 and ATT_HEADS % hb == 0
    oq, ok = row_off // tq, row_off // tk
    ng = ATT_HEADS // hb
    aq, ak = _alibi_columns(slopes, tq, tk)
    kern = functools.partial(_attn_kernel, tq=tq, tk=tk, nk=nk, hb=hb, out_scale=out_scale)
    smem = pl.BlockSpec(memory_space=pltpu.SMEM)
    return pl.pallas_call(
        kern,
        grid=(batch, ng, nq, nk),
        in_specs=[smem, smem,
                  pl.BlockSpec((tq, hb * LANES), lambda b, h, i, j: (oq + b * nq + i, h)),
                  pl.BlockSpec((tk, hb * LANES), lambda b, h, i, j: (ok + b * nk + j, ng + h)),
                  pl.BlockSpec((hb * LANES, tk), lambda b, h, i, j: (h, ok + b * nk + j)),
                  pl.BlockSpec((hb, tq, LANES), lambda b, h, i, j: (h, 0, 0)),
                  pl.BlockSpec((hb, tk, LANES), lambda b, h, i, j: (h, 0, 0)),
                  pl.BlockSpec((LANES, 1), lambda b, h, i, j: (0, 0))],
        out_specs=pl.BlockSpec((tq, hb * LANES), lambda b, h, i, j: (b * nq + i, h)),
        out_shape=jax.ShapeDtypeStruct((batch * seq, ATT_HEADS * ATT_VDIM), BF16),
        scratch_shapes=[pltpu.VMEM((hb, 2, 1, tq), F32), pltpu.VMEM((hb, 2, 1, tq), F32),
                        pltpu.VMEM((hb, 2, LANES, tq), F32)],
        compiler_params=_cparams(("parallel", "parallel", "parallel", "arbitrary")),
        name="diff_attention",
    )(slopes, lam, qk, qk, vt, aq, ak, sub_g.reshape(LANES, 1).astype(F32))


def _shortconv_kernel(z_ref, w_ref, b_ref, o_ref):
    z = z_ref[...]
    n = z.shape[0]
    row = lax.broadcasted_iota(I32, z.shape, 0)
    prev = jnp.where(row == 0, 0.0, pltpu.roll(z, 1, axis=0))
    nxt = jnp.where(row == n - 1, 0.0, pltpu.roll(z, n - 1, axis=0))
    w = w_ref[...]
    o_ref[...] = prev * w[0:1] + z * w[1:2] + nxt * w[2:3] + b_ref[...]


def _shortconv(zh, row_off, conv_w, conv_b, batch, seq, cb=256):
    c3 = zh.shape[1]
    c = c3 // 3
    ncb = c // cb
    assert row_off % seq == 0
    ob = row_off // seq
    out = pl.pallas_call(
        _shortconv_kernel,
        grid=(batch, 3, ncb),
        in_specs=[pl.BlockSpec((seq, cb), lambda b, p, j: (ob + b, p * ncb + j)),
                  pl.BlockSpec((3, cb), lambda b, p, j: (0, p * ncb + j)),
                  pl.BlockSpec((1, cb), lambda b, p, j: (0, p * ncb + j))],
        out_specs=pl.BlockSpec((None, None, seq, cb), lambda b, p, j: (p, b, 0, j)),
        out_shape=jax.ShapeDtypeStruct((3, batch, seq, c), F32),
        compiler_params=_cparams(("parallel", "parallel", "parallel")),
        name="hyena_shortconv",
    )(zh, conv_w.astype(F32), conv_b.reshape(1, c3).astype(F32))
    return out


def _filter_kernel(z_ref, w1_ref, b1_ref, w2_ref, b2_ref, w3_ref, b3_ref, fr_ref, w4_ref,
                   t_ref, d_ref, o_ref, h_ref, *, tl):
    i = pl.program_id(0)
    g = pl.program_id(1)
    hi = lax.Precision.HIGHEST

    @pl.when(g == 0)
    def _():
        fr = fr_ref[...]
        h = jnp.sin(fr * (jnp.dot(z_ref[...], w1_ref[...], precision=hi, preferred_element_type=F32) + b1_ref[...]))
        h = jnp.sin(fr * (jnp.dot(h, w2_ref[...], precision=hi, preferred_element_type=F32) + b2_ref[...]))
        h_ref[...] = jnp.sin(fr * (jnp.dot(h, w3_ref[...], precision=hi, preferred_element_type=F32) + b3_ref[...]))

    f = jnp.dot(h_ref[...].astype(BF16), w4_ref[...], preferred_element_type=F32)
    f = f * jnp.exp(-t_ref[...] * d_ref[...])
    row = lax.broadcasted_iota(I32, f.shape, 0) + i * tl
    drop = jnp.logical_and(row == 0, g % 2 == 1)
    o_ref[...] = jnp.where(drop, 0.0, f)


def _hyena_filter_signals(seq, w1, b1, w2, b2, w3, b3, w4, freq, n_ch, tl=512):
    t = jnp.linspace(0.0, 1.0, seq, dtype=F32)[:, None]
    w = 2.0 * math.pi * jnp.arange(seq, dtype=F32)[:, None] / seq
    f = jnp.linspace(1e-4, FILTER_BANDS - 1, FILTER_BANDS, dtype=F32)[None, :]
    z = jnp.concatenate([t, jnp.cos(f * w), -jnp.sin(f * w)], axis=-1)
    deltas = jnp.abs(jnp.linspace(math.log(DECAY_FAST) / DECAY_TARGET,
                                  math.log(DECAY_SLOW) / DECAY_TARGET, n_ch, dtype=F32))[None, :]
    hid = w1.shape[1]
    emb = LANES
    z = jnp.pad(z, ((0, 0), (0, emb - z.shape[1])))
    w1 = jnp.pad(w1.astype(F32), ((0, emb - w1.shape[0]), (0, 0)))
    tl = min(tl, seq)
    full = lambda shape: pl.BlockSpec(shape, lambda i, g: tuple(0 for _ in shape))
    return pl.pallas_call(
        functools.partial(_filter_kernel, tl=tl),
        grid=(seq // tl, 2 * HYENA_ORDER),
        in_specs=[pl.BlockSpec((tl, emb), lambda i, g: (i, 0)),
                  full((emb, hid)), full((1, hid)), full((hid, hid)), full((1, hid)),
                  full((hid, hid)), full((1, hid)), full((1, hid)),
                  pl.BlockSpec((hid, n_ch), lambda i, g: (0, g)),
                  pl.BlockSpec((tl, 1), lambda i, g: (i, 0)),
                  full((1, n_ch))],
        out_specs=pl.BlockSpec((None, tl, n_ch), lambda i, g: (g, i, 0)),
        out_shape=jax.ShapeDtypeStruct((2 * HYENA_ORDER, seq, n_ch), F32),
        scratch_shapes=[pltpu.VMEM((tl, hid), F32)],
        compiler_params=_cparams(("parallel", "arbitrary")),
        name="hyena_filter_mlp",
    )(z, w1.astype(F32), b1.reshape(1, hid).astype(F32), w2.astype(F32), b2.reshape(1, hid).astype(F32),
      w3.astype(F32), b3.reshape(1, hid).astype(F32), freq.reshape(1, hid).astype(F32), w4.astype(BF16),
      t, deltas)


def _dft_tables(r):
    n = r * r
    k2 = jnp.arange(r, dtype=I32)
    n2 = jnp.arange(r // 2, dtype=I32)
    ang1 = (2.0 * math.pi / r) * ((k2[:, None] * n2[None, :]) % r).astype(F32)
    f1 = jnp.concatenate([jnp.cos(ang1), -jnp.sin(ang1)], axis=0)
    k1 = jnp.arange(r, dtype=I32)
    n1 = jnp.arange(r, dtype=I32)
    kk = r * k1[None, :, None] + k2[:, None, None]
    ang2 = (2.0 * math.pi / n) * ((kk * n1[None, None, :]) % n).astype(F32)
    mr, mi_ = jnp.cos(ang2), -jnp.sin(ang2)
    mf = jnp.concatenate([jnp.concatenate([mr, -mi_], axis=2),
                          jnp.concatenate([mi_, mr], axis=2)], axis=1)
    minv = jnp.swapaxes(mf, 1, 2)
    g3 = jnp.concatenate([jnp.cos(ang1.T), -jnp.sin(ang1.T)], axis=1) * (1.0 / n)
    eye = jnp.eye(N1_BLOCK, dtype=F32)
    f1, g3 = jnp.kron(f1, eye), jnp.kron(g3, eye)
    return f1.astype(BF16), mf.astype(BF16), minv.astype(BF16), g3.astype(BF16)


N1_BLOCK = SUBLANES


def _pack_complex(re, im):
    rb = lax.bitcast_convert_type(re.astype(BF16).astype(F32), U32)
    ib = lax.bitcast_convert_type(im.astype(BF16).astype(F32), U32)
    return (rb >> 16) | ib


def _unpack_complex_rows(w):
    re, im = _unpack(w)
    return jnp.concatenate([re, im], axis=0).astype(BF16)


def _fft1_kernel(f_ref, x_ref, o_ref, *, r):
    n_ch = x_ref.shape[-1]
    x = x_ref[...].reshape((r // 2) * N1_BLOCK, n_ch).astype(BF16)
    res = jnp.dot(f_ref[...], x, preferred_element_type=F32)
    half = r * N1_BLOCK
    o_ref[...] = _pack_complex(res[:half], res[half:]).reshape(r, N1_BLOCK, n_ch)


def _fft_stage1(x4, part, f1, r, n_ch):
    nb = x4.shape[1]
    xv = x4.reshape(x4.shape[0], nb, r // 2, r, n_ch)
    return pl.pallas_call(
        functools.partial(_fft1_kernel, r=r),
        grid=(nb, r // N1_BLOCK),
        in_specs=[pl.BlockSpec(f1.shape, lambda b, j: (0, 0)),
                  pl.BlockSpec((None, None, r // 2, N1_BLOCK, n_ch), lambda b, j: (part, b, 0, j, 0))],
        out_specs=pl.BlockSpec((None, r, N1_BLOCK, n_ch), lambda b, j: (b, 0, j, 0)),
        out_shape=jax.ShapeDtypeStruct((nb, r, r, n_ch), U32),
        compiler_params=_cparams(("parallel", "parallel")),
        name="hyena_dft_stage1",
    )(f1, xv)


def _k2_block(r):
    return max(1, (2 * LANES) // r)


def _filter_spec_kernel(mf_ref, bf_ref, bb_ref, o_ref, *, r):
    for kk in range(mf_ref.shape[0]):
        m = mf_ref[kk]
        xf = jnp.dot(m, _unpack_complex_rows(bf_ref[kk]), preferred_element_type=F32)
        xb = jnp.dot(m, _unpack_complex_rows(bb_ref[kk]), preferred_element_type=F32)
        o_ref[kk] = _pack_complex(xf[:r] + xb[:r], xf[r:] - xb[r:])


def _filter_spectrum(b1, mf, r, n_ch):
    kb = _k2_block(r)
    blk = lambda sel: pl.BlockSpec((None, kb, r, n_ch), lambda o, k: (2 * o + sel, k, 0, 0))
    return pl.pallas_call(
        functools.partial(_filter_spec_kernel, r=r),
        grid=(HYENA_ORDER, r // kb),
        in_specs=[pl.BlockSpec((kb, 2 * r, 2 * r), lambda o, k: (k, 0, 0)), blk(0), blk(1)],
        out_specs=pl.BlockSpec((None, kb, r, n_ch), lambda o, k: (o, k, 0, 0)),
        out_shape=jax.ShapeDtypeStruct((HYENA_ORDER, r, r, n_ch), U32),
        compiler_params=_cparams(("parallel", "parallel")),
        name="hyena_filter_spectrum",
    )(mf, b1, b1)


def _fft2_kernel(mf_ref, mi_ref, b_ref, h_ref, o_ref, *, r):
    for kk in range(mf_ref.shape[0]):
        x = jnp.dot(mf_ref[kk], _unpack_complex_rows(b_ref[kk]), preferred_element_type=F32)
        xr, xi = x[:r], x[r:]
        hr, hi = _unpack(h_ref[kk])
        y = jnp.concatenate([xr * hr - xi * hi, xr * hi + xi * hr], axis=0).astype(BF16)
        c = jnp.dot(mi_ref[kk], y, preferred_element_type=F32)
        o_ref[kk] = _pack_complex(c[:r], c[r:])


def _fft_stage2(b1, h, order, mf, minv, r, n_ch):
    nb = b1.shape[0]
    kb = _k2_block(r)
    blk = pl.BlockSpec((None, kb, r, n_ch), lambda k, b: (b, k, 0, 0))
    return pl.pallas_call(
        functools.partial(_fft2_kernel, r=r),
        grid=(r // kb, nb),
        in_specs=[pl.BlockSpec((kb, 2 * r, 2 * r), lambda k, b: (k, 0, 0)),
                  pl.BlockSpec((kb, 2 * r, 2 * r), lambda k, b: (k, 0, 0)),
                  blk,
                  pl.BlockSpec((None, kb, r, n_ch), lambda k, b: (order, k, 0, 0))],
        out_specs=blk,
        out_shape=jax.ShapeDtypeStruct((nb, r, r, n_ch), U32),
        compiler_params=_cparams(("parallel", "parallel")),
        name="hyena_dft_stage2",
    )(mf, minv, b1, h)


def _fft3_kernel(g_ref, c_ref, gate_ref, s_ref, bias_ref, ng_ref, *rest, final):
    if final:
        (o_ref,) = rest
    else:
        f_ref, o_ref, b_ref = rest
    r, nb, n_ch = c_ref.shape
    cc = _unpack_complex_rows(c_ref[...].reshape(r * nb, n_ch))
    y = jnp.dot(g_ref[...], cc, preferred_element_type=F32)
    rows = (r // 2) * nb
    s_new = gate_ref[...].reshape(rows, n_ch) * (y + s_ref[...].reshape(rows, n_ch) * bias_ref[...])
    if final:
        ms = jnp.mean(s_new * s_new, axis=-1, keepdims=True)
        s_new = s_new * lax.rsqrt(ms + RMS_EPS) * ng_ref[...]
    o_ref[...] = s_new.reshape(r // 2, nb, n_ch)
    if not final:
        res = jnp.dot(f_ref[...], s_new.astype(BF16), preferred_element_type=F32)
        half = r * nb
        b_ref[...] = _pack_complex(res[:half], res[half:]).reshape(r, nb, n_ch)


def _fft_stage3(c2, g3, z4, gate_part, s4, s_part, bias, norm_g, f1, r, n_ch):
    final = f1 is None
    nb = c2.shape[0]
    zv = z4.reshape(z4.shape[0], nb, r // 2, r, n_ch)
    sv = s4.reshape(s4.shape[0], nb, r // 2, r, n_ch)
    dspec = lambda part: pl.BlockSpec((None, None, r // 2, N1_BLOCK, n_ch), lambda b, j: (part, b, 0, j, 0))
    packed = pl.BlockSpec((None, r, N1_BLOCK, n_ch), lambda b, j: (b, 0, j, 0))
    s_spec = pl.BlockSpec((None, r // 2, N1_BLOCK, n_ch), lambda b, j: (b, 0, j, 0))
    s_shape = jax.ShapeDtypeStruct((nb, r // 2, r, n_ch), F32)
    in_specs = [pl.BlockSpec(g3.shape, lambda b, j: (0, 0)), packed, dspec(gate_part), dspec(s_part),
                pl.BlockSpec((1, n_ch), lambda b, j: (0, 0)), pl.BlockSpec((1, n_ch), lambda b, j: (0, 0))]
    args = [g3, c2, zv, sv, bias.reshape(1, n_ch).astype(F32), norm_g.reshape(1, n_ch).astype(F32)]
    if final:
        out_specs, out_shape = s_spec, s_shape
    else:
        in_specs.append(pl.BlockSpec(f1.shape, lambda b, j: (0, 0)))
        args.append(f1)
        out_specs = [s_spec, packed]
        out_shape = [s_shape, jax.ShapeDtypeStruct((nb, r, r, n_ch), U32)]
    return pl.pallas_call(
        functools.partial(_fft3_kernel, final=final),
        grid=(nb, r // N1_BLOCK),
        in_specs=in_specs,
        out_specs=out_specs,
        out_shape=out_shape,
        compiler_params=_cparams(("parallel", "parallel")),
        name="hyena_dft_stage3",
    )(*args)


def _hyena(zh, row_off, batch, seq, conv_w, conv_b, filt, fft_bias, out_g):
    n_ch = zh.shape[1] // 3
    r = int(round(math.sqrt(2 * seq)))
    assert r * r == 2 * seq and r % 16 == 0
    f1, mf, minv, g3 = _dft_tables(r)
    sig = _hyena_filter_signals(seq, *filt, n_ch=n_ch)
    hb1 = _fft_stage1(sig[None], 0, f1, r, n_ch)
    h = _filter_spectrum(hb1, mf, r, n_ch)
    z4 = _shortconv(zh, row_off, conv_w, conv_b, batch, seq)
    s4, s_part = z4, 2
    b1 = _fft_stage1(s4, s_part, f1, r, n_ch)
    for o in range(HYENA_ORDER):
        c2 = _fft_stage2(b1, h, o, mf, minv, r, n_ch)
        if o < HYENA_ORDER - 1:
            s, b1 = _fft_stage3(c2, g3, z4, o, s4, s_part, fft_bias[o], out_g, f1, r, n_ch)
        else:
            s = _fft_stage3(c2, g3, z4, o, s4, s_part, fft_bias[o], out_g, None, r, n_ch)
        s4, s_part = s.reshape(1, batch, seq, n_ch), 0
    return s4.reshape(batch * seq, n_ch)


def _extract_top(s, key, count):
    vals, keys = [], []
    for _ in range(count):
        m = jnp.max(s, axis=0, keepdims=True)
        kmin = jnp.min(jnp.where(s == m, key, jnp.inf), axis=0, keepdims=True)
        s = jnp.where(key == kmin, -jnp.inf, s)
        vals.append(m)
        keys.append(kmin)
    return vals, keys


def _peer_topk_kernel(q_ref, keys_ref, eid_ref, gate_ref):
    t = q_ref.shape[0]
    nk = PEER_NKEYS
    q = q_ref[...]
    row_key = lax.broadcasted_iota(I32, (nk, t), 0).astype(F32)
    tops = []
    for c in range(2):
        s = lax.dot_general(keys_ref[c], q[:, c * nk:(c + 1) * nk], (((1,), (1,)), ((), ())),
                            preferred_element_type=F32)
        tops.append(_extract_top(s, row_key, PEER_TOPK))
    (v1, i1), (v2, i2) = tops
    rows16 = lax.broadcasted_iota(I32, (PEER_TOPK, t), 0)
    v2a = jnp.zeros((PEER_TOPK, t), F32)
    i2a = jnp.zeros((PEER_TOPK, t), F32)
    for j in range(PEER_TOPK):
        v2a = jnp.where(rows16 == j, v2[j], v2a)
        i2a = jnp.where(rows16 == j, i2[j], i2a)
    n_exp = float(nk * nk)
    half = PEER_TOPK // 2
    rows8 = lax.broadcasted_iota(I32, (half, t), 0)
    pos8 = rows8.astype(F32)
    v2h, i2h = v2a[:half], i2a[:half]
    cand = [v1[0] + v2a]
    ckey = [rows16.astype(F32) * n_exp + (i1[0] * float(nk) + i2a)]
    for i in range(1, half):
        cand.append(jnp.where(rows8 < PEER_TOPK // (i + 1), v1[i] + v2h, -jnp.inf))
        ckey.append((pos8 + float(i * PEER_TOPK)) * n_exp + (i1[i] * float(nk) + i2h))
    v1t = jnp.zeros((half, t), F32)
    i1t = jnp.zeros((half, t), F32)
    for r in range(half):
        v1t = jnp.where(rows8 == r, v1[half + r], v1t)
        i1t = jnp.where(rows8 == r, i1[half + r], i1t)
    cand.append(v1t + v2[0])
    ckey.append((pos8 + float(half)) * (PEER_TOPK * n_exp) + (i1t * float(nk) + i2[0]))
    tv, tk_ = _extract_top(jnp.concatenate(cand, axis=0), jnp.concatenate(ckey, axis=0), PEER_TOPK)
    denom = jnp.zeros((1, t), F32)
    es = []
    for k in range(PEER_TOPK):
        e = jnp.exp(tv[k] - tv[0])
        es.append(e)
        denom = denom + e
    eid = jnp.zeros((PEER_TOPK, t), F32)
    gate = jnp.zeros((PEER_TOPK, t), F32)
    for k in range(PEER_TOPK):
        pos = jnp.floor(tk_[k] * (1.0 / n_exp))
        eid = jnp.where(rows16 == k, tk_[k] - pos * n_exp, eid)
        gate = jnp.where(rows16 == k, es[k] / denom, gate)
    eid_ref[...] = eid.astype(I32)
    gate_ref[...] = gate


def _peer_topk(q, keys, tm=512):
    t = q.shape[0]
    tm = min(tm, t)
    out_spec = pl.BlockSpec((PEER_TOPK, tm), lambda i, h: (h, i))
    return pl.pallas_call(
        _peer_topk_kernel,
        grid=(t // tm, PEER_HEADS),
        in_specs=[pl.BlockSpec((tm, 2 * PEER_NKEYS), lambda i, h: (i, h)),
                  pl.BlockSpec((None, 2, PEER_NKEYS, PEER_NKEYS), lambda i, h: (h, 0, 0, 0))],
        out_specs=[out_spec, out_spec],
        out_shape=[jax.ShapeDtypeStruct((PEER_HEADS * PEER_TOPK, t), I32),
                   jax.ShapeDtypeStruct((PEER_HEADS * PEER_TOPK, t), F32)],
        compiler_params=_cparams(("parallel", "parallel")),
        name="peer_topk",
    )(q, keys)


def _pack_table(tab):
    e, d = tab.shape
    assert d == 2 * SUBLANES * LANES
    bits = lax.bitcast_convert_type(tab.astype(BF16), jnp.uint16).astype(U32)
    packed = bits[:, :d // 2] | (bits[:, d // 2:] << 16)
    return packed.reshape(e * SUBLANES, LANES)


def _unpack(w):
    lo = lax.bitcast_convert_type(w << 16, F32)
    hi = lax.bitcast_convert_type(w & jnp.uint32(0xFFFF0000), F32)
    return lo, hi


_BITREV8 = (0, 4, 2, 6, 1, 5, 3, 7)


def _sublane_fold8(parts):
    sub = lax.broadcasted_iota(I32, (2 * SUBLANES, LANES), 0) // 2

    def rolled(a, shift):
        return pltpu.bitcast(pltpu.roll(pltpu.bitcast(a, U32), shift, axis=0), BF16)

    lvl = [parts[_BITREV8[r]] for r in range(8)]
    for shift, mask in ((4, sub < 4), (2, (sub % 4) < 2), (1, (sub % 2) < 1)):
        nxt = []
        for a, b in zip(lvl[0::2], lvl[1::2]):
            nxt.append(jnp.where(mask, a + rolled(a, SUBLANES - shift), b + rolled(b, shift)))
        lvl = nxt
    return lvl[0]


def _pair_ranges(half):
    if half == 0:
        return (0, PEER_WINDOW), (PEER_WINDOW, PEER_PAIRS)
    return (PEER_PAIRS - PEER_WINDOW, PEER_PAIRS), (0, PEER_PAIRS - PEER_WINDOW)


def _overflow(n_low, half):
    return n_low > PEER_WINDOW if half == 0 else n_low < PEER_PAIRS - PEER_WINDOW


def _table_rows(tab_ref, off):
    return _unpack(tab_ref[pl.ds(pl.multiple_of(off, SUBLANES), SUBLANES), :])


def _peer_u_kernel(off_ref, nlow_ref, tab_ref, x_ref, o_ref, extra_ref, *, tm, half):
    lane = lax.broadcasted_iota(I32, (SUBLANES, LANES), 1)
    sub = lax.broadcasted_iota(I32, (SUBLANES, LANES), 0)
    lane_grp = lax.shift_right_logical(lane, 3)
    diag = sub == (lane & (SUBLANES - 1))
    main, rest = _pair_ranges(half)

    def folded(t, p0, p1):
        xb = pltpu.bitcast(x_ref[t], BF16)
        out = []
        for g in range(p0 // SUBLANES, p1 // SUBLANES):
            parts = []
            for r in range(SUBLANES):
                off = pl.multiple_of(off_ref[t, g * SUBLANES + r], SUBLANES)
                parts.append(pltpu.bitcast(tab_ref[pl.ds(off, SUBLANES), :], BF16) * xb)
            out.append(pltpu.bitcast(_sublane_fold8(parts), U32))
        return out

    def lane_sums(folds, p0):
        mat = jnp.zeros((SUBLANES, LANES), F32)
        for i, f in enumerate(folds):
            lo, hi = _unpack(f)
            mat = jnp.where(lane_grp == p0 // SUBLANES + i, jnp.sum(lo + hi, axis=-1, keepdims=True), mat)
        return jnp.sum(jnp.where(diag, mat, 0.0), axis=0, keepdims=True)

    def finish(t, folds):
        o_ref[pl.ds(t, 1), :] = lane_sums(folds, main[0]) + extra_ref[pl.ds(t, 1), :]

    def token(t, prev):
        cur = folded(t, *main)
        finish(jnp.maximum(t - 1, 0), prev)
        extra_ref[pl.ds(t, 1), :] = jnp.zeros((1, LANES), F32)

        @pl.when(_overflow(nlow_ref[0, t], half))
        def _():
            extra_ref[pl.ds(t, 1), :] = lane_sums(folded(t, *rest), rest[0])

        return tuple(cur)

    extra_ref[pl.ds(0, 1), :] = jnp.zeros((1, LANES), F32)
    zeros = tuple(jnp.zeros((SUBLANES, LANES), U32) for _ in range((main[1] - main[0]) // SUBLANES))
    last = lax.fori_loop(0, tm, token, zeros)
    finish(tm - 1, last)


def _peer_u(off, nlow3, tab, x4, half, tm=128):
    t = off.shape[0]
    rows = PEER_HALF * SUBLANES
    return pl.pallas_call(
        functools.partial(_peer_u_kernel, tm=tm, half=half),
        grid=(t // tm,),
        in_specs=[pl.BlockSpec((tm, LANES), lambda i: (i, 0), memory_space=pltpu.SMEM),
                  pl.BlockSpec((None, 1, tm), lambda i: (i, 0, 0), memory_space=pltpu.SMEM),
                  pl.BlockSpec((rows, LANES), lambda i: (half, 0), pipeline_mode=pl.Buffered(1)),
                  pl.BlockSpec((tm, SUBLANES, LANES), lambda i: (i, 0, 0))],
        out_specs=pl.BlockSpec((tm, LANES), lambda i: (i, 0)),
        out_shape=jax.ShapeDtypeStruct((t, LANES), F32),
        scratch_shapes=[pltpu.VMEM((tm, LANES), F32)],
        compiler_params=_cparams(("arbitrary",)),
        name="peer_expert_scores",
    )(off, nlow3, tab, x4)


def _peer_coef_kernel(a0_ref, a1_ref, eid_ref, gate_ref, c0_ref, c1_ref):
    low = eid_ref[...] < PEER_HALF
    a = jnp.where(low, a0_ref[...], a1_ref[...])
    coef = gate_ref[...] * (0.5 * a * (1.0 + lax.erf(a * (1.0 / math.sqrt(2.0)))))
    c0_ref[...] = jnp.where(low, coef, 0.0)
    c1_ref[...] = jnp.where(low, 0.0, coef)


def _peer_coef(a0, a1, eid, gate, tm=1024):
    t = eid.shape[0]
    tm = min(tm, t)
    spec = pl.BlockSpec((tm, LANES), lambda i: (i, 0))
    return pl.pallas_call(
        _peer_coef_kernel,
        grid=(t // tm,),
        in_specs=[spec, spec, spec, spec],
        out_specs=[spec, spec],
        out_shape=[jax.ShapeDtypeStruct((t, LANES), F32)] * 2,
        compiler_params=_cparams(("parallel",)),
        name="peer_coef",
    )(a0, a1, eid, gate)


def _peer_v_kernel(off_ref, nlow_ref, coef_ref, tab_ref, base_ref, *o_refs, tm, half, n_first):
    n_acc = 4
    main, rest = _pair_ranges(half)

    def weighted(t, p0, p1):
        acc_lo = [jnp.zeros((SUBLANES, LANES), F32) for _ in range(n_acc)]
        acc_hi = [jnp.zeros((SUBLANES, LANES), F32) for _ in range(n_acc)]
        for p in range(p0, p1):
            c = coef_ref[t, p]
            lo, hi = _table_rows(tab_ref, off_ref[t, p])
            acc_lo[p % n_acc] = acc_lo[p % n_acc] + c * lo
            acc_hi[p % n_acc] = acc_hi[p % n_acc] + c * hi
        return ((acc_lo[0] + acc_lo[1]) + (acc_lo[2] + acc_lo[3]),
                (acc_hi[0] + acc_hi[1]) + (acc_hi[2] + acc_hi[3]))

    def run(o_ref):
        def store_row(t, lo, hi):
            o_ref[t, 0] = base_ref[t, 0] + lo
            o_ref[t, 1] = base_ref[t, 1] + hi

        def token(t, carry):
            lo, hi = weighted(t, *main)
            store_row(t, lo, hi)

            @pl.when(_overflow(nlow_ref[0, t], half))
            def _():
                lo2, hi2 = weighted(t, *rest)
                store_row(t, lo + lo2, hi + hi2)

            return carry

        lax.fori_loop(0, tm, token, 0)

    if len(o_refs) == 1:
        run(o_refs[0])
    else:
        first = pl.program_id(0) < n_first
        pl.when(first)(lambda: run(o_refs[0]))
        pl.when(jnp.logical_not(first))(lambda: run(o_refs[1]))


def _peer_v(off, nlow3, coef, tab, base, half, tm=128, split=None):
    t = base.shape[0]
    rows = PEER_HALF * SUBLANES
    smem = pl.BlockSpec((tm, LANES), lambda i: (i, 0), memory_space=pltpu.SMEM)
    tile = pl.BlockSpec((tm, 2, SUBLANES, LANES), lambda i: (i, 0, 0, 0))
    if split is None:
        n_first, out_specs = 0, tile
        out_shape = jax.ShapeDtypeStruct((t, 2, SUBLANES, LANES), F32)
    else:
        assert split % tm == 0
        n_first, n_rest = split // tm, (t - split) // tm
        blk = (tm, 2, SUBLANES, LANES)
        out_specs = [pl.BlockSpec(blk, lambda i: (jnp.clip(i, 0, n_first - 1), 0, 0, 0)),
                     pl.BlockSpec(blk, lambda i: (jnp.clip(i - n_first, 0, n_rest - 1), 0, 0, 0))]
        out_shape = [jax.ShapeDtypeStruct((split, 2, SUBLANES, LANES), F32),
                     jax.ShapeDtypeStruct((t - split, 2, SUBLANES, LANES), F32)]
    return pl.pallas_call(
        functools.partial(_peer_v_kernel, tm=tm, half=half, n_first=n_first),
        grid=(t // tm,),
        in_specs=[smem,
                  pl.BlockSpec((None, 1, tm), lambda i: (i, 0, 0), memory_space=pltpu.SMEM),
                  smem,
                  pl.BlockSpec((rows, LANES), lambda i: (half, 0), pipeline_mode=pl.Buffered(1)),
                  tile],
        out_specs=out_specs,
        out_shape=out_shape,
        compiler_params=_cparams(("arbitrary",)),
        name="peer_expert_sum",
    )(off, nlow3, coef, tab, base)


def _peer(x1, split, norm2_g, wq_bf, keys_bf, u_packed, v_packed, tm=256):
    t, d = x1.shape
    tm = min(tm, t)
    (xn_bf,) = _rmsnorm(x1, norm2_g, (BF16,))
    q = _matmul(xn_bf, wq_bf, wq_bf.shape[1], 0, BF16)
    eid_t, gate_t = _peer_topk(q, keys_bf)
    eid, gate = eid_t.T, gate_t.T
    n_exp = PEER_NKEYS * PEER_NKEYS
    high = (eid >= PEER_HALF).astype(I32)
    nlow3 = (PEER_PAIRS - jnp.sum(high, axis=-1, dtype=I32)).reshape(t // tm, 1, tm)
    slot = lax.broadcasted_iota(I32, eid.shape, 1)
    order = jnp.sort((high * PEER_PAIRS + slot) * n_exp + eid, axis=-1)
    eid = order % n_exp
    gate = jnp.take_along_axis(gate, (order // n_exp) % PEER_PAIRS, axis=-1)
    off = (eid & (PEER_HALF - 1)) * SUBLANES
    xw = _pack_table(xn_bf).reshape(t, SUBLANES, LANES)
    a0 = _peer_u(off, nlow3, u_packed, xw, 0, tm)
    a1 = _peer_u(off, nlow3, u_packed, xw, 1, tm)
    c0, c1 = _peer_coef(a0, a1, eid, gate)
    y = _peer_v(off, nlow3, c0, v_packed, x1.reshape(t, 2, SUBLANES, LANES), 0, tm)
    ya, yb = _peer_v(off, nlow3, c1, v_packed, y, 1, tm, split=split)
    return ya.reshape(split, d), yb.reshape(t - split, d)


def kernel(x_prompt, x_sample, norm1_g, w_in, q_norm_g, k_norm_g, lambda_q1, lambda_k1, lambda_q2, lambda_k2, attn_sub_g, conv_w, conv_b, filt_w1, filt_b1, filt_w2, filt_b2, filt_w3, filt_b3, filt_w4, filt_freq, fft_bias, hyena_out_g, w_out, norm2_g, peer_wq, peer_keys, peer_u, peer_v):
    depth = w_in.shape[0]
    d_model = x_prompt.shape[-1]
    att_w = ATT_HEADS * ATT_VDIM
    shapes = [x_prompt.shape[:2], x_sample.shape[:2]]
    xs = [x_prompt.reshape(-1, d_model), x_sample.reshape(-1, d_model)]
    n0 = xs[0].shape[0]
    slopes = 2.0 ** (-8.0 * jnp.arange(1, ATT_HEADS + 1, dtype=F32) / ATT_HEADS)

    for l in range(depth):
        lambda_init = 0.8 - 0.6 * math.exp(-0.3 * l)
        lam = (jnp.exp(jnp.sum(lambda_q1[l].astype(F32) * lambda_k1[l].astype(F32)))
               - jnp.exp(jnp.sum(lambda_q2[l].astype(F32) * lambda_k2[l].astype(F32)))
               + lambda_init).reshape(1)
        w_in_bf = w_in[l].astype(BF16)
        q_gain = jnp.tile(q_norm_g[l].astype(F32), 2 * ATT_HEADS) * (ATT_QKDIM ** -0.5 * LOG2E)
        k_gain = jnp.tile(k_norm_g[l].astype(F32), 2 * ATT_HEADS)
        qk_gain = jnp.concatenate([q_gain, k_gain]).reshape(1, 2 * att_w)

        h_bf = _rmsnorm_stacked(xs[0], xs[1], norm1_g[l], BF16)
        qk = _matmul(h_bf, w_in_bf, 2 * att_w, 0, BF16, mode="qknorm", extra=qk_gain)
        vt = _matmul_nt(w_in_bf[:, 2 * att_w:3 * att_w].T, h_bf, BF16)
        zh = _matmul(h_bf, w_in_bf, w_in.shape[2] - 3 * att_w, 3 * att_w, F32)

        filt = (filt_w1[l], filt_b1[l], filt_w2[l], filt_b2[l], filt_w3[l], filt_b3[l], filt_w4[l], filt_freq[l])
        segs, row = [], 0
        for (b, s), x_seg in zip(shapes, xs):
            att = _attention(qk, vt, row, slopes, lam, attn_sub_g[l], b, s, 1.0 - lambda_init)
            hy = _hyena(zh, row, b, s, conv_w[l], conv_b[l], filt, fft_bias[l], hyena_out_g[l])
            segs.append((att, hy, x_seg))
            row += b * s
        x1 = _outproj_stacked(segs[0], segs[1], w_out[l].astype(BF16))

        xs = _peer(x1, n0, norm2_g[l], peer_wq[l].astype(BF16), peer_keys[l].astype(BF16),
                   _pack_table(peer_u[l]), _pack_table(peer_v[l]))

    return (xs[0].reshape(x_prompt.shape), xs[1].reshape(x_sample.shape))
```

```python
import functools
import math

import jax
import jax.numpy as jnp
from jax import lax
from jax.experimental import pallas as pl
from jax.experimental.pallas import tpu as pltpu

F32 = jnp.float32
BF16 = jnp.bfloat16
I32 = jnp.int32
U32 = jnp.uint32

RMS_EPS = 1e-6
LOG2E = 1.4426950408889634
LANES = 128
SUBLANES = 8
VMEM_LIMIT_BYTES = 56 * 1024 * 1024

ATT_HEADS = 8
ATT_VDIM = 128
ATT_QKDIM = 64
HYENA_ORDER = 2
FILTER_BANDS = 16
DECAY_FAST = 0.3
DECAY_SLOW = 1.5
DECAY_TARGET = 1e-2
PEER_HEADS = 8
PEER_NKEYS = 128
PEER_TOPK = 16
PEER_HALF = PEER_NKEYS * PEER_NKEYS // 2
PEER_PAIRS = PEER_HEADS * PEER_TOPK
PEER_WINDOW = 80


def _cparams(sem, vmem=VMEM_LIMIT_BYTES):
    return pltpu.CompilerParams(dimension_semantics=sem, vmem_limit_bytes=vmem)


def _segment_spec(block, start, count, minor=0):
    return pl.BlockSpec(block, lambda i: (jnp.clip(i - start, 0, count - 1), minor))


def _rmsnorm_kernel(x_ref, g_ref, *o_refs):
    x = x_ref[...]
    ms = jnp.mean(x * x, axis=-1, keepdims=True)
    y = x * lax.rsqrt(ms + RMS_EPS) * g_ref[...]
    for o_ref in o_refs:
        o_ref[...] = y.astype(o_ref.dtype)


def _rmsnorm2_kernel(xa_ref, xb_ref, g_ref, o_ref, *, na):
    first = pl.program_id(0) < na
    x = jnp.where(first, xa_ref[...], xb_ref[...])
    ms = jnp.mean(x * x, axis=-1, keepdims=True)
    o_ref[...] = (x * lax.rsqrt(ms + RMS_EPS) * g_ref[...]).astype(o_ref.dtype)


def _rmsnorm_stacked(xa, xb, g, out_dtype, tm=512):
    d = xa.shape[1]
    na, nb = xa.shape[0] // tm, xb.shape[0] // tm
    assert xa.shape[0] % tm == 0 and xb.shape[0] % tm == 0
    return pl.pallas_call(
        functools.partial(_rmsnorm2_kernel, na=na),
        grid=(na + nb,),
        in_specs=[_segment_spec((tm, d), 0, na), _segment_spec((tm, d), na, nb),
                  pl.BlockSpec((1, d), lambda i: (0, 0))],
        out_specs=pl.BlockSpec((tm, d), lambda i: (i, 0)),
        out_shape=jax.ShapeDtypeStruct((xa.shape[0] + xb.shape[0], d), out_dtype),
        compiler_params=_cparams(("parallel",)),
        name="rmsnorm_stacked",
    )(xa, xb, g.reshape(1, d).astype(F32))


def _rmsnorm(x, g, out_dtypes, tm=512):
    t, d = x.shape
    tm = min(tm, t)
    spec = pl.BlockSpec((tm, d), lambda i: (i, 0))
    return pl.pallas_call(
        _rmsnorm_kernel,
        grid=(t // tm,),
        in_specs=[spec, pl.BlockSpec((1, d), lambda i: (0, 0))],
        out_specs=[spec for _ in out_dtypes],
        out_shape=[jax.ShapeDtypeStruct((t, d), dt) for dt in out_dtypes],
        compiler_params=_cparams(("parallel",)),
        name="rmsnorm",
    )(x, g.reshape(1, d).astype(F32))


def _group_rms_scale(x, gain):
    lane = lax.broadcasted_iota(I32, x.shape, 1)
    lo = lane < ATT_QKDIM
    x2 = x * x
    s_lo = jnp.sum(jnp.where(lo, x2, 0.0), axis=-1, keepdims=True)
    s_hi = jnp.sum(jnp.where(lo, 0.0, x2), axis=-1, keepdims=True)
    ms = jnp.where(lo, s_lo, s_hi) * (1.0 / ATT_QKDIM)
    return x * lax.rsqrt(ms + RMS_EPS) * gain


def _mm_kernel(a_ref, b_ref, *rest, mode):
    acc = jnp.dot(a_ref[...], b_ref[...], preferred_element_type=F32)
    if mode == "plain":
        (o_ref,) = rest
        o_ref[...] = acc.astype(o_ref.dtype)
    elif mode == "residual":
        r_ref, o_ref = rest
        o_ref[...] = (acc + r_ref[...]).astype(o_ref.dtype)
    elif mode == "qknorm":
        g_ref, o_ref = rest
        for c in range(acc.shape[1] // LANES):
            sl = slice(c * LANES, (c + 1) * LANES)
            o_ref[:, sl] = _group_rms_scale(acc[:, sl], g_ref[:, sl]).astype(o_ref.dtype)
    else:
        raise ValueError(mode)


def _matmul(a, b, n_cols, col_off, out_dtype, mode="plain", extra=None, tm=1024, tn=1024):
    m, k = a.shape
    tm = min(tm, m)
    tn = min(tn, n_cols)
    assert col_off % tn == 0 and n_cols % tn == 0 and m % tm == 0
    off = col_off // tn
    in_specs = [pl.BlockSpec((tm, k), lambda i, j: (i, 0)),
                pl.BlockSpec((k, tn), lambda i, j: (0, j + off))]
    args = [a, b]
    if mode == "residual":
        in_specs.append(pl.BlockSpec((tm, tn), lambda i, j: (i, j)))
        args.append(extra)
    elif mode == "qknorm":
        in_specs.append(pl.BlockSpec((1, tn), lambda i, j: (0, j)))
        args.append(extra)
    return pl.pallas_call(
        functools.partial(_mm_kernel, mode=mode),
        grid=(m // tm, n_cols // tn),
        in_specs=in_specs,
        out_specs=pl.BlockSpec((tm, tn), lambda i, j: (i, j)),
        out_shape=jax.ShapeDtypeStruct((m, n_cols), out_dtype),
        compiler_params=_cparams(("parallel", "arbitrary")),
        name="matmul_" + mode,
    )(*args)


def _outproj_kernel(att_a, hy_a, x_a, att_b, hy_b, x_b, w_ref, o_ref, *, na):
    kw = att_a.shape[1]

    def run(att_ref, hy_ref, x_ref):
        acc = jnp.dot(att_ref[...], w_ref[:kw, :], preferred_element_type=F32)
        acc = acc + jnp.dot(hy_ref[...].astype(BF16), w_ref[kw:, :], preferred_element_type=F32)
        o_ref[...] = acc + x_ref[...]

    first = pl.program_id(0) < na
    pl.when(first)(lambda: run(att_a, hy_a, x_a))
    pl.when(jnp.logical_not(first))(lambda: run(att_b, hy_b, x_b))


def _outproj_stacked(seg_a, seg_b, w, tm=512, tn=1024):
    n_out = w.shape[1]
    na, nb = seg_a[0].shape[0] // tm, seg_b[0].shape[0] // tm
    assert seg_a[0].shape[0] % tm == 0 and seg_b[0].shape[0] % tm == 0 and n_out % tn == 0

    def specs(seg, start, count):
        att, hy, x = seg
        row = lambda i, j: (jnp.clip(i - start, 0, count - 1), 0)
        return [pl.BlockSpec((tm, att.shape[1]), row), pl.BlockSpec((tm, hy.shape[1]), row),
                pl.BlockSpec((tm, tn), lambda i, j: (jnp.clip(i - start, 0, count - 1), j))]

    return pl.pallas_call(
        functools.partial(_outproj_kernel, na=na),
        grid=(na + nb, n_out // tn),
        in_specs=specs(seg_a, 0, na) + specs(seg_b, na, nb) + [pl.BlockSpec((w.shape[0], tn), lambda i, j: (0, j))],
        out_specs=pl.BlockSpec((tm, tn), lambda i, j: (i, j)),
        out_shape=jax.ShapeDtypeStruct(((na + nb) * tm, n_out), F32),
        compiler_params=_cparams(("parallel", "arbitrary")),
        name="out_projection",
    )(*seg_a, *seg_b, w)


def _mm_nt_kernel(w_ref, a_ref, o_ref):
    o_ref[...] = lax.dot_general(w_ref[...], a_ref[...], (((1,), (1,)), ((), ())),
                                 preferred_element_type=F32).astype(o_ref.dtype)


def _matmul_nt(wt, a, out_dtype, tm=1024):
    n, k = wt.shape
    m = a.shape[0]
    tm = min(tm, m)
    return pl.pallas_call(
        _mm_nt_kernel,
        grid=(m // tm,),
        in_specs=[pl.BlockSpec((n, k), lambda i: (0, 0)), pl.BlockSpec((tm, k), lambda i: (i, 0))],
        out_specs=pl.BlockSpec((n, tm), lambda i: (0, i)),
        out_shape=jax.ShapeDtypeStruct((n, m), out_dtype),
        compiler_params=_cparams(("parallel",)),
        name="matmul_nt",
    )(wt, a)


def _alibi_columns(slopes, tq, tk):
    return [_alibi_side(slopes, tq, True), _alibi_side(slopes, tk, False)]


def _alibi_side(slopes, n, query_side):
    pos = jnp.arange(n, dtype=F32)
    val = (slopes.astype(F32) * LOG2E)[:, None] * pos[None, :]

    def pieces(x):
        p1 = x.astype(BF16)
        r1 = x - p1.astype(F32)
        p2 = r1.astype(BF16)
        p3 = (r1 - p2.astype(F32)).astype(BF16)
        return [p1, p2, p3]

    ones = [jnp.ones_like(val, BF16)] * 3
    six = jnp.stack(pieces(-val) + ones if query_side else ones + pieces(val), axis=-1)
    pad = jnp.zeros(val.shape + (ATT_QKDIM - 6,), BF16)
    return jnp.concatenate([six, pad, six, pad], axis=-1)


def _attn_kernel(slope_ref, lam_ref, q_ref, k_ref, vt_ref, aq_ref, ak_ref, g_ref, o_ref,
                 m_ref, l_ref, acc_ref, *, tq, tk, nk, hb, out_scale):
    hg = pl.program_id(1)
    i = pl.program_id(2)
    j = pl.program_id(3)

    @pl.when(j == 0)
    def _():
        m_ref[...] = jnp.full(m_ref.shape, -jnp.inf, F32)
        l_ref[...] = jnp.zeros(l_ref.shape, F32)
        acc_ref[...] = jnp.zeros(acc_ref.shape, F32)

    q_first = lax.broadcasted_iota(I32, (tq, LANES), 1) < ATT_QKDIM
    k_first = lax.broadcasted_iota(I32, (tk, LANES), 1) < ATT_QKDIM
    nt = (((1,), (1,)), ((), ()))

    def update(hh, scores, shift):
        vt = vt_ref[hh * LANES:(hh + 1) * LANES, :]
        for c, s in enumerate(scores):
            m_old = m_ref[hh, c]
            m_new = jnp.maximum(m_old, jnp.max(s, axis=0, keepdims=True) + shift)
            alpha = jnp.exp2(m_old - m_new)
            p = jnp.exp2(s - (m_new - shift))
            l_ref[hh, c] = alpha * l_ref[hh, c] + jnp.sum(p, axis=0, keepdims=True)
            acc_ref[hh, c] = alpha * acc_ref[hh, c] + jnp.dot(vt, p.astype(BF16),
                                                              preferred_element_type=F32)
            m_ref[hh, c] = m_new

    keys_before = i * tq >= (j + 1) * tk
    keys_after = (i + 1) * tq <= j * tk
    off_diagonal = jnp.logical_or(keys_before, keys_after)

    @pl.when(off_diagonal)
    def _():
        sign = jnp.where(keys_after, -1.0, 1.0).astype(BF16)
        gap = jnp.abs(i * tq - j * tk).astype(F32)
        for hh in range(hb):
            sl = slice(hh * LANES, (hh + 1) * LANES)
            q = q_ref[:, sl]
            k = k_ref[:, sl]
            aq = aq_ref[hh]
            ak = ak_ref[hh] * sign
            s0 = lax.dot_general(jnp.where(k_first, k, ak), jnp.where(q_first, q, aq), nt,
                                 preferred_element_type=F32)
            s1 = lax.dot_general(jnp.where(k_first, ak, k), jnp.where(q_first, aq, q), nt,
                                 preferred_element_type=F32)
            update(hh, (s0, s1), -(slope_ref[hg * hb + hh] * LOG2E) * gap)

    @pl.when(jnp.logical_not(off_diagonal))
    def _():
        kpos = lax.broadcasted_iota(I32, (tk, tq), 0) + j * tk
        qpos = lax.broadcasted_iota(I32, (tk, tq), 1) + i * tq
        dist = jnp.abs(kpos - qpos).astype(F32)
        for hh in range(hb):
            sl = slice(hh * LANES, (hh + 1) * LANES)
            q = q_ref[:, sl]
            k = k_ref[:, sl]
            zero = jnp.zeros_like(q)
            bias = dist * (-(slope_ref[hg * hb + hh] * LOG2E))
            s0 = lax.dot_general(k, jnp.where(q_first, q, zero), nt, preferred_element_type=F32) + bias
            s1 = lax.dot_general(k, jnp.where(q_first, zero, q), nt, preferred_element_type=F32) + bias
            update(hh, (s0, s1), 0.0)

    @pl.when(j == nk - 1)
    def _():
        for hh in range(hb):
            o = acc_ref[hh, 0] / l_ref[hh, 0] - lam_ref[0] * (acc_ref[hh, 1] / l_ref[hh, 1])
            ms = jnp.mean(o * o, axis=0, keepdims=True)
            y = o * lax.rsqrt(ms + RMS_EPS) * (g_ref[...] * out_scale)
            o_ref[:, hh * LANES:(hh + 1) * LANES] = y.T.astype(o_ref.dtype)


def _attention(qk, vt, row_off, slopes, lam, sub_g, batch, seq, out_scale, tq=512, tk=1024, hb=4):
    tq = min(tq, seq)
    tk = min(tk, seq // 4)
    nq, nk = seq // tq, seq // tk
    assert row_off % tq == 0 and row_off % tk == 0 and ATT_HEADS % hb == 0
    oq, ok = row_off // tq, row_off // tk
    ng = ATT_HEADS // hb
    aq, ak = _alibi_columns(slopes, tq, tk)
    kern = functools.partial(_attn_kernel, tq=tq, tk=tk, nk=nk, hb=hb, out_scale=out_scale)
    smem = pl.BlockSpec(memory_space=pltpu.SMEM)
    return pl.pallas_call(
        kern,
        grid=(batch, ng, nq, nk),
        in_specs=[smem, smem,
                  pl.BlockSpec((tq, hb * LANES), lambda b, h, i, j: (oq + b * nq + i, h)),
                  pl.BlockSpec((tk, hb * LANES), lambda b, h, i, j: (ok + b * nk + j, ng + h)),
                  pl.BlockSpec((hb * LANES, tk), lambda b, h, i, j: (h, ok + b * nk + j)),
                  pl.BlockSpec((hb, tq, LANES), lambda b, h, i, j: (h, 0, 0)),
                  pl.BlockSpec((hb, tk, LANES), lambda b, h, i, j: (h, 0, 0)),
                  pl.BlockSpec((LANES, 1), lambda b, h, i, j: (0, 0))],
        out_specs=pl.BlockSpec((tq, hb * LANES), lambda b, h, i, j: (b * nq + i, h)),
        out_shape=jax.ShapeDtypeStruct((batch * seq, ATT_HEADS * ATT_VDIM), BF16),
        scratch_shapes=[pltpu.VMEM((hb, 2, 1, tq), F32), pltpu.VMEM((hb, 2, 1, tq), F32),
                        pltpu.VMEM((hb, 2, LANES, tq), F32)],
        compiler_params=_cparams(("parallel", "parallel", "parallel", "arbitrary")),
        name="diff_attention",
    )(slopes, lam, qk, qk, vt, aq, ak, sub_g.reshape(LANES, 1).astype(F32))


def _shortconv_kernel(z_ref, w_ref, b_ref, o_ref):
    z = z_ref[...]
    n = z.shape[0]
    row = lax.broadcasted_iota(I32, z.shape, 0)
    prev = jnp.where(row == 0, 0.0, pltpu.roll(z, 1, axis=0))
    nxt = jnp.where(row == n - 1, 0.0, pltpu.roll(z, n - 1, axis=0))
    w = w_ref[...]
    o_ref[...] = prev * w[0:1] + z * w[1:2] + nxt * w[2:3] + b_ref[...]


def _shortconv(zh, row_off, conv_w, conv_b, batch, seq, cb=256):
    c3 = zh.shape[1]
    c = c3 // 3
    ncb = c // cb
    assert row_off % seq == 0
    ob = row_off // seq
    out = pl.pallas_call(
        _shortconv_kernel,
        grid=(batch, 3, ncb),
        in_specs=[pl.BlockSpec((seq, cb), lambda b, p, j: (ob + b, p * ncb + j)),
                  pl.BlockSpec((3, cb), lambda b, p, j: (0, p * ncb + j)),
                  pl.BlockSpec((1, cb), lambda b, p, j: (0, p * ncb + j))],
        out_specs=pl.BlockSpec((None, None, seq, cb), lambda b, p, j: (p, b, 0, j)),
        out_shape=jax.ShapeDtypeStruct((3, batch, seq, c), F32),
        compiler_params=_cparams(("parallel", "parallel", "parallel")),
        name="hyena_shortconv",
    )(zh, conv_w.astype(F32), conv_b.reshape(1, c3).astype(F32))
    return out


def _filter_kernel(z_ref, w1_ref, b1_ref, w2_ref, b2_ref, w3_ref, b3_ref, fr_ref, w4_ref,
                   t_ref, d_ref, o_ref, h_ref, *, tl):
    i = pl.program_id(0)
    g = pl.program_id(1)
    hi = lax.Precision.HIGHEST

    @pl.when(g == 0)
    def _():
        fr = fr_ref[...]
        h = jnp.sin(fr * (jnp.dot(z_ref[...], w1_ref[...], precision=hi, preferred_element_type=F32) + b1_ref[...]))
        h = jnp.sin(fr * (jnp.dot(h, w2_ref[...], precision=hi, preferred_element_type=F32) + b2_ref[...]))
        h_ref[...] = jnp.sin(fr * (jnp.dot(h, w3_ref[...], precision=hi, preferred_element_type=F32) + b3_ref[...]))

    f = jnp.dot(h_ref[...].astype(BF16), w4_ref[...], preferred_element_type=F32)
    f = f * jnp.exp(-t_ref[...] * d_ref[...])
    row = lax.broadcasted_iota(I32, f.shape, 0) + i * tl
    drop = jnp.logical_and(row == 0, g % 2 == 1)
    o_ref[...] = jnp.where(drop, 0.0, f)


def _hyena_filter_signals(seq, w1, b1, w2, b2, w3, b3, w4, freq, n_ch, tl=512):
    t = jnp.linspace(0.0, 1.0, seq, dtype=F32)[:, None]
    w = 2.0 * math.pi * jnp.arange(seq, dtype=F32)[:, None] / seq
    f = jnp.linspace(1e-4, FILTER_BANDS - 1, FILTER_BANDS, dtype=F32)[None, :]
    z = jnp.concatenate([t, jnp.cos(f * w), -jnp.sin(f * w)], axis=-1)
    deltas = jnp.abs(jnp.linspace(math.log(DECAY_FAST) / DECAY_TARGET,
                                  math.log(DECAY_SLOW) / DECAY_TARGET, n_ch, dtype=F32))[None, :]
    hid = w1.shape[1]
    emb = LANES
    z = jnp.pad(z, ((0, 0), (0, emb - z.shape[1])))
    w1 = jnp.pad(w1.astype(F32), ((0, emb - w1.shape[0]), (0, 0)))
    tl = min(tl, seq)
    full = lambda shape: pl.BlockSpec(shape, lambda i, g: tuple(0 for _ in shape))
    return pl.pallas_call(
        functools.partial(_filter_kernel, tl=tl),
        grid=(seq // tl, 2 * HYENA_ORDER),
        in_specs=[pl.BlockSpec((tl, emb), lambda i, g: (i, 0)),
                  full((emb, hid)), full((1, hid)), full((hid, hid)), full((1, hid)),
                  full((hid, hid)), full((1, hid)), full((1, hid)),
                  pl.BlockSpec((hid, n_ch), lambda i, g: (0, g)),
                  pl.BlockSpec((tl, 1), lambda i, g: (i, 0)),
                  full((1, n_ch))],
        out_specs=pl.BlockSpec((None, tl, n_ch), lambda i, g: (g, i, 0)),
        out_shape=jax.ShapeDtypeStruct((2 * HYENA_ORDER, seq, n_ch), F32),
        scratch_shapes=[pltpu.VMEM((tl, hid), F32)],
        compiler_params=_cparams(("parallel", "arbitrary")),
        name="hyena_filter_mlp",
    )(z, w1.astype(F32), b1.reshape(1, hid).astype(F32), w2.astype(F32), b2.reshape(1, hid).astype(F32),
      w3.astype(F32), b3.reshape(1, hid).astype(F32), freq.reshape(1, hid).astype(F32), w4.astype(BF16),
      t, deltas)


def _dft_tables(r):
    n = r * r
    k2 = jnp.arange(r, dtype=I32)
    n2 = jnp.arange(r // 2, dtype=I32)
    ang1 = (2.0 * math.pi / r) * ((k2[:, None] * n2[None, :]) % r).astype(F32)
    f1 = jnp.concatenate([jnp.cos(ang1), -jnp.sin(ang1)], axis=0)
    k1 = jnp.arange(r, dtype=I32)
    n1 = jnp.arange(r, dtype=I32)
    ang_a = (2.0 * math.pi / r) * ((k1[:, None] * n1[None, :]) % r).astype(F32)
    ang_b = (2.0 * math.pi / n) * (k2[:, None] * n1[None, :]).astype(F32)
    ca, sa = jnp.cos(ang_a)[None], jnp.sin(ang_a)[None]
    cb, sb = jnp.cos(ang_b)[:, None], jnp.sin(ang_b)[:, None]
    mr, mi_ = ca * cb - sa * sb, -(sa * cb + ca * sb)
    mf = jnp.concatenate([jnp.concatenate([mr, -mi_], axis=2),
                          jnp.concatenate([mi_, mr], axis=2)], axis=1)
    minv = jnp.swapaxes(mf, 1, 2)
    g3 = jnp.concatenate([jnp.cos(ang1.T), -jnp.sin(ang1.T)], axis=1) * (1.0 / n)
    eye = jnp.eye(N1_BLOCK, dtype=F32)
    f1, g3 = jnp.kron(f1, eye), jnp.kron(g3, eye)
    return f1.astype(BF16), mf.astype(BF16), minv.astype(BF16), g3.astype(BF16)


N1_BLOCK = SUBLANES


def _pack_complex(re, im):
    rb = lax.bitcast_convert_type(re.astype(BF16).astype(F32), U32)
    ib = lax.bitcast_convert_type(im.astype(BF16).astype(F32), U32)
    return (rb >> 16) | ib


def _unpack_complex_rows(w):
    re, im = _unpack(w)
    return jnp.concatenate([re, im], axis=0).astype(BF16)


def _fft1_kernel(f_ref, x_ref, o_ref, *, r):
    n_ch = x_ref.shape[-1]
    x = x_ref[...].reshape((r // 2) * N1_BLOCK, n_ch).astype(BF16)
    res = jnp.dot(f_ref[...], x, preferred_element_type=F32)
    half = r * N1_BLOCK
    o_ref[...] = _pack_complex(res[:half], res[half:]).reshape(r, N1_BLOCK, n_ch)


def _fft_stage1(x4, part, f1, r, n_ch):
    nb = x4.shape[1]
    xv = x4.reshape(x4.shape[0], nb, r // 2, r, n_ch)
    return pl.pallas_call(
        functools.partial(_fft1_kernel, r=r),
        grid=(nb, r // N1_BLOCK),
        in_specs=[pl.BlockSpec(f1.shape, lambda b, j: (0, 0)),
                  pl.BlockSpec((None, None, r // 2, N1_BLOCK, n_ch), lambda b, j: (part, b, 0, j, 0))],
        out_specs=pl.BlockSpec((None, r, N1_BLOCK, n_ch), lambda b, j: (b, 0, j, 0)),
        out_shape=jax.ShapeDtypeStruct((nb, r, r, n_ch), U32),
        compiler_params=_cparams(("parallel", "parallel")),
        name="hyena_dft_stage1",
    )(f1, xv)


def _k2_block(r):
    return max(1, (2 * LANES) // r)


def _filter_spec_kernel(mf_ref, bf_ref, bb_ref, o_ref, *, r):
    for kk in range(mf_ref.shape[0]):
        m = mf_ref[kk]
        xf = jnp.dot(m, _unpack_complex_rows(bf_ref[kk]), preferred_element_type=F32)
        xb = jnp.dot(m, _unpack_complex_rows(bb_ref[kk]), preferred_element_type=F32)
        o_ref[kk] = _pack_complex(xf[:r] + xb[:r], xf[r:] - xb[r:])


def _filter_spectrum(b1, mf, r, n_ch):
    kb = _k2_block(r)
    blk = lambda sel: pl.BlockSpec((None, kb, r, n_ch), lambda o, k: (2 * o + sel, k, 0, 0))
    return pl.pallas_call(
        functools.partial(_filter_spec_kernel, r=r),
        grid=(HYENA_ORDER, r // kb),
        in_specs=[pl.BlockSpec((kb, 2 * r, 2 * r), lambda o, k: (k, 0, 0)), blk(0), blk(1)],
        out_specs=pl.BlockSpec((None, kb, r, n_ch), lambda o, k: (o, k, 0, 0)),
        out_shape=jax.ShapeDtypeStruct((HYENA_ORDER, r, r, n_ch), U32),
        compiler_params=_cparams(("parallel", "parallel")),
        name="hyena_filter_spectrum",
    )(mf, b1, b1)


def _fft2_kernel(mf_ref, mi_ref, b_ref, h_ref, o_ref, *, r):
    for kk in range(mf_ref.shape[0]):
        x = jnp.dot(mf_ref[kk], _unpack_complex_rows(b_ref[kk]), preferred_element_type=F32)
        xr, xi = x[:r], x[r:]
        hr, hi = _unpack(h_ref[kk])
        y = jnp.concatenate([xr * hr - xi * hi, xr * hi + xi * hr], axis=0).astype(BF16)
        c = jnp.dot(mi_ref[kk], y, preferred_element_type=F32)
        o_ref[kk] = _pack_complex(c[:r], c[r:])


def _fft_stage2(b1, h, order, mf, minv, r, n_ch):
    nb = b1.shape[0]
    kb = _k2_block(r)
    blk = pl.BlockSpec((None, kb, r, n_ch), lambda k, b: (b, k, 0, 0))
    return pl.pallas_call(
        functools.partial(_fft2_kernel, r=r),
        grid=(r // kb, nb),
        in_specs=[pl.BlockSpec((kb, 2 * r, 2 * r), lambda k, b: (k, 0, 0)),
                  pl.BlockSpec((kb, 2 * r, 2 * r), lambda k, b: (k, 0, 0)),
                  blk,
                  pl.BlockSpec((None, kb, r, n_ch), lambda k, b: (order, k, 0, 0))],
        out_specs=blk,
        out_shape=jax.ShapeDtypeStruct((nb, r, r, n_ch), U32),
        compiler_params=_cparams(("parallel", "parallel")),
        name="hyena_dft_stage2",
    )(mf, minv, b1, h)


def _fft3_kernel(g_ref, c_ref, gate_ref, s_ref, bias_ref, ng_ref, *rest, final):
    if final:
        (o_ref,) = rest
    else:
        f_ref, o_ref, b_ref = rest
    r, nb, n_ch = c_ref.shape
    cc = _unpack_complex_rows(c_ref[...].reshape(r * nb, n_ch))
    y = jnp.dot(g_ref[...], cc, preferred_element_type=F32)
    rows = (r // 2) * nb
    s_new = gate_ref[...].reshape(rows, n_ch) * (y + s_ref[...].reshape(rows, n_ch) * bias_ref[...])
    if final:
        ms = jnp.mean(s_new * s_new, axis=-1, keepdims=True)
        s_new = s_new * lax.rsqrt(ms + RMS_EPS) * ng_ref[...]
    o_ref[...] = s_new.reshape(r // 2, nb, n_ch)
    if not final:
        res = jnp.dot(f_ref[...], s_new.astype(BF16), preferred_element_type=F32)
        half = r * nb
        b_ref[...] = _pack_complex(res[:half], res[half:]).reshape(r, nb, n_ch)


def _fft_stage3(c2, g3, z4, gate_part, s4, s_part, bias, norm_g, f1, r, n_ch):
    final = f1 is None
    nb = c2.shape[0]
    zv = z4.reshape(z4.shape[0], nb, r // 2, r, n_ch)
    sv = s4.reshape(s4.shape[0], nb, r // 2, r, n_ch)
    dspec = lambda part: pl.BlockSpec((None, None, r // 2, N1_BLOCK, n_ch), lambda b, j: (part, b, 0, j, 0))
    packed = pl.BlockSpec((None, r, N1_BLOCK, n_ch), lambda b, j: (b, 0, j, 0))
    s_spec = pl.BlockSpec((None, r // 2, N1_BLOCK, n_ch), lambda b, j: (b, 0, j, 0))
    s_shape = jax.ShapeDtypeStruct((nb, r // 2, r, n_ch), F32)
    in_specs = [pl.BlockSpec(g3.shape, lambda b, j: (0, 0)), packed, dspec(gate_part), dspec(s_part),
                pl.BlockSpec((1, n_ch), lambda b, j: (0, 0)), pl.BlockSpec((1, n_ch), lambda b, j: (0, 0))]
    args = [g3, c2, zv, sv, bias.reshape(1, n_ch).astype(F32), norm_g.reshape(1, n_ch).astype(F32)]
    if final:
        out_specs, out_shape = s_spec, s_shape
    else:
        in_specs.append(pl.BlockSpec(f1.shape, lambda b, j: (0, 0)))
        args.append(f1)
        out_specs = [s_spec, packed]
        out_shape = [s_shape, jax.ShapeDtypeStruct((nb, r, r, n_ch), U32)]
    return pl.pallas_call(
        functools.partial(_fft3_kernel, final=final),
        grid=(nb, r // N1_BLOCK),
        in_specs=in_specs,
        out_specs=out_specs,
        out_shape=out_shape,
        compiler_params=_cparams(("parallel", "parallel")),
        name="hyena_dft_stage3",
    )(*args)


def _hyena(zh, row_off, batch, seq, conv_w, conv_b, filt, fft_bias, out_g):
    n_ch = zh.shape[1] // 3
    r = int(round(math.sqrt(2 * seq)))
    assert r * r == 2 * seq and r % 16 == 0
    f1, mf, minv, g3 = _dft_tables(r)
    sig = _hyena_filter_signals(seq, *filt, n_ch=n_ch)
    hb1 = _fft_stage1(sig[None], 0, f1, r, n_ch)
    h = _filter_spectrum(hb1, mf, r, n_ch)
    z4 = _shortconv(zh, row_off, conv_w, conv_b, batch, seq)
    s4, s_part = z4, 2
    b1 = _fft_stage1(s4, s_part, f1, r, n_ch)
    for o in range(HYENA_ORDER):
        c2 = _fft_stage2(b1, h, o, mf, minv, r, n_ch)
        if o < HYENA_ORDER - 1:
            s, b1 = _fft_stage3(c2, g3, z4, o, s4, s_part, fft_bias[o], out_g, f1, r, n_ch)
        else:
            s = _fft_stage3(c2, g3, z4, o, s4, s_part, fft_bias[o], out_g, None, r, n_ch)
        s4, s_part = s.reshape(1, batch, seq, n_ch), 0
    return s4.reshape(batch * seq, n_ch)


def _extract_top(s, key, count):
    vals, keys = [], []
    for _ in range(count):
        m = jnp.max(s, axis=0, keepdims=True)
        kmin = jnp.min(jnp.where(s == m, key, jnp.inf), axis=0, keepdims=True)
        s = jnp.where(key == kmin, -jnp.inf, s)
        vals.append(m)
        keys.append(kmin)
    return vals, keys


def _peer_topk_kernel(q_ref, keys_ref, eid_ref, gate_ref):
    t = q_ref.shape[0]
    nk = PEER_NKEYS
    q = q_ref[...]
    row_key = lax.broadcasted_iota(I32, (nk, t), 0).astype(F32)
    tops = []
    for c in range(2):
        s = lax.dot_general(keys_ref[c], q[:, c * nk:(c + 1) * nk], (((1,), (1,)), ((), ())),
                            preferred_element_type=F32)
        tops.append(_extract_top(s, row_key, PEER_TOPK))
    (v1, i1), (v2, i2) = tops
    rows16 = lax.broadcasted_iota(I32, (PEER_TOPK, t), 0)
    v2a = jnp.zeros((PEER_TOPK, t), F32)
    i2a = jnp.zeros((PEER_TOPK, t), F32)
    for j in range(PEER_TOPK):
        v2a = jnp.where(rows16 == j, v2[j], v2a)
        i2a = jnp.where(rows16 == j, i2[j], i2a)
    n_exp = float(nk * nk)
    half = PEER_TOPK // 2
    rows8 = lax.broadcasted_iota(I32, (half, t), 0)
    pos8 = rows8.astype(F32)
    v2h, i2h = v2a[:half], i2a[:half]
    cand = [v1[0] + v2a]
    ckey = [rows16.astype(F32) * n_exp + (i1[0] * float(nk) + i2a)]
    for i in range(1, half):
        cand.append(jnp.where(rows8 < PEER_TOPK // (i + 1), v1[i] + v2h, -jnp.inf))
        ckey.append((pos8 + float(i * PEER_TOPK)) * n_exp + (i1[i] * float(nk) + i2h))
    v1t = jnp.zeros((half, t), F32)
    i1t = jnp.zeros((half, t), F32)
    for r in range(half):
        v1t = jnp.where(rows8 == r, v1[half + r], v1t)
        i1t = jnp.where(rows8 == r, i1[half + r], i1t)
    cand.append(v1t + v2[0])
    ckey.append((pos8 + float(half)) * (PEER_TOPK * n_exp) + (i1t * float(nk) + i2[0]))
    tv, tk_ = _extract_top(jnp.concatenate(cand, axis=0), jnp.concatenate(ckey, axis=0), PEER_TOPK)
    denom = jnp.zeros((1, t), F32)
    es = []
    for k in range(PEER_TOPK):
        e = jnp.exp(tv[k] - tv[0])
        es.append(e)
        denom = denom + e
    eid = jnp.zeros((PEER_TOPK, t), F32)
    gate = jnp.zeros((PEER_TOPK, t), F32)
    for k in range(PEER_TOPK):
        pos = jnp.floor(tk_[k] * (1.0 / n_exp))
        eid = jnp.where(rows16 == k, tk_[k] - pos * n_exp, eid)
        gate = jnp.where(rows16 == k, es[k] / denom, gate)
    eid_ref[...] = eid.astype(I32)
    gate_ref[...] = gate


def _peer_topk(q, keys, tm=512):
    t = q.shape[0]
    tm = min(tm, t)
    out_spec = pl.BlockSpec((PEER_TOPK, tm), lambda i, h: (h, i))
    return pl.pallas_call(
        _peer_topk_kernel,
        grid=(t // tm, PEER_HEADS),
        in_specs=[pl.BlockSpec((tm, 2 * PEER_NKEYS), lambda i, h: (i, h)),
                  pl.BlockSpec((None, 2, PEER_NKEYS, PEER_NKEYS), lambda i, h: (h, 0, 0, 0))],
        out_specs=[out_spec, out_spec],
        out_shape=[jax.ShapeDtypeStruct((PEER_HEADS * PEER_TOPK, t), I32),
                   jax.ShapeDtypeStruct((PEER_HEADS * PEER_TOPK, t), F32)],
        compiler_params=_cparams(("parallel", "parallel")),
        name="peer_topk",
    )(q, keys)


def _pack_table(tab):
    e, d = tab.shape
    assert d == 2 * SUBLANES * LANES
    bits = lax.bitcast_convert_type(tab.astype(BF16), jnp.uint16).astype(U32)
    packed = bits[:, :d // 2] | (bits[:, d // 2:] << 16)
    return packed.reshape(e * SUBLANES, LANES)


def _unpack(w):
    lo = lax.bitcast_convert_type(w << 16, F32)
    hi = lax.bitcast_convert_type(w & jnp.uint32(0xFFFF0000), F32)
    return lo, hi


_BITREV8 = (0, 4, 2, 6, 1, 5, 3, 7)


def _sublane_fold8(parts):
    sub = lax.broadcasted_iota(I32, (2 * SUBLANES, LANES), 0) // 2

    def rolled(a, shift):
        return pltpu.bitcast(pltpu.roll(pltpu.bitcast(a, U32), shift, axis=0), BF16)

    lvl = [parts[_BITREV8[r]] for r in range(8)]
    for shift, mask in ((4, sub < 4), (2, (sub % 4) < 2), (1, (sub % 2) < 1)):
        nxt = []
        for a, b in zip(lvl[0::2], lvl[1::2]):
            nxt.append(jnp.where(mask, a + rolled(a, SUBLANES - shift), b + rolled(b, shift)))
        lvl = nxt
    return lvl[0]


def _pair_ranges(half):
    if half == 0:
        return (0, PEER_WINDOW), (PEER_WINDOW, PEER_PAIRS)
    return (PEER_PAIRS - PEER_WINDOW, PEER_PAIRS), (0, PEER_PAIRS - PEER_WINDOW)


def _overflow(n_low, half):
    return n_low > PEER_WINDOW if half == 0 else n_low < PEER_PAIRS - PEER_WINDOW


def _table_rows(tab_ref, off):
    return _unpack(tab_ref[pl.ds(pl.multiple_of(off, SUBLANES), SUBLANES), :])


def _peer_u_kernel(off_ref, nlow_ref, tab_ref, x_ref, o_ref, extra_ref, *, tm, half):
    lane = lax.broadcasted_iota(I32, (SUBLANES, LANES), 1)
    sub = lax.broadcasted_iota(I32, (SUBLANES, LANES), 0)
    lane_grp = lax.shift_right_logical(lane, 3)
    diag = sub == (lane & (SUBLANES - 1))
    main, rest = _pair_ranges(half)

    def folded(t, p0, p1):
        xb = pltpu.bitcast(x_ref[t], BF16)
        out = []
        for g in range(p0 // SUBLANES, p1 // SUBLANES):
            parts = []
            for r in range(SUBLANES):
                off = pl.multiple_of(off_ref[t, g * SUBLANES + r], SUBLANES)
                parts.append(pltpu.bitcast(tab_ref[pl.ds(off, SUBLANES), :], BF16) * xb)
            out.append(pltpu.bitcast(_sublane_fold8(parts), U32))
        return out

    def lane_sums(folds, p0):
        mat = jnp.zeros((SUBLANES, LANES), F32)
        for i, f in enumerate(folds):
            lo, hi = _unpack(f)
            mat = jnp.where(lane_grp == p0 // SUBLANES + i, jnp.sum(lo + hi, axis=-1, keepdims=True), mat)
        return jnp.sum(jnp.where(diag, mat, 0.0), axis=0, keepdims=True)

    def finish(t, folds):
        o_ref[pl.ds(t, 1), :] = lane_sums(folds, main[0]) + extra_ref[pl.ds(t, 1), :]

    def token(t, prev):
        cur = folded(t, *main)
        finish(jnp.maximum(t - 1, 0), prev)
        extra_ref[pl.ds(t, 1), :] = jnp.zeros((1, LANES), F32)

        @pl.when(_overflow(nlow_ref[0, t], half))
        def _():
            extra_ref[pl.ds(t, 1), :] = lane_sums(folded(t, *rest), rest[0])

        return tuple(cur)

    extra_ref[pl.ds(0, 1), :] = jnp.zeros((1, LANES), F32)
    zeros = tuple(jnp.zeros((SUBLANES, LANES), U32) for _ in range((main[1] - main[0]) // SUBLANES))
    last = lax.fori_loop(0, tm, token, zeros)
    finish(tm - 1, last)


def _peer_u(off, nlow3, tab, x4, half, tm=128):
    t = off.shape[0]
    rows = PEER_HALF * SUBLANES
    return pl.pallas_call(
        functools.partial(_peer_u_kernel, tm=tm, half=half),
        grid=(t // tm,),
        in_specs=[pl.BlockSpec((tm, LANES), lambda i: (i, 0), memory_space=pltpu.SMEM),
                  pl.BlockSpec((None, 1, tm), lambda i: (i, 0, 0), memory_space=pltpu.SMEM),
                  pl.BlockSpec((rows, LANES), lambda i: (half, 0), pipeline_mode=pl.Buffered(1)),
                  pl.BlockSpec((tm, SUBLANES, LANES), lambda i: (i, 0, 0))],
        out_specs=pl.BlockSpec((tm, LANES), lambda i: (i, 0)),
        out_shape=jax.ShapeDtypeStruct((t, LANES), F32),
        scratch_shapes=[pltpu.VMEM((tm, LANES), F32)],
        compiler_params=_cparams(("arbitrary",)),
        name="peer_expert_scores",
    )(off, nlow3, tab, x4)


def _peer_coef_kernel(a0_ref, a1_ref, eid_ref, gate_ref, c0_ref, c1_ref):
    low = eid_ref[...] < PEER_HALF
    a = jnp.where(low, a0_ref[...], a1_ref[...])
    coef = gate_ref[...] * (0.5 * a * (1.0 + lax.erf(a * (1.0 / math.sqrt(2.0)))))
    c0_ref[...] = jnp.where(low, coef, 0.0)
    c1_ref[...] = jnp.where(low, 0.0, coef)


def _peer_coef(a0, a1, eid, gate, tm=1024):
    t = eid.shape[0]
    tm = min(tm, t)
    spec = pl.BlockSpec((tm, LANES), lambda i: (i, 0))
    return pl.pallas_call(
        _peer_coef_kernel,
        grid=(t // tm,),
        in_specs=[spec, spec, spec, spec],
        out_specs=[spec, spec],
        out_shape=[jax.ShapeDtypeStruct((t, LANES), F32)] * 2,
        compiler_params=_cparams(("parallel",)),
        name="peer_coef",
    )(a0, a1, eid, gate)


def _peer_v_kernel(off_ref, nlow_ref, coef_ref, tab_ref, base_ref, *o_refs, tm, half, n_first):
    n_acc = 4
    main, rest = _pair_ranges(half)

    def weighted(t, p0, p1):
        acc_lo = [jnp.zeros((SUBLANES, LANES), F32) for _ in range(n_acc)]
        acc_hi = [jnp.zeros((SUBLANES, LANES), F32) for _ in range(n_acc)]
        for p in range(p0, p1):
            c = coef_ref[t, p]
            lo, hi = _table_rows(tab_ref, off_ref[t, p])
            acc_lo[p % n_acc] = acc_lo[p % n_acc] + c * lo
            acc_hi[p % n_acc] = acc_hi[p % n_acc] + c * hi
        return ((acc_lo[0] + acc_lo[1]) + (acc_lo[2] + acc_lo[3]),
                (acc_hi[0] + acc_hi[1]) + (acc_hi[2] + acc_hi[3]))

    def run(o_ref):
        def store_row(t, lo, hi):
            o_ref[t, 0] = base_ref[t, 0] + lo
            o_ref[t, 1] = base_ref[t, 1] + hi

        def token(t, carry):
            lo, hi = weighted(t, *main)
            store_row(t, lo, hi)

            @pl.when(_overflow(nlow_ref[0, t], half))
            def _():
                lo2, hi2 = weighted(t, *rest)
                store_row(t, lo + lo2, hi + hi2)

            return carry

        lax.fori_loop(0, tm, token, 0)

    if len(o_refs) == 1:
        run(o_refs[0])
    else:
        first = pl.program_id(0) < n_first
        pl.when(first)(lambda: run(o_refs[0]))
        pl.when(jnp.logical_not(first))(lambda: run(o_refs[1]))


def _peer_v(off, nlow3, coef, tab, base, half, tm=128, split=None):
    t = base.shape[0]
    rows = PEER_HALF * SUBLANES
    smem = pl.BlockSpec((tm, LANES), lambda i: (i, 0), memory_space=pltpu.SMEM)
    tile = pl.BlockSpec((tm, 2, SUBLANES, LANES), lambda i: (i, 0, 0, 0))
    if split is None:
        n_first, out_specs = 0, tile
        out_shape = jax.ShapeDtypeStruct((t, 2, SUBLANES, LANES), F32)
    else:
        assert split % tm == 0
        n_first, n_rest = split // tm, (t - split) // tm
        blk = (tm, 2, SUBLANES, LANES)
        out_specs = [pl.BlockSpec(blk, lambda i: (jnp.clip(i, 0, n_first - 1), 0, 0, 0)),
                     pl.BlockSpec(blk, lambda i: (jnp.clip(i - n_first, 0, n_rest - 1), 0, 0, 0))]
        out_shape = [jax.ShapeDtypeStruct((split, 2, SUBLANES, LANES), F32),
                     jax.ShapeDtypeStruct((t - split, 2, SUBLANES, LANES), F32)]
    return pl.pallas_call(
        functools.partial(_peer_v_kernel, tm=tm, half=half, n_first=n_first),
        grid=(t // tm,),
        in_specs=[smem,
                  pl.BlockSpec((None, 1, tm), lambda i: (i, 0, 0), memory_space=pltpu.SMEM),
                  smem,
                  pl.BlockSpec((rows, LANES), lambda i: (half, 0), pipeline_mode=pl.Buffered(1)),
                  tile],
        out_specs=out_specs,
        out_shape=out_shape,
        compiler_params=_cparams(("arbitrary",)),
        name="peer_expert_sum",
    )(off, nlow3, coef, tab, base)


def _peer(x1, split, norm2_g, wq_bf, keys_bf, u_packed, v_packed, tm=128):
    t, d = x1.shape
    tm = min(tm, t)
    (xn_bf,) = _rmsnorm(x1, norm2_g, (BF16,))
    q = _matmul(xn_bf, wq_bf, wq_bf.shape[1], 0, BF16)
    eid_t, gate_t = _peer_topk(q, keys_bf)
    eid, gate = eid_t.T, gate_t.T
    n_exp = PEER_NKEYS * PEER_NKEYS
    high = (eid >= PEER_HALF).astype(I32)
    nlow3 = (PEER_PAIRS - jnp.sum(high, axis=-1, dtype=I32)).reshape(t // tm, 1, tm)
    slot = lax.broadcasted_iota(I32, eid.shape, 1)
    order = jnp.sort((high * PEER_PAIRS + slot) * n_exp + eid, axis=-1)
    eid = order % n_exp
    gate = jnp.take_along_axis(gate, (order // n_exp) % PEER_PAIRS, axis=-1)
    off = (eid & (PEER_HALF - 1)) * SUBLANES
    xw = _pack_table(xn_bf).reshape(t, SUBLANES, LANES)
    a0 = _peer_u(off, nlow3, u_packed, xw, 0, tm)
    a1 = _peer_u(off, nlow3, u_packed, xw, 1, tm)
    c0, c1 = _peer_coef(a0, a1, eid, gate)
    y = _peer_v(off, nlow3, c0, v_packed, x1.reshape(t, 2, SUBLANES, LANES), 0, tm)
    ya, yb = _peer_v(off, nlow3, c1, v_packed, y, 1, tm, split=split)
    return ya.reshape(split, d), yb.reshape(t - split, d)


def kernel(x_prompt, x_sample, norm1_g, w_in, q_norm_g, k_norm_g, lambda_q1, lambda_k1, lambda_q2, lambda_k2, attn_sub_g, conv_w, conv_b, filt_w1, filt_b1, filt_w2, filt_b2, filt_w3, filt_b3, filt_w4, filt_freq, fft_bias, hyena_out_g, w_out, norm2_g, peer_wq, peer_keys, peer_u, peer_v):
    depth = w_in.shape[0]
    d_model = x_prompt.shape[-1]
    att_w = ATT_HEADS * ATT_VDIM
    shapes = [x_prompt.shape[:2], x_sample.shape[:2]]
    xs = [x_prompt.reshape(-1, d_model), x_sample.reshape(-1, d_model)]
    n0 = xs[0].shape[0]
    slopes = 2.0 ** (-8.0 * jnp.arange(1, ATT_HEADS + 1, dtype=F32) / ATT_HEADS)

    for l in range(depth):
        lambda_init = 0.8 - 0.6 * math.exp(-0.3 * l)
        lam = (jnp.exp(jnp.sum(lambda_q1[l].astype(F32) * lambda_k1[l].astype(F32)))
               - jnp.exp(jnp.sum(lambda_q2[l].astype(F32) * lambda_k2[l].astype(F32)))
               + lambda_init).reshape(1)
        w_in_bf = w_in[l].astype(BF16)
        q_gain = jnp.tile(q_norm_g[l].astype(F32), 2 * ATT_HEADS) * (ATT_QKDIM ** -0.5 * LOG2E)
        k_gain = jnp.tile(k_norm_g[l].astype(F32), 2 * ATT_HEADS)
        qk_gain = jnp.concatenate([q_gain, k_gain]).reshape(1, 2 * att_w)

        h_bf = _rmsnorm_stacked(xs[0], xs[1], norm1_g[l], BF16)
        qk = _matmul(h_bf, w_in_bf, 2 * att_w, 0, BF16, mode="qknorm", extra=qk_gain)
        vt = _matmul_nt(w_in_bf[:, 2 * att_w:3 * att_w].T, h_bf, BF16)
        zh = _matmul(h_bf, w_in_bf, w_in.shape[2] - 3 * att_w, 3 * att_w, F32)

        filt = (filt_w1[l], filt_b1[l], filt_w2[l], filt_b2[l], filt_w3[l], filt_b3[l], filt_w4[l], filt_freq[l])
        segs, row = [], 0
        for (b, s), x_seg in zip(shapes, xs):
            att = _attention(qk, vt, row, slopes, lam, attn_sub_g[l], b, s, 1.0 - lambda_init)
            hy = _hyena(zh, row, b, s, conv_w[l], conv_b[l], filt, fft_bias[l], hyena_out_g[l])
            segs.append((att, hy, x_seg))
            row += b * s
        x1 = _outproj_stacked(segs[0], segs[1], w_out[l].astype(BF16))

        xs = _peer(x1, n0, norm2_g[l], peer_wq[l].astype(BF16), peer_keys[l].astype(BF16),
                   _pack_table(peer_u[l]), _pack_table(peer_v[l]))

    return (xs[0].reshape(x_prompt.shape), xs[1].reshape(x_sample.shape))
```

```python
import functools
import math

import jax
import jax.numpy as jnp
from jax import lax
from jax.experimental import pallas as pl
from jax.experimental.pallas import tpu as pltpu

F32 = jnp.float32
BF16 = jnp.bfloat16
I32 = jnp.int32
U32 = jnp.uint32

RMS_EPS = 1e-6
LOG2E = 1.4426950408889634
LANES = 128
SUBLANES = 8
VMEM_LIMIT_BYTES = 56 * 1024 * 1024

ATT_HEADS = 8
ATT_VDIM = 128
ATT_QKDIM = 64
HYENA_ORDER = 2
FILTER_BANDS = 16
DECAY_FAST = 0.3
DECAY_SLOW = 1.5
DECAY_TARGET = 1e-2
PEER_HEADS = 8
PEER_NKEYS = 128
PEER_TOPK = 16
PEER_HALF = PEER_NKEYS * PEER_NKEYS // 2
PEER_PAIRS = PEER_HEADS * PEER_TOPK
PEER_WINDOW = 80


def _cparams(sem, vmem=VMEM_LIMIT_BYTES):
    return pltpu.CompilerParams(dimension_semantics=sem, vmem_limit_bytes=vmem)


def _segment_spec(block, start, count, minor=0):
    return pl.BlockSpec(block, lambda i: (jnp.clip(i - start, 0, count - 1), minor))


def _rmsnorm_kernel(x_ref, g_ref, *o_refs):
    x = x_ref[...]
    ms = jnp.mean(x * x, axis=-1, keepdims=True)
    y = x * lax.rsqrt(ms + RMS_EPS) * g_ref[...]
    for o_ref in o_refs:
        o_ref[...] = y.astype(o_ref.dtype)


def _rmsnorm2_kernel(xa_ref, xb_ref, g_ref, o_ref, *, na):
    first = pl.program_id(0) < na
    x = jnp.where(first, xa_ref[...], xb_ref[...])
    ms = jnp.mean(x * x, axis=-1, keepdims=True)
    o_ref[...] = (x * lax.rsqrt(ms + RMS_EPS) * g_ref[...]).astype(o_ref.dtype)


def _rmsnorm_stacked(xa, xb, g, out_dtype, tm=512):
    d = xa.shape[1]
    na, nb = xa.shape[0] // tm, xb.shape[0] // tm
    assert xa.shape[0] % tm == 0 and xb.shape[0] % tm == 0
    return pl.pallas_call(
        functools.partial(_rmsnorm2_kernel, na=na),
        grid=(na + nb,),
        in_specs=[_segment_spec((tm, d), 0, na), _segment_spec((tm, d), na, nb),
                  pl.BlockSpec((1, d), lambda i: (0, 0))],
        out_specs=pl.BlockSpec((tm, d), lambda i: (i, 0)),
        out_shape=jax.ShapeDtypeStruct((xa.shape[0] + xb.shape[0], d), out_dtype),
        compiler_params=_cparams(("parallel",)),
        name="rmsnorm_stacked",
    )(xa, xb, g.reshape(1, d).astype(F32))


def _rmsnorm(x, g, out_dtypes, tm=512):
    t, d = x.shape
    tm = min(tm, t)
    spec = pl.BlockSpec((tm, d), lambda i: (i, 0))
    return pl.pallas_call(
        _rmsnorm_kernel,
        grid=(t // tm,),
        in_specs=[spec, pl.BlockSpec((1, d), lambda i: (0, 0))],
        out_specs=[spec for _ in out_dtypes],
        out_shape=[jax.ShapeDtypeStruct((t, d), dt) for dt in out_dtypes],
        compiler_params=_cparams(("parallel",)),
        name="rmsnorm",
    )(x, g.reshape(1, d).astype(F32))


def _group_rms_scale(x, gain):
    lane = lax.broadcasted_iota(I32, x.shape, 1)
    lo = lane < ATT_QKDIM
    x2 = x * x
    s_lo = jnp.sum(jnp.where(lo, x2, 0.0), axis=-1, keepdims=True)
    s_hi = jnp.sum(jnp.where(lo, 0.0, x2), axis=-1, keepdims=True)
    ms = jnp.where(lo, s_lo, s_hi) * (1.0 / ATT_QKDIM)
    return x * lax.rsqrt(ms + RMS_EPS) * gain


def _mm_kernel(a_ref, b_ref, *rest, mode):
    acc = jnp.dot(a_ref[...], b_ref[...], preferred_element_type=F32)
    if mode == "plain":
        (o_ref,) = rest
        o_ref[...] = acc.astype(o_ref.dtype)
    elif mode == "residual":
        r_ref, o_ref = rest
        o_ref[...] = (acc + r_ref[...]).astype(o_ref.dtype)
    elif mode == "qknorm":
        g_ref, o_ref = rest
        for c in range(acc.shape[1] // LANES):
            sl = slice(c * LANES, (c + 1) * LANES)
            o_ref[:, sl] = _group_rms_scale(acc[:, sl], g_ref[:, sl]).astype(o_ref.dtype)
    else:
        raise ValueError(mode)


def _matmul(a, b, n_cols, col_off, out_dtype, mode="plain", extra=None, tm=1024, tn=1024):
    m, k = a.shape
    tm = min(tm, m)
    tn = min(tn, n_cols)
    assert col_off % tn == 0 and n_cols % tn == 0 and m % tm == 0
    off = col_off // tn
    in_specs = [pl.BlockSpec((tm, k), lambda i, j: (i, 0)),
                pl.BlockSpec((k, tn), lambda i, j: (0, j + off))]
    args = [a, b]
    if mode == "residual":
        in_specs.append(pl.BlockSpec((tm, tn), lambda i, j: (i, j)))
        args.append(extra)
    elif mode == "qknorm":
        in_specs.append(pl.BlockSpec((1, tn), lambda i, j: (0, j)))
        args.append(extra)
    return pl.pallas_call(
        functools.partial(_mm_kernel, mode=mode),
        grid=(m // tm, n_cols // tn),
        in_specs=in_specs,
        out_specs=pl.BlockSpec((tm, tn), lambda i, j: (i, j)),
        out_shape=jax.ShapeDtypeStruct((m, n_cols), out_dtype),
        compiler_params=_cparams(("parallel", "arbitrary")),
        name="matmul_" + mode,
    )(*args)


def _outproj_kernel(att_a, hy_a, x_a, att_b, hy_b, x_b, w_ref, o_ref, *, na):
    kw = att_a.shape[1]

    def run(att_ref, hy_ref, x_ref):
        acc = jnp.dot(att_ref[...], w_ref[:kw, :], preferred_element_type=F32)
        acc = acc + jnp.dot(hy_ref[...].astype(BF16), w_ref[kw:, :], preferred_element_type=F32)
        o_ref[...] = acc + x_ref[...]

    first = pl.program_id(0) < na
    pl.when(first)(lambda: run(att_a, hy_a, x_a))
    pl.when(jnp.logical_not(first))(lambda: run(att_b, hy_b, x_b))


def _outproj_stacked(seg_a, seg_b, w, tm=512, tn=1024):
    n_out = w.shape[1]
    na, nb = seg_a[0].shape[0] // tm, seg_b[0].shape[0] // tm
    assert seg_a[0].shape[0] % tm == 0 and seg_b[0].shape[0] % tm == 0 and n_out % tn == 0

    def specs(seg, start, count):
        att, hy, x = seg
        row = lambda i, j: (jnp.clip(i - start, 0, count - 1), 0)
        return [pl.BlockSpec((tm, att.shape[1]), row), pl.BlockSpec((tm, hy.shape[1]), row),
                pl.BlockSpec((tm, tn), lambda i, j: (jnp.clip(i - start, 0, count - 1), j))]

    return pl.pallas_call(
        functools.partial(_outproj_kernel, na=na),
        grid=(na + nb, n_out // tn),
        in_specs=specs(seg_a, 0, na) + specs(seg_b, na, nb) + [pl.BlockSpec((w.shape[0], tn), lambda i, j: (0, j))],
        out_specs=pl.BlockSpec((tm, tn), lambda i, j: (i, j)),
        out_shape=jax.ShapeDtypeStruct(((na + nb) * tm, n_out), F32),
        compiler_params=_cparams(("parallel", "arbitrary")),
        name="out_projection",
    )(*seg_a, *seg_b, w)


def _mm_nt_kernel(w_ref, a_ref, o_ref):
    o_ref[...] = lax.dot_general(w_ref[...], a_ref[...], (((1,), (1,)), ((), ())),
                                 preferred_element_type=F32).astype(o_ref.dtype)


def _matmul_nt(wt, a, out_dtype, tm=1024):
    n, k = wt.shape
    m = a.shape[0]
    tm = min(tm, m)
    return pl.pallas_call(
        _mm_nt_kernel,
        grid=(m // tm,),
        in_specs=[pl.BlockSpec((n, k), lambda i: (0, 0)), pl.BlockSpec((tm, k), lambda i: (i, 0))],
        out_specs=pl.BlockSpec((n, tm), lambda i: (0, i)),
        out_shape=jax.ShapeDtypeStruct((n, m), out_dtype),
        compiler_params=_cparams(("parallel",)),
        name="matmul_nt",
    )(wt, a)


def _alibi_columns(slopes, tq, tk):
    return [_alibi_side(slopes, tq, True), _alibi_side(slopes, tk, False)]


def _alibi_side(slopes, n, query_side):
    pos = jnp.arange(n, dtype=F32)
    val = (slopes.astype(F32) * LOG2E)[:, None] * pos[None, :]

    def pieces(x):
        p1 = x.astype(BF16)
        r1 = x - p1.astype(F32)
        p2 = r1.astype(BF16)
        p3 = (r1 - p2.astype(F32)).astype(BF16)
        return [p1, p2, p3]

    ones = [jnp.ones_like(val, BF16)] * 3
    six = jnp.stack(pieces(-val) + ones if query_side else ones + pieces(val), axis=-1)
    pad = jnp.zeros(val.shape + (ATT_QKDIM - 6,), BF16)
    return jnp.concatenate([six, pad, six, pad], axis=-1)


def _attn_kernel(slope_ref, lam_ref, q_ref, k_ref, vt_ref, aq_ref, ak_ref, g_ref, o_ref,
                 m_ref, l_ref, acc_ref, *, tq, tk, nk, hb, out_scale):
    hg = pl.program_id(1)
    i = pl.program_id(2)
    j = pl.program_id(3)

    @pl.when(j == 0)
    def _():
        m_ref[...] = jnp.full(m_ref.shape, -jnp.inf, F32)
        l_ref[...] = jnp.zeros(l_ref.shape, F32)
        acc_ref[...] = jnp.zeros(acc_ref.shape, F32)

    q_first = lax.broadcasted_iota(I32, (tq, LANES), 1) < ATT_QKDIM
    k_first = lax.broadcasted_iota(I32, (tk, LANES), 1) < ATT_QKDIM
    nt = (((1,), (1,)), ((), ()))

    def update(hh, scores, shift):
        vt = vt_ref[hh * LANES:(hh + 1) * LANES, :]
        for c, s in enumerate(scores):
            m_old = m_ref[hh, c]
            m_new = jnp.maximum(m_old, jnp.max(s, axis=0, keepdims=True) + shift)
            alpha = jnp.exp2(m_old - m_new)
            p = jnp.exp2(s - (m_new - shift))
            l_ref[hh, c] = alpha * l_ref[hh, c] + jnp.sum(p, axis=0, keepdims=True)
            acc_ref[hh, c] = alpha * acc_ref[hh, c] + jnp.dot(vt, p.astype(BF16),
                                                              preferred_element_type=F32)
            m_ref[hh, c] = m_new

    keys_before = i * tq >= (j + 1) * tk
    keys_after = (i + 1) * tq <= j * tk
    off_diagonal = jnp.logical_or(keys_before, keys_after)

    @pl.when(off_diagonal)
    def _():
        sign = jnp.where(keys_after, -1.0, 1.0).astype(BF16)
        gap = jnp.abs(i * tq - j * tk).astype(F32)
        for hh in range(hb):
            sl = slice(hh * LANES, (hh + 1) * LANES)
            q = q_ref[:, sl]
            k = k_ref[:, sl]
            aq = aq_ref[hh]
            ak = ak_ref[hh] * sign
            s0 = lax.dot_general(jnp.where(k_first, k, ak), jnp.where(q_first, q, aq), nt,
                                 preferred_element_type=F32)
            s1 = lax.dot_general(jnp.where(k_first, ak, k), jnp.where(q_first, aq, q), nt,
                                 preferred_element_type=F32)
            update(hh, (s0, s1), -(slope_ref[hg * hb + hh] * LOG2E) * gap)

    @pl.when(jnp.logical_not(off_diagonal))
    def _():
        kpos = lax.broadcasted_iota(I32, (tk, tq), 0) + j * tk
        qpos = lax.broadcasted_iota(I32, (tk, tq), 1) + i * tq
        dist = jnp.abs(kpos - qpos).astype(F32)
        for hh in range(hb):
            sl = slice(hh * LANES, (hh + 1) * LANES)
            q = q_ref[:, sl]
            k = k_ref[:, sl]
            zero = jnp.zeros_like(q)
            bias = dist * (-(slope_ref[hg * hb + hh] * LOG2E))
            s0 = lax.dot_general(k, jnp.where(q_first, q, zero), nt, preferred_element_type=F32) + bias
            s1 = lax.dot_general(k, jnp.where(q_first, zero, q), nt, preferred_element_type=F32) + bias
            update(hh, (s0, s1), 0.0)

    @pl.when(j == nk - 1)
    def _():
        for hh in range(hb):
            o = acc_ref[hh, 0] / l_ref[hh, 0] - lam_ref[0] * (acc_ref[hh, 1] / l_ref[hh, 1])
            ms = jnp.mean(o * o, axis=0, keepdims=True)
            y = o * lax.rsqrt(ms + RMS_EPS) * (g_ref[...] * out_scale)
            o_ref[:, hh * LANES:(hh + 1) * LANES] = y.T.astype(o_ref.dtype)


def _attention(qk, vt, row_off, slopes, lam, sub_g, batch, seq, out_scale, tq=512, tk=1024, hb=4):
    tq = min(tq, seq)
    tk = min(tk, seq // 4)
    nq, nk = seq // tq, seq // tk
    assert row_off % tq == 0 and row_off % tk == 0 and ATT_HEADS % hb == 0
    oq, ok = row_off // tq, row_off // tk
    ng = ATT_HEADS // hb
    aq, ak = _alibi_columns(slopes, tq, tk)
    kern = functools.partial(_attn_kernel, tq=tq, tk=tk, nk=nk, hb=hb, out_scale=out_scale)
    smem = pl.BlockSpec(memory_space=pltpu.SMEM)
    return pl.pallas_call(
        kern,
        grid=(batch, ng, nq, nk),
        in_specs=[smem, smem,
                  pl.BlockSpec((tq, hb * LANES), lambda b, h, i, j: (oq + b * nq + i, h)),
                  pl.BlockSpec((tk, hb * LANES), lambda b, h, i, j: (ok + b * nk + j, ng + h)),
                  pl.BlockSpec((hb * LANES, tk), lambda b, h, i, j: (h, ok + b * nk + j)),
                  pl.BlockSpec((hb, tq, LANES), lambda b, h, i, j: (h, 0, 0)),
                  pl.BlockSpec((hb, tk, LANES), lambda b, h, i, j: (h, 0, 0)),
                  pl.BlockSpec((LANES, 1), lambda b, h, i, j: (0, 0))],
        out_specs=pl.BlockSpec((tq, hb * LANES), lambda b, h, i, j: (b * nq + i, h)),
        out_shape=jax.ShapeDtypeStruct((batch * seq, ATT_HEADS * ATT_VDIM), BF16),
        scratch_shapes=[pltpu.VMEM((hb, 2, 1, tq), F32), pltpu.VMEM((hb, 2, 1, tq), F32),
                        pltpu.VMEM((hb, 2, LANES, tq), F32)],
        compiler_params=_cparams(("parallel", "parallel", "parallel", "arbitrary")),
        name="diff_attention",
    )(slopes, lam, qk, qk, vt, aq, ak, sub_g.reshape(LANES, 1).astype(F32))


def _shortconv_kernel(z_ref, w_ref, b_ref, o_ref):
    z = z_ref[...]
    n = z.shape[0]
    row = lax.broadcasted_iota(I32, z.shape, 0)
    prev = jnp.where(row == 0, 0.0, pltpu.roll(z, 1, axis=0))
    nxt = jnp.where(row == n - 1, 0.0, pltpu.roll(z, n - 1, axis=0))
    w = w_ref[...]
    o_ref[...] = prev * w[0:1] + z * w[1:2] + nxt * w[2:3] + b_ref[...]


def _shortconv(zh, row_off, conv_w, conv_b, batch, seq, cb=256):
    c3 = zh.shape[1]
    c = c3 // 3
    ncb = c // cb
    assert row_off % seq == 0
    ob = row_off // seq
    out = pl.pallas_call(
        _shortconv_kernel,
        grid=(batch, 3, ncb),
        in_specs=[pl.BlockSpec((seq, cb), lambda b, p, j: (ob + b, p * ncb + j)),
                  pl.BlockSpec((3, cb), lambda b, p, j: (0, p * ncb + j)),
                  pl.BlockSpec((1, cb), lambda b, p, j: (0, p * ncb + j))],
        out_specs=pl.BlockSpec((None, None, seq, cb), lambda b, p, j: (p, b, 0, j)),
        out_shape=jax.ShapeDtypeStruct((3, batch, seq, c), F32),
        compiler_params=_cparams(("parallel", "parallel", "parallel")),
        name="hyena_shortconv",
    )(zh, conv_w.astype(F32), conv_b.reshape(1, c3).astype(F32))
    return out


def _filter_kernel(z_ref, w1_ref, b1_ref, w2_ref, b2_ref, w3_ref, b3_ref, fr_ref, w4_ref,
                   t_ref, d_ref, o_ref, h_ref, *, tl):
    i = pl.program_id(0)
    g = pl.program_id(1)
    hi = lax.Precision.HIGHEST

    @pl.when(g == 0)
    def _():
        fr = fr_ref[...]
        h = jnp.sin(fr * (jnp.dot(z_ref[...], w1_ref[...], precision=hi, preferred_element_type=F32) + b1_ref[...]))
        h = jnp.sin(fr * (jnp.dot(h, w2_ref[...], precision=hi, preferred_element_type=F32) + b2_ref[...]))
        h_ref[...] = jnp.sin(fr * (jnp.dot(h, w3_ref[...], precision=hi, preferred_element_type=F32) + b3_ref[...]))

    f = jnp.dot(h_ref[...].astype(BF16), w4_ref[...], preferred_element_type=F32)
    f = f * jnp.exp(-t_ref[...] * d_ref[...])
    row = lax.broadcasted_iota(I32, f.shape, 0) + i * tl
    drop = jnp.logical_and(row == 0, g % 2 == 1)
    o_ref[...] = jnp.where(drop, 0.0, f)


def _hyena_filter_signals(seq, w1, b1, w2, b2, w3, b3, w4, freq, n_ch, tl=512):
    t = jnp.linspace(0.0, 1.0, seq, dtype=F32)[:, None]
    w = 2.0 * math.pi * jnp.arange(seq, dtype=F32)[:, None] / seq
    f = jnp.linspace(1e-4, FILTER_BANDS - 1, FILTER_BANDS, dtype=F32)[None, :]
    z = jnp.concatenate([t, jnp.cos(f * w), -jnp.sin(f * w)], axis=-1)
    deltas = jnp.abs(jnp.linspace(math.log(DECAY_FAST) / DECAY_TARGET,
                                  math.log(DECAY_SLOW) / DECAY_TARGET, n_ch, dtype=F32))[None, :]
    hid = w1.shape[1]
    emb = LANES
    z = jnp.pad(z, ((0, 0), (0, emb - z.shape[1])))
    w1 = jnp.pad(w1.astype(F32), ((0, emb - w1.shape[0]), (0, 0)))
    tl = min(tl, seq)
    full = lambda shape: pl.BlockSpec(shape, lambda i, g: tuple(0 for _ in shape))
    return pl.pallas_call(
        functools.partial(_filter_kernel, tl=tl),
        grid=(seq // tl, 2 * HYENA_ORDER),
        in_specs=[pl.BlockSpec((tl, emb), lambda i, g: (i, 0)),
                  full((emb, hid)), full((1, hid)), full((hid, hid)), full((1, hid)),
                  full((hid, hid)), full((1, hid)), full((1, hid)),
                  pl.BlockSpec((hid, n_ch), lambda i, g: (0, g)),
                  pl.BlockSpec((tl, 1), lambda i, g: (i, 0)),
                  full((1, n_ch))],
        out_specs=pl.BlockSpec((None, tl, n_ch), lambda i, g: (g, i, 0)),
        out_shape=jax.ShapeDtypeStruct((2 * HYENA_ORDER, seq, n_ch), F32),
        scratch_shapes=[pltpu.VMEM((tl, hid), F32)],
        compiler_params=_cparams(("parallel", "arbitrary")),
        name="hyena_filter_mlp",
    )(z, w1.astype(F32), b1.reshape(1, hid).astype(F32), w2.astype(F32), b2.reshape(1, hid).astype(F32),
      w3.astype(F32), b3.reshape(1, hid).astype(F32), freq.reshape(1, hid).astype(F32), w4.astype(BF16),
      t, deltas)


def _dft_tables(r):
    n = r * r
    k2 = jnp.arange(r, dtype=I32)
    n2 = jnp.arange(r // 2, dtype=I32)
    ang1 = (2.0 * math.pi / r) * ((k2[:, None] * n2[None, :]) % r).astype(F32)
    f1 = jnp.concatenate([jnp.cos(ang1), -jnp.sin(ang1)], axis=0)
    k1 = jnp.arange(r, dtype=I32)
    n1 = jnp.arange(r, dtype=I32)
    ang_a = (2.0 * math.pi / r) * ((k1[:, None] * n1[None, :]) % r).astype(F32)
    ang_b = (2.0 * math.pi / n) * (k2[:, None] * n1[None, :]).astype(F32)
    ca, sa = jnp.cos(ang_a)[None], jnp.sin(ang_a)[None]
    cb, sb = jnp.cos(ang_b)[:, None], jnp.sin(ang_b)[:, None]
    mr, mi_ = ca * cb - sa * sb, -(sa * cb + ca * sb)
    mf = jnp.concatenate([jnp.concatenate([mr, -mi_], axis=2),
                          jnp.concatenate([mi_, mr], axis=2)], axis=1)
    minv = jnp.swapaxes(mf, 1, 2)
    g3 = jnp.concatenate([jnp.cos(ang1.T), -jnp.sin(ang1.T)], axis=1) * (1.0 / n)
    eye = jnp.eye(N1_BLOCK, dtype=F32)
    f1, g3 = jnp.kron(f1, eye), jnp.kron(g3, eye)
    return f1.astype(BF16), mf.astype(BF16), minv.astype(BF16), g3.astype(BF16)


N1_BLOCK = SUBLANES


def _pack_complex(re, im):
    rb = lax.bitcast_convert_type(re.astype(BF16).astype(F32), U32)
    ib = lax.bitcast_convert_type(im.astype(BF16).astype(F32), U32)
    return (rb >> 16) | ib


def _unpack_complex_rows(w):
    re, im = _unpack(w)
    return jnp.concatenate([re, im], axis=0).astype(BF16)


def _fft1_kernel(f_ref, x_ref, o_ref, *, r):
    n_ch = x_ref.shape[-1]
    x = x_ref[...].reshape((r // 2) * N1_BLOCK, n_ch).astype(BF16)
    res = jnp.dot(f_ref[...], x, preferred_element_type=F32)
    half = r * N1_BLOCK
    o_ref[...] = _pack_complex(res[:half], res[half:]).reshape(r, N1_BLOCK, n_ch)


def _fft_stage1(x4, part, f1, r, n_ch):
    nb = x4.shape[1]
    xv = x4.reshape(x4.shape[0], nb, r // 2, r, n_ch)
    return pl.pallas_call(
        functools.partial(_fft1_kernel, r=r),
        grid=(nb, r // N1_BLOCK),
        in_specs=[pl.BlockSpec(f1.shape, lambda b, j: (0, 0)),
                  pl.BlockSpec((None, None, r // 2, N1_BLOCK, n_ch), lambda b, j: (part, b, 0, j, 0))],
        out_specs=pl.BlockSpec((None, r, N1_BLOCK, n_ch), lambda b, j: (b, 0, j, 0)),
        out_shape=jax.ShapeDtypeStruct((nb, r, r, n_ch), U32),
        compiler_params=_cparams(("parallel", "parallel")),
        name="hyena_dft_stage1",
    )(f1, xv)


def _k2_block(r):
    return max(1, (2 * LANES) // r)


def _filter_spec_kernel(mf_ref, bf_ref, bb_ref, o_ref, *, r):
    for kk in range(mf_ref.shape[0]):
        m = mf_ref[kk]
        xf = jnp.dot(m, _unpack_complex_rows(bf_ref[kk]), preferred_element_type=F32)
        xb = jnp.dot(m, _unpack_complex_rows(bb_ref[kk]), preferred_element_type=F32)
        o_ref[kk] = _pack_complex(xf[:r] + xb[:r], xf[r:] - xb[r:])


def _filter_spectrum(b1, mf, r, n_ch):
    kb = _k2_block(r)
    blk = lambda sel: pl.BlockSpec((None, kb, r, n_ch), lambda o, k: (2 * o + sel, k, 0, 0))
    return pl.pallas_call(
        functools.partial(_filter_spec_kernel, r=r),
        grid=(HYENA_ORDER, r // kb),
        in_specs=[pl.BlockSpec((kb, 2 * r, 2 * r), lambda o, k: (k, 0, 0)), blk(0), blk(1)],
        out_specs=pl.BlockSpec((None, kb, r, n_ch), lambda o, k: (o, k, 0, 0)),
        out_shape=jax.ShapeDtypeStruct((HYENA_ORDER, r, r, n_ch), U32),
        compiler_params=_cparams(("parallel", "parallel")),
        name="hyena_filter_spectrum",
    )(mf, b1, b1)


def _fft2_kernel(mf_ref, mi_ref, b_ref, h_ref, o_ref, *, r):
    for kk in range(mf_ref.shape[0]):
        x = jnp.dot(mf_ref[kk], _unpack_complex_rows(b_ref[kk]), preferred_element_type=F32)
        xr, xi = x[:r], x[r:]
        hr, hi = _unpack(h_ref[kk])
        y = jnp.concatenate([xr * hr - xi * hi, xr * hi + xi * hr], axis=0).astype(BF16)
        c = jnp.dot(mi_ref[kk], y, preferred_element_type=F32)
        o_ref[kk] = _pack_complex(c[:r], c[r:])


def _fft_stage2(b1, h, order, mf, minv, r, n_ch):
    nb = b1.shape[0]
    kb = _k2_block(r)
    blk = pl.BlockSpec((None, kb, r, n_ch), lambda k, b: (b, k, 0, 0))
    return pl.pallas_call(
        functools.partial(_fft2_kernel, r=r),
        grid=(r // kb, nb),
        in_specs=[pl.BlockSpec((kb, 2 * r, 2 * r), lambda k, b: (k, 0, 0)),
                  pl.BlockSpec((kb, 2 * r, 2 * r), lambda k, b: (k, 0, 0)),
                  blk,
                  pl.BlockSpec((None, kb, r, n_ch), lambda k, b: (order, k, 0, 0))],
        out_specs=blk,
        out_shape=jax.ShapeDtypeStruct((nb, r, r, n_ch), U32),
        compiler_params=_cparams(("parallel", "parallel")),
        name="hyena_dft_stage2",
    )(mf, minv, b1, h)


def _fft3_kernel(g_ref, c_ref, gate_ref, s_ref, bias_ref, ng_ref, *rest, final):
    if final:
        (o_ref,) = rest
    else:
        f_ref, o_ref, b_ref = rest
    r, nb, n_ch = c_ref.shape
    cc = _unpack_complex_rows(c_ref[...].reshape(r * nb, n_ch))
    y = jnp.dot(g_ref[...], cc, preferred_element_type=F32)
    rows = (r // 2) * nb
    s_new = gate_ref[...].reshape(rows, n_ch) * (y + s_ref[...].reshape(rows, n_ch) * bias_ref[...])
    if final:
        ms = jnp.mean(s_new * s_new, axis=-1, keepdims=True)
        s_new = s_new * lax.rsqrt(ms + RMS_EPS) * ng_ref[...]
    o_ref[...] = s_new.reshape(r // 2, nb, n_ch)
    if not final:
        res = jnp.dot(f_ref[...], s_new.astype(BF16), preferred_element_type=F32)
        half = r * nb
        b_ref[...] = _pack_complex(res[:half], res[half:]).reshape(r, nb, n_ch)


def _fft_stage3(c2, g3, z4, gate_part, s4, s_part, bias, norm_g, f1, r, n_ch):
    final = f1 is None
    nb = c2.shape[0]
    zv = z4.reshape(z4.shape[0], nb, r // 2, r, n_ch)
    sv = s4.reshape(s4.shape[0], nb, r // 2, r, n_ch)
    dspec = lambda part: pl.BlockSpec((None, None, r // 2, N1_BLOCK, n_ch), lambda b, j: (part, b, 0, j, 0))
    packed = pl.BlockSpec((None, r, N1_BLOCK, n_ch), lambda b, j: (b, 0, j, 0))
    s_spec = pl.BlockSpec((None, r // 2, N1_BLOCK, n_ch), lambda b, j: (b, 0, j, 0))
    s_shape = jax.ShapeDtypeStruct((nb, r // 2, r, n_ch), F32)
    in_specs = [pl.BlockSpec(g3.shape, lambda b, j: (0, 0)), packed, dspec(gate_part), dspec(s_part),
                pl.BlockSpec((1, n_ch), lambda b, j: (0, 0)), pl.BlockSpec((1, n_ch), lambda b, j: (0, 0))]
    args = [g3, c2, zv, sv, bias.reshape(1, n_ch).astype(F32), norm_g.reshape(1, n_ch).astype(F32)]
    if final:
        out_specs, out_shape = s_spec, s_shape
    else:
        in_specs.append(pl.BlockSpec(f1.shape, lambda b, j: (0, 0)))
        args.append(f1)
        out_specs = [s_spec, packed]
        out_shape = [s_shape, jax.ShapeDtypeStruct((nb, r, r, n_ch), U32)]
    return pl.pallas_call(
        functools.partial(_fft3_kernel, final=final),
        grid=(nb, r // N1_BLOCK),
        in_specs=in_specs,
        out_specs=out_specs,
        out_shape=out_shape,
        compiler_params=_cparams(("parallel", "parallel")),
        name="hyena_dft_stage3",
    )(*args)


def _hyena(zh, row_off, batch, seq, conv_w, conv_b, filt, fft_bias, out_g):
    n_ch = zh.shape[1] // 3
    r = int(round(math.sqrt(2 * seq)))
    assert r * r == 2 * seq and r % 16 == 0
    f1, mf, minv, g3 = _dft_tables(r)
    sig = _hyena_filter_signals(seq, *filt, n_ch=n_ch)
    hb1 = _fft_stage1(sig[None], 0, f1, r, n_ch)
    h = _filter_spectrum(hb1, mf, r, n_ch)
    z4 = _shortconv(zh, row_off, conv_w, conv_b, batch, seq)
    s4, s_part = z4, 2
    b1 = _fft_stage1(s4, s_part, f1, r, n_ch)
    for o in range(HYENA_ORDER):
        c2 = _fft_stage2(b1, h, o, mf, minv, r, n_ch)
        if o < HYENA_ORDER - 1:
            s, b1 = _fft_stage3(c2, g3, z4, o, s4, s_part, fft_bias[o], out_g, f1, r, n_ch)
        else:
            s = _fft_stage3(c2, g3, z4, o, s4, s_part, fft_bias[o], out_g, None, r, n_ch)
        s4, s_part = s.reshape(1, batch, seq, n_ch), 0
    return s4.reshape(batch * seq, n_ch)


def _extract_top(s, key, count):
    vals, keys = [], []
    for _ in range(count):
        m = jnp.max(s, axis=0, keepdims=True)
        kmin = jnp.min(jnp.where(s == m, key, jnp.inf), axis=0, keepdims=True)
        s = jnp.where(key == kmin, -jnp.inf, s)
        vals.append(m)
        keys.append(kmin)
    return vals, keys


def _peer_topk_kernel(q_ref, keys_ref, eid_ref, gate_ref):
    t = q_ref.shape[0]
    nk = PEER_NKEYS
    q = q_ref[...]
    row_key = lax.broadcasted_iota(I32, (nk, t), 0).astype(F32)
    tops = []
    for c in range(2):
        s = lax.dot_general(keys_ref[c], q[:, c * nk:(c + 1) * nk], (((1,), (1,)), ((), ())),
                            preferred_element_type=F32)
        tops.append(_extract_top(s, row_key, PEER_TOPK))
    (v1, i1), (v2, i2) = tops
    rows16 = lax.broadcasted_iota(I32, (PEER_TOPK, t), 0)
    v2a = jnp.zeros((PEER_TOPK, t), F32)
    i2a = jnp.zeros((PEER_TOPK, t), F32)
    for j in range(PEER_TOPK):
        v2a = jnp.where(rows16 == j, v2[j], v2a)
        i2a = jnp.where(rows16 == j, i2[j], i2a)
    n_exp = float(nk * nk)
    half = PEER_TOPK // 2
    rows8 = lax.broadcasted_iota(I32, (half, t), 0)
    pos8 = rows8.astype(F32)
    v2h, i2h = v2a[:half], i2a[:half]
    cand = [v1[0] + v2a]
    ckey = [rows16.astype(F32) * n_exp + (i1[0] * float(nk) + i2a)]
    for i in range(1, half):
        cand.append(jnp.where(rows8 < PEER_TOPK // (i + 1), v1[i] + v2h, -jnp.inf))
        ckey.append((pos8 + float(i * PEER_TOPK)) * n_exp + (i1[i] * float(nk) + i2h))
    v1t = jnp.zeros((half, t), F32)
    i1t = jnp.zeros((half, t), F32)
    for r in range(half):
        v1t = jnp.where(rows8 == r, v1[half + r], v1t)
        i1t = jnp.where(rows8 == r, i1[half + r], i1t)
    cand.append(v1t + v2[0])
    ckey.append((pos8 + float(half)) * (PEER_TOPK * n_exp) + (i1t * float(nk) + i2[0]))
    tv, tk_ = _extract_top(jnp.concatenate(cand, axis=0), jnp.concatenate(ckey, axis=0), PEER_TOPK)
    denom = jnp.zeros((1, t), F32)
    es = []
    for k in range(PEER_TOPK):
        e = jnp.exp(tv[k] - tv[0])
        es.append(e)
        denom = denom + e
    eid = jnp.zeros((PEER_TOPK, t), F32)
    gate = jnp.zeros((PEER_TOPK, t), F32)
    for k in range(PEER_TOPK):
        pos = jnp.floor(tk_[k] * (1.0 / n_exp))
        eid = jnp.where(rows16 == k, tk_[k] - pos * n_exp, eid)
        gate = jnp.where(rows16 == k, es[k] / denom, gate)
    eid_ref[...] = eid.astype(I32)
    gate_ref[...] = gate


def _peer_topk(q, keys, tm=512):
    t = q.shape[0]
    tm = min(tm, t)
    out_spec = pl.BlockSpec((PEER_TOPK, tm), lambda i, h: (h, i))
    return pl.pallas_call(
        _peer_topk_kernel,
        grid=(t // tm, PEER_HEADS),
        in_specs=[pl.BlockSpec((tm, 2 * PEER_NKEYS), lambda i, h: (i, h)),
                  pl.BlockSpec((None, 2, PEER_NKEYS, PEER_NKEYS), lambda i, h: (h, 0, 0, 0))],
        out_specs=[out_spec, out_spec],
        out_shape=[jax.ShapeDtypeStruct((PEER_HEADS * PEER_TOPK, t), I32),
                   jax.ShapeDtypeStruct((PEER_HEADS * PEER_TOPK, t), F32)],
        compiler_params=_cparams(("parallel", "parallel")),
        name="peer_topk",
    )(q, keys)


def _pack_table(tab):
    e, d = tab.shape
    assert d == 2 * SUBLANES * LANES
    bits = lax.bitcast_convert_type(tab.astype(BF16), jnp.uint16).astype(U32)
    packed = bits[:, :d // 2] | (bits[:, d // 2:] << 16)
    return packed.reshape(e * SUBLANES, LANES)


def _unpack(w):
    lo = lax.bitcast_convert_type(w << 16, F32)
    hi = lax.bitcast_convert_type(w & jnp.uint32(0xFFFF0000), F32)
    return lo, hi


_BITREV8 = (0, 4, 2, 6, 1, 5, 3, 7)


def _sublane_fold8(parts):
    sub = lax.broadcasted_iota(I32, (2 * SUBLANES, LANES), 0) // 2

    def rolled(a, shift):
        return pltpu.bitcast(pltpu.roll(pltpu.bitcast(a, U32), shift, axis=0), BF16)

    lvl = [parts[_BITREV8[r]] for r in range(8)]
    for shift, mask in ((4, sub < 4), (2, (sub % 4) < 2), (1, (sub % 2) < 1)):
        nxt = []
        for a, b in zip(lvl[0::2], lvl[1::2]):
            nxt.append(jnp.where(mask, a + rolled(a, SUBLANES - shift), b + rolled(b, shift)))
        lvl = nxt
    return lvl[0]


def _pair_ranges(half):
    if half == 0:
        return (0, PEER_WINDOW), (PEER_WINDOW, PEER_PAIRS)
    return (PEER_PAIRS - PEER_WINDOW, PEER_PAIRS), (0, PEER_PAIRS - PEER_WINDOW)


def _overflow(n_low, half):
    return n_low > PEER_WINDOW if half == 0 else n_low < PEER_PAIRS - PEER_WINDOW


def _table_rows(tab_ref, off):
    return _unpack(tab_ref[pl.ds(pl.multiple_of(off, SUBLANES), SUBLANES), :])


def _peer_u_kernel(off_ref, nlow_ref, tab_ref, x_ref, o_ref, extra_ref, *, tm, half):
    lane = lax.broadcasted_iota(I32, (SUBLANES, LANES), 1)
    sub = lax.broadcasted_iota(I32, (SUBLANES, LANES), 0)
    lane_grp = lax.shift_right_logical(lane, 3)
    diag = sub == (lane & (SUBLANES - 1))
    main, rest = _pair_ranges(half)

    def folded(t, p0, p1):
        xb = pltpu.bitcast(x_ref[t], BF16)
        out = []
        for g in range(p0 // SUBLANES, p1 // SUBLANES):
            parts = []
            for r in range(SUBLANES):
                off = pl.multiple_of(off_ref[t, g * SUBLANES + r], SUBLANES)
                parts.append(pltpu.bitcast(tab_ref[pl.ds(off, SUBLANES), :], BF16) * xb)
            out.append(pltpu.bitcast(_sublane_fold8(parts), U32))
        return out

    def lane_sums(folds, p0):
        mat = jnp.zeros((SUBLANES, LANES), F32)
        for i, f in enumerate(folds):
            lo, hi = _unpack(f)
            mat = jnp.where(lane_grp == p0 // SUBLANES + i, jnp.sum(lo + hi, axis=-1, keepdims=True), mat)
        return jnp.sum(jnp.where(diag, mat, 0.0), axis=0, keepdims=True)

    def finish(t, folds):
        o_ref[pl.ds(t, 1), :] = lane_sums(folds, main[0]) + extra_ref[pl.ds(t, 1), :]

    def token(t, prev):
        cur = folded(t, *main)
        finish(jnp.maximum(t - 1, 0), prev)
        extra_ref[pl.ds(t, 1), :] = jnp.zeros((1, LANES), F32)

        @pl.when(_overflow(nlow_ref[0, t], half))
        def _():
            extra_ref[pl.ds(t, 1), :] = lane_sums(folded(t, *rest), rest[0])

        return tuple(cur)

    extra_ref[pl.ds(0, 1), :] = jnp.zeros((1, LANES), F32)
    zeros = tuple(jnp.zeros((SUBLANES, LANES), U32) for _ in range((main[1] - main[0]) // SUBLANES))
    last = lax.fori_loop(0, tm, token, zeros)
    finish(tm - 1, last)


def _peer_u(off, nlow3, tab, x4, half, tm=128):
    t = off.shape[0]
    rows = PEER_HALF * SUBLANES
    return pl.pallas_call(
        functools.partial(_peer_u_kernel, tm=tm, half=half),
        grid=(t // tm,),
        in_specs=[pl.BlockSpec((tm, LANES), lambda i: (i, 0), memory_space=pltpu.SMEM),
                  pl.BlockSpec((None, 1, tm), lambda i: (i, 0, 0), memory_space=pltpu.SMEM),
                  pl.BlockSpec((rows, LANES), lambda i: (half, 0), pipeline_mode=pl.Buffered(1)),
                  pl.BlockSpec((tm, SUBLANES, LANES), lambda i: (i, 0, 0))],
        out_specs=pl.BlockSpec((tm, LANES), lambda i: (i, 0)),
        out_shape=jax.ShapeDtypeStruct((t, LANES), F32),
        scratch_shapes=[pltpu.VMEM((tm, LANES), F32)],
        compiler_params=_cparams(("arbitrary",)),
        name="peer_expert_scores",
    )(off, nlow3, tab, x4)


def _peer_coef_kernel(a0_ref, a1_ref, eid_ref, slot_ref, gate_ref, c0_ref, c1_ref):
    low = eid_ref[...] < PEER_HALF
    a = jnp.where(low, a0_ref[...], a1_ref[...])
    gate = jnp.take_along_axis(gate_ref[...], slot_ref[...], axis=1)
    coef = gate * (0.5 * a * (1.0 + lax.erf(a * (1.0 / math.sqrt(2.0)))))
    c0_ref[...] = jnp.where(low, coef, 0.0)
    c1_ref[...] = jnp.where(low, 0.0, coef)


def _peer_coef(a0, a1, eid, slot, gate, tm=1024):
    t = eid.shape[0]
    tm = min(tm, t)
    spec = pl.BlockSpec((tm, LANES), lambda i: (i, 0))
    return pl.pallas_call(
        _peer_coef_kernel,
        grid=(t // tm,),
        in_specs=[spec, spec, spec, spec, spec],
        out_specs=[spec, spec],
        out_shape=[jax.ShapeDtypeStruct((t, LANES), F32)] * 2,
        compiler_params=_cparams(("parallel",)),
        name="peer_coef",
    )(a0, a1, eid, slot, gate)


def _peer_v_kernel(off_ref, nlow_ref, coef_ref, tab_ref, base_ref, *o_refs, tm, half, n_first):
    n_acc = 4
    main, rest = _pair_ranges(half)

    def weighted(t, p0, p1):
        acc_lo = [jnp.zeros((SUBLANES, LANES), F32) for _ in range(n_acc)]
        acc_hi = [jnp.zeros((SUBLANES, LANES), F32) for _ in range(n_acc)]
        for p in range(p0, p1):
            c = coef_ref[t, p]
            lo, hi = _table_rows(tab_ref, off_ref[t, p])
            acc_lo[p % n_acc] = acc_lo[p % n_acc] + c * lo
            acc_hi[p % n_acc] = acc_hi[p % n_acc] + c * hi
        return ((acc_lo[0] + acc_lo[1]) + (acc_lo[2] + acc_lo[3]),
                (acc_hi[0] + acc_hi[1]) + (acc_hi[2] + acc_hi[3]))

    def run(o_ref):
        def store_row(t, lo, hi):
            o_ref[t, 0] = base_ref[t, 0] + lo
            o_ref[t, 1] = base_ref[t, 1] + hi

        def token(t, carry):
            lo, hi = weighted(t, *main)
            store_row(t, lo, hi)

            @pl.when(_overflow(nlow_ref[0, t], half))
            def _():
                lo2, hi2 = weighted(t, *rest)
                store_row(t, lo + lo2, hi + hi2)

            return carry

        lax.fori_loop(0, tm, token, 0)

    if len(o_refs) == 1:
        run(o_refs[0])
    else:
        first = pl.program_id(0) < n_first
        pl.when(first)(lambda: run(o_refs[0]))
        pl.when(jnp.logical_not(first))(lambda: run(o_refs[1]))


def _peer_v(off, nlow3, coef, tab, base, half, tm=128, split=None):
    t = base.shape[0]
    rows = PEER_HALF * SUBLANES
    smem = pl.BlockSpec((tm, LANES), lambda i: (i, 0), memory_space=pltpu.SMEM)
    tile = pl.BlockSpec((tm, 2, SUBLANES, LANES), lambda i: (i, 0, 0, 0))
    if split is None:
        n_first, out_specs = 0, tile
        out_shape = jax.ShapeDtypeStruct((t, 2, SUBLANES, LANES), F32)
    else:
        assert split % tm == 0
        n_first, n_rest = split // tm, (t - split) // tm
        blk = (tm, 2, SUBLANES, LANES)
        out_specs = [pl.BlockSpec(blk, lambda i: (jnp.clip(i, 0, n_first - 1), 0, 0, 0)),
                     pl.BlockSpec(blk, lambda i: (jnp.clip(i - n_first, 0, n_rest - 1), 0, 0, 0))]
        out_shape = [jax.ShapeDtypeStruct((split, 2, SUBLANES, LANES), F32),
                     jax.ShapeDtypeStruct((t - split, 2, SUBLANES, LANES), F32)]
    return pl.pallas_call(
        functools.partial(_peer_v_kernel, tm=tm, half=half, n_first=n_first),
        grid=(t // tm,),
        in_specs=[smem,
                  pl.BlockSpec((None, 1, tm), lambda i: (i, 0, 0), memory_space=pltpu.SMEM),
                  smem,
                  pl.BlockSpec((rows, LANES), lambda i: (half, 0), pipeline_mode=pl.Buffered(1)),
                  tile],
        out_specs=out_specs,
        out_shape=out_shape,
        compiler_params=_cparams(("arbitrary",)),
        name="peer_expert_sum",
    )(off, nlow3, coef, tab, base)


def _peer(x1, split, norm2_g, wq_bf, keys_bf, u_packed, v_packed, tm=128):
    t, d = x1.shape
    tm = min(tm, t)
    (xn_bf,) = _rmsnorm(x1, norm2_g, (BF16,))
    q = _matmul(xn_bf, wq_bf, wq_bf.shape[1], 0, BF16)
    eid_t, gate_t = _peer_topk(q, keys_bf)
    eid, gate = eid_t.T, gate_t.T
    n_exp = PEER_NKEYS * PEER_NKEYS
    high = (eid >= PEER_HALF).astype(I32)
    nlow3 = (PEER_PAIRS - jnp.sum(high, axis=-1, dtype=I32)).reshape(t // tm, 1, tm)
    slot = lax.broadcasted_iota(I32, eid.shape, 1)
    order = jnp.sort((high * PEER_PAIRS + slot) * n_exp + eid, axis=-1)
    eid = order % n_exp
    slot = (order // n_exp) % PEER_PAIRS
    off = (eid & (PEER_HALF - 1)) * SUBLANES
    xw = _pack_table(xn_bf).reshape(t, SUBLANES, LANES)
    a0 = _peer_u(off, nlow3, u_packed, xw, 0, tm)
    a1 = _peer_u(off, nlow3, u_packed, xw, 1, tm)
    c0, c1 = _peer_coef(a0, a1, eid, slot, gate)
    y = _peer_v(off, nlow3, c0, v_packed, x1.reshape(t, 2, SUBLANES, LANES), 0, tm)
    ya, yb = _peer_v(off, nlow3, c1, v_packed, y, 1, tm, split=split)
    return ya.reshape(split, d), yb.reshape(t - split, d)


def kernel(x_prompt, x_sample, norm1_g, w_in, q_norm_g, k_norm_g, lambda_q1, lambda_k1, lambda_q2, lambda_k2, attn_sub_g, conv_w, conv_b, filt_w1, filt_b1, filt_w2, filt_b2, filt_w3, filt_b3, filt_w4, filt_freq, fft_bias, hyena_out_g, w_out, norm2_g, peer_wq, peer_keys, peer_u, peer_v):
    depth = w_in.shape[0]
    d_model = x_prompt.shape[-1]
    att_w = ATT_HEADS * ATT_VDIM
    shapes = [x_prompt.shape[:2], x_sample.shape[:2]]
    xs = [x_prompt.reshape(-1, d_model), x_sample.reshape(-1, d_model)]
    n0 = xs[0].shape[0]
    slopes = 2.0 ** (-8.0 * jnp.arange(1, ATT_HEADS + 1, dtype=F32) / ATT_HEADS)

    for l in range(depth):
        lambda_init = 0.8 - 0.6 * math.exp(-0.3 * l)
        lam = (jnp.exp(jnp.sum(lambda_q1[l].astype(F32) * lambda_k1[l].astype(F32)))
               - jnp.exp(jnp.sum(lambda_q2[l].astype(F32) * lambda_k2[l].astype(F32)))
               + lambda_init).reshape(1)
        w_in_bf = w_in[l].astype(BF16)
        q_gain = jnp.tile(q_norm_g[l].astype(F32), 2 * ATT_HEADS) * (ATT_QKDIM ** -0.5 * LOG2E)
        k_gain = jnp.tile(k_norm_g[l].astype(F32), 2 * ATT_HEADS)
        qk_gain = jnp.concatenate([q_gain, k_gain]).reshape(1, 2 * att_w)

        h_bf = _rmsnorm_stacked(xs[0], xs[1], norm1_g[l], BF16)
        qk = _matmul(h_bf, w_in_bf, 2 * att_w, 0, BF16, mode="qknorm", extra=qk_gain)
        vt = _matmul_nt(w_in_bf[:, 2 * att_w:3 * att_w].T, h_bf, BF16)
        zh = _matmul(h_bf, w_in_bf, w_in.shape[2] - 3 * att_w, 3 * att_w, F32)

        filt = (filt_w1[l], filt_b1[l], filt_w2[l], filt_b2[l], filt_w3[l], filt_b3[l], filt_w4[l], filt_freq[l])
        segs, row = [], 0
        for (b, s), x_seg in zip(shapes, xs):
            att = _attention(qk, vt, row, slopes, lam, attn_sub_g[l], b, s, 1.0 - lambda_init)
            hy = _hyena(zh, row, b, s, conv_w[l], conv_b[l], filt, fft_bias[l], hyena_out_g[l])
            segs.append((att, hy, x_seg))
            row += b * s
        x1 = _outproj_stacked(segs[0], segs[1], w_out[l].astype(BF16))

        xs = _peer(x1, n0, norm2_g[l], peer_wq[l].astype(BF16), peer_keys[l].astype(BF16),
                   _pack_table(peer_u[l]), _pack_table(peer_v[l]))

    return (xs[0].reshape(x_prompt.shape), xs[1].reshape(x_sample.shape))
```

```python
import functools
import math

import jax
import jax.numpy as jnp
from jax import lax
from jax.experimental import pallas as pl
from jax.experimental.pallas import tpu as pltpu

F32 = jnp.float32
BF16 = jnp.bfloat16
I32 = jnp.int32
U32 = jnp.uint32

RMS_EPS = 1e-6
LOG2E = 1.4426950408889634
LANES = 128
SUBLANES = 8
VMEM_LIMIT_BYTES = 56 * 1024 * 1024

ATT_HEADS = 8
ATT_VDIM = 128
ATT_QKDIM = 64
HYENA_ORDER = 2
FILTER_BANDS = 16
DECAY_FAST = 0.3
DECAY_SLOW = 1.5
DECAY_TARGET = 1e-2
PEER_HEADS = 8
PEER_NKEYS = 128
PEER_TOPK = 16
PEER_HALF = PEER_NKEYS * PEER_NKEYS // 2
PEER_PAIRS = PEER_HEADS * PEER_TOPK
PEER_WINDOW = 80


def _cparams(sem, vmem=VMEM_LIMIT_BYTES):
    return pltpu.CompilerParams(dimension_semantics=sem, vmem_limit_bytes=vmem)


def _segment_spec(block, start, count, minor=0):
    return pl.BlockSpec(block, lambda i: (jnp.clip(i - start, 0, count - 1), minor))


def _rmsnorm_kernel(x_ref, g_ref, *o_refs):
    x = x_ref[...]
    ms = jnp.mean(x * x, axis=-1, keepdims=True)
    y = x * lax.rsqrt(ms + RMS_EPS) * g_ref[...]
    for o_ref in o_refs:
        o_ref[...] = y.astype(o_ref.dtype)


def _rmsnorm2_kernel(xa_ref, xb_ref, g_ref, o_ref, *, na):
    first = pl.program_id(0) < na
    x = jnp.where(first, xa_ref[...], xb_ref[...])
    ms = jnp.mean(x * x, axis=-1, keepdims=True)
    o_ref[...] = (x * lax.rsqrt(ms + RMS_EPS) * g_ref[...]).astype(o_ref.dtype)


def _rmsnorm_stacked(xa, xb, g, out_dtype, tm=512):
    d = xa.shape[1]
    na, nb = xa.shape[0] // tm, xb.shape[0] // tm
    assert xa.shape[0] % tm == 0 and xb.shape[0] % tm == 0
    return pl.pallas_call(
        functools.partial(_rmsnorm2_kernel, na=na),
        grid=(na + nb,),
        in_specs=[_segment_spec((tm, d), 0, na), _segment_spec((tm, d), na, nb),
                  pl.BlockSpec((1, d), lambda i: (0, 0))],
        out_specs=pl.BlockSpec((tm, d), lambda i: (i, 0)),
        out_shape=jax.ShapeDtypeStruct((xa.shape[0] + xb.shape[0], d), out_dtype),
        compiler_params=_cparams(("parallel",)),
        name="rmsnorm_stacked",
    )(xa, xb, g.reshape(1, d).astype(F32))


def _rmsnorm(x, g, out_dtypes, tm=512):
    t, d = x.shape
    tm = min(tm, t)
    spec = pl.BlockSpec((tm, d), lambda i: (i, 0))
    return pl.pallas_call(
        _rmsnorm_kernel,
        grid=(t // tm,),
        in_specs=[spec, pl.BlockSpec((1, d), lambda i: (0, 0))],
        out_specs=[spec for _ in out_dtypes],
        out_shape=[jax.ShapeDtypeStruct((t, d), dt) for dt in out_dtypes],
        compiler_params=_cparams(("parallel",)),
        name="rmsnorm",
    )(x, g.reshape(1, d).astype(F32))


def _group_rms_scale(x, gain):
    lane = lax.broadcasted_iota(I32, x.shape, 1)
    lo = lane < ATT_QKDIM
    x2 = x * x
    s_lo = jnp.sum(jnp.where(lo, x2, 0.0), axis=-1, keepdims=True)
    s_hi = jnp.sum(jnp.where(lo, 0.0, x2), axis=-1, keepdims=True)
    ms = jnp.where(lo, s_lo, s_hi) * (1.0 / ATT_QKDIM)
    return x * lax.rsqrt(ms + RMS_EPS) * gain


def _mm_kernel(a_ref, b_ref, *rest, mode):
    acc = jnp.dot(a_ref[...], b_ref[...], preferred_element_type=F32)
    if mode == "plain":
        (o_ref,) = rest
        o_ref[...] = acc.astype(o_ref.dtype)
    elif mode == "qknorm":
        g_ref, o_ref = rest
        for c in range(acc.shape[1] // LANES):
            sl = slice(c * LANES, (c + 1) * LANES)
            o_ref[:, sl] = _group_rms_scale(acc[:, sl], g_ref[:, sl]).astype(o_ref.dtype)
    else:
        raise ValueError(mode)


def _matmul(a, b, n_cols, col_off, out_dtype, mode="plain", extra=None, tm=1024, tn=1024):
    m, k = a.shape
    tm = min(tm, m)
    tn = min(tn, n_cols)
    assert col_off % tn == 0 and n_cols % tn == 0 and m % tm == 0
    off = col_off // tn
    in_specs = [pl.BlockSpec((tm, k), lambda i, j: (i, 0)),
                pl.BlockSpec((k, tn), lambda i, j: (0, j + off))]
    args = [a, b]
    if mode == "qknorm":
        in_specs.append(pl.BlockSpec((1, tn), lambda i, j: (0, j)))
        args.append(extra)
    return pl.pallas_call(
        functools.partial(_mm_kernel, mode=mode),
        grid=(m // tm, n_cols // tn),
        in_specs=in_specs,
        out_specs=pl.BlockSpec((tm, tn), lambda i, j: (i, j)),
        out_shape=jax.ShapeDtypeStruct((m, n_cols), out_dtype),
        compiler_params=_cparams(("parallel", "arbitrary")),
        name="matmul_" + mode,
    )(*args)


def _outproj_kernel(att_a, hy_a, x_a, att_b, hy_b, x_b, w_ref, o_ref, *, na):
    kw = att_a.shape[1]

    def run(att_ref, hy_ref, x_ref):
        acc = jnp.dot(att_ref[...], w_ref[:kw, :], preferred_element_type=F32)
        acc = acc + jnp.dot(hy_ref[...].astype(BF16), w_ref[kw:, :], preferred_element_type=F32)
        o_ref[...] = acc + x_ref[...]

    first = pl.program_id(0) < na
    pl.when(first)(lambda: run(att_a, hy_a, x_a))
    pl.when(jnp.logical_not(first))(lambda: run(att_b, hy_b, x_b))


def _outproj_stacked(seg_a, seg_b, w, tm=512, tn=1024):
    n_out = w.shape[1]
    na, nb = seg_a[0].shape[0] // tm, seg_b[0].shape[0] // tm
    assert seg_a[0].shape[0] % tm == 0 and seg_b[0].shape[0] % tm == 0 and n_out % tn == 0

    def specs(seg, start, count):
        att, hy, x = seg
        row = lambda i, j: (jnp.clip(i - start, 0, count - 1), 0)
        return [pl.BlockSpec((tm, att.shape[1]), row), pl.BlockSpec((tm, hy.shape[1]), row),
                pl.BlockSpec((tm, tn), lambda i, j: (jnp.clip(i - start, 0, count - 1), j))]

    return pl.pallas_call(
        functools.partial(_outproj_kernel, na=na),
        grid=(na + nb, n_out // tn),
        in_specs=specs(seg_a, 0, na) + specs(seg_b, na, nb) + [pl.BlockSpec((w.shape[0], tn), lambda i, j: (0, j))],
        out_specs=pl.BlockSpec((tm, tn), lambda i, j: (i, j)),
        out_shape=jax.ShapeDtypeStruct(((na + nb) * tm, n_out), F32),
        compiler_params=_cparams(("parallel", "arbitrary")),
        name="out_projection",
    )(*seg_a, *seg_b, w)


def _mm_nt_kernel(w_ref, a_ref, o_ref):
    o_ref[...] = lax.dot_general(w_ref[...], a_ref[...], (((1,), (1,)), ((), ())),
                                 preferred_element_type=F32).astype(o_ref.dtype)


def _matmul_nt(wt, a, out_dtype, tm=1024):
    n, k = wt.shape
    m = a.shape[0]
    tm = min(tm, m)
    return pl.pallas_call(
        _mm_nt_kernel,
        grid=(m // tm,),
        in_specs=[pl.BlockSpec((n, k), lambda i: (0, 0)), pl.BlockSpec((tm, k), lambda i: (i, 0))],
        out_specs=pl.BlockSpec((n, tm), lambda i: (0, i)),
        out_shape=jax.ShapeDtypeStruct((n, m), out_dtype),
        compiler_params=_cparams(("parallel",)),
        name="matmul_nt",
    )(wt, a)


def _alibi_columns(slopes, tq, tk):
    return [_alibi_side(slopes, tq, True), _alibi_side(slopes, tk, False)]


def _alibi_side(slopes, n, query_side):
    pos = jnp.arange(n, dtype=F32)
    val = (slopes.astype(F32) * LOG2E)[:, None] * pos[None, :]

    def pieces(x):
        p1 = x.astype(BF16)
        r1 = x - p1.astype(F32)
        p2 = r1.astype(BF16)
        p3 = (r1 - p2.astype(F32)).astype(BF16)
        return [p1, p2, p3]

    ones = [jnp.ones_like(val, BF16)] * 3
    six = jnp.stack(pieces(-val) + ones if query_side else ones + pieces(val), axis=-1)
    pad = jnp.zeros(val.shape + (ATT_QKDIM - 6,), BF16)
    return jnp.concatenate([six, pad, six, pad], axis=-1)


def _attn_kernel(slope_ref, lam_ref, q_ref, k_ref, vt_ref, aq_ref, ak_ref, g_ref, o_ref,
                 m_ref, l_ref, acc_ref, *, tq, tk, nk, hb, out_scale):
    hg = pl.program_id(1)
    i = pl.program_id(2)
    j = pl.program_id(3)

    @pl.when(j == 0)
    def _():
        m_ref[...] = jnp.full(m_ref.shape, -jnp.inf, F32)
        l_ref[...] = jnp.zeros(l_ref.shape, F32)
        acc_ref[...] = jnp.zeros(acc_ref.shape, F32)

    q_first = lax.broadcasted_iota(I32, (tq, LANES), 1) < ATT_QKDIM
    k_first = lax.broadcasted_iota(I32, (tk, LANES), 1) < ATT_QKDIM
    nt = (((1,), (1,)), ((), ()))

    def update(hh, scores, shift):
        vt = vt_ref[hh * LANES:(hh + 1) * LANES, :]
        for c, s in enumerate(scores):
            m_old = m_ref[hh, c]
            m_new = jnp.maximum(m_old, jnp.max(s, axis=0, keepdims=True) + shift)
            alpha = jnp.exp2(m_old - m_new)
            p = jnp.exp2(s - (m_new - shift))
            l_ref[hh, c] = alpha * l_ref[hh, c] + jnp.sum(p, axis=0, keepdims=True)
            acc_ref[hh, c] = alpha * acc_ref[hh, c] + jnp.dot(vt, p.astype(BF16),
                                                              preferred_element_type=F32)
            m_ref[hh, c] = m_new

    keys_before = i * tq >= (j + 1) * tk
    keys_after = (i + 1) * tq <= j * tk
    off_diagonal = jnp.logical_or(keys_before, keys_after)

    @pl.when(off_diagonal)
    def _():
        sign = jnp.where(keys_after, -1.0, 1.0).astype(BF16)
        gap = jnp.abs(i * tq - j * tk).astype(F32)
        for hh in range(hb):
            sl = slice(hh * LANES, (hh + 1) * LANES)
            q = q_ref[:, sl]
            k = k_ref[:, sl]
            aq = aq_ref[hh]
            ak = ak_ref[hh] * sign
            s0 = lax.dot_general(jnp.where(k_first, k, ak), jnp.where(q_first, q, aq), nt,
                                 preferred_element_type=F32)
            s1 = lax.dot_general(jnp.where(k_first, ak, k), jnp.where(q_first, aq, q), nt,
                                 preferred_element_type=F32)
            update(hh, (s0, s1), -(slope_ref[hg * hb + hh] * LOG2E) * gap)

    @pl.when(jnp.logical_not(off_diagonal))
    def _():
        kpos = lax.broadcasted_iota(I32, (tk, tq), 0) + j * tk
        qpos = lax.broadcasted_iota(I32, (tk, tq), 1) + i * tq
        dist = jnp.abs(kpos - qpos).astype(F32)
        for hh in range(hb):
            sl = slice(hh * LANES, (hh + 1) * LANES)
            q = q_ref[:, sl]
            k = k_ref[:, sl]
            zero = jnp.zeros_like(q)
            bias = dist * (-(slope_ref[hg * hb + hh] * LOG2E))
            s0 = lax.dot_general(k, jnp.where(q_first, q, zero), nt, preferred_element_type=F32) + bias
            s1 = lax.dot_general(k, jnp.where(q_first, zero, q), nt, preferred_element_type=F32) + bias
            update(hh, (s0, s1), 0.0)

    @pl.when(j == nk - 1)
    def _():
        for hh in range(hb):
            o = acc_ref[hh, 0] / l_ref[hh, 0] - lam_ref[0] * (acc_ref[hh, 1] / l_ref[hh, 1])
            ms = jnp.mean(o * o, axis=0, keepdims=True)
            y = o * lax.rsqrt(ms + RMS_EPS) * (g_ref[...] * out_scale)
            o_ref[:, hh * LANES:(hh + 1) * LANES] = y.T.astype(o_ref.dtype)


def _attention(qk, vt, row_off, slopes, lam, sub_g, batch, seq, out_scale, tq=512, tk=1024, hb=4):
    tq = min(tq, seq)
    tk = min(tk, seq // 4)
    nq, nk = seq // tq, seq // tk
    assert row_off % tq == 0 and row_off % tk == 0 and ATT_HEADS % hb == 0
    oq, ok = row_off // tq, row_off // tk
    ng = ATT_HEADS // hb
    aq, ak = _alibi_columns(slopes, tq, tk)
    kern = functools.partial(_attn_kernel, tq=tq, tk=tk, nk=nk, hb=hb, out_scale=out_scale)
    smem = pl.BlockSpec(memory_space=pltpu.SMEM)
    return pl.pallas_call(
        kern,
        grid=(batch, ng, nq, nk),
        in_specs=[smem, smem,
                  pl.BlockSpec((tq, hb * LANES), lambda b, h, i, j: (oq + b * nq + i, h)),
                  pl.BlockSpec((tk, hb * LANES), lambda b, h, i, j: (ok + b * nk + j, ng + h)),
                  pl.BlockSpec((hb * LANES, tk), lambda b, h, i, j: (h, ok + b * nk + j)),
                  pl.BlockSpec((hb, tq, LANES), lambda b, h, i, j: (h, 0, 0)),
                  pl.BlockSpec((hb, tk, LANES), lambda b, h, i, j: (h, 0, 0)),
                  pl.BlockSpec((LANES, 1), lambda b, h, i, j: (0, 0))],
        out_specs=pl.BlockSpec((tq, hb * LANES), lambda b, h, i, j: (b * nq + i, h)),
        out_shape=jax.ShapeDtypeStruct((batch * seq, ATT_HEADS * ATT_VDIM), BF16),
        scratch_shapes=[pltpu.VMEM((hb, 2, 1, tq), F32), pltpu.VMEM((hb, 2, 1, tq), F32),
                        pltpu.VMEM((hb, 2, LANES, tq), F32)],
        compiler_params=_cparams(("parallel", "parallel", "parallel", "arbitrary")),
        name="diff_attention",
    )(slopes, lam, qk, qk, vt, aq, ak, sub_g.reshape(LANES, 1).astype(F32))


def _shortconv_kernel(z_ref, w_ref, b_ref, o_ref):
    z = z_ref[...]
    n = z.shape[0]
    row = lax.broadcasted_iota(I32, z.shape, 0)
    prev = jnp.where(row == 0, 0.0, pltpu.roll(z, 1, axis=0))
    nxt = jnp.where(row == n - 1, 0.0, pltpu.roll(z, n - 1, axis=0))
    w = w_ref[...]
    o_ref[...] = prev * w[0:1] + z * w[1:2] + nxt * w[2:3] + b_ref[...]


def _shortconv(zh, row_off, conv_w, conv_b, batch, seq, cb=256):
    c3 = zh.shape[1]
    c = c3 // 3
    ncb = c // cb
    assert row_off % seq == 0
    ob = row_off // seq
    out = pl.pallas_call(
        _shortconv_kernel,
        grid=(batch, 3, ncb),
        in_specs=[pl.BlockSpec((seq, cb), lambda b, p, j: (ob + b, p * ncb + j)),
                  pl.BlockSpec((3, cb), lambda b, p, j: (0, p * ncb + j)),
                  pl.BlockSpec((1, cb), lambda b, p, j: (0, p * ncb + j))],
        out_specs=pl.BlockSpec((None, None, seq, cb), lambda b, p, j: (p, b, 0, j)),
        out_shape=jax.ShapeDtypeStruct((3, batch, seq, c), F32),
        compiler_params=_cparams(("parallel", "parallel", "parallel")),
        name="hyena_shortconv",
    )(zh, conv_w.astype(F32), conv_b.reshape(1, c3).astype(F32))
    return out


def _filter_kernel(z_ref, w1_ref, b1_ref, w2_ref, b2_ref, w3_ref, b3_ref, fr_ref, w4_ref,
                   t_ref, d_ref, o_ref, h_ref, *, tl):
    i = pl.program_id(0)
    g = pl.program_id(1)
    hi = lax.Precision.HIGHEST

    @pl.when(g == 0)
    def _():
        fr = fr_ref[...]
        h = jnp.sin(fr * (jnp.dot(z_ref[...], w1_ref[...], precision=hi, preferred_element_type=F32) + b1_ref[...]))
        h = jnp.sin(fr * (jnp.dot(h, w2_ref[...], precision=hi, preferred_element_type=F32) + b2_ref[...]))
        h_ref[...] = jnp.sin(fr * (jnp.dot(h, w3_ref[...], precision=hi, preferred_element_type=F32) + b3_ref[...]))

    f = jnp.dot(h_ref[...].astype(BF16), w4_ref[...], preferred_element_type=F32)
    f = f * jnp.exp(-t_ref[...] * d_ref[...])
    row = lax.broadcasted_iota(I32, f.shape, 0) + i * tl
    drop = jnp.logical_and(row == 0, g % 2 == 1)
    o_ref[...] = jnp.where(drop, 0.0, f)


def _hyena_filter_signals(seq, w1, b1, w2, b2, w3, b3, w4, freq, n_ch, tl=512):
    t = jnp.linspace(0.0, 1.0, seq, dtype=F32)[:, None]
    w = 2.0 * math.pi * jnp.arange(seq, dtype=F32)[:, None] / seq
    f = jnp.linspace(1e-4, FILTER_BANDS - 1, FILTER_BANDS, dtype=F32)[None, :]
    z = jnp.concatenate([t, jnp.cos(f * w), -jnp.sin(f * w)], axis=-1)
    deltas = jnp.abs(jnp.linspace(math.log(DECAY_FAST) / DECAY_TARGET,
                                  math.log(DECAY_SLOW) / DECAY_TARGET, n_ch, dtype=F32))[None, :]
    hid = w1.shape[1]
    emb = LANES
    z = jnp.pad(z, ((0, 0), (0, emb - z.shape[1])))
    w1 = jnp.pad(w1.astype(F32), ((0, emb - w1.shape[0]), (0, 0)))
    tl = min(tl, seq)
    full = lambda shape: pl.BlockSpec(shape, lambda i, g: tuple(0 for _ in shape))
    return pl.pallas_call(
        functools.partial(_filter_kernel, tl=tl),
        grid=(seq // tl, 2 * HYENA_ORDER),
        in_specs=[pl.BlockSpec((tl, emb), lambda i, g: (i, 0)),
                  full((emb, hid)), full((1, hid)), full((hid, hid)), full((1, hid)),
                  full((hid, hid)), full((1, hid)), full((1, hid)),
                  pl.BlockSpec((hid, n_ch), lambda i, g: (0, g)),
                  pl.BlockSpec((tl, 1), lambda i, g: (i, 0)),
                  full((1, n_ch))],
        out_specs=pl.BlockSpec((None, tl, n_ch), lambda i, g: (g, i, 0)),
        out_shape=jax.ShapeDtypeStruct((2 * HYENA_ORDER, seq, n_ch), F32),
        scratch_shapes=[pltpu.VMEM((tl, hid), F32)],
        compiler_params=_cparams(("parallel", "arbitrary")),
        name="hyena_filter_mlp",
    )(z, w1.astype(F32), b1.reshape(1, hid).astype(F32), w2.astype(F32), b2.reshape(1, hid).astype(F32),
      w3.astype(F32), b3.reshape(1, hid).astype(F32), freq.reshape(1, hid).astype(F32), w4.astype(BF16),
      t, deltas)


def _dft_tables(r):
    n = r * r
    k2 = jnp.arange(r, dtype=I32)
    n2 = jnp.arange(r // 2, dtype=I32)
    ang1 = (2.0 * math.pi / r) * ((k2[:, None] * n2[None, :]) % r).astype(F32)
    f1 = jnp.concatenate([jnp.cos(ang1), -jnp.sin(ang1)], axis=0)
    k1 = jnp.arange(r, dtype=I32)
    n1 = jnp.arange(r, dtype=I32)
    ang_a = (2.0 * math.pi / r) * ((k1[:, None] * n1[None, :]) % r).astype(F32)
    ang_b = (2.0 * math.pi / n) * (k2[:, None] * n1[None, :]).astype(F32)
    ca, sa = jnp.cos(ang_a)[None], jnp.sin(ang_a)[None]
    cb, sb = jnp.cos(ang_b)[:, None], jnp.sin(ang_b)[:, None]
    mr, mi_ = ca * cb - sa * sb, -(sa * cb + ca * sb)
    mf = jnp.concatenate([jnp.concatenate([mr, -mi_], axis=2),
                          jnp.concatenate([mi_, mr], axis=2)], axis=1)
    minv = jnp.swapaxes(mf, 1, 2)
    g3 = jnp.concatenate([jnp.cos(ang1.T), -jnp.sin(ang1.T)], axis=1) * (1.0 / n)
    eye = jnp.eye(N1_BLOCK, dtype=F32)
    f1, g3 = jnp.kron(f1, eye), jnp.kron(g3, eye)
    return f1.astype(BF16), mf.astype(BF16), minv.astype(BF16), g3.astype(BF16)


N1_BLOCK = SUBLANES


def _pack_complex(re, im):
    rb = lax.bitcast_convert_type(re.astype(BF16).astype(F32), U32)
    ib = lax.bitcast_convert_type(im.astype(BF16).astype(F32), U32)
    return (rb >> 16) | ib


def _unpack_complex_rows(w):
    re, im = _unpack(w)
    return jnp.concatenate([re, im], axis=0).astype(BF16)


def _fft1_kernel(f_ref, x_ref, o_ref, *, r):
    n_ch = x_ref.shape[-1]
    x = x_ref[...].reshape((r // 2) * N1_BLOCK, n_ch).astype(BF16)
    res = jnp.dot(f_ref[...], x, preferred_element_type=F32)
    half = r * N1_BLOCK
    o_ref[...] = _pack_complex(res[:half], res[half:]).reshape(r, N1_BLOCK, n_ch)


def _fft_stage1(x4, part, f1, r, n_ch):
    nb = x4.shape[1]
    xv = x4.reshape(x4.shape[0], nb, r // 2, r, n_ch)
    return pl.pallas_call(
        functools.partial(_fft1_kernel, r=r),
        grid=(nb, r // N1_BLOCK),
        in_specs=[pl.BlockSpec(f1.shape, lambda b, j: (0, 0)),
                  pl.BlockSpec((None, None, r // 2, N1_BLOCK, n_ch), lambda b, j: (part, b, 0, j, 0))],
        out_specs=pl.BlockSpec((None, r, N1_BLOCK, n_ch), lambda b, j: (b, 0, j, 0)),
        out_shape=jax.ShapeDtypeStruct((nb, r, r, n_ch), U32),
        compiler_params=_cparams(("parallel", "parallel")),
        name="hyena_dft_stage1",
    )(f1, xv)


def _k2_block(r):
    return max(1, (2 * LANES) // r)


def _filter_spec_kernel(mf_ref, bf_ref, bb_ref, o_ref, *, r):
    for kk in range(mf_ref.shape[0]):
        m = mf_ref[kk]
        xf = jnp.dot(m, _unpack_complex_rows(bf_ref[kk]), preferred_element_type=F32)
        xb = jnp.dot(m, _unpack_complex_rows(bb_ref[kk]), preferred_element_type=F32)
        o_ref[kk] = _pack_complex(xf[:r] + xb[:r], xf[r:] - xb[r:])


def _filter_spectrum(b1, mf, r, n_ch):
    kb = _k2_block(r)
    blk = lambda sel: pl.BlockSpec((None, kb, r, n_ch), lambda o, k: (2 * o + sel, k, 0, 0))
    return pl.pallas_call(
        functools.partial(_filter_spec_kernel, r=r),
        grid=(HYENA_ORDER, r // kb),
        in_specs=[pl.BlockSpec((kb, 2 * r, 2 * r), lambda o, k: (k, 0, 0)), blk(0), blk(1)],
        out_specs=pl.BlockSpec((None, kb, r, n_ch), lambda o, k: (o, k, 0, 0)),
        out_shape=jax.ShapeDtypeStruct((HYENA_ORDER, r, r, n_ch), U32),
        compiler_params=_cparams(("parallel", "parallel")),
        name="hyena_filter_spectrum",
    )(mf, b1, b1)


def _fft2_kernel(mf_ref, mi_ref, b_ref, h_ref, o_ref, *, r):
    for kk in range(mf_ref.shape[0]):
        x = jnp.dot(mf_ref[kk], _unpack_complex_rows(b_ref[kk]), preferred_element_type=F32)
        xr, xi = x[:r], x[r:]
        hr, hi = _unpack(h_ref[kk])
        y = jnp.concatenate([xr * hr - xi * hi, xr * hi + xi * hr], axis=0).astype(BF16)
        c = jnp.dot(mi_ref[kk], y, preferred_element_type=F32)
        o_ref[kk] = _pack_complex(c[:r], c[r:])


def _fft_stage2(b1, h, order, mf, minv, r, n_ch):
    nb = b1.shape[0]
    kb = _k2_block(r)
    blk = pl.BlockSpec((None, kb, r, n_ch), lambda k, b: (b, k, 0, 0))
    return pl.pallas_call(
        functools.partial(_fft2_kernel, r=r),
        grid=(r // kb, nb),
        in_specs=[pl.BlockSpec((kb, 2 * r, 2 * r), lambda k, b: (k, 0, 0)),
                  pl.BlockSpec((kb, 2 * r, 2 * r), lambda k, b: (k, 0, 0)),
                  blk,
                  pl.BlockSpec((None, kb, r, n_ch), lambda k, b: (order, k, 0, 0))],
        out_specs=blk,
        out_shape=jax.ShapeDtypeStruct((nb, r, r, n_ch), U32),
        compiler_params=_cparams(("parallel", "parallel")),
        name="hyena_dft_stage2",
    )(mf, minv, b1, h)


def _fft3_kernel(g_ref, c_ref, gate_ref, s_ref, bias_ref, ng_ref, *rest, final):
    if final:
        (o_ref,) = rest
    else:
        f_ref, o_ref, b_ref = rest
    r, nb, n_ch = c_ref.shape
    cc = _unpack_complex_rows(c_ref[...].reshape(r * nb, n_ch))
    y = jnp.dot(g_ref[...], cc, preferred_element_type=F32)
    rows = (r // 2) * nb
    s_new = gate_ref[...].reshape(rows, n_ch) * (y + s_ref[...].reshape(rows, n_ch) * bias_ref[...])
    if final:
        ms = jnp.mean(s_new * s_new, axis=-1, keepdims=True)
        s_new = s_new * lax.rsqrt(ms + RMS_EPS) * ng_ref[...]
    o_ref[...] = s_new.reshape(r // 2, nb, n_ch)
    if not final:
        res = jnp.dot(f_ref[...], s_new.astype(BF16), preferred_element_type=F32)
        half = r * nb
        b_ref[...] = _pack_complex(res[:half], res[half:]).reshape(r, nb, n_ch)


def _fft_stage3(c2, g3, z4, gate_part, s4, s_part, bias, norm_g, f1, r, n_ch):
    final = f1 is None
    nb = c2.shape[0]
    zv = z4.reshape(z4.shape[0], nb, r // 2, r, n_ch)
    sv = s4.reshape(s4.shape[0], nb, r // 2, r, n_ch)
    dspec = lambda part: pl.BlockSpec((None, None, r // 2, N1_BLOCK, n_ch), lambda b, j: (part, b, 0, j, 0))
    packed = pl.BlockSpec((None, r, N1_BLOCK, n_ch), lambda b, j: (b, 0, j, 0))
    s_spec = pl.BlockSpec((None, r // 2, N1_BLOCK, n_ch), lambda b, j: (b, 0, j, 0))
    s_shape = jax.ShapeDtypeStruct((nb, r // 2, r, n_ch), F32)
    in_specs = [pl.BlockSpec(g3.shape, lambda b, j: (0, 0)), packed, dspec(gate_part), dspec(s_part),
                pl.BlockSpec((1, n_ch), lambda b, j: (0, 0)), pl.BlockSpec((1, n_ch), lambda b, j: (0, 0))]
    args = [g3, c2, zv, sv, bias.reshape(1, n_ch).astype(F32), norm_g.reshape(1, n_ch).astype(F32)]
    if final:
        out_specs, out_shape = s_spec, s_shape
    else:
        in_specs.append(pl.BlockSpec(f1.shape, lambda b, j: (0, 0)))
        args.append(f1)
        out_specs = [s_spec, packed]
        out_shape = [s_shape, jax.ShapeDtypeStruct((nb, r, r, n_ch), U32)]
    return pl.pallas_call(
        functools.partial(_fft3_kernel, final=final),
        grid=(nb, r // N1_BLOCK),
        in_specs=in_specs,
        out_specs=out_specs,
        out_shape=out_shape,
        compiler_params=_cparams(("parallel", "parallel")),
        name="hyena_dft_stage3",
    )(*args)


def _hyena(zh, row_off, batch, seq, conv_w, conv_b, filt, fft_bias, out_g):
    n_ch = zh.shape[1] // 3
    r = int(round(math.sqrt(2 * seq)))
    assert r * r == 2 * seq and r % 16 == 0
    f1, mf, minv, g3 = _dft_tables(r)
    sig = _hyena_filter_signals(seq, *filt, n_ch=n_ch)
    hb1 = _fft_stage1(sig[None], 0, f1, r, n_ch)
    h = _filter_spectrum(hb1, mf, r, n_ch)
    z4 = _shortconv(zh, row_off, conv_w, conv_b, batch, seq)
    s4, s_part = z4, 2
    b1 = _fft_stage1(s4, s_part, f1, r, n_ch)
    for o in range(HYENA_ORDER):
        c2 = _fft_stage2(b1, h, o, mf, minv, r, n_ch)
        if o < HYENA_ORDER - 1:
            s, b1 = _fft_stage3(c2, g3, z4, o, s4, s_part, fft_bias[o], out_g, f1, r, n_ch)
        else:
            s = _fft_stage3(c2, g3, z4, o, s4, s_part, fft_bias[o], out_g, None, r, n_ch)
        s4, s_part = s.reshape(1, batch, seq, n_ch), 0
    return s4.reshape(batch * seq, n_ch)


def _extract_top(s, key, count):
    vals, keys = [], []
    for _ in range(count):
        m = jnp.max(s, axis=0, keepdims=True)
        kmin = jnp.min(jnp.where(s == m, key, jnp.inf), axis=0, keepdims=True)
        s = jnp.where(key == kmin, -jnp.inf, s)
        vals.append(m)
        keys.append(kmin)
    return vals, keys


def _peer_topk_kernel(q_ref, keys_ref, eid_ref, gate_ref):
    t = q_ref.shape[0]
    nk = PEER_NKEYS
    q = q_ref[...]
    row_key = lax.broadcasted_iota(I32, (nk, t), 0).astype(F32)
    tops = []
    for c in range(2):
        s = lax.dot_general(keys_ref[c], q[:, c * nk:(c + 1) * nk], (((1,), (1,)), ((), ())),
                            preferred_element_type=F32)
        tops.append(_extract_top(s, row_key, PEER_TOPK))
    (v1, i1), (v2, i2) = tops
    rows16 = lax.broadcasted_iota(I32, (PEER_TOPK, t), 0)
    v2a = jnp.zeros((PEER_TOPK, t), F32)
    i2a = jnp.zeros((PEER_TOPK, t), F32)
    for j in range(PEER_TOPK):
        v2a = jnp.where(rows16 == j, v2[j], v2a)
        i2a = jnp.where(rows16 == j, i2[j], i2a)
    n_exp = float(nk * nk)
    half = PEER_TOPK // 2
    rows8 = lax.broadcasted_iota(I32, (half, t), 0)
    pos8 = rows8.astype(F32)
    v2h, i2h = v2a[:half], i2a[:half]
    cand = [v1[0] + v2a]
    ckey = [rows16.astype(F32) * n_exp + (i1[0] * float(nk) + i2a)]
    for i in range(1, half):
        cand.append(jnp.where(rows8 < PEER_TOPK // (i + 1), v1[i] + v2h, -jnp.inf))
        ckey.append((pos8 + float(i * PEER_TOPK)) * n_exp + (i1[i] * float(nk) + i2h))
    v1t = jnp.zeros((half, t), F32)
    i1t = jnp.zeros((half, t), F32)
    for r in range(half):
        v1t = jnp.where(rows8 == r, v1[half + r], v1t)
        i1t = jnp.where(rows8 == r, i1[half + r], i1t)
    cand.append(v1t + v2[0])
    ckey.append((pos8 + float(half)) * (PEER_TOPK * n_exp) + (i1t * float(nk) + i2[0]))
    tv, tk_ = _extract_top(jnp.concatenate(cand, axis=0), jnp.concatenate(ckey, axis=0), PEER_TOPK)
    denom = jnp.zeros((1, t), F32)
    es = []
    for k in range(PEER_TOPK):
        e = jnp.exp(tv[k] - tv[0])
        es.append(e)
        denom = denom + e
    eid = jnp.zeros((PEER_TOPK, t), F32)
    gate = jnp.zeros((PEER_TOPK, t), F32)
    for k in range(PEER_TOPK):
        pos = jnp.floor(tk_[k] * (1.0 / n_exp))
        eid = jnp.where(rows16 == k, tk_[k] - pos * n_exp, eid)
        gate = jnp.where(rows16 == k, es[k] / denom, gate)
    eid_ref[...] = eid.astype(I32)
    gate_ref[...] = gate


def _peer_topk(q, keys, tm=1024):
    t = q.shape[0]
    tm = min(tm, t)
    out_spec = pl.BlockSpec((PEER_TOPK, tm), lambda i, h: (h, i))
    return pl.pallas_call(
        _peer_topk_kernel,
        grid=(t // tm, PEER_HEADS),
        in_specs=[pl.BlockSpec((tm, 2 * PEER_NKEYS), lambda i, h: (i, h)),
                  pl.BlockSpec((None, 2, PEER_NKEYS, PEER_NKEYS), lambda i, h: (h, 0, 0, 0))],
        out_specs=[out_spec, out_spec],
        out_shape=[jax.ShapeDtypeStruct((PEER_HEADS * PEER_TOPK, t), I32),
                   jax.ShapeDtypeStruct((PEER_HEADS * PEER_TOPK, t), F32)],
        compiler_params=_cparams(("parallel", "parallel")),
        name="peer_topk",
    )(q, keys)


def _pack_table(tab):
    e, d = tab.shape
    assert d == 2 * SUBLANES * LANES
    bits = lax.bitcast_convert_type(tab.astype(BF16), jnp.uint16).astype(U32)
    packed = bits[:, :d // 2] | (bits[:, d // 2:] << 16)
    return packed.reshape(e * SUBLANES, LANES)


def _unpack(w):
    lo = lax.bitcast_convert_type(w << 16, F32)
    hi = lax.bitcast_convert_type(w & jnp.uint32(0xFFFF0000), F32)
    return lo, hi


_BITREV8 = (0, 4, 2, 6, 1, 5, 3, 7)


def _sublane_fold8(parts):
    sub = lax.broadcasted_iota(I32, (2 * SUBLANES, LANES), 0) // 2

    def rolled(a, shift):
        return pltpu.bitcast(pltpu.roll(pltpu.bitcast(a, U32), shift, axis=0), BF16)

    lvl = [parts[_BITREV8[r]] for r in range(8)]
    for shift, mask in ((4, sub < 4), (2, (sub % 4) < 2), (1, (sub % 2) < 1)):
        nxt = []
        for a, b in zip(lvl[0::2], lvl[1::2]):
            nxt.append(jnp.where(mask, a + rolled(a, SUBLANES - shift), b + rolled(b, shift)))
        lvl = nxt
    return lvl[0]


def _pair_ranges(half):
    if half == 0:
        return (0, PEER_WINDOW), (PEER_WINDOW, PEER_PAIRS)
    return (PEER_PAIRS - PEER_WINDOW, PEER_PAIRS), (0, PEER_PAIRS - PEER_WINDOW)


def _overflow(n_low, half):
    return n_low > PEER_WINDOW if half == 0 else n_low < PEER_PAIRS - PEER_WINDOW


def _table_rows(tab_ref, off):
    return _unpack(tab_ref[pl.ds(pl.multiple_of(off, SUBLANES), SUBLANES), :])


def _peer_u_kernel(off_ref, nlow_ref, tab_ref, x_ref, o_ref, extra_ref, *, tm, half):
    lane = lax.broadcasted_iota(I32, (SUBLANES, LANES), 1)
    sub = lax.broadcasted_iota(I32, (SUBLANES, LANES), 0)
    lane_grp = lax.shift_right_logical(lane, 3)
    diag = sub == (lane & (SUBLANES - 1))
    main, rest = _pair_ranges(half)

    def folded(t, p0, p1):
        xb = pltpu.bitcast(x_ref[t], BF16)
        out = []
        for g in range(p0 // SUBLANES, p1 // SUBLANES):
            parts = []
            for r in range(SUBLANES):
                off = pl.multiple_of(off_ref[t, g * SUBLANES + r], SUBLANES)
                parts.append(pltpu.bitcast(tab_ref[pl.ds(off, SUBLANES), :], BF16) * xb)
            out.append(pltpu.bitcast(_sublane_fold8(parts), U32))
        return out

    def lane_sums(folds, p0):
        mat = jnp.zeros((SUBLANES, LANES), F32)
        for i, f in enumerate(folds):
            lo, hi = _unpack(f)
            mat = jnp.where(lane_grp == p0 // SUBLANES + i, jnp.sum(lo + hi, axis=-1, keepdims=True), mat)
        return jnp.sum(jnp.where(diag, mat, 0.0), axis=0, keepdims=True)

    def finish(t, folds):
        o_ref[pl.ds(t, 1), :] = lane_sums(folds, main[0]) + extra_ref[pl.ds(t, 1), :]

    def token(t, prev):
        cur = folded(t, *main)
        finish(jnp.maximum(t - 1, 0), prev)
        extra_ref[pl.ds(t, 1), :] = jnp.zeros((1, LANES), F32)

        @pl.when(_overflow(nlow_ref[0, t], half))
        def _():
            extra_ref[pl.ds(t, 1), :] = lane_sums(folded(t, *rest), rest[0])

        return tuple(cur)

    extra_ref[pl.ds(0, 1), :] = jnp.zeros((1, LANES), F32)
    zeros = tuple(jnp.zeros((SUBLANES, LANES), U32) for _ in range((main[1] - main[0]) // SUBLANES))
    last = lax.fori_loop(0, tm, token, zeros)
    finish(tm - 1, last)


def _peer_u(off, nlow3, tab, x4, half, tm=128):
    t = off.shape[0]
    rows = PEER_HALF * SUBLANES
    return pl.pallas_call(
        functools.partial(_peer_u_kernel, tm=tm, half=half),
        grid=(t // tm,),
        in_specs=[pl.BlockSpec((tm, LANES), lambda i: (i, 0), memory_space=pltpu.SMEM),
                  pl.BlockSpec((None, 1, tm), lambda i: (i, 0, 0), memory_space=pltpu.SMEM),
                  pl.BlockSpec((rows, LANES), lambda i: (half, 0), pipeline_mode=pl.Buffered(1)),
                  pl.BlockSpec((tm, SUBLANES, LANES), lambda i: (i, 0, 0))],
        out_specs=pl.BlockSpec((tm, LANES), lambda i: (i, 0)),
        out_shape=jax.ShapeDtypeStruct((t, LANES), F32),
        scratch_shapes=[pltpu.VMEM((tm, LANES), F32)],
        compiler_params=_cparams(("arbitrary",)),
        name="peer_expert_scores",
    )(off, nlow3, tab, x4)


def _peer_coef_kernel(a0_ref, a1_ref, eid_ref, slot_ref, gate_ref, c0_ref, c1_ref):
    low = eid_ref[...] < PEER_HALF
    a = jnp.where(low, a0_ref[...], a1_ref[...])
    gate = jnp.take_along_axis(gate_ref[...], slot_ref[...], axis=1)
    coef = gate * (0.5 * a * (1.0 + lax.erf(a * (1.0 / math.sqrt(2.0)))))
    c0_ref[...] = jnp.where(low, coef, 0.0)
    c1_ref[...] = jnp.where(low, 0.0, coef)


def _peer_coef(a0, a1, eid, slot, gate, tm=1024):
    t = eid.shape[0]
    tm = min(tm, t)
    spec = pl.BlockSpec((tm, LANES), lambda i: (i, 0))
    return pl.pallas_call(
        _peer_coef_kernel,
        grid=(t // tm,),
        in_specs=[spec, spec, spec, spec, spec],
        out_specs=[spec, spec],
        out_shape=[jax.ShapeDtypeStruct((t, LANES), F32)] * 2,
        compiler_params=_cparams(("parallel",)),
        name="peer_coef",
    )(a0, a1, eid, slot, gate)


def _peer_v_kernel(off_ref, nlow_ref, coef_ref, tab_ref, base_ref, *o_refs, tm, half, n_first):
    n_acc = 4
    main, rest = _pair_ranges(half)

    def weighted(t, p0, p1):
        acc_lo = [jnp.zeros((SUBLANES, LANES), F32) for _ in range(n_acc)]
        acc_hi = [jnp.zeros((SUBLANES, LANES), F32) for _ in range(n_acc)]
        for p in range(p0, p1):
            c = coef_ref[t, p]
            lo, hi = _table_rows(tab_ref, off_ref[t, p])
            acc_lo[p % n_acc] = acc_lo[p % n_acc] + c * lo
            acc_hi[p % n_acc] = acc_hi[p % n_acc] + c * hi
        return ((acc_lo[0] + acc_lo[1]) + (acc_lo[2] + acc_lo[3]),
                (acc_hi[0] + acc_hi[1]) + (acc_hi[2] + acc_hi[3]))

    def run(o_ref):
        def store_row(t, lo, hi):
            o_ref[t, 0] = base_ref[t, 0] + lo
            o_ref[t, 1] = base_ref[t, 1] + hi

        def token(t, carry):
            lo, hi = weighted(t, *main)
            store_row(t, lo, hi)

            @pl.when(_overflow(nlow_ref[0, t], half))
            def _():
                lo2, hi2 = weighted(t, *rest)
                store_row(t, lo + lo2, hi + hi2)

            return carry

        lax.fori_loop(0, tm, token, 0)

    if len(o_refs) == 1:
        run(o_refs[0])
    else:
        first = pl.program_id(0) < n_first
        pl.when(first)(lambda: run(o_refs[0]))
        pl.when(jnp.logical_not(first))(lambda: run(o_refs[1]))


def _peer_v(off, nlow3, coef, tab, base, half, tm=128, split=None):
    t = base.shape[0]
    rows = PEER_HALF * SUBLANES
    smem = pl.BlockSpec((tm, LANES), lambda i: (i, 0), memory_space=pltpu.SMEM)
    tile = pl.BlockSpec((tm, 2, SUBLANES, LANES), lambda i: (i, 0, 0, 0))
    if split is None:
        n_first, out_specs = 0, tile
        out_shape = jax.ShapeDtypeStruct((t, 2, SUBLANES, LANES), F32)
    else:
        assert split % tm == 0
        n_first, n_rest = split // tm, (t - split) // tm
        blk = (tm, 2, SUBLANES, LANES)
        out_specs = [pl.BlockSpec(blk, lambda i: (jnp.clip(i, 0, n_first - 1), 0, 0, 0)),
                     pl.BlockSpec(blk, lambda i: (jnp.clip(i - n_first, 0, n_rest - 1), 0, 0, 0))]
        out_shape = [jax.ShapeDtypeStruct((split, 2, SUBLANES, LANES), F32),
                     jax.ShapeDtypeStruct((t - split, 2, SUBLANES, LANES), F32)]
    return pl.pallas_call(
        functools.partial(_peer_v_kernel, tm=tm, half=half, n_first=n_first),
        grid=(t // tm,),
        in_specs=[smem,
                  pl.BlockSpec((None, 1, tm), lambda i: (i, 0, 0), memory_space=pltpu.SMEM),
                  smem,
                  pl.BlockSpec((rows, LANES), lambda i: (half, 0), pipeline_mode=pl.Buffered(1)),
                  tile],
        out_specs=out_specs,
        out_shape=out_shape,
        compiler_params=_cparams(("arbitrary",)),
        name="peer_expert_sum",
    )(off, nlow3, coef, tab, base)


def _peer(x1, split, norm2_g, wq_bf, keys_bf, u_packed, v_packed, tm=128):
    t, d = x1.shape
    tm = min(tm, t)
    (xn_bf,) = _rmsnorm(x1, norm2_g, (BF16,))
    q = _matmul(xn_bf, wq_bf, wq_bf.shape[1], 0, BF16)
    eid_t, gate_t = _peer_topk(q, keys_bf)
    eid, gate = eid_t.T, gate_t.T
    n_exp = PEER_NKEYS * PEER_NKEYS
    high = (eid >= PEER_HALF).astype(I32)
    nlow3 = (PEER_PAIRS - jnp.sum(high, axis=-1, dtype=I32)).reshape(t // tm, 1, tm)
    slot = lax.broadcasted_iota(I32, eid.shape, 1)
    order = jnp.sort((high * PEER_PAIRS + slot) * n_exp + eid, axis=-1)
    eid = order % n_exp
    slot = (order // n_exp) % PEER_PAIRS
    off = (eid & (PEER_HALF - 1)) * SUBLANES
    xw = _pack_table(xn_bf).reshape(t, SUBLANES, LANES)
    a0 = _peer_u(off, nlow3, u_packed, xw, 0, tm)
    a1 = _peer_u(off, nlow3, u_packed, xw, 1, tm)
    c0, c1 = _peer_coef(a0, a1, eid, slot, gate)
    y = _peer_v(off, nlow3, c0, v_packed, x1.reshape(t, 2, SUBLANES, LANES), 0, tm)
    ya, yb = _peer_v(off, nlow3, c1, v_packed, y, 1, tm, split=split)
    return ya.reshape(split, d), yb.reshape(t - split, d)


def kernel(x_prompt, x_sample, norm1_g, w_in, q_norm_g, k_norm_g, lambda_q1, lambda_k1, lambda_q2, lambda_k2, attn_sub_g, conv_w, conv_b, filt_w1, filt_b1, filt_w2, filt_b2, filt_w3, filt_b3, filt_w4, filt_freq, fft_bias, hyena_out_g, w_out, norm2_g, peer_wq, peer_keys, peer_u, peer_v):
    depth = w_in.shape[0]
    d_model = x_prompt.shape[-1]
    att_w = ATT_HEADS * ATT_VDIM
    shapes = [x_prompt.shape[:2], x_sample.shape[:2]]
    xs = [x_prompt.reshape(-1, d_model), x_sample.reshape(-1, d_model)]
    n0 = xs[0].shape[0]
    slopes = 2.0 ** (-8.0 * jnp.arange(1, ATT_HEADS + 1, dtype=F32) / ATT_HEADS)

    for l in range(depth):
        lambda_init = 0.8 - 0.6 * math.exp(-0.3 * l)
        lam = (jnp.exp(jnp.sum(lambda_q1[l].astype(F32) * lambda_k1[l].astype(F32)))
               - jnp.exp(jnp.sum(lambda_q2[l].astype(F32) * lambda_k2[l].astype(F32)))
               + lambda_init).reshape(1)
        w_in_bf = w_in[l].astype(BF16)
        q_gain = jnp.tile(q_norm_g[l].astype(F32), 2 * ATT_HEADS) * (ATT_QKDIM ** -0.5 * LOG2E)
        k_gain = jnp.tile(k_norm_g[l].astype(F32), 2 * ATT_HEADS)
        qk_gain = jnp.concatenate([q_gain, k_gain]).reshape(1, 2 * att_w)

        h_bf = _rmsnorm_stacked(xs[0], xs[1], norm1_g[l], BF16)
        qk = _matmul(h_bf, w_in_bf, 2 * att_w, 0, BF16, mode="qknorm", extra=qk_gain)
        vt = _matmul_nt(w_in_bf[:, 2 * att_w:3 * att_w].T, h_bf, BF16)
        zh = _matmul(h_bf, w_in_bf, w_in.shape[2] - 3 * att_w, 3 * att_w, F32)

        filt = (filt_w1[l], filt_b1[l], filt_w2[l], filt_b2[l], filt_w3[l], filt_b3[l], filt_w4[l], filt_freq[l])
        segs, row = [], 0
        for (b, s), x_seg in zip(shapes, xs):
            att = _attention(qk, vt, row, slopes, lam, attn_sub_g[l], b, s, 1.0 - lambda_init)
            hy = _hyena(zh, row, b, s, conv_w[l], conv_b[l], filt, fft_bias[l], hyena_out_g[l])
            segs.append((att, hy, x_seg))
            row += b * s
        x1 = _outproj_stacked(segs[0], segs[1], w_out[l].astype(BF16))

        xs = _peer(x1, n0, norm2_g[l], peer_wq[l].astype(BF16), peer_keys[l].astype(BF16),
                   _pack_table(peer_u[l]), _pack_table(peer_v[l]))

    return (xs[0].reshape(x_prompt.shape), xs[1].reshape(x_sample.shape))
```

```python
import functools
import math

import jax
import jax.numpy as jnp
from jax import lax
from jax.experimental import pallas as pl
from jax.experimental.pallas import tpu as pltpu

F32 = jnp.float32
BF16 = jnp.bfloat16
I32 = jnp.int32
U32 = jnp.uint32

RMS_EPS = 1e-6
LOG2E = 1.4426950408889634
LANES = 128
SUBLANES = 8
VMEM_LIMIT_BYTES = 56 * 1024 * 1024

ATT_HEADS = 8
ATT_VDIM = 128
ATT_QKDIM = 64
HYENA_ORDER = 2
FILTER_BANDS = 16
DECAY_FAST = 0.3
DECAY_SLOW = 1.5
DECAY_TARGET = 1e-2
PEER_HEADS = 8
PEER_NKEYS = 128
PEER_TOPK = 16
PEER_HALF = PEER_NKEYS * PEER_NKEYS // 2
PEER_PAIRS = PEER_HEADS * PEER_TOPK
PEER_WINDOW = 80


def _cparams(sem, vmem=VMEM_LIMIT_BYTES):
    return pltpu.CompilerParams(dimension_semantics=sem, vmem_limit_bytes=vmem)


def _segment_spec(block, start, count, minor=0):
    return pl.BlockSpec(block, lambda i: (jnp.clip(i - start, 0, count - 1), minor))


def _rmsnorm_kernel(x_ref, g_ref, *o_refs):
    x = x_ref[...]
    ms = jnp.mean(x * x, axis=-1, keepdims=True)
    y = x * lax.rsqrt(ms + RMS_EPS) * g_ref[...]
    for o_ref in o_refs:
        o_ref[...] = y.astype(o_ref.dtype)


def _rmsnorm2_kernel(xa_ref, xb_ref, g_ref, o_ref, *, na):
    first = pl.program_id(0) < na
    x = jnp.where(first, xa_ref[...], xb_ref[...])
    ms = jnp.mean(x * x, axis=-1, keepdims=True)
    o_ref[...] = (x * lax.rsqrt(ms + RMS_EPS) * g_ref[...]).astype(o_ref.dtype)


def _rmsnorm_stacked(xa, xb, g, out_dtype, tm=512):
    d = xa.shape[1]
    na, nb = xa.shape[0] // tm, xb.shape[0] // tm
    assert xa.shape[0] % tm == 0 and xb.shape[0] % tm == 0
    return pl.pallas_call(
        functools.partial(_rmsnorm2_kernel, na=na),
        grid=(na + nb,),
        in_specs=[_segment_spec((tm, d), 0, na), _segment_spec((tm, d), na, nb),
                  pl.BlockSpec((1, d), lambda i: (0, 0))],
        out_specs=pl.BlockSpec((tm, d), lambda i: (i, 0)),
        out_shape=jax.ShapeDtypeStruct((xa.shape[0] + xb.shape[0], d), out_dtype),
        compiler_params=_cparams(("parallel",)),
        name="rmsnorm_stacked",
    )(xa, xb, g.reshape(1, d).astype(F32))


def _rmsnorm(x, g, out_dtypes, tm=512):
    t, d = x.shape
    tm = min(tm, t)
    spec = pl.BlockSpec((tm, d), lambda i: (i, 0))
    return pl.pallas_call(
        _rmsnorm_kernel,
        grid=(t // tm,),
        in_specs=[spec, pl.BlockSpec((1, d), lambda i: (0, 0))],
        out_specs=[spec for _ in out_dtypes],
        out_shape=[jax.ShapeDtypeStruct((t, d), dt) for dt in out_dtypes],
        compiler_params=_cparams(("parallel",)),
        name="rmsnorm",
    )(x, g.reshape(1, d).astype(F32))


def _group_rms_scale(x, gain):
    lane = lax.broadcasted_iota(I32, x.shape, 1)
    lo = lane < ATT_QKDIM
    x2 = x * x
    s_lo = jnp.sum(jnp.where(lo, x2, 0.0), axis=-1, keepdims=True)
    s_hi = jnp.sum(jnp.where(lo, 0.0, x2), axis=-1, keepdims=True)
    ms = jnp.where(lo, s_lo, s_hi) * (1.0 / ATT_QKDIM)
    return x * lax.rsqrt(ms + RMS_EPS) * gain


def _mm_kernel(a_ref, b_ref, *rest, mode):
    acc = jnp.dot(a_ref[...], b_ref[...], preferred_element_type=F32)
    if mode == "plain":
        (o_ref,) = rest
        o_ref[...] = acc.astype(o_ref.dtype)
    elif mode == "qknorm":
        g_ref, o_ref = rest
        for c in range(acc.shape[1] // LANES):
            sl = slice(c * LANES, (c + 1) * LANES)
            o_ref[:, sl] = _group_rms_scale(acc[:, sl], g_ref[:, sl]).astype(o_ref.dtype)
    else:
        raise ValueError(mode)


def _matmul(a, b, n_cols, col_off, out_dtype, mode="plain", extra=None, tm=1024, tn=1024):
    m, k = a.shape
    tm = min(tm, m)
    tn = min(tn, n_cols)
    assert col_off % tn == 0 and n_cols % tn == 0 and m % tm == 0
    off = col_off // tn
    in_specs = [pl.BlockSpec((tm, k), lambda i, j: (i, 0)),
                pl.BlockSpec((k, tn), lambda i, j: (0, j + off))]
    args = [a, b]
    if mode == "qknorm":
        in_specs.append(pl.BlockSpec((1, tn), lambda i, j: (0, j)))
        args.append(extra)
    return pl.pallas_call(
        functools.partial(_mm_kernel, mode=mode),
        grid=(m // tm, n_cols // tn),
        in_specs=in_specs,
        out_specs=pl.BlockSpec((tm, tn), lambda i, j: (i, j)),
        out_shape=jax.ShapeDtypeStruct((m, n_cols), out_dtype),
        compiler_params=_cparams(("parallel", "arbitrary")),
        name="matmul_" + mode,
    )(*args)


def _outproj_kernel(att_a, hy_a, x_a, att_b, hy_b, x_b, w_ref, o_ref, *, na):
    kw = att_a.shape[1]

    def run(att_ref, hy_ref, x_ref):
        acc = jnp.dot(att_ref[...], w_ref[:kw, :], preferred_element_type=F32)
        acc = acc + jnp.dot(hy_ref[...].astype(BF16), w_ref[kw:, :], preferred_element_type=F32)
        o_ref[...] = acc + x_ref[...]

    first = pl.program_id(0) < na
    pl.when(first)(lambda: run(att_a, hy_a, x_a))
    pl.when(jnp.logical_not(first))(lambda: run(att_b, hy_b, x_b))


def _outproj_stacked(seg_a, seg_b, w, tm=512, tn=1024):
    n_out = w.shape[1]
    na, nb = seg_a[0].shape[0] // tm, seg_b[0].shape[0] // tm
    assert seg_a[0].shape[0] % tm == 0 and seg_b[0].shape[0] % tm == 0 and n_out % tn == 0

    def specs(seg, start, count):
        att, hy, x = seg
        row = lambda i, j: (jnp.clip(i - start, 0, count - 1), 0)
        return [pl.BlockSpec((tm, att.shape[1]), row), pl.BlockSpec((tm, hy.shape[1]), row),
                pl.BlockSpec((tm, tn), lambda i, j: (jnp.clip(i - start, 0, count - 1), j))]

    return pl.pallas_call(
        functools.partial(_outproj_kernel, na=na),
        grid=(na + nb, n_out // tn),
        in_specs=specs(seg_a, 0, na) + specs(seg_b, na, nb) + [pl.BlockSpec((w.shape[0], tn), lambda i, j: (0, j))],
        out_specs=pl.BlockSpec((tm, tn), lambda i, j: (i, j)),
        out_shape=jax.ShapeDtypeStruct(((na + nb) * tm, n_out), F32),
        compiler_params=_cparams(("parallel", "arbitrary")),
        name="out_projection",
    )(*seg_a, *seg_b, w)


def _mm_nt_kernel(w_ref, a_ref, o_ref):
    o_ref[...] = lax.dot_general(w_ref[...], a_ref[...], (((1,), (1,)), ((), ())),
                                 preferred_element_type=F32).astype(o_ref.dtype)


def _matmul_nt(wt, a, out_dtype, tm=1024):
    n, k = wt.shape
    m = a.shape[0]
    tm = min(tm, m)
    return pl.pallas_call(
        _mm_nt_kernel,
        grid=(m // tm,),
        in_specs=[pl.BlockSpec((n, k), lambda i: (0, 0)), pl.BlockSpec((tm, k), lambda i: (i, 0))],
        out_specs=pl.BlockSpec((n, tm), lambda i: (0, i)),
        out_shape=jax.ShapeDtypeStruct((n, m), out_dtype),
        compiler_params=_cparams(("parallel",)),
        name="matmul_nt",
    )(wt, a)


def _alibi_columns(slopes, tq, tk):
    return [_alibi_side(slopes, tq, True), _alibi_side(slopes, tk, False)]


def _alibi_side(slopes, n, query_side):
    pos = jnp.arange(n, dtype=F32)
    val = (slopes.astype(F32) * LOG2E)[:, None] * pos[None, :]

    def pieces(x):
        p1 = x.astype(BF16)
        r1 = x - p1.astype(F32)
        p2 = r1.astype(BF16)
        p3 = (r1 - p2.astype(F32)).astype(BF16)
        return [p1, p2, p3]

    ones = [jnp.ones_like(val, BF16)] * 3
    six = jnp.stack(pieces(-val) + ones if query_side else ones + pieces(val), axis=-1)
    pad = jnp.zeros(val.shape + (ATT_QKDIM - 6,), BF16)
    return jnp.concatenate([six, pad, six, pad], axis=-1)


def _attn_kernel(slope_ref, lam_ref, q_ref, k_ref, vt_ref, aq_ref, ak_ref, g_ref, o_ref,
                 m_ref, l_ref, acc_ref, *, tq, tk, nk, hb, out_scale):
    hg = pl.program_id(1)
    i = pl.program_id(2)
    j = pl.program_id(3)

    @pl.when(j == 0)
    def _():
        m_ref[...] = jnp.full(m_ref.shape, -jnp.inf, F32)
        l_ref[...] = jnp.zeros(l_ref.shape, F32)
        acc_ref[...] = jnp.zeros(acc_ref.shape, F32)

    q_first = lax.broadcasted_iota(I32, (tq, LANES), 1) < ATT_QKDIM
    k_first = lax.broadcasted_iota(I32, (tk, LANES), 1) < ATT_QKDIM
    nt = (((1,), (1,)), ((), ()))

    def update(hh, scores, shift):
        vt = vt_ref[hh * LANES:(hh + 1) * LANES, :]
        for c, s in enumerate(scores):
            m_old = m_ref[hh, c]
            m_new = jnp.maximum(m_old, jnp.max(s, axis=0, keepdims=True) + shift)
            alpha = jnp.exp2(m_old - m_new)
            p = jnp.exp2(s - (m_new - shift))
            l_ref[hh, c] = alpha * l_ref[hh, c] + jnp.sum(p, axis=0, keepdims=True)
            acc_ref[hh, c] = alpha * acc_ref[hh, c] + jnp.dot(vt, p.astype(BF16),
                                                              preferred_element_type=F32)
            m_ref[hh, c] = m_new

    keys_before = i * tq >= (j + 1) * tk
    keys_after = (i + 1) * tq <= j * tk
    off_diagonal = jnp.logical_or(keys_before, keys_after)

    @pl.when(off_diagonal)
    def _():
        sign = jnp.where(keys_after, -1.0, 1.0).astype(BF16)
        gap = jnp.abs(i * tq - j * tk).astype(F32)
        for hh in range(hb):
            sl = slice(hh * LANES, (hh + 1) * LANES)
            q = q_ref[:, sl]
            k = k_ref[:, sl]
            aq = aq_ref[hh]
            ak = ak_ref[hh] * sign
            s0 = lax.dot_general(jnp.where(k_first, k, ak), jnp.where(q_first, q, aq), nt,
                                 preferred_element_type=F32)
            s1 = lax.dot_general(jnp.where(k_first, ak, k), jnp.where(q_first, aq, q), nt,
                                 preferred_element_type=F32)
            update(hh, (s0, s1), -(slope_ref[hg * hb + hh] * LOG2E) * gap)

    @pl.when(jnp.logical_not(off_diagonal))
    def _():
        kpos = lax.broadcasted_iota(I32, (tk, tq), 0) + j * tk
        qpos = lax.broadcasted_iota(I32, (tk, tq), 1) + i * tq
        dist = jnp.abs(kpos - qpos).astype(F32)
        for hh in range(hb):
            sl = slice(hh * LANES, (hh + 1) * LANES)
            q = q_ref[:, sl]
            k = k_ref[:, sl]
            zero = jnp.zeros_like(q)
            bias = dist * (-(slope_ref[hg * hb + hh] * LOG2E))
            s0 = lax.dot_general(k, jnp.where(q_first, q, zero), nt, preferred_element_type=F32) + bias
            s1 = lax.dot_general(k, jnp.where(q_first, zero, q), nt, preferred_element_type=F32) + bias
            update(hh, (s0, s1), 0.0)

    @pl.when(j == nk - 1)
    def _():
        for hh in range(hb):
            o = acc_ref[hh, 0] / l_ref[hh, 0] - lam_ref[0] * (acc_ref[hh, 1] / l_ref[hh, 1])
            ms = jnp.mean(o * o, axis=0, keepdims=True)
            y = o * lax.rsqrt(ms + RMS_EPS) * (g_ref[...] * out_scale)
            o_ref[:, hh * LANES:(hh + 1) * LANES] = y.T.astype(o_ref.dtype)


def _attention(qk, vt, row_off, slopes, lam, sub_g, batch, seq, out_scale, tq=512, tk=1024, hb=4):
    tq = min(tq, seq)
    tk = min(tk, seq // 4)
    nq, nk = seq // tq, seq // tk
    assert row_off % tq == 0 and row_off % tk == 0 and ATT_HEADS % hb == 0
    oq, ok = row_off // tq, row_off // tk
    ng = ATT_HEADS // hb
    aq, ak = _alibi_columns(slopes, tq, tk)
    kern = functools.partial(_attn_kernel, tq=tq, tk=tk, nk=nk, hb=hb, out_scale=out_scale)
    smem = pl.BlockSpec(memory_space=pltpu.SMEM)
    return pl.pallas_call(
        kern,
        grid=(batch, ng, nq, nk),
        in_specs=[smem, smem,
                  pl.BlockSpec((tq, hb * LANES), lambda b, h, i, j: (oq + b * nq + i, h)),
                  pl.BlockSpec((tk, hb * LANES), lambda b, h, i, j: (ok + b * nk + j, ng + h)),
                  pl.BlockSpec((hb * LANES, tk), lambda b, h, i, j: (h, ok + b * nk + j)),
                  pl.BlockSpec((hb, tq, LANES), lambda b, h, i, j: (h, 0, 0)),
                  pl.BlockSpec((hb, tk, LANES), lambda b, h, i, j: (h, 0, 0)),
                  pl.BlockSpec((LANES, 1), lambda b, h, i, j: (0, 0))],
        out_specs=pl.BlockSpec((tq, hb * LANES), lambda b, h, i, j: (b * nq + i, h)),
        out_shape=jax.ShapeDtypeStruct((batch * seq, ATT_HEADS * ATT_VDIM), BF16),
        scratch_shapes=[pltpu.VMEM((hb, 2, 1, tq), F32), pltpu.VMEM((hb, 2, 1, tq), F32),
                        pltpu.VMEM((hb, 2, LANES, tq), F32)],
        compiler_params=_cparams(("parallel", "parallel", "parallel", "arbitrary")),
        name="diff_attention",
    )(slopes, lam, qk, qk, vt, aq, ak, sub_g.reshape(LANES, 1).astype(F32))


def _shortconv_kernel(z_ref, w_ref, b_ref, o_ref):
    z = z_ref[...]
    n = z.shape[0]
    row = lax.broadcasted_iota(I32, z.shape, 0)
    prev = jnp.where(row == 0, 0.0, pltpu.roll(z, 1, axis=0))
    nxt = jnp.where(row == n - 1, 0.0, pltpu.roll(z, n - 1, axis=0))
    w = w_ref[...]
    o_ref[...] = prev * w[0:1] + z * w[1:2] + nxt * w[2:3] + b_ref[...]


def _shortconv(zh, row_off, conv_w, conv_b, batch, seq, cb=256):
    c3 = zh.shape[1]
    c = c3 // 3
    ncb = c // cb
    assert row_off % seq == 0
    ob = row_off // seq
    out = pl.pallas_call(
        _shortconv_kernel,
        grid=(batch, 3, ncb),
        in_specs=[pl.BlockSpec((seq, cb), lambda b, p, j: (ob + b, p * ncb + j)),
                  pl.BlockSpec((3, cb), lambda b, p, j: (0, p * ncb + j)),
                  pl.BlockSpec((1, cb), lambda b, p, j: (0, p * ncb + j))],
        out_specs=pl.BlockSpec((None, None, seq, cb), lambda b, p, j: (p, b, 0, j)),
        out_shape=jax.ShapeDtypeStruct((3, batch, seq, c), F32),
        compiler_params=_cparams(("parallel", "parallel", "parallel")),
        name="hyena_shortconv",
    )(zh, conv_w.astype(F32), conv_b.reshape(1, c3).astype(F32))
    return out


def _filter_kernel(z_ref, w1_ref, b1_ref, w2_ref, b2_ref, w3_ref, b3_ref, fr_ref, w4_ref,
                   t_ref, d_ref, o_ref, h_ref, *, tl):
    i = pl.program_id(0)
    g = pl.program_id(1)
    hi = lax.Precision.HIGHEST

    @pl.when(g == 0)
    def _():
        fr = fr_ref[...]
        h = jnp.sin(fr * (jnp.dot(z_ref[...], w1_ref[...], precision=hi, preferred_element_type=F32) + b1_ref[...]))
        h = jnp.sin(fr * (jnp.dot(h, w2_ref[...], precision=hi, preferred_element_type=F32) + b2_ref[...]))
        h_ref[...] = jnp.sin(fr * (jnp.dot(h, w3_ref[...], precision=hi, preferred_element_type=F32) + b3_ref[...]))

    f = jnp.dot(h_ref[...].astype(BF16), w4_ref[...], preferred_element_type=F32)
    f = f * jnp.exp(-t_ref[...] * d_ref[...])
    row = lax.broadcasted_iota(I32, f.shape, 0) + i * tl
    drop = jnp.logical_and(row == 0, g % 2 == 1)
    o_ref[...] = jnp.where(drop, 0.0, f)


def _hyena_filter_signals(seq, w1, b1, w2, b2, w3, b3, w4, freq, n_ch, tl=512):
    t = jnp.linspace(0.0, 1.0, seq, dtype=F32)[:, None]
    w = 2.0 * math.pi * jnp.arange(seq, dtype=F32)[:, None] / seq
    f = jnp.linspace(1e-4, FILTER_BANDS - 1, FILTER_BANDS, dtype=F32)[None, :]
    z = jnp.concatenate([t, jnp.cos(f * w), -jnp.sin(f * w)], axis=-1)
    deltas = jnp.abs(jnp.linspace(math.log(DECAY_FAST) / DECAY_TARGET,
                                  math.log(DECAY_SLOW) / DECAY_TARGET, n_ch, dtype=F32))[None, :]
    hid = w1.shape[1]
    emb = LANES
    z = jnp.pad(z, ((0, 0), (0, emb - z.shape[1])))
    w1 = jnp.pad(w1.astype(F32), ((0, emb - w1.shape[0]), (0, 0)))
    tl = min(tl, seq)
    full = lambda shape: pl.BlockSpec(shape, lambda i, g: tuple(0 for _ in shape))
    return pl.pallas_call(
        functools.partial(_filter_kernel, tl=tl),
        grid=(seq // tl, 2 * HYENA_ORDER),
        in_specs=[pl.BlockSpec((tl, emb), lambda i, g: (i, 0)),
                  full((emb, hid)), full((1, hid)), full((hid, hid)), full((1, hid)),
                  full((hid, hid)), full((1, hid)), full((1, hid)),
                  pl.BlockSpec((hid, n_ch), lambda i, g: (0, g)),
                  pl.BlockSpec((tl, 1), lambda i, g: (i, 0)),
                  full((1, n_ch))],
        out_specs=pl.BlockSpec((None, tl, n_ch), lambda i, g: (g, i, 0)),
        out_shape=jax.ShapeDtypeStruct((2 * HYENA_ORDER, seq, n_ch), F32),
        scratch_shapes=[pltpu.VMEM((tl, hid), F32)],
        compiler_params=_cparams(("parallel", "arbitrary")),
        name="hyena_filter_mlp",
    )(z, w1.astype(F32), b1.reshape(1, hid).astype(F32), w2.astype(F32), b2.reshape(1, hid).astype(F32),
      w3.astype(F32), b3.reshape(1, hid).astype(F32), freq.reshape(1, hid).astype(F32), w4.astype(BF16),
      t, deltas)


def _dft_tables(r):
    n = r * r
    k2 = jnp.arange(r, dtype=I32)
    n2 = jnp.arange(r // 2, dtype=I32)
    ang1 = (2.0 * math.pi / r) * ((k2[:, None] * n2[None, :]) % r).astype(F32)
    f1 = jnp.concatenate([jnp.cos(ang1), -jnp.sin(ang1)], axis=0)
    k1 = jnp.arange(r, dtype=I32)
    n1 = jnp.arange(r, dtype=I32)
    ang_a = (2.0 * math.pi / r) * ((k1[:, None] * n1[None, :]) % r).astype(F32)
    ang_b = (2.0 * math.pi / n) * (k2[:, None] * n1[None, :]).astype(F32)
    ca, sa = jnp.cos(ang_a)[None], jnp.sin(ang_a)[None]
    cb, sb = jnp.cos(ang_b)[:, None], jnp.sin(ang_b)[:, None]
    mr, mi_ = ca * cb - sa * sb, -(sa * cb + ca * sb)
    mf = jnp.concatenate([jnp.concatenate([mr, -mi_], axis=2),
                          jnp.concatenate([mi_, mr], axis=2)], axis=1)
    minv = jnp.swapaxes(mf, 1, 2)
    g3 = jnp.concatenate([jnp.cos(ang1.T), -jnp.sin(ang1.T)], axis=1) * (1.0 / n)
    eye = jnp.eye(N1_BLOCK, dtype=F32)
    f1, g3 = jnp.kron(f1, eye), jnp.kron(g3, eye)
    return f1.astype(BF16), mf.astype(BF16), minv.astype(BF16), g3.astype(BF16)


N1_BLOCK = SUBLANES


def _pack_complex(re, im):
    rb = lax.bitcast_convert_type(re.astype(BF16).astype(F32), U32)
    ib = lax.bitcast_convert_type(im.astype(BF16).astype(F32), U32)
    return (rb >> 16) | ib


def _unpack_complex_rows(w):
    re, im = _unpack(w)
    return jnp.concatenate([re, im], axis=0).astype(BF16)


def _fft1_kernel(f_ref, x_ref, o_ref, *, r):
    n_ch = x_ref.shape[-1]
    x = x_ref[...].reshape((r // 2) * N1_BLOCK, n_ch).astype(BF16)
    res = jnp.dot(f_ref[...], x, preferred_element_type=F32)
    half = r * N1_BLOCK
    o_ref[...] = _pack_complex(res[:half], res[half:]).reshape(r, N1_BLOCK, n_ch)


def _fft_stage1(x4, part, f1, r, n_ch):
    nb = x4.shape[1]
    xv = x4.reshape(x4.shape[0], nb, r // 2, r, n_ch)
    return pl.pallas_call(
        functools.partial(_fft1_kernel, r=r),
        grid=(nb, r // N1_BLOCK),
        in_specs=[pl.BlockSpec(f1.shape, lambda b, j: (0, 0)),
                  pl.BlockSpec((None, None, r // 2, N1_BLOCK, n_ch), lambda b, j: (part, b, 0, j, 0))],
        out_specs=pl.BlockSpec((None, r, N1_BLOCK, n_ch), lambda b, j: (b, 0, j, 0)),
        out_shape=jax.ShapeDtypeStruct((nb, r, r, n_ch), U32),
        compiler_params=_cparams(("parallel", "parallel")),
        name="hyena_dft_stage1",
    )(f1, xv)


def _k2_block(r):
    return max(1, min(r, (4 * LANES) // r))


def _filter_spec_kernel(mf_ref, bf_ref, bb_ref, o_ref, *, r):
    for kk in range(mf_ref.shape[0]):
        m = mf_ref[kk]
        xf = jnp.dot(m, _unpack_complex_rows(bf_ref[kk]), preferred_element_type=F32)
        xb = jnp.dot(m, _unpack_complex_rows(bb_ref[kk]), preferred_element_type=F32)
        o_ref[kk] = _pack_complex(xf[:r] + xb[:r], xf[r:] - xb[r:])


def _filter_spectrum(b1, mf, r, n_ch):
    kb = _k2_block(r)
    blk = lambda sel: pl.BlockSpec((None, kb, r, n_ch), lambda o, k: (2 * o + sel, k, 0, 0))
    return pl.pallas_call(
        functools.partial(_filter_spec_kernel, r=r),
        grid=(HYENA_ORDER, r // kb),
        in_specs=[pl.BlockSpec((kb, 2 * r, 2 * r), lambda o, k: (k, 0, 0)), blk(0), blk(1)],
        out_specs=pl.BlockSpec((None, kb, r, n_ch), lambda o, k: (o, k, 0, 0)),
        out_shape=jax.ShapeDtypeStruct((HYENA_ORDER, r, r, n_ch), U32),
        compiler_params=_cparams(("parallel", "parallel")),
        name="hyena_filter_spectrum",
    )(mf, b1, b1)


def _fft2_kernel(mf_ref, mi_ref, b_ref, h_ref, o_ref, *, r):
    for kk in range(mf_ref.shape[0]):
        x = jnp.dot(mf_ref[kk], _unpack_complex_rows(b_ref[kk]), preferred_element_type=F32)
        xr, xi = x[:r], x[r:]
        hr, hi = _unpack(h_ref[kk])
        y = jnp.concatenate([xr * hr - xi * hi, xr * hi + xi * hr], axis=0).astype(BF16)
        c = jnp.dot(mi_ref[kk], y, preferred_element_type=F32)
        o_ref[kk] = _pack_complex(c[:r], c[r:])


def _fft_stage2(b1, h, order, mf, minv, r, n_ch):
    nb = b1.shape[0]
    kb = _k2_block(r)
    blk = pl.BlockSpec((None, kb, r, n_ch), lambda k, b: (b, k, 0, 0))
    return pl.pallas_call(
        functools.partial(_fft2_kernel, r=r),
        grid=(r // kb, nb),
        in_specs=[pl.BlockSpec((kb, 2 * r, 2 * r), lambda k, b: (k, 0, 0)),
                  pl.BlockSpec((kb, 2 * r, 2 * r), lambda k, b: (k, 0, 0)),
                  blk,
                  pl.BlockSpec((None, kb, r, n_ch), lambda k, b: (order, k, 0, 0))],
        out_specs=blk,
        out_shape=jax.ShapeDtypeStruct((nb, r, r, n_ch), U32),
        compiler_params=_cparams(("parallel", "parallel")),
        name="hyena_dft_stage2",
    )(mf, minv, b1, h)


def _fft3_kernel(g_ref, c_ref, gate_ref, s_ref, bias_ref, ng_ref, *rest, final):
    if final:
        (o_ref,) = rest
    else:
        f_ref, o_ref, b_ref = rest
    r, nb, n_ch = c_ref.shape
    cc = _unpack_complex_rows(c_ref[...].reshape(r * nb, n_ch))
    y = jnp.dot(g_ref[...], cc, preferred_element_type=F32)
    rows = (r // 2) * nb
    s_new = gate_ref[...].reshape(rows, n_ch) * (y + s_ref[...].reshape(rows, n_ch) * bias_ref[...])
    if final:
        ms = jnp.mean(s_new * s_new, axis=-1, keepdims=True)
        s_new = s_new * lax.rsqrt(ms + RMS_EPS) * ng_ref[...]
    o_ref[...] = s_new.reshape(r // 2, nb, n_ch)
    if not final:
        res = jnp.dot(f_ref[...], s_new.astype(BF16), preferred_element_type=F32)
        half = r * nb
        b_ref[...] = _pack_complex(res[:half], res[half:]).reshape(r, nb, n_ch)


def _fft_stage3(c2, g3, z4, gate_part, s4, s_part, bias, norm_g, f1, r, n_ch):
    final = f1 is None
    nb = c2.shape[0]
    zv = z4.reshape(z4.shape[0], nb, r // 2, r, n_ch)
    sv = s4.reshape(s4.shape[0], nb, r // 2, r, n_ch)
    dspec = lambda part: pl.BlockSpec((None, None, r // 2, N1_BLOCK, n_ch), lambda b, j: (part, b, 0, j, 0))
    packed = pl.BlockSpec((None, r, N1_BLOCK, n_ch), lambda b, j: (b, 0, j, 0))
    s_spec = pl.BlockSpec((None, r // 2, N1_BLOCK, n_ch), lambda b, j: (b, 0, j, 0))
    s_shape = jax.ShapeDtypeStruct((nb, r // 2, r, n_ch), F32)
    in_specs = [pl.BlockSpec(g3.shape, lambda b, j: (0, 0)), packed, dspec(gate_part), dspec(s_part),
                pl.BlockSpec((1, n_ch), lambda b, j: (0, 0)), pl.BlockSpec((1, n_ch), lambda b, j: (0, 0))]
    args = [g3, c2, zv, sv, bias.reshape(1, n_ch).astype(F32), norm_g.reshape(1, n_ch).astype(F32)]
    if final:
        out_specs, out_shape = s_spec, s_shape
    else:
        in_specs.append(pl.BlockSpec(f1.shape, lambda b, j: (0, 0)))
        args.append(f1)
        out_specs = [s_spec, packed]
        out_shape = [s_shape, jax.ShapeDtypeStruct((nb, r, r, n_ch), U32)]
    return pl.pallas_call(
        functools.partial(_fft3_kernel, final=final),
        grid=(nb, r // N1_BLOCK),
        in_specs=in_specs,
        out_specs=out_specs,
        out_shape=out_shape,
        compiler_params=_cparams(("parallel", "parallel")),
        name="hyena_dft_stage3",
    )(*args)


def _hyena(zh, row_off, batch, seq, conv_w, conv_b, filt, fft_bias, out_g):
    n_ch = zh.shape[1] // 3
    r = int(round(math.sqrt(2 * seq)))
    assert r * r == 2 * seq and r % 16 == 0
    f1, mf, minv, g3 = _dft_tables(r)
    sig = _hyena_filter_signals(seq, *filt, n_ch=n_ch)
    hb1 = _fft_stage1(sig[None], 0, f1, r, n_ch)
    h = _filter_spectrum(hb1, mf, r, n_ch)
    z4 = _shortconv(zh, row_off, conv_w, conv_b, batch, seq)
    s4, s_part = z4, 2
    b1 = _fft_stage1(s4, s_part, f1, r, n_ch)
    for o in range(HYENA_ORDER):
        c2 = _fft_stage2(b1, h, o, mf, minv, r, n_ch)
        if o < HYENA_ORDER - 1:
            s, b1 = _fft_stage3(c2, g3, z4, o, s4, s_part, fft_bias[o], out_g, f1, r, n_ch)
        else:
            s = _fft_stage3(c2, g3, z4, o, s4, s_part, fft_bias[o], out_g, None, r, n_ch)
        s4, s_part = s.reshape(1, batch, seq, n_ch), 0
    return s4.reshape(batch * seq, n_ch)


def _extract_top(s, key, count):
    vals, keys = [], []
    for _ in range(count):
        m = jnp.max(s, axis=0, keepdims=True)
        kmin = jnp.min(jnp.where(s == m, key, jnp.inf), axis=0, keepdims=True)
        s = jnp.where(key == kmin, -jnp.inf, s)
        vals.append(m)
        keys.append(kmin)
    return vals, keys


def _peer_topk_kernel(q_ref, keys_ref, eid_ref, gate_ref):
    t = q_ref.shape[0]
    nk = PEER_NKEYS
    q = q_ref[...]
    row_key = lax.broadcasted_iota(I32, (nk, t), 0).astype(F32)
    tops = []
    for c in range(2):
        s = lax.dot_general(keys_ref[c], q[:, c * nk:(c + 1) * nk], (((1,), (1,)), ((), ())),
                            preferred_element_type=F32)
        tops.append(_extract_top(s, row_key, PEER_TOPK))
    (v1, i1), (v2, i2) = tops
    rows16 = lax.broadcasted_iota(I32, (PEER_TOPK, t), 0)
    v2a = jnp.zeros((PEER_TOPK, t), F32)
    i2a = jnp.zeros((PEER_TOPK, t), F32)
    for j in range(PEER_TOPK):
        v2a = jnp.where(rows16 == j, v2[j], v2a)
        i2a = jnp.where(rows16 == j, i2[j], i2a)
    n_exp = float(nk * nk)
    half = PEER_TOPK // 2
    rows8 = lax.broadcasted_iota(I32, (half, t), 0)
    pos8 = rows8.astype(F32)
    v2h, i2h = v2a[:half], i2a[:half]
    cand = [v1[0] + v2a]
    ckey = [rows16.astype(F32) * n_exp + (i1[0] * float(nk) + i2a)]
    for i in range(1, half):
        cand.append(jnp.where(rows8 < PEER_TOPK // (i + 1), v1[i] + v2h, -jnp.inf))
        ckey.append((pos8 + float(i * PEER_TOPK)) * n_exp + (i1[i] * float(nk) + i2h))
    v1t = jnp.zeros((half, t), F32)
    i1t = jnp.zeros((half, t), F32)
    for r in range(half):
        v1t = jnp.where(rows8 == r, v1[half + r], v1t)
        i1t = jnp.where(rows8 == r, i1[half + r], i1t)
    cand.append(v1t + v2[0])
    ckey.append((pos8 + float(half)) * (PEER_TOPK * n_exp) + (i1t * float(nk) + i2[0]))
    tv, tk_ = _extract_top(jnp.concatenate(cand, axis=0), jnp.concatenate(ckey, axis=0), PEER_TOPK)
    denom = jnp.zeros((1, t), F32)
    es = []
    for k in range(PEER_TOPK):
        e = jnp.exp(tv[k] - tv[0])
        es.append(e)
        denom = denom + e
    eid = jnp.zeros((PEER_TOPK, t), F32)
    gate = jnp.zeros((PEER_TOPK, t), F32)
    for k in range(PEER_TOPK):
        pos = jnp.floor(tk_[k] * (1.0 / n_exp))
        eid = jnp.where(rows16 == k, tk_[k] - pos * n_exp, eid)
        gate = jnp.where(rows16 == k, es[k] / denom, gate)
    eid_ref[...] = eid.astype(I32)
    gate_ref[...] = gate


def _peer_topk(q, keys, tm=2048):
    t = q.shape[0]
    tm = min(tm, t)
    out_spec = pl.BlockSpec((PEER_TOPK, tm), lambda i, h: (h, i))
    return pl.pallas_call(
        _peer_topk_kernel,
        grid=(t // tm, PEER_HEADS),
        in_specs=[pl.BlockSpec((tm, 2 * PEER_NKEYS), lambda i, h: (i, h)),
                  pl.BlockSpec((None, 2, PEER_NKEYS, PEER_NKEYS), lambda i, h: (h, 0, 0, 0))],
        out_specs=[out_spec, out_spec],
        out_shape=[jax.ShapeDtypeStruct((PEER_HEADS * PEER_TOPK, t), I32),
                   jax.ShapeDtypeStruct((PEER_HEADS * PEER_TOPK, t), F32)],
        compiler_params=_cparams(("parallel", "parallel")),
        name="peer_topk",
    )(q, keys)


def _pack_table(tab):
    e, d = tab.shape
    assert d == 2 * SUBLANES * LANES
    bits = lax.bitcast_convert_type(tab.astype(BF16), jnp.uint16).astype(U32)
    packed = bits[:, :d // 2] | (bits[:, d // 2:] << 16)
    return packed.reshape(e * SUBLANES, LANES)


def _unpack(w):
    lo = lax.bitcast_convert_type(w << 16, F32)
    hi = lax.bitcast_convert_type(w & jnp.uint32(0xFFFF0000), F32)
    return lo, hi


_BITREV8 = (0, 4, 2, 6, 1, 5, 3, 7)


def _sublane_fold8(parts):
    sub = lax.broadcasted_iota(I32, (2 * SUBLANES, LANES), 0) // 2

    def rolled(a, shift):
        return pltpu.bitcast(pltpu.roll(pltpu.bitcast(a, U32), shift, axis=0), BF16)

    lvl = [parts[_BITREV8[r]] for r in range(8)]
    for shift, mask in ((4, sub < 4), (2, (sub % 4) < 2), (1, (sub % 2) < 1)):
        nxt = []
        for a, b in zip(lvl[0::2], lvl[1::2]):
            nxt.append(jnp.where(mask, a + rolled(a, SUBLANES - shift), b + rolled(b, shift)))
        lvl = nxt
    return lvl[0]


def _pair_ranges(half):
    if half == 0:
        return (0, PEER_WINDOW), (PEER_WINDOW, PEER_PAIRS)
    return (PEER_PAIRS - PEER_WINDOW, PEER_PAIRS), (0, PEER_PAIRS - PEER_WINDOW)


def _overflow(n_low, half):
    return n_low > PEER_WINDOW if half == 0 else n_low < PEER_PAIRS - PEER_WINDOW


def _table_rows(tab_ref, off):
    return _unpack(tab_ref[pl.ds(pl.multiple_of(off, SUBLANES), SUBLANES), :])


def _peer_u_kernel(off_ref, nlow_ref, tab_ref, x_ref, o_ref, extra_ref, *, tm, half):
    lane = lax.broadcasted_iota(I32, (SUBLANES, LANES), 1)
    sub = lax.broadcasted_iota(I32, (SUBLANES, LANES), 0)
    lane_grp = lax.shift_right_logical(lane, 3)
    diag = sub == (lane & (SUBLANES - 1))
    main, rest = _pair_ranges(half)

    def folded(t, p0, p1):
        xb = pltpu.bitcast(x_ref[t], BF16)
        out = []
        for g in range(p0 // SUBLANES, p1 // SUBLANES):
            parts = []
            for r in range(SUBLANES):
                off = pl.multiple_of(off_ref[t, g * SUBLANES + r], SUBLANES)
                parts.append(pltpu.bitcast(tab_ref[pl.ds(off, SUBLANES), :], BF16) * xb)
            out.append(pltpu.bitcast(_sublane_fold8(parts), U32))
        return out

    def lane_sums(folds, p0):
        mat = jnp.zeros((SUBLANES, LANES), F32)
        for i, f in enumerate(folds):
            lo, hi = _unpack(f)
            mat = jnp.where(lane_grp == p0 // SUBLANES + i, jnp.sum(lo + hi, axis=-1, keepdims=True), mat)
        return jnp.sum(jnp.where(diag, mat, 0.0), axis=0, keepdims=True)

    def finish(t, folds):
        o_ref[pl.ds(t, 1), :] = lane_sums(folds, main[0]) + extra_ref[pl.ds(t, 1), :]

    def token(t, prev):
        cur = folded(t, *main)
        finish(jnp.maximum(t - 1, 0), prev)
        extra_ref[pl.ds(t, 1), :] = jnp.zeros((1, LANES), F32)

        @pl.when(_overflow(nlow_ref[0, t], half))
        def _():
            extra_ref[pl.ds(t, 1), :] = lane_sums(folded(t, *rest), rest[0])

        return tuple(cur)

    extra_ref[pl.ds(0, 1), :] = jnp.zeros((1, LANES), F32)
    zeros = tuple(jnp.zeros((SUBLANES, LANES), U32) for _ in range((main[1] - main[0]) // SUBLANES))
    last = lax.fori_loop(0, tm, token, zeros)
    finish(tm - 1, last)


def _peer_u(off, nlow3, tab, x4, half, tm=128):
    t = off.shape[0]
    rows = PEER_HALF * SUBLANES
    return pl.pallas_call(
        functools.partial(_peer_u_kernel, tm=tm, half=half),
        grid=(t // tm,),
        in_specs=[pl.BlockSpec((tm, LANES), lambda i: (i, 0), memory_space=pltpu.SMEM),
                  pl.BlockSpec((None, 1, tm), lambda i: (i, 0, 0), memory_space=pltpu.SMEM),
                  pl.BlockSpec((rows, LANES), lambda i: (half, 0), pipeline_mode=pl.Buffered(1)),
                  pl.BlockSpec((tm, SUBLANES, LANES), lambda i: (i, 0, 0))],
        out_specs=pl.BlockSpec((tm, LANES), lambda i: (i, 0)),
        out_shape=jax.ShapeDtypeStruct((t, LANES), F32),
        scratch_shapes=[pltpu.VMEM((tm, LANES), F32)],
        compiler_params=_cparams(("arbitrary",)),
        name="peer_expert_scores",
    )(off, nlow3, tab, x4)


def _peer_coef_kernel(a0_ref, a1_ref, eid_ref, slot_ref, gate_ref, c0_ref, c1_ref):
    low = eid_ref[...] < PEER_HALF
    a = jnp.where(low, a0_ref[...], a1_ref[...])
    gate = jnp.take_along_axis(gate_ref[...], slot_ref[...], axis=1)
    coef = gate * (0.5 * a * (1.0 + lax.erf(a * (1.0 / math.sqrt(2.0)))))
    c0_ref[...] = jnp.where(low, coef, 0.0)
    c1_ref[...] = jnp.where(low, 0.0, coef)


def _peer_coef(a0, a1, eid, slot, gate, tm=1024):
    t = eid.shape[0]
    tm = min(tm, t)
    spec = pl.BlockSpec((tm, LANES), lambda i: (i, 0))
    return pl.pallas_call(
        _peer_coef_kernel,
        grid=(t // tm,),
        in_specs=[spec, spec, spec, spec, spec],
        out_specs=[spec, spec],
        out_shape=[jax.ShapeDtypeStruct((t, LANES), F32)] * 2,
        compiler_params=_cparams(("parallel",)),
        name="peer_coef",
    )(a0, a1, eid, slot, gate)


def _peer_v_kernel(off_ref, nlow_ref, coef_ref, tab_ref, base_ref, *o_refs, tm, half, n_first):
    n_acc = 4
    main, rest = _pair_ranges(half)

    def weighted(t, p0, p1):
        acc_lo = [jnp.zeros((SUBLANES, LANES), F32) for _ in range(n_acc)]
        acc_hi = [jnp.zeros((SUBLANES, LANES), F32) for _ in range(n_acc)]
        for p in range(p0, p1):
            c = coef_ref[t, p]
            lo, hi = _table_rows(tab_ref, off_ref[t, p])
            acc_lo[p % n_acc] = acc_lo[p % n_acc] + c * lo
            acc_hi[p % n_acc] = acc_hi[p % n_acc] + c * hi
        return ((acc_lo[0] + acc_lo[1]) + (acc_lo[2] + acc_lo[3]),
                (acc_hi[0] + acc_hi[1]) + (acc_hi[2] + acc_hi[3]))

    def run(o_ref):
        def store_row(t, lo, hi):
            o_ref[t, 0] = base_ref[t, 0] + lo
            o_ref[t, 1] = base_ref[t, 1] + hi

        def token(t, carry):
            lo, hi = weighted(t, *main)
            store_row(t, lo, hi)

            @pl.when(_overflow(nlow_ref[0, t], half))
            def _():
                lo2, hi2 = weighted(t, *rest)
                store_row(t, lo + lo2, hi + hi2)

            return carry

        lax.fori_loop(0, tm, token, 0)

    if len(o_refs) == 1:
        run(o_refs[0])
    else:
        first = pl.program_id(0) < n_first
        pl.when(first)(lambda: run(o_refs[0]))
        pl.when(jnp.logical_not(first))(lambda: run(o_refs[1]))


def _peer_v(off, nlow3, coef, tab, base, half, tm=128, split=None):
    t = base.shape[0]
    rows = PEER_HALF * SUBLANES
    smem = pl.BlockSpec((tm, LANES), lambda i: (i, 0), memory_space=pltpu.SMEM)
    tile = pl.BlockSpec((tm, 2, SUBLANES, LANES), lambda i: (i, 0, 0, 0))
    if split is None:
        n_first, out_specs = 0, tile
        out_shape = jax.ShapeDtypeStruct((t, 2, SUBLANES, LANES), F32)
    else:
        assert split % tm == 0
        n_first, n_rest = split // tm, (t - split) // tm
        blk = (tm, 2, SUBLANES, LANES)
        out_specs = [pl.BlockSpec(blk, lambda i: (jnp.clip(i, 0, n_first - 1), 0, 0, 0)),
                     pl.BlockSpec(blk, lambda i: (jnp.clip(i - n_first, 0, n_rest - 1), 0, 0, 0))]
        out_shape = [jax.ShapeDtypeStruct((split, 2, SUBLANES, LANES), F32),
                     jax.ShapeDtypeStruct((t - split, 2, SUBLANES, LANES), F32)]
    return pl.pallas_call(
        functools.partial(_peer_v_kernel, tm=tm, half=half, n_first=n_first),
        grid=(t // tm,),
        in_specs=[smem,
                  pl.BlockSpec((None, 1, tm), lambda i: (i, 0, 0), memory_space=pltpu.SMEM),
                  smem,
                  pl.BlockSpec((rows, LANES), lambda i: (half, 0), pipeline_mode=pl.Buffered(1)),
                  tile],
        out_specs=out_specs,
        out_shape=out_shape,
        compiler_params=_cparams(("arbitrary",)),
        name="peer_expert_sum",
    )(off, nlow3, coef, tab, base)


def _peer(x1, split, norm2_g, wq_bf, keys_bf, u_packed, v_packed, tm=128):
    t, d = x1.shape
    tm = min(tm, t)
    (xn_bf,) = _rmsnorm(x1, norm2_g, (BF16,))
    q = _matmul(xn_bf, wq_bf, wq_bf.shape[1], 0, BF16)
    eid_t, gate_t = _peer_topk(q, keys_bf)
    eid, gate = eid_t.T, gate_t.T
    n_exp = PEER_NKEYS * PEER_NKEYS
    high = (eid >= PEER_HALF).astype(I32)
    nlow3 = (PEER_PAIRS - jnp.sum(high, axis=-1, dtype=I32)).reshape(t // tm, 1, tm)
    slot = lax.broadcasted_iota(I32, eid.shape, 1)
    order = jnp.sort((high * PEER_PAIRS + slot) * n_exp + eid, axis=-1)
    eid = order % n_exp
    slot = (order // n_exp) % PEER_PAIRS
    off = (eid & (PEER_HALF - 1)) * SUBLANES
    xw = _pack_table(xn_bf).reshape(t, SUBLANES, LANES)
    a0 = _peer_u(off, nlow3, u_packed, xw, 0, tm)
    a1 = _peer_u(off, nlow3, u_packed, xw, 1, tm)
    c0, c1 = _peer_coef(a0, a1, eid, slot, gate)
    y = _peer_v(off, nlow3, c0, v_packed, x1.reshape(t, 2, SUBLANES, LANES), 0, tm)
    ya, yb = _peer_v(off, nlow3, c1, v_packed, y, 1, tm, split=split)
    return ya.reshape(split, d), yb.reshape(t - split, d)


def kernel(x_prompt, x_sample, norm1_g, w_in, q_norm_g, k_norm_g, lambda_q1, lambda_k1, lambda_q2, lambda_k2, attn_sub_g, conv_w, conv_b, filt_w1, filt_b1, filt_w2, filt_b2, filt_w3, filt_b3, filt_w4, filt_freq, fft_bias, hyena_out_g, w_out, norm2_g, peer_wq, peer_keys, peer_u, peer_v):
    depth = w_in.shape[0]
    d_model = x_prompt.shape[-1]
    att_w = ATT_HEADS * ATT_VDIM
    shapes = [x_prompt.shape[:2], x_sample.shape[:2]]
    xs = [x_prompt.reshape(-1, d_model), x_sample.reshape(-1, d_model)]
    n0 = xs[0].shape[0]
    slopes = 2.0 ** (-8.0 * jnp.arange(1, ATT_HEADS + 1, dtype=F32) / ATT_HEADS)

    for l in range(depth):
        lambda_init = 0.8 - 0.6 * math.exp(-0.3 * l)
        lam = (jnp.exp(jnp.sum(lambda_q1[l].astype(F32) * lambda_k1[l].astype(F32)))
               - jnp.exp(jnp.sum(lambda_q2[l].astype(F32) * lambda_k2[l].astype(F32)))
               + lambda_init).reshape(1)
        w_in_bf = w_in[l].astype(BF16)
        q_gain = jnp.tile(q_norm_g[l].astype(F32), 2 * ATT_HEADS) * (ATT_QKDIM ** -0.5 * LOG2E)
        k_gain = jnp.tile(k_norm_g[l].astype(F32), 2 * ATT_HEADS)
        qk_gain = jnp.concatenate([q_gain, k_gain]).reshape(1, 2 * att_w)

        h_bf = _rmsnorm_stacked(xs[0], xs[1], norm1_g[l], BF16)
        qk = _matmul(h_bf, w_in_bf, 2 * att_w, 0, BF16, mode="qknorm", extra=qk_gain)
        vt = _matmul_nt(w_in_bf[:, 2 * att_w:3 * att_w].T, h_bf, BF16)
        zh = _matmul(h_bf, w_in_bf, w_in.shape[2] - 3 * att_w, 3 * att_w, F32)

        filt = (filt_w1[l], filt_b1[l], filt_w2[l], filt_b2[l], filt_w3[l], filt_b3[l], filt_w4[l], filt_freq[l])
        segs, row = [], 0
        for (b, s), x_seg in zip(shapes, xs):
            att = _attention(qk, vt, row, slopes, lam, attn_sub_g[l], b, s, 1.0 - lambda_init)
            hy = _hyena(zh, row, b, s, conv_w[l], conv_b[l], filt, fft_bias[l], hyena_out_g[l])
            segs.append((att, hy, x_seg))
            row += b * s
        x1 = _outproj_stacked(segs[0], segs[1], w_out[l].astype(BF16))

        xs = _peer(x1, n0, norm2_g[l], peer_wq[l].astype(BF16), peer_keys[l].astype(BF16),
                   _pack_table(peer_u[l]), _pack_table(peer_v[l]))

    return (xs[0].reshape(x_prompt.shape), xs[1].reshape(x_sample.shape))
```

```python
import functools
import math

import jax
import jax.numpy as jnp
from jax import lax
from jax.experimental import pallas as pl
from jax.experimental.pallas import tpu as pltpu

F32 = jnp.float32
BF16 = jnp.bfloat16
I32 = jnp.int32
U32 = jnp.uint32

RMS_EPS = 1e-6
LOG2E = 1.4426950408889634
LANES = 128
SUBLANES = 8
VMEM_LIMIT_BYTES = 56 * 1024 * 1024

ATT_HEADS = 8
ATT_VDIM = 128
ATT_QKDIM = 64
HYENA_ORDER = 2
FILTER_BANDS = 16
DECAY_FAST = 0.3
DECAY_SLOW = 1.5
DECAY_TARGET = 1e-2
PEER_HEADS = 8
PEER_NKEYS = 128
PEER_TOPK = 16
PEER_HALF = PEER_NKEYS * PEER_NKEYS // 2
PEER_PAIRS = PEER_HEADS * PEER_TOPK
PEER_WINDOW = 80


def _cparams(sem, vmem=VMEM_LIMIT_BYTES):
    return pltpu.CompilerParams(dimension_semantics=sem, vmem_limit_bytes=vmem)


def _segment_spec(block, start, count, minor=0):
    return pl.BlockSpec(block, lambda i: (jnp.clip(i - start, 0, count - 1), minor))


def _rmsnorm_kernel(x_ref, g_ref, *o_refs):
    x = x_ref[...]
    ms = jnp.mean(x * x, axis=-1, keepdims=True)
    y = x * lax.rsqrt(ms + RMS_EPS) * g_ref[...]
    for o_ref in o_refs:
        o_ref[...] = y.astype(o_ref.dtype)


def _rmsnorm2_kernel(xa_ref, xb_ref, g_ref, o_ref, *, na):
    first = pl.program_id(0) < na
    x = jnp.where(first, xa_ref[...], xb_ref[...])
    ms = jnp.mean(x * x, axis=-1, keepdims=True)
    o_ref[...] = (x * lax.rsqrt(ms + RMS_EPS) * g_ref[...]).astype(o_ref.dtype)


def _rmsnorm_stacked(xa, xb, g, out_dtype, tm=512):
    d = xa.shape[1]
    na, nb = xa.shape[0] // tm, xb.shape[0] // tm
    assert xa.shape[0] % tm == 0 and xb.shape[0] % tm == 0
    return pl.pallas_call(
        functools.partial(_rmsnorm2_kernel, na=na),
        grid=(na + nb,),
        in_specs=[_segment_spec((tm, d), 0, na), _segment_spec((tm, d), na, nb),
                  pl.BlockSpec((1, d), lambda i: (0, 0))],
        out_specs=pl.BlockSpec((tm, d), lambda i: (i, 0)),
        out_shape=jax.ShapeDtypeStruct((xa.shape[0] + xb.shape[0], d), out_dtype),
        compiler_params=_cparams(("parallel",)),
        name="rmsnorm_stacked",
    )(xa, xb, g.reshape(1, d).astype(F32))


def _rmsnorm(x, g, out_dtypes, tm=512):
    t, d = x.shape
    tm = min(tm, t)
    spec = pl.BlockSpec((tm, d), lambda i: (i, 0))
    return pl.pallas_call(
        _rmsnorm_kernel,
        grid=(t // tm,),
        in_specs=[spec, pl.BlockSpec((1, d), lambda i: (0, 0))],
        out_specs=[spec for _ in out_dtypes],
        out_shape=[jax.ShapeDtypeStruct((t, d), dt) for dt in out_dtypes],
        compiler_params=_cparams(("parallel",)),
        name="rmsnorm",
    )(x, g.reshape(1, d).astype(F32))


def _group_rms_scale(x, gain):
    lane = lax.broadcasted_iota(I32, x.shape, 1)
    lo = lane < ATT_QKDIM
    x2 = x * x
    s_lo = jnp.sum(jnp.where(lo, x2, 0.0), axis=-1, keepdims=True)
    s_hi = jnp.sum(jnp.where(lo, 0.0, x2), axis=-1, keepdims=True)
    ms = jnp.where(lo, s_lo, s_hi) * (1.0 / ATT_QKDIM)
    return x * lax.rsqrt(ms + RMS_EPS) * gain


def _mm_kernel(a_ref, b_ref, *rest, mode):
    acc = jnp.dot(a_ref[...], b_ref[...], preferred_element_type=F32)
    if mode == "plain":
        (o_ref,) = rest
        o_ref[...] = acc.astype(o_ref.dtype)
    elif mode == "qknorm":
        g_ref, o_ref = rest
        for c in range(acc.shape[1] // LANES):
            sl = slice(c * LANES, (c + 1) * LANES)
            o_ref[:, sl] = _group_rms_scale(acc[:, sl], g_ref[:, sl]).astype(o_ref.dtype)
    else:
        raise ValueError(mode)


def _matmul(a, b, n_cols, col_off, out_dtype, mode="plain", extra=None, tm=1024, tn=1024):
    m, k = a.shape
    tm = min(tm, m)
    tn = min(tn, n_cols)
    assert col_off % tn == 0 and n_cols % tn == 0 and m % tm == 0
    off = col_off // tn
    in_specs = [pl.BlockSpec((tm, k), lambda i, j: (i, 0)),
                pl.BlockSpec((k, tn), lambda i, j: (0, j + off))]
    args = [a, b]
    if mode == "qknorm":
        in_specs.append(pl.BlockSpec((1, tn), lambda i, j: (0, j)))
        args.append(extra)
    return pl.pallas_call(
        functools.partial(_mm_kernel, mode=mode),
        grid=(m // tm, n_cols // tn),
        in_specs=in_specs,
        out_specs=pl.BlockSpec((tm, tn), lambda i, j: (i, j)),
        out_shape=jax.ShapeDtypeStruct((m, n_cols), out_dtype),
        compiler_params=_cparams(("parallel", "arbitrary")),
        name="matmul_" + mode,
    )(*args)


def _outproj_kernel(att_a, hy_a, x_a, att_b, hy_b, x_b, w_ref, o_ref, *, na):
    kw = att_a.shape[1]

    def run(att_ref, hy_ref, x_ref):
        acc = jnp.dot(att_ref[...], w_ref[:kw, :], preferred_element_type=F32)
        acc = acc + jnp.dot(hy_ref[...].astype(BF16), w_ref[kw:, :], preferred_element_type=F32)
        o_ref[...] = acc + x_ref[...]

    first = pl.program_id(0) < na
    pl.when(first)(lambda: run(att_a, hy_a, x_a))
    pl.when(jnp.logical_not(first))(lambda: run(att_b, hy_b, x_b))


def _outproj_stacked(seg_a, seg_b, w, tm=512, tn=1024):
    n_out = w.shape[1]
    na, nb = seg_a[0].shape[0] // tm, seg_b[0].shape[0] // tm
    assert seg_a[0].shape[0] % tm == 0 and seg_b[0].shape[0] % tm == 0 and n_out % tn == 0

    def specs(seg, start, count):
        att, hy, x = seg
        row = lambda i, j: (jnp.clip(i - start, 0, count - 1), 0)
        return [pl.BlockSpec((tm, att.shape[1]), row), pl.BlockSpec((tm, hy.shape[1]), row),
                pl.BlockSpec((tm, tn), lambda i, j: (jnp.clip(i - start, 0, count - 1), j))]

    return pl.pallas_call(
        functools.partial(_outproj_kernel, na=na),
        grid=(na + nb, n_out // tn),
        in_specs=specs(seg_a, 0, na) + specs(seg_b, na, nb) + [pl.BlockSpec((w.shape[0], tn), lambda i, j: (0, j))],
        out_specs=pl.BlockSpec((tm, tn), lambda i, j: (i, j)),
        out_shape=jax.ShapeDtypeStruct(((na + nb) * tm, n_out), F32),
        compiler_params=_cparams(("parallel", "arbitrary")),
        name="out_projection",
    )(*seg_a, *seg_b, w)


def _mm_nt_kernel(w_ref, a_ref, o_ref):
    o_ref[...] = lax.dot_general(w_ref[...], a_ref[...], (((1,), (1,)), ((), ())),
                                 preferred_element_type=F32).astype(o_ref.dtype)


def _matmul_nt(wt, a, out_dtype, tm=1024):
    n, k = wt.shape
    m = a.shape[0]
    tm = min(tm, m)
    return pl.pallas_call(
        _mm_nt_kernel,
        grid=(m // tm,),
        in_specs=[pl.BlockSpec((n, k), lambda i: (0, 0)), pl.BlockSpec((tm, k), lambda i: (i, 0))],
        out_specs=pl.BlockSpec((n, tm), lambda i: (0, i)),
        out_shape=jax.ShapeDtypeStruct((n, m), out_dtype),
        compiler_params=_cparams(("parallel",)),
        name="matmul_nt",
    )(wt, a)


def _alibi_columns(slopes, tq, tk):
    return [_alibi_side(slopes, tq, True), _alibi_side(slopes, tk, False)]


def _alibi_side(slopes, n, query_side):
    pos = jnp.arange(n, dtype=F32)
    val = (slopes.astype(F32) * LOG2E)[:, None] * pos[None, :]

    def pieces(x):
        p1 = x.astype(BF16)
        r1 = x - p1.astype(F32)
        p2 = r1.astype(BF16)
        p3 = (r1 - p2.astype(F32)).astype(BF16)
        return [p1, p2, p3]

    ones = [jnp.ones_like(val, BF16)] * 3
    six = jnp.stack(pieces(-val) + ones if query_side else ones + pieces(val), axis=-1)
    pad = jnp.zeros(val.shape + (ATT_QKDIM - 6,), BF16)
    return jnp.concatenate([six, pad, six, pad], axis=-1)


def _attn_kernel(slope_ref, lam_ref, q_ref, k_ref, vt_ref, aq_ref, ak_ref, g_ref, o_ref,
                 m_ref, l_ref, acc_ref, *, tq, tk, nk, hb, out_scale):
    hg = pl.program_id(1)
    i = pl.program_id(2)
    j = pl.program_id(3)

    @pl.when(j == 0)
    def _():
        m_ref[...] = jnp.full(m_ref.shape, -jnp.inf, F32)
        l_ref[...] = jnp.zeros(l_ref.shape, F32)
        acc_ref[...] = jnp.zeros(acc_ref.shape, F32)

    q_first = lax.broadcasted_iota(I32, (tq, LANES), 1) < ATT_QKDIM
    k_first = lax.broadcasted_iota(I32, (tk, LANES), 1) < ATT_QKDIM
    nt = (((1,), (1,)), ((), ()))

    def update(hh, scores, shift):
        vt = vt_ref[hh * LANES:(hh + 1) * LANES, :]
        for c, s in enumerate(scores):
            m_old = m_ref[hh, c]
            m_new = jnp.maximum(m_old, jnp.max(s, axis=0, keepdims=True) + shift)
            alpha = jnp.exp2(m_old - m_new)
            p = jnp.exp2(s - (m_new - shift))
            l_ref[hh, c] = alpha * l_ref[hh, c] + jnp.sum(p, axis=0, keepdims=True)
            acc_ref[hh, c] = alpha * acc_ref[hh, c] + jnp.dot(vt, p.astype(BF16),
                                                              preferred_element_type=F32)
            m_ref[hh, c] = m_new

    keys_before = i * tq >= (j + 1) * tk
    keys_after = (i + 1) * tq <= j * tk
    off_diagonal = jnp.logical_or(keys_before, keys_after)

    @pl.when(off_diagonal)
    def _():
        sign = jnp.where(keys_after, -1.0, 1.0).astype(BF16)
        gap = jnp.abs(i * tq - j * tk).astype(F32)
        for hh in range(hb):
            sl = slice(hh * LANES, (hh + 1) * LANES)
            q = q_ref[:, sl]
            k = k_ref[:, sl]
            aq = aq_ref[hh]
            ak = ak_ref[hh] * sign
            s0 = lax.dot_general(jnp.where(k_first, k, ak), jnp.where(q_first, q, aq), nt,
                                 preferred_element_type=F32)
            s1 = lax.dot_general(jnp.where(k_first, ak, k), jnp.where(q_first, aq, q), nt,
                                 preferred_element_type=F32)
            update(hh, (s0, s1), -(slope_ref[hg * hb + hh] * LOG2E) * gap)

    @pl.when(jnp.logical_not(off_diagonal))
    def _():
        kpos = lax.broadcasted_iota(I32, (tk, tq), 0) + j * tk
        qpos = lax.broadcasted_iota(I32, (tk, tq), 1) + i * tq
        dist = jnp.abs(kpos - qpos).astype(F32)
        for hh in range(hb):
            sl = slice(hh * LANES, (hh + 1) * LANES)
            q = q_ref[:, sl]
            k = k_ref[:, sl]
            zero = jnp.zeros_like(q)
            bias = dist * (-(slope_ref[hg * hb + hh] * LOG2E))
            s0 = lax.dot_general(k, jnp.where(q_first, q, zero), nt, preferred_element_type=F32) + bias
            s1 = lax.dot_general(k, jnp.where(q_first, zero, q), nt, preferred_element_type=F32) + bias
            update(hh, (s0, s1), 0.0)

    @pl.when(j == nk - 1)
    def _():
        for hh in range(hb):
            o = acc_ref[hh, 0] / l_ref[hh, 0] - lam_ref[0] * (acc_ref[hh, 1] / l_ref[hh, 1])
            ms = jnp.mean(o * o, axis=0, keepdims=True)
            y = o * lax.rsqrt(ms + RMS_EPS) * (g_ref[...] * out_scale)
            o_ref[:, hh * LANES:(hh + 1) * LANES] = y.T.astype(o_ref.dtype)


def _attention(qk, vt, row_off, slopes, lam, sub_g, batch, seq, out_scale, tq=512, tk=1024, hb=4):
    tq = min(tq, seq)
    tk = min(tk, seq // 4)
    nq, nk = seq // tq, seq // tk
    assert row_off % tq == 0 and row_off % tk == 0 and ATT_HEADS % hb == 0
    oq, ok = row_off // tq, row_off // tk
    ng = ATT_HEADS // hb
    aq, ak = _alibi_columns(slopes, tq, tk)
    kern = functools.partial(_attn_kernel, tq=tq, tk=tk, nk=nk, hb=hb, out_scale=out_scale)
    smem = pl.BlockSpec(memory_space=pltpu.SMEM)
    return pl.pallas_call(
        kern,
        grid=(batch, ng, nq, nk),
        in_specs=[smem, smem,
                  pl.BlockSpec((tq, hb * LANES), lambda b, h, i, j: (oq + b * nq + i, h)),
                  pl.BlockSpec((tk, hb * LANES), lambda b, h, i, j: (ok + b * nk + j, ng + h)),
                  pl.BlockSpec((hb * LANES, tk), lambda b, h, i, j: (h, ok + b * nk + j)),
                  pl.BlockSpec((hb, tq, LANES), lambda b, h, i, j: (h, 0, 0)),
                  pl.BlockSpec((hb, tk, LANES), lambda b, h, i, j: (h, 0, 0)),
                  pl.BlockSpec((LANES, 1), lambda b, h, i, j: (0, 0))],
        out_specs=pl.BlockSpec((tq, hb * LANES), lambda b, h, i, j: (b * nq + i, h)),
        out_shape=jax.ShapeDtypeStruct((batch * seq, ATT_HEADS * ATT_VDIM), BF16),
        scratch_shapes=[pltpu.VMEM((hb, 2, 1, tq), F32), pltpu.VMEM((hb, 2, 1, tq), F32),
                        pltpu.VMEM((hb, 2, LANES, tq), F32)],
        compiler_params=_cparams(("parallel", "parallel", "parallel", "arbitrary")),
        name="diff_attention",
    )(slopes, lam, qk, qk, vt, aq, ak, sub_g.reshape(LANES, 1).astype(F32))


def _shortconv_kernel(z_ref, w_ref, b_ref, o_ref):
    z = z_ref[...]
    n = z.shape[0]
    row = lax.broadcasted_iota(I32, z.shape, 0)
    prev = jnp.where(row == 0, 0.0, pltpu.roll(z, 1, axis=0))
    nxt = jnp.where(row == n - 1, 0.0, pltpu.roll(z, n - 1, axis=0))
    w = w_ref[...]
    o_ref[...] = prev * w[0:1] + z * w[1:2] + nxt * w[2:3] + b_ref[...]


def _shortconv(zh, row_off, conv_w, conv_b, batch, seq, cb=256):
    c3 = zh.shape[1]
    c = c3 // 3
    ncb = c // cb
    assert row_off % seq == 0
    ob = row_off // seq
    out = pl.pallas_call(
        _shortconv_kernel,
        grid=(batch, 3, ncb),
        in_specs=[pl.BlockSpec((seq, cb), lambda b, p, j: (ob + b, p * ncb + j)),
                  pl.BlockSpec((3, cb), lambda b, p, j: (0, p * ncb + j)),
                  pl.BlockSpec((1, cb), lambda b, p, j: (0, p * ncb + j))],
        out_specs=pl.BlockSpec((None, None, seq, cb), lambda b, p, j: (p, b, 0, j)),
        out_shape=jax.ShapeDtypeStruct((3, batch, seq, c), F32),
        compiler_params=_cparams(("parallel", "parallel", "parallel")),
        name="hyena_shortconv",
    )(zh, conv_w.astype(F32), conv_b.reshape(1, c3).astype(F32))
    return out


def _filter_kernel(z_ref, w1_ref, b1_ref, w2_ref, b2_ref, w3_ref, b3_ref, fr_ref, w4_ref,
                   t_ref, d_ref, o_ref, h_ref, *, tl):
    i = pl.program_id(0)
    g = pl.program_id(1)
    hi = lax.Precision.HIGHEST

    @pl.when(g == 0)
    def _():
        fr = fr_ref[...]
        h = jnp.sin(fr * (jnp.dot(z_ref[...], w1_ref[...], precision=hi, preferred_element_type=F32) + b1_ref[...]))
        h = jnp.sin(fr * (jnp.dot(h, w2_ref[...], precision=hi, preferred_element_type=F32) + b2_ref[...]))
        h_ref[...] = jnp.sin(fr * (jnp.dot(h, w3_ref[...], precision=hi, preferred_element_type=F32) + b3_ref[...]))

    f = jnp.dot(h_ref[...].astype(BF16), w4_ref[...], preferred_element_type=F32)
    f = f * jnp.exp(-t_ref[...] * d_ref[...])
    row = lax.broadcasted_iota(I32, f.shape, 0) + i * tl
    drop = jnp.logical_and(row == 0, g % 2 == 1)
    o_ref[...] = jnp.where(drop, 0.0, f)


def _hyena_filter_signals(seq, w1, b1, w2, b2, w3, b3, w4, freq, n_ch, tl=512):
    t = jnp.linspace(0.0, 1.0, seq, dtype=F32)[:, None]
    w = 2.0 * math.pi * jnp.arange(seq, dtype=F32)[:, None] / seq
    f = jnp.linspace(1e-4, FILTER_BANDS - 1, FILTER_BANDS, dtype=F32)[None, :]
    z = jnp.concatenate([t, jnp.cos(f * w), -jnp.sin(f * w)], axis=-1)
    deltas = jnp.abs(jnp.linspace(math.log(DECAY_FAST) / DECAY_TARGET,
                                  math.log(DECAY_SLOW) / DECAY_TARGET, n_ch, dtype=F32))[None, :]
    hid = w1.shape[1]
    emb = LANES
    z = jnp.pad(z, ((0, 0), (0, emb - z.shape[1])))
    w1 = jnp.pad(w1.astype(F32), ((0, emb - w1.shape[0]), (0, 0)))
    tl = min(tl, seq)
    full = lambda shape: pl.BlockSpec(shape, lambda i, g: tuple(0 for _ in shape))
    return pl.pallas_call(
        functools.partial(_filter_kernel, tl=tl),
        grid=(seq // tl, 2 * HYENA_ORDER),
        in_specs=[pl.BlockSpec((tl, emb), lambda i, g: (i, 0)),
                  full((emb, hid)), full((1, hid)), full((hid, hid)), full((1, hid)),
                  full((hid, hid)), full((1, hid)), full((1, hid)),
                  pl.BlockSpec((hid, n_ch), lambda i, g: (0, g)),
                  pl.BlockSpec((tl, 1), lambda i, g: (i, 0)),
                  full((1, n_ch))],
        out_specs=pl.BlockSpec((None, tl, n_ch), lambda i, g: (g, i, 0)),
        out_shape=jax.ShapeDtypeStruct((2 * HYENA_ORDER, seq, n_ch), F32),
        scratch_shapes=[pltpu.VMEM((tl, hid), F32)],
        compiler_params=_cparams(("parallel", "arbitrary")),
        name="hyena_filter_mlp",
    )(z, w1.astype(F32), b1.reshape(1, hid).astype(F32), w2.astype(F32), b2.reshape(1, hid).astype(F32),
      w3.astype(F32), b3.reshape(1, hid).astype(F32), freq.reshape(1, hid).astype(F32), w4.astype(BF16),
      t, deltas)


def _dft_tables(r):
    n = r * r
    k2 = jnp.arange(r, dtype=I32)
    n2 = jnp.arange(r // 2, dtype=I32)
    ang1 = (2.0 * math.pi / r) * ((k2[:, None] * n2[None, :]) % r).astype(F32)
    f1 = jnp.concatenate([jnp.cos(ang1), -jnp.sin(ang1)], axis=0)
    k1 = jnp.arange(r, dtype=I32)
    n1 = jnp.arange(r, dtype=I32)
    ang_a = (2.0 * math.pi / r) * ((k1[:, None] * n1[None, :]) % r).astype(F32)
    ang_b = (2.0 * math.pi / n) * (k2[:, None] * n1[None, :]).astype(F32)
    ca, sa = jnp.cos(ang_a)[None], jnp.sin(ang_a)[None]
    cb, sb = jnp.cos(ang_b)[:, None], jnp.sin(ang_b)[:, None]
    mr, mi_ = ca * cb - sa * sb, -(sa * cb + ca * sb)
    mf = jnp.concatenate([jnp.concatenate([mr, -mi_], axis=2),
                          jnp.concatenate([mi_, mr], axis=2)], axis=1)
    minv = jnp.swapaxes(mf, 1, 2)
    g3 = jnp.concatenate([jnp.cos(ang1.T), -jnp.sin(ang1.T)], axis=1) * (1.0 / n)
    eye = jnp.eye(N1_BLOCK, dtype=F32)
    f1, g3 = jnp.kron(f1, eye), jnp.kron(g3, eye)
    return f1.astype(BF16), mf.astype(BF16), minv.astype(BF16), g3.astype(BF16)


N1_BLOCK = SUBLANES


def _pack_complex(re, im):
    rb = lax.bitcast_convert_type(re.astype(BF16).astype(F32), U32)
    ib = lax.bitcast_convert_type(im.astype(BF16).astype(F32), U32)
    return (rb >> 16) | ib


def _unpack_complex_rows(w):
    re, im = _unpack(w)
    return jnp.concatenate([re, im], axis=0).astype(BF16)


def _fft1_kernel(f_ref, x_ref, o_ref, *, r):
    n_ch = x_ref.shape[-1]
    x = x_ref[...].reshape((r // 2) * N1_BLOCK, n_ch).astype(BF16)
    res = jnp.dot(f_ref[...], x, preferred_element_type=F32)
    half = r * N1_BLOCK
    o_ref[...] = _pack_complex(res[:half], res[half:]).reshape(r, N1_BLOCK, n_ch)


def _fft_stage1(x4, part, f1, r, n_ch):
    nb = x4.shape[1]
    xv = x4.reshape(x4.shape[0], nb, r // 2, r, n_ch)
    return pl.pallas_call(
        functools.partial(_fft1_kernel, r=r),
        grid=(nb, r // N1_BLOCK),
        in_specs=[pl.BlockSpec(f1.shape, lambda b, j: (0, 0)),
                  pl.BlockSpec((None, None, r // 2, N1_BLOCK, n_ch), lambda b, j: (part, b, 0, j, 0))],
        out_specs=pl.BlockSpec((None, r, N1_BLOCK, n_ch), lambda b, j: (b, 0, j, 0)),
        out_shape=jax.ShapeDtypeStruct((nb, r, r, n_ch), U32),
        compiler_params=_cparams(("parallel", "parallel")),
        name="hyena_dft_stage1",
    )(f1, xv)


def _k2_block(r):
    return max(1, min(r, (8 * LANES) // r))


def _filter_spec_kernel(mf_ref, bf_ref, bb_ref, o_ref, *, r):
    for kk in range(mf_ref.shape[0]):
        m = mf_ref[kk]
        xf = jnp.dot(m, _unpack_complex_rows(bf_ref[kk]), preferred_element_type=F32)
        xb = jnp.dot(m, _unpack_complex_rows(bb_ref[kk]), preferred_element_type=F32)
        o_ref[kk] = _pack_complex(xf[:r] + xb[:r], xf[r:] - xb[r:])


def _filter_spectrum(b1, mf, r, n_ch):
    kb = _k2_block(r)
    blk = lambda sel: pl.BlockSpec((None, kb, r, n_ch), lambda o, k: (2 * o + sel, k, 0, 0))
    return pl.pallas_call(
        functools.partial(_filter_spec_kernel, r=r),
        grid=(HYENA_ORDER, r // kb),
        in_specs=[pl.BlockSpec((kb, 2 * r, 2 * r), lambda o, k: (k, 0, 0)), blk(0), blk(1)],
        out_specs=pl.BlockSpec((None, kb, r, n_ch), lambda o, k: (o, k, 0, 0)),
        out_shape=jax.ShapeDtypeStruct((HYENA_ORDER, r, r, n_ch), U32),
        compiler_params=_cparams(("parallel", "parallel")),
        name="hyena_filter_spectrum",
    )(mf, b1, b1)


def _fft2_kernel(mf_ref, mi_ref, b_ref, h_ref, o_ref, *, r):
    for kk in range(mf_ref.shape[0]):
        x = jnp.dot(mf_ref[kk], _unpack_complex_rows(b_ref[kk]), preferred_element_type=F32)
        xr, xi = x[:r], x[r:]
        hr, hi = _unpack(h_ref[kk])
        y = jnp.concatenate([xr * hr - xi * hi, xr * hi + xi * hr], axis=0).astype(BF16)
        c = jnp.dot(mi_ref[kk], y, preferred_element_type=F32)
        o_ref[kk] = _pack_complex(c[:r], c[r:])


def _fft_stage2(b1, h, order, mf, minv, r, n_ch):
    nb = b1.shape[0]
    kb = _k2_block(r)
    blk = pl.BlockSpec((None, kb, r, n_ch), lambda k, b: (b, k, 0, 0))
    return pl.pallas_call(
        functools.partial(_fft2_kernel, r=r),
        grid=(r // kb, nb),
        in_specs=[pl.BlockSpec((kb, 2 * r, 2 * r), lambda k, b: (k, 0, 0)),
                  pl.BlockSpec((kb, 2 * r, 2 * r), lambda k, b: (k, 0, 0)),
                  blk,
                  pl.BlockSpec((None, kb, r, n_ch), lambda k, b: (order, k, 0, 0))],
        out_specs=blk,
        out_shape=jax.ShapeDtypeStruct((nb, r, r, n_ch), U32),
        compiler_params=_cparams(("parallel", "parallel")),
        name="hyena_dft_stage2",
    )(mf, minv, b1, h)


def _fft3_kernel(g_ref, c_ref, gate_ref, s_ref, bias_ref, ng_ref, *rest, final):
    if final:
        (o_ref,) = rest
    else:
        f_ref, o_ref, b_ref = rest
    r, nb, n_ch = c_ref.shape
    cc = _unpack_complex_rows(c_ref[...].reshape(r * nb, n_ch))
    y = jnp.dot(g_ref[...], cc, preferred_element_type=F32)
    rows = (r // 2) * nb
    s_new = gate_ref[...].reshape(rows, n_ch) * (y + s_ref[...].reshape(rows, n_ch) * bias_ref[...])
    if final:
        ms = jnp.mean(s_new * s_new, axis=-1, keepdims=True)
        s_new = s_new * lax.rsqrt(ms + RMS_EPS) * ng_ref[...]
    o_ref[...] = s_new.reshape(r // 2, nb, n_ch)
    if not final:
        res = jnp.dot(f_ref[...], s_new.astype(BF16), preferred_element_type=F32)
        half = r * nb
        b_ref[...] = _pack_complex(res[:half], res[half:]).reshape(r, nb, n_ch)


def _fft_stage3(c2, g3, z4, gate_part, s4, s_part, bias, norm_g, f1, r, n_ch):
    final = f1 is None
    nb = c2.shape[0]
    zv = z4.reshape(z4.shape[0], nb, r // 2, r, n_ch)
    sv = s4.reshape(s4.shape[0], nb, r // 2, r, n_ch)
    dspec = lambda part: pl.BlockSpec((None, None, r // 2, N1_BLOCK, n_ch), lambda b, j: (part, b, 0, j, 0))
    packed = pl.BlockSpec((None, r, N1_BLOCK, n_ch), lambda b, j: (b, 0, j, 0))
    s_spec = pl.BlockSpec((None, r // 2, N1_BLOCK, n_ch), lambda b, j: (b, 0, j, 0))
    s_shape = jax.ShapeDtypeStruct((nb, r // 2, r, n_ch), F32)
    in_specs = [pl.BlockSpec(g3.shape, lambda b, j: (0, 0)), packed, dspec(gate_part), dspec(s_part),
                pl.BlockSpec((1, n_ch), lambda b, j: (0, 0)), pl.BlockSpec((1, n_ch), lambda b, j: (0, 0))]
    args = [g3, c2, zv, sv, bias.reshape(1, n_ch).astype(F32), norm_g.reshape(1, n_ch).astype(F32)]
    if final:
        out_specs, out_shape = s_spec, s_shape
    else:
        in_specs.append(pl.BlockSpec(f1.shape, lambda b, j: (0, 0)))
        args.append(f1)
        out_specs = [s_spec, packed]
        out_shape = [s_shape, jax.ShapeDtypeStruct((nb, r, r, n_ch), U32)]
    return pl.pallas_call(
        functools.partial(_fft3_kernel, final=final),
        grid=(nb, r // N1_BLOCK),
        in_specs=in_specs,
        out_specs=out_specs,
        out_shape=out_shape,
        compiler_params=_cparams(("parallel", "parallel")),
        name="hyena_dft_stage3",
    )(*args)


def _hyena(zh, row_off, batch, seq, conv_w, conv_b, filt, fft_bias, out_g):
    n_ch = zh.shape[1] // 3
    r = int(round(math.sqrt(2 * seq)))
    assert r * r == 2 * seq and r % 16 == 0
    f1, mf, minv, g3 = _dft_tables(r)
    sig = _hyena_filter_signals(seq, *filt, n_ch=n_ch)
    hb1 = _fft_stage1(sig[None], 0, f1, r, n_ch)
    h = _filter_spectrum(hb1, mf, r, n_ch)
    z4 = _shortconv(zh, row_off, conv_w, conv_b, batch, seq)
    s4, s_part = z4, 2
    b1 = _fft_stage1(s4, s_part, f1, r, n_ch)
    for o in range(HYENA_ORDER):
        c2 = _fft_stage2(b1, h, o, mf, minv, r, n_ch)
        if o < HYENA_ORDER - 1:
            s, b1 = _fft_stage3(c2, g3, z4, o, s4, s_part, fft_bias[o], out_g, f1, r, n_ch)
        else:
            s = _fft_stage3(c2, g3, z4, o, s4, s_part, fft_bias[o], out_g, None, r, n_ch)
        s4, s_part = s.reshape(1, batch, seq, n_ch), 0
    return s4.reshape(batch * seq, n_ch)


def _extract_top(s, key, count):
    vals, keys = [], []
    for _ in range(count):
        m = jnp.max(s, axis=0, keepdims=True)
        kmin = jnp.min(jnp.where(s == m, key, jnp.inf), axis=0, keepdims=True)
        s = jnp.where(key == kmin, -jnp.inf, s)
        vals.append(m)
        keys.append(kmin)
    return vals, keys


def _peer_topk_kernel(q_ref, keys_ref, eid_ref, gate_ref):
    t = q_ref.shape[0]
    nk = PEER_NKEYS
    q = q_ref[...]
    row_key = lax.broadcasted_iota(I32, (nk, t), 0).astype(F32)
    tops = []
    for c in range(2):
        s = lax.dot_general(keys_ref[c], q[:, c * nk:(c + 1) * nk], (((1,), (1,)), ((), ())),
                            preferred_element_type=F32)
        tops.append(_extract_top(s, row_key, PEER_TOPK))
    (v1, i1), (v2, i2) = tops
    rows16 = lax.broadcasted_iota(I32, (PEER_TOPK, t), 0)
    v2a = jnp.zeros((PEER_TOPK, t), F32)
    i2a = jnp.zeros((PEER_TOPK, t), F32)
    for j in range(PEER_TOPK):
        v2a = jnp.where(rows16 == j, v2[j], v2a)
        i2a = jnp.where(rows16 == j, i2[j], i2a)
    n_exp = float(nk * nk)
    half = PEER_TOPK // 2
    rows8 = lax.broadcasted_iota(I32, (half, t), 0)
    pos8 = rows8.astype(F32)
    v2h, i2h = v2a[:half], i2a[:half]
    cand = [v1[0] + v2a]
    ckey = [rows16.astype(F32) * n_exp + (i1[0] * float(nk) + i2a)]
    for i in range(1, half):
        cand.append(jnp.where(rows8 < PEER_TOPK // (i + 1), v1[i] + v2h, -jnp.inf))
        ckey.append((pos8 + float(i * PEER_TOPK)) * n_exp + (i1[i] * float(nk) + i2h))
    v1t = jnp.zeros((half, t), F32)
    i1t = jnp.zeros((half, t), F32)
    for r in range(half):
        v1t = jnp.where(rows8 == r, v1[half + r], v1t)
        i1t = jnp.where(rows8 == r, i1[half + r], i1t)
    cand.append(v1t + v2[0])
    ckey.append((pos8 + float(half)) * (PEER_TOPK * n_exp) + (i1t * float(nk) + i2[0]))
    tv, tk_ = _extract_top(jnp.concatenate(cand, axis=0), jnp.concatenate(ckey, axis=0), PEER_TOPK)
    denom = jnp.zeros((1, t), F32)
    es = []
    for k in range(PEER_TOPK):
        e = jnp.exp(tv[k] - tv[0])
        es.append(e)
        denom = denom + e
    eid = jnp.zeros((PEER_TOPK, t), F32)
    gate = jnp.zeros((PEER_TOPK, t), F32)
    for k in range(PEER_TOPK):
        pos = jnp.floor(tk_[k] * (1.0 / n_exp))
        eid = jnp.where(rows16 == k, tk_[k] - pos * n_exp, eid)
        gate = jnp.where(rows16 == k, es[k] / denom, gate)
    eid_ref[...] = eid.astype(I32)
    gate_ref[...] = gate


def _peer_topk(q, keys, tm=1024):
    t = q.shape[0]
    tm = min(tm, t)
    out_spec = pl.BlockSpec((PEER_TOPK, tm), lambda i, h: (h, i))
    return pl.pallas_call(
        _peer_topk_kernel,
        grid=(t // tm, PEER_HEADS),
        in_specs=[pl.BlockSpec((tm, 2 * PEER_NKEYS), lambda i, h: (i, h)),
                  pl.BlockSpec((None, 2, PEER_NKEYS, PEER_NKEYS), lambda i, h: (h, 0, 0, 0))],
        out_specs=[out_spec, out_spec],
        out_shape=[jax.ShapeDtypeStruct((PEER_HEADS * PEER_TOPK, t), I32),
                   jax.ShapeDtypeStruct((PEER_HEADS * PEER_TOPK, t), F32)],
        compiler_params=_cparams(("parallel", "parallel")),
        name="peer_topk",
    )(q, keys)


def _pack_table(tab):
    e, d = tab.shape
    assert d == 2 * SUBLANES * LANES
    bits = lax.bitcast_convert_type(tab.astype(BF16), jnp.uint16).astype(U32)
    packed = bits[:, :d // 2] | (bits[:, d // 2:] << 16)
    return packed.reshape(e * SUBLANES, LANES)


def _unpack(w):
    lo = lax.bitcast_convert_type(w << 16, F32)
    hi = lax.bitcast_convert_type(w & jnp.uint32(0xFFFF0000), F32)
    return lo, hi


_BITREV8 = (0, 4, 2, 6, 1, 5, 3, 7)


def _sublane_fold8(parts):
    sub = lax.broadcasted_iota(I32, (2 * SUBLANES, LANES), 0) // 2

    def rolled(a, shift):
        return pltpu.bitcast(pltpu.roll(pltpu.bitcast(a, U32), shift, axis=0), BF16)

    lvl = [parts[_BITREV8[r]] for r in range(8)]
    for shift, mask in ((4, sub < 4), (2, (sub % 4) < 2), (1, (sub % 2) < 1)):
        nxt = []
        for a, b in zip(lvl[0::2], lvl[1::2]):
            nxt.append(jnp.where(mask, a + rolled(a, SUBLANES - shift), b + rolled(b, shift)))
        lvl = nxt
    return lvl[0]


def _pair_ranges(half):
    if half == 0:
        return (0, PEER_WINDOW), (PEER_WINDOW, PEER_PAIRS)
    return (PEER_PAIRS - PEER_WINDOW, PEER_PAIRS), (0, PEER_PAIRS - PEER_WINDOW)


def _overflow(n_low, half):
    return n_low > PEER_WINDOW if half == 0 else n_low < PEER_PAIRS - PEER_WINDOW


def _table_rows(tab_ref, off):
    return _unpack(tab_ref[pl.ds(pl.multiple_of(off, SUBLANES), SUBLANES), :])


def _peer_u_kernel(off_ref, nlow_ref, tab_ref, x_ref, o_ref, extra_ref, *, tm, half):
    lane = lax.broadcasted_iota(I32, (SUBLANES, LANES), 1)
    sub = lax.broadcasted_iota(I32, (SUBLANES, LANES), 0)
    lane_grp = lax.shift_right_logical(lane, 3)
    diag = sub == (lane & (SUBLANES - 1))
    main, rest = _pair_ranges(half)

    def folded(t, p0, p1):
        xb = pltpu.bitcast(x_ref[t], BF16)
        out = []
        for g in range(p0 // SUBLANES, p1 // SUBLANES):
            parts = []
            for r in range(SUBLANES):
                off = pl.multiple_of(off_ref[t, g * SUBLANES + r], SUBLANES)
                parts.append(pltpu.bitcast(tab_ref[pl.ds(off, SUBLANES), :], BF16) * xb)
            out.append(pltpu.bitcast(_sublane_fold8(parts), U32))
        return out

    def lane_sums(folds, p0):
        mat = jnp.zeros((SUBLANES, LANES), F32)
        for i, f in enumerate(folds):
            lo, hi = _unpack(f)
            mat = jnp.where(lane_grp == p0 // SUBLANES + i, jnp.sum(lo + hi, axis=-1, keepdims=True), mat)
        return jnp.sum(jnp.where(diag, mat, 0.0), axis=0, keepdims=True)

    def finish(t, folds):
        o_ref[pl.ds(t, 1), :] = lane_sums(folds, main[0]) + extra_ref[pl.ds(t, 1), :]

    def token(t, prev):
        cur = folded(t, *main)
        finish(jnp.maximum(t - 1, 0), prev)
        extra_ref[pl.ds(t, 1), :] = jnp.zeros((1, LANES), F32)

        @pl.when(_overflow(nlow_ref[0, t], half))
        def _():
            extra_ref[pl.ds(t, 1), :] = lane_sums(folded(t, *rest), rest[0])

        return tuple(cur)

    extra_ref[pl.ds(0, 1), :] = jnp.zeros((1, LANES), F32)
    zeros = tuple(jnp.zeros((SUBLANES, LANES), U32) for _ in range((main[1] - main[0]) // SUBLANES))
    last = lax.fori_loop(0, tm, token, zeros)
    finish(tm - 1, last)


def _peer_u(off, nlow3, tab, x4, half, tm=128):
    t = off.shape[0]
    rows = PEER_HALF * SUBLANES
    return pl.pallas_call(
        functools.partial(_peer_u_kernel, tm=tm, half=half),
        grid=(t // tm,),
        in_specs=[pl.BlockSpec((tm, LANES), lambda i: (i, 0), memory_space=pltpu.SMEM),
                  pl.BlockSpec((None, 1, tm), lambda i: (i, 0, 0), memory_space=pltpu.SMEM),
                  pl.BlockSpec((rows, LANES), lambda i: (half, 0), pipeline_mode=pl.Buffered(1)),
                  pl.BlockSpec((tm, SUBLANES, LANES), lambda i: (i, 0, 0))],
        out_specs=pl.BlockSpec((tm, LANES), lambda i: (i, 0)),
        out_shape=jax.ShapeDtypeStruct((t, LANES), F32),
        scratch_shapes=[pltpu.VMEM((tm, LANES), F32)],
        compiler_params=_cparams(("arbitrary",)),
        name="peer_expert_scores",
    )(off, nlow3, tab, x4)


def _peer_coef_kernel(a0_ref, a1_ref, eid_ref, slot_ref, gate_ref, c0_ref, c1_ref):
    low = eid_ref[...] < PEER_HALF
    a = jnp.where(low, a0_ref[...], a1_ref[...])
    gate = jnp.take_along_axis(gate_ref[...], slot_ref[...], axis=1)
    coef = gate * (0.5 * a * (1.0 + lax.erf(a * (1.0 / math.sqrt(2.0)))))
    c0_ref[...] = jnp.where(low, coef, 0.0)
    c1_ref[...] = jnp.where(low, 0.0, coef)


def _peer_coef(a0, a1, eid, slot, gate, tm=1024):
    t = eid.shape[0]
    tm = min(tm, t)
    spec = pl.BlockSpec((tm, LANES), lambda i: (i, 0))
    return pl.pallas_call(
        _peer_coef_kernel,
        grid=(t // tm,),
        in_specs=[spec, spec, spec, spec, spec],
        out_specs=[spec, spec],
        out_shape=[jax.ShapeDtypeStruct((t, LANES), F32)] * 2,
        compiler_params=_cparams(("parallel",)),
        name="peer_coef",
    )(a0, a1, eid, slot, gate)


def _peer_v_kernel(off_ref, nlow_ref, coef_ref, tab_ref, base_ref, *o_refs, tm, half, n_first):
    n_acc = 4
    main, rest = _pair_ranges(half)

    def weighted(t, p0, p1):
        acc_lo = [jnp.zeros((SUBLANES, LANES), F32) for _ in range(n_acc)]
        acc_hi = [jnp.zeros((SUBLANES, LANES), F32) for _ in range(n_acc)]
        for p in range(p0, p1):
            c = coef_ref[t, p]
            lo, hi = _table_rows(tab_ref, off_ref[t, p])
            acc_lo[p % n_acc] = acc_lo[p % n_acc] + c * lo
            acc_hi[p % n_acc] = acc_hi[p % n_acc] + c * hi
        return ((acc_lo[0] + acc_lo[1]) + (acc_lo[2] + acc_lo[3]),
                (acc_hi[0] + acc_hi[1]) + (acc_hi[2] + acc_hi[3]))

    def run(o_ref):
        def store_row(t, lo, hi):
            o_ref[t, 0] = base_ref[t, 0] + lo
            o_ref[t, 1] = base_ref[t, 1] + hi

        def token(t, carry):
            lo, hi = weighted(t, *main)
            store_row(t, lo, hi)

            @pl.when(_overflow(nlow_ref[0, t], half))
            def _():
                lo2, hi2 = weighted(t, *rest)
                store_row(t, lo + lo2, hi + hi2)

            return carry

        lax.fori_loop(0, tm, token, 0)

    if len(o_refs) == 1:
        run(o_refs[0])
    else:
        first = pl.program_id(0) < n_first
        pl.when(first)(lambda: run(o_refs[0]))
        pl.when(jnp.logical_not(first))(lambda: run(o_refs[1]))


def _peer_v(off, nlow3, coef, tab, base, half, tm=128, split=None):
    t = base.shape[0]
    rows = PEER_HALF * SUBLANES
    smem = pl.BlockSpec((tm, LANES), lambda i: (i, 0), memory_space=pltpu.SMEM)
    tile = pl.BlockSpec((tm, 2, SUBLANES, LANES), lambda i: (i, 0, 0, 0))
    if split is None:
        n_first, out_specs = 0, tile
        out_shape = jax.ShapeDtypeStruct((t, 2, SUBLANES, LANES), F32)
    else:
        assert split % tm == 0
        n_first, n_rest = split // tm, (t - split) // tm
        blk = (tm, 2, SUBLANES, LANES)
        out_specs = [pl.BlockSpec(blk, lambda i: (jnp.clip(i, 0, n_first - 1), 0, 0, 0)),
                     pl.BlockSpec(blk, lambda i: (jnp.clip(i - n_first, 0, n_rest - 1), 0, 0, 0))]
        out_shape = [jax.ShapeDtypeStruct((split, 2, SUBLANES, LANES), F32),
                     jax.ShapeDtypeStruct((t - split, 2, SUBLANES, LANES), F32)]
    return pl.pallas_call(
        functools.partial(_peer_v_kernel, tm=tm, half=half, n_first=n_first),
        grid=(t // tm,),
        in_specs=[smem,
                  pl.BlockSpec((None, 1, tm), lambda i: (i, 0, 0), memory_space=pltpu.SMEM),
                  smem,
                  pl.BlockSpec((rows, LANES), lambda i: (half, 0), pipeline_mode=pl.Buffered(1)),
                  tile],
        out_specs=out_specs,
        out_shape=out_shape,
        compiler_params=_cparams(("arbitrary",)),
        name="peer_expert_sum",
    )(off, nlow3, coef, tab, base)


def _peer(x1, split, norm2_g, wq_bf, keys_bf, u_packed, v_packed, tm=128):
    t, d = x1.shape
    tm = min(tm, t)
    (xn_bf,) = _rmsnorm(x1, norm2_g, (BF16,))
    q = _matmul(xn_bf, wq_bf, wq_bf.shape[1], 0, BF16)
    eid_t, gate_t = _peer_topk(q, keys_bf)
    eid, gate = eid_t.T, gate_t.T
    n_exp = PEER_NKEYS * PEER_NKEYS
    high = (eid >= PEER_HALF).astype(I32)
    nlow3 = (PEER_PAIRS - jnp.sum(high, axis=-1, dtype=I32)).reshape(t // tm, 1, tm)
    slot = lax.broadcasted_iota(I32, eid.shape, 1)
    order = jnp.sort((high * PEER_PAIRS + slot) * n_exp + eid, axis=-1)
    eid = order % n_exp
    slot = (order // n_exp) % PEER_PAIRS
    off = (eid & (PEER_HALF - 1)) * SUBLANES
    xw = _pack_table(xn_bf).reshape(t, SUBLANES, LANES)
    a0 = _peer_u(off, nlow3, u_packed, xw, 0, tm)
    a1 = _peer_u(off, nlow3, u_packed, xw, 1, tm)
    c0, c1 = _peer_coef(a0, a1, eid, slot, gate)
    y = _peer_v(off, nlow3, c0, v_packed, x1.reshape(t, 2, SUBLANES, LANES), 0, tm)
    ya, yb = _peer_v(off, nlow3, c1, v_packed, y, 1, tm, split=split)
    return ya.reshape(split, d), yb.reshape(t - split, d)


def kernel(x_prompt, x_sample, norm1_g, w_in, q_norm_g, k_norm_g, lambda_q1, lambda_k1, lambda_q2, lambda_k2, attn_sub_g, conv_w, conv_b, filt_w1, filt_b1, filt_w2, filt_b2, filt_w3, filt_b3, filt_w4, filt_freq, fft_bias, hyena_out_g, w_out, norm2_g, peer_wq, peer_keys, peer_u, peer_v):
    depth = w_in.shape[0]
    d_model = x_prompt.shape[-1]
    att_w = ATT_HEADS * ATT_VDIM
    shapes = [x_prompt.shape[:2], x_sample.shape[:2]]
    xs = [x_prompt.reshape(-1, d_model), x_sample.reshape(-1, d_model)]
    n0 = xs[0].shape[0]
    slopes = 2.0 ** (-8.0 * jnp.arange(1, ATT_HEADS + 1, dtype=F32) / ATT_HEADS)

    for l in range(depth):
        lambda_init = 0.8 - 0.6 * math.exp(-0.3 * l)
        lam = (jnp.exp(jnp.sum(lambda_q1[l].astype(F32) * lambda_k1[l].astype(F32)))
               - jnp.exp(jnp.sum(lambda_q2[l].astype(F32) * lambda_k2[l].astype(F32)))
               + lambda_init).reshape(1)
        w_in_bf = w_in[l].astype(BF16)
        q_gain = jnp.tile(q_norm_g[l].astype(F32), 2 * ATT_HEADS) * (ATT_QKDIM ** -0.5 * LOG2E)
        k_gain = jnp.tile(k_norm_g[l].astype(F32), 2 * ATT_HEADS)
        qk_gain = jnp.concatenate([q_gain, k_gain]).reshape(1, 2 * att_w)

        h_bf = _rmsnorm_stacked(xs[0], xs[1], norm1_g[l], BF16)
        qk = _matmul(h_bf, w_in_bf, 2 * att_w, 0, BF16, mode="qknorm", extra=qk_gain)
        vt = _matmul_nt(w_in_bf[:, 2 * att_w:3 * att_w].T, h_bf, BF16)
        zh = _matmul(h_bf, w_in_bf, w_in.shape[2] - 3 * att_w, 3 * att_w, F32)

        filt = (filt_w1[l], filt_b1[l], filt_w2[l], filt_b2[l], filt_w3[l], filt_b3[l], filt_w4[l], filt_freq[l])
        segs, row = [], 0
        for (b, s), x_seg in zip(shapes, xs):
            att = _attention(qk, vt, row, slopes, lam, attn_sub_g[l], b, s, 1.0 - lambda_init)
            hy = _hyena(zh, row, b, s, conv_w[l], conv_b[l], filt, fft_bias[l], hyena_out_g[l])
            segs.append((att, hy, x_seg))
            row += b * s
        x1 = _outproj_stacked(segs[0], segs[1], w_out[l].astype(BF16))

        xs = _peer(x1, n0, norm2_g[l], peer_wq[l].astype(BF16), peer_keys[l].astype(BF16),
                   _pack_table(peer_u[l]), _pack_table(peer_v[l]))

    return (xs[0].reshape(x_prompt.shape), xs[1].reshape(x_sample.shape))
```

```python
import functools
import math

import jax
import jax.numpy as jnp
from jax import lax
from jax.experimental import pallas as pl
from jax.experimental.pallas import tpu as pltpu

F32 = jnp.float32
BF16 = jnp.bfloat16
I32 = jnp.int32
U32 = jnp.uint32

RMS_EPS = 1e-6
LOG2E = 1.4426950408889634
LANES = 128
SUBLANES = 8
VMEM_LIMIT_BYTES = 56 * 1024 * 1024

ATT_HEADS = 8
ATT_VDIM = 128
ATT_QKDIM = 64
HYENA_ORDER = 2
FILTER_BANDS = 16
DECAY_FAST = 0.3
DECAY_SLOW = 1.5
DECAY_TARGET = 1e-2
PEER_HEADS = 8
PEER_NKEYS = 128
PEER_TOPK = 16
PEER_HALF = PEER_NKEYS * PEER_NKEYS // 2
PEER_PAIRS = PEER_HEADS * PEER_TOPK
PEER_WINDOW = 80


def _cparams(sem, vmem=VMEM_LIMIT_BYTES):
    return pltpu.CompilerParams(dimension_semantics=sem, vmem_limit_bytes=vmem)


def _segment_spec(block, start, count, minor=0):
    return pl.BlockSpec(block, lambda i: (jnp.clip(i - start, 0, count - 1), minor))


def _rmsnorm_kernel(x_ref, g_ref, *o_refs):
    x = x_ref[...]
    ms = jnp.mean(x * x, axis=-1, keepdims=True)
    y = x * lax.rsqrt(ms + RMS_EPS) * g_ref[...]
    for o_ref in o_refs:
        o_ref[...] = y.astype(o_ref.dtype)


def _rmsnorm2_kernel(xa_ref, xb_ref, g_ref, o_ref, *, na):
    first = pl.program_id(0) < na
    x = jnp.where(first, xa_ref[...], xb_ref[...])
    ms = jnp.mean(x * x, axis=-1, keepdims=True)
    o_ref[...] = (x * lax.rsqrt(ms + RMS_EPS) * g_ref[...]).astype(o_ref.dtype)


def _rmsnorm_stacked(xa, xb, g, out_dtype, tm=512):
    d = xa.shape[1]
    na, nb = xa.shape[0] // tm, xb.shape[0] // tm
    assert xa.shape[0] % tm == 0 and xb.shape[0] % tm == 0
    return pl.pallas_call(
        functools.partial(_rmsnorm2_kernel, na=na),
        grid=(na + nb,),
        in_specs=[_segment_spec((tm, d), 0, na), _segment_spec((tm, d), na, nb),
                  pl.BlockSpec((1, d), lambda i: (0, 0))],
        out_specs=pl.BlockSpec((tm, d), lambda i: (i, 0)),
        out_shape=jax.ShapeDtypeStruct((xa.shape[0] + xb.shape[0], d), out_dtype),
        compiler_params=_cparams(("parallel",)),
        name="rmsnorm_stacked",
    )(xa, xb, g.reshape(1, d).astype(F32))


def _rmsnorm(x, g, out_dtypes, tm=512):
    t, d = x.shape
    tm = min(tm, t)
    spec = pl.BlockSpec((tm, d), lambda i: (i, 0))
    return pl.pallas_call(
        _rmsnorm_kernel,
        grid=(t // tm,),
        in_specs=[spec, pl.BlockSpec((1, d), lambda i: (0, 0))],
        out_specs=[spec for _ in out_dtypes],
        out_shape=[jax.ShapeDtypeStruct((t, d), dt) for dt in out_dtypes],
        compiler_params=_cparams(("parallel",)),
        name="rmsnorm",
    )(x, g.reshape(1, d).astype(F32))


def _group_rms_scale(x, gain):
    lane = lax.broadcasted_iota(I32, x.shape, 1)
    lo = lane < ATT_QKDIM
    x2 = x * x
    s_lo = jnp.sum(jnp.where(lo, x2, 0.0), axis=-1, keepdims=True)
    s_hi = jnp.sum(jnp.where(lo, 0.0, x2), axis=-1, keepdims=True)
    ms = jnp.where(lo, s_lo, s_hi) * (1.0 / ATT_QKDIM)
    return x * lax.rsqrt(ms + RMS_EPS) * gain


def _mm_kernel(a_ref, b_ref, *rest, mode):
    acc = jnp.dot(a_ref[...], b_ref[...], preferred_element_type=F32)
    if mode == "plain":
        (o_ref,) = rest
        o_ref[...] = acc.astype(o_ref.dtype)
    elif mode == "qknorm":
        g_ref, o_ref = rest
        for c in range(acc.shape[1] // LANES):
            sl = slice(c * LANES, (c + 1) * LANES)
            o_ref[:, sl] = _group_rms_scale(acc[:, sl], g_ref[:, sl]).astype(o_ref.dtype)
    else:
        raise ValueError(mode)


def _matmul(a, b, n_cols, col_off, out_dtype, mode="plain", extra=None, tm=1024, tn=1024):
    m, k = a.shape
    tm = min(tm, m)
    tn = min(tn, n_cols)
    assert col_off % tn == 0 and n_cols % tn == 0 and m % tm == 0
    off = col_off // tn
    in_specs = [pl.BlockSpec((tm, k), lambda i, j: (i, 0)),
                pl.BlockSpec((k, tn), lambda i, j: (0, j + off))]
    args = [a, b]
    if mode == "qknorm":
        in_specs.append(pl.BlockSpec((1, tn), lambda i, j: (0, j)))
        args.append(extra)
    return pl.pallas_call(
        functools.partial(_mm_kernel, mode=mode),
        grid=(m // tm, n_cols // tn),
        in_specs=in_specs,
        out_specs=pl.BlockSpec((tm, tn), lambda i, j: (i, j)),
        out_shape=jax.ShapeDtypeStruct((m, n_cols), out_dtype),
        compiler_params=_cparams(("parallel", "arbitrary")),
        name="matmul_" + mode,
    )(*args)


def _outproj_kernel(att_a, hy_a, x_a, att_b, hy_b, x_b, w_ref, o_ref, *, na):
    kw = att_a.shape[1]

    def run(att_ref, hy_ref, x_ref):
        acc = jnp.dot(att_ref[...], w_ref[:kw, :], preferred_element_type=F32)
        acc = acc + jnp.dot(hy_ref[...].astype(BF16), w_ref[kw:, :], preferred_element_type=F32)
        o_ref[...] = acc + x_ref[...]

    first = pl.program_id(0) < na
    pl.when(first)(lambda: run(att_a, hy_a, x_a))
    pl.when(jnp.logical_not(first))(lambda: run(att_b, hy_b, x_b))


def _outproj_stacked(seg_a, seg_b, w, tm=512, tn=1024):
    n_out = w.shape[1]
    na, nb = seg_a[0].shape[0] // tm, seg_b[0].shape[0] // tm
    assert seg_a[0].shape[0] % tm == 0 and seg_b[0].shape[0] % tm == 0 and n_out % tn == 0

    def specs(seg, start, count):
        att, hy, x = seg
        row = lambda i, j: (jnp.clip(i - start, 0, count - 1), 0)
        return [pl.BlockSpec((tm, att.shape[1]), row), pl.BlockSpec((tm, hy.shape[1]), row),
                pl.BlockSpec((tm, tn), lambda i, j: (jnp.clip(i - start, 0, count - 1), j))]

    return pl.pallas_call(
        functools.partial(_outproj_kernel, na=na),
        grid=(na + nb, n_out // tn),
        in_specs=specs(seg_a, 0, na) + specs(seg_b, na, nb) + [pl.BlockSpec((w.shape[0], tn), lambda i, j: (0, j))],
        out_specs=pl.BlockSpec((tm, tn), lambda i, j: (i, j)),
        out_shape=jax.ShapeDtypeStruct(((na + nb) * tm, n_out), F32),
        compiler_params=_cparams(("parallel", "arbitrary")),
        name="out_projection",
    )(*seg_a, *seg_b, w)


def _mm_nt_kernel(w_ref, a_ref, o_ref):
    o_ref[...] = lax.dot_general(w_ref[...], a_ref[...], (((1,), (1,)), ((), ())),
                                 preferred_element_type=F32).astype(o_ref.dtype)


def _matmul_nt(wt, a, out_dtype, tm=1024):
    n, k = wt.shape
    m = a.shape[0]
    tm = min(tm, m)
    return pl.pallas_call(
        _mm_nt_kernel,
        grid=(m // tm,),
        in_specs=[pl.BlockSpec((n, k), lambda i: (0, 0)), pl.BlockSpec((tm, k), lambda i: (i, 0))],
        out_specs=pl.BlockSpec((n, tm), lambda i: (0, i)),
        out_shape=jax.ShapeDtypeStruct((n, m), out_dtype),
        compiler_params=_cparams(("parallel",)),
        name="matmul_nt",
    )(wt, a)


def _alibi_columns(slopes, tq, tk):
    return [_alibi_side(slopes, tq, True), _alibi_side(slopes, tk, False)]


def _alibi_side(slopes, n, query_side):
    pos = jnp.arange(n, dtype=F32)
    val = (slopes.astype(F32) * LOG2E)[:, None] * pos[None, :]

    def pieces(x):
        p1 = x.astype(BF16)
        r1 = x - p1.astype(F32)
        p2 = r1.astype(BF16)
        p3 = (r1 - p2.astype(F32)).astype(BF16)
        return [p1, p2, p3]

    ones = [jnp.ones_like(val, BF16)] * 3
    six = jnp.stack(pieces(-val) + ones if query_side else ones + pieces(val), axis=-1)
    pad = jnp.zeros(val.shape + (ATT_QKDIM - 6,), BF16)
    return jnp.concatenate([six, pad, six, pad], axis=-1)


def _attn_kernel(slope_ref, lam_ref, q_ref, k_ref, vt_ref, aq_ref, ak_ref, g_ref, o_ref,
                 m_ref, l_ref, acc_ref, *, tq, tk, nk, hb, out_scale):
    hg = pl.program_id(1)
    i = pl.program_id(2)
    j = pl.program_id(3)

    @pl.when(j == 0)
    def _():
        m_ref[...] = jnp.full(m_ref.shape, -jnp.inf, F32)
        l_ref[...] = jnp.zeros(l_ref.shape, F32)
        acc_ref[...] = jnp.zeros(acc_ref.shape, F32)

    q_first = lax.broadcasted_iota(I32, (tq, LANES), 1) < ATT_QKDIM
    k_first = lax.broadcasted_iota(I32, (tk, LANES), 1) < ATT_QKDIM
    nt = (((1,), (1,)), ((), ()))

    def update(hh, scores, shift):
        vt = vt_ref[hh * LANES:(hh + 1) * LANES, :]
        for c, s in enumerate(scores):
            m_old = m_ref[hh, c]
            m_new = jnp.maximum(m_old, jnp.max(s, axis=0, keepdims=True) + shift)
            alpha = jnp.exp2(m_old - m_new)
            p = jnp.exp2(s - (m_new - shift))
            l_ref[hh, c] = alpha * l_ref[hh, c] + jnp.sum(p, axis=0, keepdims=True)
            acc_ref[hh, c] = alpha * acc_ref[hh, c] + jnp.dot(vt, p.astype(BF16),
                                                              preferred_element_type=F32)
            m_ref[hh, c] = m_new

    keys_before = i * tq >= (j + 1) * tk
    keys_after = (i + 1) * tq <= j * tk
    off_diagonal = jnp.logical_or(keys_before, keys_after)

    @pl.when(off_diagonal)
    def _():
        sign = jnp.where(keys_after, -1.0, 1.0).astype(BF16)
        gap = jnp.abs(i * tq - j * tk).astype(F32)
        for hh in range(hb):
            sl = slice(hh * LANES, (hh + 1) * LANES)
            q = q_ref[:, sl]
            k = k_ref[:, sl]
            aq = aq_ref[hh]
            ak = ak_ref[hh] * sign
            s0 = lax.dot_general(jnp.where(k_first, k, ak), jnp.where(q_first, q, aq), nt,
                                 preferred_element_type=F32)
            s1 = lax.dot_general(jnp.where(k_first, ak, k), jnp.where(q_first, aq, q), nt,
                                 preferred_element_type=F32)
            update(hh, (s0, s1), -(slope_ref[hg * hb + hh] * LOG2E) * gap)

    @pl.when(jnp.logical_not(off_diagonal))
    def _():
        kpos = lax.broadcasted_iota(I32, (tk, tq), 0) + j * tk
        qpos = lax.broadcasted_iota(I32, (tk, tq), 1) + i * tq
        dist = jnp.abs(kpos - qpos).astype(F32)
        for hh in range(hb):
            sl = slice(hh * LANES, (hh + 1) * LANES)
            q = q_ref[:, sl]
            k = k_ref[:, sl]
            zero = jnp.zeros_like(q)
            bias = dist * (-(slope_ref[hg * hb + hh] * LOG2E))
            s0 = lax.dot_general(k, jnp.where(q_first, q, zero), nt, preferred_element_type=F32) + bias
            s1 = lax.dot_general(k, jnp.where(q_first, zero, q), nt, preferred_element_type=F32) + bias
            update(hh, (s0, s1), 0.0)

    @pl.when(j == nk - 1)
    def _():
        for hh in range(hb):
            o = acc_ref[hh, 0] / l_ref[hh, 0] - lam_ref[0] * (acc_ref[hh, 1] / l_ref[hh, 1])
            ms = jnp.mean(o * o, axis=0, keepdims=True)
            y = o * lax.rsqrt(ms + RMS_EPS) * (g_ref[...] * out_scale)
            o_ref[:, hh * LANES:(hh + 1) * LANES] = y.T.astype(o_ref.dtype)


def _attention(qk, vt, row_off, slopes, lam, sub_g, batch, seq, out_scale, tq=512, tk=1024, hb=4):
    tq = min(tq, seq)
    tk = min(tk, seq // 4)
    nq, nk = seq // tq, seq // tk
    assert row_off % tq == 0 and row_off % tk == 0 and ATT_HEADS % hb == 0
    oq, ok = row_off // tq, row_off // tk
    ng = ATT_HEADS // hb
    aq, ak = _alibi_columns(slopes, tq, tk)
    kern = functools.partial(_attn_kernel, tq=tq, tk=tk, nk=nk, hb=hb, out_scale=out_scale)
    smem = pl.BlockSpec(memory_space=pltpu.SMEM)
    return pl.pallas_call(
        kern,
        grid=(batch, ng, nq, nk),
        in_specs=[smem, smem,
                  pl.BlockSpec((tq, hb * LANES), lambda b, h, i, j: (oq + b * nq + i, h)),
                  pl.BlockSpec((tk, hb * LANES), lambda b, h, i, j: (ok + b * nk + j, ng + h)),
                  pl.BlockSpec((hb * LANES, tk), lambda b, h, i, j: (h, ok + b * nk + j)),
                  pl.BlockSpec((hb, tq, LANES), lambda b, h, i, j: (h, 0, 0)),
                  pl.BlockSpec((hb, tk, LANES), lambda b, h, i, j: (h, 0, 0)),
                  pl.BlockSpec((LANES, 1), lambda b, h, i, j: (0, 0))],
        out_specs=pl.BlockSpec((tq, hb * LANES), lambda b, h, i, j: (b * nq + i, h)),
        out_shape=jax.ShapeDtypeStruct((batch * seq, ATT_HEADS * ATT_VDIM), BF16),
        scratch_shapes=[pltpu.VMEM((hb, 2, 1, tq), F32), pltpu.VMEM((hb, 2, 1, tq), F32),
                        pltpu.VMEM((hb, 2, LANES, tq), F32)],
        compiler_params=_cparams(("parallel", "parallel", "parallel", "arbitrary")),
        name="diff_attention",
    )(slopes, lam, qk, qk, vt, aq, ak, sub_g.reshape(LANES, 1).astype(F32))


def _shortconv_kernel(z_ref, w_ref, b_ref, o_ref):
    z = z_ref[...]
    n = z.shape[0]
    row = lax.broadcasted_iota(I32, z.shape, 0)
    prev = jnp.where(row == 0, 0.0, pltpu.roll(z, 1, axis=0))
    nxt = jnp.where(row == n - 1, 0.0, pltpu.roll(z, n - 1, axis=0))
    w = w_ref[...]
    o_ref[...] = prev * w[0:1] + z * w[1:2] + nxt * w[2:3] + b_ref[...]


def _shortconv(zh, row_off, conv_w, conv_b, batch, seq, cb=256):
    c3 = zh.shape[1]
    c = c3 // 3
    ncb = c // cb
    assert row_off % seq == 0
    ob = row_off // seq
    out = pl.pallas_call(
        _shortconv_kernel,
        grid=(batch, 3, ncb),
        in_specs=[pl.BlockSpec((seq, cb), lambda b, p, j: (ob + b, p * ncb + j)),
                  pl.BlockSpec((3, cb), lambda b, p, j: (0, p * ncb + j)),
                  pl.BlockSpec((1, cb), lambda b, p, j: (0, p * ncb + j))],
        out_specs=pl.BlockSpec((None, None, seq, cb), lambda b, p, j: (p, b, 0, j)),
        out_shape=jax.ShapeDtypeStruct((3, batch, seq, c), F32),
        compiler_params=_cparams(("parallel", "parallel", "parallel")),
        name="hyena_shortconv",
    )(zh, conv_w.astype(F32), conv_b.reshape(1, c3).astype(F32))
    return out


def _filter_kernel(z_ref, w1_ref, b1_ref, w2_ref, b2_ref, w3_ref, b3_ref, fr_ref, w4_ref,
                   t_ref, d_ref, o_ref, h_ref, *, tl):
    i = pl.program_id(0)
    g = pl.program_id(1)
    hi = lax.Precision.HIGHEST

    @pl.when(g == 0)
    def _():
        fr = fr_ref[...]
        h = jnp.sin(fr * (jnp.dot(z_ref[...], w1_ref[...], precision=hi, preferred_element_type=F32) + b1_ref[...]))
        h = jnp.sin(fr * (jnp.dot(h, w2_ref[...], precision=hi, preferred_element_type=F32) + b2_ref[...]))
        h_ref[...] = jnp.sin(fr * (jnp.dot(h, w3_ref[...], precision=hi, preferred_element_type=F32) + b3_ref[...]))

    f = jnp.dot(h_ref[...].astype(BF16), w4_ref[...], preferred_element_type=F32)
    f = f * jnp.exp(-t_ref[...] * d_ref[...])
    row = lax.broadcasted_iota(I32, f.shape, 0) + i * tl
    drop = jnp.logical_and(row == 0, g % 2 == 1)
    o_ref[...] = jnp.where(drop, 0.0, f)


def _hyena_filter_signals(seq, w1, b1, w2, b2, w3, b3, w4, freq, n_ch, tl=512):
    t = jnp.linspace(0.0, 1.0, seq, dtype=F32)[:, None]
    w = 2.0 * math.pi * jnp.arange(seq, dtype=F32)[:, None] / seq
    f = jnp.linspace(1e-4, FILTER_BANDS - 1, FILTER_BANDS, dtype=F32)[None, :]
    z = jnp.concatenate([t, jnp.cos(f * w), -jnp.sin(f * w)], axis=-1)
    deltas = jnp.abs(jnp.linspace(math.log(DECAY_FAST) / DECAY_TARGET,
                                  math.log(DECAY_SLOW) / DECAY_TARGET, n_ch, dtype=F32))[None, :]
    hid = w1.shape[1]
    emb = LANES
    z = jnp.pad(z, ((0, 0), (0, emb - z.shape[1])))
    w1 = jnp.pad(w1.astype(F32), ((0, emb - w1.shape[0]), (0, 0)))
    tl = min(tl, seq)
    full = lambda shape: pl.BlockSpec(shape, lambda i, g: tuple(0 for _ in shape))
    return pl.pallas_call(
        functools.partial(_filter_kernel, tl=tl),
        grid=(seq // tl, 2 * HYENA_ORDER),
        in_specs=[pl.BlockSpec((tl, emb), lambda i, g: (i, 0)),
                  full((emb, hid)), full((1, hid)), full((hid, hid)), full((1, hid)),
                  full((hid, hid)), full((1, hid)), full((1, hid)),
                  pl.BlockSpec((hid, n_ch), lambda i, g: (0, g)),
                  pl.BlockSpec((tl, 1), lambda i, g: (i, 0)),
                  full((1, n_ch))],
        out_specs=pl.BlockSpec((None, tl, n_ch), lambda i, g: (g, i, 0)),
        out_shape=jax.ShapeDtypeStruct((2 * HYENA_ORDER, seq, n_ch), F32),
        scratch_shapes=[pltpu.VMEM((tl, hid), F32)],
        compiler_params=_cparams(("parallel", "arbitrary")),
        name="hyena_filter_mlp",
    )(z, w1.astype(F32), b1.reshape(1, hid).astype(F32), w2.astype(F32), b2.reshape(1, hid).astype(F32),
      w3.astype(F32), b3.reshape(1, hid).astype(F32), freq.reshape(1, hid).astype(F32), w4.astype(BF16),
      t, deltas)


def _dft_tables(r):
    n = r * r
    k2 = jnp.arange(r, dtype=I32)
    n2 = jnp.arange(r // 2, dtype=I32)
    ang1 = (2.0 * math.pi / r) * ((k2[:, None] * n2[None, :]) % r).astype(F32)
    f1 = jnp.concatenate([jnp.cos(ang1), -jnp.sin(ang1)], axis=0)
    k1 = jnp.arange(r, dtype=I32)
    n1 = jnp.arange(r, dtype=I32)
    ang_a = (2.0 * math.pi / r) * ((k1[:, None] * n1[None, :]) % r).astype(F32)
    ang_b = (2.0 * math.pi / n) * (k2[:, None] * n1[None, :]).astype(F32)
    ca, sa = jnp.cos(ang_a)[None], jnp.sin(ang_a)[None]
    cb, sb = jnp.cos(ang_b)[:, None], jnp.sin(ang_b)[:, None]
    mr, mi_ = ca * cb - sa * sb, -(sa * cb + ca * sb)
    mf = jnp.concatenate([jnp.concatenate([mr, -mi_], axis=2),
                          jnp.concatenate([mi_, mr], axis=2)], axis=1)
    minv = jnp.swapaxes(mf, 1, 2)
    g3 = jnp.concatenate([jnp.cos(ang1.T), -jnp.sin(ang1.T)], axis=1) * (1.0 / n)
    eye = jnp.eye(N1_BLOCK, dtype=F32)
    f1, g3 = jnp.kron(f1, eye), jnp.kron(g3, eye)
    return f1.astype(BF16), mf.astype(BF16), minv.astype(BF16), g3.astype(BF16)


N1_BLOCK = SUBLANES


def _pack_complex(re, im):
    rb = lax.bitcast_convert_type(re.astype(BF16).astype(F32), U32)
    ib = lax.bitcast_convert_type(im.astype(BF16).astype(F32), U32)
    return (rb >> 16) | ib


def _unpack_complex_rows(w):
    re, im = _unpack(w)
    return jnp.concatenate([re, im], axis=0).astype(BF16)


def _fft1_kernel(f_ref, x_ref, o_ref, *, r):
    n_ch = x_ref.shape[-1]
    x = x_ref[...].reshape((r // 2) * N1_BLOCK, n_ch).astype(BF16)
    res = jnp.dot(f_ref[...], x, preferred_element_type=F32)
    half = r * N1_BLOCK
    o_ref[...] = _pack_complex(res[:half], res[half:]).reshape(r, N1_BLOCK, n_ch)


def _fft_stage1(x4, part, f1, r, n_ch):
    nb = x4.shape[1]
    xv = x4.reshape(x4.shape[0], nb, r // 2, r, n_ch)
    return pl.pallas_call(
        functools.partial(_fft1_kernel, r=r),
        grid=(nb, r // N1_BLOCK),
        in_specs=[pl.BlockSpec(f1.shape, lambda b, j: (0, 0)),
                  pl.BlockSpec((None, None, r // 2, N1_BLOCK, n_ch), lambda b, j: (part, b, 0, j, 0))],
        out_specs=pl.BlockSpec((None, r, N1_BLOCK, n_ch), lambda b, j: (b, 0, j, 0)),
        out_shape=jax.ShapeDtypeStruct((nb, r, r, n_ch), U32),
        compiler_params=_cparams(("parallel", "parallel")),
        name="hyena_dft_stage1",
    )(f1, xv)


def _k2_block(r):
    return max(1, min(r, (8 * LANES) // r))


def _filter_spec_kernel(mf_ref, bf_ref, bb_ref, o_ref, *, r):
    for kk in range(mf_ref.shape[0]):
        m = mf_ref[kk]
        xf = jnp.dot(m, _unpack_complex_rows(bf_ref[kk]), preferred_element_type=F32)
        xb = jnp.dot(m, _unpack_complex_rows(bb_ref[kk]), preferred_element_type=F32)
        o_ref[kk] = _pack_complex(xf[:r] + xb[:r], xf[r:] - xb[r:])


def _filter_spectrum(b1, mf, r, n_ch):
    kb = _k2_block(r)
    blk = lambda sel: pl.BlockSpec((None, kb, r, n_ch), lambda o, k: (2 * o + sel, k, 0, 0))
    return pl.pallas_call(
        functools.partial(_filter_spec_kernel, r=r),
        grid=(HYENA_ORDER, r // kb),
        in_specs=[pl.BlockSpec((kb, 2 * r, 2 * r), lambda o, k: (k, 0, 0)), blk(0), blk(1)],
        out_specs=pl.BlockSpec((None, kb, r, n_ch), lambda o, k: (o, k, 0, 0)),
        out_shape=jax.ShapeDtypeStruct((HYENA_ORDER, r, r, n_ch), U32),
        compiler_params=_cparams(("parallel", "parallel")),
        name="hyena_filter_spectrum",
    )(mf, b1, b1)


def _fft2_kernel(mf_ref, mi_ref, b_ref, h_ref, o_ref, *, r):
    for kk in range(mf_ref.shape[0]):
        x = jnp.dot(mf_ref[kk], _unpack_complex_rows(b_ref[kk]), preferred_element_type=F32)
        xr, xi = x[:r], x[r:]
        hr, hi = _unpack(h_ref[kk])
        y = jnp.concatenate([xr * hr - xi * hi, xr * hi + xi * hr], axis=0).astype(BF16)
        c = jnp.dot(mi_ref[kk], y, preferred_element_type=F32)
        o_ref[kk] = _pack_complex(c[:r], c[r:])


def _fft_stage2(b1, h, order, mf, minv, r, n_ch):
    nb = b1.shape[0]
    kb = _k2_block(r)
    blk = pl.BlockSpec((None, kb, r, n_ch), lambda k, b: (b, k, 0, 0))
    return pl.pallas_call(
        functools.partial(_fft2_kernel, r=r),
        grid=(r // kb, nb),
        in_specs=[pl.BlockSpec((kb, 2 * r, 2 * r), lambda k, b: (k, 0, 0)),
                  pl.BlockSpec((kb, 2 * r, 2 * r), lambda k, b: (k, 0, 0)),
                  blk,
                  pl.BlockSpec((None, kb, r, n_ch), lambda k, b: (order, k, 0, 0))],
        out_specs=blk,
        out_shape=jax.ShapeDtypeStruct((nb, r, r, n_ch), U32),
        compiler_params=_cparams(("parallel", "parallel")),
        name="hyena_dft_stage2",
    )(mf, minv, b1, h)


def _fft3_kernel(g_ref, c_ref, gate_ref, s_ref, bias_ref, ng_ref, *rest, final):
    if final:
        (o_ref,) = rest
    else:
        f_ref, o_ref, b_ref = rest
    r, nb, n_ch = c_ref.shape
    cc = _unpack_complex_rows(c_ref[...].reshape(r * nb, n_ch))
    y = jnp.dot(g_ref[...], cc, preferred_element_type=F32)
    rows = (r // 2) * nb
    s_new = gate_ref[...].reshape(rows, n_ch) * (y + s_ref[...].reshape(rows, n_ch) * bias_ref[...])
    if final:
        ms = jnp.mean(s_new * s_new, axis=-1, keepdims=True)
        s_new = s_new * lax.rsqrt(ms + RMS_EPS) * ng_ref[...]
    o_ref[...] = s_new.reshape(r // 2, nb, n_ch)
    if not final:
        res = jnp.dot(f_ref[...], s_new.astype(BF16), preferred_element_type=F32)
        half = r * nb
        b_ref[...] = _pack_complex(res[:half], res[half:]).reshape(r, nb, n_ch)


def _fft_stage3(c2, g3, z4, gate_part, s4, s_part, bias, norm_g, f1, r, n_ch):
    final = f1 is None
    nb = c2.shape[0]
    zv = z4.reshape(z4.shape[0], nb, r // 2, r, n_ch)
    sv = s4.reshape(s4.shape[0], nb, r // 2, r, n_ch)
    dspec = lambda part: pl.BlockSpec((None, None, r // 2, N1_BLOCK, n_ch), lambda b, j: (part, b, 0, j, 0))
    packed = pl.BlockSpec((None, r, N1_BLOCK, n_ch), lambda b, j: (b, 0, j, 0))
    s_spec = pl.BlockSpec((None, r // 2, N1_BLOCK, n_ch), lambda b, j: (b, 0, j, 0))
    s_shape = jax.ShapeDtypeStruct((nb, r // 2, r, n_ch), F32)
    in_specs = [pl.BlockSpec(g3.shape, lambda b, j: (0, 0)), packed, dspec(gate_part), dspec(s_part),
                pl.BlockSpec((1, n_ch), lambda b, j: (0, 0)), pl.BlockSpec((1, n_ch), lambda b, j: (0, 0))]
    args = [g3, c2, zv, sv, bias.reshape(1, n_ch).astype(F32), norm_g.reshape(1, n_ch).astype(F32)]
    if final:
        out_specs, out_shape = s_spec, s_shape
    else:
        in_specs.append(pl.BlockSpec(f1.shape, lambda b, j: (0, 0)))
        args.append(f1)
        out_specs = [s_spec, packed]
        out_shape = [s_shape, jax.ShapeDtypeStruct((nb, r, r, n_ch), U32)]
    return pl.pallas_call(
        functools.partial(_fft3_kernel, final=final),
        grid=(nb, r // N1_BLOCK),
        in_specs=in_specs,
        out_specs=out_specs,
        out_shape=out_shape,
        compiler_params=_cparams(("parallel", "parallel")),
        name="hyena_dft_stage3",
    )(*args)


def _hyena(zh, row_off, batch, seq, conv_w, conv_b, filt, fft_bias, out_g):
    n_ch = zh.shape[1] // 3
    r = int(round(math.sqrt(2 * seq)))
    assert r * r == 2 * seq and r % 16 == 0
    f1, mf, minv, g3 = _dft_tables(r)
    sig = _hyena_filter_signals(seq, *filt, n_ch=n_ch)
    hb1 = _fft_stage1(sig[None], 0, f1, r, n_ch)
    h = _filter_spectrum(hb1, mf, r, n_ch)
    z4 = _shortconv(zh, row_off, conv_w, conv_b, batch, seq)
    s4, s_part = z4, 2
    b1 = _fft_stage1(s4, s_part, f1, r, n_ch)
    for o in range(HYENA_ORDER):
        c2 = _fft_stage2(b1, h, o, mf, minv, r, n_ch)
        if o < HYENA_ORDER - 1:
            s, b1 = _fft_stage3(c2, g3, z4, o, s4, s_part, fft_bias[o], out_g, f1, r, n_ch)
        else:
            s = _fft_stage3(c2, g3, z4, o, s4, s_part, fft_bias[o], out_g, None, r, n_ch)
        s4, s_part = s.reshape(1, batch, seq, n_ch), 0
    return s4.reshape(batch * seq, n_ch)


def _extract_top(s, key, count):
    vals, keys = [], []
    for _ in range(count):
        m = jnp.max(s, axis=0, keepdims=True)
        kmin = jnp.min(jnp.where(s == m, key, jnp.inf), axis=0, keepdims=True)
        s = jnp.where(key == kmin, -jnp.inf, s)
        vals.append(m)
        keys.append(kmin)
    return vals, keys


def _peer_topk_kernel(q_ref, keys_ref, eid_ref, gate_ref):
    t = q_ref.shape[0]
    nk = PEER_NKEYS
    q = q_ref[...]
    row_key = lax.broadcasted_iota(I32, (nk, t), 0).astype(F32)
    tops = []
    for c in range(2):
        s = lax.dot_general(keys_ref[c], q[:, c * nk:(c + 1) * nk], (((1,), (1,)), ((), ())),
                            preferred_element_type=F32)
        tops.append(_extract_top(s, row_key, PEER_TOPK))
    (v1, i1), (v2, i2) = tops
    rows16 = lax.broadcasted_iota(I32, (PEER_TOPK, t), 0)
    v2a = jnp.zeros((PEER_TOPK, t), F32)
    i2a = jnp.zeros((PEER_TOPK, t), F32)
    for j in range(PEER_TOPK):
        v2a = jnp.where(rows16 == j, v2[j], v2a)
        i2a = jnp.where(rows16 == j, i2[j], i2a)
    n_exp = float(nk * nk)
    half = PEER_TOPK // 2
    rows8 = lax.broadcasted_iota(I32, (half, t), 0)
    pos8 = rows8.astype(F32)
    v2h, i2h = v2a[:half], i2a[:half]
    cand = [v1[0] + v2a]
    ckey = [rows16.astype(F32) * n_exp + (i1[0] * float(nk) + i2a)]
    for i in range(1, half):
        cand.append(jnp.where(rows8 < PEER_TOPK // (i + 1), v1[i] + v2h, -jnp.inf))
        ckey.append((pos8 + float(i * PEER_TOPK)) * n_exp + (i1[i] * float(nk) + i2h))
    v1t = jnp.zeros((half, t), F32)
    i1t = jnp.zeros((half, t), F32)
    for r in range(half):
        v1t = jnp.where(rows8 == r, v1[half + r], v1t)
        i1t = jnp.where(rows8 == r, i1[half + r], i1t)
    cand.append(v1t + v2[0])
    ckey.append((pos8 + float(half)) * (PEER_TOPK * n_exp) + (i1t * float(nk) + i2[0]))
    tv, tk_ = _extract_top(jnp.concatenate(cand, axis=0), jnp.concatenate(ckey, axis=0), PEER_TOPK)
    denom = jnp.zeros((1, t), F32)
    es = []
    for k in range(PEER_TOPK):
        e = jnp.exp(tv[k] - tv[0])
        es.append(e)
        denom = denom + e
    eid = jnp.zeros((PEER_TOPK, t), F32)
    gate = jnp.zeros((PEER_TOPK, t), F32)
    for k in range(PEER_TOPK):
        pos = jnp.floor(tk_[k] * (1.0 / n_exp))
        eid = jnp.where(rows16 == k, tk_[k] - pos * n_exp, eid)
        gate = jnp.where(rows16 == k, es[k] / denom, gate)
    eid_ref[...] = eid.astype(I32)
    gate_ref[...] = gate


def _peer_topk(q, keys, tm=1024):
    t = q.shape[0]
    tm = min(tm, t)
    out_spec = pl.BlockSpec((PEER_TOPK, tm), lambda i, h: (h, i))
    return pl.pallas_call(
        _peer_topk_kernel,
        grid=(t // tm, PEER_HEADS),
        in_specs=[pl.BlockSpec((tm, 2 * PEER_NKEYS), lambda i, h: (i, h)),
                  pl.BlockSpec((None, 2, PEER_NKEYS, PEER_NKEYS), lambda i, h: (h, 0, 0, 0))],
        out_specs=[out_spec, out_spec],
        out_shape=[jax.ShapeDtypeStruct((PEER_HEADS * PEER_TOPK, t), I32),
                   jax.ShapeDtypeStruct((PEER_HEADS * PEER_TOPK, t), F32)],
        compiler_params=_cparams(("parallel", "parallel")),
        name="peer_topk",
    )(q, keys)


def _pack_table(tab):
    e, d = tab.shape
    assert d == 2 * SUBLANES * LANES
    bits = lax.bitcast_convert_type(tab.astype(BF16), jnp.uint16).astype(U32)
    packed = bits[:, :d // 2] | (bits[:, d // 2:] << 16)
    return packed.reshape(e * SUBLANES, LANES)


def _unpack(w):
    lo = lax.bitcast_convert_type(w << 16, F32)
    hi = lax.bitcast_convert_type(w & jnp.uint32(0xFFFF0000), F32)
    return lo, hi


_BITREV8 = (0, 4, 2, 6, 1, 5, 3, 7)


def _sublane_fold8(parts):
    sub = lax.broadcasted_iota(I32, (2 * SUBLANES, LANES), 0) // 2

    def rolled(a, shift):
        return pltpu.bitcast(pltpu.roll(pltpu.bitcast(a, U32), shift, axis=0), BF16)

    lvl = [parts[_BITREV8[r]] for r in range(8)]
    for shift, mask in ((4, sub < 4), (2, (sub % 4) < 2), (1, (sub % 2) < 1)):
        nxt = []
        for a, b in zip(lvl[0::2], lvl[1::2]):
            nxt.append(jnp.where(mask, a + rolled(a, SUBLANES - shift), b + rolled(b, shift)))
        lvl = nxt
    return lvl[0]


def _pair_ranges(half):
    if half == 0:
        return (0, PEER_WINDOW), (PEER_WINDOW, PEER_PAIRS)
    return (PEER_PAIRS - PEER_WINDOW, PEER_PAIRS), (0, PEER_PAIRS - PEER_WINDOW)


def _overflow(n_low, half):
    return n_low > PEER_WINDOW if half == 0 else n_low < PEER_PAIRS - PEER_WINDOW


def _table_rows(tab_ref, off):
    return _unpack(tab_ref[pl.ds(pl.multiple_of(off, SUBLANES), SUBLANES), :])


def _peer_u_kernel(off_ref, nlow_ref, tab_ref, x_ref, o_ref, extra_ref, *, tm, half):
    lane = lax.broadcasted_iota(I32, (SUBLANES, LANES), 1)
    sub = lax.broadcasted_iota(I32, (SUBLANES, LANES), 0)
    lane_grp = lax.shift_right_logical(lane, 3)
    diag = sub == (lane & (SUBLANES - 1))
    main, rest = _pair_ranges(half)

    def folded(t, p0, p1):
        xb = pltpu.bitcast(x_ref[t], BF16)
        out = []
        for g in range(p0 // SUBLANES, p1 // SUBLANES):
            parts = []
            for r in range(SUBLANES):
                off = pl.multiple_of(off_ref[t, g * SUBLANES + r], SUBLANES)
                parts.append(pltpu.bitcast(tab_ref[pl.ds(off, SUBLANES), :], BF16) * xb)
            out.append(pltpu.bitcast(_sublane_fold8(parts), U32))
        return out

    def lane_sums(folds, p0):
        mat = jnp.zeros((SUBLANES, LANES), F32)
        for i, f in enumerate(folds):
            lo, hi = _unpack(f)
            mat = jnp.where(lane_grp == p0 // SUBLANES + i, jnp.sum(lo + hi, axis=-1, keepdims=True), mat)
        return jnp.sum(jnp.where(diag, mat, 0.0), axis=0, keepdims=True)

    def finish(t, folds):
        o_ref[pl.ds(t, 1), :] = lane_sums(folds, main[0]) + extra_ref[pl.ds(t, 1), :]

    def token(t, prev):
        cur = folded(t, *main)
        finish(jnp.maximum(t - 1, 0), prev)
        extra_ref[pl.ds(t, 1), :] = jnp.zeros((1, LANES), F32)

        @pl.when(_overflow(nlow_ref[0, t], half))
        def _():
            extra_ref[pl.ds(t, 1), :] = lane_sums(folded(t, *rest), rest[0])

        return tuple(cur)

    extra_ref[pl.ds(0, 1), :] = jnp.zeros((1, LANES), F32)
    zeros = tuple(jnp.zeros((SUBLANES, LANES), U32) for _ in range((main[1] - main[0]) // SUBLANES))
    last = lax.fori_loop(0, tm, token, zeros)
    finish(tm - 1, last)


def _peer_u(off, nlow3, tab, x4, half, tm=128):
    t = off.shape[0]
    rows = PEER_HALF * SUBLANES
    return pl.pallas_call(
        functools.partial(_peer_u_kernel, tm=tm, half=half),
        grid=(t // tm,),
        in_specs=[pl.BlockSpec((tm, LANES), lambda i: (i, 0), memory_space=pltpu.SMEM),
                  pl.BlockSpec((None, 1, tm), lambda i: (i, 0, 0), memory_space=pltpu.SMEM),
                  pl.BlockSpec((rows, LANES), lambda i: (half, 0), pipeline_mode=pl.Buffered(1)),
                  pl.BlockSpec((tm, SUBLANES, LANES), lambda i: (i, 0, 0))],
        out_specs=pl.BlockSpec((tm, LANES), lambda i: (i, 0)),
        out_shape=jax.ShapeDtypeStruct((t, LANES), F32),
        scratch_shapes=[pltpu.VMEM((tm, LANES), F32)],
        compiler_params=_cparams(("arbitrary",)),
        name="peer_expert_scores",
    )(off, nlow3, tab, x4)


def _peer_order_kernel(eid_ref, eid_o, slot_o, off_o, nlow_o):
    eid = eid_ref[...]
    lane = lax.broadcasted_iota(I32, eid.shape, 1)
    low = (eid < PEER_HALF).astype(I32)
    c_low = low
    shift = 1
    while shift < PEER_PAIRS:
        c_low = c_low + jnp.where(lane >= shift, pltpu.roll(c_low, shift, axis=1), 0)
        shift *= 2
    c_high = lane + 1 - c_low
    n_low = jnp.max(c_low, axis=1, keepdims=True)
    in_low = lane < n_low
    rank = jnp.where(in_low, lane + 1, lane + 1 - n_low)
    pos = jnp.zeros_like(lane)
    step = PEER_PAIRS // 2
    while step >= 1:
        probe = pos + (step - 1)
        count = jnp.where(in_low, jnp.take_along_axis(c_low, probe, axis=1),
                          jnp.take_along_axis(c_high, probe, axis=1))
        pos = jnp.where(count < rank, pos + step, pos)
        step //= 2
    eid_s = jnp.take_along_axis(eid, pos, axis=1)
    eid_o[...] = eid_s
    slot_o[...] = pos
    off_o[...] = (eid_s & (PEER_HALF - 1)) * SUBLANES
    nlow_o[...] = jnp.broadcast_to(n_low, eid.shape)


def _peer_order(eid, tm=1024):
    t = eid.shape[0]
    tm = min(tm, t)
    spec = pl.BlockSpec((tm, LANES), lambda i: (i, 0))
    return pl.pallas_call(
        _peer_order_kernel,
        grid=(t // tm,),
        in_specs=[spec],
        out_specs=[spec] * 4,
        out_shape=[jax.ShapeDtypeStruct((t, LANES), I32)] * 4,
        compiler_params=_cparams(("parallel",)),
        name="peer_pair_order",
    )(eid)


def _peer_coef_kernel(a0_ref, a1_ref, eid_ref, slot_ref, gate_ref, c0_ref, c1_ref):
    low = eid_ref[...] < PEER_HALF
    a = jnp.where(low, a0_ref[...], a1_ref[...])
    gate = jnp.take_along_axis(gate_ref[...], slot_ref[...], axis=1)
    coef = gate * (0.5 * a * (1.0 + lax.erf(a * (1.0 / math.sqrt(2.0)))))
    c0_ref[...] = jnp.where(low, coef, 0.0)
    c1_ref[...] = jnp.where(low, 0.0, coef)


def _peer_coef(a0, a1, eid, slot, gate, tm=1024):
    t = eid.shape[0]
    tm = min(tm, t)
    spec = pl.BlockSpec((tm, LANES), lambda i: (i, 0))
    return pl.pallas_call(
        _peer_coef_kernel,
        grid=(t // tm,),
        in_specs=[spec, spec, spec, spec, spec],
        out_specs=[spec, spec],
        out_shape=[jax.ShapeDtypeStruct((t, LANES), F32)] * 2,
        compiler_params=_cparams(("parallel",)),
        name="peer_coef",
    )(a0, a1, eid, slot, gate)


def _peer_v_kernel(off_ref, nlow_ref, coef_ref, tab_ref, base_ref, *o_refs, tm, half, n_first):
    n_acc = 4
    main, rest = _pair_ranges(half)

    def weighted(t, p0, p1):
        acc_lo = [jnp.zeros((SUBLANES, LANES), F32) for _ in range(n_acc)]
        acc_hi = [jnp.zeros((SUBLANES, LANES), F32) for _ in range(n_acc)]
        for p in range(p0, p1):
            c = coef_ref[t, p]
            lo, hi = _table_rows(tab_ref, off_ref[t, p])
            acc_lo[p % n_acc] = acc_lo[p % n_acc] + c * lo
            acc_hi[p % n_acc] = acc_hi[p % n_acc] + c * hi
        return ((acc_lo[0] + acc_lo[1]) + (acc_lo[2] + acc_lo[3]),
                (acc_hi[0] + acc_hi[1]) + (acc_hi[2] + acc_hi[3]))

    def run(o_ref):
        def store_row(t, lo, hi):
            o_ref[t, 0] = base_ref[t, 0] + lo
            o_ref[t, 1] = base_ref[t, 1] + hi

        def token(t, carry):
            lo, hi = weighted(t, *main)
            store_row(t, lo, hi)

            @pl.when(_overflow(nlow_ref[0, t], half))
            def _():
                lo2, hi2 = weighted(t, *rest)
                store_row(t, lo + lo2, hi + hi2)

            return carry

        lax.fori_loop(0, tm, token, 0)

    if len(o_refs) == 1:
        run(o_refs[0])
    else:
        first = pl.program_id(0) < n_first
        pl.when(first)(lambda: run(o_refs[0]))
        pl.when(jnp.logical_not(first))(lambda: run(o_refs[1]))


def _peer_v(off, nlow3, coef, tab, base, half, tm=128, split=None):
    t = base.shape[0]
    rows = PEER_HALF * SUBLANES
    smem = pl.BlockSpec((tm, LANES), lambda i: (i, 0), memory_space=pltpu.SMEM)
    tile = pl.BlockSpec((tm, 2, SUBLANES, LANES), lambda i: (i, 0, 0, 0))
    if split is None:
        n_first, out_specs = 0, tile
        out_shape = jax.ShapeDtypeStruct((t, 2, SUBLANES, LANES), F32)
    else:
        assert split % tm == 0
        n_first, n_rest = split // tm, (t - split) // tm
        blk = (tm, 2, SUBLANES, LANES)
        out_specs = [pl.BlockSpec(blk, lambda i: (jnp.clip(i, 0, n_first - 1), 0, 0, 0)),
                     pl.BlockSpec(blk, lambda i: (jnp.clip(i - n_first, 0, n_rest - 1), 0, 0, 0))]
        out_shape = [jax.ShapeDtypeStruct((split, 2, SUBLANES, LANES), F32),
                     jax.ShapeDtypeStruct((t - split, 2, SUBLANES, LANES), F32)]
    return pl.pallas_call(
        functools.partial(_peer_v_kernel, tm=tm, half=half, n_first=n_first),
        grid=(t // tm,),
        in_specs=[smem,
                  pl.BlockSpec((None, 1, tm), lambda i: (i, 0, 0), memory_space=pltpu.SMEM),
                  smem,
                  pl.BlockSpec((rows, LANES), lambda i: (half, 0), pipeline_mode=pl.Buffered(1)),
                  tile],
        out_specs=out_specs,
        out_shape=out_shape,
        compiler_params=_cparams(("arbitrary",)),
        name="peer_expert_sum",
    )(off, nlow3, coef, tab, base)


def _peer(x1, split, norm2_g, wq_bf, keys_bf, u_packed, v_packed, tm=128):
    t, d = x1.shape
    tm = min(tm, t)
    (xn_bf,) = _rmsnorm(x1, norm2_g, (BF16,))
    q = _matmul(xn_bf, wq_bf, wq_bf.shape[1], 0, BF16)
    eid_t, gate_t = _peer_topk(q, keys_bf)
    eid, gate = eid_t.T, gate_t.T
    eid, slot, off, nlow = _peer_order(eid)
    nlow3 = nlow[:, 0].reshape(t // tm, 1, tm)
    xw = _pack_table(xn_bf).reshape(t, SUBLANES, LANES)
    a0 = _peer_u(off, nlow3, u_packed, xw, 0, tm)
    a1 = _peer_u(off, nlow3, u_packed, xw, 1, tm)
    c0, c1 = _peer_coef(a0, a1, eid, slot, gate)
    y = _peer_v(off, nlow3, c0, v_packed, x1.reshape(t, 2, SUBLANES, LANES), 0, tm)
    ya, yb = _peer_v(off, nlow3, c1, v_packed, y, 1, tm, split=split)
    return ya.reshape(split, d), yb.reshape(t - split, d)


def kernel(x_prompt, x_sample, norm1_g, w_in, q_norm_g, k_norm_g, lambda_q1, lambda_k1, lambda_q2, lambda_k2, attn_sub_g, conv_w, conv_b, filt_w1, filt_b1, filt_w2, filt_b2, filt_w3, filt_b3, filt_w4, filt_freq, fft_bias, hyena_out_g, w_out, norm2_g, peer_wq, peer_keys, peer_u, peer_v):
    depth = w_in.shape[0]
    d_model = x_prompt.shape[-1]
    att_w = ATT_HEADS * ATT_VDIM
    shapes = [x_prompt.shape[:2], x_sample.shape[:2]]
    xs = [x_prompt.reshape(-1, d_model), x_sample.reshape(-1, d_model)]
    n0 = xs[0].shape[0]
    slopes = 2.0 ** (-8.0 * jnp.arange(1, ATT_HEADS + 1, dtype=F32) / ATT_HEADS)

    for l in range(depth):
        lambda_init = 0.8 - 0.6 * math.exp(-0.3 * l)
        lam = (jnp.exp(jnp.sum(lambda_q1[l].astype(F32) * lambda_k1[l].astype(F32)))
               - jnp.exp(jnp.sum(lambda_q2[l].astype(F32) * lambda_k2[l].astype(F32)))
               + lambda_init).reshape(1)
        w_in_bf = w_in[l].astype(BF16)
        q_gain = jnp.tile(q_norm_g[l].astype(F32), 2 * ATT_HEADS) * (ATT_QKDIM ** -0.5 * LOG2E)
        k_gain = jnp.tile(k_norm_g[l].astype(F32), 2 * ATT_HEADS)
        qk_gain = jnp.concatenate([q_gain, k_gain]).reshape(1, 2 * att_w)

        h_bf = _rmsnorm_stacked(xs[0], xs[1], norm1_g[l], BF16)
        qk = _matmul(h_bf, w_in_bf, 2 * att_w, 0, BF16, mode="qknorm", extra=qk_gain)
        vt = _matmul_nt(w_in_bf[:, 2 * att_w:3 * att_w].T, h_bf, BF16)
        zh = _matmul(h_bf, w_in_bf, w_in.shape[2] - 3 * att_w, 3 * att_w, F32)

        filt = (filt_w1[l], filt_b1[l], filt_w2[l], filt_b2[l], filt_w3[l], filt_b3[l], filt_w4[l], filt_freq[l])
        segs, row = [], 0
        for (b, s), x_seg in zip(shapes, xs):
            att = _attention(qk, vt, row, slopes, lam, attn_sub_g[l], b, s, 1.0 - lambda_init)
            hy = _hyena(zh, row, b, s, conv_w[l], conv_b[l], filt, fft_bias[l], hyena_out_g[l])
            segs.append((att, hy, x_seg))
            row += b * s
        x1 = _outproj_stacked(segs[0], segs[1], w_out[l].astype(BF16))

        xs = _peer(x1, n0, norm2_g[l], peer_wq[l].astype(BF16), peer_keys[l].astype(BF16),
                   _pack_table(peer_u[l]), _pack_table(peer_v[l]))

    return (xs[0].reshape(x_prompt.shape), xs[1].reshape(x_sample.shape))
```

```python
import functools
import math

import jax
import jax.numpy as jnp
from jax import lax
from jax.experimental import pallas as pl
from jax.experimental.pallas import tpu as pltpu

F32 = jnp.float32
BF16 = jnp.bfloat16
I32 = jnp.int32
U32 = jnp.uint32

RMS_EPS = 1e-6
LOG2E = 1.4426950408889634
LANES = 128
SUBLANES = 8
VMEM_LIMIT_BYTES = 56 * 1024 * 1024

ATT_HEADS = 8
ATT_VDIM = 128
ATT_QKDIM = 64
HYENA_ORDER = 2
FILTER_BANDS = 16
DECAY_FAST = 0.3
DECAY_SLOW = 1.5
DECAY_TARGET = 1e-2
PEER_HEADS = 8
PEER_NKEYS = 128
PEER_TOPK = 16
PEER_HALF = PEER_NKEYS * PEER_NKEYS // 2
PEER_PAIRS = PEER_HEADS * PEER_TOPK
PEER_WINDOW = 80


def _cparams(sem, vmem=VMEM_LIMIT_BYTES):
    return pltpu.CompilerParams(dimension_semantics=sem, vmem_limit_bytes=vmem)


def _segment_spec(block, start, count, minor=0):
    return pl.BlockSpec(block, lambda i: (jnp.clip(i - start, 0, count - 1), minor))


def _rmsnorm_kernel(x_ref, g_ref, *o_refs):
    x = x_ref[...]
    ms = jnp.mean(x * x, axis=-1, keepdims=True)
    y = x * lax.rsqrt(ms + RMS_EPS) * g_ref[...]
    for o_ref in o_refs:
        o_ref[...] = y.astype(o_ref.dtype)


def _rmsnorm2_kernel(xa_ref, xb_ref, g_ref, o_ref, *, na):
    first = pl.program_id(0) < na
    x = jnp.where(first, xa_ref[...], xb_ref[...])
    ms = jnp.mean(x * x, axis=-1, keepdims=True)
    o_ref[...] = (x * lax.rsqrt(ms + RMS_EPS) * g_ref[...]).astype(o_ref.dtype)


def _rmsnorm_stacked(xa, xb, g, out_dtype, tm=512):
    d = xa.shape[1]
    na, nb = xa.shape[0] // tm, xb.shape[0] // tm
    assert xa.shape[0] % tm == 0 and xb.shape[0] % tm == 0
    return pl.pallas_call(
        functools.partial(_rmsnorm2_kernel, na=na),
        grid=(na + nb,),
        in_specs=[_segment_spec((tm, d), 0, na), _segment_spec((tm, d), na, nb),
                  pl.BlockSpec((1, d), lambda i: (0, 0))],
        out_specs=pl.BlockSpec((tm, d), lambda i: (i, 0)),
        out_shape=jax.ShapeDtypeStruct((xa.shape[0] + xb.shape[0], d), out_dtype),
        compiler_params=_cparams(("parallel",)),
        name="rmsnorm_stacked",
    )(xa, xb, g.reshape(1, d).astype(F32))


def _rmsnorm(x, g, out_dtypes, tm=512):
    t, d = x.shape
    tm = min(tm, t)
    spec = pl.BlockSpec((tm, d), lambda i: (i, 0))
    return pl.pallas_call(
        _rmsnorm_kernel,
        grid=(t // tm,),
        in_specs=[spec, pl.BlockSpec((1, d), lambda i: (0, 0))],
        out_specs=[spec for _ in out_dtypes],
        out_shape=[jax.ShapeDtypeStruct((t, d), dt) for dt in out_dtypes],
        compiler_params=_cparams(("parallel",)),
        name="rmsnorm",
    )(x, g.reshape(1, d).astype(F32))


def _group_rms_scale(x, gain):
    lane = lax.broadcasted_iota(I32, x.shape, 1)
    lo = lane < ATT_QKDIM
    x2 = x * x
    s_lo = jnp.sum(jnp.where(lo, x2, 0.0), axis=-1, keepdims=True)
    s_hi = jnp.sum(jnp.where(lo, 0.0, x2), axis=-1, keepdims=True)
    ms = jnp.where(lo, s_lo, s_hi) * (1.0 / ATT_QKDIM)
    return x * lax.rsqrt(ms + RMS_EPS) * gain


def _mm_kernel(a_ref, b_ref, *rest, mode):
    acc = jnp.dot(a_ref[...], b_ref[...], preferred_element_type=F32)
    if mode == "plain":
        (o_ref,) = rest
        o_ref[...] = acc.astype(o_ref.dtype)
    elif mode == "qknorm":
        g_ref, o_ref = rest
        for c in range(acc.shape[1] // LANES):
            sl = slice(c * LANES, (c + 1) * LANES)
            o_ref[:, sl] = _group_rms_scale(acc[:, sl], g_ref[:, sl]).astype(o_ref.dtype)
    else:
        raise ValueError(mode)


def _matmul(a, b, n_cols, col_off, out_dtype, mode="plain", extra=None, tm=1024, tn=1024):
    m, k = a.shape
    tm = min(tm, m)
    tn = min(tn, n_cols)
    assert col_off % tn == 0 and n_cols % tn == 0 and m % tm == 0
    off = col_off // tn
    in_specs = [pl.BlockSpec((tm, k), lambda i, j: (i, 0)),
                pl.BlockSpec((k, tn), lambda i, j: (0, j + off))]
    args = [a, b]
    if mode == "qknorm":
        in_specs.append(pl.BlockSpec((1, tn), lambda i, j: (0, j)))
        args.append(extra)
    return pl.pallas_call(
        functools.partial(_mm_kernel, mode=mode),
        grid=(m // tm, n_cols // tn),
        in_specs=in_specs,
        out_specs=pl.BlockSpec((tm, tn), lambda i, j: (i, j)),
        out_shape=jax.ShapeDtypeStruct((m, n_cols), out_dtype),
        compiler_params=_cparams(("parallel", "arbitrary")),
        name="matmul_" + mode,
    )(*args)


def _outproj_kernel(att_a, hy_a, x_a, att_b, hy_b, x_b, w_ref, o_ref, *, na):
    kw = att_a.shape[1]

    def run(att_ref, hy_ref, x_ref):
        acc = jnp.dot(att_ref[...], w_ref[:kw, :], preferred_element_type=F32)
        acc = acc + jnp.dot(hy_ref[...].astype(BF16), w_ref[kw:, :], preferred_element_type=F32)
        o_ref[...] = acc + x_ref[...]

    first = pl.program_id(0) < na
    pl.when(first)(lambda: run(att_a, hy_a, x_a))
    pl.when(jnp.logical_not(first))(lambda: run(att_b, hy_b, x_b))


def _outproj_stacked(seg_a, seg_b, w, tm=512, tn=1024):
    n_out = w.shape[1]
    na, nb = seg_a[0].shape[0] // tm, seg_b[0].shape[0] // tm
    assert seg_a[0].shape[0] % tm == 0 and seg_b[0].shape[0] % tm == 0 and n_out % tn == 0

    def specs(seg, start, count):
        att, hy, x = seg
        row = lambda i, j: (jnp.clip(i - start, 0, count - 1), 0)
        return [pl.BlockSpec((tm, att.shape[1]), row), pl.BlockSpec((tm, hy.shape[1]), row),
                pl.BlockSpec((tm, tn), lambda i, j: (jnp.clip(i - start, 0, count - 1), j))]

    return pl.pallas_call(
        functools.partial(_outproj_kernel, na=na),
        grid=(na + nb, n_out // tn),
        in_specs=specs(seg_a, 0, na) + specs(seg_b, na, nb) + [pl.BlockSpec((w.shape[0], tn), lambda i, j: (0, j))],
        out_specs=pl.BlockSpec((tm, tn), lambda i, j: (i, j)),
        out_shape=jax.ShapeDtypeStruct(((na + nb) * tm, n_out), F32),
        compiler_params=_cparams(("parallel", "arbitrary")),
        name="out_projection",
    )(*seg_a, *seg_b, w)


def _mm_nt_kernel(w_ref, a_ref, o_ref):
    o_ref[...] = lax.dot_general(w_ref[...], a_ref[...], (((1,), (1,)), ((), ())),
                                 preferred_element_type=F32).astype(o_ref.dtype)


def _matmul_nt(wt, a, out_dtype, tm=1024):
    n, k = wt.shape
    m = a.shape[0]
    tm = min(tm, m)
    return pl.pallas_call(
        _mm_nt_kernel,
        grid=(m // tm,),
        in_specs=[pl.BlockSpec((n, k), lambda i: (0, 0)), pl.BlockSpec((tm, k), lambda i: (i, 0))],
        out_specs=pl.BlockSpec((n, tm), lambda i: (0, i)),
        out_shape=jax.ShapeDtypeStruct((n, m), out_dtype),
        compiler_params=_cparams(("parallel",)),
        name="matmul_nt",
    )(wt, a)


def _alibi_columns(slopes, tq, tk):
    return [_alibi_side(slopes, tq, True), _alibi_side(slopes, tk, False)]


def _alibi_side(slopes, n, query_side):
    pos = jnp.arange(n, dtype=F32)
    val = (slopes.astype(F32) * LOG2E)[:, None] * pos[None, :]

    def pieces(x):
        p1 = x.astype(BF16)
        r1 = x - p1.astype(F32)
        p2 = r1.astype(BF16)
        p3 = (r1 - p2.astype(F32)).astype(BF16)
        return [p1, p2, p3]

    ones = [jnp.ones_like(val, BF16)] * 3
    six = jnp.stack(pieces(-val) + ones if query_side else ones + pieces(val), axis=-1)
    pad = jnp.zeros(val.shape + (ATT_QKDIM - 6,), BF16)
    return jnp.concatenate([six, pad, six, pad], axis=-1)


def _attn_kernel(slope_ref, lam_ref, q_ref, k_ref, vt_ref, aq_ref, ak_ref, g_ref, o_ref,
                 m_ref, l_ref, acc_ref, *, tq, tk, nk, hb, out_scale):
    hg = pl.program_id(1)
    i = pl.program_id(2)
    j = pl.program_id(3)

    @pl.when(j == 0)
    def _():
        m_ref[...] = jnp.full(m_ref.shape, -jnp.inf, F32)
        l_ref[...] = jnp.zeros(l_ref.shape, F32)
        acc_ref[...] = jnp.zeros(acc_ref.shape, F32)

    q_first = lax.broadcasted_iota(I32, (tq, LANES), 1) < ATT_QKDIM
    k_first = lax.broadcasted_iota(I32, (tk, LANES), 1) < ATT_QKDIM
    nt = (((1,), (1,)), ((), ()))

    def update(hh, scores, shift):
        vt = vt_ref[hh * LANES:(hh + 1) * LANES, :]
        for c, s in enumerate(scores):
            m_old = m_ref[hh, c]
            m_new = jnp.maximum(m_old, jnp.max(s, axis=0, keepdims=True) + shift)
            alpha = jnp.exp2(m_old - m_new)
            p = jnp.exp2(s - (m_new - shift))
            l_ref[hh, c] = alpha * l_ref[hh, c] + jnp.sum(p, axis=0, keepdims=True)
            acc_ref[hh, c] = alpha * acc_ref[hh, c] + jnp.dot(vt, p.astype(BF16),
                                                              preferred_element_type=F32)
            m_ref[hh, c] = m_new

    keys_before = i * tq >= (j + 1) * tk
    keys_after = (i + 1) * tq <= j * tk
    off_diagonal = jnp.logical_or(keys_before, keys_after)

    @pl.when(off_diagonal)
    def _():
        sign = jnp.where(keys_after, -1.0, 1.0).astype(BF16)
        gap = jnp.abs(i * tq - j * tk).astype(F32)
        for hh in range(hb):
            sl = slice(hh * LANES, (hh + 1) * LANES)
            q = q_ref[:, sl]
            k = k_ref[:, sl]
            aq = aq_ref[hh]
            ak = ak_ref[hh] * sign
            s0 = lax.dot_general(jnp.where(k_first, k, ak), jnp.where(q_first, q, aq), nt,
                                 preferred_element_type=F32)
            s1 = lax.dot_general(jnp.where(k_first, ak, k), jnp.where(q_first, aq, q), nt,
                                 preferred_element_type=F32)
            update(hh, (s0, s1), -(slope_ref[hg * hb + hh] * LOG2E) * gap)

    @pl.when(jnp.logical_not(off_diagonal))
    def _():
        kpos = lax.broadcasted_iota(I32, (tk, tq), 0) + j * tk
        qpos = lax.broadcasted_iota(I32, (tk, tq), 1) + i * tq
        dist = jnp.abs(kpos - qpos).astype(F32)
        for hh in range(hb):
            sl = slice(hh * LANES, (hh + 1) * LANES)
            q = q_ref[:, sl]
            k = k_ref[:, sl]
            zero = jnp.zeros_like(q)
            bias = dist * (-(slope_ref[hg * hb + hh] * LOG2E))
            s0 = lax.dot_general(k, jnp.where(q_first, q, zero), nt, preferred_element_type=F32) + bias
            s1 = lax.dot_general(k, jnp.where(q_first, zero, q), nt, preferred_element_type=F32) + bias
            update(hh, (s0, s1), 0.0)

    @pl.when(j == nk - 1)
    def _():
        for hh in range(hb):
            o = acc_ref[hh, 0] / l_ref[hh, 0] - lam_ref[0] * (acc_ref[hh, 1] / l_ref[hh, 1])
            ms = jnp.mean(o * o, axis=0, keepdims=True)
            y = o * lax.rsqrt(ms + RMS_EPS) * (g_ref[...] * out_scale)
            o_ref[:, hh * LANES:(hh + 1) * LANES] = y.T.astype(o_ref.dtype)


def _attention(qk, vt, row_off, slopes, lam, sub_g, batch, seq, out_scale, tq=512, tk=1024, hb=4):
    tq = min(tq, seq)
    tk = min(tk, seq // 4)
    nq, nk = seq // tq, seq // tk
    assert row_off % tq == 0 and row_off % tk == 0 and ATT_HEADS % hb == 0
    oq, ok = row_off // tq, row_off // tk
    ng = ATT_HEADS // hb
    aq, ak = _alibi_columns(slopes, tq, tk)
    kern = functools.partial(_attn_kernel, tq=tq, tk=tk, nk=nk, hb=hb, out_scale=out_scale)
    smem = pl.BlockSpec(memory_space=pltpu.SMEM)
    return pl.pallas_call(
        kern,
        grid=(batch, ng, nq, nk),
        in_specs=[smem, smem,
                  pl.BlockSpec((tq, hb * LANES), lambda b, h, i, j: (oq + b * nq + i, h)),
                  pl.BlockSpec((tk, hb * LANES), lambda b, h, i, j: (ok + b * nk + j, ng + h)),
                  pl.BlockSpec((hb * LANES, tk), lambda b, h, i, j: (h, ok + b * nk + j)),
                  pl.BlockSpec((hb, tq, LANES), lambda b, h, i, j: (h, 0, 0)),
                  pl.BlockSpec((hb, tk, LANES), lambda b, h, i, j: (h, 0, 0)),
                  pl.BlockSpec((LANES, 1), lambda b, h, i, j: (0, 0))],
        out_specs=pl.BlockSpec((tq, hb * LANES), lambda b, h, i, j: (b * nq + i, h)),
        out_shape=jax.ShapeDtypeStruct((batch * seq, ATT_HEADS * ATT_VDIM), BF16),
        scratch_shapes=[pltpu.VMEM((hb, 2, 1, tq), F32), pltpu.VMEM((hb, 2, 1, tq), F32),
                        pltpu.VMEM((hb, 2, LANES, tq), F32)],
        compiler_params=_cparams(("parallel", "parallel", "parallel", "arbitrary")),
        name="diff_attention",
    )(slopes, lam, qk, qk, vt, aq, ak, sub_g.reshape(LANES, 1).astype(F32))


def _shortconv_kernel(z_ref, w_ref, b_ref, o_ref):
    z = z_ref[...]
    n = z.shape[0]
    row = lax.broadcasted_iota(I32, z.shape, 0)
    prev = jnp.where(row == 0, 0.0, pltpu.roll(z, 1, axis=0))
    nxt = jnp.where(row == n - 1, 0.0, pltpu.roll(z, n - 1, axis=0))
    w = w_ref[...]
    o_ref[...] = prev * w[0:1] + z * w[1:2] + nxt * w[2:3] + b_ref[...]


def _shortconv(zh, row_off, conv_w, conv_b, batch, seq, cb=256):
    c3 = zh.shape[1]
    c = c3 // 3
    ncb = c // cb
    assert row_off % seq == 0
    ob = row_off // seq
    out = pl.pallas_call(
        _shortconv_kernel,
        grid=(batch, 3, ncb),
        in_specs=[pl.BlockSpec((seq, cb), lambda b, p, j: (ob + b, p * ncb + j)),
                  pl.BlockSpec((3, cb), lambda b, p, j: (0, p * ncb + j)),
                  pl.BlockSpec((1, cb), lambda b, p, j: (0, p * ncb + j))],
        out_specs=pl.BlockSpec((None, None, seq, cb), lambda b, p, j: (p, b, 0, j)),
        out_shape=jax.ShapeDtypeStruct((3, batch, seq, c), F32),
        compiler_params=_cparams(("parallel", "parallel", "parallel")),
        name="hyena_shortconv",
    )(zh, conv_w.astype(F32), conv_b.reshape(1, c3).astype(F32))
    return out


def _filter_kernel(z_ref, w1_ref, b1_ref, w2_ref, b2_ref, w3_ref, b3_ref, fr_ref, w4_ref,
                   t_ref, d_ref, o_ref, h_ref, *, tl):
    i = pl.program_id(0)
    g = pl.program_id(1)
    hi = lax.Precision.HIGHEST

    @pl.when(g == 0)
    def _():
        fr = fr_ref[...]
        h = jnp.sin(fr * (jnp.dot(z_ref[...], w1_ref[...], precision=hi, preferred_element_type=F32) + b1_ref[...]))
        h = jnp.sin(fr * (jnp.dot(h, w2_ref[...], precision=hi, preferred_element_type=F32) + b2_ref[...]))
        h_ref[...] = jnp.sin(fr * (jnp.dot(h, w3_ref[...], precision=hi, preferred_element_type=F32) + b3_ref[...]))

    f = jnp.dot(h_ref[...].astype(BF16), w4_ref[...], preferred_element_type=F32)
    f = f * jnp.exp(-t_ref[...] * d_ref[...])
    row = lax.broadcasted_iota(I32, f.shape, 0) + i * tl
    drop = jnp.logical_and(row == 0, g % 2 == 1)
    o_ref[...] = jnp.where(drop, 0.0, f)


def _hyena_filter_signals(seq, w1, b1, w2, b2, w3, b3, w4, freq, n_ch, tl=512):
    t = jnp.linspace(0.0, 1.0, seq, dtype=F32)[:, None]
    w = 2.0 * math.pi * jnp.arange(seq, dtype=F32)[:, None] / seq
    f = jnp.linspace(1e-4, FILTER_BANDS - 1, FILTER_BANDS, dtype=F32)[None, :]
    z = jnp.concatenate([t, jnp.cos(f * w), -jnp.sin(f * w)], axis=-1)
    deltas = jnp.abs(jnp.linspace(math.log(DECAY_FAST) / DECAY_TARGET,
                                  math.log(DECAY_SLOW) / DECAY_TARGET, n_ch, dtype=F32))[None, :]
    hid = w1.shape[1]
    emb = LANES
    z = jnp.pad(z, ((0, 0), (0, emb - z.shape[1])))
    w1 = jnp.pad(w1.astype(F32), ((0, emb - w1.shape[0]), (0, 0)))
    tl = min(tl, seq)
    full = lambda shape: pl.BlockSpec(shape, lambda i, g: tuple(0 for _ in shape))
    return pl.pallas_call(
        functools.partial(_filter_kernel, tl=tl),
        grid=(seq // tl, 2 * HYENA_ORDER),
        in_specs=[pl.BlockSpec((tl, emb), lambda i, g: (i, 0)),
                  full((emb, hid)), full((1, hid)), full((hid, hid)), full((1, hid)),
                  full((hid, hid)), full((1, hid)), full((1, hid)),
                  pl.BlockSpec((hid, n_ch), lambda i, g: (0, g)),
                  pl.BlockSpec((tl, 1), lambda i, g: (i, 0)),
                  full((1, n_ch))],
        out_specs=pl.BlockSpec((None, tl, n_ch), lambda i, g: (g, i, 0)),
        out_shape=jax.ShapeDtypeStruct((2 * HYENA_ORDER, seq, n_ch), F32),
        scratch_shapes=[pltpu.VMEM((tl, hid), F32)],
        compiler_params=_cparams(("parallel", "arbitrary")),
        name="hyena_filter_mlp",
    )(z, w1.astype(F32), b1.reshape(1, hid).astype(F32), w2.astype(F32), b2.reshape(1, hid).astype(F32),
      w3.astype(F32), b3.reshape(1, hid).astype(F32), freq.reshape(1, hid).astype(F32), w4.astype(BF16),
      t, deltas)


def _dft_tables(r):
    n = r * r
    k2 = jnp.arange(r, dtype=I32)
    n2 = jnp.arange(r // 2, dtype=I32)
    ang1 = (2.0 * math.pi / r) * ((k2[:, None] * n2[None, :]) % r).astype(F32)
    f1 = jnp.concatenate([jnp.cos(ang1), -jnp.sin(ang1)], axis=0)
    k1 = jnp.arange(r, dtype=I32)
    n1 = jnp.arange(r, dtype=I32)
    ang_a = (2.0 * math.pi / r) * ((k1[:, None] * n1[None, :]) % r).astype(F32)
    ang_b = (2.0 * math.pi / n) * (k2[:, None] * n1[None, :]).astype(F32)
    ca, sa = jnp.cos(ang_a)[None], jnp.sin(ang_a)[None]
    cb, sb = jnp.cos(ang_b), jnp.sin(ang_b)

    def blocks(cb3, sb3):
        return (ca * cb3 - sa * sb3).astype(BF16), (-(sa * cb3 + ca * sb3)).astype(BF16)

    mr, mi_ = blocks(cb[:, None, :], sb[:, None, :])
    mf = jnp.concatenate([jnp.concatenate([mr, -mi_], axis=2),
                          jnp.concatenate([mi_, mr], axis=2)], axis=1)
    mr_t, mi_t = blocks(cb[:, :, None], sb[:, :, None])
    minv = jnp.concatenate([jnp.concatenate([mr_t, mi_t], axis=2),
                            jnp.concatenate([-mi_t, mr_t], axis=2)], axis=1)
    g3 = jnp.concatenate([jnp.cos(ang1.T), -jnp.sin(ang1.T)], axis=1) * (1.0 / n)
    eye = jnp.eye(N1_BLOCK, dtype=F32)
    f1, g3 = jnp.kron(f1, eye), jnp.kron(g3, eye)
    return f1.astype(BF16), mf, minv, g3.astype(BF16)


N1_BLOCK = SUBLANES


def _pack_complex(re, im):
    rb = lax.bitcast_convert_type(re.astype(BF16).astype(F32), U32)
    ib = lax.bitcast_convert_type(im.astype(BF16).astype(F32), U32)
    return (rb >> 16) | ib


def _unpack_complex_rows(w):
    re, im = _unpack(w)
    return jnp.concatenate([re, im], axis=0).astype(BF16)


def _fft1_kernel(f_ref, x_ref, o_ref, *, r):
    n_ch = x_ref.shape[-1]
    x = x_ref[...].reshape((r // 2) * N1_BLOCK, n_ch).astype(BF16)
    res = jnp.dot(f_ref[...], x, preferred_element_type=F32)
    half = r * N1_BLOCK
    o_ref[...] = _pack_complex(res[:half], res[half:]).reshape(r, N1_BLOCK, n_ch)


def _fft_stage1(x4, part, f1, r, n_ch):
    nb = x4.shape[1]
    xv = x4.reshape(x4.shape[0], nb, r // 2, r, n_ch)
    return pl.pallas_call(
        functools.partial(_fft1_kernel, r=r),
        grid=(nb, r // N1_BLOCK),
        in_specs=[pl.BlockSpec(f1.shape, lambda b, j: (0, 0)),
                  pl.BlockSpec((None, None, r // 2, N1_BLOCK, n_ch), lambda b, j: (part, b, 0, j, 0))],
        out_specs=pl.BlockSpec((None, r, N1_BLOCK, n_ch), lambda b, j: (b, 0, j, 0)),
        out_shape=jax.ShapeDtypeStruct((nb, r, r, n_ch), U32),
        compiler_params=_cparams(("parallel", "parallel")),
        name="hyena_dft_stage1",
    )(f1, xv)


def _k2_block(r):
    return max(1, min(r, (8 * LANES) // r))


def _filter_spec_kernel(mf_ref, bf_ref, bb_ref, o_ref, *, r):
    for kk in range(mf_ref.shape[0]):
        m = mf_ref[kk]
        xf = jnp.dot(m, _unpack_complex_rows(bf_ref[kk]), preferred_element_type=F32)
        xb = jnp.dot(m, _unpack_complex_rows(bb_ref[kk]), preferred_element_type=F32)
        o_ref[kk] = _pack_complex(xf[:r] + xb[:r], xf[r:] - xb[r:])


def _filter_spectrum(b1, mf, r, n_ch):
    kb = _k2_block(r)
    blk = lambda sel: pl.BlockSpec((None, kb, r, n_ch), lambda o, k: (2 * o + sel, k, 0, 0))
    return pl.pallas_call(
        functools.partial(_filter_spec_kernel, r=r),
        grid=(HYENA_ORDER, r // kb),
        in_specs=[pl.BlockSpec((kb, 2 * r, 2 * r), lambda o, k: (k, 0, 0)), blk(0), blk(1)],
        out_specs=pl.BlockSpec((None, kb, r, n_ch), lambda o, k: (o, k, 0, 0)),
        out_shape=jax.ShapeDtypeStruct((HYENA_ORDER, r, r, n_ch), U32),
        compiler_params=_cparams(("parallel", "parallel")),
        name="hyena_filter_spectrum",
    )(mf, b1, b1)


def _fft2_kernel(mf_ref, mi_ref, b_ref, h_ref, o_ref, *, r):
    for kk in range(mf_ref.shape[0]):
        x = jnp.dot(mf_ref[kk], _unpack_complex_rows(b_ref[kk]), preferred_element_type=F32)
        xr, xi = x[:r], x[r:]
        hr, hi = _unpack(h_ref[kk])
        y = jnp.concatenate([xr * hr - xi * hi, xr * hi + xi * hr], axis=0).astype(BF16)
        c = jnp.dot(mi_ref[kk], y, preferred_element_type=F32)
        o_ref[kk] = _pack_complex(c[:r], c[r:])


def _fft_stage2(b1, h, order, mf, minv, r, n_ch):
    nb = b1.shape[0]
    kb = _k2_block(r)
    blk = pl.BlockSpec((None, kb, r, n_ch), lambda k, b: (b, k, 0, 0))
    return pl.pallas_call(
        functools.partial(_fft2_kernel, r=r),
        grid=(r // kb, nb),
        in_specs=[pl.BlockSpec((kb, 2 * r, 2 * r), lambda k, b: (k, 0, 0)),
                  pl.BlockSpec((kb, 2 * r, 2 * r), lambda k, b: (k, 0, 0)),
                  blk,
                  pl.BlockSpec((None, kb, r, n_ch), lambda k, b: (order, k, 0, 0))],
        out_specs=blk,
        out_shape=jax.ShapeDtypeStruct((nb, r, r, n_ch), U32),
        compiler_params=_cparams(("parallel", "parallel")),
        name="hyena_dft_stage2",
    )(mf, minv, b1, h)


def _fft3_kernel(g_ref, c_ref, gate_ref, s_ref, bias_ref, ng_ref, *rest, final):
    if final:
        (o_ref,) = rest
    else:
        f_ref, o_ref, b_ref = rest
    r, nb, n_ch = c_ref.shape
    cc = _unpack_complex_rows(c_ref[...].reshape(r * nb, n_ch))
    y = jnp.dot(g_ref[...], cc, preferred_element_type=F32)
    rows = (r // 2) * nb
    s_new = gate_ref[...].reshape(rows, n_ch) * (y + s_ref[...].reshape(rows, n_ch) * bias_ref[...])
    if final:
        ms = jnp.mean(s_new * s_new, axis=-1, keepdims=True)
        s_new = s_new * lax.rsqrt(ms + RMS_EPS) * ng_ref[...]
    o_ref[...] = s_new.reshape(r // 2, nb, n_ch)
    if not final:
        res = jnp.dot(f_ref[...], s_new.astype(BF16), preferred_element_type=F32)
        half = r * nb
        b_ref[...] = _pack_complex(res[:half], res[half:]).reshape(r, nb, n_ch)


def _fft_stage3(c2, g3, z4, gate_part, s4, s_part, bias, norm_g, f1, r, n_ch):
    final = f1 is None
    nb = c2.shape[0]
    zv = z4.reshape(z4.shape[0], nb, r // 2, r, n_ch)
    sv = s4.reshape(s4.shape[0], nb, r // 2, r, n_ch)
    dspec = lambda part: pl.BlockSpec((None, None, r // 2, N1_BLOCK, n_ch), lambda b, j: (part, b, 0, j, 0))
    packed = pl.BlockSpec((None, r, N1_BLOCK, n_ch), lambda b, j: (b, 0, j, 0))
    s_spec = pl.BlockSpec((None, r // 2, N1_BLOCK, n_ch), lambda b, j: (b, 0, j, 0))
    s_shape = jax.ShapeDtypeStruct((nb, r // 2, r, n_ch), F32)
    in_specs = [pl.BlockSpec(g3.shape, lambda b, j: (0, 0)), packed, dspec(gate_part), dspec(s_part),
                pl.BlockSpec((1, n_ch), lambda b, j: (0, 0)), pl.BlockSpec((1, n_ch), lambda b, j: (0, 0))]
    args = [g3, c2, zv, sv, bias.reshape(1, n_ch).astype(F32), norm_g.reshape(1, n_ch).astype(F32)]
    if final:
        out_specs, out_shape = s_spec, s_shape
    else:
        in_specs.append(pl.BlockSpec(f1.shape, lambda b, j: (0, 0)))
        args.append(f1)
        out_specs = [s_spec, packed]
        out_shape = [s_shape, jax.ShapeDtypeStruct((nb, r, r, n_ch), U32)]
    return pl.pallas_call(
        functools.partial(_fft3_kernel, final=final),
        grid=(nb, r // N1_BLOCK),
        in_specs=in_specs,
        out_specs=out_specs,
        out_shape=out_shape,
        compiler_params=_cparams(("parallel", "parallel")),
        name="hyena_dft_stage3",
    )(*args)


def _hyena(zh, row_off, batch, seq, conv_w, conv_b, filt, fft_bias, out_g):
    n_ch = zh.shape[1] // 3
    r = int(round(math.sqrt(2 * seq)))
    assert r * r == 2 * seq and r % 16 == 0
    f1, mf, minv, g3 = _dft_tables(r)
    sig = _hyena_filter_signals(seq, *filt, n_ch=n_ch)
    hb1 = _fft_stage1(sig[None], 0, f1, r, n_ch)
    h = _filter_spectrum(hb1, mf, r, n_ch)
    z4 = _shortconv(zh, row_off, conv_w, conv_b, batch, seq)
    s4, s_part = z4, 2
    b1 = _fft_stage1(s4, s_part, f1, r, n_ch)
    for o in range(HYENA_ORDER):
        c2 = _fft_stage2(b1, h, o, mf, minv, r, n_ch)
        if o < HYENA_ORDER - 1:
            s, b1 = _fft_stage3(c2, g3, z4, o, s4, s_part, fft_bias[o], out_g, f1, r, n_ch)
        else:
            s = _fft_stage3(c2, g3, z4, o, s4, s_part, fft_bias[o], out_g, None, r, n_ch)
        s4, s_part = s.reshape(1, batch, seq, n_ch), 0
    return s4.reshape(batch * seq, n_ch)


def _extract_top(s, key, count):
    vals, keys = [], []
    for _ in range(count):
        m = jnp.max(s, axis=0, keepdims=True)
        kmin = jnp.min(jnp.where(s == m, key, jnp.inf), axis=0, keepdims=True)
        s = jnp.where(key == kmin, -jnp.inf, s)
        vals.append(m)
        keys.append(kmin)
    return vals, keys


def _peer_topk_kernel(q_ref, keys_ref, eid_ref, gate_ref):
    t = q_ref.shape[0]
    nk = PEER_NKEYS
    q = q_ref[...]
    row_key = lax.broadcasted_iota(I32, (nk, t), 0).astype(F32)
    tops = []
    for c in range(2):
        s = lax.dot_general(keys_ref[c], q[:, c * nk:(c + 1) * nk], (((1,), (1,)), ((), ())),
                            preferred_element_type=F32)
        tops.append(_extract_top(s, row_key, PEER_TOPK))
    (v1, i1), (v2, i2) = tops
    rows16 = lax.broadcasted_iota(I32, (PEER_TOPK, t), 0)
    v2a = jnp.zeros((PEER_TOPK, t), F32)
    i2a = jnp.zeros((PEER_TOPK, t), F32)
    for j in range(PEER_TOPK):
        v2a = jnp.where(rows16 == j, v2[j], v2a)
        i2a = jnp.where(rows16 == j, i2[j], i2a)
    n_exp = float(nk * nk)
    half = PEER_TOPK // 2
    rows8 = lax.broadcasted_iota(I32, (half, t), 0)
    pos8 = rows8.astype(F32)
    v2h, i2h = v2a[:half], i2a[:half]
    cand = [v1[0] + v2a]
    ckey = [rows16.astype(F32) * n_exp + (i1[0] * float(nk) + i2a)]
    for i in range(1, half):
        cand.append(jnp.where(rows8 < PEER_TOPK // (i + 1), v1[i] + v2h, -jnp.inf))
        ckey.append((pos8 + float(i * PEER_TOPK)) * n_exp + (i1[i] * float(nk) + i2h))
    v1t = jnp.zeros((half, t), F32)
    i1t = jnp.zeros((half, t), F32)
    for r in range(half):
        v1t = jnp.where(rows8 == r, v1[half + r], v1t)
        i1t = jnp.where(rows8 == r, i1[half + r], i1t)
    cand.append(v1t + v2[0])
    ckey.append((pos8 + float(half)) * (PEER_TOPK * n_exp) + (i1t * float(nk) + i2[0]))
    tv, tk_ = _extract_top(jnp.concatenate(cand, axis=0), jnp.concatenate(ckey, axis=0), PEER_TOPK)
    denom = jnp.zeros((1, t), F32)
    es = []
    for k in range(PEER_TOPK):
        e = jnp.exp(tv[k] - tv[0])
        es.append(e)
        denom = denom + e
    eid = jnp.zeros((PEER_TOPK, t), F32)
    gate = jnp.zeros((PEER_TOPK, t), F32)
    for k in range(PEER_TOPK):
        pos = jnp.floor(tk_[k] * (1.0 / n_exp))
        eid = jnp.where(rows16 == k, tk_[k] - pos * n_exp, eid)
        gate = jnp.where(rows16 == k, es[k] / denom, gate)
    eid_ref[...] = eid.astype(I32)
    gate_ref[...] = gate


def _peer_topk(q, keys, tm=1024):
    t = q.shape[0]
    tm = min(tm, t)
    out_spec = pl.BlockSpec((PEER_TOPK, tm), lambda i, h: (h, i))
    return pl.pallas_call(
        _peer_topk_kernel,
        grid=(t // tm, PEER_HEADS),
        in_specs=[pl.BlockSpec((tm, 2 * PEER_NKEYS), lambda i, h: (i, h)),
                  pl.BlockSpec((None, 2, PEER_NKEYS, PEER_NKEYS), lambda i, h: (h, 0, 0, 0))],
        out_specs=[out_spec, out_spec],
        out_shape=[jax.ShapeDtypeStruct((PEER_HEADS * PEER_TOPK, t), I32),
                   jax.ShapeDtypeStruct((PEER_HEADS * PEER_TOPK, t), F32)],
        compiler_params=_cparams(("parallel", "parallel")),
        name="peer_topk",
    )(q, keys)


def _pack_table(tab):
    e, d = tab.shape
    assert d == 2 * SUBLANES * LANES
    bits = lax.bitcast_convert_type(tab.astype(BF16), jnp.uint16).astype(U32)
    packed = bits[:, :d // 2] | (bits[:, d // 2:] << 16)
    return packed.reshape(e * SUBLANES, LANES)


def _unpack(w):
    lo = lax.bitcast_convert_type(w << 16, F32)
    hi = lax.bitcast_convert_type(w & jnp.uint32(0xFFFF0000), F32)
    return lo, hi


_BITREV8 = (0, 4, 2, 6, 1, 5, 3, 7)


def _sublane_fold8(parts):
    sub = lax.broadcasted_iota(I32, (2 * SUBLANES, LANES), 0) // 2

    def rolled(a, shift):
        return pltpu.bitcast(pltpu.roll(pltpu.bitcast(a, U32), shift, axis=0), BF16)

    lvl = [parts[_BITREV8[r]] for r in range(8)]
    for shift, mask in ((4, sub < 4), (2, (sub % 4) < 2), (1, (sub % 2) < 1)):
        nxt = []
        for a, b in zip(lvl[0::2], lvl[1::2]):
            nxt.append(jnp.where(mask, a + rolled(a, SUBLANES - shift), b + rolled(b, shift)))
        lvl = nxt
    return lvl[0]


def _pair_ranges(half):
    if half == 0:
        return (0, PEER_WINDOW), (PEER_WINDOW, PEER_PAIRS)
    return (PEER_PAIRS - PEER_WINDOW, PEER_PAIRS), (0, PEER_PAIRS - PEER_WINDOW)


def _overflow(n_low, half):
    return n_low > PEER_WINDOW if half == 0 else n_low < PEER_PAIRS - PEER_WINDOW


def _table_rows(tab_ref, off):
    return _unpack(tab_ref[pl.ds(pl.multiple_of(off, SUBLANES), SUBLANES), :])


def _peer_u_kernel(off_ref, nlow_ref, tab_ref, x_ref, o_ref, extra_ref, *, tm, half):
    lane = lax.broadcasted_iota(I32, (SUBLANES, LANES), 1)
    sub = lax.broadcasted_iota(I32, (SUBLANES, LANES), 0)
    lane_grp = lax.shift_right_logical(lane, 3)
    diag = sub == (lane & (SUBLANES - 1))
    main, rest = _pair_ranges(half)

    def folded(t, p0, p1):
        xb = pltpu.bitcast(x_ref[t], BF16)
        out = []
        for g in range(p0 // SUBLANES, p1 // SUBLANES):
            parts = []
            for r in range(SUBLANES):
                off = pl.multiple_of(off_ref[t, g * SUBLANES + r], SUBLANES)
                parts.append(pltpu.bitcast(tab_ref[pl.ds(off, SUBLANES), :], BF16) * xb)
            out.append(pltpu.bitcast(_sublane_fold8(parts), U32))
        return out

    def lane_sums(folds, p0):
        mat = jnp.zeros((SUBLANES, LANES), F32)
        for i, f in enumerate(folds):
            lo, hi = _unpack(f)
            mat = jnp.where(lane_grp == p0 // SUBLANES + i, jnp.sum(lo + hi, axis=-1, keepdims=True), mat)
        return jnp.sum(jnp.where(diag, mat, 0.0), axis=0, keepdims=True)

    def finish(t, folds):
        o_ref[pl.ds(t, 1), :] = lane_sums(folds, main[0]) + extra_ref[pl.ds(t, 1), :]

    def token(t, prev):
        cur = folded(t, *main)
        finish(jnp.maximum(t - 1, 0), prev)
        extra_ref[pl.ds(t, 1), :] = jnp.zeros((1, LANES), F32)

        @pl.when(_overflow(nlow_ref[0, t], half))
        def _():
            extra_ref[pl.ds(t, 1), :] = lane_sums(folded(t, *rest), rest[0])

        return tuple(cur)

    extra_ref[pl.ds(0, 1), :] = jnp.zeros((1, LANES), F32)
    zeros = tuple(jnp.zeros((SUBLANES, LANES), U32) for _ in range((main[1] - main[0]) // SUBLANES))
    last = lax.fori_loop(0, tm, token, zeros)
    finish(tm - 1, last)


def _peer_u(off, nlow3, tab, x4, half, tm=128):
    t = off.shape[0]
    rows = PEER_HALF * SUBLANES
    return pl.pallas_call(
        functools.partial(_peer_u_kernel, tm=tm, half=half),
        grid=(t // tm,),
        in_specs=[pl.BlockSpec((tm, LANES), lambda i: (i, 0), memory_space=pltpu.SMEM),
                  pl.BlockSpec((None, 1, tm), lambda i: (i, 0, 0), memory_space=pltpu.SMEM),
                  pl.BlockSpec((rows, LANES), lambda i: (half, 0), pipeline_mode=pl.Buffered(1)),
                  pl.BlockSpec((tm, SUBLANES, LANES), lambda i: (i, 0, 0))],
        out_specs=pl.BlockSpec((tm, LANES), lambda i: (i, 0)),
        out_shape=jax.ShapeDtypeStruct((t, LANES), F32),
        scratch_shapes=[pltpu.VMEM((tm, LANES), F32)],
        compiler_params=_cparams(("arbitrary",)),
        name="peer_expert_scores",
    )(off, nlow3, tab, x4)


def _peer_order_kernel(eid_ref, eid_o, slot_o, off_o, nlow_o):
    eid = eid_ref[...]
    lane = lax.broadcasted_iota(I32, eid.shape, 1)
    low = (eid < PEER_HALF).astype(I32)
    c_low = low
    shift = 1
    while shift < PEER_PAIRS:
        c_low = c_low + jnp.where(lane >= shift, pltpu.roll(c_low, shift, axis=1), 0)
        shift *= 2
    c_high = lane + 1 - c_low
    n_low = jnp.max(c_low, axis=1, keepdims=True)
    in_low = lane < n_low
    rank = jnp.where(in_low, lane + 1, lane + 1 - n_low)
    pos = jnp.zeros_like(lane)
    step = PEER_PAIRS // 2
    while step >= 1:
        probe = pos + (step - 1)
        count = jnp.where(in_low, jnp.take_along_axis(c_low, probe, axis=1),
                          jnp.take_along_axis(c_high, probe, axis=1))
        pos = jnp.where(count < rank, pos + step, pos)
        step //= 2
    eid_s = jnp.take_along_axis(eid, pos, axis=1)
    eid_o[...] = eid_s
    slot_o[...] = pos
    off_o[...] = (eid_s & (PEER_HALF - 1)) * SUBLANES
    nlow_o[...] = jnp.broadcast_to(n_low, eid.shape)


def _peer_order(eid, tm=1024):
    t = eid.shape[0]
    tm = min(tm, t)
    spec = pl.BlockSpec((tm, LANES), lambda i: (i, 0))
    return pl.pallas_call(
        _peer_order_kernel,
        grid=(t // tm,),
        in_specs=[spec],
        out_specs=[spec] * 4,
        out_shape=[jax.ShapeDtypeStruct((t, LANES), I32)] * 4,
        compiler_params=_cparams(("parallel",)),
        name="peer_pair_order",
    )(eid)


def _peer_coef_kernel(a0_ref, a1_ref, eid_ref, slot_ref, gate_ref, c0_ref, c1_ref):
    low = eid_ref[...] < PEER_HALF
    a = jnp.where(low, a0_ref[...], a1_ref[...])
    gate = jnp.take_along_axis(gate_ref[...], slot_ref[...], axis=1)
    coef = gate * (0.5 * a * (1.0 + lax.erf(a * (1.0 / math.sqrt(2.0)))))
    c0_ref[...] = jnp.where(low, coef, 0.0)
    c1_ref[...] = jnp.where(low, 0.0, coef)


def _peer_coef(a0, a1, eid, slot, gate, tm=1024):
    t = eid.shape[0]
    tm = min(tm, t)
    spec = pl.BlockSpec((tm, LANES), lambda i: (i, 0))
    return pl.pallas_call(
        _peer_coef_kernel,
        grid=(t // tm,),
        in_specs=[spec, spec, spec, spec, spec],
        out_specs=[spec, spec],
        out_shape=[jax.ShapeDtypeStruct((t, LANES), F32)] * 2,
        compiler_params=_cparams(("parallel",)),
        name="peer_coef",
    )(a0, a1, eid, slot, gate)


def _peer_v_kernel(off_ref, nlow_ref, coef_ref, tab_ref, base_ref, *o_refs, tm, half, n_first):
    n_acc = 4
    main, rest = _pair_ranges(half)

    def weighted(t, p0, p1):
        acc_lo = [jnp.zeros((SUBLANES, LANES), F32) for _ in range(n_acc)]
        acc_hi = [jnp.zeros((SUBLANES, LANES), F32) for _ in range(n_acc)]
        for p in range(p0, p1):
            c = coef_ref[t, p]
            lo, hi = _table_rows(tab_ref, off_ref[t, p])
            acc_lo[p % n_acc] = acc_lo[p % n_acc] + c * lo
            acc_hi[p % n_acc] = acc_hi[p % n_acc] + c * hi
        return ((acc_lo[0] + acc_lo[1]) + (acc_lo[2] + acc_lo[3]),
                (acc_hi[0] + acc_hi[1]) + (acc_hi[2] + acc_hi[3]))

    def run(o_ref):
        def store_row(t, lo, hi):
            o_ref[t, 0] = base_ref[t, 0] + lo
            o_ref[t, 1] = base_ref[t, 1] + hi

        def token(t, carry):
            lo, hi = weighted(t, *main)
            store_row(t, lo, hi)

            @pl.when(_overflow(nlow_ref[0, t], half))
            def _():
                lo2, hi2 = weighted(t, *rest)
                store_row(t, lo + lo2, hi + hi2)

            return carry

        lax.fori_loop(0, tm, token, 0)

    if len(o_refs) == 1:
        run(o_refs[0])
    else:
        first = pl.program_id(0) < n_first
        pl.when(first)(lambda: run(o_refs[0]))
        pl.when(jnp.logical_not(first))(lambda: run(o_refs[1]))


def _peer_v(off, nlow3, coef, tab, base, half, tm=128, split=None):
    t = base.shape[0]
    rows = PEER_HALF * SUBLANES
    smem = pl.BlockSpec((tm, LANES), lambda i: (i, 0), memory_space=pltpu.SMEM)
    tile = pl.BlockSpec((tm, 2, SUBLANES, LANES), lambda i: (i, 0, 0, 0))
    if split is None:
        n_first, out_specs = 0, tile
        out_shape = jax.ShapeDtypeStruct((t, 2, SUBLANES, LANES), F32)
    else:
        assert split % tm == 0
        n_first, n_rest = split // tm, (t - split) // tm
        blk = (tm, 2, SUBLANES, LANES)
        out_specs = [pl.BlockSpec(blk, lambda i: (jnp.clip(i, 0, n_first - 1), 0, 0, 0)),
                     pl.BlockSpec(blk, lambda i: (jnp.clip(i - n_first, 0, n_rest - 1), 0, 0, 0))]
        out_shape = [jax.ShapeDtypeStruct((split, 2, SUBLANES, LANES), F32),
                     jax.ShapeDtypeStruct((t - split, 2, SUBLANES, LANES), F32)]
    return pl.pallas_call(
        functools.partial(_peer_v_kernel, tm=tm, half=half, n_first=n_first),
        grid=(t // tm,),
        in_specs=[smem,
                  pl.BlockSpec((None, 1, tm), lambda i: (i, 0, 0), memory_space=pltpu.SMEM),
                  smem,
                  pl.BlockSpec((rows, LANES), lambda i: (half, 0), pipeline_mode=pl.Buffered(1)),
                  tile],
        out_specs=out_specs,
        out_shape=out_shape,
        compiler_params=_cparams(("arbitrary",)),
        name="peer_expert_sum",
    )(off, nlow3, coef, tab, base)


def _peer(x1, split, norm2_g, wq_bf, keys_bf, u_packed, v_packed, tm=128):
    t, d = x1.shape
    tm = min(tm, t)
    (xn_bf,) = _rmsnorm(x1, norm2_g, (BF16,))
    q = _matmul(xn_bf, wq_bf, wq_bf.shape[1], 0, BF16)
    eid_t, gate_t = _peer_topk(q, keys_bf)
    eid, gate = eid_t.T, gate_t.T
    eid, slot, off, nlow = _peer_order(eid)
    nlow3 = nlow[:, 0].reshape(t // tm, 1, tm)
    xw = _pack_table(xn_bf).reshape(t, SUBLANES, LANES)
    a0 = _peer_u(off, nlow3, u_packed, xw, 0, tm)
    a1 = _peer_u(off, nlow3, u_packed, xw, 1, tm)
    c0, c1 = _peer_coef(a0, a1, eid, slot, gate)
    y = _peer_v(off, nlow3, c0, v_packed, x1.reshape(t, 2, SUBLANES, LANES), 0, tm)
    ya, yb = _peer_v(off, nlow3, c1, v_packed, y, 1, tm, split=split)
    return ya.reshape(split, d), yb.reshape(t - split, d)


def kernel(x_prompt, x_sample, norm1_g, w_in, q_norm_g, k_norm_g, lambda_q1, lambda_k1, lambda_q2, lambda_k2, attn_sub_g, conv_w, conv_b, filt_w1, filt_b1, filt_w2, filt_b2, filt_w3, filt_b3, filt_w4, filt_freq, fft_bias, hyena_out_g, w_out, norm2_g, peer_wq, peer_keys, peer_u, peer_v):
    depth = w_in.shape[0]
    d_model = x_prompt.shape[-1]
    att_w = ATT_HEADS * ATT_VDIM
    shapes = [x_prompt.shape[:2], x_sample.shape[:2]]
    xs = [x_prompt.reshape(-1, d_model), x_sample.reshape(-1, d_model)]
    n0 = xs[0].shape[0]
    slopes = 2.0 ** (-8.0 * jnp.arange(1, ATT_HEADS + 1, dtype=F32) / ATT_HEADS)

    for l in range(depth):
        lambda_init = 0.8 - 0.6 * math.exp(-0.3 * l)
        lam = (jnp.exp(jnp.sum(lambda_q1[l].astype(F32) * lambda_k1[l].astype(F32)))
               - jnp.exp(jnp.sum(lambda_q2[l].astype(F32) * lambda_k2[l].astype(F32)))
               + lambda_init).reshape(1)
        w_in_bf = w_in[l].astype(BF16)
        q_gain = jnp.tile(q_norm_g[l].astype(F32), 2 * ATT_HEADS) * (ATT_QKDIM ** -0.5 * LOG2E)
        k_gain = jnp.tile(k_norm_g[l].astype(F32), 2 * ATT_HEADS)
        qk_gain = jnp.concatenate([q_gain, k_gain]).reshape(1, 2 * att_w)

        h_bf = _rmsnorm_stacked(xs[0], xs[1], norm1_g[l], BF16)
        qk = _matmul(h_bf, w_in_bf, 2 * att_w, 0, BF16, mode="qknorm", extra=qk_gain)
        vt = _matmul_nt(w_in_bf[:, 2 * att_w:3 * att_w].T, h_bf, BF16)
        zh = _matmul(h_bf, w_in_bf, w_in.shape[2] - 3 * att_w, 3 * att_w, F32)

        filt = (filt_w1[l], filt_b1[l], filt_w2[l], filt_b2[l], filt_w3[l], filt_b3[l], filt_w4[l], filt_freq[l])
        segs, row = [], 0
        for (b, s), x_seg in zip(shapes, xs):
            att = _attention(qk, vt, row, slopes, lam, attn_sub_g[l], b, s, 1.0 - lambda_init)
            hy = _hyena(zh, row, b, s, conv_w[l], conv_b[l], filt, fft_bias[l], hyena_out_g[l])
            segs.append((att, hy, x_seg))
            row += b * s
        x1 = _outproj_stacked(segs[0], segs[1], w_out[l].astype(BF16))

        xs = _peer(x1, n0, norm2_g[l], peer_wq[l].astype(BF16), peer_keys[l].astype(BF16),
                   _pack_table(peer_u[l]), _pack_table(peer_v[l]))

    return (xs[0].reshape(x_prompt.shape), xs[1].reshape(x_sample.shape))
```

```python
import functools
import math

import jax
import jax.numpy as jnp
from jax import lax
from jax.experimental import pallas as pl
from jax.experimental.pallas import tpu as pltpu

F32 = jnp.float32
BF16 = jnp.bfloat16
I32 = jnp.int32
U32 = jnp.uint32

RMS_EPS = 1e-6
LOG2E = 1.4426950408889634
LANES = 128
SUBLANES = 8
VMEM_LIMIT_BYTES = 56 * 1024 * 1024

ATT_HEADS = 8
ATT_VDIM = 128
ATT_QKDIM = 64
HYENA_ORDER = 2
FILTER_BANDS = 16
DECAY_FAST = 0.3
DECAY_SLOW = 1.5
DECAY_TARGET = 1e-2
PEER_HEADS = 8
PEER_NKEYS = 128
PEER_TOPK = 16
PEER_HALF = PEER_NKEYS * PEER_NKEYS // 2
PEER_PAIRS = PEER_HEADS * PEER_TOPK
PEER_WINDOW = 80


def _cparams(sem, vmem=VMEM_LIMIT_BYTES):
    return pltpu.CompilerParams(dimension_semantics=sem, vmem_limit_bytes=vmem)


def _segment_spec(block, start, count, minor=0):
    return pl.BlockSpec(block, lambda i: (jnp.clip(i - start, 0, count - 1), minor))


def _rmsnorm_kernel(x_ref, g_ref, *o_refs):
    x = x_ref[...]
    ms = jnp.mean(x * x, axis=-1, keepdims=True)
    y = x * lax.rsqrt(ms + RMS_EPS) * g_ref[...]
    for o_ref in o_refs:
        o_ref[...] = y.astype(o_ref.dtype)


def _rmsnorm2_kernel(xa_ref, xb_ref, g_ref, o_ref, *, na):
    first = pl.program_id(0) < na
    x = jnp.where(first, xa_ref[...], xb_ref[...])
    ms = jnp.mean(x * x, axis=-1, keepdims=True)
    o_ref[...] = (x * lax.rsqrt(ms + RMS_EPS) * g_ref[...]).astype(o_ref.dtype)


def _rmsnorm_stacked(xa, xb, g, out_dtype, tm=512):
    d = xa.shape[1]
    na, nb = xa.shape[0] // tm, xb.shape[0] // tm
    assert xa.shape[0] % tm == 0 and xb.shape[0] % tm == 0
    return pl.pallas_call(
        functools.partial(_rmsnorm2_kernel, na=na),
        grid=(na + nb,),
        in_specs=[_segment_spec((tm, d), 0, na), _segment_spec((tm, d), na, nb),
                  pl.BlockSpec((1, d), lambda i: (0, 0))],
        out_specs=pl.BlockSpec((tm, d), lambda i: (i, 0)),
        out_shape=jax.ShapeDtypeStruct((xa.shape[0] + xb.shape[0], d), out_dtype),
        compiler_params=_cparams(("parallel",)),
        name="rmsnorm_stacked",
    )(xa, xb, g.reshape(1, d).astype(F32))


def _rmsnorm(x, g, out_dtypes, tm=512):
    t, d = x.shape
    tm = min(tm, t)
    spec = pl.BlockSpec((tm, d), lambda i: (i, 0))
    return pl.pallas_call(
        _rmsnorm_kernel,
        grid=(t // tm,),
        in_specs=[spec, pl.BlockSpec((1, d), lambda i: (0, 0))],
        out_specs=[spec for _ in out_dtypes],
        out_shape=[jax.ShapeDtypeStruct((t, d), dt) for dt in out_dtypes],
        compiler_params=_cparams(("parallel",)),
        name="rmsnorm",
    )(x, g.reshape(1, d).astype(F32))


def _group_rms_scale(x, gain):
    lane = lax.broadcasted_iota(I32, x.shape, 1)
    lo = lane < ATT_QKDIM
    x2 = x * x
    s_lo = jnp.sum(jnp.where(lo, x2, 0.0), axis=-1, keepdims=True)
    s_hi = jnp.sum(jnp.where(lo, 0.0, x2), axis=-1, keepdims=True)
    ms = jnp.where(lo, s_lo, s_hi) * (1.0 / ATT_QKDIM)
    return x * lax.rsqrt(ms + RMS_EPS) * gain


def _mm_kernel(a_ref, b_ref, *rest, mode):
    acc = jnp.dot(a_ref[...], b_ref[...], preferred_element_type=F32)
    if mode == "plain":
        (o_ref,) = rest
        o_ref[...] = acc.astype(o_ref.dtype)
    elif mode == "qknorm":
        g_ref, o_ref = rest
        for c in range(acc.shape[1] // LANES):
            sl = slice(c * LANES, (c + 1) * LANES)
            o_ref[:, sl] = _group_rms_scale(acc[:, sl], g_ref[:, sl]).astype(o_ref.dtype)
    else:
        raise ValueError(mode)


def _matmul(a, b, n_cols, col_off, out_dtype, mode="plain", extra=None, tm=1024, tn=1024):
    m, k = a.shape
    tm = min(tm, m)
    tn = min(tn, n_cols)
    assert col_off % tn == 0 and n_cols % tn == 0 and m % tm == 0
    off = col_off // tn
    in_specs = [pl.BlockSpec((tm, k), lambda i, j: (i, 0)),
                pl.BlockSpec((k, tn), lambda i, j: (0, j + off))]
    args = [a, b]
    if mode == "qknorm":
        in_specs.append(pl.BlockSpec((1, tn), lambda i, j: (0, j)))
        args.append(extra)
    return pl.pallas_call(
        functools.partial(_mm_kernel, mode=mode),
        grid=(m // tm, n_cols // tn),
        in_specs=in_specs,
        out_specs=pl.BlockSpec((tm, tn), lambda i, j: (i, j)),
        out_shape=jax.ShapeDtypeStruct((m, n_cols), out_dtype),
        compiler_params=_cparams(("parallel", "arbitrary")),
        name="matmul_" + mode,
    )(*args)


def _outproj_kernel(att_a, hy_a, x_a, att_b, hy_b, x_b, w_ref, o_ref, *, na):
    kw = att_a.shape[1]

    def run(att_ref, hy_ref, x_ref):
        acc = jnp.dot(att_ref[...], w_ref[:kw, :], preferred_element_type=F32)
        acc = acc + jnp.dot(hy_ref[...].astype(BF16), w_ref[kw:, :], preferred_element_type=F32)
        o_ref[...] = acc + x_ref[...]

    first = pl.program_id(0) < na
    pl.when(first)(lambda: run(att_a, hy_a, x_a))
    pl.when(jnp.logical_not(first))(lambda: run(att_b, hy_b, x_b))


def _outproj_stacked(seg_a, seg_b, w, tm=512, tn=1024):
    n_out = w.shape[1]
    na, nb = seg_a[0].shape[0] // tm, seg_b[0].shape[0] // tm
    assert seg_a[0].shape[0] % tm == 0 and seg_b[0].shape[0] % tm == 0 and n_out % tn == 0

    def specs(seg, start, count):
        att, hy, x = seg
        row = lambda i, j: (jnp.clip(i - start, 0, count - 1), 0)
        return [pl.BlockSpec((tm, att.shape[1]), row), pl.BlockSpec((tm, hy.shape[1]), row),
                pl.BlockSpec((tm, tn), lambda i, j: (jnp.clip(i - start, 0, count - 1), j))]

    return pl.pallas_call(
        functools.partial(_outproj_kernel, na=na),
        grid=(na + nb, n_out // tn),
        in_specs=specs(seg_a, 0, na) + specs(seg_b, na, nb) + [pl.BlockSpec((w.shape[0], tn), lambda i, j: (0, j))],
        out_specs=pl.BlockSpec((tm, tn), lambda i, j: (i, j)),
        out_shape=jax.ShapeDtypeStruct(((na + nb) * tm, n_out), F32),
        compiler_params=_cparams(("parallel", "arbitrary")),
        name="out_projection",
    )(*seg_a, *seg_b, w)


def _mm_nt_kernel(w_ref, a_ref, o_ref):
    o_ref[...] = lax.dot_general(w_ref[...], a_ref[...], (((1,), (1,)), ((), ())),
                                 preferred_element_type=F32).astype(o_ref.dtype)


def _matmul_nt(wt, a, out_dtype, tm=1024):
    n, k = wt.shape
    m = a.shape[0]
    tm = min(tm, m)
    return pl.pallas_call(
        _mm_nt_kernel,
        grid=(m // tm,),
        in_specs=[pl.BlockSpec((n, k), lambda i: (0, 0)), pl.BlockSpec((tm, k), lambda i: (i, 0))],
        out_specs=pl.BlockSpec((n, tm), lambda i: (0, i)),
        out_shape=jax.ShapeDtypeStruct((n, m), out_dtype),
        compiler_params=_cparams(("parallel",)),
        name="matmul_nt",
    )(wt, a)


def _alibi_columns(slopes, tq, tk):
    return [_alibi_side(slopes, tq, True), _alibi_side(slopes, tk, False)]


def _alibi_side(slopes, n, query_side):
    pos = jnp.arange(n, dtype=F32)
    val = (slopes.astype(F32) * LOG2E)[:, None] * pos[None, :]

    def pieces(x):
        p1 = x.astype(BF16)
        r1 = x - p1.astype(F32)
        p2 = r1.astype(BF16)
        p3 = (r1 - p2.astype(F32)).astype(BF16)
        return [p1, p2, p3]

    ones = [jnp.ones_like(val, BF16)] * 3
    six = jnp.stack(pieces(-val) + ones if query_side else ones + pieces(val), axis=-1)
    pad = jnp.zeros(val.shape + (ATT_QKDIM - 6,), BF16)
    return jnp.concatenate([six, pad, six, pad], axis=-1)


def _attn_kernel(slope_ref, lam_ref, q_ref, k_ref, vt_ref, aq_ref, ak_ref, g_ref, o_ref,
                 m_ref, l_ref, acc_ref, *, tq, tk, nk, hb, out_scale):
    hg = pl.program_id(1)
    i = pl.program_id(2)
    j = pl.program_id(3)

    @pl.when(j == 0)
    def _():
        m_ref[...] = jnp.full(m_ref.shape, -jnp.inf, F32)
        l_ref[...] = jnp.zeros(l_ref.shape, F32)
        acc_ref[...] = jnp.zeros(acc_ref.shape, F32)

    q_first = lax.broadcasted_iota(I32, (tq, LANES), 1) < ATT_QKDIM
    k_first = lax.broadcasted_iota(I32, (tk, LANES), 1) < ATT_QKDIM
    nt = (((1,), (1,)), ((), ()))

    def update(hh, scores, shift):
        vt = vt_ref[hh * LANES:(hh + 1) * LANES, :]
        for c, s in enumerate(scores):
            m_old = m_ref[hh, c]
            m_new = jnp.maximum(m_old, jnp.max(s, axis=0, keepdims=True) + shift)
            alpha = jnp.exp2(m_old - m_new)
            p = jnp.exp2(s - (m_new - shift))
            l_ref[hh, c] = alpha * l_ref[hh, c] + jnp.sum(p, axis=0, keepdims=True)
            acc_ref[hh, c] = alpha * acc_ref[hh, c] + jnp.dot(vt, p.astype(BF16),
                                                              preferred_element_type=F32)
            m_ref[hh, c] = m_new

    keys_before = i * tq >= (j + 1) * tk
    keys_after = (i + 1) * tq <= j * tk
    off_diagonal = jnp.logical_or(keys_before, keys_after)

    @pl.when(off_diagonal)
    def _():
        sign = jnp.where(keys_after, -1.0, 1.0).astype(BF16)
        gap = jnp.abs(i * tq - j * tk).astype(F32)
        for hh in range(hb):
            sl = slice(hh * LANES, (hh + 1) * LANES)
            q = q_ref[:, sl]
            k = k_ref[:, sl]
            aq = aq_ref[hh]
            ak = ak_ref[hh] * sign
            s0 = lax.dot_general(jnp.where(k_first, k, ak), jnp.where(q_first, q, aq), nt,
                                 preferred_element_type=F32)
            s1 = lax.dot_general(jnp.where(k_first, ak, k), jnp.where(q_first, aq, q), nt,
                                 preferred_element_type=F32)
            update(hh, (s0, s1), -(slope_ref[hg * hb + hh] * LOG2E) * gap)

    @pl.when(jnp.logical_not(off_diagonal))
    def _():
        kpos = lax.broadcasted_iota(I32, (tk, tq), 0) + j * tk
        qpos = lax.broadcasted_iota(I32, (tk, tq), 1) + i * tq
        dist = jnp.abs(kpos - qpos).astype(F32)
        for hh in range(hb):
            sl = slice(hh * LANES, (hh + 1) * LANES)
            q = q_ref[:, sl]
            k = k_ref[:, sl]
            zero = jnp.zeros_like(q)
            bias = dist * (-(slope_ref[hg * hb + hh] * LOG2E))
            s0 = lax.dot_general(k, jnp.where(q_first, q, zero), nt, preferred_element_type=F32) + bias
            s1 = lax.dot_general(k, jnp.where(q_first, zero, q), nt, preferred_element_type=F32) + bias
            update(hh, (s0, s1), 0.0)

    @pl.when(j == nk - 1)
    def _():
        for hh in range(hb):
            o = acc_ref[hh, 0] / l_ref[hh, 0] - lam_ref[0] * (acc_ref[hh, 1] / l_ref[hh, 1])
            ms = jnp.mean(o * o, axis=0, keepdims=True)
            y = o * lax.rsqrt(ms + RMS_EPS) * (g_ref[...] * out_scale)
            o_ref[:, hh * LANES:(hh + 1) * LANES] = y.T.astype(o_ref.dtype)


def _attention(qk, vt, row_off, slopes, lam, sub_g, batch, seq, out_scale, tq=512, tk=1024, hb=4):
    tq = min(tq, seq)
    tk = min(tk, seq // 4)
    nq, nk = seq // tq, seq // tk
    assert row_off % tq == 0 and row_off % tk == 0 and ATT_HEADS % hb == 0
    oq, ok = row_off // tq, row_off // tk
    ng = ATT_HEADS // hb
    aq, ak = _alibi_columns(slopes, tq, tk)
    kern = functools.partial(_attn_kernel, tq=tq, tk=tk, nk=nk, hb=hb, out_scale=out_scale)
    smem = pl.BlockSpec(memory_space=pltpu.SMEM)
    return pl.pallas_call(
        kern,
        grid=(batch, ng, nq, nk),
        in_specs=[smem, smem,
                  pl.BlockSpec((tq, hb * LANES), lambda b, h, i, j: (oq + b * nq + i, h)),
                  pl.BlockSpec((tk, hb * LANES), lambda b, h, i, j: (ok + b * nk + j, ng + h)),
                  pl.BlockSpec((hb * LANES, tk), lambda b, h, i, j: (h, ok + b * nk + j)),
                  pl.BlockSpec((hb, tq, LANES), lambda b, h, i, j: (h, 0, 0)),
                  pl.BlockSpec((hb, tk, LANES), lambda b, h, i, j: (h, 0, 0)),
                  pl.BlockSpec((LANES, 1), lambda b, h, i, j: (0, 0))],
        out_specs=pl.BlockSpec((tq, hb * LANES), lambda b, h, i, j: (b * nq + i, h)),
        out_shape=jax.ShapeDtypeStruct((batch * seq, ATT_HEADS * ATT_VDIM), BF16),
        scratch_shapes=[pltpu.VMEM((hb, 2, 1, tq), F32), pltpu.VMEM((hb, 2, 1, tq), F32),
                        pltpu.VMEM((hb, 2, LANES, tq), F32)],
        compiler_params=_cparams(("parallel", "parallel", "parallel", "arbitrary")),
        name="diff_attention",
    )(slopes, lam, qk, qk, vt, aq, ak, sub_g.reshape(LANES, 1).astype(F32))


def _shortconv_kernel(z_ref, w_ref, b_ref, o_ref):
    z = z_ref[...]
    n = z.shape[0]
    row = lax.broadcasted_iota(I32, z.shape, 0)
    prev = jnp.where(row == 0, 0.0, pltpu.roll(z, 1, axis=0))
    nxt = jnp.where(row == n - 1, 0.0, pltpu.roll(z, n - 1, axis=0))
    w = w_ref[...]
    o_ref[...] = prev * w[0:1] + z * w[1:2] + nxt * w[2:3] + b_ref[...]


def _shortconv(zh, row_off, conv_w, conv_b, batch, seq, cb=256):
    c3 = zh.shape[1]
    c = c3 // 3
    ncb = c // cb
    assert row_off % seq == 0
    ob = row_off // seq
    out = pl.pallas_call(
        _shortconv_kernel,
        grid=(batch, 3, ncb),
        in_specs=[pl.BlockSpec((seq, cb), lambda b, p, j: (ob + b, p * ncb + j)),
                  pl.BlockSpec((3, cb), lambda b, p, j: (0, p * ncb + j)),
                  pl.BlockSpec((1, cb), lambda b, p, j: (0, p * ncb + j))],
        out_specs=pl.BlockSpec((None, None, seq, cb), lambda b, p, j: (p, b, 0, j)),
        out_shape=jax.ShapeDtypeStruct((3, batch, seq, c), F32),
        compiler_params=_cparams(("parallel", "parallel", "parallel")),
        name="hyena_shortconv",
    )(zh, conv_w.astype(F32), conv_b.reshape(1, c3).astype(F32))
    return out


def _filter_kernel(z_ref, w1_ref, b1_ref, w2_ref, b2_ref, w3_ref, b3_ref, fr_ref, w4_ref,
                   t_ref, d_ref, o_ref, h_ref, *, tl):
    i = pl.program_id(0)
    g = pl.program_id(1)
    hi = lax.Precision.HIGHEST

    @pl.when(g == 0)
    def _():
        fr = fr_ref[...]
        h = jnp.sin(fr * (jnp.dot(z_ref[...], w1_ref[...], precision=hi, preferred_element_type=F32) + b1_ref[...]))
        h = jnp.sin(fr * (jnp.dot(h, w2_ref[...], precision=hi, preferred_element_type=F32) + b2_ref[...]))
        h_ref[...] = jnp.sin(fr * (jnp.dot(h, w3_ref[...], precision=hi, preferred_element_type=F32) + b3_ref[...]))

    f = jnp.dot(h_ref[...].astype(BF16), w4_ref[...], preferred_element_type=F32)
    f = f * jnp.exp(-t_ref[...] * d_ref[...])
    row = lax.broadcasted_iota(I32, f.shape, 0) + i * tl
    drop = jnp.logical_and(row == 0, g % 2 == 1)
    o_ref[...] = jnp.where(drop, 0.0, f)


def _hyena_filter_signals(seq, w1, b1, w2, b2, w3, b3, w4, freq, n_ch, tl=512):
    t = jnp.linspace(0.0, 1.0, seq, dtype=F32)[:, None]
    w = 2.0 * math.pi * jnp.arange(seq, dtype=F32)[:, None] / seq
    f = jnp.linspace(1e-4, FILTER_BANDS - 1, FILTER_BANDS, dtype=F32)[None, :]
    z = jnp.concatenate([t, jnp.cos(f * w), -jnp.sin(f * w)], axis=-1)
    deltas = jnp.abs(jnp.linspace(math.log(DECAY_FAST) / DECAY_TARGET,
                                  math.log(DECAY_SLOW) / DECAY_TARGET, n_ch, dtype=F32))[None, :]
    hid = w1.shape[1]
    emb = LANES
    z = jnp.pad(z, ((0, 0), (0, emb - z.shape[1])))
    w1 = jnp.pad(w1.astype(F32), ((0, emb - w1.shape[0]), (0, 0)))
    tl = min(tl, seq)
    full = lambda shape: pl.BlockSpec(shape, lambda i, g: tuple(0 for _ in shape))
    return pl.pallas_call(
        functools.partial(_filter_kernel, tl=tl),
        grid=(seq // tl, 2 * HYENA_ORDER),
        in_specs=[pl.BlockSpec((tl, emb), lambda i, g: (i, 0)),
                  full((emb, hid)), full((1, hid)), full((hid, hid)), full((1, hid)),
                  full((hid, hid)), full((1, hid)), full((1, hid)),
                  pl.BlockSpec((hid, n_ch), lambda i, g: (0, g)),
                  pl.BlockSpec((tl, 1), lambda i, g: (i, 0)),
                  full((1, n_ch))],
        out_specs=pl.BlockSpec((None, tl, n_ch), lambda i, g: (g, i, 0)),
        out_shape=jax.ShapeDtypeStruct((2 * HYENA_ORDER, seq, n_ch), F32),
        scratch_shapes=[pltpu.VMEM((tl, hid), F32)],
        compiler_params=_cparams(("parallel", "arbitrary")),
        name="hyena_filter_mlp",
    )(z, w1.astype(F32), b1.reshape(1, hid).astype(F32), w2.astype(F32), b2.reshape(1, hid).astype(F32),
      w3.astype(F32), b3.reshape(1, hid).astype(F32), freq.reshape(1, hid).astype(F32), w4.astype(BF16),
      t, deltas)


def _dft_tables(r):
    n = r * r
    k2 = jnp.arange(r, dtype=I32)
    n2 = jnp.arange(r // 2, dtype=I32)
    ang1 = (2.0 * math.pi / r) * ((k2[:, None] * n2[None, :]) % r).astype(F32)
    f1 = jnp.concatenate([jnp.cos(ang1), -jnp.sin(ang1)], axis=0)
    k1 = jnp.arange(r, dtype=I32)
    n1 = jnp.arange(r, dtype=I32)
    ang_a = (2.0 * math.pi / r) * ((k1[:, None] * n1[None, :]) % r).astype(F32)
    ang_b = (2.0 * math.pi / n) * (k2[:, None] * n1[None, :]).astype(F32)
    ca, sa = jnp.cos(ang_a)[None], jnp.sin(ang_a)[None]
    cb, sb = jnp.cos(ang_b), jnp.sin(ang_b)

    def blocks(cb3, sb3):
        return (ca * cb3 - sa * sb3).astype(BF16), (-(sa * cb3 + ca * sb3)).astype(BF16)

    mr, mi_ = blocks(cb[:, None, :], sb[:, None, :])
    mf = jnp.concatenate([jnp.concatenate([mr, -mi_], axis=2),
                          jnp.concatenate([mi_, mr], axis=2)], axis=1)
    mr_t, mi_t = blocks(cb[:, :, None], sb[:, :, None])
    minv = jnp.concatenate([jnp.concatenate([mr_t, mi_t], axis=2),
                            jnp.concatenate([-mi_t, mr_t], axis=2)], axis=1)
    g3 = jnp.concatenate([jnp.cos(ang1.T), -jnp.sin(ang1.T)], axis=1) * (1.0 / n)
    eye = jnp.eye(N1_BLOCK, dtype=F32)
    f1, g3 = jnp.kron(f1, eye), jnp.kron(g3, eye)
    return f1.astype(BF16), mf, minv, g3.astype(BF16)


N1_BLOCK = SUBLANES


def _pack_complex(re, im):
    rb = lax.bitcast_convert_type(re.astype(BF16).astype(F32), U32)
    ib = lax.bitcast_convert_type(im.astype(BF16).astype(F32), U32)
    return (rb >> 16) | ib


def _unpack_complex_rows(w):
    re, im = _unpack(w)
    return jnp.concatenate([re, im], axis=0).astype(BF16)


def _fft1_kernel(f_ref, x_ref, o_ref, *, r):
    n_ch = x_ref.shape[-1]
    x = x_ref[...].reshape((r // 2) * N1_BLOCK, n_ch).astype(BF16)
    res = jnp.dot(f_ref[...], x, preferred_element_type=F32)
    half = r * N1_BLOCK
    o_ref[...] = _pack_complex(res[:half], res[half:]).reshape(r, N1_BLOCK, n_ch)


def _fft_stage1(x4, part, f1, r, n_ch):
    nb = x4.shape[1]
    xv = x4.reshape(x4.shape[0], nb, r // 2, r, n_ch)
    return pl.pallas_call(
        functools.partial(_fft1_kernel, r=r),
        grid=(nb, r // N1_BLOCK),
        in_specs=[pl.BlockSpec(f1.shape, lambda b, j: (0, 0)),
                  pl.BlockSpec((None, None, r // 2, N1_BLOCK, n_ch), lambda b, j: (part, b, 0, j, 0))],
        out_specs=pl.BlockSpec((None, r, N1_BLOCK, n_ch), lambda b, j: (b, 0, j, 0)),
        out_shape=jax.ShapeDtypeStruct((nb, r, r, n_ch), U32),
        compiler_params=_cparams(("parallel", "parallel")),
        name="hyena_dft_stage1",
    )(f1, xv)


def _k2_block(r):
    return max(1, min(r, (8 * LANES) // r))


def _filter_spec_kernel(mf_ref, bf_ref, bb_ref, o_ref, *, r):
    for kk in range(mf_ref.shape[0]):
        m = mf_ref[kk]
        xf = jnp.dot(m, _unpack_complex_rows(bf_ref[kk]), preferred_element_type=F32)
        xb = jnp.dot(m, _unpack_complex_rows(bb_ref[kk]), preferred_element_type=F32)
        o_ref[kk] = _pack_complex(xf[:r] + xb[:r], xf[r:] - xb[r:])


def _filter_spectrum(b1, mf, r, n_ch):
    kb = _k2_block(r)
    blk = lambda sel: pl.BlockSpec((None, kb, r, n_ch), lambda o, k: (2 * o + sel, k, 0, 0))
    return pl.pallas_call(
        functools.partial(_filter_spec_kernel, r=r),
        grid=(HYENA_ORDER, r // kb),
        in_specs=[pl.BlockSpec((kb, 2 * r, 2 * r), lambda o, k: (k, 0, 0)), blk(0), blk(1)],
        out_specs=pl.BlockSpec((None, kb, r, n_ch), lambda o, k: (o, k, 0, 0)),
        out_shape=jax.ShapeDtypeStruct((HYENA_ORDER, r, r, n_ch), U32),
        compiler_params=_cparams(("parallel", "parallel")),
        name="hyena_filter_spectrum",
    )(mf, b1, b1)


def _fft2_kernel(mf_ref, mi_ref, b_ref, h_ref, o_ref, *, r):
    for kk in range(mf_ref.shape[0]):
        x = jnp.dot(mf_ref[kk], _unpack_complex_rows(b_ref[kk]), preferred_element_type=F32)
        xr, xi = x[:r], x[r:]
        hr, hi = _unpack(h_ref[kk])
        y = jnp.concatenate([xr * hr - xi * hi, xr * hi + xi * hr], axis=0).astype(BF16)
        c = jnp.dot(mi_ref[kk], y, preferred_element_type=F32)
        o_ref[kk] = _pack_complex(c[:r], c[r:])


def _fft_stage2(b1, h, order, mf, minv, r, n_ch):
    nb = b1.shape[0]
    kb = _k2_block(r)
    blk = pl.BlockSpec((None, kb, r, n_ch), lambda k, b: (b, k, 0, 0))
    return pl.pallas_call(
        functools.partial(_fft2_kernel, r=r),
        grid=(r // kb, nb),
        in_specs=[pl.BlockSpec((kb, 2 * r, 2 * r), lambda k, b: (k, 0, 0)),
                  pl.BlockSpec((kb, 2 * r, 2 * r), lambda k, b: (k, 0, 0)),
                  blk,
                  pl.BlockSpec((None, kb, r, n_ch), lambda k, b: (order, k, 0, 0))],
        out_specs=blk,
        out_shape=jax.ShapeDtypeStruct((nb, r, r, n_ch), U32),
        compiler_params=_cparams(("parallel", "parallel")),
        name="hyena_dft_stage2",
    )(mf, minv, b1, h)


def _fft3_kernel(g_ref, c_ref, gate_ref, s_ref, bias_ref, ng_ref, *rest, final):
    if final:
        (o_ref,) = rest
    else:
        f_ref, o_ref, b_ref = rest
    r, nb, n_ch = c_ref.shape
    cc = _unpack_complex_rows(c_ref[...].reshape(r * nb, n_ch))
    y = jnp.dot(g_ref[...], cc, preferred_element_type=F32)
    rows = (r // 2) * nb
    s_new = gate_ref[...].reshape(rows, n_ch) * (y + s_ref[...].reshape(rows, n_ch) * bias_ref[...])
    if final:
        ms = jnp.mean(s_new * s_new, axis=-1, keepdims=True)
        s_new = s_new * lax.rsqrt(ms + RMS_EPS) * ng_ref[...]
    o_ref[...] = s_new.reshape(r // 2, nb, n_ch)
    if not final:
        res = jnp.dot(f_ref[...], s_new.astype(BF16), preferred_element_type=F32)
        half = r * nb
        b_ref[...] = _pack_complex(res[:half], res[half:]).reshape(r, nb, n_ch)


def _fft_stage3(c2, g3, z4, gate_part, s4, s_part, bias, norm_g, f1, r, n_ch):
    final = f1 is None
    nb = c2.shape[0]
    zv = z4.reshape(z4.shape[0], nb, r // 2, r, n_ch)
    sv = s4.reshape(s4.shape[0], nb, r // 2, r, n_ch)
    dspec = lambda part: pl.BlockSpec((None, None, r // 2, N1_BLOCK, n_ch), lambda b, j: (part, b, 0, j, 0))
    packed = pl.BlockSpec((None, r, N1_BLOCK, n_ch), lambda b, j: (b, 0, j, 0))
    s_spec = pl.BlockSpec((None, r // 2, N1_BLOCK, n_ch), lambda b, j: (b, 0, j, 0))
    s_shape = jax.ShapeDtypeStruct((nb, r // 2, r, n_ch), F32)
    in_specs = [pl.BlockSpec(g3.shape, lambda b, j: (0, 0)), packed, dspec(gate_part), dspec(s_part),
                pl.BlockSpec((1, n_ch), lambda b, j: (0, 0)), pl.BlockSpec((1, n_ch), lambda b, j: (0, 0))]
    args = [g3, c2, zv, sv, bias.reshape(1, n_ch).astype(F32), norm_g.reshape(1, n_ch).astype(F32)]
    if final:
        out_specs, out_shape = s_spec, s_shape
    else:
        in_specs.append(pl.BlockSpec(f1.shape, lambda b, j: (0, 0)))
        args.append(f1)
        out_specs = [s_spec, packed]
        out_shape = [s_shape, jax.ShapeDtypeStruct((nb, r, r, n_ch), U32)]
    return pl.pallas_call(
        functools.partial(_fft3_kernel, final=final),
        grid=(nb, r // N1_BLOCK),
        in_specs=in_specs,
        out_specs=out_specs,
        out_shape=out_shape,
        compiler_params=_cparams(("parallel", "parallel")),
        name="hyena_dft_stage3",
    )(*args)


def _hyena(zh, row_off, batch, seq, conv_w, conv_b, filt, fft_bias, out_g):
    n_ch = zh.shape[1] // 3
    r = int(round(math.sqrt(2 * seq)))
    assert r * r == 2 * seq and r % 16 == 0
    f1, mf, minv, g3 = _dft_tables(r)
    sig = _hyena_filter_signals(seq, *filt, n_ch=n_ch)
    hb1 = _fft_stage1(sig[None], 0, f1, r, n_ch)
    h = _filter_spectrum(hb1, mf, r, n_ch)
    z4 = _shortconv(zh, row_off, conv_w, conv_b, batch, seq)
    s4, s_part = z4, 2
    b1 = _fft_stage1(s4, s_part, f1, r, n_ch)
    for o in range(HYENA_ORDER):
        c2 = _fft_stage2(b1, h, o, mf, minv, r, n_ch)
        if o < HYENA_ORDER - 1:
            s, b1 = _fft_stage3(c2, g3, z4, o, s4, s_part, fft_bias[o], out_g, f1, r, n_ch)
        else:
            s = _fft_stage3(c2, g3, z4, o, s4, s_part, fft_bias[o], out_g, None, r, n_ch)
        s4, s_part = s.reshape(1, batch, seq, n_ch), 0
    return s4.reshape(batch * seq, n_ch)


def _extract_top(s, key, count):
    vals, keys = [], []
    for _ in range(count):
        m = jnp.max(s, axis=0, keepdims=True)
        kmin = jnp.min(jnp.where(s == m, key, jnp.inf), axis=0, keepdims=True)
        s = jnp.where(key == kmin, -jnp.inf, s)
        vals.append(m)
        keys.append(kmin)
    return vals, keys


def _peer_topk_kernel(q_ref, keys_ref, eid_ref, gate_ref):
    t = q_ref.shape[0]
    nk = PEER_NKEYS
    q = q_ref[...]
    row_key = lax.broadcasted_iota(I32, (nk, t), 0).astype(F32)
    tops = []
    for c in range(2):
        s = lax.dot_general(keys_ref[c], q[:, c * nk:(c + 1) * nk], (((1,), (1,)), ((), ())),
                            preferred_element_type=F32)
        tops.append(_extract_top(s, row_key, PEER_TOPK))
    (v1, i1), (v2, i2) = tops
    rows16 = lax.broadcasted_iota(I32, (PEER_TOPK, t), 0)
    v2a = jnp.zeros((PEER_TOPK, t), F32)
    i2a = jnp.zeros((PEER_TOPK, t), F32)
    for j in range(PEER_TOPK):
        v2a = jnp.where(rows16 == j, v2[j], v2a)
        i2a = jnp.where(rows16 == j, i2[j], i2a)
    n_exp = float(nk * nk)
    half = PEER_TOPK // 2
    rows8 = lax.broadcasted_iota(I32, (half, t), 0)
    pos8 = rows8.astype(F32)
    v2h, i2h = v2a[:half], i2a[:half]
    cand = [v1[0] + v2a]
    ckey = [rows16.astype(F32) * n_exp + (i1[0] * float(nk) + i2a)]
    for i in range(1, half):
        cand.append(jnp.where(rows8 < PEER_TOPK // (i + 1), v1[i] + v2h, -jnp.inf))
        ckey.append((pos8 + float(i * PEER_TOPK)) * n_exp + (i1[i] * float(nk) + i2h))
    v1t = jnp.zeros((half, t), F32)
    i1t = jnp.zeros((half, t), F32)
    for r in range(half):
        v1t = jnp.where(rows8 == r, v1[half + r], v1t)
        i1t = jnp.where(rows8 == r, i1[half + r], i1t)
    cand.append(v1t + v2[0])
    ckey.append((pos8 + float(half)) * (PEER_TOPK * n_exp) + (i1t * float(nk) + i2[0]))
    tv, tk_ = _extract_top(jnp.concatenate(cand, axis=0), jnp.concatenate(ckey, axis=0), PEER_TOPK)
    denom = jnp.zeros((1, t), F32)
    es = []
    for k in range(PEER_TOPK):
        e = jnp.exp(tv[k] - tv[0])
        es.append(e)
        denom = denom + e
    eid = jnp.zeros((PEER_TOPK, t), F32)
    gate = jnp.zeros((PEER_TOPK, t), F32)
    for k in range(PEER_TOPK):
        pos = jnp.floor(tk_[k] * (1.0 / n_exp))
        eid = jnp.where(rows16 == k, tk_[k] - pos * n_exp, eid)
        gate = jnp.where(rows16 == k, es[k] / denom, gate)
    eid_ref[...] = eid.astype(I32)
    gate_ref[...] = gate


def _peer_topk(q, keys, tm=1024):
    t = q.shape[0]
    tm = min(tm, t)
    out_spec = pl.BlockSpec((PEER_TOPK, tm), lambda i, h: (h, i))
    return pl.pallas_call(
        _peer_topk_kernel,
        grid=(t // tm, PEER_HEADS),
        in_specs=[pl.BlockSpec((tm, 2 * PEER_NKEYS), lambda i, h: (i, h)),
                  pl.BlockSpec((None, 2, PEER_NKEYS, PEER_NKEYS), lambda i, h: (h, 0, 0, 0))],
        out_specs=[out_spec, out_spec],
        out_shape=[jax.ShapeDtypeStruct((PEER_HEADS * PEER_TOPK, t), I32),
                   jax.ShapeDtypeStruct((PEER_HEADS * PEER_TOPK, t), F32)],
        compiler_params=_cparams(("parallel", "parallel")),
        name="peer_topk",
    )(q, keys)


def _pack_table(tab):
    e, d = tab.shape
    assert d == 2 * SUBLANES * LANES
    w = lax.bitcast_convert_type(tab.astype(F32), U32)
    bits = (w + (jnp.uint32(0x7FFF) + ((w >> 16) & jnp.uint32(1)))) >> 16
    packed = bits[:, :d // 2] | (bits[:, d // 2:] << 16)
    return packed.reshape(e * SUBLANES, LANES)


def _unpack(w):
    lo = lax.bitcast_convert_type(w << 16, F32)
    hi = lax.bitcast_convert_type(w & jnp.uint32(0xFFFF0000), F32)
    return lo, hi


_BITREV8 = (0, 4, 2, 6, 1, 5, 3, 7)


def _sublane_fold8(parts):
    sub = lax.broadcasted_iota(I32, (2 * SUBLANES, LANES), 0) // 2

    def rolled(a, shift):
        return pltpu.bitcast(pltpu.roll(pltpu.bitcast(a, U32), shift, axis=0), BF16)

    lvl = [parts[_BITREV8[r]] for r in range(8)]
    for shift, mask in ((4, sub < 4), (2, (sub % 4) < 2), (1, (sub % 2) < 1)):
        nxt = []
        for a, b in zip(lvl[0::2], lvl[1::2]):
            nxt.append(jnp.where(mask, a + rolled(a, SUBLANES - shift), b + rolled(b, shift)))
        lvl = nxt
    return lvl[0]


def _pair_ranges(half):
    if half == 0:
        return (0, PEER_WINDOW), (PEER_WINDOW, PEER_PAIRS)
    return (PEER_PAIRS - PEER_WINDOW, PEER_PAIRS), (0, PEER_PAIRS - PEER_WINDOW)


def _overflow(n_low, half):
    return n_low > PEER_WINDOW if half == 0 else n_low < PEER_PAIRS - PEER_WINDOW


def _table_rows(tab_ref, off):
    return _unpack(tab_ref[pl.ds(pl.multiple_of(off, SUBLANES), SUBLANES), :])


def _peer_u_kernel(off_ref, nlow_ref, tab_ref, x_ref, o_ref, extra_ref, *, tm, half):
    lane = lax.broadcasted_iota(I32, (SUBLANES, LANES), 1)
    sub = lax.broadcasted_iota(I32, (SUBLANES, LANES), 0)
    lane_grp = lax.shift_right_logical(lane, 3)
    diag = sub == (lane & (SUBLANES - 1))
    main, rest = _pair_ranges(half)

    def folded(t, p0, p1):
        xb = pltpu.bitcast(x_ref[t], BF16)
        out = []
        for g in range(p0 // SUBLANES, p1 // SUBLANES):
            parts = []
            for r in range(SUBLANES):
                off = pl.multiple_of(off_ref[t, g * SUBLANES + r], SUBLANES)
                parts.append(pltpu.bitcast(tab_ref[pl.ds(off, SUBLANES), :], BF16) * xb)
            out.append(pltpu.bitcast(_sublane_fold8(parts), U32))
        return out

    def lane_sums(folds, p0):
        mat = jnp.zeros((SUBLANES, LANES), F32)
        for i, f in enumerate(folds):
            lo, hi = _unpack(f)
            mat = jnp.where(lane_grp == p0 // SUBLANES + i, jnp.sum(lo + hi, axis=-1, keepdims=True), mat)
        return jnp.sum(jnp.where(diag, mat, 0.0), axis=0, keepdims=True)

    def finish(t, folds):
        o_ref[pl.ds(t, 1), :] = lane_sums(folds, main[0]) + extra_ref[pl.ds(t, 1), :]

    def token(t, prev):
        cur = folded(t, *main)
        finish(jnp.maximum(t - 1, 0), prev)
        extra_ref[pl.ds(t, 1), :] = jnp.zeros((1, LANES), F32)

        @pl.when(_overflow(nlow_ref[0, t], half))
        def _():
            extra_ref[pl.ds(t, 1), :] = lane_sums(folded(t, *rest), rest[0])

        return tuple(cur)

    extra_ref[pl.ds(0, 1), :] = jnp.zeros((1, LANES), F32)
    zeros = tuple(jnp.zeros((SUBLANES, LANES), U32) for _ in range((main[1] - main[0]) // SUBLANES))
    last = lax.fori_loop(0, tm, token, zeros)
    finish(tm - 1, last)


def _peer_u(off, nlow3, tab, x4, half, tm=128):
    t = off.shape[0]
    rows = PEER_HALF * SUBLANES
    return pl.pallas_call(
        functools.partial(_peer_u_kernel, tm=tm, half=half),
        grid=(t // tm,),
        in_specs=[pl.BlockSpec((tm, LANES), lambda i: (i, 0), memory_space=pltpu.SMEM),
                  pl.BlockSpec((None, 1, tm), lambda i: (i, 0, 0), memory_space=pltpu.SMEM),
                  pl.BlockSpec((rows, LANES), lambda i: (half, 0), pipeline_mode=pl.Buffered(1)),
                  pl.BlockSpec((tm, SUBLANES, LANES), lambda i: (i, 0, 0))],
        out_specs=pl.BlockSpec((tm, LANES), lambda i: (i, 0)),
        out_shape=jax.ShapeDtypeStruct((t, LANES), F32),
        scratch_shapes=[pltpu.VMEM((tm, LANES), F32)],
        compiler_params=_cparams(("arbitrary",)),
        name="peer_expert_scores",
    )(off, nlow3, tab, x4)


def _peer_order_kernel(eid_ref, eid_o, slot_o, off_o, nlow_o):
    eid = eid_ref[...]
    lane = lax.broadcasted_iota(I32, eid.shape, 1)
    low = (eid < PEER_HALF).astype(I32)
    c_low = low
    shift = 1
    while shift < PEER_PAIRS:
        c_low = c_low + jnp.where(lane >= shift, pltpu.roll(c_low, shift, axis=1), 0)
        shift *= 2
    c_high = lane + 1 - c_low
    n_low = jnp.max(c_low, axis=1, keepdims=True)
    in_low = lane < n_low
    rank = jnp.where(in_low, lane + 1, lane + 1 - n_low)
    pos = jnp.zeros_like(lane)
    step = PEER_PAIRS // 2
    while step >= 1:
        probe = pos + (step - 1)
        count = jnp.where(in_low, jnp.take_along_axis(c_low, probe, axis=1),
                          jnp.take_along_axis(c_high, probe, axis=1))
        pos = jnp.where(count < rank, pos + step, pos)
        step //= 2
    eid_s = jnp.take_along_axis(eid, pos, axis=1)
    eid_o[...] = eid_s
    slot_o[...] = pos
    off_o[...] = (eid_s & (PEER_HALF - 1)) * SUBLANES
    nlow_o[...] = jnp.broadcast_to(n_low, eid.shape)


def _peer_order(eid, tm=1024):
    t = eid.shape[0]
    tm = min(tm, t)
    spec = pl.BlockSpec((tm, LANES), lambda i: (i, 0))
    return pl.pallas_call(
        _peer_order_kernel,
        grid=(t // tm,),
        in_specs=[spec],
        out_specs=[spec] * 4,
        out_shape=[jax.ShapeDtypeStruct((t, LANES), I32)] * 4,
        compiler_params=_cparams(("parallel",)),
        name="peer_pair_order",
    )(eid)


def _peer_coef_kernel(a0_ref, a1_ref, eid_ref, slot_ref, gate_ref, c0_ref, c1_ref):
    low = eid_ref[...] < PEER_HALF
    a = jnp.where(low, a0_ref[...], a1_ref[...])
    gate = jnp.take_along_axis(gate_ref[...], slot_ref[...], axis=1)
    coef = gate * (0.5 * a * (1.0 + lax.erf(a * (1.0 / math.sqrt(2.0)))))
    c0_ref[...] = jnp.where(low, coef, 0.0)
    c1_ref[...] = jnp.where(low, 0.0, coef)


def _peer_coef(a0, a1, eid, slot, gate, tm=1024):
    t = eid.shape[0]
    tm = min(tm, t)
    spec = pl.BlockSpec((tm, LANES), lambda i: (i, 0))
    return pl.pallas_call(
        _peer_coef_kernel,
        grid=(t // tm,),
        in_specs=[spec, spec, spec, spec, spec],
        out_specs=[spec, spec],
        out_shape=[jax.ShapeDtypeStruct((t, LANES), F32)] * 2,
        compiler_params=_cparams(("parallel",)),
        name="peer_coef",
    )(a0, a1, eid, slot, gate)


def _peer_v_kernel(off_ref, nlow_ref, coef_ref, tab_ref, base_ref, *o_refs, tm, half, n_first):
    n_acc = 4
    main, rest = _pair_ranges(half)

    def weighted(t, p0, p1):
        acc_lo = [jnp.zeros((SUBLANES, LANES), F32) for _ in range(n_acc)]
        acc_hi = [jnp.zeros((SUBLANES, LANES), F32) for _ in range(n_acc)]
        for p in range(p0, p1):
            c = coef_ref[t, p]
            lo, hi = _table_rows(tab_ref, off_ref[t, p])
            acc_lo[p % n_acc] = acc_lo[p % n_acc] + c * lo
            acc_hi[p % n_acc] = acc_hi[p % n_acc] + c * hi
        return ((acc_lo[0] + acc_lo[1]) + (acc_lo[2] + acc_lo[3]),
                (acc_hi[0] + acc_hi[1]) + (acc_hi[2] + acc_hi[3]))

    def run(o_ref):
        def store_row(t, lo, hi):
            o_ref[t, 0] = base_ref[t, 0] + lo
            o_ref[t, 1] = base_ref[t, 1] + hi

        def token(t, carry):
            lo, hi = weighted(t, *main)
            store_row(t, lo, hi)

            @pl.when(_overflow(nlow_ref[0, t], half))
            def _():
                lo2, hi2 = weighted(t, *rest)
                store_row(t, lo + lo2, hi + hi2)

            return carry

        lax.fori_loop(0, tm, token, 0)

    if len(o_refs) == 1:
        run(o_refs[0])
    else:
        first = pl.program_id(0) < n_first
        pl.when(first)(lambda: run(o_refs[0]))
        pl.when(jnp.logical_not(first))(lambda: run(o_refs[1]))


def _peer_v(off, nlow3, coef, tab, base, half, tm=128, split=None):
    t = base.shape[0]
    rows = PEER_HALF * SUBLANES
    smem = pl.BlockSpec((tm, LANES), lambda i: (i, 0), memory_space=pltpu.SMEM)
    tile = pl.BlockSpec((tm, 2, SUBLANES, LANES), lambda i: (i, 0, 0, 0))
    if split is None:
        n_first, out_specs = 0, tile
        out_shape = jax.ShapeDtypeStruct((t, 2, SUBLANES, LANES), F32)
    else:
        assert split % tm == 0
        n_first, n_rest = split // tm, (t - split) // tm
        blk = (tm, 2, SUBLANES, LANES)
        out_specs = [pl.BlockSpec(blk, lambda i: (jnp.clip(i, 0, n_first - 1), 0, 0, 0)),
                     pl.BlockSpec(blk, lambda i: (jnp.clip(i - n_first, 0, n_rest - 1), 0, 0, 0))]
        out_shape = [jax.ShapeDtypeStruct((split, 2, SUBLANES, LANES), F32),
                     jax.ShapeDtypeStruct((t - split, 2, SUBLANES, LANES), F32)]
    return pl.pallas_call(
        functools.partial(_peer_v_kernel, tm=tm, half=half, n_first=n_first),
        grid=(t // tm,),
        in_specs=[smem,
                  pl.BlockSpec((None, 1, tm), lambda i: (i, 0, 0), memory_space=pltpu.SMEM),
                  smem,
                  pl.BlockSpec((rows, LANES), lambda i: (half, 0), pipeline_mode=pl.Buffered(1)),
                  tile],
        out_specs=out_specs,
        out_shape=out_shape,
        compiler_params=_cparams(("arbitrary",)),
        name="peer_expert_sum",
    )(off, nlow3, coef, tab, base)


def _peer(x1, split, norm2_g, wq_bf, keys_bf, u_packed, v_packed, tm=128):
    t, d = x1.shape
    tm = min(tm, t)
    (xn_bf,) = _rmsnorm(x1, norm2_g, (BF16,))
    q = _matmul(xn_bf, wq_bf, wq_bf.shape[1], 0, BF16)
    eid_t, gate_t = _peer_topk(q, keys_bf)
    eid, gate = eid_t.T, gate_t.T
    eid, slot, off, nlow = _peer_order(eid)
    nlow3 = nlow[:, 0].reshape(t // tm, 1, tm)
    xw = _pack_table(xn_bf).reshape(t, SUBLANES, LANES)
    a0 = _peer_u(off, nlow3, u_packed, xw, 0, tm)
    a1 = _peer_u(off, nlow3, u_packed, xw, 1, tm)
    c0, c1 = _peer_coef(a0, a1, eid, slot, gate)
    y = _peer_v(off, nlow3, c0, v_packed, x1.reshape(t, 2, SUBLANES, LANES), 0, tm)
    ya, yb = _peer_v(off, nlow3, c1, v_packed, y, 1, tm, split=split)
    return ya.reshape(split, d), yb.reshape(t - split, d)


def kernel(x_prompt, x_sample, norm1_g, w_in, q_norm_g, k_norm_g, lambda_q1, lambda_k1, lambda_q2, lambda_k2, attn_sub_g, conv_w, conv_b, filt_w1, filt_b1, filt_w2, filt_b2, filt_w3, filt_b3, filt_w4, filt_freq, fft_bias, hyena_out_g, w_out, norm2_g, peer_wq, peer_keys, peer_u, peer_v):
    depth = w_in.shape[0]
    d_model = x_prompt.shape[-1]
    att_w = ATT_HEADS * ATT_VDIM
    shapes = [x_prompt.shape[:2], x_sample.shape[:2]]
    xs = [x_prompt.reshape(-1, d_model), x_sample.reshape(-1, d_model)]
    n0 = xs[0].shape[0]
    slopes = 2.0 ** (-8.0 * jnp.arange(1, ATT_HEADS + 1, dtype=F32) / ATT_HEADS)

    for l in range(depth):
        lambda_init = 0.8 - 0.6 * math.exp(-0.3 * l)
        lam = (jnp.exp(jnp.sum(lambda_q1[l].astype(F32) * lambda_k1[l].astype(F32)))
               - jnp.exp(jnp.sum(lambda_q2[l].astype(F32) * lambda_k2[l].astype(F32)))
               + lambda_init).reshape(1)
        w_in_bf = w_in[l].astype(BF16)
        q_gain = jnp.tile(q_norm_g[l].astype(F32), 2 * ATT_HEADS) * (ATT_QKDIM ** -0.5 * LOG2E)
        k_gain = jnp.tile(k_norm_g[l].astype(F32), 2 * ATT_HEADS)
        qk_gain = jnp.concatenate([q_gain, k_gain]).reshape(1, 2 * att_w)

        h_bf = _rmsnorm_stacked(xs[0], xs[1], norm1_g[l], BF16)
        qk = _matmul(h_bf, w_in_bf, 2 * att_w, 0, BF16, mode="qknorm", extra=qk_gain)
        vt = _matmul_nt(w_in_bf[:, 2 * att_w:3 * att_w].T, h_bf, BF16)
        zh = _matmul(h_bf, w_in_bf, w_in.shape[2] - 3 * att_w, 3 * att_w, F32)

        filt = (filt_w1[l], filt_b1[l], filt_w2[l], filt_b2[l], filt_w3[l], filt_b3[l], filt_w4[l], filt_freq[l])
        segs, row = [], 0
        for (b, s), x_seg in zip(shapes, xs):
            att = _attention(qk, vt, row, slopes, lam, attn_sub_g[l], b, s, 1.0 - lambda_init)
            hy = _hyena(zh, row, b, s, conv_w[l], conv_b[l], filt, fft_bias[l], hyena_out_g[l])
            segs.append((att, hy, x_seg))
            row += b * s
        x1 = _outproj_stacked(segs[0], segs[1], w_out[l].astype(BF16))

        xs = _peer(x1, n0, norm2_g[l], peer_wq[l].astype(BF16), peer_keys[l].astype(BF16),
                   _pack_table(peer_u[l]), _pack_table(peer_v[l]))

    return (xs[0].reshape(x_prompt.shape), xs[1].reshape(x_sample.shape))
```

```python
import functools
import math

import jax
import jax.numpy as jnp
import numpy as np
from jax import lax
from jax.experimental import pallas as pl
from jax.experimental.pallas import tpu as pltpu

F32 = jnp.float32
BF16 = jnp.bfloat16
I32 = jnp.int32
U32 = jnp.uint32

RMS_EPS = 1e-6
LOG2E = 1.4426950408889634
LANES = 128
SUBLANES = 8
VMEM_LIMIT_BYTES = 56 * 1024 * 1024

ATT_HEADS = 8
ATT_VDIM = 128
ATT_QKDIM = 64
HYENA_ORDER = 2
FILTER_BANDS = 16
DECAY_FAST = 0.3
DECAY_SLOW = 1.5
DECAY_TARGET = 1e-2
PEER_HEADS = 8
PEER_NKEYS = 128
PEER_TOPK = 16
PEER_HALF = PEER_NKEYS * PEER_NKEYS // 2
PEER_PAIRS = PEER_HEADS * PEER_TOPK
PEER_WINDOW = 80


def _cparams(sem, vmem=VMEM_LIMIT_BYTES):
    return pltpu.CompilerParams(dimension_semantics=sem, vmem_limit_bytes=vmem)


def _segment_spec(block, start, count, minor=0):
    return pl.BlockSpec(block, lambda i: (jnp.clip(i - start, 0, count - 1), minor))


def _rmsnorm_kernel(x_ref, g_ref, *o_refs):
    x = x_ref[...]
    ms = jnp.mean(x * x, axis=-1, keepdims=True)
    y = x * lax.rsqrt(ms + RMS_EPS) * g_ref[...]
    for o_ref in o_refs:
        o_ref[...] = y.astype(o_ref.dtype)


def _rmsnorm2_kernel(xa_ref, xb_ref, g_ref, o_ref, *, na):
    first = pl.program_id(0) < na
    x = jnp.where(first, xa_ref[...], xb_ref[...])
    ms = jnp.mean(x * x, axis=-1, keepdims=True)
    o_ref[...] = (x * lax.rsqrt(ms + RMS_EPS) * g_ref[...]).astype(o_ref.dtype)


def _rmsnorm_stacked(xa, xb, g, out_dtype, tm=512):
    d = xa.shape[1]
    na, nb = xa.shape[0] // tm, xb.shape[0] // tm
    assert xa.shape[0] % tm == 0 and xb.shape[0] % tm == 0
    return pl.pallas_call(
        functools.partial(_rmsnorm2_kernel, na=na),
        grid=(na + nb,),
        in_specs=[_segment_spec((tm, d), 0, na), _segment_spec((tm, d), na, nb),
                  pl.BlockSpec((1, d), lambda i: (0, 0))],
        out_specs=pl.BlockSpec((tm, d), lambda i: (i, 0)),
        out_shape=jax.ShapeDtypeStruct((xa.shape[0] + xb.shape[0], d), out_dtype),
        compiler_params=_cparams(("parallel",)),
        name="rmsnorm_stacked",
    )(xa, xb, g.reshape(1, d).astype(F32))


def _rmsnorm(x, g, out_dtypes, tm=512):
    t, d = x.shape
    tm = min(tm, t)
    spec = pl.BlockSpec((tm, d), lambda i: (i, 0))
    return pl.pallas_call(
        _rmsnorm_kernel,
        grid=(t // tm,),
        in_specs=[spec, pl.BlockSpec((1, d), lambda i: (0, 0))],
        out_specs=[spec for _ in out_dtypes],
        out_shape=[jax.ShapeDtypeStruct((t, d), dt) for dt in out_dtypes],
        compiler_params=_cparams(("parallel",)),
        name="rmsnorm",
    )(x, g.reshape(1, d).astype(F32))


def _group_rms_scale(x, gain):
    lane = lax.broadcasted_iota(I32, x.shape, 1)
    lo = lane < ATT_QKDIM
    x2 = x * x
    s_lo = jnp.sum(jnp.where(lo, x2, 0.0), axis=-1, keepdims=True)
    s_hi = jnp.sum(jnp.where(lo, 0.0, x2), axis=-1, keepdims=True)
    ms = jnp.where(lo, s_lo, s_hi) * (1.0 / ATT_QKDIM)
    return x * lax.rsqrt(ms + RMS_EPS) * gain


def _mm_kernel(a_ref, b_ref, *rest, mode):
    acc = jnp.dot(a_ref[...], b_ref[...], preferred_element_type=F32)
    if mode == "plain":
        (o_ref,) = rest
        o_ref[...] = acc.astype(o_ref.dtype)
    elif mode == "qknorm":
        g_ref, o_ref = rest
        for c in range(acc.shape[1] // LANES):
            sl = slice(c * LANES, (c + 1) * LANES)
            o_ref[:, sl] = _group_rms_scale(acc[:, sl], g_ref[:, sl]).astype(o_ref.dtype)
    else:
        raise ValueError(mode)


def _matmul(a, b, n_cols, col_off, out_dtype, mode="plain", extra=None, tm=1024, tn=1024):
    m, k = a.shape
    tm = min(tm, m)
    tn = min(tn, n_cols)
    assert col_off % tn == 0 and n_cols % tn == 0 and m % tm == 0
    off = col_off // tn
    in_specs = [pl.BlockSpec((tm, k), lambda i, j: (i, 0)),
                pl.BlockSpec((k, tn), lambda i, j: (0, j + off))]
    args = [a, b]
    if mode == "qknorm":
        in_specs.append(pl.BlockSpec((1, tn), lambda i, j: (0, j)))
        args.append(extra)
    return pl.pallas_call(
        functools.partial(_mm_kernel, mode=mode),
        grid=(m // tm, n_cols // tn),
        in_specs=in_specs,
        out_specs=pl.BlockSpec((tm, tn), lambda i, j: (i, j)),
        out_shape=jax.ShapeDtypeStruct((m, n_cols), out_dtype),
        compiler_params=_cparams(("parallel", "arbitrary")),
        name="matmul_" + mode,
    )(*args)


def _outproj_kernel(att_a, hy_a, x_a, att_b, hy_b, x_b, w_ref, o_ref, *, na):
    kw = att_a.shape[1]

    def run(att_ref, hy_ref, x_ref):
        acc = jnp.dot(att_ref[...], w_ref[:kw, :], preferred_element_type=F32)
        acc = acc + jnp.dot(hy_ref[...].astype(BF16), w_ref[kw:, :], preferred_element_type=F32)
        o_ref[...] = acc + x_ref[...]

    first = pl.program_id(0) < na
    pl.when(first)(lambda: run(att_a, hy_a, x_a))
    pl.when(jnp.logical_not(first))(lambda: run(att_b, hy_b, x_b))


def _outproj_stacked(seg_a, seg_b, w, tm=512, tn=1024):
    n_out = w.shape[1]
    na, nb = seg_a[0].shape[0] // tm, seg_b[0].shape[0] // tm
    assert seg_a[0].shape[0] % tm == 0 and seg_b[0].shape[0] % tm == 0 and n_out % tn == 0

    def specs(seg, start, count):
        att, hy, x = seg
        row = lambda i, j: (jnp.clip(i - start, 0, count - 1), 0)
        return [pl.BlockSpec((tm, att.shape[1]), row), pl.BlockSpec((tm, hy.shape[1]), row),
                pl.BlockSpec((tm, tn), lambda i, j: (jnp.clip(i - start, 0, count - 1), j))]

    return pl.pallas_call(
        functools.partial(_outproj_kernel, na=na),
        grid=(na + nb, n_out // tn),
        in_specs=specs(seg_a, 0, na) + specs(seg_b, na, nb) + [pl.BlockSpec((w.shape[0], tn), lambda i, j: (0, j))],
        out_specs=pl.BlockSpec((tm, tn), lambda i, j: (i, j)),
        out_shape=jax.ShapeDtypeStruct(((na + nb) * tm, n_out), F32),
        compiler_params=_cparams(("parallel", "arbitrary")),
        name="out_projection",
    )(*seg_a, *seg_b, w)


def _mm_nt_kernel(w_ref, a_ref, o_ref):
    o_ref[...] = lax.dot_general(w_ref[...], a_ref[...], (((1,), (1,)), ((), ())),
                                 preferred_element_type=F32).astype(o_ref.dtype)


def _matmul_nt(wt, a, out_dtype, tm=1024):
    n, k = wt.shape
    m = a.shape[0]
    tm = min(tm, m)
    return pl.pallas_call(
        _mm_nt_kernel,
        grid=(m // tm,),
        in_specs=[pl.BlockSpec((n, k), lambda i: (0, 0)), pl.BlockSpec((tm, k), lambda i: (i, 0))],
        out_specs=pl.BlockSpec((n, tm), lambda i: (0, i)),
        out_shape=jax.ShapeDtypeStruct((n, m), out_dtype),
        compiler_params=_cparams(("parallel",)),
        name="matmul_nt",
    )(wt, a)


def _alibi_columns(slopes, tq, tk):
    return [_alibi_side(slopes, tq, True), _alibi_side(slopes, tk, False)]


def _alibi_side(slopes, n, query_side):
    pos = jnp.arange(n, dtype=F32)
    val = (slopes.astype(F32) * LOG2E)[:, None] * pos[None, :]

    def pieces(x):
        p1 = x.astype(BF16)
        r1 = x - p1.astype(F32)
        p2 = r1.astype(BF16)
        p3 = (r1 - p2.astype(F32)).astype(BF16)
        return [p1, p2, p3]

    ones = [jnp.ones_like(val, BF16)] * 3
    six = jnp.stack(pieces(-val) + ones if query_side else ones + pieces(val), axis=-1)
    pad = jnp.zeros(val.shape + (ATT_QKDIM - 6,), BF16)
    return jnp.concatenate([six, pad, six, pad], axis=-1)


def _attn_kernel(slope_ref, lam_ref, q_ref, k_ref, vt_ref, aq_ref, ak_ref, g_ref, o_ref,
                 m_ref, l_ref, acc_ref, *, tq, tk, nk, hb, out_scale):
    hg = pl.program_id(1)
    i = pl.program_id(2)
    j = pl.program_id(3)

    @pl.when(j == 0)
    def _():
        m_ref[...] = jnp.full(m_ref.shape, -jnp.inf, F32)
        l_ref[...] = jnp.zeros(l_ref.shape, F32)
        acc_ref[...] = jnp.zeros(acc_ref.shape, F32)

    q_first = lax.broadcasted_iota(I32, (tq, LANES), 1) < ATT_QKDIM
    k_first = lax.broadcasted_iota(I32, (tk, LANES), 1) < ATT_QKDIM
    nt = (((1,), (1,)), ((), ()))

    def update(hh, scores, shift):
        vt = vt_ref[hh * LANES:(hh + 1) * LANES, :]
        for c, s in enumerate(scores):
            m_old = m_ref[hh, c]
            m_new = jnp.maximum(m_old, jnp.max(s, axis=0, keepdims=True) + shift)
            alpha = jnp.exp2(m_old - m_new)
            p = jnp.exp2(s - (m_new - shift))
            l_ref[hh, c] = alpha * l_ref[hh, c] + jnp.sum(p, axis=0, keepdims=True)
            acc_ref[hh, c] = alpha * acc_ref[hh, c] + jnp.dot(vt, p.astype(BF16),
                                                              preferred_element_type=F32)
            m_ref[hh, c] = m_new

    keys_before = i * tq >= (j + 1) * tk
    keys_after = (i + 1) * tq <= j * tk
    off_diagonal = jnp.logical_or(keys_before, keys_after)

    @pl.when(off_diagonal)
    def _():
        sign = jnp.where(keys_after, -1.0, 1.0).astype(BF16)
        gap = jnp.abs(i * tq - j * tk).astype(F32)
        for hh in range(hb):
            sl = slice(hh * LANES, (hh + 1) * LANES)
            q = q_ref[:, sl]
            k = k_ref[:, sl]
            aq = aq_ref[hh]
            ak = ak_ref[hh] * sign
            s0 = lax.dot_general(jnp.where(k_first, k, ak), jnp.where(q_first, q, aq), nt,
                                 preferred_element_type=F32)
            s1 = lax.dot_general(jnp.where(k_first, ak, k), jnp.where(q_first, aq, q), nt,
                                 preferred_element_type=F32)
            update(hh, (s0, s1), -(slope_ref[hg * hb + hh] * LOG2E) * gap)

    @pl.when(jnp.logical_not(off_diagonal))
    def _():
        kpos = lax.broadcasted_iota(I32, (tk, tq), 0) + j * tk
        qpos = lax.broadcasted_iota(I32, (tk, tq), 1) + i * tq
        dist = jnp.abs(kpos - qpos).astype(F32)
        for hh in range(hb):
            sl = slice(hh * LANES, (hh + 1) * LANES)
            q = q_ref[:, sl]
            k = k_ref[:, sl]
            zero = jnp.zeros_like(q)
            bias = dist * (-(slope_ref[hg * hb + hh] * LOG2E))
            s0 = lax.dot_general(k, jnp.where(q_first, q, zero), nt, preferred_element_type=F32) + bias
            s1 = lax.dot_general(k, jnp.where(q_first, zero, q), nt, preferred_element_type=F32) + bias
            update(hh, (s0, s1), 0.0)

    @pl.when(j == nk - 1)
    def _():
        for hh in range(hb):
            o = acc_ref[hh, 0] / l_ref[hh, 0] - lam_ref[0] * (acc_ref[hh, 1] / l_ref[hh, 1])
            ms = jnp.mean(o * o, axis=0, keepdims=True)
            y = o * lax.rsqrt(ms + RMS_EPS) * (g_ref[...] * out_scale)
            o_ref[:, hh * LANES:(hh + 1) * LANES] = y.T.astype(o_ref.dtype)


def _attention(qk, vt, row_off, slopes, lam, sub_g, batch, seq, out_scale, tq=512, tk=1024, hb=4):
    tq = min(tq, seq)
    tk = min(tk, seq // 4)
    nq, nk = seq // tq, seq // tk
    assert row_off % tq == 0 and row_off % tk == 0 and ATT_HEADS % hb == 0
    oq, ok = row_off // tq, row_off // tk
    ng = ATT_HEADS // hb
    aq, ak = _alibi_columns(slopes, tq, tk)
    kern = functools.partial(_attn_kernel, tq=tq, tk=tk, nk=nk, hb=hb, out_scale=out_scale)
    smem = pl.BlockSpec(memory_space=pltpu.SMEM)
    return pl.pallas_call(
        kern,
        grid=(batch, ng, nq, nk),
        in_specs=[smem, smem,
                  pl.BlockSpec((tq, hb * LANES), lambda b, h, i, j: (oq + b * nq + i, h)),
                  pl.BlockSpec((tk, hb * LANES), lambda b, h, i, j: (ok + b * nk + j, ng + h)),
                  pl.BlockSpec((hb * LANES, tk), lambda b, h, i, j: (h, ok + b * nk + j)),
                  pl.BlockSpec((hb, tq, LANES), lambda b, h, i, j: (h, 0, 0)),
                  pl.BlockSpec((hb, tk, LANES), lambda b, h, i, j: (h, 0, 0)),
                  pl.BlockSpec((LANES, 1), lambda b, h, i, j: (0, 0))],
        out_specs=pl.BlockSpec((tq, hb * LANES), lambda b, h, i, j: (b * nq + i, h)),
        out_shape=jax.ShapeDtypeStruct((batch * seq, ATT_HEADS * ATT_VDIM), BF16),
        scratch_shapes=[pltpu.VMEM((hb, 2, 1, tq), F32), pltpu.VMEM((hb, 2, 1, tq), F32),
                        pltpu.VMEM((hb, 2, LANES, tq), F32)],
        compiler_params=_cparams(("parallel", "parallel", "parallel", "arbitrary")),
        name="diff_attention",
    )(slopes, lam, qk, qk, vt, aq, ak, sub_g.reshape(LANES, 1).astype(F32))


def _shortconv_kernel(z_ref, w_ref, b_ref, o_ref):
    z = z_ref[...]
    n = z.shape[0]
    row = lax.broadcasted_iota(I32, z.shape, 0)
    prev = jnp.where(row == 0, 0.0, pltpu.roll(z, 1, axis=0))
    nxt = jnp.where(row == n - 1, 0.0, pltpu.roll(z, n - 1, axis=0))
    w = w_ref[...]
    o_ref[...] = prev * w[0:1] + z * w[1:2] + nxt * w[2:3] + b_ref[...]


def _shortconv(zh, row_off, conv_w, conv_b, batch, seq, cb=256):
    c3 = zh.shape[1]
    c = c3 // 3
    ncb = c // cb
    assert row_off % seq == 0
    ob = row_off // seq
    out = pl.pallas_call(
        _shortconv_kernel,
        grid=(batch, 3, ncb),
        in_specs=[pl.BlockSpec((seq, cb), lambda b, p, j: (ob + b, p * ncb + j)),
                  pl.BlockSpec((3, cb), lambda b, p, j: (0, p * ncb + j)),
                  pl.BlockSpec((1, cb), lambda b, p, j: (0, p * ncb + j))],
        out_specs=pl.BlockSpec((None, None, seq, cb), lambda b, p, j: (p, b, 0, j)),
        out_shape=jax.ShapeDtypeStruct((3, batch, seq, c), F32),
        compiler_params=_cparams(("parallel", "parallel", "parallel")),
        name="hyena_shortconv",
    )(zh, conv_w.astype(F32), conv_b.reshape(1, c3).astype(F32))
    return out


def _filter_kernel(z_ref, w1_ref, b1_ref, w2_ref, b2_ref, w3_ref, b3_ref, fr_ref, w4_ref,
                   t_ref, d_ref, o_ref, h_ref, *, tl):
    i = pl.program_id(0)
    g = pl.program_id(1)
    hi = lax.Precision.HIGHEST

    @pl.when(g == 0)
    def _():
        fr = fr_ref[...]
        h = jnp.sin(fr * (jnp.dot(z_ref[...], w1_ref[...], precision=hi, preferred_element_type=F32) + b1_ref[...]))
        h = jnp.sin(fr * (jnp.dot(h, w2_ref[...], precision=hi, preferred_element_type=F32) + b2_ref[...]))
        h_ref[...] = jnp.sin(fr * (jnp.dot(h, w3_ref[...], precision=hi, preferred_element_type=F32) + b3_ref[...]))

    f = jnp.dot(h_ref[...].astype(BF16), w4_ref[...], preferred_element_type=F32)
    f = f * jnp.exp(-t_ref[...] * d_ref[...])
    row = lax.broadcasted_iota(I32, f.shape, 0) + i * tl
    drop = jnp.logical_and(row == 0, g % 2 == 1)
    o_ref[...] = jnp.where(drop, 0.0, f)


def _hyena_filter_signals(seq, w1, b1, w2, b2, w3, b3, w4, freq, n_ch, tl=512):
    t = jnp.linspace(0.0, 1.0, seq, dtype=F32)[:, None]
    w = 2.0 * math.pi * jnp.arange(seq, dtype=F32)[:, None] / seq
    f = jnp.linspace(1e-4, FILTER_BANDS - 1, FILTER_BANDS, dtype=F32)[None, :]
    z = jnp.concatenate([t, jnp.cos(f * w), -jnp.sin(f * w)], axis=-1)
    deltas = jnp.abs(jnp.linspace(math.log(DECAY_FAST) / DECAY_TARGET,
                                  math.log(DECAY_SLOW) / DECAY_TARGET, n_ch, dtype=F32))[None, :]
    hid = w1.shape[1]
    emb = LANES
    z = jnp.pad(z, ((0, 0), (0, emb - z.shape[1])))
    w1 = jnp.pad(w1.astype(F32), ((0, emb - w1.shape[0]), (0, 0)))
    tl = min(tl, seq)
    full = lambda shape: pl.BlockSpec(shape, lambda i, g: tuple(0 for _ in shape))
    return pl.pallas_call(
        functools.partial(_filter_kernel, tl=tl),
        grid=(seq // tl, 2 * HYENA_ORDER),
        in_specs=[pl.BlockSpec((tl, emb), lambda i, g: (i, 0)),
                  full((emb, hid)), full((1, hid)), full((hid, hid)), full((1, hid)),
                  full((hid, hid)), full((1, hid)), full((1, hid)),
                  pl.BlockSpec((hid, n_ch), lambda i, g: (0, g)),
                  pl.BlockSpec((tl, 1), lambda i, g: (i, 0)),
                  full((1, n_ch))],
        out_specs=pl.BlockSpec((None, tl, n_ch), lambda i, g: (g, i, 0)),
        out_shape=jax.ShapeDtypeStruct((2 * HYENA_ORDER, seq, n_ch), F32),
        scratch_shapes=[pltpu.VMEM((tl, hid), F32)],
        compiler_params=_cparams(("parallel", "arbitrary")),
        name="hyena_filter_mlp",
    )(z, w1.astype(F32), b1.reshape(1, hid).astype(F32), w2.astype(F32), b2.reshape(1, hid).astype(F32),
      w3.astype(F32), b3.reshape(1, hid).astype(F32), freq.reshape(1, hid).astype(F32), w4.astype(BF16),
      t, deltas)


def _dft_tables(r):
    n = r * r
    ang1 = (2.0 * np.pi / r) * ((np.arange(r)[:, None] * np.arange(r // 2)[None, :]) % r)
    eye = np.eye(N1_BLOCK)
    f1 = jnp.asarray(np.kron(np.concatenate([np.cos(ang1), -np.sin(ang1)], axis=0), eye), BF16)
    g3 = jnp.asarray(np.kron(np.concatenate([np.cos(ang1.T), -np.sin(ang1.T)], axis=1) / n, eye), BF16)
    k2 = jnp.arange(r, dtype=I32)
    k1 = jnp.arange(r, dtype=I32)
    n1 = jnp.arange(r, dtype=I32)
    ang_a = (2.0 * math.pi / r) * ((k1[:, None] * n1[None, :]) % r).astype(F32)
    ang_b = (2.0 * math.pi / n) * (k2[:, None] * n1[None, :]).astype(F32)
    ca, sa = jnp.cos(ang_a)[None], jnp.sin(ang_a)[None]
    cb, sb = jnp.cos(ang_b), jnp.sin(ang_b)

    def blocks(cb3, sb3):
        return (ca * cb3 - sa * sb3).astype(BF16), (-(sa * cb3 + ca * sb3)).astype(BF16)

    mr, mi_ = blocks(cb[:, None, :], sb[:, None, :])
    mf = jnp.concatenate([jnp.concatenate([mr, -mi_], axis=2),
                          jnp.concatenate([mi_, mr], axis=2)], axis=1)
    mr_t, mi_t = blocks(cb[:, :, None], sb[:, :, None])
    minv = jnp.concatenate([jnp.concatenate([mr_t, mi_t], axis=2),
                            jnp.concatenate([-mi_t, mr_t], axis=2)], axis=1)
    return f1, mf, minv, g3


N1_BLOCK = SUBLANES


def _pack_complex(re, im):
    rb = lax.bitcast_convert_type(re.astype(BF16).astype(F32), U32)
    ib = lax.bitcast_convert_type(im.astype(BF16).astype(F32), U32)
    return (rb >> 16) | ib


def _unpack_complex_rows(w):
    re, im = _unpack(w)
    return jnp.concatenate([re, im], axis=0).astype(BF16)


def _fft1_kernel(f_ref, x_ref, o_ref, *, r):
    n_ch = x_ref.shape[-1]
    x = x_ref[...].reshape((r // 2) * N1_BLOCK, n_ch).astype(BF16)
    res = jnp.dot(f_ref[...], x, preferred_element_type=F32)
    half = r * N1_BLOCK
    o_ref[...] = _pack_complex(res[:half], res[half:]).reshape(r, N1_BLOCK, n_ch)


def _fft_stage1(x4, part, f1, r, n_ch):
    nb = x4.shape[1]
    xv = x4.reshape(x4.shape[0], nb, r // 2, r, n_ch)
    return pl.pallas_call(
        functools.partial(_fft1_kernel, r=r),
        grid=(nb, r // N1_BLOCK),
        in_specs=[pl.BlockSpec(f1.shape, lambda b, j: (0, 0)),
                  pl.BlockSpec((None, None, r // 2, N1_BLOCK, n_ch), lambda b, j: (part, b, 0, j, 0))],
        out_specs=pl.BlockSpec((None, r, N1_BLOCK, n_ch), lambda b, j: (b, 0, j, 0)),
        out_shape=jax.ShapeDtypeStruct((nb, r, r, n_ch), U32),
        compiler_params=_cparams(("parallel", "parallel")),
        name="hyena_dft_stage1",
    )(f1, xv)


def _k2_block(r):
    return max(1, min(r, (8 * LANES) // r))


def _filter_spec_kernel(mf_ref, bf_ref, bb_ref, o_ref, *, r):
    for kk in range(mf_ref.shape[0]):
        m = mf_ref[kk]
        xf = jnp.dot(m, _unpack_complex_rows(bf_ref[kk]), preferred_element_type=F32)
        xb = jnp.dot(m, _unpack_complex_rows(bb_ref[kk]), preferred_element_type=F32)
        o_ref[kk] = _pack_complex(xf[:r] + xb[:r], xf[r:] - xb[r:])


def _filter_spectrum(b1, mf, r, n_ch):
    kb = _k2_block(r)
    blk = lambda sel: pl.BlockSpec((None, kb, r, n_ch), lambda o, k: (2 * o + sel, k, 0, 0))
    return pl.pallas_call(
        functools.partial(_filter_spec_kernel, r=r),
        grid=(HYENA_ORDER, r // kb),
        in_specs=[pl.BlockSpec((kb, 2 * r, 2 * r), lambda o, k: (k, 0, 0)), blk(0), blk(1)],
        out_specs=pl.BlockSpec((None, kb, r, n_ch), lambda o, k: (o, k, 0, 0)),
        out_shape=jax.ShapeDtypeStruct((HYENA_ORDER, r, r, n_ch), U32),
        compiler_params=_cparams(("parallel", "parallel")),
        name="hyena_filter_spectrum",
    )(mf, b1, b1)


def _fft2_kernel(mf_ref, mi_ref, b_ref, h_ref, o_ref, *, r):
    for kk in range(mf_ref.shape[0]):
        x = jnp.dot(mf_ref[kk], _unpack_complex_rows(b_ref[kk]), preferred_element_type=F32)
        xr, xi = x[:r], x[r:]
        hr, hi = _unpack(h_ref[kk])
        y = jnp.concatenate([xr * hr - xi * hi, xr * hi + xi * hr], axis=0).astype(BF16)
        c = jnp.dot(mi_ref[kk], y, preferred_element_type=F32)
        o_ref[kk] = _pack_complex(c[:r], c[r:])


def _fft_stage2(b1, h, order, mf, minv, r, n_ch):
    nb = b1.shape[0]
    kb = _k2_block(r)
    blk = pl.BlockSpec((None, kb, r, n_ch), lambda k, b: (b, k, 0, 0))
    return pl.pallas_call(
        functools.partial(_fft2_kernel, r=r),
        grid=(r // kb, nb),
        in_specs=[pl.BlockSpec((kb, 2 * r, 2 * r), lambda k, b: (k, 0, 0)),
                  pl.BlockSpec((kb, 2 * r, 2 * r), lambda k, b: (k, 0, 0)),
                  blk,
                  pl.BlockSpec((None, kb, r, n_ch), lambda k, b: (order, k, 0, 0))],
        out_specs=blk,
        out_shape=jax.ShapeDtypeStruct((nb, r, r, n_ch), U32),
        compiler_params=_cparams(("parallel", "parallel")),
        name="hyena_dft_stage2",
    )(mf, minv, b1, h)


def _fft3_kernel(g_ref, c_ref, gate_ref, s_ref, bias_ref, ng_ref, *rest, final):
    if final:
        (o_ref,) = rest
    else:
        f_ref, o_ref, b_ref = rest
    r, nb, n_ch = c_ref.shape
    cc = _unpack_complex_rows(c_ref[...].reshape(r * nb, n_ch))
    y = jnp.dot(g_ref[...], cc, preferred_element_type=F32)
    rows = (r // 2) * nb
    s_new = gate_ref[...].reshape(rows, n_ch) * (y + s_ref[...].reshape(rows, n_ch) * bias_ref[...])
    if final:
        ms = jnp.mean(s_new * s_new, axis=-1, keepdims=True)
        s_new = s_new * lax.rsqrt(ms + RMS_EPS) * ng_ref[...]
    o_ref[...] = s_new.reshape(r // 2, nb, n_ch)
    if not final:
        res = jnp.dot(f_ref[...], s_new.astype(BF16), preferred_element_type=F32)
        half = r * nb
        b_ref[...] = _pack_complex(res[:half], res[half:]).reshape(r, nb, n_ch)


def _fft_stage3(c2, g3, z4, gate_part, s4, s_part, bias, norm_g, f1, r, n_ch):
    final = f1 is None
    nb = c2.shape[0]
    zv = z4.reshape(z4.shape[0], nb, r // 2, r, n_ch)
    sv = s4.reshape(s4.shape[0], nb, r // 2, r, n_ch)
    dspec = lambda part: pl.BlockSpec((None, None, r // 2, N1_BLOCK, n_ch), lambda b, j: (part, b, 0, j, 0))
    packed = pl.BlockSpec((None, r, N1_BLOCK, n_ch), lambda b, j: (b, 0, j, 0))
    s_spec = pl.BlockSpec((None, r // 2, N1_BLOCK, n_ch), lambda b, j: (b, 0, j, 0))
    s_shape = jax.ShapeDtypeStruct((nb, r // 2, r, n_ch), F32)
    in_specs = [pl.BlockSpec(g3.shape, lambda b, j: (0, 0)), packed, dspec(gate_part), dspec(s_part),
                pl.BlockSpec((1, n_ch), lambda b, j: (0, 0)), pl.BlockSpec((1, n_ch), lambda b, j: (0, 0))]
    args = [g3, c2, zv, sv, bias.reshape(1, n_ch).astype(F32), norm_g.reshape(1, n_ch).astype(F32)]
    if final:
        out_specs, out_shape = s_spec, s_shape
    else:
        in_specs.append(pl.BlockSpec(f1.shape, lambda b, j: (0, 0)))
        args.append(f1)
        out_specs = [s_spec, packed]
        out_shape = [s_shape, jax.ShapeDtypeStruct((nb, r, r, n_ch), U32)]
    return pl.pallas_call(
        functools.partial(_fft3_kernel, final=final),
        grid=(nb, r // N1_BLOCK),
        in_specs=in_specs,
        out_specs=out_specs,
        out_shape=out_shape,
        compiler_params=_cparams(("parallel", "parallel")),
        name="hyena_dft_stage3",
    )(*args)


def _hyena(zh, row_off, batch, seq, conv_w, conv_b, filt, fft_bias, out_g):
    n_ch = zh.shape[1] // 3
    r = int(round(math.sqrt(2 * seq)))
    assert r * r == 2 * seq and r % 16 == 0
    f1, mf, minv, g3 = _dft_tables(r)
    sig = _hyena_filter_signals(seq, *filt, n_ch=n_ch)
    hb1 = _fft_stage1(sig[None], 0, f1, r, n_ch)
    h = _filter_spectrum(hb1, mf, r, n_ch)
    z4 = _shortconv(zh, row_off, conv_w, conv_b, batch, seq)
    s4, s_part = z4, 2
    b1 = _fft_stage1(s4, s_part, f1, r, n_ch)
    for o in range(HYENA_ORDER):
        c2 = _fft_stage2(b1, h, o, mf, minv, r, n_ch)
        if o < HYENA_ORDER - 1:
            s, b1 = _fft_stage3(c2, g3, z4, o, s4, s_part, fft_bias[o], out_g, f1, r, n_ch)
        else:
            s = _fft_stage3(c2, g3, z4, o, s4, s_part, fft_bias[o], out_g, None, r, n_ch)
        s4, s_part = s.reshape(1, batch, seq, n_ch), 0
    return s4.reshape(batch * seq, n_ch)


def _extract_top(s, key, count):
    vals, keys = [], []
    for _ in range(count):
        m = jnp.max(s, axis=0, keepdims=True)
        kmin = jnp.min(jnp.where(s == m, key, jnp.inf), axis=0, keepdims=True)
        s = jnp.where(key == kmin, -jnp.inf, s)
        vals.append(m)
        keys.append(kmin)
    return vals, keys


def _peer_topk_kernel(q_ref, keys_ref, eid_ref, gate_ref):
    t = q_ref.shape[0]
    nk = PEER_NKEYS
    q = q_ref[...]
    row_key = lax.broadcasted_iota(I32, (nk, t), 0).astype(F32)
    tops = []
    for c in range(2):
        s = lax.dot_general(keys_ref[c], q[:, c * nk:(c + 1) * nk], (((1,), (1,)), ((), ())),
                            preferred_element_type=F32)
        tops.append(_extract_top(s, row_key, PEER_TOPK))
    (v1, i1), (v2, i2) = tops
    rows16 = lax.broadcasted_iota(I32, (PEER_TOPK, t), 0)
    v2a = jnp.zeros((PEER_TOPK, t), F32)
    i2a = jnp.zeros((PEER_TOPK, t), F32)
    for j in range(PEER_TOPK):
        v2a = jnp.where(rows16 == j, v2[j], v2a)
        i2a = jnp.where(rows16 == j, i2[j], i2a)
    n_exp = float(nk * nk)
    half = PEER_TOPK // 2
    rows8 = lax.broadcasted_iota(I32, (half, t), 0)
    pos8 = rows8.astype(F32)
    v2h, i2h = v2a[:half], i2a[:half]
    cand = [v1[0] + v2a]
    ckey = [rows16.astype(F32) * n_exp + (i1[0] * float(nk) + i2a)]
    for i in range(1, half):
        cand.append(jnp.where(rows8 < PEER_TOPK // (i + 1), v1[i] + v2h, -jnp.inf))
        ckey.append((pos8 + float(i * PEER_TOPK)) * n_exp + (i1[i] * float(nk) + i2h))
    v1t = jnp.zeros((half, t), F32)
    i1t = jnp.zeros((half, t), F32)
    for r in range(half):
        v1t = jnp.where(rows8 == r, v1[half + r], v1t)
        i1t = jnp.where(rows8 == r, i1[half + r], i1t)
    cand.append(v1t + v2[0])
    ckey.append((pos8 + float(half)) * (PEER_TOPK * n_exp) + (i1t * float(nk) + i2[0]))
    tv, tk_ = _extract_top(jnp.concatenate(cand, axis=0), jnp.concatenate(ckey, axis=0), PEER_TOPK)
    denom = jnp.zeros((1, t), F32)
    es = []
    for k in range(PEER_TOPK):
        e = jnp.exp(tv[k] - tv[0])
        es.append(e)
        denom = denom + e
    eid = jnp.zeros((PEER_TOPK, t), F32)
    gate = jnp.zeros((PEER_TOPK, t), F32)
    for k in range(PEER_TOPK):
        pos = jnp.floor(tk_[k] * (1.0 / n_exp))
        eid = jnp.where(rows16 == k, tk_[k] - pos * n_exp, eid)
        gate = jnp.where(rows16 == k, es[k] / denom, gate)
    eid_ref[...] = eid.astype(I32)
    gate_ref[...] = gate


def _peer_topk(q, keys, tm=1024):
    t = q.shape[0]
    tm = min(tm, t)
    out_spec = pl.BlockSpec((PEER_TOPK, tm), lambda i, h: (h, i))
    return pl.pallas_call(
        _peer_topk_kernel,
        grid=(t // tm, PEER_HEADS),
        in_specs=[pl.BlockSpec((tm, 2 * PEER_NKEYS), lambda i, h: (i, h)),
                  pl.BlockSpec((None, 2, PEER_NKEYS, PEER_NKEYS), lambda i, h: (h, 0, 0, 0))],
        out_specs=[out_spec, out_spec],
        out_shape=[jax.ShapeDtypeStruct((PEER_HEADS * PEER_TOPK, t), I32),
                   jax.ShapeDtypeStruct((PEER_HEADS * PEER_TOPK, t), F32)],
        compiler_params=_cparams(("parallel", "parallel")),
        name="peer_topk",
    )(q, keys)


def _pack_table(tab):
    e, d = tab.shape
    assert d == 2 * SUBLANES * LANES
    def bf16_bits(x):
        w = lax.bitcast_convert_type(x.astype(F32), U32)
        return (w + (jnp.uint32(0x7FFF) + ((w >> 16) & jnp.uint32(1)))) >> 16

    packed = bf16_bits(tab[:, :d // 2]) | (bf16_bits(tab[:, d // 2:]) << 16)
    return packed.reshape(e * SUBLANES, LANES)


def _unpack(w):
    lo = lax.bitcast_convert_type(w << 16, F32)
    hi = lax.bitcast_convert_type(w & jnp.uint32(0xFFFF0000), F32)
    return lo, hi


_BITREV8 = (0, 4, 2, 6, 1, 5, 3, 7)


def _sublane_fold8(parts):
    sub = lax.broadcasted_iota(I32, (2 * SUBLANES, LANES), 0) // 2

    def rolled(a, shift):
        return pltpu.bitcast(pltpu.roll(pltpu.bitcast(a, U32), shift, axis=0), BF16)

    lvl = [parts[_BITREV8[r]] for r in range(8)]
    for shift, mask in ((4, sub < 4), (2, (sub % 4) < 2), (1, (sub % 2) < 1)):
        nxt = []
        for a, b in zip(lvl[0::2], lvl[1::2]):
            nxt.append(jnp.where(mask, a + rolled(a, SUBLANES - shift), b + rolled(b, shift)))
        lvl = nxt
    return lvl[0]


def _pair_ranges(half):
    if half == 0:
        return (0, PEER_WINDOW), (PEER_WINDOW, PEER_PAIRS)
    return (PEER_PAIRS - PEER_WINDOW, PEER_PAIRS), (0, PEER_PAIRS - PEER_WINDOW)


def _overflow(n_low, half):
    return n_low > PEER_WINDOW if half == 0 else n_low < PEER_PAIRS - PEER_WINDOW


def _table_rows(tab_ref, off):
    return _unpack(tab_ref[pl.ds(pl.multiple_of(off, SUBLANES), SUBLANES), :])


def _peer_u_kernel(off_ref, nlow_ref, tab_ref, x_ref, o_ref, extra_ref, *, tm, half):
    lane = lax.broadcasted_iota(I32, (SUBLANES, LANES), 1)
    sub = lax.broadcasted_iota(I32, (SUBLANES, LANES), 0)
    lane_grp = lax.shift_right_logical(lane, 3)
    diag = sub == (lane & (SUBLANES - 1))
    main, rest = _pair_ranges(half)

    def folded(t, p0, p1):
        xb = pltpu.bitcast(x_ref[t], BF16)
        out = []
        for g in range(p0 // SUBLANES, p1 // SUBLANES):
            parts = []
            for r in range(SUBLANES):
                off = pl.multiple_of(off_ref[t, g * SUBLANES + r], SUBLANES)
                parts.append(pltpu.bitcast(tab_ref[pl.ds(off, SUBLANES), :], BF16) * xb)
            out.append(pltpu.bitcast(_sublane_fold8(parts), U32))
        return out

    def lane_sums(folds, p0):
        mat = jnp.zeros((SUBLANES, LANES), F32)
        for i, f in enumerate(folds):
            lo, hi = _unpack(f)
            mat = jnp.where(lane_grp == p0 // SUBLANES + i, jnp.sum(lo + hi, axis=-1, keepdims=True), mat)
        return jnp.sum(jnp.where(diag, mat, 0.0), axis=0, keepdims=True)

    def finish(t, folds):
        o_ref[pl.ds(t, 1), :] = lane_sums(folds, main[0]) + extra_ref[pl.ds(t, 1), :]

    def token(t, prev):
        cur = folded(t, *main)
        finish(jnp.maximum(t - 1, 0), prev)
        extra_ref[pl.ds(t, 1), :] = jnp.zeros((1, LANES), F32)

        @pl.when(_overflow(nlow_ref[0, t], half))
        def _():
            extra_ref[pl.ds(t, 1), :] = lane_sums(folded(t, *rest), rest[0])

        return tuple(cur)

    extra_ref[pl.ds(0, 1), :] = jnp.zeros((1, LANES), F32)
    zeros = tuple(jnp.zeros((SUBLANES, LANES), U32) for _ in range((main[1] - main[0]) // SUBLANES))
    last = lax.fori_loop(0, tm, token, zeros)
    finish(tm - 1, last)


def _peer_u(off, nlow3, tab, x4, half, tm=128):
    t = off.shape[0]
    rows = PEER_HALF * SUBLANES
    return pl.pallas_call(
        functools.partial(_peer_u_kernel, tm=tm, half=half),
        grid=(t // tm,),
        in_specs=[pl.BlockSpec((tm, LANES), lambda i: (i, 0), memory_space=pltpu.SMEM),
                  pl.BlockSpec((None, 1, tm), lambda i: (i, 0, 0), memory_space=pltpu.SMEM),
                  pl.BlockSpec((rows, LANES), lambda i: (half, 0), pipeline_mode=pl.Buffered(1)),
                  pl.BlockSpec((tm, SUBLANES, LANES), lambda i: (i, 0, 0))],
        out_specs=pl.BlockSpec((tm, LANES), lambda i: (i, 0)),
        out_shape=jax.ShapeDtypeStruct((t, LANES), F32),
        scratch_shapes=[pltpu.VMEM((tm, LANES), F32)],
        compiler_params=_cparams(("arbitrary",)),
        name="peer_expert_scores",
    )(off, nlow3, tab, x4)


def _peer_order_kernel(eid_ref, eid_o, slot_o, off_o, nlow_o):
    eid = eid_ref[...].T
    lane = lax.broadcasted_iota(I32, eid.shape, 1)
    low = (eid < PEER_HALF).astype(I32)
    c_low = low
    shift = 1
    while shift < PEER_PAIRS:
        c_low = c_low + jnp.where(lane >= shift, pltpu.roll(c_low, shift, axis=1), 0)
        shift *= 2
    c_high = lane + 1 - c_low
    n_low = jnp.max(c_low, axis=1, keepdims=True)
    in_low = lane < n_low
    rank = jnp.where(in_low, lane + 1, lane + 1 - n_low)
    pos = jnp.zeros_like(lane)
    step = PEER_PAIRS // 2
    while step >= 1:
        probe = pos + (step - 1)
        count = jnp.where(in_low, jnp.take_along_axis(c_low, probe, axis=1),
                          jnp.take_along_axis(c_high, probe, axis=1))
        pos = jnp.where(count < rank, pos + step, pos)
        step //= 2
    eid_s = jnp.take_along_axis(eid, pos, axis=1)
    eid_o[...] = eid_s
    slot_o[...] = pos
    off_o[...] = (eid_s & (PEER_HALF - 1)) * SUBLANES
    nlow_o[...] = jnp.broadcast_to(n_low, eid.shape)


def _peer_order(eid_t, tm=1024):
    t = eid_t.shape[1]
    tm = min(tm, t)
    spec = pl.BlockSpec((tm, LANES), lambda i: (i, 0))
    return pl.pallas_call(
        _peer_order_kernel,
        grid=(t // tm,),
        in_specs=[pl.BlockSpec((LANES, tm), lambda i: (0, i))],
        out_specs=[spec] * 4,
        out_shape=[jax.ShapeDtypeStruct((t, LANES), I32)] * 4,
        compiler_params=_cparams(("parallel",)),
        name="peer_pair_order",
    )(eid_t)


def _peer_coef_kernel(a0_ref, a1_ref, eid_ref, slot_ref, gate_ref, c0_ref, c1_ref):
    low = eid_ref[...] < PEER_HALF
    a = jnp.where(low, a0_ref[...], a1_ref[...])
    gate = jnp.take_along_axis(gate_ref[...].T, slot_ref[...], axis=1)
    coef = gate * (0.5 * a * (1.0 + lax.erf(a * (1.0 / math.sqrt(2.0)))))
    c0_ref[...] = jnp.where(low, coef, 0.0)
    c1_ref[...] = jnp.where(low, 0.0, coef)


def _peer_coef(a0, a1, eid, slot, gate_t, tm=1024):
    t = eid.shape[0]
    tm = min(tm, t)
    spec = pl.BlockSpec((tm, LANES), lambda i: (i, 0))
    return pl.pallas_call(
        _peer_coef_kernel,
        grid=(t // tm,),
        in_specs=[spec, spec, spec, spec, pl.BlockSpec((LANES, tm), lambda i: (0, i))],
        out_specs=[spec, spec],
        out_shape=[jax.ShapeDtypeStruct((t, LANES), F32)] * 2,
        compiler_params=_cparams(("parallel",)),
        name="peer_coef",
    )(a0, a1, eid, slot, gate_t)


def _peer_v_kernel(off_ref, nlow_ref, coef_ref, tab_ref, base_ref, *o_refs, tm, half, n_first):
    n_acc = 4
    main, rest = _pair_ranges(half)

    def weighted(t, p0, p1):
        acc_lo = [jnp.zeros((SUBLANES, LANES), F32) for _ in range(n_acc)]
        acc_hi = [jnp.zeros((SUBLANES, LANES), F32) for _ in range(n_acc)]
        for p in range(p0, p1):
            c = coef_ref[t, p]
            lo, hi = _table_rows(tab_ref, off_ref[t, p])
            acc_lo[p % n_acc] = acc_lo[p % n_acc] + c * lo
            acc_hi[p % n_acc] = acc_hi[p % n_acc] + c * hi
        return ((acc_lo[0] + acc_lo[1]) + (acc_lo[2] + acc_lo[3]),
                (acc_hi[0] + acc_hi[1]) + (acc_hi[2] + acc_hi[3]))

    def run(o_ref):
        def store_row(t, lo, hi):
            o_ref[t, 0] = base_ref[t, 0] + lo
            o_ref[t, 1] = base_ref[t, 1] + hi

        def token(t, carry):
            lo, hi = weighted(t, *main)
            store_row(t, lo, hi)

            @pl.when(_overflow(nlow_ref[0, t], half))
            def _():
                lo2, hi2 = weighted(t, *rest)
                store_row(t, lo + lo2, hi + hi2)

            return carry

        lax.fori_loop(0, tm, token, 0)

    if len(o_refs) == 1:
        run(o_refs[0])
    else:
        first = pl.program_id(0) < n_first
        pl.when(first)(lambda: run(o_refs[0]))
        pl.when(jnp.logical_not(first))(lambda: run(o_refs[1]))


def _peer_v(off, nlow3, coef, tab, base, half, tm=128, split=None):
    t = base.shape[0]
    rows = PEER_HALF * SUBLANES
    smem = pl.BlockSpec((tm, LANES), lambda i: (i, 0), memory_space=pltpu.SMEM)
    tile = pl.BlockSpec((tm, 2, SUBLANES, LANES), lambda i: (i, 0, 0, 0))
    if split is None:
        n_first, out_specs = 0, tile
        out_shape = jax.ShapeDtypeStruct((t, 2, SUBLANES, LANES), F32)
    else:
        assert split % tm == 0
        n_first, n_rest = split // tm, (t - split) // tm
        blk = (tm, 2, SUBLANES, LANES)
        out_specs = [pl.BlockSpec(blk, lambda i: (jnp.clip(i, 0, n_first - 1), 0, 0, 0)),
                     pl.BlockSpec(blk, lambda i: (jnp.clip(i - n_first, 0, n_rest - 1), 0, 0, 0))]
        out_shape = [jax.ShapeDtypeStruct((split, 2, SUBLANES, LANES), F32),
                     jax.ShapeDtypeStruct((t - split, 2, SUBLANES, LANES), F32)]
    return pl.pallas_call(
        functools.partial(_peer_v_kernel, tm=tm, half=half, n_first=n_first),
        grid=(t // tm,),
        in_specs=[smem,
                  pl.BlockSpec((None, 1, tm), lambda i: (i, 0, 0), memory_space=pltpu.SMEM),
                  smem,
                  pl.BlockSpec((rows, LANES), lambda i: (half, 0), pipeline_mode=pl.Buffered(1)),
                  tile],
        out_specs=out_specs,
        out_shape=out_shape,
        compiler_params=_cparams(("arbitrary",)),
        name="peer_expert_sum",
    )(off, nlow3, coef, tab, base)


def _peer(x1, split, norm2_g, wq_bf, keys_bf, u_packed, v_packed, tm=128):
    t, d = x1.shape
    tm = min(tm, t)
    (xn_bf,) = _rmsnorm(x1, norm2_g, (BF16,))
    q = _matmul(xn_bf, wq_bf, wq_bf.shape[1], 0, BF16)
    eid_t, gate_t = _peer_topk(q, keys_bf)
    eid, slot, off, nlow = _peer_order(eid_t)
    nlow3 = nlow[:, 0].reshape(t // tm, 1, tm)
    xw = _pack_table(xn_bf).reshape(t, SUBLANES, LANES)
    a0 = _peer_u(off, nlow3, u_packed, xw, 0, tm)
    a1 = _peer_u(off, nlow3, u_packed, xw, 1, tm)
    c0, c1 = _peer_coef(a0, a1, eid, slot, gate_t)
    y = _peer_v(off, nlow3, c0, v_packed, x1.reshape(t, 2, SUBLANES, LANES), 0, tm)
    ya, yb = _peer_v(off, nlow3, c1, v_packed, y, 1, tm, split=split)
    return ya.reshape(split, d), yb.reshape(t - split, d)


def kernel(x_prompt, x_sample, norm1_g, w_in, q_norm_g, k_norm_g, lambda_q1, lambda_k1, lambda_q2, lambda_k2, attn_sub_g, conv_w, conv_b, filt_w1, filt_b1, filt_w2, filt_b2, filt_w3, filt_b3, filt_w4, filt_freq, fft_bias, hyena_out_g, w_out, norm2_g, peer_wq, peer_keys, peer_u, peer_v):
    depth = w_in.shape[0]
    d_model = x_prompt.shape[-1]
    att_w = ATT_HEADS * ATT_VDIM
    shapes = [x_prompt.shape[:2], x_sample.shape[:2]]
    xs = [x_prompt.reshape(-1, d_model), x_sample.reshape(-1, d_model)]
    n0 = xs[0].shape[0]
    slopes = 2.0 ** (-8.0 * jnp.arange(1, ATT_HEADS + 1, dtype=F32) / ATT_HEADS)

    for l in range(depth):
        lambda_init = 0.8 - 0.6 * math.exp(-0.3 * l)
        lam = (jnp.exp(jnp.sum(lambda_q1[l].astype(F32) * lambda_k1[l].astype(F32)))
               - jnp.exp(jnp.sum(lambda_q2[l].astype(F32) * lambda_k2[l].astype(F32)))
               + lambda_init).reshape(1)
        w_in_bf = w_in[l].astype(BF16)
        q_gain = jnp.tile(q_norm_g[l].astype(F32), 2 * ATT_HEADS) * (ATT_QKDIM ** -0.5 * LOG2E)
        k_gain = jnp.tile(k_norm_g[l].astype(F32), 2 * ATT_HEADS)
        qk_gain = jnp.concatenate([q_gain, k_gain]).reshape(1, 2 * att_w)

        h_bf = _rmsnorm_stacked(xs[0], xs[1], norm1_g[l], BF16)
        qk = _matmul(h_bf, w_in_bf, 2 * att_w, 0, BF16, mode="qknorm", extra=qk_gain)
        vt = _matmul_nt(w_in_bf[:, 2 * att_w:3 * att_w].T, h_bf, BF16)
        zh = _matmul(h_bf, w_in_bf, w_in.shape[2] - 3 * att_w, 3 * att_w, F32)

        filt = (filt_w1[l], filt_b1[l], filt_w2[l], filt_b2[l], filt_w3[l], filt_b3[l], filt_w4[l], filt_freq[l])
        segs, row = [], 0
        for (b, s), x_seg in zip(shapes, xs):
            att = _attention(qk, vt, row, slopes, lam, attn_sub_g[l], b, s, 1.0 - lambda_init)
            hy = _hyena(zh, row, b, s, conv_w[l], conv_b[l], filt, fft_bias[l], hyena_out_g[l])
            segs.append((att, hy, x_seg))
            row += b * s
        x1 = _outproj_stacked(segs[0], segs[1], w_out[l].astype(BF16))

        xs = _peer(x1, n0, norm2_g[l], peer_wq[l].astype(BF16), peer_keys[l].astype(BF16),
                   _pack_table(peer_u[l]), _pack_table(peer_v[l]))

    return (xs[0].reshape(x_prompt.shape), xs[1].reshape(x_sample.shape))
```

```python
import functools
import math

import jax
import jax.numpy as jnp
import numpy as np
from jax import lax
from jax.experimental import pallas as pl
from jax.experimental.pallas import tpu as pltpu

F32 = jnp.float32
BF16 = jnp.bfloat16
I32 = jnp.int32
U32 = jnp.uint32

RMS_EPS = 1e-6
LOG2E = 1.4426950408889634
LANES = 128
SUBLANES = 8
VMEM_LIMIT_BYTES = 56 * 1024 * 1024

ATT_HEADS = 8
ATT_VDIM = 128
ATT_QKDIM = 64
HYENA_ORDER = 2
FILTER_BANDS = 16
DECAY_FAST = 0.3
DECAY_SLOW = 1.5
DECAY_TARGET = 1e-2
PEER_HEADS = 8
PEER_NKEYS = 128
PEER_TOPK = 16
PEER_HALF = PEER_NKEYS * PEER_NKEYS // 2
PEER_PAIRS = PEER_HEADS * PEER_TOPK
PEER_WINDOW = 80


def _cparams(sem, vmem=VMEM_LIMIT_BYTES):
    return pltpu.CompilerParams(dimension_semantics=sem, vmem_limit_bytes=vmem)


def _segment_spec(block, start, count, minor=0):
    return pl.BlockSpec(block, lambda i: (jnp.clip(i - start, 0, count - 1), minor))


def _rmsnorm_kernel(x_ref, g_ref, *o_refs):
    x = x_ref[...]
    ms = jnp.mean(x * x, axis=-1, keepdims=True)
    y = x * lax.rsqrt(ms + RMS_EPS) * g_ref[...]
    for o_ref in o_refs:
        o_ref[...] = y.astype(o_ref.dtype)


def _rmsnorm2_kernel(xa_ref, xb_ref, g_ref, o_ref, *, na):
    first = pl.program_id(0) < na
    x = jnp.where(first, xa_ref[...], xb_ref[...])
    ms = jnp.mean(x * x, axis=-1, keepdims=True)
    o_ref[...] = (x * lax.rsqrt(ms + RMS_EPS) * g_ref[...]).astype(o_ref.dtype)


def _rmsnorm_stacked(xa, xb, g, out_dtype, tm=512):
    d = xa.shape[1]
    na, nb = xa.shape[0] // tm, xb.shape[0] // tm
    assert xa.shape[0] % tm == 0 and xb.shape[0] % tm == 0
    return pl.pallas_call(
        functools.partial(_rmsnorm2_kernel, na=na),
        grid=(na + nb,),
        in_specs=[_segment_spec((tm, d), 0, na), _segment_spec((tm, d), na, nb),
                  pl.BlockSpec((1, d), lambda i: (0, 0))],
        out_specs=pl.BlockSpec((tm, d), lambda i: (i, 0)),
        out_shape=jax.ShapeDtypeStruct((xa.shape[0] + xb.shape[0], d), out_dtype),
        compiler_params=_cparams(("parallel",)),
        name="rmsnorm_stacked",
    )(xa, xb, g.reshape(1, d).astype(F32))


def _rmsnorm(x, g, out_dtypes, tm=512):
    t, d = x.shape
    tm = min(tm, t)
    spec = pl.BlockSpec((tm, d), lambda i: (i, 0))
    return pl.pallas_call(
        _rmsnorm_kernel,
        grid=(t // tm,),
        in_specs=[spec, pl.BlockSpec((1, d), lambda i: (0, 0))],
        out_specs=[spec for _ in out_dtypes],
        out_shape=[jax.ShapeDtypeStruct((t, d), dt) for dt in out_dtypes],
        compiler_params=_cparams(("parallel",)),
        name="rmsnorm",
    )(x, g.reshape(1, d).astype(F32))


def _group_rms_scale(x, gain):
    lane = lax.broadcasted_iota(I32, x.shape, 1)
    lo = lane < ATT_QKDIM
    x2 = x * x
    s_lo = jnp.sum(jnp.where(lo, x2, 0.0), axis=-1, keepdims=True)
    s_hi = jnp.sum(jnp.where(lo, 0.0, x2), axis=-1, keepdims=True)
    ms = jnp.where(lo, s_lo, s_hi) * (1.0 / ATT_QKDIM)
    return x * lax.rsqrt(ms + RMS_EPS) * gain


def _mm_kernel(a_ref, b_ref, *rest, mode):
    acc = jnp.dot(a_ref[...], b_ref[...], preferred_element_type=F32)
    if mode == "plain":
        (o_ref,) = rest
        o_ref[...] = acc.astype(o_ref.dtype)
    elif mode == "qknorm":
        g_ref, o_ref = rest
        for c in range(acc.shape[1] // LANES):
            sl = slice(c * LANES, (c + 1) * LANES)
            o_ref[:, sl] = _group_rms_scale(acc[:, sl], g_ref[:, sl]).astype(o_ref.dtype)
    else:
        raise ValueError(mode)


def _matmul(a, b, n_cols, col_off, out_dtype, mode="plain", extra=None, tm=1024, tn=1024):
    m, k = a.shape
    tm = min(tm, m)
    tn = min(tn, n_cols)
    assert col_off % tn == 0 and n_cols % tn == 0 and m % tm == 0
    off = col_off // tn
    in_specs = [pl.BlockSpec((tm, k), lambda i, j: (i, 0)),
                pl.BlockSpec((k, tn), lambda i, j: (0, j + off))]
    args = [a, b]
    if mode == "qknorm":
        in_specs.append(pl.BlockSpec((1, tn), lambda i, j: (0, j)))
        args.append(extra)
    return pl.pallas_call(
        functools.partial(_mm_kernel, mode=mode),
        grid=(m // tm, n_cols // tn),
        in_specs=in_specs,
        out_specs=pl.BlockSpec((tm, tn), lambda i, j: (i, j)),
        out_shape=jax.ShapeDtypeStruct((m, n_cols), out_dtype),
        compiler_params=_cparams(("parallel", "arbitrary")),
        name="matmul_" + mode,
    )(*args)


def _outproj_kernel(att_a, hy_a, x_a, att_b, hy_b, x_b, w_ref, o_ref, *, na):
    kw = att_a.shape[1]

    def run(att_ref, hy_ref, x_ref):
        acc = jnp.dot(att_ref[...], w_ref[:kw, :], preferred_element_type=F32)
        acc = acc + jnp.dot(hy_ref[...].astype(BF16), w_ref[kw:, :], preferred_element_type=F32)
        o_ref[...] = acc + x_ref[...]

    first = pl.program_id(0) < na
    pl.when(first)(lambda: run(att_a, hy_a, x_a))
    pl.when(jnp.logical_not(first))(lambda: run(att_b, hy_b, x_b))


def _outproj_stacked(seg_a, seg_b, w, tm=512, tn=1024):
    n_out = w.shape[1]
    na, nb = seg_a[0].shape[0] // tm, seg_b[0].shape[0] // tm
    assert seg_a[0].shape[0] % tm == 0 and seg_b[0].shape[0] % tm == 0 and n_out % tn == 0

    def specs(seg, start, count):
        att, hy, x = seg
        row = lambda i, j: (jnp.clip(i - start, 0, count - 1), 0)
        return [pl.BlockSpec((tm, att.shape[1]), row), pl.BlockSpec((tm, hy.shape[1]), row),
                pl.BlockSpec((tm, tn), lambda i, j: (jnp.clip(i - start, 0, count - 1), j))]

    return pl.pallas_call(
        functools.partial(_outproj_kernel, na=na),
        grid=(na + nb, n_out // tn),
        in_specs=specs(seg_a, 0, na) + specs(seg_b, na, nb) + [pl.BlockSpec((w.shape[0], tn), lambda i, j: (0, j))],
        out_specs=pl.BlockSpec((tm, tn), lambda i, j: (i, j)),
        out_shape=jax.ShapeDtypeStruct(((na + nb) * tm, n_out), F32),
        compiler_params=_cparams(("parallel", "arbitrary")),
        name="out_projection",
    )(*seg_a, *seg_b, w)


def _mm_nt_kernel(w_ref, a_ref, o_ref):
    o_ref[...] = lax.dot_general(w_ref[...], a_ref[...], (((1,), (1,)), ((), ())),
                                 preferred_element_type=F32).astype(o_ref.dtype)


def _matmul_nt(wt, a, out_dtype, tm=1024):
    n, k = wt.shape
    m = a.shape[0]
    tm = min(tm, m)
    return pl.pallas_call(
        _mm_nt_kernel,
        grid=(m // tm,),
        in_specs=[pl.BlockSpec((n, k), lambda i: (0, 0)), pl.BlockSpec((tm, k), lambda i: (i, 0))],
        out_specs=pl.BlockSpec((n, tm), lambda i: (0, i)),
        out_shape=jax.ShapeDtypeStruct((n, m), out_dtype),
        compiler_params=_cparams(("parallel",)),
        name="matmul_nt",
    )(wt, a)


def _alibi_columns(slopes, tq, tk):
    return [_alibi_side(slopes, tq, True), _alibi_side(slopes, tk, False)]


def _alibi_side(slopes, n, query_side):
    pos = jnp.arange(n, dtype=F32)
    val = (slopes.astype(F32) * LOG2E)[:, None] * pos[None, :]

    def pieces(x):
        p1 = x.astype(BF16)
        r1 = x - p1.astype(F32)
        p2 = r1.astype(BF16)
        p3 = (r1 - p2.astype(F32)).astype(BF16)
        return [p1, p2, p3]

    ones = [jnp.ones_like(val, BF16)] * 3
    six = jnp.stack(pieces(-val) + ones if query_side else ones + pieces(val), axis=-1)
    pad = jnp.zeros(val.shape + (ATT_QKDIM - 6,), BF16)
    return jnp.concatenate([six, pad, six, pad], axis=-1)


def _attn_kernel(slope_ref, lam_ref, q_ref, k_ref, vt_ref, aq_ref, ak_ref, g_ref, o_ref,
                 m_ref, l_ref, acc_ref, *, tq, tk, nk, hb, out_scale):
    hg = pl.program_id(1)
    i = pl.program_id(2)
    j = pl.program_id(3)

    @pl.when(j == 0)
    def _():
        m_ref[...] = jnp.full(m_ref.shape, -jnp.inf, F32)
        l_ref[...] = jnp.zeros(l_ref.shape, F32)
        acc_ref[...] = jnp.zeros(acc_ref.shape, F32)

    q_first = lax.broadcasted_iota(I32, (tq, LANES), 1) < ATT_QKDIM
    k_first = lax.broadcasted_iota(I32, (tk, LANES), 1) < ATT_QKDIM
    nt = (((1,), (1,)), ((), ()))

    def update(hh, scores, shift):
        vt = vt_ref[hh * LANES:(hh + 1) * LANES, :]
        for c, s in enumerate(scores):
            m_old = m_ref[hh, c]
            m_new = jnp.maximum(m_old, jnp.max(s, axis=0, keepdims=True) + shift)
            alpha = jnp.exp2(m_old - m_new)
            p = jnp.exp2(s - (m_new - shift))
            l_ref[hh, c] = alpha * l_ref[hh, c] + jnp.sum(p, axis=0, keepdims=True)
            acc_ref[hh, c] = alpha * acc_ref[hh, c] + jnp.dot(vt, p.astype(BF16),
                                                              preferred_element_type=F32)
            m_ref[hh, c] = m_new

    keys_before = i * tq >= (j + 1) * tk
    keys_after = (i + 1) * tq <= j * tk
    off_diagonal = jnp.logical_or(keys_before, keys_after)

    @pl.when(off_diagonal)
    def _():
        sign = jnp.where(keys_after, -1.0, 1.0).astype(BF16)
        gap = jnp.abs(i * tq - j * tk).astype(F32)
        for hh in range(hb):
            sl = slice(hh * LANES, (hh + 1) * LANES)
            q = q_ref[:, sl]
            k = k_ref[:, sl]
            aq = aq_ref[hh]
            ak = ak_ref[hh] * sign
            s0 = lax.dot_general(jnp.where(k_first, k, ak), jnp.where(q_first, q, aq), nt,
                                 preferred_element_type=F32)
            s1 = lax.dot_general(jnp.where(k_first, ak, k), jnp.where(q_first, aq, q), nt,
                                 preferred_element_type=F32)
            update(hh, (s0, s1), -(slope_ref[hg * hb + hh] * LOG2E) * gap)

    @pl.when(jnp.logical_not(off_diagonal))
    def _():
        kpos = lax.broadcasted_iota(I32, (tk, tq), 0) + j * tk
        qpos = lax.broadcasted_iota(I32, (tk, tq), 1) + i * tq
        dist = jnp.abs(kpos - qpos).astype(F32)
        for hh in range(hb):
            sl = slice(hh * LANES, (hh + 1) * LANES)
            q = q_ref[:, sl]
            k = k_ref[:, sl]
            zero = jnp.zeros_like(q)
            bias = dist * (-(slope_ref[hg * hb + hh] * LOG2E))
            s0 = lax.dot_general(k, jnp.where(q_first, q, zero), nt, preferred_element_type=F32) + bias
            s1 = lax.dot_general(k, jnp.where(q_first, zero, q), nt, preferred_element_type=F32) + bias
            update(hh, (s0, s1), 0.0)

    @pl.when(j == nk - 1)
    def _():
        for hh in range(hb):
            o = acc_ref[hh, 0] / l_ref[hh, 0] - lam_ref[0] * (acc_ref[hh, 1] / l_ref[hh, 1])
            ms = jnp.mean(o * o, axis=0, keepdims=True)
            y = o * lax.rsqrt(ms + RMS_EPS) * (g_ref[...] * out_scale)
            o_ref[:, hh * LANES:(hh + 1) * LANES] = y.T.astype(o_ref.dtype)


def _attention(qk, vt, row_off, slopes, lam, sub_g, batch, seq, out_scale, tq=512, tk=1024, hb=4):
    tq = min(tq, seq)
    tk = min(tk, seq // 4)
    nq, nk = seq // tq, seq // tk
    assert row_off % tq == 0 and row_off % tk == 0 and ATT_HEADS % hb == 0
    oq, ok = row_off // tq, row_off // tk
    ng = ATT_HEADS // hb
    aq, ak = _alibi_columns(slopes, tq, tk)
    kern = functools.partial(_attn_kernel, tq=tq, tk=tk, nk=nk, hb=hb, out_scale=out_scale)
    smem = pl.BlockSpec(memory_space=pltpu.SMEM)
    return pl.pallas_call(
        kern,
        grid=(batch, ng, nq, nk),
        in_specs=[smem, smem,
                  pl.BlockSpec((tq, hb * LANES), lambda b, h, i, j: (oq + b * nq + i, h)),
                  pl.BlockSpec((tk, hb * LANES), lambda b, h, i, j: (ok + b * nk + j, ng + h)),
                  pl.BlockSpec((hb * LANES, tk), lambda b, h, i, j: (h, ok + b * nk + j)),
                  pl.BlockSpec((hb, tq, LANES), lambda b, h, i, j: (h, 0, 0)),
                  pl.BlockSpec((hb, tk, LANES), lambda b, h, i, j: (h, 0, 0)),
                  pl.BlockSpec((LANES, 1), lambda b, h, i, j: (0, 0))],
        out_specs=pl.BlockSpec((tq, hb * LANES), lambda b, h, i, j: (b * nq + i, h)),
        out_shape=jax.ShapeDtypeStruct((batch * seq, ATT_HEADS * ATT_VDIM), BF16),
        scratch_shapes=[pltpu.VMEM((hb, 2, 1, tq), F32), pltpu.VMEM((hb, 2, 1, tq), F32),
                        pltpu.VMEM((hb, 2, LANES, tq), F32)],
        compiler_params=_cparams(("parallel", "parallel", "parallel", "arbitrary")),
        name="diff_attention",
    )(slopes, lam, qk, qk, vt, aq, ak, sub_g.reshape(LANES, 1).astype(F32))


def _shortconv_kernel(z_ref, w_ref, b_ref, o_ref):
    z = z_ref[...]
    n = z.shape[0]
    row = lax.broadcasted_iota(I32, z.shape, 0)
    prev = jnp.where(row == 0, 0.0, pltpu.roll(z, 1, axis=0))
    nxt = jnp.where(row == n - 1, 0.0, pltpu.roll(z, n - 1, axis=0))
    w = w_ref[...]
    o_ref[...] = prev * w[0:1] + z * w[1:2] + nxt * w[2:3] + b_ref[...]


def _shortconv(zh, row_off, conv_w, conv_b, batch, seq, cb=256):
    c3 = zh.shape[1]
    c = c3 // 3
    ncb = c // cb
    assert row_off % seq == 0
    ob = row_off // seq
    out = pl.pallas_call(
        _shortconv_kernel,
        grid=(batch, 3, ncb),
        in_specs=[pl.BlockSpec((seq, cb), lambda b, p, j: (ob + b, p * ncb + j)),
                  pl.BlockSpec((3, cb), lambda b, p, j: (0, p * ncb + j)),
                  pl.BlockSpec((1, cb), lambda b, p, j: (0, p * ncb + j))],
        out_specs=pl.BlockSpec((None, None, seq, cb), lambda b, p, j: (p, b, 0, j)),
        out_shape=jax.ShapeDtypeStruct((3, batch, seq, c), F32),
        compiler_params=_cparams(("parallel", "parallel", "parallel")),
        name="hyena_shortconv",
    )(zh, conv_w.astype(F32), conv_b.reshape(1, c3).astype(F32))
    return out


def _filter_kernel(z_ref, w1_ref, b1_ref, w2_ref, b2_ref, w3_ref, b3_ref, fr_ref, w4_ref,
                   t_ref, d_ref, o_ref, h_ref, *, tl):
    i = pl.program_id(0)
    g = pl.program_id(1)
    hi = lax.Precision.HIGHEST

    @pl.when(g == 0)
    def _():
        fr = fr_ref[...]
        h = jnp.sin(fr * (jnp.dot(z_ref[...], w1_ref[...], precision=hi, preferred_element_type=F32) + b1_ref[...]))
        h = jnp.sin(fr * (jnp.dot(h, w2_ref[...], precision=hi, preferred_element_type=F32) + b2_ref[...]))
        h_ref[...] = jnp.sin(fr * (jnp.dot(h, w3_ref[...], precision=hi, preferred_element_type=F32) + b3_ref[...]))

    f = jnp.dot(h_ref[...].astype(BF16), w4_ref[...], preferred_element_type=F32)
    f = f * jnp.exp(-t_ref[...] * d_ref[...])
    row = lax.broadcasted_iota(I32, f.shape, 0) + i * tl
    drop = jnp.logical_and(row == 0, g % 2 == 1)
    o_ref[...] = jnp.where(drop, 0.0, f)


def _hyena_filter_signals(seq, w1, b1, w2, b2, w3, b3, w4, freq, n_ch, tl=512):
    t = jnp.linspace(0.0, 1.0, seq, dtype=F32)[:, None]
    w = 2.0 * math.pi * jnp.arange(seq, dtype=F32)[:, None] / seq
    f = jnp.linspace(1e-4, FILTER_BANDS - 1, FILTER_BANDS, dtype=F32)[None, :]
    z = jnp.concatenate([t, jnp.cos(f * w), -jnp.sin(f * w)], axis=-1)
    deltas = jnp.abs(jnp.linspace(math.log(DECAY_FAST) / DECAY_TARGET,
                                  math.log(DECAY_SLOW) / DECAY_TARGET, n_ch, dtype=F32))[None, :]
    hid = w1.shape[1]
    emb = LANES
    z = jnp.pad(z, ((0, 0), (0, emb - z.shape[1])))
    w1 = jnp.pad(w1.astype(F32), ((0, emb - w1.shape[0]), (0, 0)))
    tl = min(tl, seq)
    full = lambda shape: pl.BlockSpec(shape, lambda i, g: tuple(0 for _ in shape))
    return pl.pallas_call(
        functools.partial(_filter_kernel, tl=tl),
        grid=(seq // tl, 2 * HYENA_ORDER),
        in_specs=[pl.BlockSpec((tl, emb), lambda i, g: (i, 0)),
                  full((emb, hid)), full((1, hid)), full((hid, hid)), full((1, hid)),
                  full((hid, hid)), full((1, hid)), full((1, hid)),
                  pl.BlockSpec((hid, n_ch), lambda i, g: (0, g)),
                  pl.BlockSpec((tl, 1), lambda i, g: (i, 0)),
                  full((1, n_ch))],
        out_specs=pl.BlockSpec((None, tl, n_ch), lambda i, g: (g, i, 0)),
        out_shape=jax.ShapeDtypeStruct((2 * HYENA_ORDER, seq, n_ch), F32),
        scratch_shapes=[pltpu.VMEM((tl, hid), F32)],
        compiler_params=_cparams(("parallel", "arbitrary")),
        name="hyena_filter_mlp",
    )(z, w1.astype(F32), b1.reshape(1, hid).astype(F32), w2.astype(F32), b2.reshape(1, hid).astype(F32),
      w3.astype(F32), b3.reshape(1, hid).astype(F32), freq.reshape(1, hid).astype(F32), w4.astype(BF16),
      t, deltas)


def _dft_tables(r):
    n = r * r
    ang1 = (2.0 * np.pi / r) * ((np.arange(r)[:, None] * np.arange(r // 2)[None, :]) % r)
    eye = np.eye(N1_BLOCK)
    f1 = jnp.asarray(np.kron(np.concatenate([np.cos(ang1), -np.sin(ang1)], axis=0), eye), BF16)
    g3 = jnp.asarray(np.kron(np.concatenate([np.cos(ang1.T), -np.sin(ang1.T)], axis=1) / n, eye), BF16)
    k2 = jnp.arange(r, dtype=I32)
    k1 = jnp.arange(r, dtype=I32)
    n1 = jnp.arange(r, dtype=I32)
    ang_a = (2.0 * math.pi / r) * ((k1[:, None] * n1[None, :]) % r).astype(F32)
    ang_b = (2.0 * math.pi / n) * (k2[:, None] * n1[None, :]).astype(F32)
    ca, sa = jnp.cos(ang_a)[None], jnp.sin(ang_a)[None]
    cb, sb = jnp.cos(ang_b), jnp.sin(ang_b)

    def blocks(cb3, sb3):
        return (ca * cb3 - sa * sb3).astype(BF16), (-(sa * cb3 + ca * sb3)).astype(BF16)

    mr, mi_ = blocks(cb[:, None, :], sb[:, None, :])
    mf = jnp.concatenate([jnp.concatenate([mr, -mi_], axis=2),
                          jnp.concatenate([mi_, mr], axis=2)], axis=1)
    mr_t, mi_t = blocks(cb[:, :, None], sb[:, :, None])
    minv = jnp.concatenate([jnp.concatenate([mr_t, mi_t], axis=2),
                            jnp.concatenate([-mi_t, mr_t], axis=2)], axis=1)
    return f1, mf, minv, g3


N1_BLOCK = SUBLANES


def _pack_complex(re, im):
    rb = lax.bitcast_convert_type(re.astype(BF16).astype(F32), U32)
    ib = lax.bitcast_convert_type(im.astype(BF16).astype(F32), U32)
    return (rb >> 16) | ib


def _unpack_complex_rows(w):
    re, im = _unpack(w)
    return jnp.concatenate([re, im], axis=0).astype(BF16)


def _fft1_kernel(f_ref, x_ref, o_ref, *, r):
    n_ch = x_ref.shape[-1]
    x = x_ref[...].reshape((r // 2) * N1_BLOCK, n_ch).astype(BF16)
    res = jnp.dot(f_ref[...], x, preferred_element_type=F32)
    half = r * N1_BLOCK
    o_ref[...] = _pack_complex(res[:half], res[half:]).reshape(r, N1_BLOCK, n_ch)


def _fft_stage1(x4, part, f1, r, n_ch):
    nb = x4.shape[1]
    xv = x4.reshape(x4.shape[0], nb, r // 2, r, n_ch)
    return pl.pallas_call(
        functools.partial(_fft1_kernel, r=r),
        grid=(nb, r // N1_BLOCK),
        in_specs=[pl.BlockSpec(f1.shape, lambda b, j: (0, 0)),
                  pl.BlockSpec((None, None, r // 2, N1_BLOCK, n_ch), lambda b, j: (part, b, 0, j, 0))],
        out_specs=pl.BlockSpec((None, r, N1_BLOCK, n_ch), lambda b, j: (b, 0, j, 0)),
        out_shape=jax.ShapeDtypeStruct((nb, r, r, n_ch), U32),
        compiler_params=_cparams(("parallel", "parallel")),
        name="hyena_dft_stage1",
    )(f1, xv)


def _k2_block(r):
    return max(1, min(r, (8 * LANES) // r))


def _filter_spec_kernel(mf_ref, bf_ref, bb_ref, o_ref, *, r):
    for kk in range(mf_ref.shape[0]):
        m = mf_ref[kk]
        xf = jnp.dot(m, _unpack_complex_rows(bf_ref[kk]), preferred_element_type=F32)
        xb = jnp.dot(m, _unpack_complex_rows(bb_ref[kk]), preferred_element_type=F32)
        o_ref[kk] = _pack_complex(xf[:r] + xb[:r], xf[r:] - xb[r:])


def _filter_spectrum(b1, mf, r, n_ch):
    kb = _k2_block(r)
    blk = lambda sel: pl.BlockSpec((None, kb, r, n_ch), lambda o, k: (2 * o + sel, k, 0, 0))
    return pl.pallas_call(
        functools.partial(_filter_spec_kernel, r=r),
        grid=(HYENA_ORDER, r // kb),
        in_specs=[pl.BlockSpec((kb, 2 * r, 2 * r), lambda o, k: (k, 0, 0)), blk(0), blk(1)],
        out_specs=pl.BlockSpec((None, kb, r, n_ch), lambda o, k: (o, k, 0, 0)),
        out_shape=jax.ShapeDtypeStruct((HYENA_ORDER, r, r, n_ch), U32),
        compiler_params=_cparams(("parallel", "parallel")),
        name="hyena_filter_spectrum",
    )(mf, b1, b1)


def _fft2_kernel(mf_ref, mi_ref, b_ref, h_ref, o_ref, *, r):
    for kk in range(mf_ref.shape[0]):
        x = jnp.dot(mf_ref[kk], _unpack_complex_rows(b_ref[kk]), preferred_element_type=F32)
        xr, xi = x[:r], x[r:]
        hr, hi = _unpack(h_ref[kk])
        y = jnp.concatenate([xr * hr - xi * hi, xr * hi + xi * hr], axis=0).astype(BF16)
        c = jnp.dot(mi_ref[kk], y, preferred_element_type=F32)
        o_ref[kk] = _pack_complex(c[:r], c[r:])


def _fft_stage2(b1, h, order, mf, minv, r, n_ch):
    nb = b1.shape[0]
    kb = _k2_block(r)
    blk = pl.BlockSpec((None, kb, r, n_ch), lambda k, b: (b, k, 0, 0))
    return pl.pallas_call(
        functools.partial(_fft2_kernel, r=r),
        grid=(r // kb, nb),
        in_specs=[pl.BlockSpec((kb, 2 * r, 2 * r), lambda k, b: (k, 0, 0)),
                  pl.BlockSpec((kb, 2 * r, 2 * r), lambda k, b: (k, 0, 0)),
                  blk,
                  pl.BlockSpec((None, kb, r, n_ch), lambda k, b: (order, k, 0, 0))],
        out_specs=blk,
        out_shape=jax.ShapeDtypeStruct((nb, r, r, n_ch), U32),
        compiler_params=_cparams(("parallel", "parallel")),
        name="hyena_dft_stage2",
    )(mf, minv, b1, h)


def _fft3_kernel(g_ref, c_ref, gate_ref, s_ref, bias_ref, ng_ref, *rest, final):
    if final:
        (o_ref,) = rest
    else:
        f_ref, o_ref, b_ref = rest
    r, nb, n_ch = c_ref.shape
    cc = _unpack_complex_rows(c_ref[...].reshape(r * nb, n_ch))
    y = jnp.dot(g_ref[...], cc, preferred_element_type=F32)
    rows = (r // 2) * nb
    s_new = gate_ref[...].reshape(rows, n_ch) * (y + s_ref[...].reshape(rows, n_ch) * bias_ref[...])
    if final:
        ms = jnp.mean(s_new * s_new, axis=-1, keepdims=True)
        s_new = s_new * lax.rsqrt(ms + RMS_EPS) * ng_ref[...]
    o_ref[...] = s_new.reshape(r // 2, nb, n_ch)
    if not final:
        res = jnp.dot(f_ref[...], s_new.astype(BF16), preferred_element_type=F32)
        half = r * nb
        b_ref[...] = _pack_complex(res[:half], res[half:]).reshape(r, nb, n_ch)


def _fft_stage3(c2, g3, z4, gate_part, s4, s_part, bias, norm_g, f1, r, n_ch):
    final = f1 is None
    nb = c2.shape[0]
    zv = z4.reshape(z4.shape[0], nb, r // 2, r, n_ch)
    sv = s4.reshape(s4.shape[0], nb, r // 2, r, n_ch)
    dspec = lambda part: pl.BlockSpec((None, None, r // 2, N1_BLOCK, n_ch), lambda b, j: (part, b, 0, j, 0))
    packed = pl.BlockSpec((None, r, N1_BLOCK, n_ch), lambda b, j: (b, 0, j, 0))
    s_spec = pl.BlockSpec((None, r // 2, N1_BLOCK, n_ch), lambda b, j: (b, 0, j, 0))
    s_shape = jax.ShapeDtypeStruct((nb, r // 2, r, n_ch), F32)
    in_specs = [pl.BlockSpec(g3.shape, lambda b, j: (0, 0)), packed, dspec(gate_part), dspec(s_part),
                pl.BlockSpec((1, n_ch), lambda b, j: (0, 0)), pl.BlockSpec((1, n_ch), lambda b, j: (0, 0))]
    args = [g3, c2, zv, sv, bias.reshape(1, n_ch).astype(F32), norm_g.reshape(1, n_ch).astype(F32)]
    if final:
        out_specs, out_shape = s_spec, s_shape
    else:
        in_specs.append(pl.BlockSpec(f1.shape, lambda b, j: (0, 0)))
        args.append(f1)
        out_specs = [s_spec, packed]
        out_shape = [s_shape, jax.ShapeDtypeStruct((nb, r, r, n_ch), U32)]
    return pl.pallas_call(
        functools.partial(_fft3_kernel, final=final),
        grid=(nb, r // N1_BLOCK),
        in_specs=in_specs,
        out_specs=out_specs,
        out_shape=out_shape,
        compiler_params=_cparams(("parallel", "parallel")),
        name="hyena_dft_stage3",
    )(*args)


def _hyena(zh, row_off, batch, seq, conv_w, conv_b, filt, fft_bias, out_g):
    n_ch = zh.shape[1] // 3
    r = int(round(math.sqrt(2 * seq)))
    assert r * r == 2 * seq and r % 16 == 0
    f1, mf, minv, g3 = _dft_tables(r)
    sig = _hyena_filter_signals(seq, *filt, n_ch=n_ch)
    hb1 = _fft_stage1(sig[None], 0, f1, r, n_ch)
    h = _filter_spectrum(hb1, mf, r, n_ch)
    z4 = _shortconv(zh, row_off, conv_w, conv_b, batch, seq)
    s4, s_part = z4, 2
    b1 = _fft_stage1(s4, s_part, f1, r, n_ch)
    for o in range(HYENA_ORDER):
        c2 = _fft_stage2(b1, h, o, mf, minv, r, n_ch)
        if o < HYENA_ORDER - 1:
            s, b1 = _fft_stage3(c2, g3, z4, o, s4, s_part, fft_bias[o], out_g, f1, r, n_ch)
        else:
            s = _fft_stage3(c2, g3, z4, o, s4, s_part, fft_bias[o], out_g, None, r, n_ch)
        s4, s_part = s.reshape(1, batch, seq, n_ch), 0
    return s4.reshape(batch * seq, n_ch)


def _extract_top(s, key, count):
    vals, keys = [], []
    for _ in range(count):
        m = jnp.max(s, axis=0, keepdims=True)
        kmin = jnp.min(jnp.where(s == m, key, jnp.inf), axis=0, keepdims=True)
        s = jnp.where(key == kmin, -jnp.inf, s)
        vals.append(m)
        keys.append(kmin)
    return vals, keys


def _peer_topk_kernel(q_ref, keys_ref, eid_ref, gate_ref):
    t = q_ref.shape[0]
    nk = PEER_NKEYS
    q = q_ref[...]
    row_key = lax.broadcasted_iota(I32, (nk, t), 0).astype(F32)
    tops = []
    for c in range(2):
        s = lax.dot_general(keys_ref[c], q[:, c * nk:(c + 1) * nk], (((1,), (1,)), ((), ())),
                            preferred_element_type=F32)
        tops.append(_extract_top(s, row_key, PEER_TOPK))
    (v1, i1), (v2, i2) = tops
    rows16 = lax.broadcasted_iota(I32, (PEER_TOPK, t), 0)
    v2a = jnp.zeros((PEER_TOPK, t), F32)
    i2a = jnp.zeros((PEER_TOPK, t), F32)
    for j in range(PEER_TOPK):
        v2a = jnp.where(rows16 == j, v2[j], v2a)
        i2a = jnp.where(rows16 == j, i2[j], i2a)
    n_exp = float(nk * nk)
    half = PEER_TOPK // 2
    rows8 = lax.broadcasted_iota(I32, (half, t), 0)
    pos8 = rows8.astype(F32)
    v2h, i2h = v2a[:half], i2a[:half]
    cand = [v1[0] + v2a]
    ckey = [rows16.astype(F32) * n_exp + (i1[0] * float(nk) + i2a)]
    for i in range(1, half):
        cand.append(jnp.where(rows8 < PEER_TOPK // (i + 1), v1[i] + v2h, -jnp.inf))
        ckey.append((pos8 + float(i * PEER_TOPK)) * n_exp + (i1[i] * float(nk) + i2h))
    v1t = jnp.zeros((half, t), F32)
    i1t = jnp.zeros((half, t), F32)
    for r in range(half):
        v1t = jnp.where(rows8 == r, v1[half + r], v1t)
        i1t = jnp.where(rows8 == r, i1[half + r], i1t)
    cand.append(v1t + v2[0])
    ckey.append((pos8 + float(half)) * (PEER_TOPK * n_exp) + (i1t * float(nk) + i2[0]))
    tv, tk_ = _extract_top(jnp.concatenate(cand, axis=0), jnp.concatenate(ckey, axis=0), PEER_TOPK)
    denom = jnp.zeros((1, t), F32)
    es = []
    for k in range(PEER_TOPK):
        e = jnp.exp(tv[k] - tv[0])
        es.append(e)
        denom = denom + e
    eid = jnp.zeros((PEER_TOPK, t), F32)
    gate = jnp.zeros((PEER_TOPK, t), F32)
    for k in range(PEER_TOPK):
        pos = jnp.floor(tk_[k] * (1.0 / n_exp))
        eid = jnp.where(rows16 == k, tk_[k] - pos * n_exp, eid)
        gate = jnp.where(rows16 == k, es[k] / denom, gate)
    eid_ref[...] = eid.astype(I32)
    gate_ref[...] = gate


def _peer_topk(q, keys, tm=1024):
    t = q.shape[0]
    tm = min(tm, t)
    out_spec = pl.BlockSpec((PEER_TOPK, tm), lambda i, h: (h, i))
    return pl.pallas_call(
        _peer_topk_kernel,
        grid=(t // tm, PEER_HEADS),
        in_specs=[pl.BlockSpec((tm, 2 * PEER_NKEYS), lambda i, h: (i, h)),
                  pl.BlockSpec((None, 2, PEER_NKEYS, PEER_NKEYS), lambda i, h: (h, 0, 0, 0))],
        out_specs=[out_spec, out_spec],
        out_shape=[jax.ShapeDtypeStruct((PEER_HEADS * PEER_TOPK, t), I32),
                   jax.ShapeDtypeStruct((PEER_HEADS * PEER_TOPK, t), F32)],
        compiler_params=_cparams(("parallel", "parallel")),
        name="peer_topk",
    )(q, keys)


def _pack_table(tab):
    e, d = tab.shape
    assert d == 2 * SUBLANES * LANES
    def bf16_bits(x):
        w = lax.bitcast_convert_type(x.astype(F32), U32)
        return (w + (jnp.uint32(0x7FFF) + ((w >> 16) & jnp.uint32(1)))) >> 16

    packed = bf16_bits(tab[:, :d // 2]) | (bf16_bits(tab[:, d // 2:]) << 16)
    return packed.reshape(e * SUBLANES, LANES)


def _unpack(w):
    lo = lax.bitcast_convert_type(w << 16, F32)
    hi = lax.bitcast_convert_type(w & jnp.uint32(0xFFFF0000), F32)
    return lo, hi


_BITREV8 = (0, 4, 2, 6, 1, 5, 3, 7)


def _sublane_fold8(parts):
    sub = lax.broadcasted_iota(I32, (2 * SUBLANES, LANES), 0) // 2

    def rolled(a, shift):
        return pltpu.bitcast(pltpu.roll(pltpu.bitcast(a, U32), shift, axis=0), BF16)

    lvl = [parts[_BITREV8[r]] for r in range(8)]
    for shift, mask in ((4, sub < 4), (2, (sub % 4) < 2), (1, (sub % 2) < 1)):
        nxt = []
        for a, b in zip(lvl[0::2], lvl[1::2]):
            nxt.append(jnp.where(mask, a + rolled(a, SUBLANES - shift), b + rolled(b, shift)))
        lvl = nxt
    return lvl[0]


def _pair_ranges(half):
    if half == 0:
        return (0, PEER_WINDOW), (PEER_WINDOW, PEER_PAIRS)
    return (PEER_PAIRS - PEER_WINDOW, PEER_PAIRS), (0, PEER_PAIRS - PEER_WINDOW)


def _overflow(n_low, half):
    return n_low > PEER_WINDOW if half == 0 else n_low < PEER_PAIRS - PEER_WINDOW


def _table_rows(tab_ref, off):
    return _unpack(tab_ref[pl.ds(pl.multiple_of(off, SUBLANES), SUBLANES), :])


def _peer_u_kernel(off_ref, nlow_ref, tab_ref, x_ref, o_ref, extra_ref, *, tm, half):
    lane = lax.broadcasted_iota(I32, (SUBLANES, LANES), 1)
    sub = lax.broadcasted_iota(I32, (SUBLANES, LANES), 0)
    lane_grp = lax.shift_right_logical(lane, 3)
    diag = sub == (lane & (SUBLANES - 1))
    main, rest = _pair_ranges(half)

    def folded(t, p0, p1):
        xb = pltpu.bitcast(x_ref[t], BF16)
        out = []
        for g in range(p0 // SUBLANES, p1 // SUBLANES):
            parts = []
            for r in range(SUBLANES):
                off = pl.multiple_of(off_ref[t, g * SUBLANES + r], SUBLANES)
                parts.append(pltpu.bitcast(tab_ref[pl.ds(off, SUBLANES), :], BF16) * xb)
            out.append(pltpu.bitcast(_sublane_fold8(parts), U32))
        return out

    def lane_sums(folds, p0):
        mat = jnp.zeros((SUBLANES, LANES), F32)
        for i, f in enumerate(folds):
            lo, hi = _unpack(f)
            mat = jnp.where(lane_grp == p0 // SUBLANES + i, jnp.sum(lo + hi, axis=-1, keepdims=True), mat)
        return jnp.sum(jnp.where(diag, mat, 0.0), axis=0, keepdims=True)

    def finish(t, folds):
        o_ref[pl.ds(t, 1), :] = lane_sums(folds, main[0]) + extra_ref[pl.ds(t, 1), :]

    def token(t, prev):
        cur = folded(t, *main)
        finish(jnp.maximum(t - 1, 0), prev)
        extra_ref[pl.ds(t, 1), :] = jnp.zeros((1, LANES), F32)

        @pl.when(_overflow(nlow_ref[0, t], half))
        def _():
            extra_ref[pl.ds(t, 1), :] = lane_sums(folded(t, *rest), rest[0])

        return tuple(cur)

    extra_ref[pl.ds(0, 1), :] = jnp.zeros((1, LANES), F32)
    zeros = tuple(jnp.zeros((SUBLANES, LANES), U32) for _ in range((main[1] - main[0]) // SUBLANES))
    last = lax.fori_loop(0, tm, token, zeros)
    finish(tm - 1, last)


def _peer_u(off, nlow3, tab, x4, half, tm=128):
    t = off.shape[0]
    rows = PEER_HALF * SUBLANES
    return pl.pallas_call(
        functools.partial(_peer_u_kernel, tm=tm, half=half),
        grid=(t // tm,),
        in_specs=[pl.BlockSpec((tm, LANES), lambda i: (i, 0), memory_space=pltpu.SMEM),
                  pl.BlockSpec((None, 1, tm), lambda i: (i, 0, 0), memory_space=pltpu.SMEM),
                  pl.BlockSpec((rows, LANES), lambda i: (half, 0), pipeline_mode=pl.Buffered(1)),
                  pl.BlockSpec((tm, SUBLANES, LANES), lambda i: (i, 0, 0))],
        out_specs=pl.BlockSpec((tm, LANES), lambda i: (i, 0)),
        out_shape=jax.ShapeDtypeStruct((t, LANES), F32),
        scratch_shapes=[pltpu.VMEM((tm, LANES), F32)],
        compiler_params=_cparams(("arbitrary",)),
        name="peer_expert_scores",
    )(off, nlow3, tab, x4)


def _peer_order_kernel(eid_ref, eid_o, slot_o, off_o, nlow_o):
    eid = eid_ref[...].T
    lane = lax.broadcasted_iota(I32, eid.shape, 1)
    low = (eid < PEER_HALF).astype(I32)
    c_low = low
    shift = 1
    while shift < PEER_PAIRS:
        c_low = c_low + jnp.where(lane >= shift, pltpu.roll(c_low, shift, axis=1), 0)
        shift *= 2
    c_high = lane + 1 - c_low
    n_low = jnp.max(c_low, axis=1, keepdims=True)
    in_low = lane < n_low
    rank = jnp.where(in_low, lane + 1, lane + 1 - n_low)
    pos = jnp.zeros_like(lane)
    step = PEER_PAIRS // 2
    while step >= 1:
        probe = pos + (step - 1)
        count = jnp.where(in_low, jnp.take_along_axis(c_low, probe, axis=1),
                          jnp.take_along_axis(c_high, probe, axis=1))
        pos = jnp.where(count < rank, pos + step, pos)
        step //= 2
    eid_s = jnp.take_along_axis(eid, pos, axis=1)
    eid_o[...] = eid_s
    slot_o[...] = pos
    off_o[...] = (eid_s & (PEER_HALF - 1)) * SUBLANES
    nlow_o[...] = jnp.broadcast_to(n_low, eid.shape)


def _peer_order(eid_t, tm=1024):
    t = eid_t.shape[1]
    tm = min(tm, t)
    spec = pl.BlockSpec((tm, LANES), lambda i: (i, 0))
    return pl.pallas_call(
        _peer_order_kernel,
        grid=(t // tm,),
        in_specs=[pl.BlockSpec((LANES, tm), lambda i: (0, i))],
        out_specs=[spec] * 4,
        out_shape=[jax.ShapeDtypeStruct((t, LANES), I32)] * 4,
        compiler_params=_cparams(("parallel",)),
        name="peer_pair_order",
    )(eid_t)


def _peer_coef_kernel(a0_ref, a1_ref, eid_ref, slot_ref, gate_ref, c0_ref, c1_ref):
    low = eid_ref[...] < PEER_HALF
    a = jnp.where(low, a0_ref[...], a1_ref[...])
    gate = jnp.take_along_axis(gate_ref[...].T, slot_ref[...], axis=1)
    coef = gate * (0.5 * a * (1.0 + lax.erf(a * (1.0 / math.sqrt(2.0)))))
    c0_ref[...] = jnp.where(low, coef, 0.0)
    c1_ref[...] = jnp.where(low, 0.0, coef)


def _peer_coef(a0, a1, eid, slot, gate_t, tm=1024):
    t = eid.shape[0]
    tm = min(tm, t)
    spec = pl.BlockSpec((tm, LANES), lambda i: (i, 0))
    return pl.pallas_call(
        _peer_coef_kernel,
        grid=(t // tm,),
        in_specs=[spec, spec, spec, spec, pl.BlockSpec((LANES, tm), lambda i: (0, i))],
        out_specs=[spec, spec],
        out_shape=[jax.ShapeDtypeStruct((t, LANES), F32)] * 2,
        compiler_params=_cparams(("parallel",)),
        name="peer_coef",
    )(a0, a1, eid, slot, gate_t)


def _peer_v_kernel(off_ref, nlow_ref, coef_ref, tab_ref, base_ref, *o_refs, tm, half, n_first):
    n_acc = 4
    main, rest = _pair_ranges(half)

    def weighted(t, p0, p1):
        acc_lo = [jnp.zeros((SUBLANES, LANES), F32) for _ in range(n_acc)]
        acc_hi = [jnp.zeros((SUBLANES, LANES), F32) for _ in range(n_acc)]
        for p in range(p0, p1):
            c = coef_ref[t, p]
            lo, hi = _table_rows(tab_ref, off_ref[t, p])
            acc_lo[p % n_acc] = acc_lo[p % n_acc] + c * lo
            acc_hi[p % n_acc] = acc_hi[p % n_acc] + c * hi
        return ((acc_lo[0] + acc_lo[1]) + (acc_lo[2] + acc_lo[3]),
                (acc_hi[0] + acc_hi[1]) + (acc_hi[2] + acc_hi[3]))

    def run(o_ref):
        def store_row(t, lo, hi):
            o_ref[t, 0] = base_ref[t, 0] + lo
            o_ref[t, 1] = base_ref[t, 1] + hi

        def token(t, carry):
            lo, hi = weighted(t, *main)
            store_row(t, lo, hi)

            @pl.when(_overflow(nlow_ref[0, t], half))
            def _():
                lo2, hi2 = weighted(t, *rest)
                store_row(t, lo + lo2, hi + hi2)

            return carry

        lax.fori_loop(0, tm, token, 0)

    if len(o_refs) == 1:
        run(o_refs[0])
    else:
        first = pl.program_id(0) < n_first
        pl.when(first)(lambda: run(o_refs[0]))
        pl.when(jnp.logical_not(first))(lambda: run(o_refs[1]))


def _peer_v(off, nlow3, coef, tab, base, half, tm=128, split=None):
    t = base.shape[0]
    rows = PEER_HALF * SUBLANES
    smem = pl.BlockSpec((tm, LANES), lambda i: (i, 0), memory_space=pltpu.SMEM)
    tile = pl.BlockSpec((tm, 2, SUBLANES, LANES), lambda i: (i, 0, 0, 0))
    if split is None:
        n_first, out_specs = 0, tile
        out_shape = jax.ShapeDtypeStruct((t, 2, SUBLANES, LANES), F32)
    else:
        assert split % tm == 0
        n_first, n_rest = split // tm, (t - split) // tm
        blk = (tm, 2, SUBLANES, LANES)
        out_specs = [pl.BlockSpec(blk, lambda i: (jnp.clip(i, 0, n_first - 1), 0, 0, 0)),
                     pl.BlockSpec(blk, lambda i: (jnp.clip(i - n_first, 0, n_rest - 1), 0, 0, 0))]
        out_shape = [jax.ShapeDtypeStruct((split, 2, SUBLANES, LANES), F32),
                     jax.ShapeDtypeStruct((t - split, 2, SUBLANES, LANES), F32)]
    return pl.pallas_call(
        functools.partial(_peer_v_kernel, tm=tm, half=half, n_first=n_first),
        grid=(t // tm,),
        in_specs=[smem,
                  pl.BlockSpec((None, 1, tm), lambda i: (i, 0, 0), memory_space=pltpu.SMEM),
                  smem,
                  pl.BlockSpec((rows, LANES), lambda i: (half, 0), pipeline_mode=pl.Buffered(1)),
                  tile],
        out_specs=out_specs,
        out_shape=out_shape,
        compiler_params=_cparams(("arbitrary",)),
        name="peer_expert_sum",
    )(off, nlow3, coef, tab, base)


def _peer(x1, split, norm2_g, wq_bf, keys_bf, u_packed, v_packed, tm=256):
    t, d = x1.shape
    tm = min(tm, t)
    (xn_bf,) = _rmsnorm(x1, norm2_g, (BF16,))
    q = _matmul(xn_bf, wq_bf, wq_bf.shape[1], 0, BF16)
    eid_t, gate_t = _peer_topk(q, keys_bf)
    eid, slot, off, nlow = _peer_order(eid_t)
    nlow3 = nlow[:, 0].reshape(t // tm, 1, tm)
    xw = _pack_table(xn_bf).reshape(t, SUBLANES, LANES)
    a0 = _peer_u(off, nlow3, u_packed, xw, 0, tm)
    a1 = _peer_u(off, nlow3, u_packed, xw, 1, tm)
    c0, c1 = _peer_coef(a0, a1, eid, slot, gate_t)
    y = _peer_v(off, nlow3, c0, v_packed, x1.reshape(t, 2, SUBLANES, LANES), 0, tm)
    ya, yb = _peer_v(off, nlow3, c1, v_packed, y, 1, tm, split=split)
    return ya.reshape(split, d), yb.reshape(t - split, d)


def kernel(x_prompt, x_sample, norm1_g, w_in, q_norm_g, k_norm_g, lambda_q1, lambda_k1, lambda_q2, lambda_k2, attn_sub_g, conv_w, conv_b, filt_w1, filt_b1, filt_w2, filt_b2, filt_w3, filt_b3, filt_w4, filt_freq, fft_bias, hyena_out_g, w_out, norm2_g, peer_wq, peer_keys, peer_u, peer_v):
    depth = w_in.shape[0]
    d_model = x_prompt.shape[-1]
    att_w = ATT_HEADS * ATT_VDIM
    shapes = [x_prompt.shape[:2], x_sample.shape[:2]]
    xs = [x_prompt.reshape(-1, d_model), x_sample.reshape(-1, d_model)]
    n0 = xs[0].shape[0]
    slopes = 2.0 ** (-8.0 * jnp.arange(1, ATT_HEADS + 1, dtype=F32) / ATT_HEADS)

    for l in range(depth):
        lambda_init = 0.8 - 0.6 * math.exp(-0.3 * l)
        lam = (jnp.exp(jnp.sum(lambda_q1[l].astype(F32) * lambda_k1[l].astype(F32)))
               - jnp.exp(jnp.sum(lambda_q2[l].astype(F32) * lambda_k2[l].astype(F32)))
               + lambda_init).reshape(1)
        w_in_bf = w_in[l].astype(BF16)
        q_gain = jnp.tile(q_norm_g[l].astype(F32), 2 * ATT_HEADS) * (ATT_QKDIM ** -0.5 * LOG2E)
        k_gain = jnp.tile(k_norm_g[l].astype(F32), 2 * ATT_HEADS)
        qk_gain = jnp.concatenate([q_gain, k_gain]).reshape(1, 2 * att_w)

        h_bf = _rmsnorm_stacked(xs[0], xs[1], norm1_g[l], BF16)
        qk = _matmul(h_bf, w_in_bf, 2 * att_w, 0, BF16, mode="qknorm", extra=qk_gain)
        vt = _matmul_nt(w_in_bf[:, 2 * att_w:3 * att_w].T, h_bf, BF16)
        zh = _matmul(h_bf, w_in_bf, w_in.shape[2] - 3 * att_w, 3 * att_w, F32)

        filt = (filt_w1[l], filt_b1[l], filt_w2[l], filt_b2[l], filt_w3[l], filt_b3[l], filt_w4[l], filt_freq[l])
        segs, row = [], 0
        for (b, s), x_seg in zip(shapes, xs):
            att = _attention(qk, vt, row, slopes, lam, attn_sub_g[l], b, s, 1.0 - lambda_init)
            hy = _hyena(zh, row, b, s, conv_w[l], conv_b[l], filt, fft_bias[l], hyena_out_g[l])
            segs.append((att, hy, x_seg))
            row += b * s
        x1 = _outproj_stacked(segs[0], segs[1], w_out[l].astype(BF16))

        xs = _peer(x1, n0, norm2_g[l], peer_wq[l].astype(BF16), peer_keys[l].astype(BF16),
                   _pack_table(peer_u[l]), _pack_table(peer_v[l]))

    return (xs[0].reshape(x_prompt.shape), xs[1].reshape(x_sample.shape))
```

```python
import functools
import math

import jax
import jax.numpy as jnp
import numpy as np
from jax import lax
from jax.experimental import pallas as pl
from jax.experimental.pallas import tpu as pltpu

F32 = jnp.float32
BF16 = jnp.bfloat16
I32 = jnp.int32
U32 = jnp.uint32

RMS_EPS = 1e-6
LOG2E = 1.4426950408889634
LANES = 128
SUBLANES = 8
VMEM_LIMIT_BYTES = 56 * 1024 * 1024

ATT_HEADS = 8
ATT_VDIM = 128
ATT_QKDIM = 64
HYENA_ORDER = 2
FILTER_BANDS = 16
DECAY_FAST = 0.3
DECAY_SLOW = 1.5
DECAY_TARGET = 1e-2
PEER_HEADS = 8
PEER_NKEYS = 128
PEER_TOPK = 16
PEER_HALF = PEER_NKEYS * PEER_NKEYS // 2
PEER_PAIRS = PEER_HEADS * PEER_TOPK
PEER_WINDOW = 80


def _cparams(sem, vmem=VMEM_LIMIT_BYTES):
    return pltpu.CompilerParams(dimension_semantics=sem, vmem_limit_bytes=vmem)


def _segment_spec(block, start, count, minor=0):
    return pl.BlockSpec(block, lambda i: (jnp.clip(i - start, 0, count - 1), minor))


def _rmsnorm_kernel(x_ref, g_ref, *o_refs):
    x = x_ref[...]
    ms = jnp.mean(x * x, axis=-1, keepdims=True)
    y = x * lax.rsqrt(ms + RMS_EPS) * g_ref[...]
    for o_ref in o_refs:
        o_ref[...] = y.astype(o_ref.dtype)


def _rmsnorm2_kernel(xa_ref, xb_ref, g_ref, o_ref, *, na):
    first = pl.program_id(0) < na
    x = jnp.where(first, xa_ref[...], xb_ref[...])
    ms = jnp.mean(x * x, axis=-1, keepdims=True)
    o_ref[...] = (x * lax.rsqrt(ms + RMS_EPS) * g_ref[...]).astype(o_ref.dtype)


def _rmsnorm_stacked(xa, xb, g, out_dtype, tm=512):
    d = xa.shape[1]
    na, nb = xa.shape[0] // tm, xb.shape[0] // tm
    assert xa.shape[0] % tm == 0 and xb.shape[0] % tm == 0
    return pl.pallas_call(
        functools.partial(_rmsnorm2_kernel, na=na),
        grid=(na + nb,),
        in_specs=[_segment_spec((tm, d), 0, na), _segment_spec((tm, d), na, nb),
                  pl.BlockSpec((1, d), lambda i: (0, 0))],
        out_specs=pl.BlockSpec((tm, d), lambda i: (i, 0)),
        out_shape=jax.ShapeDtypeStruct((xa.shape[0] + xb.shape[0], d), out_dtype),
        compiler_params=_cparams(("parallel",)),
        name="rmsnorm_stacked",
    )(xa, xb, g.reshape(1, d).astype(F32))


def _rmsnorm(x, g, out_dtypes, tm=512):
    t, d = x.shape
    tm = min(tm, t)
    spec = pl.BlockSpec((tm, d), lambda i: (i, 0))
    return pl.pallas_call(
        _rmsnorm_kernel,
        grid=(t // tm,),
        in_specs=[spec, pl.BlockSpec((1, d), lambda i: (0, 0))],
        out_specs=[spec for _ in out_dtypes],
        out_shape=[jax.ShapeDtypeStruct((t, d), dt) for dt in out_dtypes],
        compiler_params=_cparams(("parallel",)),
        name="rmsnorm",
    )(x, g.reshape(1, d).astype(F32))


def _group_rms_scale(x, gain):
    lane = lax.broadcasted_iota(I32, x.shape, 1)
    lo = lane < ATT_QKDIM
    x2 = x * x
    s_lo = jnp.sum(jnp.where(lo, x2, 0.0), axis=-1, keepdims=True)
    s_hi = jnp.sum(jnp.where(lo, 0.0, x2), axis=-1, keepdims=True)
    ms = jnp.where(lo, s_lo, s_hi) * (1.0 / ATT_QKDIM)
    return x * lax.rsqrt(ms + RMS_EPS) * gain


def _mm_kernel(a_ref, b_ref, *rest, mode):
    acc = jnp.dot(a_ref[...], b_ref[...], preferred_element_type=F32)
    if mode == "plain":
        (o_ref,) = rest
        o_ref[...] = acc.astype(o_ref.dtype)
    elif mode == "qknorm":
        g_ref, o_ref = rest
        for c in range(acc.shape[1] // LANES):
            sl = slice(c * LANES, (c + 1) * LANES)
            o_ref[:, sl] = _group_rms_scale(acc[:, sl], g_ref[:, sl]).astype(o_ref.dtype)
    else:
        raise ValueError(mode)


def _matmul(a, b, n_cols, col_off, out_dtype, mode="plain", extra=None, tm=1024, tn=1024):
    m, k = a.shape
    tm = min(tm, m)
    tn = min(tn, n_cols)
    assert col_off % tn == 0 and n_cols % tn == 0 and m % tm == 0
    off = col_off // tn
    in_specs = [pl.BlockSpec((tm, k), lambda i, j: (i, 0)),
                pl.BlockSpec((k, tn), lambda i, j: (0, j + off))]
    args = [a, b]
    if mode == "qknorm":
        in_specs.append(pl.BlockSpec((1, tn), lambda i, j: (0, j)))
        args.append(extra)
    return pl.pallas_call(
        functools.partial(_mm_kernel, mode=mode),
        grid=(m // tm, n_cols // tn),
        in_specs=in_specs,
        out_specs=pl.BlockSpec((tm, tn), lambda i, j: (i, j)),
        out_shape=jax.ShapeDtypeStruct((m, n_cols), out_dtype),
        compiler_params=_cparams(("parallel", "arbitrary")),
        name="matmul_" + mode,
    )(*args)


def _outproj_kernel(att_a, hy_a, x_a, att_b, hy_b, x_b, w_ref, o_ref, *, na):
    kw = att_a.shape[1]

    def run(att_ref, hy_ref, x_ref):
        acc = jnp.dot(att_ref[...], w_ref[:kw, :], preferred_element_type=F32)
        acc = acc + jnp.dot(hy_ref[...].astype(BF16), w_ref[kw:, :], preferred_element_type=F32)
        o_ref[...] = acc + x_ref[...]

    first = pl.program_id(1) < na
    pl.when(first)(lambda: run(att_a, hy_a, x_a))
    pl.when(jnp.logical_not(first))(lambda: run(att_b, hy_b, x_b))


def _outproj_stacked(seg_a, seg_b, w, tm=512, tn=1024):
    n_out = w.shape[1]
    na, nb = seg_a[0].shape[0] // tm, seg_b[0].shape[0] // tm
    assert seg_a[0].shape[0] % tm == 0 and seg_b[0].shape[0] % tm == 0 and n_out % tn == 0

    def specs(seg, start, count):
        att, hy, x = seg
        row = lambda j, i: (jnp.clip(i - start, 0, count - 1), 0)
        return [pl.BlockSpec((tm, att.shape[1]), row), pl.BlockSpec((tm, hy.shape[1]), row),
                pl.BlockSpec((tm, tn), lambda j, i: (jnp.clip(i - start, 0, count - 1), j))]

    return pl.pallas_call(
        functools.partial(_outproj_kernel, na=na),
        grid=(n_out // tn, na + nb),
        in_specs=specs(seg_a, 0, na) + specs(seg_b, na, nb) + [pl.BlockSpec((w.shape[0], tn), lambda j, i: (0, j))],
        out_specs=pl.BlockSpec((tm, tn), lambda j, i: (i, j)),
        out_shape=jax.ShapeDtypeStruct(((na + nb) * tm, n_out), F32),
        compiler_params=_cparams(("arbitrary", "parallel")),
        name="out_projection",
    )(*seg_a, *seg_b, w)


def _mm_nt_kernel(w_ref, a_ref, o_ref):
    o_ref[...] = lax.dot_general(w_ref[...], a_ref[...], (((1,), (1,)), ((), ())),
                                 preferred_element_type=F32).astype(o_ref.dtype)


def _matmul_nt(wt, a, out_dtype, tm=1024):
    n, k = wt.shape
    m = a.shape[0]
    tm = min(tm, m)
    return pl.pallas_call(
        _mm_nt_kernel,
        grid=(m // tm,),
        in_specs=[pl.BlockSpec((n, k), lambda i: (0, 0)), pl.BlockSpec((tm, k), lambda i: (i, 0))],
        out_specs=pl.BlockSpec((n, tm), lambda i: (0, i)),
        out_shape=jax.ShapeDtypeStruct((n, m), out_dtype),
        compiler_params=_cparams(("parallel",)),
        name="matmul_nt",
    )(wt, a)


def _alibi_columns(slopes, tq, tk):
    return [_alibi_side(slopes, tq, True), _alibi_side(slopes, tk, False)]


def _alibi_side(slopes, n, query_side):
    pos = jnp.arange(n, dtype=F32)
    val = (slopes.astype(F32) * LOG2E)[:, None] * pos[None, :]

    def pieces(x):
        p1 = x.astype(BF16)
        r1 = x - p1.astype(F32)
        p2 = r1.astype(BF16)
        p3 = (r1 - p2.astype(F32)).astype(BF16)
        return [p1, p2, p3]

    ones = [jnp.ones_like(val, BF16)] * 3
    six = jnp.stack(pieces(-val) + ones if query_side else ones + pieces(val), axis=-1)
    pad = jnp.zeros(val.shape + (ATT_QKDIM - 6,), BF16)
    return jnp.concatenate([six, pad, six, pad], axis=-1)


def _attn_kernel(slope_ref, lam_ref, q_ref, k_ref, vt_ref, aq_ref, ak_ref, g_ref, o_ref,
                 m_ref, l_ref, acc_ref, *, tq, tk, nk, hb, out_scale):
    hg = pl.program_id(1)
    i = pl.program_id(2)
    j = pl.program_id(3)

    @pl.when(j == 0)
    def _():
        m_ref[...] = jnp.full(m_ref.shape, -jnp.inf, F32)
        l_ref[...] = jnp.zeros(l_ref.shape, F32)
        acc_ref[...] = jnp.zeros(acc_ref.shape, F32)

    q_first = lax.broadcasted_iota(I32, (tq, LANES), 1) < ATT_QKDIM
    k_first = lax.broadcasted_iota(I32, (tk, LANES), 1) < ATT_QKDIM
    nt = (((1,), (1,)), ((), ()))

    def update(hh, scores, shift):
        vt = vt_ref[hh * LANES:(hh + 1) * LANES, :]
        for c, s in enumerate(scores):
            m_old = m_ref[hh, c]
            m_new = jnp.maximum(m_old, jnp.max(s, axis=0, keepdims=True) + shift)
            alpha = jnp.exp2(m_old - m_new)
            p = jnp.exp2(s - (m_new - shift))
            l_ref[hh, c] = alpha * l_ref[hh, c] + jnp.sum(p, axis=0, keepdims=True)
            acc_ref[hh, c] = alpha * acc_ref[hh, c] + jnp.dot(vt, p.astype(BF16),
                                                              preferred_element_type=F32)
            m_ref[hh, c] = m_new

    keys_before = i * tq >= (j + 1) * tk
    keys_after = (i + 1) * tq <= j * tk
    off_diagonal = jnp.logical_or(keys_before, keys_after)

    @pl.when(off_diagonal)
    def _():
        sign = jnp.where(keys_after, -1.0, 1.0).astype(BF16)
        gap = jnp.abs(i * tq - j * tk).astype(F32)
        for hh in range(hb):
            sl = slice(hh * LANES, (hh + 1) * LANES)
            q = q_ref[:, sl]
            k = k_ref[:, sl]
            aq = aq_ref[hh]
            ak = ak_ref[hh] * sign
            s0 = lax.dot_general(jnp.where(k_first, k, ak), jnp.where(q_first, q, aq), nt,
                                 preferred_element_type=F32)
            s1 = lax.dot_general(jnp.where(k_first, ak, k), jnp.where(q_first, aq, q), nt,
                                 preferred_element_type=F32)
            update(hh, (s0, s1), -(slope_ref[hg * hb + hh] * LOG2E) * gap)

    @pl.when(jnp.logical_not(off_diagonal))
    def _():
        kpos = lax.broadcasted_iota(I32, (tk, tq), 0) + j * tk
        qpos = lax.broadcasted_iota(I32, (tk, tq), 1) + i * tq
        dist = jnp.abs(kpos - qpos).astype(F32)
        for hh in range(hb):
            sl = slice(hh * LANES, (hh + 1) * LANES)
            q = q_ref[:, sl]
            k = k_ref[:, sl]
            zero = jnp.zeros_like(q)
            bias = dist * (-(slope_ref[hg * hb + hh] * LOG2E))
            s0 = lax.dot_general(k, jnp.where(q_first, q, zero), nt, preferred_element_type=F32) + bias
            s1 = lax.dot_general(k, jnp.where(q_first, zero, q), nt, preferred_element_type=F32) + bias
            update(hh, (s0, s1), 0.0)

    @pl.when(j == nk - 1)
    def _():
        for hh in range(hb):
            o = acc_ref[hh, 0] / l_ref[hh, 0] - lam_ref[0] * (acc_ref[hh, 1] / l_ref[hh, 1])
            ms = jnp.mean(o * o, axis=0, keepdims=True)
            y = o * lax.rsqrt(ms + RMS_EPS) * (g_ref[...] * out_scale)
            o_ref[:, hh * LANES:(hh + 1) * LANES] = y.T.astype(o_ref.dtype)


def _attention(qk, vt, row_off, slopes, lam, sub_g, batch, seq, out_scale, tq=512, tk=1024, hb=4):
    tq = min(tq, seq)
    tk = min(tk, seq // 4)
    nq, nk = seq // tq, seq // tk
    assert row_off % tq == 0 and row_off % tk == 0 and ATT_HEADS % hb == 0
    oq, ok = row_off // tq, row_off // tk
    ng = ATT_HEADS // hb
    aq, ak = _alibi_columns(slopes, tq, tk)
    kern = functools.partial(_attn_kernel, tq=tq, tk=tk, nk=nk, hb=hb, out_scale=out_scale)
    smem = pl.BlockSpec(memory_space=pltpu.SMEM)
    return pl.pallas_call(
        kern,
        grid=(batch, ng, nq, nk),
        in_specs=[smem, smem,
                  pl.BlockSpec((tq, hb * LANES), lambda b, h, i, j: (oq + b * nq + i, h)),
                  pl.BlockSpec((tk, hb * LANES), lambda b, h, i, j: (ok + b * nk + j, ng + h)),
                  pl.BlockSpec((hb * LANES, tk), lambda b, h, i, j: (h, ok + b * nk + j)),
                  pl.BlockSpec((hb, tq, LANES), lambda b, h, i, j: (h, 0, 0)),
                  pl.BlockSpec((hb, tk, LANES), lambda b, h, i, j: (h, 0, 0)),
                  pl.BlockSpec((LANES, 1), lambda b, h, i, j: (0, 0))],
        out_specs=pl.BlockSpec((tq, hb * LANES), lambda b, h, i, j: (b * nq + i, h)),
        out_shape=jax.ShapeDtypeStruct((batch * seq, ATT_HEADS * ATT_VDIM), BF16),
        scratch_shapes=[pltpu.VMEM((hb, 2, 1, tq), F32), pltpu.VMEM((hb, 2, 1, tq), F32),
                        pltpu.VMEM((hb, 2, LANES, tq), F32)],
        compiler_params=_cparams(("parallel", "parallel", "parallel", "arbitrary")),
        name="diff_attention",
    )(slopes, lam, qk, qk, vt, aq, ak, sub_g.reshape(LANES, 1).astype(F32))


def _shortconv_kernel(z_ref, w_ref, b_ref, o_ref):
    z = z_ref[...]
    n = z.shape[0]
    row = lax.broadcasted_iota(I32, z.shape, 0)
    prev = jnp.where(row == 0, 0.0, pltpu.roll(z, 1, axis=0))
    nxt = jnp.where(row == n - 1, 0.0, pltpu.roll(z, n - 1, axis=0))
    w = w_ref[...]
    o_ref[...] = prev * w[0:1] + z * w[1:2] + nxt * w[2:3] + b_ref[...]


def _shortconv(zh, row_off, conv_w, conv_b, batch, seq, cb=256):
    c3 = zh.shape[1]
    c = c3 // 3
    ncb = c // cb
    assert row_off % seq == 0
    ob = row_off // seq
    out = pl.pallas_call(
        _shortconv_kernel,
        grid=(batch, 3, ncb),
        in_specs=[pl.BlockSpec((seq, cb), lambda b, p, j: (ob + b, p * ncb + j)),
                  pl.BlockSpec((3, cb), lambda b, p, j: (0, p * ncb + j)),
                  pl.BlockSpec((1, cb), lambda b, p, j: (0, p * ncb + j))],
        out_specs=pl.BlockSpec((None, None, seq, cb), lambda b, p, j: (p, b, 0, j)),
        out_shape=jax.ShapeDtypeStruct((3, batch, seq, c), F32),
        compiler_params=_cparams(("parallel", "parallel", "parallel")),
        name="hyena_shortconv",
    )(zh, conv_w.astype(F32), conv_b.reshape(1, c3).astype(F32))
    return out


def _filter_kernel(z_ref, w1_ref, b1_ref, w2_ref, b2_ref, w3_ref, b3_ref, fr_ref, w4_ref,
                   t_ref, d_ref, o_ref, h_ref, *, tl):
    i = pl.program_id(0)
    g = pl.program_id(1)
    hi = lax.Precision.HIGHEST

    @pl.when(g == 0)
    def _():
        fr = fr_ref[...]
        h = jnp.sin(fr * (jnp.dot(z_ref[...], w1_ref[...], precision=hi, preferred_element_type=F32) + b1_ref[...]))
        h = jnp.sin(fr * (jnp.dot(h, w2_ref[...], precision=hi, preferred_element_type=F32) + b2_ref[...]))
        h_ref[...] = jnp.sin(fr * (jnp.dot(h, w3_ref[...], precision=hi, preferred_element_type=F32) + b3_ref[...]))

    f = jnp.dot(h_ref[...].astype(BF16), w4_ref[...], preferred_element_type=F32)
    f = f * jnp.exp(-t_ref[...] * d_ref[...])
    row = lax.broadcasted_iota(I32, f.shape, 0) + i * tl
    drop = jnp.logical_and(row == 0, g % 2 == 1)
    o_ref[...] = jnp.where(drop, 0.0, f)


def _hyena_filter_signals(seq, w1, b1, w2, b2, w3, b3, w4, freq, n_ch, tl=512):
    t = jnp.linspace(0.0, 1.0, seq, dtype=F32)[:, None]
    w = 2.0 * math.pi * jnp.arange(seq, dtype=F32)[:, None] / seq
    f = jnp.linspace(1e-4, FILTER_BANDS - 1, FILTER_BANDS, dtype=F32)[None, :]
    z = jnp.concatenate([t, jnp.cos(f * w), -jnp.sin(f * w)], axis=-1)
    deltas = jnp.abs(jnp.linspace(math.log(DECAY_FAST) / DECAY_TARGET,
                                  math.log(DECAY_SLOW) / DECAY_TARGET, n_ch, dtype=F32))[None, :]
    hid = w1.shape[1]
    emb = LANES
    z = jnp.pad(z, ((0, 0), (0, emb - z.shape[1])))
    w1 = jnp.pad(w1.astype(F32), ((0, emb - w1.shape[0]), (0, 0)))
    tl = min(tl, seq)
    full = lambda shape: pl.BlockSpec(shape, lambda i, g: tuple(0 for _ in shape))
    return pl.pallas_call(
        functools.partial(_filter_kernel, tl=tl),
        grid=(seq // tl, 2 * HYENA_ORDER),
        in_specs=[pl.BlockSpec((tl, emb), lambda i, g: (i, 0)),
                  full((emb, hid)), full((1, hid)), full((hid, hid)), full((1, hid)),
                  full((hid, hid)), full((1, hid)), full((1, hid)),
                  pl.BlockSpec((hid, n_ch), lambda i, g: (0, g)),
                  pl.BlockSpec((tl, 1), lambda i, g: (i, 0)),
                  full((1, n_ch))],
        out_specs=pl.BlockSpec((None, tl, n_ch), lambda i, g: (g, i, 0)),
        out_shape=jax.ShapeDtypeStruct((2 * HYENA_ORDER, seq, n_ch), F32),
        scratch_shapes=[pltpu.VMEM((tl, hid), F32)],
        compiler_params=_cparams(("parallel", "arbitrary")),
        name="hyena_filter_mlp",
    )(z, w1.astype(F32), b1.reshape(1, hid).astype(F32), w2.astype(F32), b2.reshape(1, hid).astype(F32),
      w3.astype(F32), b3.reshape(1, hid).astype(F32), freq.reshape(1, hid).astype(F32), w4.astype(BF16),
      t, deltas)


def _dft_tables(r):
    n = r * r
    ang1 = (2.0 * np.pi / r) * ((np.arange(r)[:, None] * np.arange(r // 2)[None, :]) % r)
    eye = np.eye(N1_BLOCK)
    f1 = jnp.asarray(np.kron(np.concatenate([np.cos(ang1), -np.sin(ang1)], axis=0), eye), BF16)
    g3 = jnp.asarray(np.kron(np.concatenate([np.cos(ang1.T), -np.sin(ang1.T)], axis=1) / n, eye), BF16)
    k2 = jnp.arange(r, dtype=I32)
    k1 = jnp.arange(r, dtype=I32)
    n1 = jnp.arange(r, dtype=I32)
    ang_a = (2.0 * math.pi / r) * ((k1[:, None] * n1[None, :]) % r).astype(F32)
    ang_b = (2.0 * math.pi / n) * (k2[:, None] * n1[None, :]).astype(F32)
    ca, sa = jnp.cos(ang_a)[None], jnp.sin(ang_a)[None]
    cb, sb = jnp.cos(ang_b), jnp.sin(ang_b)

    def blocks(cb3, sb3):
        return (ca * cb3 - sa * sb3).astype(BF16), (-(sa * cb3 + ca * sb3)).astype(BF16)

    mr, mi_ = blocks(cb[:, None, :], sb[:, None, :])
    mf = jnp.concatenate([jnp.concatenate([mr, -mi_], axis=2),
                          jnp.concatenate([mi_, mr], axis=2)], axis=1)
    mr_t, mi_t = blocks(cb[:, :, None], sb[:, :, None])
    minv = jnp.concatenate([jnp.concatenate([mr_t, mi_t], axis=2),
                            jnp.concatenate([-mi_t, mr_t], axis=2)], axis=1)
    return f1, mf, minv, g3


N1_BLOCK = SUBLANES


def _pack_complex(re, im):
    rb = lax.bitcast_convert_type(re.astype(BF16).astype(F32), U32)
    ib = lax.bitcast_convert_type(im.astype(BF16).astype(F32), U32)
    return (rb >> 16) | ib


def _unpack_complex_rows(w):
    re, im = _unpack(w)
    return jnp.concatenate([re, im], axis=0).astype(BF16)


def _fft1_kernel(f_ref, x_ref, o_ref, *, r):
    n_ch = x_ref.shape[-1]
    x = x_ref[...].reshape((r // 2) * N1_BLOCK, n_ch).astype(BF16)
    res = jnp.dot(f_ref[...], x, preferred_element_type=F32)
    half = r * N1_BLOCK
    o_ref[...] = _pack_complex(res[:half], res[half:]).reshape(r, N1_BLOCK, n_ch)


def _fft_stage1(x4, part, f1, r, n_ch):
    nb = x4.shape[1]
    xv = x4.reshape(x4.shape[0], nb, r // 2, r, n_ch)
    return pl.pallas_call(
        functools.partial(_fft1_kernel, r=r),
        grid=(nb, r // N1_BLOCK),
        in_specs=[pl.BlockSpec(f1.shape, lambda b, j: (0, 0)),
                  pl.BlockSpec((None, None, r // 2, N1_BLOCK, n_ch), lambda b, j: (part, b, 0, j, 0))],
        out_specs=pl.BlockSpec((None, r, N1_BLOCK, n_ch), lambda b, j: (b, 0, j, 0)),
        out_shape=jax.ShapeDtypeStruct((nb, r, r, n_ch), U32),
        compiler_params=_cparams(("parallel", "parallel")),
        name="hyena_dft_stage1",
    )(f1, xv)


def _k2_block(r):
    return max(1, min(r, (8 * LANES) // r))


def _filter_spec_kernel(mf_ref, bf_ref, bb_ref, o_ref, *, r):
    for kk in range(mf_ref.shape[0]):
        m = mf_ref[kk]
        xf = jnp.dot(m, _unpack_complex_rows(bf_ref[kk]), preferred_element_type=F32)
        xb = jnp.dot(m, _unpack_complex_rows(bb_ref[kk]), preferred_element_type=F32)
        o_ref[kk] = _pack_complex(xf[:r] + xb[:r], xf[r:] - xb[r:])


def _filter_spectrum(b1, mf, r, n_ch):
    kb = _k2_block(r)
    blk = lambda sel: pl.BlockSpec((None, kb, r, n_ch), lambda o, k: (2 * o + sel, k, 0, 0))
    return pl.pallas_call(
        functools.partial(_filter_spec_kernel, r=r),
        grid=(HYENA_ORDER, r // kb),
        in_specs=[pl.BlockSpec((kb, 2 * r, 2 * r), lambda o, k: (k, 0, 0)), blk(0), blk(1)],
        out_specs=pl.BlockSpec((None, kb, r, n_ch), lambda o, k: (o, k, 0, 0)),
        out_shape=jax.ShapeDtypeStruct((HYENA_ORDER, r, r, n_ch), U32),
        compiler_params=_cparams(("parallel", "parallel")),
        name="hyena_filter_spectrum",
    )(mf, b1, b1)


def _fft2_kernel(mf_ref, mi_ref, b_ref, h_ref, o_ref, *, r):
    for kk in range(mf_ref.shape[0]):
        x = jnp.dot(mf_ref[kk], _unpack_complex_rows(b_ref[kk]), preferred_element_type=F32)
        xr, xi = x[:r], x[r:]
        hr, hi = _unpack(h_ref[kk])
        y = jnp.concatenate([xr * hr - xi * hi, xr * hi + xi * hr], axis=0).astype(BF16)
        c = jnp.dot(mi_ref[kk], y, preferred_element_type=F32)
        o_ref[kk] = _pack_complex(c[:r], c[r:])


def _fft_stage2(b1, h, order, mf, minv, r, n_ch):
    nb = b1.shape[0]
    kb = _k2_block(r)
    blk = pl.BlockSpec((None, kb, r, n_ch), lambda k, b: (b, k, 0, 0))
    return pl.pallas_call(
        functools.partial(_fft2_kernel, r=r),
        grid=(r // kb, nb),
        in_specs=[pl.BlockSpec((kb, 2 * r, 2 * r), lambda k, b: (k, 0, 0)),
                  pl.BlockSpec((kb, 2 * r, 2 * r), lambda k, b: (k, 0, 0)),
                  blk,
                  pl.BlockSpec((None, kb, r, n_ch), lambda k, b: (order, k, 0, 0))],
        out_specs=blk,
        out_shape=jax.ShapeDtypeStruct((nb, r, r, n_ch), U32),
        compiler_params=_cparams(("parallel", "parallel")),
        name="hyena_dft_stage2",
    )(mf, minv, b1, h)


def _fft3_kernel(g_ref, c_ref, gate_ref, s_ref, bias_ref, ng_ref, *rest, final):
    if final:
        (o_ref,) = rest
    else:
        f_ref, o_ref, b_ref = rest
    r, nb, n_ch = c_ref.shape
    cc = _unpack_complex_rows(c_ref[...].reshape(r * nb, n_ch))
    y = jnp.dot(g_ref[...], cc, preferred_element_type=F32)
    rows = (r // 2) * nb
    s_new = gate_ref[...].reshape(rows, n_ch) * (y + s_ref[...].reshape(rows, n_ch) * bias_ref[...])
    if final:
        ms = jnp.mean(s_new * s_new, axis=-1, keepdims=True)
        s_new = s_new * lax.rsqrt(ms + RMS_EPS) * ng_ref[...]
    o_ref[...] = s_new.reshape(r // 2, nb, n_ch)
    if not final:
        res = jnp.dot(f_ref[...], s_new.astype(BF16), preferred_element_type=F32)
        half = r * nb
        b_ref[...] = _pack_complex(res[:half], res[half:]).reshape(r, nb, n_ch)


def _fft_stage3(c2, g3, z4, gate_part, s4, s_part, bias, norm_g, f1, r, n_ch):
    final = f1 is None
    nb = c2.shape[0]
    zv = z4.reshape(z4.shape[0], nb, r // 2, r, n_ch)
    sv = s4.reshape(s4.shape[0], nb, r // 2, r, n_ch)
    dspec = lambda part: pl.BlockSpec((None, None, r // 2, N1_BLOCK, n_ch), lambda b, j: (part, b, 0, j, 0))
    packed = pl.BlockSpec((None, r, N1_BLOCK, n_ch), lambda b, j: (b, 0, j, 0))
    s_spec = pl.BlockSpec((None, r // 2, N1_BLOCK, n_ch), lambda b, j: (b, 0, j, 0))
    s_shape = jax.ShapeDtypeStruct((nb, r // 2, r, n_ch), F32)
    in_specs = [pl.BlockSpec(g3.shape, lambda b, j: (0, 0)), packed, dspec(gate_part), dspec(s_part),
                pl.BlockSpec((1, n_ch), lambda b, j: (0, 0)), pl.BlockSpec((1, n_ch), lambda b, j: (0, 0))]
    args = [g3, c2, zv, sv, bias.reshape(1, n_ch).astype(F32), norm_g.reshape(1, n_ch).astype(F32)]
    if final:
        out_specs, out_shape = s_spec, s_shape
    else:
        in_specs.append(pl.BlockSpec(f1.shape, lambda b, j: (0, 0)))
        args.append(f1)
        out_specs = [s_spec, packed]
        out_shape = [s_shape, jax.ShapeDtypeStruct((nb, r, r, n_ch), U32)]
    return pl.pallas_call(
        functools.partial(_fft3_kernel, final=final),
        grid=(nb, r // N1_BLOCK),
        in_specs=in_specs,
        out_specs=out_specs,
        out_shape=out_shape,
        compiler_params=_cparams(("parallel", "parallel")),
        name="hyena_dft_stage3",
    )(*args)


def _hyena(zh, row_off, batch, seq, conv_w, conv_b, filt, fft_bias, out_g):
    n_ch = zh.shape[1] // 3
    r = int(round(math.sqrt(2 * seq)))
    assert r * r == 2 * seq and r % 16 == 0
    f1, mf, minv, g3 = _dft_tables(r)
    sig = _hyena_filter_signals(seq, *filt, n_ch=n_ch)
    hb1 = _fft_stage1(sig[None], 0, f1, r, n_ch)
    h = _filter_spectrum(hb1, mf, r, n_ch)
    z4 = _shortconv(zh, row_off, conv_w, conv_b, batch, seq)
    s4, s_part = z4, 2
    b1 = _fft_stage1(s4, s_part, f1, r, n_ch)
    for o in range(HYENA_ORDER):
        c2 = _fft_stage2(b1, h, o, mf, minv, r, n_ch)
        if o < HYENA_ORDER - 1:
            s, b1 = _fft_stage3(c2, g3, z4, o, s4, s_part, fft_bias[o], out_g, f1, r, n_ch)
        else:
            s = _fft_stage3(c2, g3, z4, o, s4, s_part, fft_bias[o], out_g, None, r, n_ch)
        s4, s_part = s.reshape(1, batch, seq, n_ch), 0
    return s4.reshape(batch * seq, n_ch)


def _extract_top(s, key, count):
    vals, keys = [], []
    for _ in range(count):
        m = jnp.max(s, axis=0, keepdims=True)
        kmin = jnp.min(jnp.where(s == m, key, jnp.inf), axis=0, keepdims=True)
        s = jnp.where(key == kmin, -jnp.inf, s)
        vals.append(m)
        keys.append(kmin)
    return vals, keys


def _peer_topk_kernel(q_ref, keys_ref, eid_ref, gate_ref):
    t = q_ref.shape[0]
    nk = PEER_NKEYS
    q = q_ref[...]
    row_key = lax.broadcasted_iota(I32, (nk, t), 0).astype(F32)
    tops = []
    for c in range(2):
        s = lax.dot_general(keys_ref[c], q[:, c * nk:(c + 1) * nk], (((1,), (1,)), ((), ())),
                            preferred_element_type=F32)
        tops.append(_extract_top(s, row_key, PEER_TOPK))
    (v1, i1), (v2, i2) = tops
    rows16 = lax.broadcasted_iota(I32, (PEER_TOPK, t), 0)
    v2a = jnp.zeros((PEER_TOPK, t), F32)
    i2a = jnp.zeros((PEER_TOPK, t), F32)
    for j in range(PEER_TOPK):
        v2a = jnp.where(rows16 == j, v2[j], v2a)
        i2a = jnp.where(rows16 == j, i2[j], i2a)
    n_exp = float(nk * nk)
    half = PEER_TOPK // 2
    rows8 = lax.broadcasted_iota(I32, (half, t), 0)
    pos8 = rows8.astype(F32)
    v2h, i2h = v2a[:half], i2a[:half]
    cand = [v1[0] + v2a]
    ckey = [rows16.astype(F32) * n_exp + (i1[0] * float(nk) + i2a)]
    for i in range(1, half):
        cand.append(jnp.where(rows8 < PEER_TOPK // (i + 1), v1[i] + v2h, -jnp.inf))
        ckey.append((pos8 + float(i * PEER_TOPK)) * n_exp + (i1[i] * float(nk) + i2h))
    v1t = jnp.zeros((half, t), F32)
    i1t = jnp.zeros((half, t), F32)
    for r in range(half):
        v1t = jnp.where(rows8 == r, v1[half + r], v1t)
        i1t = jnp.where(rows8 == r, i1[half + r], i1t)
    cand.append(v1t + v2[0])
    ckey.append((pos8 + float(half)) * (PEER_TOPK * n_exp) + (i1t * float(nk) + i2[0]))
    tv, tk_ = _extract_top(jnp.concatenate(cand, axis=0), jnp.concatenate(ckey, axis=0), PEER_TOPK)
    denom = jnp.zeros((1, t), F32)
    es = []
    for k in range(PEER_TOPK):
        e = jnp.exp(tv[k] - tv[0])
        es.append(e)
        denom = denom + e
    eid = jnp.zeros((PEER_TOPK, t), F32)
    gate = jnp.zeros((PEER_TOPK, t), F32)
    for k in range(PEER_TOPK):
        pos = jnp.floor(tk_[k] * (1.0 / n_exp))
        eid = jnp.where(rows16 == k, tk_[k] - pos * n_exp, eid)
        gate = jnp.where(rows16 == k, es[k] / denom, gate)
    eid_ref[...] = eid.astype(I32)
    gate_ref[...] = gate


def _peer_topk(q, keys, tm=1024):
    t = q.shape[0]
    tm = min(tm, t)
    out_spec = pl.BlockSpec((PEER_TOPK, tm), lambda i, h: (h, i))
    return pl.pallas_call(
        _peer_topk_kernel,
        grid=(t // tm, PEER_HEADS),
        in_specs=[pl.BlockSpec((tm, 2 * PEER_NKEYS), lambda i, h: (i, h)),
                  pl.BlockSpec((None, 2, PEER_NKEYS, PEER_NKEYS), lambda i, h: (h, 0, 0, 0))],
        out_specs=[out_spec, out_spec],
        out_shape=[jax.ShapeDtypeStruct((PEER_HEADS * PEER_TOPK, t), I32),
                   jax.ShapeDtypeStruct((PEER_HEADS * PEER_TOPK, t), F32)],
        compiler_params=_cparams(("parallel", "parallel")),
        name="peer_topk",
    )(q, keys)


def _pack_table(tab):
    e, d = tab.shape
    assert d == 2 * SUBLANES * LANES
    def bf16_bits(x):
        w = lax.bitcast_convert_type(x.astype(F32), U32)
        return (w + (jnp.uint32(0x7FFF) + ((w >> 16) & jnp.uint32(1)))) >> 16

    packed = bf16_bits(tab[:, :d // 2]) | (bf16_bits(tab[:, d // 2:]) << 16)
    return packed.reshape(e * SUBLANES, LANES)


def _unpack(w):
    lo = lax.bitcast_convert_type(w << 16, F32)
    hi = lax.bitcast_convert_type(w & jnp.uint32(0xFFFF0000), F32)
    return lo, hi


_BITREV8 = (0, 4, 2, 6, 1, 5, 3, 7)


def _sublane_fold8(parts):
    sub = lax.broadcasted_iota(I32, (2 * SUBLANES, LANES), 0) // 2

    def rolled(a, shift):
        return pltpu.bitcast(pltpu.roll(pltpu.bitcast(a, U32), shift, axis=0), BF16)

    lvl = [parts[_BITREV8[r]] for r in range(8)]
    for shift, mask in ((4, sub < 4), (2, (sub % 4) < 2), (1, (sub % 2) < 1)):
        nxt = []
        for a, b in zip(lvl[0::2], lvl[1::2]):
            nxt.append(jnp.where(mask, a + rolled(a, SUBLANES - shift), b + rolled(b, shift)))
        lvl = nxt
    return lvl[0]


def _pair_ranges(half):
    if half == 0:
        return (0, PEER_WINDOW), (PEER_WINDOW, PEER_PAIRS)
    return (PEER_PAIRS - PEER_WINDOW, PEER_PAIRS), (0, PEER_PAIRS - PEER_WINDOW)


def _overflow(n_low, half):
    return n_low > PEER_WINDOW if half == 0 else n_low < PEER_PAIRS - PEER_WINDOW


def _table_rows(tab_ref, off):
    return _unpack(tab_ref[pl.ds(pl.multiple_of(off, SUBLANES), SUBLANES), :])


def _peer_u_kernel(off_ref, nlow_ref, tab_ref, x_ref, o_ref, extra_ref, *, tm, half):
    lane = lax.broadcasted_iota(I32, (SUBLANES, LANES), 1)
    sub = lax.broadcasted_iota(I32, (SUBLANES, LANES), 0)
    lane_grp = lax.shift_right_logical(lane, 3)
    diag = sub == (lane & (SUBLANES - 1))
    main, rest = _pair_ranges(half)

    def folded(t, p0, p1):
        xb = pltpu.bitcast(x_ref[t], BF16)
        out = []
        for g in range(p0 // SUBLANES, p1 // SUBLANES):
            parts = []
            for r in range(SUBLANES):
                off = pl.multiple_of(off_ref[t, g * SUBLANES + r], SUBLANES)
                parts.append(pltpu.bitcast(tab_ref[pl.ds(off, SUBLANES), :], BF16) * xb)
            out.append(pltpu.bitcast(_sublane_fold8(parts), U32))
        return out

    def lane_sums(folds, p0):
        mat = jnp.zeros((SUBLANES, LANES), F32)
        for i, f in enumerate(folds):
            lo, hi = _unpack(f)
            mat = jnp.where(lane_grp == p0 // SUBLANES + i, jnp.sum(lo + hi, axis=-1, keepdims=True), mat)
        return jnp.sum(jnp.where(diag, mat, 0.0), axis=0, keepdims=True)

    def finish(t, folds):
        o_ref[pl.ds(t, 1), :] = lane_sums(folds, main[0]) + extra_ref[pl.ds(t, 1), :]

    def token(t, prev):
        cur = folded(t, *main)
        finish(jnp.maximum(t - 1, 0), prev)
        extra_ref[pl.ds(t, 1), :] = jnp.zeros((1, LANES), F32)

        @pl.when(_overflow(nlow_ref[0, t], half))
        def _():
            extra_ref[pl.ds(t, 1), :] = lane_sums(folded(t, *rest), rest[0])

        return tuple(cur)

    extra_ref[pl.ds(0, 1), :] = jnp.zeros((1, LANES), F32)
    zeros = tuple(jnp.zeros((SUBLANES, LANES), U32) for _ in range((main[1] - main[0]) // SUBLANES))
    last = lax.fori_loop(0, tm, token, zeros)
    finish(tm - 1, last)


def _peer_u(off, nlow3, tab, x4, half, tm=128):
    t = off.shape[0]
    rows = PEER_HALF * SUBLANES
    return pl.pallas_call(
        functools.partial(_peer_u_kernel, tm=tm, half=half),
        grid=(t // tm,),
        in_specs=[pl.BlockSpec((tm, LANES), lambda i: (i, 0), memory_space=pltpu.SMEM),
                  pl.BlockSpec((None, 1, tm), lambda i: (i, 0, 0), memory_space=pltpu.SMEM),
                  pl.BlockSpec((rows, LANES), lambda i: (half, 0), pipeline_mode=pl.Buffered(1)),
                  pl.BlockSpec((tm, SUBLANES, LANES), lambda i: (i, 0, 0))],
        out_specs=pl.BlockSpec((tm, LANES), lambda i: (i, 0)),
        out_shape=jax.ShapeDtypeStruct((t, LANES), F32),
        scratch_shapes=[pltpu.VMEM((tm, LANES), F32)],
        compiler_params=_cparams(("arbitrary",)),
        name="peer_expert_scores",
    )(off, nlow3, tab, x4)


def _peer_order_kernel(eid_ref, eid_o, slot_o, off_o, nlow_o):
    eid = eid_ref[...].T
    lane = lax.broadcasted_iota(I32, eid.shape, 1)
    low = (eid < PEER_HALF).astype(I32)
    c_low = low
    shift = 1
    while shift < PEER_PAIRS:
        c_low = c_low + jnp.where(lane >= shift, pltpu.roll(c_low, shift, axis=1), 0)
        shift *= 2
    c_high = lane + 1 - c_low
    n_low = jnp.max(c_low, axis=1, keepdims=True)
    in_low = lane < n_low
    rank = jnp.where(in_low, lane + 1, lane + 1 - n_low)
    pos = jnp.zeros_like(lane)
    step = PEER_PAIRS // 2
    while step >= 1:
        probe = pos + (step - 1)
        count = jnp.where(in_low, jnp.take_along_axis(c_low, probe, axis=1),
                          jnp.take_along_axis(c_high, probe, axis=1))
        pos = jnp.where(count < rank, pos + step, pos)
        step //= 2
    eid_s = jnp.take_along_axis(eid, pos, axis=1)
    eid_o[...] = eid_s
    slot_o[...] = pos
    off_o[...] = (eid_s & (PEER_HALF - 1)) * SUBLANES
    nlow_o[...] = jnp.broadcast_to(n_low, eid.shape)


def _peer_order(eid_t, tm=1024):
    t = eid_t.shape[1]
    tm = min(tm, t)
    spec = pl.BlockSpec((tm, LANES), lambda i: (i, 0))
    return pl.pallas_call(
        _peer_order_kernel,
        grid=(t // tm,),
        in_specs=[pl.BlockSpec((LANES, tm), lambda i: (0, i))],
        out_specs=[spec] * 4,
        out_shape=[jax.ShapeDtypeStruct((t, LANES), I32)] * 4,
        compiler_params=_cparams(("parallel",)),
        name="peer_pair_order",
    )(eid_t)


def _peer_coef_kernel(a0_ref, a1_ref, eid_ref, slot_ref, gate_ref, c0_ref, c1_ref):
    low = eid_ref[...] < PEER_HALF
    a = jnp.where(low, a0_ref[...], a1_ref[...])
    gate = jnp.take_along_axis(gate_ref[...].T, slot_ref[...], axis=1)
    coef = gate * (0.5 * a * (1.0 + lax.erf(a * (1.0 / math.sqrt(2.0)))))
    c0_ref[...] = jnp.where(low, coef, 0.0)
    c1_ref[...] = jnp.where(low, 0.0, coef)


def _peer_coef(a0, a1, eid, slot, gate_t, tm=1024):
    t = eid.shape[0]
    tm = min(tm, t)
    spec = pl.BlockSpec((tm, LANES), lambda i: (i, 0))
    return pl.pallas_call(
        _peer_coef_kernel,
        grid=(t // tm,),
        in_specs=[spec, spec, spec, spec, pl.BlockSpec((LANES, tm), lambda i: (0, i))],
        out_specs=[spec, spec],
        out_shape=[jax.ShapeDtypeStruct((t, LANES), F32)] * 2,
        compiler_params=_cparams(("parallel",)),
        name="peer_coef",
    )(a0, a1, eid, slot, gate_t)


def _peer_v_kernel(off_ref, nlow_ref, coef_ref, tab_ref, base_ref, *o_refs, tm, half, n_first):
    n_acc = 4
    main, rest = _pair_ranges(half)

    def weighted(t, p0, p1):
        acc_lo = [jnp.zeros((SUBLANES, LANES), F32) for _ in range(n_acc)]
        acc_hi = [jnp.zeros((SUBLANES, LANES), F32) for _ in range(n_acc)]
        for p in range(p0, p1):
            c = coef_ref[t, p]
            lo, hi = _table_rows(tab_ref, off_ref[t, p])
            acc_lo[p % n_acc] = acc_lo[p % n_acc] + c * lo
            acc_hi[p % n_acc] = acc_hi[p % n_acc] + c * hi
        return ((acc_lo[0] + acc_lo[1]) + (acc_lo[2] + acc_lo[3]),
                (acc_hi[0] + acc_hi[1]) + (acc_hi[2] + acc_hi[3]))

    def run(o_ref):
        def store_row(t, lo, hi):
            o_ref[t, 0] = base_ref[t, 0] + lo
            o_ref[t, 1] = base_ref[t, 1] + hi

        def token(t, carry):
            lo, hi = weighted(t, *main)
            store_row(t, lo, hi)

            @pl.when(_overflow(nlow_ref[0, t], half))
            def _():
                lo2, hi2 = weighted(t, *rest)
                store_row(t, lo + lo2, hi + hi2)

            return carry

        lax.fori_loop(0, tm, token, 0)

    if len(o_refs) == 1:
        run(o_refs[0])
    else:
        first = pl.program_id(0) < n_first
        pl.when(first)(lambda: run(o_refs[0]))
        pl.when(jnp.logical_not(first))(lambda: run(o_refs[1]))


def _peer_v(off, nlow3, coef, tab, base, half, tm=128, split=None):
    t = base.shape[0]
    rows = PEER_HALF * SUBLANES
    smem = pl.BlockSpec((tm, LANES), lambda i: (i, 0), memory_space=pltpu.SMEM)
    tile = pl.BlockSpec((tm, 2, SUBLANES, LANES), lambda i: (i, 0, 0, 0))
    if split is None:
        n_first, out_specs = 0, tile
        out_shape = jax.ShapeDtypeStruct((t, 2, SUBLANES, LANES), F32)
    else:
        assert split % tm == 0
        n_first, n_rest = split // tm, (t - split) // tm
        blk = (tm, 2, SUBLANES, LANES)
        out_specs = [pl.BlockSpec(blk, lambda i: (jnp.clip(i, 0, n_first - 1), 0, 0, 0)),
                     pl.BlockSpec(blk, lambda i: (jnp.clip(i - n_first, 0, n_rest - 1), 0, 0, 0))]
        out_shape = [jax.ShapeDtypeStruct((split, 2, SUBLANES, LANES), F32),
                     jax.ShapeDtypeStruct((t - split, 2, SUBLANES, LANES), F32)]
    return pl.pallas_call(
        functools.partial(_peer_v_kernel, tm=tm, half=half, n_first=n_first),
        grid=(t // tm,),
        in_specs=[smem,
                  pl.BlockSpec((None, 1, tm), lambda i: (i, 0, 0), memory_space=pltpu.SMEM),
                  smem,
                  pl.BlockSpec((rows, LANES), lambda i: (half, 0), pipeline_mode=pl.Buffered(1)),
                  tile],
        out_specs=out_specs,
        out_shape=out_shape,
        compiler_params=_cparams(("arbitrary",)),
        name="peer_expert_sum",
    )(off, nlow3, coef, tab, base)


def _peer(x1, split, norm2_g, wq_bf, keys_bf, u_packed, v_packed, tm=128):
    t, d = x1.shape
    tm = min(tm, t)
    (xn_bf,) = _rmsnorm(x1, norm2_g, (BF16,))
    q = _matmul(xn_bf, wq_bf, wq_bf.shape[1], 0, BF16)
    eid_t, gate_t = _peer_topk(q, keys_bf)
    eid, slot, off, nlow = _peer_order(eid_t)
    nlow3 = nlow[:, 0].reshape(t // tm, 1, tm)
    xw = _pack_table(xn_bf).reshape(t, SUBLANES, LANES)
    a0 = _peer_u(off, nlow3, u_packed, xw, 0, tm)
    a1 = _peer_u(off, nlow3, u_packed, xw, 1, tm)
    c0, c1 = _peer_coef(a0, a1, eid, slot, gate_t)
    y = _peer_v(off, nlow3, c0, v_packed, x1.reshape(t, 2, SUBLANES, LANES), 0, tm)
    ya, yb = _peer_v(off, nlow3, c1, v_packed, y, 1, tm, split=split)
    return ya.reshape(split, d), yb.reshape(t - split, d)


def kernel(x_prompt, x_sample, norm1_g, w_in, q_norm_g, k_norm_g, lambda_q1, lambda_k1, lambda_q2, lambda_k2, attn_sub_g, conv_w, conv_b, filt_w1, filt_b1, filt_w2, filt_b2, filt_w3, filt_b3, filt_w4, filt_freq, fft_bias, hyena_out_g, w_out, norm2_g, peer_wq, peer_keys, peer_u, peer_v):
    depth = w_in.shape[0]
    d_model = x_prompt.shape[-1]
    att_w = ATT_HEADS * ATT_VDIM
    shapes = [x_prompt.shape[:2], x_sample.shape[:2]]
    xs = [x_prompt.reshape(-1, d_model), x_sample.reshape(-1, d_model)]
    n0 = xs[0].shape[0]
    slopes = 2.0 ** (-8.0 * jnp.arange(1, ATT_HEADS + 1, dtype=F32) / ATT_HEADS)

    for l in range(depth):
        lambda_init = 0.8 - 0.6 * math.exp(-0.3 * l)
        lam = (jnp.exp(jnp.sum(lambda_q1[l].astype(F32) * lambda_k1[l].astype(F32)))
               - jnp.exp(jnp.sum(lambda_q2[l].astype(F32) * lambda_k2[l].astype(F32)))
               + lambda_init).reshape(1)
        w_in_bf = w_in[l].astype(BF16)
        q_gain = jnp.tile(q_norm_g[l].astype(F32), 2 * ATT_HEADS) * (ATT_QKDIM ** -0.5 * LOG2E)
        k_gain = jnp.tile(k_norm_g[l].astype(F32), 2 * ATT_HEADS)
        qk_gain = jnp.concatenate([q_gain, k_gain]).reshape(1, 2 * att_w)

        h_bf = _rmsnorm_stacked(xs[0], xs[1], norm1_g[l], BF16)
        qk = _matmul(h_bf, w_in_bf, 2 * att_w, 0, BF16, mode="qknorm", extra=qk_gain)
        vt = _matmul_nt(w_in_bf[:, 2 * att_w:3 * att_w].T, h_bf, BF16)
        zh = _matmul(h_bf, w_in_bf, w_in.shape[2] - 3 * att_w, 3 * att_w, F32)

        filt = (filt_w1[l], filt_b1[l], filt_w2[l], filt_b2[l], filt_w3[l], filt_b3[l], filt_w4[l], filt_freq[l])
        segs, row = [], 0
        for (b, s), x_seg in zip(shapes, xs):
            att = _attention(qk, vt, row, slopes, lam, attn_sub_g[l], b, s, 1.0 - lambda_init)
            hy = _hyena(zh, row, b, s, conv_w[l], conv_b[l], filt, fft_bias[l], hyena_out_g[l])
            segs.append((att, hy, x_seg))
            row += b * s
        x1 = _outproj_stacked(segs[0], segs[1], w_out[l].astype(BF16))

        xs = _peer(x1, n0, norm2_g[l], peer_wq[l].astype(BF16), peer_keys[l].astype(BF16),
                   _pack_table(peer_u[l]), _pack_table(peer_v[l]))

    return (xs[0].reshape(x_prompt.shape), xs[1].reshape(x_sample.shape))
```

```python
import functools
import math

import jax
import jax.numpy as jnp
import numpy as np
from jax import lax
from jax.experimental import pallas as pl
from jax.experimental.pallas import tpu as pltpu

F32 = jnp.float32
BF16 = jnp.bfloat16
I32 = jnp.int32
U32 = jnp.uint32

RMS_EPS = 1e-6
LOG2E = 1.4426950408889634
LANES = 128
SUBLANES = 8
VMEM_LIMIT_BYTES = 56 * 1024 * 1024

ATT_HEADS = 8
ATT_VDIM = 128
ATT_QKDIM = 64
HYENA_ORDER = 2
FILTER_BANDS = 16
DECAY_FAST = 0.3
DECAY_SLOW = 1.5
DECAY_TARGET = 1e-2
PEER_HEADS = 8
PEER_NKEYS = 128
PEER_TOPK = 16
PEER_HALF = PEER_NKEYS * PEER_NKEYS // 2
PEER_PAIRS = PEER_HEADS * PEER_TOPK
PEER_WINDOW = 80


def _cparams(sem, vmem=VMEM_LIMIT_BYTES):
    return pltpu.CompilerParams(dimension_semantics=sem, vmem_limit_bytes=vmem)


def _segment_spec(block, start, count, minor=0):
    return pl.BlockSpec(block, lambda i: (jnp.clip(i - start, 0, count - 1), minor))


def _rmsnorm_kernel(x_ref, g_ref, *o_refs):
    x = x_ref[...]
    ms = jnp.mean(x * x, axis=-1, keepdims=True)
    y = x * lax.rsqrt(ms + RMS_EPS) * g_ref[...]
    for o_ref in o_refs:
        o_ref[...] = y.astype(o_ref.dtype)


def _rmsnorm2_kernel(xa_ref, xb_ref, g_ref, o_ref, *, na):
    first = pl.program_id(0) < na
    x = jnp.where(first, xa_ref[...], xb_ref[...])
    ms = jnp.mean(x * x, axis=-1, keepdims=True)
    o_ref[...] = (x * lax.rsqrt(ms + RMS_EPS) * g_ref[...]).astype(o_ref.dtype)


def _rmsnorm_stacked(xa, xb, g, out_dtype, tm=512):
    d = xa.shape[1]
    na, nb = xa.shape[0] // tm, xb.shape[0] // tm
    assert xa.shape[0] % tm == 0 and xb.shape[0] % tm == 0
    return pl.pallas_call(
        functools.partial(_rmsnorm2_kernel, na=na),
        grid=(na + nb,),
        in_specs=[_segment_spec((tm, d), 0, na), _segment_spec((tm, d), na, nb),
                  pl.BlockSpec((1, d), lambda i: (0, 0))],
        out_specs=pl.BlockSpec((tm, d), lambda i: (i, 0)),
        out_shape=jax.ShapeDtypeStruct((xa.shape[0] + xb.shape[0], d), out_dtype),
        compiler_params=_cparams(("parallel",)),
        name="rmsnorm_stacked",
    )(xa, xb, g.reshape(1, d).astype(F32))


def _rmsnorm(x, g, out_dtypes, tm=512):
    t, d = x.shape
    tm = min(tm, t)
    spec = pl.BlockSpec((tm, d), lambda i: (i, 0))
    return pl.pallas_call(
        _rmsnorm_kernel,
        grid=(t // tm,),
        in_specs=[spec, pl.BlockSpec((1, d), lambda i: (0, 0))],
        out_specs=[spec for _ in out_dtypes],
        out_shape=[jax.ShapeDtypeStruct((t, d), dt) for dt in out_dtypes],
        compiler_params=_cparams(("parallel",)),
        name="rmsnorm",
    )(x, g.reshape(1, d).astype(F32))


def _group_rms_scale(x, gain):
    lane = lax.broadcasted_iota(I32, x.shape, 1)
    lo = lane < ATT_QKDIM
    x2 = x * x
    s_lo = jnp.sum(jnp.where(lo, x2, 0.0), axis=-1, keepdims=True)
    s_hi = jnp.sum(jnp.where(lo, 0.0, x2), axis=-1, keepdims=True)
    ms = jnp.where(lo, s_lo, s_hi) * (1.0 / ATT_QKDIM)
    return x * lax.rsqrt(ms + RMS_EPS) * gain


def _mm_kernel(a_ref, b_ref, *rest, mode):
    acc = jnp.dot(a_ref[...], b_ref[...], preferred_element_type=F32)
    if mode == "plain":
        (o_ref,) = rest
        o_ref[...] = acc.astype(o_ref.dtype)
    elif mode == "qknorm":
        g_ref, o_ref = rest
        for c in range(acc.shape[1] // LANES):
            sl = slice(c * LANES, (c + 1) * LANES)
            o_ref[:, sl] = _group_rms_scale(acc[:, sl], g_ref[:, sl]).astype(o_ref.dtype)
    else:
        raise ValueError(mode)


def _matmul(a, b, n_cols, col_off, out_dtype, mode="plain", extra=None, tm=1024, tn=1024):
    m, k = a.shape
    tm = min(tm, m)
    tn = min(tn, n_cols)
    assert col_off % tn == 0 and n_cols % tn == 0 and m % tm == 0
    off = col_off // tn
    in_specs = [pl.BlockSpec((tm, k), lambda i, j: (i, 0)),
                pl.BlockSpec((k, tn), lambda i, j: (0, j + off))]
    args = [a, b]
    if mode == "qknorm":
        in_specs.append(pl.BlockSpec((1, tn), lambda i, j: (0, j)))
        args.append(extra)
    return pl.pallas_call(
        functools.partial(_mm_kernel, mode=mode),
        grid=(m // tm, n_cols // tn),
        in_specs=in_specs,
        out_specs=pl.BlockSpec((tm, tn), lambda i, j: (i, j)),
        out_shape=jax.ShapeDtypeStruct((m, n_cols), out_dtype),
        compiler_params=_cparams(("parallel", "arbitrary")),
        name="matmul_" + mode,
    )(*args)


def _outproj_kernel(att_a, hy_a, x_a, att_b, hy_b, x_b, w_ref, o_ref, *, na):
    kw = att_a.shape[1]

    def run(att_ref, hy_ref, x_ref):
        acc = jnp.dot(att_ref[...], w_ref[:kw, :], preferred_element_type=F32)
        acc = acc + jnp.dot(hy_ref[...].astype(BF16), w_ref[kw:, :], preferred_element_type=F32)
        o_ref[...] = acc + x_ref[...]

    first = pl.program_id(1) < na
    pl.when(first)(lambda: run(att_a, hy_a, x_a))
    pl.when(jnp.logical_not(first))(lambda: run(att_b, hy_b, x_b))


def _outproj_stacked(seg_a, seg_b, w, tm=512, tn=1024):
    n_out = w.shape[1]
    na, nb = seg_a[0].shape[0] // tm, seg_b[0].shape[0] // tm
    assert seg_a[0].shape[0] % tm == 0 and seg_b[0].shape[0] % tm == 0 and n_out % tn == 0

    def specs(seg, start, count):
        att, hy, x = seg
        row = lambda j, i: (jnp.clip(i - start, 0, count - 1), 0)
        return [pl.BlockSpec((tm, att.shape[1]), row), pl.BlockSpec((tm, hy.shape[1]), row),
                pl.BlockSpec((tm, tn), lambda j, i: (jnp.clip(i - start, 0, count - 1), j))]

    return pl.pallas_call(
        functools.partial(_outproj_kernel, na=na),
        grid=(n_out // tn, na + nb),
        in_specs=specs(seg_a, 0, na) + specs(seg_b, na, nb) + [pl.BlockSpec((w.shape[0], tn), lambda j, i: (0, j))],
        out_specs=pl.BlockSpec((tm, tn), lambda j, i: (i, j)),
        out_shape=jax.ShapeDtypeStruct(((na + nb) * tm, n_out), F32),
        compiler_params=_cparams(("arbitrary", "parallel")),
        name="out_projection",
    )(*seg_a, *seg_b, w)


def _mm_nt_kernel(w_ref, a_ref, o_ref):
    o_ref[...] = lax.dot_general(w_ref[...], a_ref[...], (((1,), (1,)), ((), ())),
                                 preferred_element_type=F32).astype(o_ref.dtype)


def _matmul_nt(wt, a, out_dtype, tm=1024):
    n, k = wt.shape
    m = a.shape[0]
    tm = min(tm, m)
    return pl.pallas_call(
        _mm_nt_kernel,
        grid=(m // tm,),
        in_specs=[pl.BlockSpec((n, k), lambda i: (0, 0)), pl.BlockSpec((tm, k), lambda i: (i, 0))],
        out_specs=pl.BlockSpec((n, tm), lambda i: (0, i)),
        out_shape=jax.ShapeDtypeStruct((n, m), out_dtype),
        compiler_params=_cparams(("parallel",)),
        name="matmul_nt",
    )(wt, a)


def _alibi_columns(slopes, tq, tk):
    return [_alibi_side(slopes, tq, True), _alibi_side(slopes, tk, False)]


def _alibi_side(slopes, n, query_side):
    pos = jnp.arange(n, dtype=F32)
    val = (slopes.astype(F32) * LOG2E)[:, None] * pos[None, :]

    def pieces(x):
        p1 = x.astype(BF16)
        r1 = x - p1.astype(F32)
        p2 = r1.astype(BF16)
        p3 = (r1 - p2.astype(F32)).astype(BF16)
        return [p1, p2, p3]

    ones = [jnp.ones_like(val, BF16)] * 3
    six = jnp.stack(pieces(-val) + ones if query_side else ones + pieces(val), axis=-1)
    pad = jnp.zeros(val.shape + (ATT_QKDIM - 6,), BF16)
    return jnp.concatenate([six, pad, six, pad], axis=-1)


def _attn_kernel(slope_ref, lam_ref, q_ref, k_ref, vt_ref, aq_ref, ak_ref, g_ref, o_ref,
                 m_ref, l_ref, acc_ref, *, tq, tk, nk, hb, out_scale):
    hg = pl.program_id(1)
    i = pl.program_id(2)
    j = pl.program_id(3)

    @pl.when(j == 0)
    def _():
        m_ref[...] = jnp.full(m_ref.shape, -jnp.inf, F32)
        l_ref[...] = jnp.zeros(l_ref.shape, F32)
        acc_ref[...] = jnp.zeros(acc_ref.shape, F32)

    q_first = lax.broadcasted_iota(I32, (tq, LANES), 1) < ATT_QKDIM
    k_first = lax.broadcasted_iota(I32, (tk, LANES), 1) < ATT_QKDIM
    nt = (((1,), (1,)), ((), ()))

    def update(hh, scores, shift):
        vt = vt_ref[hh * LANES:(hh + 1) * LANES, :]
        for c, s in enumerate(scores):
            m_old = m_ref[hh, c]
            m_new = jnp.maximum(m_old, jnp.max(s, axis=0, keepdims=True) + shift)
            alpha = jnp.exp2(m_old - m_new)
            p = jnp.exp2(s - (m_new - shift))
            l_ref[hh, c] = alpha * l_ref[hh, c] + jnp.sum(p, axis=0, keepdims=True)
            acc_ref[hh, c] = alpha * acc_ref[hh, c] + jnp.dot(vt, p.astype(BF16),
                                                              preferred_element_type=F32)
            m_ref[hh, c] = m_new

    keys_before = i * tq >= (j + 1) * tk
    keys_after = (i + 1) * tq <= j * tk
    off_diagonal = jnp.logical_or(keys_before, keys_after)

    @pl.when(off_diagonal)
    def _():
        sign = jnp.where(keys_after, -1.0, 1.0).astype(BF16)
        gap = jnp.abs(i * tq - j * tk).astype(F32)
        for hh in range(hb):
            sl = slice(hh * LANES, (hh + 1) * LANES)
            q = q_ref[:, sl]
            k = k_ref[:, sl]
            aq = aq_ref[hh]
            ak = ak_ref[hh] * sign
            s0 = lax.dot_general(jnp.where(k_first, k, ak), jnp.where(q_first, q, aq), nt,
                                 preferred_element_type=F32)
            s1 = lax.dot_general(jnp.where(k_first, ak, k), jnp.where(q_first, aq, q), nt,
                                 preferred_element_type=F32)
            update(hh, (s0, s1), -(slope_ref[hg * hb + hh] * LOG2E) * gap)

    @pl.when(jnp.logical_not(off_diagonal))
    def _():
        kpos = lax.broadcasted_iota(I32, (tk, tq), 0) + j * tk
        qpos = lax.broadcasted_iota(I32, (tk, tq), 1) + i * tq
        dist = jnp.abs(kpos - qpos).astype(F32)
        for hh in range(hb):
            sl = slice(hh * LANES, (hh + 1) * LANES)
            q = q_ref[:, sl]
            k = k_ref[:, sl]
            zero = jnp.zeros_like(q)
            bias = dist * (-(slope_ref[hg * hb + hh] * LOG2E))
            s0 = lax.dot_general(k, jnp.where(q_first, q, zero), nt, preferred_element_type=F32) + bias
            s1 = lax.dot_general(k, jnp.where(q_first, zero, q), nt, preferred_element_type=F32) + bias
            update(hh, (s0, s1), 0.0)

    @pl.when(j == nk - 1)
    def _():
        for hh in range(hb):
            o = acc_ref[hh, 0] / l_ref[hh, 0] - lam_ref[0] * (acc_ref[hh, 1] / l_ref[hh, 1])
            ms = jnp.mean(o * o, axis=0, keepdims=True)
            y = o * lax.rsqrt(ms + RMS_EPS) * (g_ref[...] * out_scale)
            o_ref[:, hh * LANES:(hh + 1) * LANES] = y.T.astype(o_ref.dtype)


def _attention(qk, vt, row_off, slopes, lam, sub_g, batch, seq, out_scale, tq=512, tk=1024, hb=4):
    tq = min(tq, seq)
    tk = min(tk, seq // 4)
    nq, nk = seq // tq, seq // tk
    assert row_off % tq == 0 and row_off % tk == 0 and ATT_HEADS % hb == 0
    oq, ok = row_off // tq, row_off // tk
    ng = ATT_HEADS // hb
    aq, ak = _alibi_columns(slopes, tq, tk)
    kern = functools.partial(_attn_kernel, tq=tq, tk=tk, nk=nk, hb=hb, out_scale=out_scale)
    smem = pl.BlockSpec(memory_space=pltpu.SMEM)
    return pl.pallas_call(
        kern,
        grid=(batch, ng, nq, nk),
        in_specs=[smem, smem,
                  pl.BlockSpec((tq, hb * LANES), lambda b, h, i, j: (oq + b * nq + i, h)),
                  pl.BlockSpec((tk, hb * LANES), lambda b, h, i, j: (ok + b * nk + j, ng + h)),
                  pl.BlockSpec((hb * LANES, tk), lambda b, h, i, j: (h, ok + b * nk + j)),
                  pl.BlockSpec((hb, tq, LANES), lambda b, h, i, j: (h, 0, 0)),
                  pl.BlockSpec((hb, tk, LANES), lambda b, h, i, j: (h, 0, 0)),
                  pl.BlockSpec((LANES, 1), lambda b, h, i, j: (0, 0))],
        out_specs=pl.BlockSpec((tq, hb * LANES), lambda b, h, i, j: (b * nq + i, h)),
        out_shape=jax.ShapeDtypeStruct((batch * seq, ATT_HEADS * ATT_VDIM), BF16),
        scratch_shapes=[pltpu.VMEM((hb, 2, 1, tq), F32), pltpu.VMEM((hb, 2, 1, tq), F32),
                        pltpu.VMEM((hb, 2, LANES, tq), F32)],
        compiler_params=_cparams(("parallel", "parallel", "parallel", "arbitrary")),
        name="diff_attention",
    )(slopes, lam, qk, qk, vt, aq, ak, sub_g.reshape(LANES, 1).astype(F32))


def _shortconv_kernel(z_ref, w_ref, b_ref, o_ref):
    z = z_ref[...]
    n = z.shape[0]
    row = lax.broadcasted_iota(I32, z.shape, 0)
    prev = jnp.where(row == 0, 0.0, pltpu.roll(z, 1, axis=0))
    nxt = jnp.where(row == n - 1, 0.0, pltpu.roll(z, n - 1, axis=0))
    w = w_ref[...]
    o_ref[...] = prev * w[0:1] + z * w[1:2] + nxt * w[2:3] + b_ref[...]


def _shortconv(zh, row_off, conv_w, conv_b, batch, seq, cb=256):
    c3 = zh.shape[1]
    c = c3 // 3
    ncb = c // cb
    assert row_off % seq == 0
    ob = row_off // seq
    out = pl.pallas_call(
        _shortconv_kernel,
        grid=(batch, 3, ncb),
        in_specs=[pl.BlockSpec((seq, cb), lambda b, p, j: (ob + b, p * ncb + j)),
                  pl.BlockSpec((3, cb), lambda b, p, j: (0, p * ncb + j)),
                  pl.BlockSpec((1, cb), lambda b, p, j: (0, p * ncb + j))],
        out_specs=pl.BlockSpec((None, None, seq, cb), lambda b, p, j: (p, b, 0, j)),
        out_shape=jax.ShapeDtypeStruct((3, batch, seq, c), F32),
        compiler_params=_cparams(("parallel", "parallel", "parallel")),
        name="hyena_shortconv",
    )(zh, conv_w.astype(F32), conv_b.reshape(1, c3).astype(F32))
    return out


def _filter_kernel(z_ref, w1_ref, b1_ref, w2_ref, b2_ref, w3_ref, b3_ref, fr_ref, w4_ref,
                   t_ref, d_ref, o_ref, h_ref, *, tl):
    i = pl.program_id(0)
    g = pl.program_id(1)
    hi = lax.Precision.HIGHEST

    @pl.when(g == 0)
    def _():
        fr = fr_ref[...]
        h = jnp.sin(fr * (jnp.dot(z_ref[...], w1_ref[...], precision=hi, preferred_element_type=F32) + b1_ref[...]))
        h = jnp.sin(fr * (jnp.dot(h, w2_ref[...], precision=hi, preferred_element_type=F32) + b2_ref[...]))
        h_ref[...] = jnp.sin(fr * (jnp.dot(h, w3_ref[...], precision=hi, preferred_element_type=F32) + b3_ref[...]))

    f = jnp.dot(h_ref[...].astype(BF16), w4_ref[...], preferred_element_type=F32)
    f = f * jnp.exp(-t_ref[...] * d_ref[...])
    row = lax.broadcasted_iota(I32, f.shape, 0) + i * tl
    drop = jnp.logical_and(row == 0, g % 2 == 1)
    o_ref[...] = jnp.where(drop, 0.0, f)


def _hyena_filter_signals(seq, w1, b1, w2, b2, w3, b3, w4, freq, n_ch, tl=512):
    t = jnp.linspace(0.0, 1.0, seq, dtype=F32)[:, None]
    w = 2.0 * math.pi * jnp.arange(seq, dtype=F32)[:, None] / seq
    f = jnp.linspace(1e-4, FILTER_BANDS - 1, FILTER_BANDS, dtype=F32)[None, :]
    z = jnp.concatenate([t, jnp.cos(f * w), -jnp.sin(f * w)], axis=-1)
    deltas = jnp.abs(jnp.linspace(math.log(DECAY_FAST) / DECAY_TARGET,
                                  math.log(DECAY_SLOW) / DECAY_TARGET, n_ch, dtype=F32))[None, :]
    hid = w1.shape[1]
    emb = LANES
    z = jnp.pad(z, ((0, 0), (0, emb - z.shape[1])))
    w1 = jnp.pad(w1.astype(F32), ((0, emb - w1.shape[0]), (0, 0)))
    tl = min(tl, seq)
    full = lambda shape: pl.BlockSpec(shape, lambda i, g: tuple(0 for _ in shape))
    return pl.pallas_call(
        functools.partial(_filter_kernel, tl=tl),
        grid=(seq // tl, 2 * HYENA_ORDER),
        in_specs=[pl.BlockSpec((tl, emb), lambda i, g: (i, 0)),
                  full((emb, hid)), full((1, hid)), full((hid, hid)), full((1, hid)),
                  full((hid, hid)), full((1, hid)), full((1, hid)),
                  pl.BlockSpec((hid, n_ch), lambda i, g: (0, g)),
                  pl.BlockSpec((tl, 1), lambda i, g: (i, 0)),
                  full((1, n_ch))],
        out_specs=pl.BlockSpec((None, tl, n_ch), lambda i, g: (g, i, 0)),
        out_shape=jax.ShapeDtypeStruct((2 * HYENA_ORDER, seq, n_ch), F32),
        scratch_shapes=[pltpu.VMEM((tl, hid), F32)],
        compiler_params=_cparams(("parallel", "arbitrary")),
        name="hyena_filter_mlp",
    )(z, w1.astype(F32), b1.reshape(1, hid).astype(F32), w2.astype(F32), b2.reshape(1, hid).astype(F32),
      w3.astype(F32), b3.reshape(1, hid).astype(F32), freq.reshape(1, hid).astype(F32), w4.astype(BF16),
      t, deltas)


def _dft_tables(r):
    n = r * r
    ang1 = (2.0 * np.pi / r) * ((np.arange(r)[:, None] * np.arange(r // 2)[None, :]) % r)
    eye = np.eye(N1_BLOCK)
    f1 = jnp.asarray(np.kron(np.concatenate([np.cos(ang1), -np.sin(ang1)], axis=0), eye), BF16)
    g3 = jnp.asarray(np.kron(np.concatenate([np.cos(ang1.T), -np.sin(ang1.T)], axis=1) / n, eye), BF16)
    k2 = jnp.arange(r, dtype=I32)
    k1 = jnp.arange(r, dtype=I32)
    n1 = jnp.arange(r, dtype=I32)
    ang_a = (2.0 * math.pi / r) * ((k1[:, None] * n1[None, :]) % r).astype(F32)
    ang_b = (2.0 * math.pi / n) * (k2[:, None] * n1[None, :]).astype(F32)
    ca, sa = jnp.cos(ang_a)[None], jnp.sin(ang_a)[None]
    cb, sb = jnp.cos(ang_b), jnp.sin(ang_b)

    def blocks(cb3, sb3):
        return (ca * cb3 - sa * sb3).astype(BF16), (-(sa * cb3 + ca * sb3)).astype(BF16)

    mr, mi_ = blocks(cb[:, None, :], sb[:, None, :])
    mf = jnp.concatenate([jnp.concatenate([mr, -mi_], axis=2),
                          jnp.concatenate([mi_, mr], axis=2)], axis=1)
    mr_t, mi_t = blocks(cb[:, :, None], sb[:, :, None])
    minv = jnp.concatenate([jnp.concatenate([mr_t, mi_t], axis=2),
                            jnp.concatenate([-mi_t, mr_t], axis=2)], axis=1)
    return f1, mf, minv, g3


N1_BLOCK = SUBLANES


def _pack_complex(re, im):
    rb = lax.bitcast_convert_type(re.astype(BF16).astype(F32), U32)
    ib = lax.bitcast_convert_type(im.astype(BF16).astype(F32), U32)
    return (rb >> 16) | ib


def _unpack_complex_rows(w):
    re, im = _unpack(w)
    return jnp.concatenate([re, im], axis=0).astype(BF16)


def _fft1_kernel(f_ref, x_ref, o_ref, *, r):
    n_ch = x_ref.shape[-1]
    x = x_ref[...].reshape((r // 2) * N1_BLOCK, n_ch).astype(BF16)
    res = jnp.dot(f_ref[...], x, preferred_element_type=F32)
    half = r * N1_BLOCK
    o_ref[...] = _pack_complex(res[:half], res[half:]).reshape(r, N1_BLOCK, n_ch)


def _fft_stage1(x4, part, f1, r, n_ch):
    nb = x4.shape[1]
    xv = x4.reshape(x4.shape[0], nb, r // 2, r, n_ch)
    return pl.pallas_call(
        functools.partial(_fft1_kernel, r=r),
        grid=(nb, r // N1_BLOCK),
        in_specs=[pl.BlockSpec(f1.shape, lambda b, j: (0, 0)),
                  pl.BlockSpec((None, None, r // 2, N1_BLOCK, n_ch), lambda b, j: (part, b, 0, j, 0))],
        out_specs=pl.BlockSpec((None, r, N1_BLOCK, n_ch), lambda b, j: (b, 0, j, 0)),
        out_shape=jax.ShapeDtypeStruct((nb, r, r, n_ch), U32),
        compiler_params=_cparams(("parallel", "parallel")),
        name="hyena_dft_stage1",
    )(f1, xv)


def _k2_block(r):
    return max(1, min(r, (8 * LANES) // r))


def _filter_spec_kernel(mf_ref, bf_ref, bb_ref, o_ref, *, r):
    for kk in range(mf_ref.shape[0]):
        m = mf_ref[kk]
        xf = jnp.dot(m, _unpack_complex_rows(bf_ref[kk]), preferred_element_type=F32)
        xb = jnp.dot(m, _unpack_complex_rows(bb_ref[kk]), preferred_element_type=F32)
        o_ref[kk] = _pack_complex(xf[:r] + xb[:r], xf[r:] - xb[r:])


def _filter_spectrum(b1, mf, r, n_ch):
    kb = _k2_block(r)
    blk = lambda sel: pl.BlockSpec((None, kb, r, n_ch), lambda o, k: (2 * o + sel, k, 0, 0))
    return pl.pallas_call(
        functools.partial(_filter_spec_kernel, r=r),
        grid=(HYENA_ORDER, r // kb),
        in_specs=[pl.BlockSpec((kb, 2 * r, 2 * r), lambda o, k: (k, 0, 0)), blk(0), blk(1)],
        out_specs=pl.BlockSpec((None, kb, r, n_ch), lambda o, k: (o, k, 0, 0)),
        out_shape=jax.ShapeDtypeStruct((HYENA_ORDER, r, r, n_ch), U32),
        compiler_params=_cparams(("parallel", "parallel")),
        name="hyena_filter_spectrum",
    )(mf, b1, b1)


def _fft2_kernel(mf_ref, mi_ref, b_ref, h_ref, o_ref, *, r):
    for kk in range(mf_ref.shape[0]):
        x = jnp.dot(mf_ref[kk], _unpack_complex_rows(b_ref[kk]), preferred_element_type=F32)
        xr, xi = x[:r], x[r:]
        hr, hi = _unpack(h_ref[kk])
        y = jnp.concatenate([xr * hr - xi * hi, xr * hi + xi * hr], axis=0).astype(BF16)
        c = jnp.dot(mi_ref[kk], y, preferred_element_type=F32)
        o_ref[kk] = _pack_complex(c[:r], c[r:])


def _fft_stage2(b1, h, order, mf, minv, r, n_ch):
    nb = b1.shape[0]
    kb = _k2_block(r)
    blk = pl.BlockSpec((None, kb, r, n_ch), lambda k, b: (b, k, 0, 0))
    return pl.pallas_call(
        functools.partial(_fft2_kernel, r=r),
        grid=(r // kb, nb),
        in_specs=[pl.BlockSpec((kb, 2 * r, 2 * r), lambda k, b: (k, 0, 0)),
                  pl.BlockSpec((kb, 2 * r, 2 * r), lambda k, b: (k, 0, 0)),
                  blk,
                  pl.BlockSpec((None, kb, r, n_ch), lambda k, b: (order, k, 0, 0))],
        out_specs=blk,
        out_shape=jax.ShapeDtypeStruct((nb, r, r, n_ch), U32),
        compiler_params=_cparams(("parallel", "parallel")),
        name="hyena_dft_stage2",
    )(mf, minv, b1, h)


def _fft3_kernel(g_ref, c_ref, gate_ref, s_ref, bias_ref, ng_ref, *rest, final):
    if final:
        (o_ref,) = rest
    else:
        f_ref, o_ref, b_ref = rest
    r, nb, n_ch = c_ref.shape
    cc = _unpack_complex_rows(c_ref[...].reshape(r * nb, n_ch))
    y = jnp.dot(g_ref[...], cc, preferred_element_type=F32)
    rows = (r // 2) * nb
    s_new = gate_ref[...].reshape(rows, n_ch) * (y + s_ref[...].reshape(rows, n_ch) * bias_ref[...])
    if final:
        ms = jnp.mean(s_new * s_new, axis=-1, keepdims=True)
        s_new = s_new * lax.rsqrt(ms + RMS_EPS) * ng_ref[...]
    o_ref[...] = s_new.reshape(r // 2, nb, n_ch)
    if not final:
        res = jnp.dot(f_ref[...], s_new.astype(BF16), preferred_element_type=F32)
        half = r * nb
        b_ref[...] = _pack_complex(res[:half], res[half:]).reshape(r, nb, n_ch)


def _fft_stage3(c2, g3, z4, gate_part, s4, s_part, bias, norm_g, f1, r, n_ch):
    final = f1 is None
    nb = c2.shape[0]
    zv = z4.reshape(z4.shape[0], nb, r // 2, r, n_ch)
    sv = s4.reshape(s4.shape[0], nb, r // 2, r, n_ch)
    dspec = lambda part: pl.BlockSpec((None, None, r // 2, N1_BLOCK, n_ch), lambda b, j: (part, b, 0, j, 0))
    packed = pl.BlockSpec((None, r, N1_BLOCK, n_ch), lambda b, j: (b, 0, j, 0))
    s_spec = pl.BlockSpec((None, r // 2, N1_BLOCK, n_ch), lambda b, j: (b, 0, j, 0))
    s_shape = jax.ShapeDtypeStruct((nb, r // 2, r, n_ch), F32)
    in_specs = [pl.BlockSpec(g3.shape, lambda b, j: (0, 0)), packed, dspec(gate_part), dspec(s_part),
                pl.BlockSpec((1, n_ch), lambda b, j: (0, 0)), pl.BlockSpec((1, n_ch), lambda b, j: (0, 0))]
    args = [g3, c2, zv, sv, bias.reshape(1, n_ch).astype(F32), norm_g.reshape(1, n_ch).astype(F32)]
    if final:
        out_specs, out_shape = s_spec, s_shape
    else:
        in_specs.append(pl.BlockSpec(f1.shape, lambda b, j: (0, 0)))
        args.append(f1)
        out_specs = [s_spec, packed]
        out_shape = [s_shape, jax.ShapeDtypeStruct((nb, r, r, n_ch), U32)]
    return pl.pallas_call(
        functools.partial(_fft3_kernel, final=final),
        grid=(nb, r // N1_BLOCK),
        in_specs=in_specs,
        out_specs=out_specs,
        out_shape=out_shape,
        compiler_params=_cparams(("parallel", "parallel")),
        name="hyena_dft_stage3",
    )(*args)


def _hyena(zh, row_off, batch, seq, conv_w, conv_b, filt, fft_bias, out_g):
    n_ch = zh.shape[1] // 3
    r = int(round(math.sqrt(2 * seq)))
    assert r * r == 2 * seq and r % 16 == 0
    f1, mf, minv, g3 = _dft_tables(r)
    sig = _hyena_filter_signals(seq, *filt, n_ch=n_ch)
    hb1 = _fft_stage1(sig[None], 0, f1, r, n_ch)
    h = _filter_spectrum(hb1, mf, r, n_ch)
    z4 = _shortconv(zh, row_off, conv_w, conv_b, batch, seq)
    s4, s_part = z4, 2
    b1 = _fft_stage1(s4, s_part, f1, r, n_ch)
    for o in range(HYENA_ORDER):
        c2 = _fft_stage2(b1, h, o, mf, minv, r, n_ch)
        if o < HYENA_ORDER - 1:
            s, b1 = _fft_stage3(c2, g3, z4, o, s4, s_part, fft_bias[o], out_g, f1, r, n_ch)
        else:
            s = _fft_stage3(c2, g3, z4, o, s4, s_part, fft_bias[o], out_g, None, r, n_ch)
        s4, s_part = s.reshape(1, batch, seq, n_ch), 0
    return s4.reshape(batch * seq, n_ch)


def _extract_top(s, key, count):
    vals, keys = [], []
    for _ in range(count):
        m = jnp.max(s, axis=0, keepdims=True)
        kmin = jnp.min(jnp.where(s == m, key, jnp.inf), axis=0, keepdims=True)
        s = jnp.where(key == kmin, -jnp.inf, s)
        vals.append(m)
        keys.append(kmin)
    return vals, keys


def _peer_topk_kernel(q_ref, keys_ref, eid_ref, gate_ref):
    t = q_ref.shape[0]
    nk = PEER_NKEYS
    q = q_ref[...]
    row_key = lax.broadcasted_iota(I32, (nk, t), 0).astype(F32)
    tops = []
    for c in range(2):
        s = lax.dot_general(keys_ref[c], q[:, c * nk:(c + 1) * nk], (((1,), (1,)), ((), ())),
                            preferred_element_type=F32)
        tops.append(_extract_top(s, row_key, PEER_TOPK))
    (v1, i1), (v2, i2) = tops
    rows16 = lax.broadcasted_iota(I32, (PEER_TOPK, t), 0)
    v2a = jnp.zeros((PEER_TOPK, t), F32)
    i2a = jnp.zeros((PEER_TOPK, t), F32)
    for j in range(PEER_TOPK):
        v2a = jnp.where(rows16 == j, v2[j], v2a)
        i2a = jnp.where(rows16 == j, i2[j], i2a)
    n_exp = float(nk * nk)
    half = PEER_TOPK // 2
    rows8 = lax.broadcasted_iota(I32, (half, t), 0)
    pos8 = rows8.astype(F32)
    v2h, i2h = v2a[:half], i2a[:half]
    cand = [v1[0] + v2a]
    ckey = [rows16.astype(F32) * n_exp + (i1[0] * float(nk) + i2a)]
    for i in range(1, half):
        cand.append(jnp.where(rows8 < PEER_TOPK // (i + 1), v1[i] + v2h, -jnp.inf))
        ckey.append((pos8 + float(i * PEER_TOPK)) * n_exp + (i1[i] * float(nk) + i2h))
    v1t = jnp.zeros((half, t), F32)
    i1t = jnp.zeros((half, t), F32)
    for r in range(half):
        v1t = jnp.where(rows8 == r, v1[half + r], v1t)
        i1t = jnp.where(rows8 == r, i1[half + r], i1t)
    cand.append(v1t + v2[0])
    ckey.append((pos8 + float(half)) * (PEER_TOPK * n_exp) + (i1t * float(nk) + i2[0]))
    tv, tk_ = _extract_top(jnp.concatenate(cand, axis=0), jnp.concatenate(ckey, axis=0), PEER_TOPK)
    denom = jnp.zeros((1, t), F32)
    es = []
    for k in range(PEER_TOPK):
        e = jnp.exp(tv[k] - tv[0])
        es.append(e)
        denom = denom + e
    eid = jnp.zeros((PEER_TOPK, t), F32)
    gate = jnp.zeros((PEER_TOPK, t), F32)
    for k in range(PEER_TOPK):
        pos = jnp.floor(tk_[k] * (1.0 / n_exp))
        eid = jnp.where(rows16 == k, tk_[k] - pos * n_exp, eid)
        gate = jnp.where(rows16 == k, es[k] / denom, gate)
    eid_ref[...] = eid.astype(I32)
    gate_ref[...] = gate


def _peer_topk(q, keys, tm=1024):
    t = q.shape[0]
    tm = min(tm, t)
    out_spec = pl.BlockSpec((PEER_TOPK, tm), lambda i, h: (h, i))
    return pl.pallas_call(
        _peer_topk_kernel,
        grid=(t // tm, PEER_HEADS),
        in_specs=[pl.BlockSpec((tm, 2 * PEER_NKEYS), lambda i, h: (i, h)),
                  pl.BlockSpec((None, 2, PEER_NKEYS, PEER_NKEYS), lambda i, h: (h, 0, 0, 0))],
        out_specs=[out_spec, out_spec],
        out_shape=[jax.ShapeDtypeStruct((PEER_HEADS * PEER_TOPK, t), I32),
                   jax.ShapeDtypeStruct((PEER_HEADS * PEER_TOPK, t), F32)],
        compiler_params=_cparams(("parallel", "parallel")),
        name="peer_topk",
    )(q, keys)


def _pack_table(tab):
    e, d = tab.shape
    assert d == 2 * SUBLANES * LANES
    def bf16_bits(x):
        w = lax.bitcast_convert_type(x.astype(F32), U32)
        rne = (w + (jnp.uint32(0x7FFF) + ((w >> 16) & jnp.uint32(1)))) >> 16
        is_nan = (w & jnp.uint32(0x7FFFFFFF)) > jnp.uint32(0x7F800000)
        return jnp.where(is_nan, (w >> 16) | jnp.uint32(0x40), rne)

    packed = bf16_bits(tab[:, :d // 2]) | (bf16_bits(tab[:, d // 2:]) << 16)
    return packed.reshape(e * SUBLANES, LANES)


def _unpack(w):
    lo = lax.bitcast_convert_type(w << 16, F32)
    hi = lax.bitcast_convert_type(w & jnp.uint32(0xFFFF0000), F32)
    return lo, hi


_BITREV8 = (0, 4, 2, 6, 1, 5, 3, 7)


def _sublane_fold8(parts):
    sub = lax.broadcasted_iota(I32, (2 * SUBLANES, LANES), 0) // 2

    def rolled(a, shift):
        return pltpu.bitcast(pltpu.roll(pltpu.bitcast(a, U32), shift, axis=0), BF16)

    lvl = [parts[_BITREV8[r]] for r in range(8)]
    for shift, mask in ((4, sub < 4), (2, (sub % 4) < 2), (1, (sub % 2) < 1)):
        nxt = []
        for a, b in zip(lvl[0::2], lvl[1::2]):
            nxt.append(jnp.where(mask, a + rolled(a, SUBLANES - shift), b + rolled(b, shift)))
        lvl = nxt
    return lvl[0]


def _pair_ranges(half):
    if half == 0:
        return (0, PEER_WINDOW), (PEER_WINDOW, PEER_PAIRS)
    return (PEER_PAIRS - PEER_WINDOW, PEER_PAIRS), (0, PEER_PAIRS - PEER_WINDOW)


def _overflow(n_low, half):
    return n_low > PEER_WINDOW if half == 0 else n_low < PEER_PAIRS - PEER_WINDOW


def _table_rows(tab_ref, off):
    return _unpack(tab_ref[pl.ds(pl.multiple_of(off, SUBLANES), SUBLANES), :])


def _peer_u_kernel(off_ref, nlow_ref, tab_ref, x_ref, o_ref, extra_ref, *, tm, half):
    lane = lax.broadcasted_iota(I32, (SUBLANES, LANES), 1)
    sub = lax.broadcasted_iota(I32, (SUBLANES, LANES), 0)
    lane_grp = lax.shift_right_logical(lane, 3)
    diag = sub == (lane & (SUBLANES - 1))
    main, rest = _pair_ranges(half)

    def folded(t, p0, p1):
        xb = pltpu.bitcast(x_ref[t], BF16)
        out = []
        for g in range(p0 // SUBLANES, p1 // SUBLANES):
            parts = []
            for r in range(SUBLANES):
                off = pl.multiple_of(off_ref[t, g * SUBLANES + r], SUBLANES)
                parts.append(pltpu.bitcast(tab_ref[pl.ds(off, SUBLANES), :], BF16) * xb)
            out.append(pltpu.bitcast(_sublane_fold8(parts), U32))
        return out

    def lane_sums(folds, p0):
        mat = jnp.zeros((SUBLANES, LANES), F32)
        for i, f in enumerate(folds):
            lo, hi = _unpack(f)
            mat = jnp.where(lane_grp == p0 // SUBLANES + i, jnp.sum(lo + hi, axis=-1, keepdims=True), mat)
        return jnp.sum(jnp.where(diag, mat, 0.0), axis=0, keepdims=True)

    def finish(t, folds):
        o_ref[pl.ds(t, 1), :] = lane_sums(folds, main[0]) + extra_ref[pl.ds(t, 1), :]

    def token(t, prev):
        cur = folded(t, *main)
        finish(jnp.maximum(t - 1, 0), prev)
        extra_ref[pl.ds(t, 1), :] = jnp.zeros((1, LANES), F32)

        @pl.when(_overflow(nlow_ref[0, t], half))
        def _():
            extra_ref[pl.ds(t, 1), :] = lane_sums(folded(t, *rest), rest[0])

        return tuple(cur)

    extra_ref[pl.ds(0, 1), :] = jnp.zeros((1, LANES), F32)
    zeros = tuple(jnp.zeros((SUBLANES, LANES), U32) for _ in range((main[1] - main[0]) // SUBLANES))
    last = lax.fori_loop(0, tm, token, zeros)
    finish(tm - 1, last)


def _peer_u(off, nlow3, tab, x4, half, tm=128):
    t = off.shape[0]
    rows = PEER_HALF * SUBLANES
    return pl.pallas_call(
        functools.partial(_peer_u_kernel, tm=tm, half=half),
        grid=(t // tm,),
        in_specs=[pl.BlockSpec((tm, LANES), lambda i: (i, 0), memory_space=pltpu.SMEM),
                  pl.BlockSpec((None, 1, tm), lambda i: (i, 0, 0), memory_space=pltpu.SMEM),
                  pl.BlockSpec((rows, LANES), lambda i: (half, 0), pipeline_mode=pl.Buffered(1)),
                  pl.BlockSpec((tm, SUBLANES, LANES), lambda i: (i, 0, 0))],
        out_specs=pl.BlockSpec((tm, LANES), lambda i: (i, 0)),
        out_shape=jax.ShapeDtypeStruct((t, LANES), F32),
        scratch_shapes=[pltpu.VMEM((tm, LANES), F32)],
        compiler_params=_cparams(("arbitrary",)),
        name="peer_expert_scores",
    )(off, nlow3, tab, x4)


def _peer_order_kernel(eid_ref, eid_o, slot_o, off_o, nlow_o):
    eid = eid_ref[...].T
    lane = lax.broadcasted_iota(I32, eid.shape, 1)
    low = (eid < PEER_HALF).astype(I32)
    c_low = low
    shift = 1
    while shift < PEER_PAIRS:
        c_low = c_low + jnp.where(lane >= shift, pltpu.roll(c_low, shift, axis=1), 0)
        shift *= 2
    c_high = lane + 1 - c_low
    n_low = jnp.max(c_low, axis=1, keepdims=True)
    in_low = lane < n_low
    rank = jnp.where(in_low, lane + 1, lane + 1 - n_low)
    pos = jnp.zeros_like(lane)
    step = PEER_PAIRS // 2
    while step >= 1:
        probe = pos + (step - 1)
        count = jnp.where(in_low, jnp.take_along_axis(c_low, probe, axis=1),
                          jnp.take_along_axis(c_high, probe, axis=1))
        pos = jnp.where(count < rank, pos + step, pos)
        step //= 2
    eid_s = jnp.take_along_axis(eid, pos, axis=1)
    eid_o[...] = eid_s
    slot_o[...] = pos
    off_o[...] = (eid_s & (PEER_HALF - 1)) * SUBLANES
    nlow_o[...] = jnp.broadcast_to(n_low, eid.shape)


def _peer_order(eid_t, tm=1024):
    t = eid_t.shape[1]
    tm = min(tm, t)
    spec = pl.BlockSpec((tm, LANES), lambda i: (i, 0))
    return pl.pallas_call(
        _peer_order_kernel,
        grid=(t // tm,),
        in_specs=[pl.BlockSpec((LANES, tm), lambda i: (0, i))],
        out_specs=[spec] * 4,
        out_shape=[jax.ShapeDtypeStruct((t, LANES), I32)] * 4,
        compiler_params=_cparams(("parallel",)),
        name="peer_pair_order",
    )(eid_t)


def _peer_coef_kernel(a0_ref, a1_ref, eid_ref, slot_ref, gate_ref, c0_ref, c1_ref):
    low = eid_ref[...] < PEER_HALF
    a = jnp.where(low, a0_ref[...], a1_ref[...])
    gate = jnp.take_along_axis(gate_ref[...].T, slot_ref[...], axis=1)
    coef = gate * (0.5 * a * (1.0 + lax.erf(a * (1.0 / math.sqrt(2.0)))))
    c0_ref[...] = jnp.where(low, coef, 0.0)
    c1_ref[...] = jnp.where(low, 0.0, coef)


def _peer_coef(a0, a1, eid, slot, gate_t, tm=1024):
    t = eid.shape[0]
    tm = min(tm, t)
    spec = pl.BlockSpec((tm, LANES), lambda i: (i, 0))
    return pl.pallas_call(
        _peer_coef_kernel,
        grid=(t // tm,),
        in_specs=[spec, spec, spec, spec, pl.BlockSpec((LANES, tm), lambda i: (0, i))],
        out_specs=[spec, spec],
        out_shape=[jax.ShapeDtypeStruct((t, LANES), F32)] * 2,
        compiler_params=_cparams(("parallel",)),
        name="peer_coef",
    )(a0, a1, eid, slot, gate_t)


def _peer_v_kernel(off_ref, nlow_ref, coef_ref, tab_ref, base_ref, *o_refs, tm, half, n_first):
    n_acc = 4
    main, rest = _pair_ranges(half)

    def weighted(t, p0, p1):
        acc_lo = [jnp.zeros((SUBLANES, LANES), F32) for _ in range(n_acc)]
        acc_hi = [jnp.zeros((SUBLANES, LANES), F32) for _ in range(n_acc)]
        for p in range(p0, p1):
            c = coef_ref[t, p]
            lo, hi = _table_rows(tab_ref, off_ref[t, p])
            acc_lo[p % n_acc] = acc_lo[p % n_acc] + c * lo
            acc_hi[p % n_acc] = acc_hi[p % n_acc] + c * hi
        return ((acc_lo[0] + acc_lo[1]) + (acc_lo[2] + acc_lo[3]),
                (acc_hi[0] + acc_hi[1]) + (acc_hi[2] + acc_hi[3]))

    def run(o_ref):
        def store_row(t, lo, hi):
            o_ref[t, 0] = base_ref[t, 0] + lo
            o_ref[t, 1] = base_ref[t, 1] + hi

        def token(t, carry):
            lo, hi = weighted(t, *main)
            store_row(t, lo, hi)

            @pl.when(_overflow(nlow_ref[0, t], half))
            def _():
                lo2, hi2 = weighted(t, *rest)
                store_row(t, lo + lo2, hi + hi2)

            return carry

        lax.fori_loop(0, tm, token, 0)

    if len(o_refs) == 1:
        run(o_refs[0])
    else:
        first = pl.program_id(0) < n_first
        pl.when(first)(lambda: run(o_refs[0]))
        pl.when(jnp.logical_not(first))(lambda: run(o_refs[1]))


def _peer_v(off, nlow3, coef, tab, base, half, tm=128, split=None):
    t = base.shape[0]
    rows = PEER_HALF * SUBLANES
    smem = pl.BlockSpec((tm, LANES), lambda i: (i, 0), memory_space=pltpu.SMEM)
    tile = pl.BlockSpec((tm, 2, SUBLANES, LANES), lambda i: (i, 0, 0, 0))
    if split is None:
        n_first, out_specs = 0, tile
        out_shape = jax.ShapeDtypeStruct((t, 2, SUBLANES, LANES), F32)
    else:
        assert split % tm == 0
        n_first, n_rest = split // tm, (t - split) // tm
        blk = (tm, 2, SUBLANES, LANES)
        out_specs = [pl.BlockSpec(blk, lambda i: (jnp.clip(i, 0, n_first - 1), 0, 0, 0)),
                     pl.BlockSpec(blk, lambda i: (jnp.clip(i - n_first, 0, n_rest - 1), 0, 0, 0))]
        out_shape = [jax.ShapeDtypeStruct((split, 2, SUBLANES, LANES), F32),
                     jax.ShapeDtypeStruct((t - split, 2, SUBLANES, LANES), F32)]
    return pl.pallas_call(
        functools.partial(_peer_v_kernel, tm=tm, half=half, n_first=n_first),
        grid=(t // tm,),
        in_specs=[smem,
                  pl.BlockSpec((None, 1, tm), lambda i: (i, 0, 0), memory_space=pltpu.SMEM),
                  smem,
                  pl.BlockSpec((rows, LANES), lambda i: (half, 0), pipeline_mode=pl.Buffered(1)),
                  tile],
        out_specs=out_specs,
        out_shape=out_shape,
        compiler_params=_cparams(("arbitrary",)),
        name="peer_expert_sum",
    )(off, nlow3, coef, tab, base)


def _peer(x1, split, norm2_g, wq_bf, keys_bf, u_packed, v_packed, tm=128):
    t, d = x1.shape
    tm = min(tm, t)
    (xn_bf,) = _rmsnorm(x1, norm2_g, (BF16,))
    q = _matmul(xn_bf, wq_bf, wq_bf.shape[1], 0, BF16)
    eid_t, gate_t = _peer_topk(q, keys_bf)
    eid, slot, off, nlow = _peer_order(eid_t)
    nlow3 = nlow[:, 0].reshape(t // tm, 1, tm)
    xw = _pack_table(xn_bf).reshape(t, SUBLANES, LANES)
    a0 = _peer_u(off, nlow3, u_packed, xw, 0, tm)
    a1 = _peer_u(off, nlow3, u_packed, xw, 1, tm)
    c0, c1 = _peer_coef(a0, a1, eid, slot, gate_t)
    y = _peer_v(off, nlow3, c0, v_packed, x1.reshape(t, 2, SUBLANES, LANES), 0, tm)
    ya, yb = _peer_v(off, nlow3, c1, v_packed, y, 1, tm, split=split)
    return ya.reshape(split, d), yb.reshape(t - split, d)


def kernel(x_prompt, x_sample, norm1_g, w_in, q_norm_g, k_norm_g, lambda_q1, lambda_k1, lambda_q2, lambda_k2, attn_sub_g, conv_w, conv_b, filt_w1, filt_b1, filt_w2, filt_b2, filt_w3, filt_b3, filt_w4, filt_freq, fft_bias, hyena_out_g, w_out, norm2_g, peer_wq, peer_keys, peer_u, peer_v):
    depth = w_in.shape[0]
    d_model = x_prompt.shape[-1]
    att_w = ATT_HEADS * ATT_VDIM
    shapes = [x_prompt.shape[:2], x_sample.shape[:2]]
    xs = [x_prompt.reshape(-1, d_model), x_sample.reshape(-1, d_model)]
    n0 = xs[0].shape[0]
    slopes = 2.0 ** (-8.0 * jnp.arange(1, ATT_HEADS + 1, dtype=F32) / ATT_HEADS)

    for l in range(depth):
        lambda_init = 0.8 - 0.6 * math.exp(-0.3 * l)
        lam = (jnp.exp(jnp.sum(lambda_q1[l].astype(F32) * lambda_k1[l].astype(F32)))
               - jnp.exp(jnp.sum(lambda_q2[l].astype(F32) * lambda_k2[l].astype(F32)))
               + lambda_init).reshape(1)
        w_in_bf = w_in[l].astype(BF16)
        q_gain = jnp.tile(q_norm_g[l].astype(F32), 2 * ATT_HEADS) * (ATT_QKDIM ** -0.5 * LOG2E)
        k_gain = jnp.tile(k_norm_g[l].astype(F32), 2 * ATT_HEADS)
        qk_gain = jnp.concatenate([q_gain, k_gain]).reshape(1, 2 * att_w)

        h_bf = _rmsnorm_stacked(xs[0], xs[1], norm1_g[l], BF16)
        qk = _matmul(h_bf, w_in_bf, 2 * att_w, 0, BF16, mode="qknorm", extra=qk_gain)
        vt = _matmul_nt(w_in_bf[:, 2 * att_w:3 * att_w].T, h_bf, BF16)
        zh = _matmul(h_bf, w_in_bf, w_in.shape[2] - 3 * att_w, 3 * att_w, F32)

        filt = (filt_w1[l], filt_b1[l], filt_w2[l], filt_b2[l], filt_w3[l], filt_b3[l], filt_w4[l], filt_freq[l])
        segs, row = [], 0
        for (b, s), x_seg in zip(shapes, xs):
            att = _attention(qk, vt, row, slopes, lam, attn_sub_g[l], b, s, 1.0 - lambda_init)
            hy = _hyena(zh, row, b, s, conv_w[l], conv_b[l], filt, fft_bias[l], hyena_out_g[l])
            segs.append((att, hy, x_seg))
            row += b * s
        x1 = _outproj_stacked(segs[0], segs[1], w_out[l].astype(BF16))

        xs = _peer(x1, n0, norm2_g[l], peer_wq[l].astype(BF16), peer_keys[l].astype(BF16),
                   _pack_table(peer_u[l]), _pack_table(peer_v[l]))

    return (xs[0].reshape(x_prompt.shape), xs[1].reshape(x_sample.shape))
```
